```python
import math
import jax, jax.numpy as jnp
from jax import lax
import numpy as np

D_MODEL = 1024
BATCH = 8
SEQ = 2048
DEPTH = 2
DEC_BATCH = 128
DEC_SEQ = 8
PAST_LEN = 16384
PAGE_SIZE = 128

BRANCH_W = D_MODEL // 2
N_BRANCH = 3
RET_HEADS = 4
RET_DK = BRANCH_W // (2 * RET_HEADS)
RET_DV = BRANCH_W // RET_HEADS
RET_CHUNK = 128
ROPE_BASE = 10000.0
SSM_GROUP = 16
SSM_GROUPS = BRANCH_W // SSM_GROUP
SSM_STATE = 64
DT_MIN = 1e-3
DT_MAX = 1e-1
LRU_BLOCKS = 8
LRU_BW = BRANCH_W // LRU_BLOCKS
CONV_W = 4
LRU_C = 8.0
MOE_GROUPS = 4
MOE_PER_GROUP = 8
MOE_EXPERTS = MOE_GROUPS * MOE_PER_GROUP
MOE_TOPK = 2
MOE_HIDDEN = D_MODEL // 2
DN_ALPHA = (2.0 * DEPTH) ** 0.25
DN_BETA = (8.0 * DEPTH) ** -0.25
LN_EPS = 1e-5
Q_W = RET_HEADS * RET_DK
K_W = RET_HEADS * RET_DK
V_W = RET_HEADS * RET_DV
G_W = RET_HEADS * RET_DV
SSM_IN_W = BRANCH_W
LRU_X_W = BRANCH_W
LRU_G_W = BRANCH_W
GATE_W = N_BRANCH * D_MODEL
D_IN = Q_W + K_W + V_W + G_W + SSM_IN_W + LRU_X_W + LRU_G_W + GATE_W

kernel_name = 'hybrid_retention_s5_rglru_hmoe_step'


def _layer_norm(x, g, b):
    xf = x.astype(jnp.float32)
    mu = jnp.mean(xf, -1, keepdims=True)
    var = jnp.mean(jnp.square(xf - mu), -1, keepdims=True)
    y = (xf - mu) * lax.rsqrt(var + LN_EPS) * g.astype(jnp.float32) + b.astype(jnp.float32)
    return y.astype(x.dtype)


def _rope(x, pos):
    half = x.shape[-1] // 2
    inv = ROPE_BASE ** (-jnp.arange(half, dtype=jnp.float32) / half)
    ang = pos[:, None] * inv[None, :]
    cos = jnp.cos(ang)[None, :, None, :]
    sin = jnp.sin(ang)[None, :, None, :]
    x1, x2 = x[..., :half], x[..., half:]
    return jnp.concatenate([x1 * cos - x2 * sin, x1 * sin + x2 * cos], axis=-1)


def _retention_scan(q, k, v, s0):
    bsz, t = q.shape[0], q.shape[1]
    c = RET_CHUNK if t % RET_CHUNK == 0 else t
    n = t // c
    log_g = jnp.log1p(-jnp.exp2(-5.0 - jnp.arange(RET_HEADS, dtype=jnp.float32)))
    idx = jnp.arange(c, dtype=jnp.float32)
    rel = idx[:, None] - idx[None, :]
    dmask = jnp.where(rel >= 0, jnp.exp(log_g[:, None, None] * jnp.maximum(rel, 0.0)), 0.0)
    q_dec = jnp.exp(log_g[None, :] * (idx[:, None] + 1.0))[None, :, :, None]
    k_dec = jnp.exp(log_g[None, :] * (c - 1.0 - idx[:, None]))[None, :, :, None]
    c_dec = jnp.exp(log_g * c)[None, :, None, None]

    def to_chunks(a):
        return a.reshape(bsz, n, c, a.shape[2], a.shape[3]).transpose(1, 0, 2, 3, 4)

    def step(s, inp):
        qc, kc, vc = inp
        sc = jnp.einsum('bihd,bjhd->bhij', qc, kc) * dmask[None]
        o = jnp.einsum('bhij,bjhv->bihv', sc, vc) + jnp.einsum('bihd,bhdv->bihv', qc, s) * q_dec
        s = s * c_dec + jnp.einsum('bjhd,bjhv->bhdv', kc * k_dec, vc)
        return s, o

    s_new, o = lax.scan(step, s0, (to_chunks(q), to_chunks(k), to_chunks(v)))
    o = o.transpose(1, 0, 2, 3, 4).reshape(bsz, t, RET_HEADS, RET_DV)
    return o, s_new


def _retention_branch(q, k, v, g, pos, s0, gn_g, gn_b):
    f32 = jnp.float32
    bsz, t = q.shape[0], q.shape[1]
    q = _rope(q.astype(f32).reshape(bsz, t, RET_HEADS, RET_DK), pos)
    k = _rope(k.astype(f32).reshape(bsz, t, RET_HEADS, RET_DK), pos) * (RET_DK ** -0.5)
    v = v.astype(f32).reshape(bsz, t, RET_HEADS, RET_DV)
    o, s_new = _retention_scan(q, k, v, s0.astype(f32))
    mu = jnp.mean(o, -1, keepdims=True)
    var = jnp.mean(jnp.square(o - mu), -1, keepdims=True)
    o = ((o - mu) * lax.rsqrt(var + LN_EPS)).reshape(bsz, t, V_W)
    o = o * gn_g.astype(f32) + gn_b.astype(f32)
    return jax.nn.silu(g.astype(f32)) * o, s_new


def _complex_combine(e1, e2):
    a1r, a1i, b1r, b1i = e1
    a2r, a2i, b2r, b2i = e2
    return (a2r * a1r - a2i * a1i,
            a2r * a1i + a2i * a1r,
            a2r * b1r - a2i * b1i + b2r,
            a2r * b1i + a2i * b1r + b2i)


def _real_combine(e1, e2):
    a1, b1 = e1
    a2, b2 = e2
    return (a2 * a1, a2 * b1 + b2)


def _ssm_branch(u, h0_re, h0_im, a_re, a_im, log_dt, b_re, b_im, c_re, c_im, d, w_glu):
    f32 = jnp.float32
    bsz, t = u.shape[0], u.shape[1]
    ug = u.astype(f32).reshape(bsz, t, SSM_GROUPS, SSM_GROUP)
    ar, ai = a_re.astype(f32), a_im.astype(f32)
    dt = jnp.exp(log_dt.astype(f32))[:, None]
    mag = jnp.exp(ar * dt)
    lb_re = mag * jnp.cos(ai * dt)
    lb_im = mag * jnp.sin(ai * dt)
    den = ar * ar + ai * ai
    nr = lb_re - 1.0
    coef_re = (nr * ar + lb_im * ai) / den
    coef_im = (lb_im * ar - nr * ai) / den
    br, bi = b_re.astype(f32), b_im.astype(f32)
    bb_re = coef_re[..., None] * br - coef_im[..., None] * bi
    bb_im = coef_re[..., None] * bi + coef_im[..., None] * br
    bu_re = jnp.einsum('btgc,gpc->btgp', ug, bb_re)
    bu_im = jnp.einsum('btgc,gpc->btgp', ug, bb_im)
    h0r, h0i = h0_re.astype(f32), h0_im.astype(f32)
    bu_re = bu_re.at[:, 0].add(lb_re * h0r - lb_im * h0i)
    bu_im = bu_im.at[:, 0].add(lb_re * h0i + lb_im * h0r)
    _, _, h_re, h_im = lax.associative_scan(
        _complex_combine,
        (jnp.broadcast_to(lb_re, bu_re.shape), jnp.broadcast_to(lb_im, bu_im.shape), bu_re, bu_im),
        axis=1)
    y = (jnp.einsum('btgp,gcp->btgc', h_re, c_re.astype(f32))
         - jnp.einsum('btgp,gcp->btgc', h_im, c_im.astype(f32))
         + d.astype(f32).reshape(SSM_GROUPS, SSM_GROUP) * ug)
    z = jax.nn.gelu(y.reshape(bsz, t, BRANCH_W))
    out = z * jax.nn.sigmoid(z @ w_glu.astype(f32))
    return out, h_re[:, -1], h_im[:, -1]


def _lru_branch(xr, xg, h0, conv0, conv_w, conv_b, wa, ba, wx, bx, lam):
    f32 = jnp.float32
    bsz, t = xr.shape[0], xr.shape[1]
    xp = jnp.concatenate([conv0.astype(f32), xr.astype(f32)], axis=1)
    cw = conv_w.astype(f32)
    xc = conv_b.astype(f32)
    for j in range(CONV_W):
        xc = xc + xp[:, j:j + t] * cw[j]
    new_conv = xp[:, xp.shape[1] - (CONV_W - 1):]
    xb = xc.reshape(bsz, t, LRU_BLOCKS, LRU_BW)
    r = jax.nn.sigmoid(jnp.einsum('btnd,nde->btne', xb, wa.astype(f32)).reshape(bsz, t, BRANCH_W) + ba.astype(f32))
    i = jax.nn.sigmoid(jnp.einsum('btnd,nde->btne', xb, wx.astype(f32)).reshape(bsz, t, BRANCH_W) + bx.astype(f32))
    log_a = -LRU_C * r * jax.nn.softplus(-lam.astype(f32))
    a = jnp.exp(log_a)
    b = jnp.sqrt(-jnp.expm1(2.0 * log_a)) * (i * xc)
    b = b.at[:, 0].add(a[:, 0] * h0.astype(f32))
    _, h = lax.associative_scan(_real_combine, (a, b), axis=1)
    return h * jax.nn.gelu(xg.astype(f32)), h[:, -1], new_conv


def _token_mixer(x, pos, st, p):
    s_ret, s_re, s_im, s_lru, s_conv = st
    bsz, t = x.shape[0], x.shape[1]
    z = x @ p['w_in']
    offs = np.cumsum([Q_W, K_W, V_W, G_W, SSM_IN_W, LRU_X_W, LRU_G_W]).tolist()
    zq, zk, zv, zg, zs, zx, zgl, zm = jnp.split(z, offs, axis=-1)
    y_ret, s_ret = _retention_branch(zq, zk, zv, zg, pos, s_ret, p['ret_gn_g'], p['ret_gn_b'])
    y_ssm, s_re, s_im = _ssm_branch(zs, s_re, s_im, p['ssm_a_re'], p['ssm_a_im'], p['ssm_log_dt'],
                                    p['ssm_b_re'], p['ssm_b_im'], p['ssm_c_re'], p['ssm_c_im'],
                                    p['ssm_d'], p['ssm_w_glu'])
    y_lru, s_lru, s_conv = _lru_branch(zx, zgl, s_lru, s_conv, p['lru_conv_w'], p['lru_conv_b'],
                                       p['lru_wa'], p['lru_ba'], p['lru_wx'], p['lru_bx'], p['lru_lambda'])
    ys = jnp.stack([y_ret, y_ssm, y_lru], axis=2).astype(x.dtype)
    proj = jnp.einsum('btnw,nwd->btnd', ys, p['w_branch'])
    gates = jax.nn.sigmoid(zm.reshape(bsz, t, N_BRANCH, D_MODEL))
    merged = jnp.sum(gates * proj, axis=2)
    return merged @ p['w_out'], (s_ret, s_re, s_im, s_lru, s_conv)


def _hmoe(x, p):
    bsz, t = x.shape[0], x.shape[1]
    xf = x.reshape(bsz * t, D_MODEL)
    lg = (xf @ p['moe_w_group'] + p['moe_b_group']).astype(jnp.float32)
    pg = jax.nn.softmax(lg, axis=-1)
    gsel = jnp.argmax(lg, axis=-1)
    pg_sel = jnp.take_along_axis(pg, gsel[:, None], axis=1)[:, 0]
    le = (xf @ p['moe_w_expert'] + p['moe_b_expert']).astype(jnp.float32)
    le = le.reshape(bsz * t, MOE_GROUPS, MOE_PER_GROUP)
    le_sel = jnp.take_along_axis(le, gsel[:, None, None], axis=1)[:, 0]
    pe = jax.nn.softmax(le_sel, axis=-1)
    vals, idx = lax.top_k(pe, MOE_TOPK)
    w_tok = pg_sel[:, None] * vals / jnp.sum(vals, axis=-1, keepdims=True)
    eidx = gsel[:, None] * MOE_PER_GROUP + idx
    combine = jnp.sum(jax.nn.one_hot(eidx, MOE_EXPERTS, dtype=jnp.float32) * w_tok[..., None], axis=1)

    def expert_step(acc, inp):
        w1, w3, w2, c = inp
        h = jax.nn.silu(xf @ w1) * (xf @ w3)
        return acc + c[:, None].astype(x.dtype) * (h @ w2), None

    acc, _ = lax.scan(expert_step, jnp.zeros_like(xf),
                      (p['moe_w1'], p['moe_w3'], p['moe_w2'], combine.T))
    return acc.reshape(bsz, t, D_MODEL)


def _layer(x, pos, st, p):
    mix, new_st = _token_mixer(x, pos, st, p)
    x = _layer_norm(DN_ALPHA * x + mix, p['ln1_g'], p['ln1_b'])
    x = _layer_norm(DN_ALPHA * x + _hmoe(x, p), p['ln2_g'], p['ln2_b'])
    return x, new_st


def setup_inputs(seed: int = 0) -> dict:
    key = jax.random.key(seed)
    ks = iter(jax.random.split(key, 64))
    f32 = jnp.float32

    def nrm(shape, scale):
        return scale * jax.random.normal(next(ks), shape, f32)

    u_dt = jax.random.uniform(next(ks), (DEPTH, SSM_GROUPS), f32)
    log_dt = math.log(DT_MIN) + u_dt * (math.log(DT_MAX) - math.log(DT_MIN))
    a_lr = jax.random.uniform(next(ks), (DEPTH, BRANCH_W), f32, minval=0.9, maxval=0.999)
    lam = jnp.log(a_lr) - jnp.log1p(-a_lr)
    a_im0 = jnp.pi * jnp.arange(SSM_STATE, dtype=f32)
    return {
        'x_prompt': nrm((BATCH, SEQ, D_MODEL), 1.0),
        'x_sample': nrm((DEC_BATCH, DEC_SEQ, D_MODEL), 1.0),
        'state_ret': nrm((DEPTH, DEC_BATCH, RET_HEADS, RET_DK, RET_DV), 0.5),
        'state_ssm_re': nrm((DEPTH, DEC_BATCH, SSM_GROUPS, SSM_STATE), 0.5),
        'state_ssm_im': nrm((DEPTH, DEC_BATCH, SSM_GROUPS, SSM_STATE), 0.5),
        'state_lru': nrm((DEPTH, DEC_BATCH, BRANCH_W), 0.5),
        'state_conv': nrm((DEPTH, DEC_BATCH, CONV_W - 1, BRANCH_W), 1.0),
        'w_in': nrm((DEPTH, D_MODEL, D_IN), D_MODEL ** -0.5),
        'ret_gn_g': 1.0 + nrm((DEPTH, V_W), 0.02),
        'ret_gn_b': nrm((DEPTH, V_W), 0.02),
        'ssm_a_re': -0.5 + nrm((DEPTH, SSM_GROUPS, SSM_STATE), 0.01),
        'ssm_a_im': a_im0 + nrm((DEPTH, SSM_GROUPS, SSM_STATE), 0.01),
        'ssm_log_dt': log_dt,
        'ssm_b_re': nrm((DEPTH, SSM_GROUPS, SSM_STATE, SSM_GROUP), (2.0 * SSM_GROUP) ** -0.5),
        'ssm_b_im': nrm((DEPTH, SSM_GROUPS, SSM_STATE, SSM_GROUP), (2.0 * SSM_GROUP) ** -0.5),
        'ssm_c_re': nrm((DEPTH, SSM_GROUPS, SSM_GROUP, SSM_STATE), (2.0 * SSM_STATE) ** -0.5),
        'ssm_c_im': nrm((DEPTH, SSM_GROUPS, SSM_GROUP, SSM_STATE), (2.0 * SSM_STATE) ** -0.5),
        'ssm_d': nrm((DEPTH, BRANCH_W), 1.0),
        'ssm_w_glu': nrm((DEPTH, BRANCH_W, BRANCH_W), BRANCH_W ** -0.5),
        'lru_conv_w': nrm((DEPTH, CONV_W, BRANCH_W), CONV_W ** -0.5),
        'lru_conv_b': nrm((DEPTH, BRANCH_W), 0.02),
        'lru_wa': nrm((DEPTH, LRU_BLOCKS, LRU_BW, LRU_BW), LRU_BW ** -0.5),
        'lru_ba': nrm((DEPTH, BRANCH_W), 0.02),
        'lru_wx': nrm((DEPTH, LRU_BLOCKS, LRU_BW, LRU_BW), LRU_BW ** -0.5),
        'lru_bx': nrm((DEPTH, BRANCH_W), 0.02),
        'lru_lambda': lam,
        'w_branch': nrm((DEPTH, N_BRANCH, BRANCH_W, D_MODEL), BRANCH_W ** -0.5),
        'w_out': nrm((DEPTH, D_MODEL, D_MODEL), DN_BETA * D_MODEL ** -0.5),
        'ln1_g': 1.0 + nrm((DEPTH, D_MODEL), 0.02),
        'ln1_b': nrm((DEPTH, D_MODEL), 0.02),
        'moe_w_group': nrm((DEPTH, D_MODEL, MOE_GROUPS), D_MODEL ** -0.5),
        'moe_b_group': nrm((DEPTH, MOE_GROUPS), 0.01),
        'moe_w_expert': nrm((DEPTH, D_MODEL, MOE_EXPERTS), D_MODEL ** -0.5),
        'moe_b_expert': nrm((DEPTH, MOE_EXPERTS), 0.01),
        'moe_w1': nrm((DEPTH, MOE_EXPERTS, D_MODEL, MOE_HIDDEN), D_MODEL ** -0.5),
        'moe_w3': nrm((DEPTH, MOE_EXPERTS, D_MODEL, MOE_HIDDEN), D_MODEL ** -0.5),
        'moe_w2': nrm((DEPTH, MOE_EXPERTS, MOE_HIDDEN, D_MODEL), DN_BETA * MOE_HIDDEN ** -0.5),
        'ln2_g': 1.0 + nrm((DEPTH, D_MODEL), 0.02),
        'ln2_b': nrm((DEPTH, D_MODEL), 0.02),
    }


def reference(x_prompt, x_sample, state_ret, state_ssm_re, state_ssm_im, state_lru, state_conv,
              w_in, ret_gn_g, ret_gn_b, ssm_a_re, ssm_a_im, ssm_log_dt, ssm_b_re, ssm_b_im,
              ssm_c_re, ssm_c_im, ssm_d, ssm_w_glu, lru_conv_w, lru_conv_b, lru_wa, lru_ba,
              lru_wx, lru_bx, lru_lambda, w_branch, w_out, ln1_g, ln1_b, moe_w_group, moe_b_group,
              moe_w_expert, moe_b_expert, moe_w1, moe_w3, moe_w2, ln2_g, ln2_b):
    f32 = jnp.float32
    bp, tp = x_prompt.shape[0], x_prompt.shape[1]
    ts = x_sample.shape[1]
    pos_p = jnp.arange(tp, dtype=f32)
    pos_s = PAST_LEN + jnp.arange(ts, dtype=f32)
    xp, xs = x_prompt, x_sample
    new_p = [[], [], [], [], []]
    new_s = [[], [], [], [], []]
    for l in range(DEPTH):
        p = {
            'w_in': w_in[l], 'ret_gn_g': ret_gn_g[l], 'ret_gn_b': ret_gn_b[l],
            'ssm_a_re': ssm_a_re[l], 'ssm_a_im': ssm_a_im[l], 'ssm_log_dt': ssm_log_dt[l],
            'ssm_b_re': ssm_b_re[l], 'ssm_b_im': ssm_b_im[l], 'ssm_c_re': ssm_c_re[l],
            'ssm_c_im': ssm_c_im[l], 'ssm_d': ssm_d[l], 'ssm_w_glu': ssm_w_glu[l],
            'lru_conv_w': lru_conv_w[l], 'lru_conv_b': lru_conv_b[l], 'lru_wa': lru_wa[l],
            'lru_ba': lru_ba[l], 'lru_wx': lru_wx[l], 'lru_bx': lru_bx[l], 'lru_lambda': lru_lambda[l],
            'w_branch': w_branch[l], 'w_out': w_out[l], 'ln1_g': ln1_g[l], 'ln1_b': ln1_b[l],
            'moe_w_group': moe_w_group[l], 'moe_b_group': moe_b_group[l],
            'moe_w_expert': moe_w_expert[l], 'moe_b_expert': moe_b_expert[l],
            'moe_w1': moe_w1[l], 'moe_w3': moe_w3[l], 'moe_w2': moe_w2[l],
            'ln2_g': ln2_g[l], 'ln2_b': ln2_b[l],
        }
        st_p = (jnp.zeros((bp, RET_HEADS, RET_DK, RET_DV), f32),
                jnp.zeros((bp, SSM_GROUPS, SSM_STATE), f32),
                jnp.zeros((bp, SSM_GROUPS, SSM_STATE), f32),
                jnp.zeros((bp, BRANCH_W), f32),
                jnp.zeros((bp, CONV_W - 1, BRANCH_W), f32))
        st_s = (state_ret[l], state_ssm_re[l], state_ssm_im[l], state_lru[l], state_conv[l])
        xp, out_p = _layer(xp, pos_p, st_p, p)
        xs, out_s = _layer(xs, pos_s, st_s, p)
        for j in range(5):
            new_p[j].append(out_p[j])
            new_s[j].append(out_s[j])
    ret_p = jnp.stack(new_p[0]).astype(state_ret.dtype)
    ssm_re_p = jnp.stack(new_p[1]).astype(state_ssm_re.dtype)
    ssm_im_p = jnp.stack(new_p[2]).astype(state_ssm_im.dtype)
    lru_p = jnp.stack(new_p[3]).astype(state_lru.dtype)
    conv_p = jnp.stack(new_p[4]).astype(state_conv.dtype)
    ret_s = jnp.stack(new_s[0]).astype(state_ret.dtype)
    ssm_re_s = jnp.stack(new_s[1]).astype(state_ssm_re.dtype)
    ssm_im_s = jnp.stack(new_s[2]).astype(state_ssm_im.dtype)
    lru_s = jnp.stack(new_s[3]).astype(state_lru.dtype)
    conv_s = jnp.stack(new_s[4]).astype(state_conv.dtype)
    return (xp, xs, ret_p, ssm_re_p, ssm_im_p, lru_p, conv_p, ret_s, ssm_re_s, ssm_im_s, lru_s, conv_s)
```

```python
import functools

import jax
import jax.numpy as jnp
import numpy as np
from jax import lax
from jax.experimental import pallas as pl
from jax.experimental.pallas import tpu as pltpu

f32 = jnp.float32
bf16 = jnp.bfloat16

D_MODEL = 1024
BATCH = 8
SEQ = 2048
DEPTH = 2
DEC_BATCH = 128
DEC_SEQ = 8
PAST_LEN = 16384
BRANCH_W = 512
N_BRANCH = 3
RET_HEADS = 4
RET_DK = 64
RET_DV = 128
ROPE_BASE = 10000.0
SSM_GROUP = 16
SSM_GROUPS = 32
SSM_STATE = 64
SSM_LANES = SSM_GROUPS * SSM_STATE
LRU_BLOCKS = 8
LRU_BW = 64
CONV_W = 4
LRU_C = 8.0
MOE_GROUPS = 4
MOE_PER_GROUP = 8
MOE_EXPERTS = 32
MOE_TOPK = 2
MOE_HIDDEN = 512
DN_ALPHA = (2.0 * DEPTH) ** 0.25
LN_EPS = 1e-5
D_IN = 6144

V7X_SUBLANES = 8
V7X_LANES = 128
V7X_VMEM_LIMIT = 56 * 1024 * 1024

N_P = BATCH * SEQ
N_S = DEC_BATCH * DEC_SEQ
N_TOK = N_P + N_S
ROW_TILE = 1024
P_TILES = N_P // ROW_TILE
N_TILES = N_TOK // ROW_TILE
P_TC = ROW_TILE // BATCH
COL_TILE = 512
RET_SUB_T = 32
RET_SUB_R = RET_SUB_T * BATCH
RET_SUBS = ROW_TILE // RET_SUB_R
S_BLOCKS = DEC_BATCH // BATCH
S_BLOCK_R = DEC_SEQ * BATCH
MERGE_TILE = 512
MOE_TM = 256
N_PAIRS = N_TOK * MOE_TOPK
MOE_MAX_TILES = (N_PAIRS + MOE_EXPERTS * (MOE_TM - 1)) // MOE_TM + 1
MOE_ROWS = MOE_MAX_TILES * MOE_TM
CMB_TILE = 256
ROUTE_LANES = 128


def _cparams(sem):
    return pltpu.CompilerParams(dimension_semantics=sem, vmem_limit_bytes=V7X_VMEM_LIMIT)


def _dot(a, b):
    return jnp.dot(a, b, preferred_element_type=f32)


def _dot_nt(a, b):
    return lax.dot_general(a, b, (((1,), (1,)), ((), ())), preferred_element_type=f32)


def _dot_tn(a, b):
    return lax.dot_general(a, b, (((0,), (0,)), ((), ())), preferred_element_type=f32)


def _layer_norm_rows(x, g, b):
    mu = jnp.mean(x, -1, keepdims=True)
    xc = x - mu
    var = jnp.mean(xc * xc, -1, keepdims=True)
    return xc * lax.rsqrt(var + LN_EPS) * g + b


def _inproj_kernel(x_ref, w_ref, o_ref, xb_ref):
    @pl.when(pl.program_id(1) == 0)
    def _():
        xb_ref[...] = x_ref[...].astype(bf16)

    o_ref[...] = _dot(xb_ref[...], w_ref[...])


def _inproj(x, w_bf):
    return pl.pallas_call(
        _inproj_kernel,
        grid=(N_TILES, D_IN // COL_TILE),
        in_specs=[pl.BlockSpec((ROW_TILE, D_MODEL), lambda i, j: (i, 0)),
                  pl.BlockSpec((D_MODEL, COL_TILE), lambda i, j: (0, j))],
        out_specs=pl.BlockSpec((ROW_TILE, COL_TILE), lambda i, j: (i, j)),
        out_shape=jax.ShapeDtypeStruct((N_TOK, D_IN), f32),
        scratch_shapes=[pltpu.VMEM((ROW_TILE, D_MODEL), bf16)],
        compiler_params=_cparams(("arbitrary", "arbitrary")),
        name="inproj",
    )(x, w_bf)


def _ret_block(q, k, v, g, cosb, sinb, mask_ref, qdec_ref, kdec_ref, cdec_ref, scat_ref, gng, gnb):
    rows = q.shape[0]
    lane_qk = lax.broadcasted_iota(jnp.int32, (rows, 2 * V7X_LANES), 1)
    first_half = (lane_qk & (RET_DK - 1)) < (RET_DK // 2)

    def rope(x):
        partner = jnp.where(first_half, pltpu.roll(x, 2 * V7X_LANES - RET_DK // 2, 1),
                            pltpu.roll(x, RET_DK // 2, 1))
        return x * cosb + partner * sinb

    q = rope(q)
    k = rope(k) * (RET_DK ** -0.5)
    kd = k * kdec_ref[...]
    lane = lax.broadcasted_iota(jnp.int32, (rows, V7X_LANES), 1)
    row_b = lax.broadcasted_iota(jnp.int32, (rows, V7X_LANES), 0) & (BATCH - 1)
    outs = []
    for p in range(2):
        qp = q[:, p * V7X_LANES:(p + 1) * V7X_LANES]
        kp = k[:, p * V7X_LANES:(p + 1) * V7X_LANES].astype(bf16)
        kdp = kd[:, p * V7X_LANES:(p + 1) * V7X_LANES]
        s_old = scat_ref[p]
        s_bf = s_old.astype(bf16)
        s_new = s_old * jnp.concatenate([cdec_ref[p]] * BATCH, axis=1)
        for hh in range(2):
            h = 2 * p + hh
            head_lanes = (lane >= RET_DK) if hh else (lane < RET_DK)
            qh = jnp.where(head_lanes, qp, 0.0).astype(bf16)
            kdh = jnp.where(head_lanes, kdp, 0.0).astype(bf16)
            vh = v[:, h * RET_DV:(h + 1) * RET_DV]
            vh_bf = vh.astype(bf16)
            sc = _dot_nt(qh, kp) * mask_ref[h]
            o = _dot(sc.astype(bf16), vh_bf)
            cross = _dot(qh, s_bf)
            oc = jnp.zeros((rows, RET_DV), f32)
            for b in range(BATCH):
                oc = oc + jnp.where(row_b == b, cross[:, b * RET_DV:(b + 1) * RET_DV], 0.0)
            o = o + oc * qdec_ref[h]
            vcat = jnp.concatenate([jnp.where(row_b == b, vh_bf, jnp.zeros_like(vh_bf))
                                    for b in range(BATCH)], axis=1)
            s_new = s_new + _dot_tn(kdh, vcat)
            mu = jnp.mean(o, -1, keepdims=True)
            oc2 = o - mu
            var = jnp.mean(oc2 * oc2, -1, keepdims=True)
            outs.append(oc2 * lax.rsqrt(var + LN_EPS))
        scat_ref[p] = s_new
    o = jnp.concatenate(outs, axis=1) * gng + gnb
    return jax.nn.silu(g) * o


def _ret_kernel(zqk_ref, zv_ref, zg_ref, cos_ref, sin_ref, s0_ref,
                mask_p_ref, qdec_p_ref, kdec_p_ref, cdec_p_ref,
                mask_s_ref, qdec_s_ref, kdec_s_ref, cdec_s_ref,
                gng_ref, gnb_ref,
                y_ref, retp_ref, rets_ref, scat_ref):
    i = pl.program_id(0)
    gng = gng_ref[...]
    gnb = gnb_ref[...]

    @pl.when(i == 0)
    def _():
        scat_ref[...] = jnp.zeros_like(scat_ref)

    @pl.when(i < P_TILES)
    def _():
        for sc in range(RET_SUBS):
            r0 = sc * RET_SUB_R
            rs = slice(r0, r0 + RET_SUB_R)
            y_ref[rs, :] = _ret_block(
                zqk_ref[rs, 0:256], zqk_ref[rs, 256:512], zv_ref[rs, :], zg_ref[rs, :],
                cos_ref[rs, :], sin_ref[rs, :],
                mask_p_ref, qdec_p_ref, kdec_p_ref, cdec_p_ref, scat_ref, gng, gnb)

    @pl.when(i == P_TILES - 1)
    def _():
        for b in range(BATCH):
            for p in range(2):
                retp_ref[b, p] = scat_ref[p, :, b * RET_DV:(b + 1) * RET_DV]

    @pl.when(i >= P_TILES)
    def _():
        bb = i - P_TILES
        for b in range(BATCH):
            for p in range(2):
                scat_ref[p, :, b * RET_DV:(b + 1) * RET_DV] = s0_ref[b, p]

        def rows_of(ref, c0, c1):
            return jnp.concatenate(
                [ref[pl.ds(pl.multiple_of(t * DEC_BATCH + bb * BATCH, BATCH), BATCH), c0:c1]
                 for t in range(DEC_SEQ)], axis=0)

        y = _ret_block(
            rows_of(zqk_ref, 0, 256), rows_of(zqk_ref, 256, 512), rows_of(zv_ref, 0, 512),
            rows_of(zg_ref, 0, 512), rows_of(cos_ref, 0, 256), rows_of(sin_ref, 0, 256),
            mask_s_ref, qdec_s_ref, kdec_s_ref, cdec_s_ref, scat_ref, gng, gnb)
        for t in range(DEC_SEQ):
            y_ref[pl.ds(pl.multiple_of(t * DEC_BATCH + bb * BATCH, BATCH), BATCH), :] = (
                y[t * BATCH:(t + 1) * BATCH, :])
        for b in range(BATCH):
            for p in range(2):
                rets_ref[b, p] = scat_ref[p, :, b * RET_DV:(b + 1) * RET_DV]


def _ret_tables(tc):
    rows = tc * BATCH
    log_g = jnp.log1p(-jnp.exp2(-5.0 - jnp.arange(RET_HEADS, dtype=f32)))
    t_idx = (jnp.arange(rows) // BATCH).astype(f32)
    b_idx = jnp.arange(rows) % BATCH
    rel = t_idx[:, None] - t_idx[None, :]
    same = b_idx[:, None] == b_idx[None, :]
    decay = jnp.exp(log_g[:, None, None] * jnp.maximum(rel, 0.0))
    mask = jnp.where((rel >= 0) & same, decay, 0.0)
    qdec = jnp.exp(log_g[:, None] * (t_idx[None, :] + 1.0))
    qdec = jnp.broadcast_to(qdec[:, :, None], (RET_HEADS, rows, RET_DV))
    kdec = jnp.exp(log_g[:, None] * (tc - 1.0 - t_idx[None, :]))
    kdec = jnp.broadcast_to(kdec.T[:, :, None], (rows, RET_HEADS, RET_DK)).reshape(rows, RET_HEADS * RET_DK)
    cdec = jnp.exp(log_g * tc)
    cdec = jnp.broadcast_to(cdec[:, None, None], (RET_HEADS, RET_DK, RET_DV)).reshape(2, 2 * RET_DK, RET_DV)
    return mask.astype(f32), qdec.astype(f32), kdec.astype(f32), cdec.astype(f32)


def _rope_tables():
    half = RET_DK // 2
    inv = ROPE_BASE ** (-jnp.arange(half, dtype=f32) / half)
    pos_p = jnp.arange(SEQ, dtype=f32)
    pos_s = PAST_LEN + jnp.arange(DEC_SEQ, dtype=f32)

    def tab(pos, nb):
        ang = pos[:, None] * inv[None, :]
        cos = jnp.cos(ang)
        sin = jnp.sin(ang)
        cos_h = jnp.concatenate([cos, cos], axis=1)
        sin_h = jnp.concatenate([-sin, sin], axis=1)
        cos_f = jnp.tile(cos_h, (1, RET_HEADS))
        sin_f = jnp.tile(sin_h, (1, RET_HEADS))
        t = pos.shape[0]
        cos_f = jnp.broadcast_to(cos_f[:, None, :], (t, nb, 256)).reshape(t * nb, 256)
        sin_f = jnp.broadcast_to(sin_f[:, None, :], (t, nb, 256)).reshape(t * nb, 256)
        return cos_f, sin_f

    cp, sp = tab(pos_p, BATCH)
    cs, ss = tab(pos_s, DEC_BATCH)
    return jnp.concatenate([cp, cs], axis=0), jnp.concatenate([sp, ss], axis=0)


def _retention(z, cos_t, sin_t, s0_s, tabs_p, tabs_s, gng, gnb):
    n_steps = P_TILES + S_BLOCKS
    tile = lambda i: jnp.minimum(i, P_TILES)
    sblk = lambda i: jnp.maximum(i - P_TILES, 0)
    full = lambda a: pl.BlockSpec(a.shape, lambda i, _n=a.ndim: (0,) * _n)
    state_blk = (BATCH, 2, 2 * RET_DK, RET_DV)
    ins = [z, z, z, cos_t, sin_t, s0_s, *tabs_p, *tabs_s, gng, gnb]
    in_specs = [
        pl.BlockSpec((ROW_TILE, COL_TILE), lambda i: (tile(i), 0)),
        pl.BlockSpec((ROW_TILE, COL_TILE), lambda i: (tile(i), 1)),
        pl.BlockSpec((ROW_TILE, COL_TILE), lambda i: (tile(i), 2)),
        pl.BlockSpec((ROW_TILE, 256), lambda i: (tile(i), 0)),
        pl.BlockSpec((ROW_TILE, 256), lambda i: (tile(i), 0)),
        pl.BlockSpec(state_blk, lambda i: (sblk(i), 0, 0, 0)),
    ] + [full(a) for a in (*tabs_p, *tabs_s, gng, gnb)]
    return pl.pallas_call(
        _ret_kernel,
        grid=(n_steps,),
        in_specs=in_specs,
        out_specs=[pl.BlockSpec((ROW_TILE, BRANCH_W), lambda i: (tile(i), 0)),
                   pl.BlockSpec(state_blk, lambda i: (0, 0, 0, 0)),
                   pl.BlockSpec(state_blk, lambda i: (sblk(i), 0, 0, 0))],
        out_shape=[jax.ShapeDtypeStruct((N_TOK, BRANCH_W), f32),
                   jax.ShapeDtypeStruct((BATCH, 2, 2 * RET_DK, RET_DV), f32),
                   jax.ShapeDtypeStruct((DEC_BATCH, 2, 2 * RET_DK, RET_DV), f32)],
        scratch_shapes=[pltpu.VMEM((2, 2 * RET_DK, BATCH * RET_DV), f32)],
        compiler_params=_cparams(("arbitrary",)),
        name="retention",
    )(*ins)


SSM_LB = 512
SSM_RC = 256


def _ssm_scan(bre_ref, bim_ref, lre_ref, lim_ref, h_re0, h_im0, row0, nb_rows, steps, lb, unroll):
    ls = slice(lb * SSM_LB, (lb + 1) * SSM_LB)
    a_re = jnp.broadcast_to(lre_ref[:, ls], (V7X_SUBLANES, SSM_LB))
    a_im = jnp.broadcast_to(lim_ref[:, ls], (V7X_SUBLANES, SSM_LB))

    def step(t, carry):
        h_re, h_im = carry
        r = pl.multiple_of(row0 + t * nb_rows, V7X_SUBLANES)
        n_re = a_re * h_re - a_im * h_im + bre_ref[pl.ds(r, V7X_SUBLANES), ls]
        n_im = a_re * h_im + a_im * h_re + bim_ref[pl.ds(r, V7X_SUBLANES), ls]
        bre_ref[pl.ds(r, V7X_SUBLANES), ls] = n_re
        bim_ref[pl.ds(r, V7X_SUBLANES), ls] = n_im
        return n_re, n_im

    return lax.fori_loop(0, steps, step, (h_re0, h_im0), unroll=unroll)


def _ssm_kernel(zs_ref, h0re_ref, h0im_ref, lre_ref, lim_ref, bbre_ref, bbim_ref, ccre_ref, ccim_ref,
                d_ref, wglu_ref,
                y_ref, pre_ref, pim_ref, sre_ref, sim_ref,
                bre_ref, bim_ref, hre_ref, him_ref):
    i = pl.program_id(0)
    for rc in range(ROW_TILE // SSM_RC):
        rs = slice(rc * SSM_RC, (rc + 1) * SSM_RC)
        ub = zs_ref[rs, :].astype(bf16)
        bre_ref[rs, :] = _dot(ub, bbre_ref[...])
        bim_ref[rs, :] = _dot(ub, bbim_ref[...])

    @pl.when(i == 0)
    def _():
        hre_ref[...] = jnp.zeros_like(hre_ref)
        him_ref[...] = jnp.zeros_like(him_ref)

    @pl.when(i < P_TILES)
    def _():
        for lb in range(SSM_LANES // SSM_LB):
            ls = slice(lb * SSM_LB, (lb + 1) * SSM_LB)
            h_re, h_im = _ssm_scan(bre_ref, bim_ref, lre_ref, lim_ref, hre_ref[:, ls], him_ref[:, ls],
                                   0, BATCH, P_TC, lb, 8)
            hre_ref[:, ls] = h_re
            him_ref[:, ls] = h_im
        pre_ref[...] = hre_ref[...]
        pim_ref[...] = him_ref[...]

    @pl.when(i >= P_TILES)
    def _():
        def per_row_tile(rt, c):
            r0 = pl.multiple_of(rt * V7X_SUBLANES, V7X_SUBLANES)
            for lb in range(SSM_LANES // SSM_LB):
                ls = slice(lb * SSM_LB, (lb + 1) * SSM_LB)
                _ssm_scan(bre_ref, bim_ref, lre_ref, lim_ref,
                          h0re_ref[pl.ds(r0, V7X_SUBLANES), ls], h0im_ref[pl.ds(r0, V7X_SUBLANES), ls],
                          r0, DEC_BATCH, DEC_SEQ, lb, True)
            return c

        lax.fori_loop(0, DEC_BATCH // V7X_SUBLANES, per_row_tile, 0)
        last = (DEC_SEQ - 1) * DEC_BATCH
        sre_ref[...] = bre_ref[last:last + DEC_BATCH, :]
        sim_ref[...] = bim_ref[last:last + DEC_BATCH, :]

    for rc in range(ROW_TILE // SSM_RC):
        rs = slice(rc * SSM_RC, (rc + 1) * SSM_RC)
        y = (_dot(bre_ref[rs, :].astype(bf16), ccre_ref[...]) - _dot(bim_ref[rs, :].astype(bf16), ccim_ref[...])
             + d_ref[...] * zs_ref[rs, :])
        zz = jax.nn.gelu(y)
        y_ref[rs, :] = zz * jax.nn.sigmoid(_dot(zz.astype(bf16), wglu_ref[...]))


def _ssm(z, h0re, h0im, lre, lim, bbre, bbim, ccre, ccim, dvec, wglu):
    full = lambda a: pl.BlockSpec(a.shape, lambda i, _n=a.ndim: (0,) * _n)
    consts = (h0re, h0im, lre, lim, bbre, bbim, ccre, ccim, dvec, wglu)
    return pl.pallas_call(
        _ssm_kernel,
        grid=(N_TILES,),
        in_specs=[pl.BlockSpec((ROW_TILE, COL_TILE), lambda i: (i, 3))] + [full(a) for a in consts],
        out_specs=[pl.BlockSpec((ROW_TILE, BRANCH_W), lambda i: (i, 0)),
                   pl.BlockSpec((BATCH, SSM_LANES), lambda i: (0, 0)),
                   pl.BlockSpec((BATCH, SSM_LANES), lambda i: (0, 0)),
                   pl.BlockSpec((DEC_BATCH, SSM_LANES), lambda i: (0, 0)),
                   pl.BlockSpec((DEC_BATCH, SSM_LANES), lambda i: (0, 0))],
        out_shape=[jax.ShapeDtypeStruct((N_TOK, BRANCH_W), f32),
                   jax.ShapeDtypeStruct((BATCH, SSM_LANES), f32),
                   jax.ShapeDtypeStruct((BATCH, SSM_LANES), f32),
                   jax.ShapeDtypeStruct((DEC_BATCH, SSM_LANES), f32),
                   jax.ShapeDtypeStruct((DEC_BATCH, SSM_LANES), f32)],
        scratch_shapes=[pltpu.VMEM((ROW_TILE, SSM_LANES), f32), pltpu.VMEM((ROW_TILE, SSM_LANES), f32),
                        pltpu.VMEM((BATCH, SSM_LANES), f32), pltpu.VMEM((BATCH, SSM_LANES), f32)],
        compiler_params=_cparams(("arbitrary",)),
        name="ssm",
    )(z, *consts)


LRU_HIST_P = (CONV_W - 1) * BATCH
LRU_HIST_S = (CONV_W - 1) * DEC_BATCH


def _lru_gates(xe_ref, nb_rows, cw_ref, cb_ref, wa_ref, ba_ref, wx_ref, bx_ref, lam_ref):
    xc = cb_ref[...] + xe_ref[0:ROW_TILE, :] * cw_ref[0:1, :]
    for j in range(1, CONV_W):
        xc = xc + xe_ref[j * nb_rows:j * nb_rows + ROW_TILE, :] * cw_ref[j:j + 1, :]
    xcb = xc.astype(bf16)
    r = jax.nn.sigmoid(_dot(xcb, wa_ref[...]) + ba_ref[...])
    ig = jax.nn.sigmoid(_dot(xcb, wx_ref[...]) + bx_ref[...])
    log_a = -LRU_C * r * jax.nn.softplus(-lam_ref[...])
    a = jnp.exp(log_a)
    b = jnp.sqrt(-jnp.tanh(log_a) * (jnp.exp(2.0 * log_a) + 1.0)) * (ig * xc)
    return a, b


def _lru_kernel(zx_ref, zg_ref, h0_ref, conv0_ref, cw_ref, cb_ref, wa_ref, ba_ref, wx_ref, bx_ref, lam_ref,
                y_ref, hp_ref, convp_ref, hs_ref, convs_ref,
                xe_ref, a_ref, b_ref, hc_ref):
    i = pl.program_id(0)
    params = (cw_ref, cb_ref, wa_ref, ba_ref, wx_ref, bx_ref, lam_ref)

    @pl.when(i == 0)
    def _():
        xe_ref[0:LRU_HIST_P, :] = jnp.zeros((LRU_HIST_P, BRANCH_W), f32)
        hc_ref[...] = jnp.zeros_like(hc_ref)

    @pl.when(i < P_TILES)
    def _():
        xe_ref[LRU_HIST_P:LRU_HIST_P + ROW_TILE, :] = zx_ref[...]
        a, b = _lru_gates(xe_ref, BATCH, *params)
        a_ref[...] = a
        b_ref[...] = b
        hist = xe_ref[ROW_TILE:ROW_TILE + LRU_HIST_P, :]
        xe_ref[0:LRU_HIST_P, :] = hist
        convp_ref[...] = hist

        def step(t, h):
            r = pl.multiple_of(t * BATCH, BATCH)
            h = a_ref[pl.ds(r, BATCH), :] * h + b_ref[pl.ds(r, BATCH), :]
            b_ref[pl.ds(r, BATCH), :] = h
            return h

        h = lax.fori_loop(0, P_TC, step, hc_ref[...], unroll=8)
        hc_ref[...] = h
        hp_ref[...] = h

    @pl.when(i >= P_TILES)
    def _():
        xe_ref[0:LRU_HIST_S, :] = conv0_ref[...]
        xe_ref[LRU_HIST_S:LRU_HIST_S + ROW_TILE, :] = zx_ref[...]
        a, b = _lru_gates(xe_ref, DEC_BATCH, *params)
        a_ref[...] = a
        b_ref[...] = b
        convs_ref[...] = xe_ref[ROW_TILE:ROW_TILE + LRU_HIST_S, :]

        def per_row_tile(rt, c):
            r0 = pl.multiple_of(rt * V7X_SUBLANES, V7X_SUBLANES)
            h = h0_ref[pl.ds(r0, V7X_SUBLANES), :]
            for t in range(DEC_SEQ):
                r = pl.multiple_of(t * DEC_BATCH + r0, V7X_SUBLANES)
                h = a_ref[pl.ds(r, V7X_SUBLANES), :] * h + b_ref[pl.ds(r, V7X_SUBLANES), :]
                b_ref[pl.ds(r, V7X_SUBLANES), :] = h
            return c

        lax.fori_loop(0, DEC_BATCH // V7X_SUBLANES, per_row_tile, 0)
        last = (DEC_SEQ - 1) * DEC_BATCH
        hs_ref[...] = b_ref[last:last + DEC_BATCH, :]

    y_ref[...] = b_ref[...] * jax.nn.gelu(zg_ref[...])


def _lru(z, h0, conv0, cw, cb, wa, ba, wx, bx, lam):
    full = lambda a: pl.BlockSpec(a.shape, lambda i, _n=a.ndim: (0,) * _n)
    consts = (h0, conv0, cw, cb, wa, ba, wx, bx, lam)
    return pl.pallas_call(
        _lru_kernel,
        grid=(N_TILES,),
        in_specs=[pl.BlockSpec((ROW_TILE, COL_TILE), lambda i: (i, 4)),
                  pl.BlockSpec((ROW_TILE, COL_TILE), lambda i: (i, 5))] + [full(a) for a in consts],
        out_specs=[pl.BlockSpec((ROW_TILE, BRANCH_W), lambda i: (i, 0)),
                   pl.BlockSpec((BATCH, BRANCH_W), lambda i: (0, 0)),
                   pl.BlockSpec((LRU_HIST_P, BRANCH_W), lambda i: (0, 0)),
                   pl.BlockSpec((DEC_BATCH, BRANCH_W), lambda i: (0, 0)),
                   pl.BlockSpec((LRU_HIST_S, BRANCH_W), lambda i: (0, 0))],
        out_shape=[jax.ShapeDtypeStruct((N_TOK, BRANCH_W), f32),
                   jax.ShapeDtypeStruct((BATCH, BRANCH_W), f32),
                   jax.ShapeDtypeStruct((LRU_HIST_P, BRANCH_W), f32),
                   jax.ShapeDtypeStruct((DEC_BATCH, BRANCH_W), f32),
                   jax.ShapeDtypeStruct((LRU_HIST_S, BRANCH_W), f32)],
        scratch_shapes=[pltpu.VMEM((ROW_TILE + LRU_HIST_S, BRANCH_W), f32),
                        pltpu.VMEM((ROW_TILE, BRANCH_W), f32), pltpu.VMEM((ROW_TILE, BRANCH_W), f32),
                        pltpu.VMEM((BATCH, BRANCH_W), f32)],
        compiler_params=_cparams(("arbitrary",)),
        name="lru",
    )(z, z, *consts)


def _merge_kernel(yr_ref, ys_ref, yl_ref, zm_ref, x_ref, wb_ref, wo_ref, g_ref, b_ref, wr_ref, br_ref,
                  x1_ref, route_ref):
    merged = jnp.zeros((MERGE_TILE, D_MODEL), f32)
    for n, y_ref in enumerate((yr_ref, ys_ref, yl_ref)):
        proj = _dot(y_ref[...].astype(bf16), wb_ref[n])
        gate = jax.nn.sigmoid(zm_ref[:, n * D_MODEL:(n + 1) * D_MODEL])
        merged = merged + gate * proj
    mix = _dot(merged.astype(bf16), wo_ref[...])
    x1 = _layer_norm_rows(DN_ALPHA * x_ref[...] + mix, g_ref[...], b_ref[...])
    x1_ref[...] = x1

    logits = _dot(x1.astype(bf16), wr_ref[...]) + br_ref[...]
    lane = lax.broadcasted_iota(jnp.int32, (MERGE_TILE, ROUTE_LANES), 1).astype(f32)
    big = jnp.float32(ROUTE_LANES)
    neg = jnp.float32(-jnp.inf)
    is_g = lane < MOE_GROUPS
    lg = jnp.where(is_g, logits, neg)
    mg = jnp.max(lg, -1, keepdims=True)
    gsel = jnp.min(jnp.where(lg == mg, lane, big), -1, keepdims=True)
    sum_g = jnp.sum(jnp.where(is_g, jnp.exp(lg - mg), 0.0), -1, keepdims=True)
    pg_sel = 1.0 / sum_g
    lo = MOE_GROUPS + gsel * MOE_PER_GROUP
    is_e = jnp.abs(lane - lo - 0.5 * (MOE_PER_GROUP - 1)) < 0.5 * MOE_PER_GROUP
    le = jnp.where(is_e, logits, neg)
    me = jnp.max(le, -1, keepdims=True)
    ex = jnp.where(is_e, jnp.exp(le - me), 0.0)
    pe = jnp.where(is_e, ex / jnp.sum(ex, -1, keepdims=True), -1.0)
    v1 = jnp.max(pe, -1, keepdims=True)
    i1 = jnp.min(jnp.where(pe == v1, lane, big), -1, keepdims=True)
    pe2 = jnp.where(lane == i1, -1.0, pe)
    v2 = jnp.max(pe2, -1, keepdims=True)
    i2 = jnp.min(jnp.where(pe2 == v2, lane, big), -1, keepdims=True)
    vsum = v1 + v2
    w1 = pg_sel * v1 / vsum
    w2 = pg_sel * v2 / vsum
    e1 = i1 - MOE_GROUPS
    e2 = i2 - MOE_GROUPS
    route_ref[...] = jnp.where(lane == 0, e1, jnp.where(lane == 1, e2, jnp.where(lane == 2, w1, jnp.where(lane == 3, w2, 0.0))))


def _merge(y_ret, y_ssm, y_lru, z, x, wb, wo, g, b, wr, br):
    full = lambda a: pl.BlockSpec(a.shape, lambda i, _n=a.ndim: (0,) * _n)
    row = lambda w: pl.BlockSpec((MERGE_TILE, w), lambda i: (i, 0))
    consts = (wb, wo, g, b, wr, br)
    return pl.pallas_call(
        _merge_kernel,
        grid=(N_TOK // MERGE_TILE,),
        in_specs=[row(BRANCH_W), row(BRANCH_W), row(BRANCH_W),
                  pl.BlockSpec((MERGE_TILE, N_BRANCH * D_MODEL), lambda i: (i, 1)),
                  row(D_MODEL)] + [full(a) for a in consts],
        out_specs=[row(D_MODEL), row(ROUTE_LANES)],
        out_shape=[jax.ShapeDtypeStruct((N_TOK, D_MODEL), f32),
                   jax.ShapeDtypeStruct((N_TOK, ROUTE_LANES), f32)],
        compiler_params=_cparams(("parallel",)),
        name="merge",
    )(y_ret, y_ssm, y_lru, z, x, *consts)


def _moe_kernel(texp_ref, nused_ref, src_ref, x1_hbm, w1_ref, w3_ref, w2_ref, ys_ref,
                xbuf, sem, w1b, w3b, w2b):
    s = pl.program_id(0)
    nused = nused_ref[0]

    def gather_copy(tok, slot, r):
        return pltpu.make_async_copy(x1_hbm.at[pl.ds(tok, 1), :], xbuf.at[slot, pl.ds(r, 1), :], sem.at[slot])

    def issue(tile, slot):
        def body(r, c):
            gather_copy(src_ref[tile * MOE_TM + r], slot, r).start()
            return c

        lax.fori_loop(0, MOE_TM, body, 0)

    @pl.when(s == 0)
    def _():
        issue(0, 0)

    @pl.when(s + 1 < nused)
    def _():
        issue(s + 1, (s + 1) % 2)

    @pl.when(s < nused)
    def _():
        slot = s % 2

        def wait_body(r, c):
            gather_copy(0, slot, r).wait()
            return c

        lax.fori_loop(0, MOE_TM, wait_body, 0)

        prev = texp_ref[jnp.maximum(s - 1, 0)]

        @pl.when((s == 0) | (texp_ref[s] != prev))
        def _():
            w1b[...] = w1_ref[0].astype(bf16)
            w3b[...] = w3_ref[0].astype(bf16)
            w2b[...] = w2_ref[0].astype(bf16)

        xt = xbuf[slot].astype(bf16)
        h = jax.nn.silu(_dot(xt, w1b[...])) * _dot(xt, w3b[...])
        ys_ref[...] = _dot(h.astype(bf16), w2b[...])

    @pl.when(s >= nused)
    def _():
        ys_ref[...] = jnp.zeros_like(ys_ref)


def _moe(texp, nused, src, x1, w1, w3, w2):
    wspec = lambda shp: pl.BlockSpec((1,) + shp, lambda s, texp, nused, src: (texp[s], 0, 0))
    grid_spec = pltpu.PrefetchScalarGridSpec(
        num_scalar_prefetch=3,
        grid=(MOE_MAX_TILES,),
        in_specs=[pl.BlockSpec(memory_space=pl.ANY),
                  wspec((D_MODEL, MOE_HIDDEN)), wspec((D_MODEL, MOE_HIDDEN)), wspec((MOE_HIDDEN, D_MODEL))],
        out_specs=pl.BlockSpec((MOE_TM, D_MODEL), lambda s, texp, nused, src: (s, 0)),
        scratch_shapes=[pltpu.VMEM((2, MOE_TM, D_MODEL), f32), pltpu.SemaphoreType.DMA((2,)),
                        pltpu.VMEM((D_MODEL, MOE_HIDDEN), bf16), pltpu.VMEM((D_MODEL, MOE_HIDDEN), bf16),
                        pltpu.VMEM((MOE_HIDDEN, D_MODEL), bf16)],
    )
    return pl.pallas_call(
        _moe_kernel,
        grid_spec=grid_spec,
        out_shape=jax.ShapeDtypeStruct((MOE_ROWS, D_MODEL), f32),
        compiler_params=_cparams(("arbitrary",)),
        name="moe",
    )(texp, nused, src, x1, w1, w3, w2)


def _combine_kernel(pos_ref, ys_hbm, x1_ref, route_ref, g_ref, b_ref, o_ref, buf, sem):
    s = pl.program_id(0)
    nsteps = pl.num_programs(0)

    def gather_copy(row, slot, k, r):
        return pltpu.make_async_copy(ys_hbm.at[pl.ds(row, 1), :], buf.at[slot, k, pl.ds(r, 1), :], sem.at[slot])

    def issue(tile, slot):
        def body(r, c):
            n = tile * CMB_TILE + r
            gather_copy(pos_ref[2 * n], slot, 0, r).start()
            gather_copy(pos_ref[2 * n + 1], slot, 1, r).start()
            return c

        lax.fori_loop(0, CMB_TILE, body, 0)

    @pl.when(s == 0)
    def _():
        issue(0, 0)

    @pl.when(s + 1 < nsteps)
    def _():
        issue(s + 1, (s + 1) % 2)

    slot = s % 2

    def wait_body(r, c):
        gather_copy(0, slot, 0, r).wait()
        gather_copy(0, slot, 1, r).wait()
        return c

    lax.fori_loop(0, CMB_TILE, wait_body, 0)
    route = route_ref[...]
    moe = route[:, 2:3] * buf[slot, 0] + route[:, 3:4] * buf[slot, 1]
    o_ref[...] = _layer_norm_rows(DN_ALPHA * x1_ref[...] + moe, g_ref[...], b_ref[...])


def _combine(pos, ys, x1, route, g, b):
    grid_spec = pltpu.PrefetchScalarGridSpec(
        num_scalar_prefetch=1,
        grid=(N_TOK // CMB_TILE,),
        in_specs=[pl.BlockSpec(memory_space=pl.ANY),
                  pl.BlockSpec((CMB_TILE, D_MODEL), lambda s, pos: (s, 0)),
                  pl.BlockSpec((CMB_TILE, ROUTE_LANES), lambda s, pos: (s, 0)),
                  pl.BlockSpec((1, D_MODEL), lambda s, pos: (0, 0)),
                  pl.BlockSpec((1, D_MODEL), lambda s, pos: (0, 0))],
        out_specs=pl.BlockSpec((CMB_TILE, D_MODEL), lambda s, pos: (s, 0)),
        scratch_shapes=[pltpu.VMEM((2, 2, CMB_TILE, D_MODEL), f32), pltpu.SemaphoreType.DMA((2,))],
    )
    return pl.pallas_call(
        _combine_kernel,
        grid_spec=grid_spec,
        out_shape=jax.ShapeDtypeStruct((N_TOK, D_MODEL), f32),
        compiler_params=_cparams(("arbitrary",)),
        name="combine",
    )(pos, ys, x1, route, g, b)


def _dispatch_plan(route):
    e = route[:, 0:2].astype(jnp.int32).reshape(N_PAIRS)
    order = jnp.argsort(e, stable=True).astype(jnp.int32)
    e_sorted = e[order]
    counts = jnp.zeros((MOE_EXPERTS,), jnp.int32).at[e].add(1)
    padded = ((counts + MOE_TM - 1) // MOE_TM) * MOE_TM
    start = jnp.cumsum(counts) - counts
    pstart = jnp.cumsum(padded) - padded
    dest_sorted = pstart[e_sorted] + jnp.arange(N_PAIRS, dtype=jnp.int32) - start[e_sorted]
    src = jnp.zeros((MOE_ROWS,), jnp.int32).at[dest_sorted].set(order // MOE_TOPK)
    pos = jnp.zeros((N_PAIRS,), jnp.int32).at[order].set(dest_sorted)
    pend = jnp.cumsum(padded)
    nused = (pend[-1] // MOE_TM).astype(jnp.int32)
    tile_start = jnp.arange(MOE_MAX_TILES, dtype=jnp.int32) * MOE_TM
    texp = jnp.searchsorted(pend, tile_start, side="right").astype(jnp.int32)
    last_e = jnp.max(jnp.where(counts > 0, jnp.arange(MOE_EXPERTS, dtype=jnp.int32), 0))
    texp = jnp.minimum(texp, last_e)
    return texp, nused.reshape(1), src, pos


def _block_diag(w):
    n, a, b = w.shape
    eye = jnp.eye(n, dtype=w.dtype)
    return (w[:, :, None, :] * eye[:, None, :, None]).reshape(n * a, n * b)


def _ssm_params(a_re, a_im, log_dt, b_re, b_im, c_re, c_im):
    ar, ai = a_re, a_im
    dt = jnp.exp(log_dt)[:, None]
    mag = jnp.exp(ar * dt)
    lb_re = mag * jnp.cos(ai * dt)
    lb_im = mag * jnp.sin(ai * dt)
    den = ar * ar + ai * ai
    nr = lb_re - 1.0
    coef_re = (nr * ar + lb_im * ai) / den
    coef_im = (lb_im * ar - nr * ai) / den
    bb_re = coef_re[..., None] * b_re - coef_im[..., None] * b_im
    bb_im = coef_re[..., None] * b_im + coef_im[..., None] * b_re
    bbre = _block_diag(bb_re.transpose(0, 2, 1)).astype(bf16)
    bbim = _block_diag(bb_im.transpose(0, 2, 1)).astype(bf16)
    ccre = _block_diag(c_re.transpose(0, 2, 1)).astype(bf16)
    ccim = _block_diag(c_im.transpose(0, 2, 1)).astype(bf16)
    return (lb_re.reshape(1, SSM_LANES), lb_im.reshape(1, SSM_LANES), bbre, bbim, ccre, ccim)


def _to_rows(x):
    b, t, d = x.shape
    return x.transpose(1, 0, 2).reshape(t * b, d)


def _from_rows(x, b, t):
    return x.reshape(t, b, x.shape[-1]).transpose(1, 0, 2)


def kernel(x_prompt, x_sample, state_ret, state_ssm_re, state_ssm_im, state_lru, state_conv, w_in, ret_gn_g, ret_gn_b, ssm_a_re, ssm_a_im, ssm_log_dt, ssm_b_re, ssm_b_im, ssm_c_re, ssm_c_im, ssm_d, ssm_w_glu, lru_conv_w, lru_conv_b, lru_wa, lru_ba, lru_wx, lru_bx, lru_lambda, w_branch, w_out, ln1_g, ln1_b, moe_w_group, moe_b_group, moe_w_expert, moe_b_expert, moe_w1, moe_w3, moe_w2, ln2_g, ln2_b):
    x = jnp.concatenate([_to_rows(x_prompt), _to_rows(x_sample)], axis=0)
    cos_t, sin_t = _rope_tables()
    tabs_p = _ret_tables(RET_SUB_T)
    tabs_s = _ret_tables(DEC_SEQ)
    row = lambda v: v.reshape(1, -1)

    outs = [[] for _ in range(10)]
    for l in range(DEPTH):
        z = _inproj(x, w_in[l].astype(bf16))

        s0 = state_ret[l].reshape(DEC_BATCH, 2, 2 * RET_DK, RET_DV)
        y_ret, ret_p, ret_s = _retention(z, cos_t, sin_t, s0, tabs_p, tabs_s,
                                         row(ret_gn_g[l]), row(ret_gn_b[l]))

        sp = _ssm_params(ssm_a_re[l], ssm_a_im[l], ssm_log_dt[l], ssm_b_re[l], ssm_b_im[l],
                         ssm_c_re[l], ssm_c_im[l])
        y_ssm, re_p, im_p, re_s, im_s = _ssm(
            z, state_ssm_re[l].reshape(DEC_BATCH, SSM_LANES), state_ssm_im[l].reshape(DEC_BATCH, SSM_LANES),
            *sp, row(ssm_d[l]), ssm_w_glu[l].astype(bf16))

        conv0 = state_conv[l].transpose(1, 0, 2).reshape(LRU_HIST_S, BRANCH_W)
        y_lru, lru_p, conv_p, lru_s, conv_s = _lru(
            z, state_lru[l], conv0, lru_conv_w[l], row(lru_conv_b[l]),
            _block_diag(lru_wa[l]).astype(bf16), row(lru_ba[l]),
            _block_diag(lru_wx[l]).astype(bf16), row(lru_bx[l]), row(lru_lambda[l]))

        wr = jnp.zeros((D_MODEL, ROUTE_LANES), f32)
        wr = wr.at[:, 0:MOE_GROUPS].set(moe_w_group[l]).at[:, MOE_GROUPS:MOE_GROUPS + MOE_EXPERTS].set(moe_w_expert[l])
        br = jnp.zeros((1, ROUTE_LANES), f32)
        br = br.at[0, 0:MOE_GROUPS].set(moe_b_group[l]).at[0, MOE_GROUPS:MOE_GROUPS + MOE_EXPERTS].set(moe_b_expert[l])
        x1, route = _merge(y_ret, y_ssm, y_lru, z, x, w_branch[l].astype(bf16), w_out[l].astype(bf16),
                           row(ln1_g[l]), row(ln1_b[l]), wr.astype(bf16), br)

        texp, nused, src, pos = _dispatch_plan(route)
        ys = _moe(texp, nused, src, x1, moe_w1[l], moe_w3[l], moe_w2[l])
        x = _combine(pos, ys, x1, route, row(ln2_g[l]), row(ln2_b[l]))

        outs[0].append(ret_p.reshape(BATCH, RET_HEADS, RET_DK, RET_DV))
        outs[1].append(re_p.reshape(BATCH, SSM_GROUPS, SSM_STATE))
        outs[2].append(im_p.reshape(BATCH, SSM_GROUPS, SSM_STATE))
        outs[3].append(lru_p)
        outs[4].append(conv_p.reshape(CONV_W - 1, BATCH, BRANCH_W).transpose(1, 0, 2))
        outs[5].append(ret_s.reshape(DEC_BATCH, RET_HEADS, RET_DK, RET_DV))
        outs[6].append(re_s.reshape(DEC_BATCH, SSM_GROUPS, SSM_STATE))
        outs[7].append(im_s.reshape(DEC_BATCH, SSM_GROUPS, SSM_STATE))
        outs[8].append(lru_s)
        outs[9].append(conv_s.reshape(CONV_W - 1, DEC_BATCH, BRANCH_W).transpose(1, 0, 2))

    y_prompt = _from_rows(x[:N_P], BATCH, SEQ)
    y_sample = _from_rows(x[N_P:], DEC_BATCH, DEC_SEQ)
    return (y_prompt, y_sample) + tuple(jnp.stack(o) for o in outs)
```

```python
import functools

import jax
import jax.numpy as jnp
import numpy as np
from jax import lax
from jax.experimental import pallas as pl
from jax.experimental.pallas import tpu as pltpu

f32 = jnp.float32
bf16 = jnp.bfloat16

D_MODEL = 1024
BATCH = 8
SEQ = 2048
DEPTH = 2
DEC_BATCH = 128
DEC_SEQ = 8
PAST_LEN = 16384
BRANCH_W = 512
N_BRANCH = 3
RET_HEADS = 4
RET_DK = 64
RET_DV = 128
ROPE_BASE = 10000.0
SSM_GROUP = 16
SSM_GROUPS = 32
SSM_STATE = 64
SSM_LANES = SSM_GROUPS * SSM_STATE
LRU_BLOCKS = 8
LRU_BW = 64
CONV_W = 4
LRU_C = 8.0
MOE_GROUPS = 4
MOE_PER_GROUP = 8
MOE_EXPERTS = 32
MOE_TOPK = 2
MOE_HIDDEN = 512
DN_ALPHA = (2.0 * DEPTH) ** 0.25
LN_EPS = 1e-5
D_IN = 6144

V7X_SUBLANES = 8
V7X_LANES = 128
V7X_VMEM_LIMIT = 56 * 1024 * 1024

N_P = BATCH * SEQ
N_S = DEC_BATCH * DEC_SEQ
N_TOK = N_P + N_S
ROW_TILE = 1024
P_TILES = N_P // ROW_TILE
N_TILES = N_TOK // ROW_TILE
P_TC = ROW_TILE // BATCH
COL_TILE = 512
RET_SUB_T = 32
RET_SUB_R = RET_SUB_T * BATCH
RET_SUBS = ROW_TILE // RET_SUB_R
S_BLOCKS = DEC_BATCH // BATCH
S_BLOCK_R = DEC_SEQ * BATCH
MERGE_TILE = 512
MOE_TM = 256
N_PAIRS = N_TOK * MOE_TOPK
MOE_MAX_ITEMS = N_PAIRS // MOE_TM + MOE_EXPERTS - 1
DSP_TILE = 512
CMB_TILE = 256
ROUTE_LANES = 128


def _cparams(sem):
    return pltpu.CompilerParams(dimension_semantics=sem, vmem_limit_bytes=V7X_VMEM_LIMIT)


def _dot(a, b):
    return jnp.dot(a, b, preferred_element_type=f32)


def _dot_nt(a, b):
    return lax.dot_general(a, b, (((1,), (1,)), ((), ())), preferred_element_type=f32)


def _dot_tn(a, b):
    return lax.dot_general(a, b, (((0,), (0,)), ((), ())), preferred_element_type=f32)


def _layer_norm_rows(x, g, b):
    mu = jnp.mean(x, -1, keepdims=True)
    xc = x - mu
    var = jnp.mean(xc * xc, -1, keepdims=True)
    return xc * lax.rsqrt(var + LN_EPS) * g + b


def _inproj_kernel(x_ref, w_ref, o_ref, xb_ref):
    @pl.when(pl.program_id(1) == 0)
    def _():
        xb_ref[...] = x_ref[...].astype(bf16)

    o_ref[...] = _dot(xb_ref[...], w_ref[...])


def _inproj(x, w_bf):
    return pl.pallas_call(
        _inproj_kernel,
        grid=(N_TILES, D_IN // COL_TILE),
        in_specs=[pl.BlockSpec((ROW_TILE, D_MODEL), lambda i, j: (i, 0)),
                  pl.BlockSpec((D_MODEL, COL_TILE), lambda i, j: (0, j))],
        out_specs=pl.BlockSpec((ROW_TILE, COL_TILE), lambda i, j: (i, j)),
        out_shape=jax.ShapeDtypeStruct((N_TOK, D_IN), f32),
        scratch_shapes=[pltpu.VMEM((ROW_TILE, D_MODEL), bf16)],
        compiler_params=_cparams(("arbitrary", "arbitrary")),
        name="inproj",
    )(x, w_bf)


def _ret_block(q, k, v, g, cosb, sinb, mask_ref, qdec_ref, kdec_ref, cdec_ref, scat_ref, gng, gnb):
    rows = q.shape[0]
    lane_qk = lax.broadcasted_iota(jnp.int32, (rows, 2 * V7X_LANES), 1)
    first_half = (lane_qk & (RET_DK - 1)) < (RET_DK // 2)

    def rope(x):
        partner = jnp.where(first_half, pltpu.roll(x, 2 * V7X_LANES - RET_DK // 2, 1),
                            pltpu.roll(x, RET_DK // 2, 1))
        return x * cosb + partner * sinb

    q = rope(q)
    k = rope(k) * (RET_DK ** -0.5)
    kd = k * kdec_ref[...]
    lane = lax.broadcasted_iota(jnp.int32, (rows, V7X_LANES), 1)
    row_b = lax.broadcasted_iota(jnp.int32, (rows, V7X_LANES), 0) & (BATCH - 1)
    outs = []
    for p in range(2):
        qp = q[:, p * V7X_LANES:(p + 1) * V7X_LANES]
        kp = k[:, p * V7X_LANES:(p + 1) * V7X_LANES].astype(bf16)
        kdp = kd[:, p * V7X_LANES:(p + 1) * V7X_LANES]
        s_old = scat_ref[p]
        s_bf = s_old.astype(bf16)
        s_new = s_old * jnp.concatenate([cdec_ref[p]] * BATCH, axis=1)
        for hh in range(2):
            h = 2 * p + hh
            head_lanes = (lane >= RET_DK) if hh else (lane < RET_DK)
            qh = jnp.where(head_lanes, qp, 0.0).astype(bf16)
            kdh = jnp.where(head_lanes, kdp, 0.0).astype(bf16)
            vh = v[:, h * RET_DV:(h + 1) * RET_DV]
            vh_bf = vh.astype(bf16)
            sc = _dot_nt(qh, kp) * mask_ref[h]
            o = _dot(sc.astype(bf16), vh_bf)
            cross = _dot(qh, s_bf)
            oc = jnp.zeros((rows, RET_DV), f32)
            for b in range(BATCH):
                oc = oc + jnp.where(row_b == b, cross[:, b * RET_DV:(b + 1) * RET_DV], 0.0)
            o = o + oc * qdec_ref[h]
            vcat = jnp.concatenate([jnp.where(row_b == b, vh_bf, jnp.zeros_like(vh_bf))
                                    for b in range(BATCH)], axis=1)
            s_new = s_new + _dot_tn(kdh, vcat)
            mu = jnp.mean(o, -1, keepdims=True)
            oc2 = o - mu
            var = jnp.mean(oc2 * oc2, -1, keepdims=True)
            outs.append(oc2 * lax.rsqrt(var + LN_EPS))
        scat_ref[p] = s_new
    o = jnp.concatenate(outs, axis=1) * gng + gnb
    return jax.nn.silu(g) * o


def _ret_kernel(zqk_ref, zv_ref, zg_ref, cos_ref, sin_ref, s0_ref,
                mask_p_ref, qdec_p_ref, kdec_p_ref, cdec_p_ref,
                mask_s_ref, qdec_s_ref, kdec_s_ref, cdec_s_ref,
                gng_ref, gnb_ref,
                y_ref, retp_ref, rets_ref, scat_ref):
    i = pl.program_id(0)
    gng = gng_ref[...]
    gnb = gnb_ref[...]

    @pl.when(i == 0)
    def _():
        scat_ref[...] = jnp.zeros_like(scat_ref)

    @pl.when(i < P_TILES)
    def _():
        for sc in range(RET_SUBS):
            r0 = sc * RET_SUB_R
            rs = slice(r0, r0 + RET_SUB_R)
            y_ref[rs, :] = _ret_block(
                zqk_ref[rs, 0:256], zqk_ref[rs, 256:512], zv_ref[rs, :], zg_ref[rs, :],
                cos_ref[rs, :], sin_ref[rs, :],
                mask_p_ref, qdec_p_ref, kdec_p_ref, cdec_p_ref, scat_ref, gng, gnb)

    @pl.when(i == P_TILES - 1)
    def _():
        for b in range(BATCH):
            for p in range(2):
                retp_ref[b, p] = scat_ref[p, :, b * RET_DV:(b + 1) * RET_DV]

    @pl.when(i >= P_TILES)
    def _():
        bb = i - P_TILES
        for b in range(BATCH):
            for p in range(2):
                scat_ref[p, :, b * RET_DV:(b + 1) * RET_DV] = s0_ref[b, p]

        def rows_of(ref, c0, c1):
            return jnp.concatenate(
                [ref[pl.ds(pl.multiple_of(t * DEC_BATCH + bb * BATCH, BATCH), BATCH), c0:c1]
                 for t in range(DEC_SEQ)], axis=0)

        y = _ret_block(
            rows_of(zqk_ref, 0, 256), rows_of(zqk_ref, 256, 512), rows_of(zv_ref, 0, 512),
            rows_of(zg_ref, 0, 512), rows_of(cos_ref, 0, 256), rows_of(sin_ref, 0, 256),
            mask_s_ref, qdec_s_ref, kdec_s_ref, cdec_s_ref, scat_ref, gng, gnb)
        for t in range(DEC_SEQ):
            y_ref[pl.ds(pl.multiple_of(t * DEC_BATCH + bb * BATCH, BATCH), BATCH), :] = (
                y[t * BATCH:(t + 1) * BATCH, :])
        for b in range(BATCH):
            for p in range(2):
                rets_ref[b, p] = scat_ref[p, :, b * RET_DV:(b + 1) * RET_DV]


def _ret_tables(tc):
    rows = tc * BATCH
    log_g = jnp.log1p(-jnp.exp2(-5.0 - jnp.arange(RET_HEADS, dtype=f32)))
    t_idx = (jnp.arange(rows) // BATCH).astype(f32)
    b_idx = jnp.arange(rows) % BATCH
    rel = t_idx[:, None] - t_idx[None, :]
    same = b_idx[:, None] == b_idx[None, :]
    decay = jnp.exp(log_g[:, None, None] * jnp.maximum(rel, 0.0))
    mask = jnp.where((rel >= 0) & same, decay, 0.0)
    qdec = jnp.exp(log_g[:, None] * (t_idx[None, :] + 1.0))
    qdec = jnp.broadcast_to(qdec[:, :, None], (RET_HEADS, rows, RET_DV))
    kdec = jnp.exp(log_g[:, None] * (tc - 1.0 - t_idx[None, :]))
    kdec = jnp.broadcast_to(kdec.T[:, :, None], (rows, RET_HEADS, RET_DK)).reshape(rows, RET_HEADS * RET_DK)
    cdec = jnp.exp(log_g * tc)
    cdec = jnp.broadcast_to(cdec[:, None, None], (RET_HEADS, RET_DK, RET_DV)).reshape(2, 2 * RET_DK, RET_DV)
    return mask.astype(f32), qdec.astype(f32), kdec.astype(f32), cdec.astype(f32)


def _rope_tables():
    half = RET_DK // 2
    inv = ROPE_BASE ** (-jnp.arange(half, dtype=f32) / half)
    pos_p = jnp.arange(SEQ, dtype=f32)
    pos_s = PAST_LEN + jnp.arange(DEC_SEQ, dtype=f32)

    def tab(pos, nb):
        ang = pos[:, None] * inv[None, :]
        cos = jnp.cos(ang)
        sin = jnp.sin(ang)
        cos_h = jnp.concatenate([cos, cos], axis=1)
        sin_h = jnp.concatenate([-sin, sin], axis=1)
        cos_f = jnp.tile(cos_h, (1, RET_HEADS))
        sin_f = jnp.tile(sin_h, (1, RET_HEADS))
        t = pos.shape[0]
        cos_f = jnp.broadcast_to(cos_f[:, None, :], (t, nb, 256)).reshape(t * nb, 256)
        sin_f = jnp.broadcast_to(sin_f[:, None, :], (t, nb, 256)).reshape(t * nb, 256)
        return cos_f, sin_f

    cp, sp = tab(pos_p, BATCH)
    cs, ss = tab(pos_s, DEC_BATCH)
    return jnp.concatenate([cp, cs], axis=0), jnp.concatenate([sp, ss], axis=0)


def _retention(z, cos_t, sin_t, s0_s, tabs_p, tabs_s, gng, gnb):
    n_steps = P_TILES + S_BLOCKS
    tile = lambda i: jnp.minimum(i, P_TILES)
    sblk = lambda i: jnp.maximum(i - P_TILES, 0)
    full = lambda a: pl.BlockSpec(a.shape, lambda i, _n=a.ndim: (0,) * _n)
    state_blk = (BATCH, 2, 2 * RET_DK, RET_DV)
    ins = [z, z, z, cos_t, sin_t, s0_s, *tabs_p, *tabs_s, gng, gnb]
    in_specs = [
        pl.BlockSpec((ROW_TILE, COL_TILE), lambda i: (tile(i), 0)),
        pl.BlockSpec((ROW_TILE, COL_TILE), lambda i: (tile(i), 1)),
        pl.BlockSpec((ROW_TILE, COL_TILE), lambda i: (tile(i), 2)),
        pl.BlockSpec((ROW_TILE, 256), lambda i: (tile(i), 0)),
        pl.BlockSpec((ROW_TILE, 256), lambda i: (tile(i), 0)),
        pl.BlockSpec(state_blk, lambda i: (sblk(i), 0, 0, 0)),
    ] + [full(a) for a in (*tabs_p, *tabs_s, gng, gnb)]
    return pl.pallas_call(
        _ret_kernel,
        grid=(n_steps,),
        in_specs=in_specs,
        out_specs=[pl.BlockSpec((ROW_TILE, BRANCH_W), lambda i: (tile(i), 0)),
                   pl.BlockSpec(state_blk, lambda i: (0, 0, 0, 0)),
                   pl.BlockSpec(state_blk, lambda i: (sblk(i), 0, 0, 0))],
        out_shape=[jax.ShapeDtypeStruct((N_TOK, BRANCH_W), f32),
                   jax.ShapeDtypeStruct((BATCH, 2, 2 * RET_DK, RET_DV), f32),
                   jax.ShapeDtypeStruct((DEC_BATCH, 2, 2 * RET_DK, RET_DV), f32)],
        scratch_shapes=[pltpu.VMEM((2, 2 * RET_DK, BATCH * RET_DV), f32)],
        compiler_params=_cparams(("arbitrary",)),
        name="retention",
    )(*ins)


SSM_LB = 512
SSM_RC = 256


def _ssm_scan(bre_ref, bim_ref, lre_ref, lim_ref, h_re0, h_im0, row0, nb_rows, steps, lb, unroll):
    ls = slice(lb * SSM_LB, (lb + 1) * SSM_LB)
    a_re = jnp.broadcast_to(lre_ref[:, ls], (V7X_SUBLANES, SSM_LB))
    a_im = jnp.broadcast_to(lim_ref[:, ls], (V7X_SUBLANES, SSM_LB))

    def step(t, carry):
        h_re, h_im = carry
        r = pl.multiple_of(row0 + t * nb_rows, V7X_SUBLANES)
        n_re = a_re * h_re - a_im * h_im + bre_ref[pl.ds(r, V7X_SUBLANES), ls]
        n_im = a_re * h_im + a_im * h_re + bim_ref[pl.ds(r, V7X_SUBLANES), ls]
        bre_ref[pl.ds(r, V7X_SUBLANES), ls] = n_re
        bim_ref[pl.ds(r, V7X_SUBLANES), ls] = n_im
        return n_re, n_im

    return lax.fori_loop(0, steps, step, (h_re0, h_im0), unroll=unroll)


def _ssm_kernel(zs_ref, h0re_ref, h0im_ref, lre_ref, lim_ref, bbre_ref, bbim_ref, ccre_ref, ccim_ref,
                d_ref, wglu_ref,
                y_ref, pre_ref, pim_ref, sre_ref, sim_ref,
                bre_ref, bim_ref, hre_ref, him_ref):
    i = pl.program_id(0)
    for rc in range(ROW_TILE // SSM_RC):
        rs = slice(rc * SSM_RC, (rc + 1) * SSM_RC)
        ub = zs_ref[rs, :].astype(bf16)
        bre_ref[rs, :] = _dot(ub, bbre_ref[...])
        bim_ref[rs, :] = _dot(ub, bbim_ref[...])

    @pl.when(i == 0)
    def _():
        hre_ref[...] = jnp.zeros_like(hre_ref)
        him_ref[...] = jnp.zeros_like(him_ref)

    @pl.when(i < P_TILES)
    def _():
        for lb in range(SSM_LANES // SSM_LB):
            ls = slice(lb * SSM_LB, (lb + 1) * SSM_LB)
            h_re, h_im = _ssm_scan(bre_ref, bim_ref, lre_ref, lim_ref, hre_ref[:, ls], him_ref[:, ls],
                                   0, BATCH, P_TC, lb, 8)
            hre_ref[:, ls] = h_re
            him_ref[:, ls] = h_im
        pre_ref[...] = hre_ref[...]
        pim_ref[...] = him_ref[...]

    @pl.when(i >= P_TILES)
    def _():
        def per_row_tile(rt, c):
            r0 = pl.multiple_of(rt * V7X_SUBLANES, V7X_SUBLANES)
            for lb in range(SSM_LANES // SSM_LB):
                ls = slice(lb * SSM_LB, (lb + 1) * SSM_LB)
                _ssm_scan(bre_ref, bim_ref, lre_ref, lim_ref,
                          h0re_ref[pl.ds(r0, V7X_SUBLANES), ls], h0im_ref[pl.ds(r0, V7X_SUBLANES), ls],
                          r0, DEC_BATCH, DEC_SEQ, lb, True)
            return c

        lax.fori_loop(0, DEC_BATCH // V7X_SUBLANES, per_row_tile, 0)
        last = (DEC_SEQ - 1) * DEC_BATCH
        sre_ref[...] = bre_ref[last:last + DEC_BATCH, :]
        sim_ref[...] = bim_ref[last:last + DEC_BATCH, :]

    for rc in range(ROW_TILE // SSM_RC):
        rs = slice(rc * SSM_RC, (rc + 1) * SSM_RC)
        y = (_dot(bre_ref[rs, :].astype(bf16), ccre_ref[...]) - _dot(bim_ref[rs, :].astype(bf16), ccim_ref[...])
             + d_ref[...] * zs_ref[rs, :])
        zz = jax.nn.gelu(y)
        y_ref[rs, :] = zz * jax.nn.sigmoid(_dot(zz.astype(bf16), wglu_ref[...]))


def _ssm(z, h0re, h0im, lre, lim, bbre, bbim, ccre, ccim, dvec, wglu):
    full = lambda a: pl.BlockSpec(a.shape, lambda i, _n=a.ndim: (0,) * _n)
    consts = (h0re, h0im, lre, lim, bbre, bbim, ccre, ccim, dvec, wglu)
    return pl.pallas_call(
        _ssm_kernel,
        grid=(N_TILES,),
        in_specs=[pl.BlockSpec((ROW_TILE, COL_TILE), lambda i: (i, 3))] + [full(a) for a in consts],
        out_specs=[pl.BlockSpec((ROW_TILE, BRANCH_W), lambda i: (i, 0)),
                   pl.BlockSpec((BATCH, SSM_LANES), lambda i: (0, 0)),
                   pl.BlockSpec((BATCH, SSM_LANES), lambda i: (0, 0)),
                   pl.BlockSpec((DEC_BATCH, SSM_LANES), lambda i: (0, 0)),
                   pl.BlockSpec((DEC_BATCH, SSM_LANES), lambda i: (0, 0))],
        out_shape=[jax.ShapeDtypeStruct((N_TOK, BRANCH_W), f32),
                   jax.ShapeDtypeStruct((BATCH, SSM_LANES), f32),
                   jax.ShapeDtypeStruct((BATCH, SSM_LANES), f32),
                   jax.ShapeDtypeStruct((DEC_BATCH, SSM_LANES), f32),
                   jax.ShapeDtypeStruct((DEC_BATCH, SSM_LANES), f32)],
        scratch_shapes=[pltpu.VMEM((ROW_TILE, SSM_LANES), f32), pltpu.VMEM((ROW_TILE, SSM_LANES), f32),
                        pltpu.VMEM((BATCH, SSM_LANES), f32), pltpu.VMEM((BATCH, SSM_LANES), f32)],
        compiler_params=_cparams(("arbitrary",)),
        name="ssm",
    )(z, *consts)


LRU_HIST_P = (CONV_W - 1) * BATCH
LRU_HIST_S = (CONV_W - 1) * DEC_BATCH


def _lru_gates(xe_ref, nb_rows, cw_ref, cb_ref, wa_ref, ba_ref, wx_ref, bx_ref, lam_ref):
    xc = cb_ref[...] + xe_ref[0:ROW_TILE, :] * cw_ref[0:1, :]
    for j in range(1, CONV_W):
        xc = xc + xe_ref[j * nb_rows:j * nb_rows + ROW_TILE, :] * cw_ref[j:j + 1, :]
    xcb = xc.astype(bf16)
    r = jax.nn.sigmoid(_dot(xcb, wa_ref[...]) + ba_ref[...])
    ig = jax.nn.sigmoid(_dot(xcb, wx_ref[...]) + bx_ref[...])
    log_a = -LRU_C * r * jax.nn.softplus(-lam_ref[...])
    a = jnp.exp(log_a)
    b = jnp.sqrt(-jnp.tanh(log_a) * (jnp.exp(2.0 * log_a) + 1.0)) * (ig * xc)
    return a, b


def _lru_kernel(zx_ref, zg_ref, h0_ref, conv0_ref, cw_ref, cb_ref, wa_ref, ba_ref, wx_ref, bx_ref, lam_ref,
                y_ref, hp_ref, convp_ref, hs_ref, convs_ref,
                xe_ref, a_ref, b_ref, hc_ref):
    i = pl.program_id(0)
    params = (cw_ref, cb_ref, wa_ref, ba_ref, wx_ref, bx_ref, lam_ref)

    @pl.when(i == 0)
    def _():
        xe_ref[0:LRU_HIST_P, :] = jnp.zeros((LRU_HIST_P, BRANCH_W), f32)
        hc_ref[...] = jnp.zeros_like(hc_ref)

    @pl.when(i < P_TILES)
    def _():
        xe_ref[LRU_HIST_P:LRU_HIST_P + ROW_TILE, :] = zx_ref[...]
        a, b = _lru_gates(xe_ref, BATCH, *params)
        a_ref[...] = a
        b_ref[...] = b
        hist = xe_ref[ROW_TILE:ROW_TILE + LRU_HIST_P, :]
        xe_ref[0:LRU_HIST_P, :] = hist
        convp_ref[...] = hist

        def step(t, h):
            r = pl.multiple_of(t * BATCH, BATCH)
            h = a_ref[pl.ds(r, BATCH), :] * h + b_ref[pl.ds(r, BATCH), :]
            b_ref[pl.ds(r, BATCH), :] = h
            return h

        h = lax.fori_loop(0, P_TC, step, hc_ref[...], unroll=8)
        hc_ref[...] = h
        hp_ref[...] = h

    @pl.when(i >= P_TILES)
    def _():
        xe_ref[0:LRU_HIST_S, :] = conv0_ref[...]
        xe_ref[LRU_HIST_S:LRU_HIST_S + ROW_TILE, :] = zx_ref[...]
        a, b = _lru_gates(xe_ref, DEC_BATCH, *params)
        a_ref[...] = a
        b_ref[...] = b
        convs_ref[...] = xe_ref[ROW_TILE:ROW_TILE + LRU_HIST_S, :]

        def per_row_tile(rt, c):
            r0 = pl.multiple_of(rt * V7X_SUBLANES, V7X_SUBLANES)
            h = h0_ref[pl.ds(r0, V7X_SUBLANES), :]
            for t in range(DEC_SEQ):
                r = pl.multiple_of(t * DEC_BATCH + r0, V7X_SUBLANES)
                h = a_ref[pl.ds(r, V7X_SUBLANES), :] * h + b_ref[pl.ds(r, V7X_SUBLANES), :]
                b_ref[pl.ds(r, V7X_SUBLANES), :] = h
            return c

        lax.fori_loop(0, DEC_BATCH // V7X_SUBLANES, per_row_tile, 0)
        last = (DEC_SEQ - 1) * DEC_BATCH
        hs_ref[...] = b_ref[last:last + DEC_BATCH, :]

    y_ref[...] = b_ref[...] * jax.nn.gelu(zg_ref[...])


def _lru(z, h0, conv0, cw, cb, wa, ba, wx, bx, lam):
    full = lambda a: pl.BlockSpec(a.shape, lambda i, _n=a.ndim: (0,) * _n)
    consts = (h0, conv0, cw, cb, wa, ba, wx, bx, lam)
    return pl.pallas_call(
        _lru_kernel,
        grid=(N_TILES,),
        in_specs=[pl.BlockSpec((ROW_TILE, COL_TILE), lambda i: (i, 4)),
                  pl.BlockSpec((ROW_TILE, COL_TILE), lambda i: (i, 5))] + [full(a) for a in consts],
        out_specs=[pl.BlockSpec((ROW_TILE, BRANCH_W), lambda i: (i, 0)),
                   pl.BlockSpec((BATCH, BRANCH_W), lambda i: (0, 0)),
                   pl.BlockSpec((LRU_HIST_P, BRANCH_W), lambda i: (0, 0)),
                   pl.BlockSpec((DEC_BATCH, BRANCH_W), lambda i: (0, 0)),
                   pl.BlockSpec((LRU_HIST_S, BRANCH_W), lambda i: (0, 0))],
        out_shape=[jax.ShapeDtypeStruct((N_TOK, BRANCH_W), f32),
                   jax.ShapeDtypeStruct((BATCH, BRANCH_W), f32),
                   jax.ShapeDtypeStruct((LRU_HIST_P, BRANCH_W), f32),
                   jax.ShapeDtypeStruct((DEC_BATCH, BRANCH_W), f32),
                   jax.ShapeDtypeStruct((LRU_HIST_S, BRANCH_W), f32)],
        scratch_shapes=[pltpu.VMEM((ROW_TILE + LRU_HIST_S, BRANCH_W), f32),
                        pltpu.VMEM((ROW_TILE, BRANCH_W), f32), pltpu.VMEM((ROW_TILE, BRANCH_W), f32),
                        pltpu.VMEM((BATCH, BRANCH_W), f32)],
        compiler_params=_cparams(("arbitrary",)),
        name="lru",
    )(z, z, *consts)


def _merge_kernel(yr_ref, ys_ref, yl_ref, zm_ref, x_ref, wb_ref, wo_ref, g_ref, b_ref, wr_ref, br_ref,
                  x1_ref, route_ref, cnt_out_ref, cnt_ref):
    merged = jnp.zeros((MERGE_TILE, D_MODEL), f32)
    for n, y_ref in enumerate((yr_ref, ys_ref, yl_ref)):
        proj = _dot(y_ref[...].astype(bf16), wb_ref[n])
        gate = jax.nn.sigmoid(zm_ref[:, n * D_MODEL:(n + 1) * D_MODEL])
        merged = merged + gate * proj
    mix = _dot(merged.astype(bf16), wo_ref[...])
    x1 = _layer_norm_rows(DN_ALPHA * x_ref[...] + mix, g_ref[...], b_ref[...])
    x1_ref[...] = x1

    logits = _dot(x1.astype(bf16), wr_ref[...]) + br_ref[...]
    lane = lax.broadcasted_iota(jnp.int32, (MERGE_TILE, ROUTE_LANES), 1).astype(f32)
    big = jnp.float32(ROUTE_LANES)
    neg = jnp.float32(-jnp.inf)
    is_g = lane < MOE_GROUPS
    lg = jnp.where(is_g, logits, neg)
    mg = jnp.max(lg, -1, keepdims=True)
    gsel = jnp.min(jnp.where(lg == mg, lane, big), -1, keepdims=True)
    sum_g = jnp.sum(jnp.where(is_g, jnp.exp(lg - mg), 0.0), -1, keepdims=True)
    pg_sel = 1.0 / sum_g
    lo = MOE_GROUPS + gsel * MOE_PER_GROUP
    is_e = jnp.abs(lane - lo - 0.5 * (MOE_PER_GROUP - 1)) < 0.5 * MOE_PER_GROUP
    le = jnp.where(is_e, logits, neg)
    me = jnp.max(le, -1, keepdims=True)
    ex = jnp.where(is_e, jnp.exp(le - me), 0.0)
    pe = jnp.where(is_e, ex / jnp.sum(ex, -1, keepdims=True), -1.0)
    v1 = jnp.max(pe, -1, keepdims=True)
    i1 = jnp.min(jnp.where(pe == v1, lane, big), -1, keepdims=True)
    pe2 = jnp.where(lane == i1, -1.0, pe)
    v2 = jnp.max(pe2, -1, keepdims=True)
    i2 = jnp.min(jnp.where(pe2 == v2, lane, big), -1, keepdims=True)
    vsum = v1 + v2
    w1 = pg_sel * v1 / vsum
    w2 = pg_sel * v2 / vsum
    e1 = i1 - MOE_GROUPS
    e2 = i2 - MOE_GROUPS

    @pl.when(pl.program_id(0) == 0)
    def _():
        cnt_ref[...] = jnp.zeros_like(cnt_ref)

    oh1 = lane == e1
    oh2 = lane == e2
    ohs = jnp.where(oh1, 1.0, jnp.where(oh2, 1.0, 0.0))
    r_i = lax.broadcasted_iota(jnp.int32, (MERGE_TILE, MERGE_TILE), 0)
    c_i = lax.broadcasted_iota(jnp.int32, (MERGE_TILE, MERGE_TILE), 1)
    strict_lower = jnp.where(c_i < r_i, 1.0, 0.0).astype(bf16)
    before = _dot(strict_lower, ohs.astype(bf16)) + cnt_ref[0:1, :]
    rank1 = jnp.sum(jnp.where(oh1, before, 0.0), -1, keepdims=True)
    rank2 = jnp.sum(jnp.where(oh2, before, 0.0), -1, keepdims=True)
    cnt_ref[0:1, :] = cnt_ref[0:1, :] + jnp.sum(ohs, 0, keepdims=True)
    cnt_out_ref[...] = cnt_ref[...]

    route = jnp.zeros((MERGE_TILE, ROUTE_LANES), f32)
    for k, val in enumerate((e1, e2, w1, w2, rank1, rank2)):
        route = jnp.where(lane == k, val, route)
    route_ref[...] = route


def _merge(y_ret, y_ssm, y_lru, z, x, wb, wo, g, b, wr, br):
    full = lambda a: pl.BlockSpec(a.shape, lambda i, _n=a.ndim: (0,) * _n)
    row = lambda w: pl.BlockSpec((MERGE_TILE, w), lambda i: (i, 0))
    consts = (wb, wo, g, b, wr, br)
    return pl.pallas_call(
        _merge_kernel,
        grid=(N_TOK // MERGE_TILE,),
        in_specs=[row(BRANCH_W), row(BRANCH_W), row(BRANCH_W),
                  pl.BlockSpec((MERGE_TILE, N_BRANCH * D_MODEL), lambda i: (i, 1)),
                  row(D_MODEL)] + [full(a) for a in consts],
        out_specs=[row(D_MODEL), row(ROUTE_LANES),
                   pl.BlockSpec((V7X_SUBLANES, ROUTE_LANES), lambda i: (0, 0))],
        out_shape=[jax.ShapeDtypeStruct((N_TOK, D_MODEL), f32),
                   jax.ShapeDtypeStruct((N_TOK, ROUTE_LANES), f32),
                   jax.ShapeDtypeStruct((V7X_SUBLANES, ROUTE_LANES), f32)],
        scratch_shapes=[pltpu.VMEM((V7X_SUBLANES, ROUTE_LANES), f32)],
        compiler_params=_cparams(("arbitrary",)),
        name="merge",
    )(y_ret, y_ssm, y_lru, z, x, *consts)


def _dispatch_kernel(pos_ref, x1_ref, xs_hbm, sem):
    base = pl.program_id(0) * DSP_TILE

    def row_copy(r, dst_row):
        return pltpu.make_async_copy(x1_ref.at[pl.ds(r, 1), :], xs_hbm.at[pl.ds(dst_row, 1), :], sem.at[0])

    def issue(r, c):
        n = base + r
        row_copy(r, pos_ref[2 * n]).start()
        row_copy(r, pos_ref[2 * n + 1]).start()
        return c

    lax.fori_loop(0, DSP_TILE, issue, 0, unroll=8)
    for _ in range(MOE_TOPK):
        pltpu.make_async_copy(x1_ref, xs_hbm.at[pl.ds(0, DSP_TILE), :], sem.at[0]).wait()


def _dispatch(pos, x1):
    grid_spec = pltpu.PrefetchScalarGridSpec(
        num_scalar_prefetch=1,
        grid=(N_TOK // DSP_TILE,),
        in_specs=[pl.BlockSpec((DSP_TILE, D_MODEL), lambda s, pos: (s, 0))],
        out_specs=pl.BlockSpec(memory_space=pl.ANY),
        scratch_shapes=[pltpu.SemaphoreType.DMA((1,))],
    )
    return pl.pallas_call(
        _dispatch_kernel,
        grid_spec=grid_spec,
        out_shape=jax.ShapeDtypeStruct((N_PAIRS, D_MODEL), f32),
        compiler_params=_cparams(("arbitrary",)),
        name="dispatch",
    )(pos, x1)


def _moe_kernel(wt_ref, we_ref, wlo_ref, whi_ref, wfirst_ref, nw_ref,
                xs_ref, w1_ref, w3_ref, w2_ref, ys_ref, w1b, w3b, w2b):
    w = pl.program_id(0)

    @pl.when(w < nw_ref[0])
    def _():
        prev = we_ref[jnp.maximum(w - 1, 0)]

        @pl.when((w == 0) | (we_ref[w] != prev))
        def _():
            w1b[...] = w1_ref[0].astype(bf16)
            w3b[...] = w3_ref[0].astype(bf16)
            w2b[...] = w2_ref[0].astype(bf16)

        xt = xs_ref[...].astype(bf16)
        h = jax.nn.silu(_dot(xt, w1b[...])) * _dot(xt, w3b[...])
        res = _dot(h.astype(bf16), w2b[...])
        row = lax.broadcasted_iota(jnp.int32, (MOE_TM, D_MODEL), 0)
        mine = jnp.where(row >= wlo_ref[w], row, MOE_TM) < whi_ref[w]

        @pl.when(wfirst_ref[w] == 1)
        def _():
            ys_ref[...] = jnp.where(mine, res, 0.0)

        @pl.when(wfirst_ref[w] == 0)
        def _():
            ys_ref[...] = jnp.where(mine, res, ys_ref[...])


def _moe(plan, xs, w1, w3, w2):
    wspec = lambda shp: pl.BlockSpec((1,) + shp, lambda w, wt, we, *_: (we[w], 0, 0))
    grid_spec = pltpu.PrefetchScalarGridSpec(
        num_scalar_prefetch=6,
        grid=(MOE_MAX_ITEMS,),
        in_specs=[pl.BlockSpec((MOE_TM, D_MODEL), lambda w, wt, *_: (wt[w], 0)),
                  wspec((D_MODEL, MOE_HIDDEN)), wspec((D_MODEL, MOE_HIDDEN)), wspec((MOE_HIDDEN, D_MODEL))],
        out_specs=pl.BlockSpec((MOE_TM, D_MODEL), lambda w, wt, *_: (wt[w], 0)),
        scratch_shapes=[pltpu.VMEM((D_MODEL, MOE_HIDDEN), bf16), pltpu.VMEM((D_MODEL, MOE_HIDDEN), bf16),
                        pltpu.VMEM((MOE_HIDDEN, D_MODEL), bf16)],
    )
    return pl.pallas_call(
        _moe_kernel,
        grid_spec=grid_spec,
        out_shape=jax.ShapeDtypeStruct((N_PAIRS, D_MODEL), f32),
        compiler_params=_cparams(("arbitrary",)),
        name="moe",
    )(*plan, xs, w1, w3, w2)


def _combine_kernel(pos_ref, ys_hbm, x1_ref, route_ref, g_ref, b_ref, o_ref, buf, sem):
    s = pl.program_id(0)
    nsteps = pl.num_programs(0)

    def gather_copy(row, slot, k, r):
        return pltpu.make_async_copy(ys_hbm.at[pl.ds(row, 1), :], buf.at[slot, k, pl.ds(r, 1), :], sem.at[slot])

    def issue(tile, slot):
        def body(r, c):
            n = tile * CMB_TILE + r
            gather_copy(pos_ref[2 * n], slot, 0, r).start()
            gather_copy(pos_ref[2 * n + 1], slot, 1, r).start()
            return c

        lax.fori_loop(0, CMB_TILE, body, 0, unroll=8)

    @pl.when(s == 0)
    def _():
        issue(0, 0)

    @pl.when(s + 1 < nsteps)
    def _():
        issue(s + 1, (s + 1) % 2)

    slot = s % 2
    for k in range(MOE_TOPK):
        pltpu.make_async_copy(ys_hbm.at[pl.ds(0, CMB_TILE), :], buf.at[slot, k], sem.at[slot]).wait()
    route = route_ref[...]
    moe = route[:, 2:3] * buf[slot, 0] + route[:, 3:4] * buf[slot, 1]
    o_ref[...] = _layer_norm_rows(DN_ALPHA * x1_ref[...] + moe, g_ref[...], b_ref[...])


def _combine(pos, ys, x1, route, g, b):
    grid_spec = pltpu.PrefetchScalarGridSpec(
        num_scalar_prefetch=1,
        grid=(N_TOK // CMB_TILE,),
        in_specs=[pl.BlockSpec(memory_space=pl.ANY),
                  pl.BlockSpec((CMB_TILE, D_MODEL), lambda s, pos: (s, 0)),
                  pl.BlockSpec((CMB_TILE, ROUTE_LANES), lambda s, pos: (s, 0)),
                  pl.BlockSpec((1, D_MODEL), lambda s, pos: (0, 0)),
                  pl.BlockSpec((1, D_MODEL), lambda s, pos: (0, 0))],
        out_specs=pl.BlockSpec((CMB_TILE, D_MODEL), lambda s, pos: (s, 0)),
        scratch_shapes=[pltpu.VMEM((2, 2, CMB_TILE, D_MODEL), f32), pltpu.SemaphoreType.DMA((2,))],
    )
    return pl.pallas_call(
        _combine_kernel,
        grid_spec=grid_spec,
        out_shape=jax.ShapeDtypeStruct((N_TOK, D_MODEL), f32),
        compiler_params=_cparams(("arbitrary",)),
        name="combine",
    )(pos, ys, x1, route, g, b)


def _lookup(table, idx):
    ar = jnp.arange(MOE_EXPERTS, dtype=jnp.int32)
    return jnp.sum(jnp.where(idx[..., None] == ar, table, 0), axis=-1)


def _dispatch_plan(route, cnt):
    i32 = jnp.int32
    e = route[:, 0:2].astype(i32)
    rank = route[:, 4:6].astype(i32)
    counts = cnt[0, :MOE_EXPERTS].astype(i32)
    ends = jnp.cumsum(counts)
    starts = ends - counts
    pos = (_lookup(starts, e) + rank).reshape(N_PAIRS)

    first_tile = starts // MOE_TM
    last_tile = (ends - 1) // MOE_TM
    ntiles = jnp.where(counts > 0, last_tile - first_tile + 1, 0)
    item_end = jnp.cumsum(ntiles)
    n_items = item_end[-1]
    w = jnp.minimum(jnp.arange(MOE_MAX_ITEMS, dtype=i32), n_items - 1)
    we = jnp.sum((item_end[None, :] <= w[:, None]).astype(i32), axis=-1)
    wt = _lookup(first_tile, we) + w - _lookup(item_end - ntiles, we)
    wlo = jnp.maximum(_lookup(starts, we) - wt * MOE_TM, 0)
    whi = jnp.minimum(_lookup(ends, we) - wt * MOE_TM, MOE_TM)
    wfirst = jnp.concatenate([jnp.ones((1,), i32), (wt[1:] != wt[:-1]).astype(i32)])
    return pos, (wt, we, wlo, whi, wfirst, n_items.reshape(1))


def _block_diag(w):
    n, a, b = w.shape
    eye = jnp.eye(n, dtype=w.dtype)
    return (w[:, :, None, :] * eye[:, None, :, None]).reshape(n * a, n * b)


def _ssm_params(a_re, a_im, log_dt, b_re, b_im, c_re, c_im):
    ar, ai = a_re, a_im
    dt = jnp.exp(log_dt)[:, None]
    mag = jnp.exp(ar * dt)
    lb_re = mag * jnp.cos(ai * dt)
    lb_im = mag * jnp.sin(ai * dt)
    den = ar * ar + ai * ai
    nr = lb_re - 1.0
    coef_re = (nr * ar + lb_im * ai) / den
    coef_im = (lb_im * ar - nr * ai) / den
    bb_re = coef_re[..., None] * b_re - coef_im[..., None] * b_im
    bb_im = coef_re[..., None] * b_im + coef_im[..., None] * b_re
    bbre = _block_diag(bb_re.transpose(0, 2, 1)).astype(bf16)
    bbim = _block_diag(bb_im.transpose(0, 2, 1)).astype(bf16)
    ccre = _block_diag(c_re.transpose(0, 2, 1)).astype(bf16)
    ccim = _block_diag(c_im.transpose(0, 2, 1)).astype(bf16)
    return (lb_re.reshape(1, SSM_LANES), lb_im.reshape(1, SSM_LANES), bbre, bbim, ccre, ccim)


def _to_rows(x):
    b, t, d = x.shape
    return x.transpose(1, 0, 2).reshape(t * b, d)


def _from_rows(x, b, t):
    return x.reshape(t, b, x.shape[-1]).transpose(1, 0, 2)


def kernel(x_prompt, x_sample, state_ret, state_ssm_re, state_ssm_im, state_lru, state_conv, w_in, ret_gn_g, ret_gn_b, ssm_a_re, ssm_a_im, ssm_log_dt, ssm_b_re, ssm_b_im, ssm_c_re, ssm_c_im, ssm_d, ssm_w_glu, lru_conv_w, lru_conv_b, lru_wa, lru_ba, lru_wx, lru_bx, lru_lambda, w_branch, w_out, ln1_g, ln1_b, moe_w_group, moe_b_group, moe_w_expert, moe_b_expert, moe_w1, moe_w3, moe_w2, ln2_g, ln2_b):
    x = jnp.concatenate([_to_rows(x_prompt), _to_rows(x_sample)], axis=0)
    cos_t, sin_t = _rope_tables()
    tabs_p = _ret_tables(RET_SUB_T)
    tabs_s = _ret_tables(DEC_SEQ)
    row = lambda v: v.reshape(1, -1)

    outs = [[] for _ in range(10)]
    for l in range(DEPTH):
        z = _inproj(x, w_in[l].astype(bf16))

        s0 = state_ret[l].reshape(DEC_BATCH, 2, 2 * RET_DK, RET_DV)
        y_ret, ret_p, ret_s = _retention(z, cos_t, sin_t, s0, tabs_p, tabs_s,
                                         row(ret_gn_g[l]), row(ret_gn_b[l]))

        sp = _ssm_params(ssm_a_re[l], ssm_a_im[l], ssm_log_dt[l], ssm_b_re[l], ssm_b_im[l],
                         ssm_c_re[l], ssm_c_im[l])
        y_ssm, re_p, im_p, re_s, im_s = _ssm(
            z, state_ssm_re[l].reshape(DEC_BATCH, SSM_LANES), state_ssm_im[l].reshape(DEC_BATCH, SSM_LANES),
            *sp, row(ssm_d[l]), ssm_w_glu[l].astype(bf16))

        conv0 = state_conv[l].transpose(1, 0, 2).reshape(LRU_HIST_S, BRANCH_W)
        y_lru, lru_p, conv_p, lru_s, conv_s = _lru(
            z, state_lru[l], conv0, lru_conv_w[l], row(lru_conv_b[l]),
            _block_diag(lru_wa[l]).astype(bf16), row(lru_ba[l]),
            _block_diag(lru_wx[l]).astype(bf16), row(lru_bx[l]), row(lru_lambda[l]))

        wr = jnp.zeros((D_MODEL, ROUTE_LANES), f32)
        wr = wr.at[:, 0:MOE_GROUPS].set(moe_w_group[l]).at[:, MOE_GROUPS:MOE_GROUPS + MOE_EXPERTS].set(moe_w_expert[l])
        br = jnp.zeros((1, ROUTE_LANES), f32)
        br = br.at[0, 0:MOE_GROUPS].set(moe_b_group[l]).at[0, MOE_GROUPS:MOE_GROUPS + MOE_EXPERTS].set(moe_b_expert[l])
        x1, route, cnt = _merge(y_ret, y_ssm, y_lru, z, x, w_branch[l].astype(bf16), w_out[l].astype(bf16),
                                row(ln1_g[l]), row(ln1_b[l]), wr.astype(bf16), br)

        pos, plan = _dispatch_plan(route, cnt)
        xs = _dispatch(pos, x1)
        ys = _moe(plan, xs, moe_w1[l], moe_w3[l], moe_w2[l])
        x = _combine(pos, ys, x1, route, row(ln2_g[l]), row(ln2_b[l]))

        outs[0].append(ret_p.reshape(BATCH, RET_HEADS, RET_DK, RET_DV))
        outs[1].append(re_p.reshape(BATCH, SSM_GROUPS, SSM_STATE))
        outs[2].append(im_p.reshape(BATCH, SSM_GROUPS, SSM_STATE))
        outs[3].append(lru_p)
        outs[4].append(conv_p.reshape(CONV_W - 1, BATCH, BRANCH_W).transpose(1, 0, 2))
        outs[5].append(ret_s.reshape(DEC_BATCH, RET_HEADS, RET_DK, RET_DV))
        outs[6].append(re_s.reshape(DEC_BATCH, SSM_GROUPS, SSM_STATE))
        outs[7].append(im_s.reshape(DEC_BATCH, SSM_GROUPS, SSM_STATE))
        outs[8].append(lru_s)
        outs[9].append(conv_s.reshape(CONV_W - 1, DEC_BATCH, BRANCH_W).transpose(1, 0, 2))

    y_prompt = _from_rows(x[:N_P], BATCH, SEQ)
    y_sample = _from_rows(x[N_P:], DEC_BATCH, DEC_SEQ)
    return (y_prompt, y_sample) + tuple(jnp.stack(o) for o in outs)
```

```python
import functools

import jax
import jax.numpy as jnp
import numpy as np
from jax import lax
from jax.experimental import pallas as pl
from jax.experimental.pallas import tpu as pltpu

f32 = jnp.float32
bf16 = jnp.bfloat16

D_MODEL = 1024
BATCH = 8
SEQ = 2048
DEPTH = 2
DEC_BATCH = 128
DEC_SEQ = 8
PAST_LEN = 16384
BRANCH_W = 512
N_BRANCH = 3
RET_HEADS = 4
RET_DK = 64
RET_DV = 128
ROPE_BASE = 10000.0
SSM_GROUP = 16
SSM_GROUPS = 32
SSM_STATE = 64
SSM_LANES = SSM_GROUPS * SSM_STATE
LRU_BLOCKS = 8
LRU_BW = 64
CONV_W = 4
LRU_C = 8.0
MOE_GROUPS = 4
MOE_PER_GROUP = 8
MOE_EXPERTS = 32
MOE_TOPK = 2
MOE_HIDDEN = 512
DN_ALPHA = (2.0 * DEPTH) ** 0.25
LN_EPS = 1e-5
D_IN = 6144

V7X_SUBLANES = 8
V7X_LANES = 128
V7X_VMEM_LIMIT = 56 * 1024 * 1024

N_P = BATCH * SEQ
N_S = DEC_BATCH * DEC_SEQ
N_TOK = N_P + N_S
ROW_TILE = 1024
P_TILES = N_P // ROW_TILE
N_TILES = N_TOK // ROW_TILE
P_TC = ROW_TILE // BATCH
COL_TILE = 512
RET_SUB_T = 32
RET_SUB_R = RET_SUB_T * BATCH
RET_SUBS = ROW_TILE // RET_SUB_R
S_BLOCKS = DEC_BATCH // BATCH
S_BLOCK_R = DEC_SEQ * BATCH
MERGE_TILE = 512
MOE_TM = 256
N_PAIRS = N_TOK * MOE_TOPK
MOE_MAX_ITEMS = N_PAIRS // MOE_TM + MOE_EXPERTS - 1
DSP_TILE = 512
CMB_TILE = 256
CMB_P_STEPS = N_P // CMB_TILE
ROUTE_LANES = 128


def _cparams(sem):
    return pltpu.CompilerParams(dimension_semantics=sem, vmem_limit_bytes=V7X_VMEM_LIMIT)


def _dot(a, b):
    return jnp.dot(a, b, preferred_element_type=f32)


def _dot_nt(a, b):
    return lax.dot_general(a, b, (((1,), (1,)), ((), ())), preferred_element_type=f32)


def _dot_tn(a, b):
    return lax.dot_general(a, b, (((0,), (0,)), ((), ())), preferred_element_type=f32)


def _layer_norm_rows(x, g, b):
    mu = jnp.mean(x, -1, keepdims=True)
    xc = x - mu
    var = jnp.mean(xc * xc, -1, keepdims=True)
    return xc * lax.rsqrt(var + LN_EPS) * g + b


def _inproj_kernel(x_ref, w_ref, o_ref, xb_ref):
    @pl.when(pl.program_id(1) == 0)
    def _():
        xb_ref[...] = x_ref[...].astype(bf16)

    o_ref[...] = _dot(xb_ref[...], w_ref[...])


def _inproj(x, w_bf):
    return pl.pallas_call(
        _inproj_kernel,
        grid=(N_TILES, D_IN // COL_TILE),
        in_specs=[pl.BlockSpec((ROW_TILE, D_MODEL), lambda i, j: (i, 0)),
                  pl.BlockSpec((D_MODEL, COL_TILE), lambda i, j: (0, j))],
        out_specs=pl.BlockSpec((ROW_TILE, COL_TILE), lambda i, j: (i, j)),
        out_shape=jax.ShapeDtypeStruct((N_TOK, D_IN), f32),
        scratch_shapes=[pltpu.VMEM((ROW_TILE, D_MODEL), bf16)],
        compiler_params=_cparams(("arbitrary", "arbitrary")),
        name="inproj",
    )(x, w_bf)


def _ret_block(q, k, v, g, cosb, sinb, mask_ref, qdec_ref, kdec_ref, cdec_ref, scat_ref, gng, gnb):
    rows = q.shape[0]
    lane_qk = lax.broadcasted_iota(jnp.int32, (rows, 2 * V7X_LANES), 1)
    first_half = (lane_qk & (RET_DK - 1)) < (RET_DK // 2)

    def rope(x):
        partner = jnp.where(first_half, pltpu.roll(x, 2 * V7X_LANES - RET_DK // 2, 1),
                            pltpu.roll(x, RET_DK // 2, 1))
        return x * cosb + partner * sinb

    q = rope(q)
    k = rope(k) * (RET_DK ** -0.5)
    kd = k * kdec_ref[...]
    lane = lax.broadcasted_iota(jnp.int32, (rows, V7X_LANES), 1)
    row_b = lax.broadcasted_iota(jnp.int32, (rows, V7X_LANES), 0) & (BATCH - 1)
    outs = []
    for p in range(2):
        qp = q[:, p * V7X_LANES:(p + 1) * V7X_LANES]
        kp = k[:, p * V7X_LANES:(p + 1) * V7X_LANES].astype(bf16)
        kdp = kd[:, p * V7X_LANES:(p + 1) * V7X_LANES]
        s_old = scat_ref[p]
        s_bf = s_old.astype(bf16)
        s_new = s_old * jnp.concatenate([cdec_ref[p]] * BATCH, axis=1)
        for hh in range(2):
            h = 2 * p + hh
            head_lanes = (lane >= RET_DK) if hh else (lane < RET_DK)
            qh = jnp.where(head_lanes, qp, 0.0).astype(bf16)
            kdh = jnp.where(head_lanes, kdp, 0.0).astype(bf16)
            vh = v[:, h * RET_DV:(h + 1) * RET_DV]
            vh_bf = vh.astype(bf16)
            sc = _dot_nt(qh, kp) * mask_ref[h]
            o = _dot(sc.astype(bf16), vh_bf)
            cross = _dot(qh, s_bf)
            oc = jnp.zeros((rows, RET_DV), f32)
            for b in range(BATCH):
                oc = oc + jnp.where(row_b == b, cross[:, b * RET_DV:(b + 1) * RET_DV], 0.0)
            o = o + oc * qdec_ref[h]
            vcat = jnp.concatenate([jnp.where(row_b == b, vh_bf, jnp.zeros_like(vh_bf))
                                    for b in range(BATCH)], axis=1)
            s_new = s_new + _dot_tn(kdh, vcat)
            mu = jnp.mean(o, -1, keepdims=True)
            oc2 = o - mu
            var = jnp.mean(oc2 * oc2, -1, keepdims=True)
            outs.append(oc2 * lax.rsqrt(var + LN_EPS))
        scat_ref[p] = s_new
    o = jnp.concatenate(outs, axis=1) * gng + gnb
    return jax.nn.silu(g) * o


def _per_step_rows(tab_ref, t0, steps):
    return jnp.concatenate(
        [jnp.broadcast_to(tab_ref[t0 + t:t0 + t + 1, :], (BATCH, tab_ref.shape[1])) for t in range(steps)], axis=0)


def _ret_kernel(zqk_ref, zv_ref, zg_ref, cos_ref, sin_ref, cos_s_ref, sin_s_ref, s0_ref,
                mask_p_ref, qdec_p_ref, kdec_p_ref, cdec_p_ref,
                mask_s_ref, qdec_s_ref, kdec_s_ref, cdec_s_ref,
                gng_ref, gnb_ref,
                y_ref, retp_ref, rets_ref, scat_ref):
    i = pl.program_id(0)
    gng = gng_ref[...]
    gnb = gnb_ref[...]

    @pl.when(i == 0)
    def _():
        scat_ref[...] = jnp.zeros_like(scat_ref)

    @pl.when(i < P_TILES)
    def _():
        for sc in range(RET_SUBS):
            r0 = sc * RET_SUB_R
            rs = slice(r0, r0 + RET_SUB_R)
            y_ref[rs, :] = _ret_block(
                zqk_ref[rs, 0:256], zqk_ref[rs, 256:512], zv_ref[rs, :], zg_ref[rs, :],
                _per_step_rows(cos_ref, sc * RET_SUB_T, RET_SUB_T), _per_step_rows(sin_ref, sc * RET_SUB_T, RET_SUB_T),
                mask_p_ref, qdec_p_ref, kdec_p_ref, cdec_p_ref, scat_ref, gng, gnb)

    @pl.when(i == P_TILES - 1)
    def _():
        for b in range(BATCH):
            for p in range(2):
                retp_ref[b, p] = scat_ref[p, :, b * RET_DV:(b + 1) * RET_DV]

    @pl.when(i >= P_TILES)
    def _():
        bb = i - P_TILES
        for b in range(BATCH):
            for p in range(2):
                scat_ref[p, :, b * RET_DV:(b + 1) * RET_DV] = s0_ref[b, p]

        def rows_of(ref, c0, c1):
            return jnp.concatenate(
                [ref[pl.ds(pl.multiple_of(t * DEC_BATCH + bb * BATCH, BATCH), BATCH), c0:c1]
                 for t in range(DEC_SEQ)], axis=0)

        y = _ret_block(
            rows_of(zqk_ref, 0, 256), rows_of(zqk_ref, 256, 512), rows_of(zv_ref, 0, 512),
            rows_of(zg_ref, 0, 512), _per_step_rows(cos_s_ref, 0, DEC_SEQ), _per_step_rows(sin_s_ref, 0, DEC_SEQ),
            mask_s_ref, qdec_s_ref, kdec_s_ref, cdec_s_ref, scat_ref, gng, gnb)
        for t in range(DEC_SEQ):
            y_ref[pl.ds(pl.multiple_of(t * DEC_BATCH + bb * BATCH, BATCH), BATCH), :] = (
                y[t * BATCH:(t + 1) * BATCH, :])
        for b in range(BATCH):
            for p in range(2):
                rets_ref[b, p] = scat_ref[p, :, b * RET_DV:(b + 1) * RET_DV]


def _ret_tables(tc):
    rows = tc * BATCH
    log_g = jnp.log1p(-jnp.exp2(-5.0 - jnp.arange(RET_HEADS, dtype=f32)))
    t_idx = (jnp.arange(rows) // BATCH).astype(f32)
    b_idx = jnp.arange(rows) % BATCH
    rel = t_idx[:, None] - t_idx[None, :]
    same = b_idx[:, None] == b_idx[None, :]
    decay = jnp.exp(log_g[:, None, None] * jnp.maximum(rel, 0.0))
    mask = jnp.where((rel >= 0) & same, decay, 0.0)
    qdec = jnp.exp(log_g[:, None] * (t_idx[None, :] + 1.0))
    qdec = jnp.broadcast_to(qdec[:, :, None], (RET_HEADS, rows, RET_DV))
    kdec = jnp.exp(log_g[:, None] * (tc - 1.0 - t_idx[None, :]))
    kdec = jnp.broadcast_to(kdec.T[:, :, None], (rows, RET_HEADS, RET_DK)).reshape(rows, RET_HEADS * RET_DK)
    cdec = jnp.exp(log_g * tc)
    cdec = jnp.broadcast_to(cdec[:, None, None], (RET_HEADS, RET_DK, RET_DV)).reshape(2, 2 * RET_DK, RET_DV)
    return mask.astype(f32), qdec.astype(f32), kdec.astype(f32), cdec.astype(f32)


def _rope_tables():
    half = RET_DK // 2
    inv = ROPE_BASE ** (-jnp.arange(half, dtype=f32) / half)
    pos_p = jnp.arange(SEQ, dtype=f32)
    pos_s = PAST_LEN + jnp.arange(DEC_SEQ, dtype=f32)

    def tab(pos):
        ang = pos[:, None] * inv[None, :]
        cos = jnp.cos(ang)
        sin = jnp.sin(ang)
        cos_h = jnp.concatenate([cos, cos], axis=1)
        sin_h = jnp.concatenate([-sin, sin], axis=1)
        return jnp.tile(cos_h, (1, RET_HEADS)), jnp.tile(sin_h, (1, RET_HEADS))

    return tab(pos_p) + tab(pos_s)


def _retention(z, rope, s0_s, tabs_p, tabs_s, gng, gnb):
    n_steps = P_TILES + S_BLOCKS
    tile = lambda i: jnp.minimum(i, P_TILES)
    sblk = lambda i: jnp.maximum(i - P_TILES, 0)
    full = lambda a: pl.BlockSpec(a.shape, lambda i, _n=a.ndim: (0,) * _n)
    state_blk = (BATCH, 2, 2 * RET_DK, RET_DV)
    cos_p, sin_p, cos_s, sin_s = rope
    ptile = lambda i: jnp.minimum(i, P_TILES - 1)
    ins = [z, z, z, cos_p, sin_p, cos_s, sin_s, s0_s, *tabs_p, *tabs_s, gng, gnb]
    in_specs = [
        pl.BlockSpec((ROW_TILE, COL_TILE), lambda i: (tile(i), 0)),
        pl.BlockSpec((ROW_TILE, COL_TILE), lambda i: (tile(i), 1)),
        pl.BlockSpec((ROW_TILE, COL_TILE), lambda i: (tile(i), 2)),
        pl.BlockSpec((P_TC, 256), lambda i: (ptile(i), 0)),
        pl.BlockSpec((P_TC, 256), lambda i: (ptile(i), 0)),
        full(cos_s), full(sin_s),
        pl.BlockSpec(state_blk, lambda i: (sblk(i), 0, 0, 0)),
    ] + [full(a) for a in (*tabs_p, *tabs_s, gng, gnb)]
    return pl.pallas_call(
        _ret_kernel,
        grid=(n_steps,),
        in_specs=in_specs,
        out_specs=[pl.BlockSpec((ROW_TILE, BRANCH_W), lambda i: (tile(i), 0)),
                   pl.BlockSpec(state_blk, lambda i: (0, 0, 0, 0)),
                   pl.BlockSpec(state_blk, lambda i: (sblk(i), 0, 0, 0))],
        out_shape=[jax.ShapeDtypeStruct((N_TOK, BRANCH_W), f32),
                   jax.ShapeDtypeStruct((BATCH, 2, 2 * RET_DK, RET_DV), f32),
                   jax.ShapeDtypeStruct((DEC_BATCH, 2, 2 * RET_DK, RET_DV), f32)],
        scratch_shapes=[pltpu.VMEM((2, 2 * RET_DK, BATCH * RET_DV), f32)],
        compiler_params=_cparams(("arbitrary",)),
        name="retention",
    )(*ins)


SSM_LB = 512
SSM_RC = 256
SSM_KB = 2
SSM_KB_U = BRANCH_W // SSM_KB
SSM_KB_H = SSM_LANES // SSM_KB


def _ssm_scan(bre_ref, bim_ref, lre_ref, lim_ref, h_re0, h_im0, row0, nb_rows, steps, lb, unroll):
    ls = slice(lb * SSM_LB, (lb + 1) * SSM_LB)
    a_re = jnp.broadcast_to(lre_ref[:, ls], (V7X_SUBLANES, SSM_LB))
    a_im = jnp.broadcast_to(lim_ref[:, ls], (V7X_SUBLANES, SSM_LB))

    def step(t, carry):
        h_re, h_im = carry
        r = pl.multiple_of(row0 + t * nb_rows, V7X_SUBLANES)
        n_re = a_re * h_re - a_im * h_im + bre_ref[pl.ds(r, V7X_SUBLANES), ls]
        n_im = a_re * h_im + a_im * h_re + bim_ref[pl.ds(r, V7X_SUBLANES), ls]
        bre_ref[pl.ds(r, V7X_SUBLANES), ls] = n_re
        bim_ref[pl.ds(r, V7X_SUBLANES), ls] = n_im
        return n_re, n_im

    return lax.fori_loop(0, steps, step, (h_re0, h_im0), unroll=unroll)


def _ssm_kernel(zs_ref, h0re_ref, h0im_ref, lre_ref, lim_ref, bbre_ref, bbim_ref, ccre_ref, ccim_ref,
                d_ref, wglu_ref,
                y_ref, pre_ref, pim_ref, sre_ref, sim_ref,
                bre_ref, bim_ref, hre_ref, him_ref):
    i = pl.program_id(0)
    for rc in range(ROW_TILE // SSM_RC):
        rs = slice(rc * SSM_RC, (rc + 1) * SSM_RC)
        ub = zs_ref[rs, :].astype(bf16)
        for k in range(SSM_KB):
            uk = ub[:, k * SSM_KB_U:(k + 1) * SSM_KB_U]
            hs = slice(k * SSM_KB_H, (k + 1) * SSM_KB_H)
            bre_ref[rs, hs] = _dot(uk, bbre_ref[k])
            bim_ref[rs, hs] = _dot(uk, bbim_ref[k])

    @pl.when(i == 0)
    def _():
        hre_ref[...] = jnp.zeros_like(hre_ref)
        him_ref[...] = jnp.zeros_like(him_ref)

    @pl.when(i < P_TILES)
    def _():
        for lb in range(SSM_LANES // SSM_LB):
            ls = slice(lb * SSM_LB, (lb + 1) * SSM_LB)
            h_re, h_im = _ssm_scan(bre_ref, bim_ref, lre_ref, lim_ref, hre_ref[:, ls], him_ref[:, ls],
                                   0, BATCH, P_TC, lb, 8)
            hre_ref[:, ls] = h_re
            him_ref[:, ls] = h_im
        pre_ref[...] = hre_ref[...]
        pim_ref[...] = him_ref[...]

    @pl.when(i >= P_TILES)
    def _():
        def per_row_tile(rt, c):
            r0 = pl.multiple_of(rt * V7X_SUBLANES, V7X_SUBLANES)
            for lb in range(SSM_LANES // SSM_LB):
                ls = slice(lb * SSM_LB, (lb + 1) * SSM_LB)
                _ssm_scan(bre_ref, bim_ref, lre_ref, lim_ref,
                          h0re_ref[pl.ds(r0, V7X_SUBLANES), ls], h0im_ref[pl.ds(r0, V7X_SUBLANES), ls],
                          r0, DEC_BATCH, DEC_SEQ, lb, True)
            return c

        lax.fori_loop(0, DEC_BATCH // V7X_SUBLANES, per_row_tile, 0)
        last = (DEC_SEQ - 1) * DEC_BATCH
        sre_ref[...] = bre_ref[last:last + DEC_BATCH, :]
        sim_ref[...] = bim_ref[last:last + DEC_BATCH, :]

    for rc in range(ROW_TILE // SSM_RC):
        rs = slice(rc * SSM_RC, (rc + 1) * SSM_RC)
        ch = []
        for k in range(SSM_KB):
            hs = slice(k * SSM_KB_H, (k + 1) * SSM_KB_H)
            ch.append(_dot(bre_ref[rs, hs].astype(bf16), ccre_ref[k]) - _dot(bim_ref[rs, hs].astype(bf16), ccim_ref[k]))
        y = jnp.concatenate(ch, axis=1) + d_ref[...] * zs_ref[rs, :]
        zz = jax.nn.gelu(y)
        y_ref[rs, :] = zz * jax.nn.sigmoid(_dot(zz.astype(bf16), wglu_ref[...]))


def _ssm(z, h0re, h0im, lre, lim, bbre, bbim, ccre, ccim, dvec, wglu):
    full = lambda a: pl.BlockSpec(a.shape, lambda i, _n=a.ndim: (0,) * _n)
    consts = (h0re, h0im, lre, lim, bbre, bbim, ccre, ccim, dvec, wglu)
    return pl.pallas_call(
        _ssm_kernel,
        grid=(N_TILES,),
        in_specs=[pl.BlockSpec((ROW_TILE, COL_TILE), lambda i: (i, 3))] + [full(a) for a in consts],
        out_specs=[pl.BlockSpec((ROW_TILE, BRANCH_W), lambda i: (i, 0)),
                   pl.BlockSpec((BATCH, SSM_LANES), lambda i: (0, 0)),
                   pl.BlockSpec((BATCH, SSM_LANES), lambda i: (0, 0)),
                   pl.BlockSpec((DEC_BATCH, SSM_LANES), lambda i: (0, 0)),
                   pl.BlockSpec((DEC_BATCH, SSM_LANES), lambda i: (0, 0))],
        out_shape=[jax.ShapeDtypeStruct((N_TOK, BRANCH_W), f32),
                   jax.ShapeDtypeStruct((BATCH, SSM_LANES), f32),
                   jax.ShapeDtypeStruct((BATCH, SSM_LANES), f32),
                   jax.ShapeDtypeStruct((DEC_BATCH, SSM_LANES), f32),
                   jax.ShapeDtypeStruct((DEC_BATCH, SSM_LANES), f32)],
        scratch_shapes=[pltpu.VMEM((ROW_TILE, SSM_LANES), f32), pltpu.VMEM((ROW_TILE, SSM_LANES), f32),
                        pltpu.VMEM((BATCH, SSM_LANES), f32), pltpu.VMEM((BATCH, SSM_LANES), f32)],
        compiler_params=_cparams(("arbitrary",)),
        name="ssm",
    )(z, *consts)


LRU_HIST_P = (CONV_W - 1) * BATCH
LRU_HIST_S = (CONV_W - 1) * DEC_BATCH


def _lru_gates(xe_ref, nb_rows, cw_ref, cb_ref, wa_ref, ba_ref, wx_ref, bx_ref, lam_ref):
    xc = cb_ref[...] + xe_ref[0:ROW_TILE, :] * cw_ref[0:1, :]
    for j in range(1, CONV_W):
        xc = xc + xe_ref[j * nb_rows:j * nb_rows + ROW_TILE, :] * cw_ref[j:j + 1, :]
    xcb = xc.astype(bf16)
    r = jax.nn.sigmoid(_dot(xcb, wa_ref[...]) + ba_ref[...])
    ig = jax.nn.sigmoid(_dot(xcb, wx_ref[...]) + bx_ref[...])
    log_a = -LRU_C * r * jax.nn.softplus(-lam_ref[...])
    a = jnp.exp(log_a)
    b = jnp.sqrt(-jnp.tanh(log_a) * (jnp.exp(2.0 * log_a) + 1.0)) * (ig * xc)
    return a, b


def _lru_kernel(zx_ref, zg_ref, h0_ref, conv0_ref, cw_ref, cb_ref, wa_ref, ba_ref, wx_ref, bx_ref, lam_ref,
                y_ref, hp_ref, convp_ref, hs_ref, convs_ref,
                xe_ref, a_ref, b_ref, hc_ref):
    i = pl.program_id(0)
    params = (cw_ref, cb_ref, wa_ref, ba_ref, wx_ref, bx_ref, lam_ref)

    @pl.when(i == 0)
    def _():
        xe_ref[0:LRU_HIST_P, :] = jnp.zeros((LRU_HIST_P, BRANCH_W), f32)
        hc_ref[...] = jnp.zeros_like(hc_ref)

    @pl.when(i < P_TILES)
    def _():
        xe_ref[LRU_HIST_P:LRU_HIST_P + ROW_TILE, :] = zx_ref[...]
        a, b = _lru_gates(xe_ref, BATCH, *params)
        a_ref[...] = a
        b_ref[...] = b
        hist = xe_ref[ROW_TILE:ROW_TILE + LRU_HIST_P, :]
        xe_ref[0:LRU_HIST_P, :] = hist
        convp_ref[...] = hist

        def step(t, h):
            r = pl.multiple_of(t * BATCH, BATCH)
            h = a_ref[pl.ds(r, BATCH), :] * h + b_ref[pl.ds(r, BATCH), :]
            b_ref[pl.ds(r, BATCH), :] = h
            return h

        h = lax.fori_loop(0, P_TC, step, hc_ref[...], unroll=8)
        hc_ref[...] = h
        hp_ref[...] = h

    @pl.when(i >= P_TILES)
    def _():
        xe_ref[0:LRU_HIST_S, :] = conv0_ref[...]
        xe_ref[LRU_HIST_S:LRU_HIST_S + ROW_TILE, :] = zx_ref[...]
        a, b = _lru_gates(xe_ref, DEC_BATCH, *params)
        a_ref[...] = a
        b_ref[...] = b
        convs_ref[...] = xe_ref[ROW_TILE:ROW_TILE + LRU_HIST_S, :]

        def per_row_tile(rt, c):
            r0 = pl.multiple_of(rt * V7X_SUBLANES, V7X_SUBLANES)
            h = h0_ref[pl.ds(r0, V7X_SUBLANES), :]
            for t in range(DEC_SEQ):
                r = pl.multiple_of(t * DEC_BATCH + r0, V7X_SUBLANES)
                h = a_ref[pl.ds(r, V7X_SUBLANES), :] * h + b_ref[pl.ds(r, V7X_SUBLANES), :]
                b_ref[pl.ds(r, V7X_SUBLANES), :] = h
            return c

        lax.fori_loop(0, DEC_BATCH // V7X_SUBLANES, per_row_tile, 0)
        last = (DEC_SEQ - 1) * DEC_BATCH
        hs_ref[...] = b_ref[last:last + DEC_BATCH, :]

    y_ref[...] = b_ref[...] * jax.nn.gelu(zg_ref[...])


def _lru(z, h0, conv0, cw, cb, wa, ba, wx, bx, lam):
    full = lambda a: pl.BlockSpec(a.shape, lambda i, _n=a.ndim: (0,) * _n)
    consts = (h0, conv0, cw, cb, wa, ba, wx, bx, lam)
    return pl.pallas_call(
        _lru_kernel,
        grid=(N_TILES,),
        in_specs=[pl.BlockSpec((ROW_TILE, COL_TILE), lambda i: (i, 4)),
                  pl.BlockSpec((ROW_TILE, COL_TILE), lambda i: (i, 5))] + [full(a) for a in consts],
        out_specs=[pl.BlockSpec((ROW_TILE, BRANCH_W), lambda i: (i, 0)),
                   pl.BlockSpec((BATCH, BRANCH_W), lambda i: (0, 0)),
                   pl.BlockSpec((LRU_HIST_P, BRANCH_W), lambda i: (0, 0)),
                   pl.BlockSpec((DEC_BATCH, BRANCH_W), lambda i: (0, 0)),
                   pl.BlockSpec((LRU_HIST_S, BRANCH_W), lambda i: (0, 0))],
        out_shape=[jax.ShapeDtypeStruct((N_TOK, BRANCH_W), f32),
                   jax.ShapeDtypeStruct((BATCH, BRANCH_W), f32),
                   jax.ShapeDtypeStruct((LRU_HIST_P, BRANCH_W), f32),
                   jax.ShapeDtypeStruct((DEC_BATCH, BRANCH_W), f32),
                   jax.ShapeDtypeStruct((LRU_HIST_S, BRANCH_W), f32)],
        scratch_shapes=[pltpu.VMEM((ROW_TILE + LRU_HIST_S, BRANCH_W), f32),
                        pltpu.VMEM((ROW_TILE, BRANCH_W), f32), pltpu.VMEM((ROW_TILE, BRANCH_W), f32),
                        pltpu.VMEM((BATCH, BRANCH_W), f32)],
        compiler_params=_cparams(("arbitrary",)),
        name="lru",
    )(z, z, *consts)


def _merge_kernel(yr_ref, ys_ref, yl_ref, zm_ref, x_ref, wb_ref, wo_ref, g_ref, b_ref, wr_ref, br_ref,
                  x1_ref, route_ref, cnt_out_ref, cnt_ref):
    merged = jnp.zeros((MERGE_TILE, D_MODEL), f32)
    for n, y_ref in enumerate((yr_ref, ys_ref, yl_ref)):
        proj = _dot(y_ref[...].astype(bf16), wb_ref[n])
        gate = jax.nn.sigmoid(zm_ref[:, n * D_MODEL:(n + 1) * D_MODEL])
        merged = merged + gate * proj
    mix = _dot(merged.astype(bf16), wo_ref[...])
    x1 = _layer_norm_rows(DN_ALPHA * x_ref[...] + mix, g_ref[...], b_ref[...])
    x1_ref[...] = x1

    logits = _dot(x1.astype(bf16), wr_ref[...]) + br_ref[...]
    lane = lax.broadcasted_iota(jnp.int32, (MERGE_TILE, ROUTE_LANES), 1).astype(f32)
    big = jnp.float32(ROUTE_LANES)
    neg = jnp.float32(-jnp.inf)
    is_g = lane < MOE_GROUPS
    lg = jnp.where(is_g, logits, neg)
    mg = jnp.max(lg, -1, keepdims=True)
    gsel = jnp.min(jnp.where(lg == mg, lane, big), -1, keepdims=True)
    sum_g = jnp.sum(jnp.where(is_g, jnp.exp(lg - mg), 0.0), -1, keepdims=True)
    pg_sel = 1.0 / sum_g
    lo = MOE_GROUPS + gsel * MOE_PER_GROUP
    is_e = jnp.abs(lane - lo - 0.5 * (MOE_PER_GROUP - 1)) < 0.5 * MOE_PER_GROUP
    le = jnp.where(is_e, logits, neg)
    me = jnp.max(le, -1, keepdims=True)
    ex = jnp.where(is_e, jnp.exp(le - me), 0.0)
    pe = jnp.where(is_e, ex / jnp.sum(ex, -1, keepdims=True), -1.0)
    v1 = jnp.max(pe, -1, keepdims=True)
    i1 = jnp.min(jnp.where(pe == v1, lane, big), -1, keepdims=True)
    pe2 = jnp.where(lane == i1, -1.0, pe)
    v2 = jnp.max(pe2, -1, keepdims=True)
    i2 = jnp.min(jnp.where(pe2 == v2, lane, big), -1, keepdims=True)
    vsum = v1 + v2
    w1 = pg_sel * v1 / vsum
    w2 = pg_sel * v2 / vsum
    e1 = i1 - MOE_GROUPS
    e2 = i2 - MOE_GROUPS

    @pl.when(pl.program_id(0) == 0)
    def _():
        cnt_ref[...] = jnp.zeros_like(cnt_ref)

    oh1 = lane == e1
    oh2 = lane == e2
    ohs = jnp.where(oh1, 1.0, jnp.where(oh2, 1.0, 0.0))
    r_i = lax.broadcasted_iota(jnp.int32, (MERGE_TILE, MERGE_TILE), 0)
    c_i = lax.broadcasted_iota(jnp.int32, (MERGE_TILE, MERGE_TILE), 1)
    strict_lower = jnp.where(c_i < r_i, 1.0, 0.0).astype(bf16)
    before = _dot(strict_lower, ohs.astype(bf16)) + cnt_ref[0:1, :]
    rank1 = jnp.sum(jnp.where(oh1, before, 0.0), -1, keepdims=True)
    rank2 = jnp.sum(jnp.where(oh2, before, 0.0), -1, keepdims=True)
    cnt_ref[0:1, :] = cnt_ref[0:1, :] + jnp.sum(ohs, 0, keepdims=True)
    cnt_out_ref[...] = cnt_ref[...]

    route = jnp.zeros((MERGE_TILE, ROUTE_LANES), f32)
    for k, val in enumerate((e1, e2, w1, w2, rank1, rank2)):
        route = jnp.where(lane == k, val, route)
    route_ref[...] = route


def _merge(y_ret, y_ssm, y_lru, z, x, wb, wo, g, b, wr, br):
    full = lambda a: pl.BlockSpec(a.shape, lambda i, _n=a.ndim: (0,) * _n)
    row = lambda w: pl.BlockSpec((MERGE_TILE, w), lambda i: (i, 0))
    consts = (wb, wo, g, b, wr, br)
    return pl.pallas_call(
        _merge_kernel,
        grid=(N_TOK // MERGE_TILE,),
        in_specs=[row(BRANCH_W), row(BRANCH_W), row(BRANCH_W),
                  pl.BlockSpec((MERGE_TILE, N_BRANCH * D_MODEL), lambda i: (i, 1)),
                  row(D_MODEL)] + [full(a) for a in consts],
        out_specs=[row(D_MODEL), row(ROUTE_LANES),
                   pl.BlockSpec((V7X_SUBLANES, ROUTE_LANES), lambda i: (0, 0))],
        out_shape=[jax.ShapeDtypeStruct((N_TOK, D_MODEL), f32),
                   jax.ShapeDtypeStruct((N_TOK, ROUTE_LANES), f32),
                   jax.ShapeDtypeStruct((V7X_SUBLANES, ROUTE_LANES), f32)],
        scratch_shapes=[pltpu.VMEM((V7X_SUBLANES, ROUTE_LANES), f32)],
        compiler_params=_cparams(("arbitrary",)),
        name="merge",
    )(y_ret, y_ssm, y_lru, z, x, *consts)


def _dispatch_kernel(pos_ref, x1_ref, xs_hbm, sem):
    base = pl.program_id(0) * DSP_TILE

    def row_copy(r, dst_row):
        return pltpu.make_async_copy(x1_ref.at[pl.ds(r, 1), :], xs_hbm.at[pl.ds(dst_row, 1), :], sem.at[0])

    def issue(r, c):
        n = base + r
        row_copy(r, pos_ref[2 * n]).start()
        row_copy(r, pos_ref[2 * n + 1]).start()
        return c

    lax.fori_loop(0, DSP_TILE, issue, 0, unroll=8)
    for _ in range(MOE_TOPK):
        pltpu.make_async_copy(x1_ref, xs_hbm.at[pl.ds(0, DSP_TILE), :], sem.at[0]).wait()


def _dispatch(pos, x1):
    grid_spec = pltpu.PrefetchScalarGridSpec(
        num_scalar_prefetch=1,
        grid=(N_TOK // DSP_TILE,),
        in_specs=[pl.BlockSpec((DSP_TILE, D_MODEL), lambda s, pos: (s, 0))],
        out_specs=pl.BlockSpec(memory_space=pl.ANY),
        scratch_shapes=[pltpu.SemaphoreType.DMA((1,))],
    )
    return pl.pallas_call(
        _dispatch_kernel,
        grid_spec=grid_spec,
        out_shape=jax.ShapeDtypeStruct((N_PAIRS, D_MODEL), f32),
        compiler_params=_cparams(("arbitrary",)),
        name="dispatch",
    )(pos, x1)


def _moe_kernel(wt_ref, we_ref, wlo_ref, whi_ref, wfirst_ref, nw_ref,
                xs_ref, w1_ref, w3_ref, w2_ref, ys_ref, w1b, w3b, w2b):
    w = pl.program_id(0)

    @pl.when(w < nw_ref[0])
    def _():
        prev = we_ref[jnp.maximum(w - 1, 0)]

        @pl.when((w == 0) | (we_ref[w] != prev))
        def _():
            w1b[...] = w1_ref[0].astype(bf16)
            w3b[...] = w3_ref[0].astype(bf16)
            w2b[...] = w2_ref[0].astype(bf16)

        xt = xs_ref[...].astype(bf16)
        h = jax.nn.silu(_dot(xt, w1b[...])) * _dot(xt, w3b[...])
        res = _dot(h.astype(bf16), w2b[...])
        row = lax.broadcasted_iota(jnp.int32, (MOE_TM, D_MODEL), 0)
        mine = jnp.where(row >= wlo_ref[w], row, MOE_TM) < whi_ref[w]

        @pl.when(wfirst_ref[w] == 1)
        def _():
            ys_ref[...] = jnp.where(mine, res, 0.0)

        @pl.when(wfirst_ref[w] == 0)
        def _():
            ys_ref[...] = jnp.where(mine, res, ys_ref[...])


def _moe(plan, xs, w1, w3, w2):
    wspec = lambda shp: pl.BlockSpec((1,) + shp, lambda w, wt, we, *_: (we[w], 0, 0))
    grid_spec = pltpu.PrefetchScalarGridSpec(
        num_scalar_prefetch=6,
        grid=(MOE_MAX_ITEMS,),
        in_specs=[pl.BlockSpec((MOE_TM, D_MODEL), lambda w, wt, *_: (wt[w], 0)),
                  wspec((D_MODEL, MOE_HIDDEN)), wspec((D_MODEL, MOE_HIDDEN)), wspec((MOE_HIDDEN, D_MODEL))],
        out_specs=pl.BlockSpec((MOE_TM, D_MODEL), lambda w, wt, *_: (wt[w], 0)),
        scratch_shapes=[pltpu.VMEM((D_MODEL, MOE_HIDDEN), bf16), pltpu.VMEM((D_MODEL, MOE_HIDDEN), bf16),
                        pltpu.VMEM((MOE_HIDDEN, D_MODEL), bf16)],
    )
    return pl.pallas_call(
        _moe_kernel,
        grid_spec=grid_spec,
        out_shape=jax.ShapeDtypeStruct((N_PAIRS, D_MODEL), f32),
        compiler_params=_cparams(("arbitrary",)),
        name="moe",
    )(*plan, xs, w1, w3, w2)


def _combine_kernel(final, pos_ref, ys_hbm, x1_ref, route_ref, g_ref, b_ref, *rest):
    if final:
        yp_ref, ysm_ref, buf, sem, t3_ref = rest
    else:
        o_ref, buf, sem = rest
    s = pl.program_id(0)
    nsteps = pl.num_programs(0)

    def gather_copy(row, slot, k, r):
        return pltpu.make_async_copy(ys_hbm.at[pl.ds(row, 1), :], buf.at[slot, k, pl.ds(r, 1), :], sem.at[slot])

    def issue(tile, slot):
        def body(r, c):
            n = tile * CMB_TILE + r
            gather_copy(pos_ref[2 * n], slot, 0, r).start()
            gather_copy(pos_ref[2 * n + 1], slot, 1, r).start()
            return c

        lax.fori_loop(0, CMB_TILE, body, 0, unroll=8)

    @pl.when(s == 0)
    def _():
        issue(0, 0)

    @pl.when(s + 1 < nsteps)
    def _():
        issue(s + 1, (s + 1) % 2)

    slot = s % 2
    for k in range(MOE_TOPK):
        pltpu.make_async_copy(ys_hbm.at[pl.ds(0, CMB_TILE), :], buf.at[slot, k], sem.at[slot]).wait()
    route = route_ref[...]
    moe = route[:, 2:3] * buf[slot, 0] + route[:, 3:4] * buf[slot, 1]
    y = _layer_norm_rows(DN_ALPHA * x1_ref[...] + moe, g_ref[...], b_ref[...])
    if not final:
        o_ref[...] = y
        return

    @pl.when(s < CMB_P_STEPS)
    def _():
        t3_ref[...] = y.reshape(CMB_TILE // BATCH, BATCH, D_MODEL)
        for b in range(BATCH):
            yp_ref[b] = t3_ref[:, b, :]

    steps_per_tile = CMB_TILE // DEC_BATCH
    for q in range(N_S // CMB_TILE):
        @pl.when(s == CMB_P_STEPS + q)
        def _():
            for h in range(steps_per_tile):
                ysm_ref[:, q * steps_per_tile + h, :] = y[h * DEC_BATCH:(h + 1) * DEC_BATCH, :]


def _combine(pos, ys, x1, route, g, b, final):
    if final:
        out_specs = [pl.BlockSpec((BATCH, CMB_TILE // BATCH, D_MODEL),
                                  lambda s, pos: (0, jnp.minimum(s, CMB_P_STEPS - 1), 0)),
                     pl.BlockSpec((DEC_BATCH, DEC_SEQ, D_MODEL), lambda s, pos: (0, 0, 0))]
        out_shape = [jax.ShapeDtypeStruct((BATCH, SEQ, D_MODEL), f32),
                     jax.ShapeDtypeStruct((DEC_BATCH, DEC_SEQ, D_MODEL), f32)]
        extra = [pltpu.VMEM((CMB_TILE // BATCH, BATCH, D_MODEL), f32)]
    else:
        out_specs = pl.BlockSpec((CMB_TILE, D_MODEL), lambda s, pos: (s, 0))
        out_shape = jax.ShapeDtypeStruct((N_TOK, D_MODEL), f32)
        extra = []
    grid_spec = pltpu.PrefetchScalarGridSpec(
        num_scalar_prefetch=1,
        grid=(N_TOK // CMB_TILE,),
        in_specs=[pl.BlockSpec(memory_space=pl.ANY),
                  pl.BlockSpec((CMB_TILE, D_MODEL), lambda s, pos: (s, 0)),
                  pl.BlockSpec((CMB_TILE, ROUTE_LANES), lambda s, pos: (s, 0)),
                  pl.BlockSpec((1, D_MODEL), lambda s, pos: (0, 0)),
                  pl.BlockSpec((1, D_MODEL), lambda s, pos: (0, 0))],
        out_specs=out_specs,
        scratch_shapes=[pltpu.VMEM((2, 2, CMB_TILE, D_MODEL), f32), pltpu.SemaphoreType.DMA((2,))] + extra,
    )
    return pl.pallas_call(
        functools.partial(_combine_kernel, final),
        grid_spec=grid_spec,
        out_shape=out_shape,
        compiler_params=_cparams(("arbitrary",)),
        name="combine_out" if final else "combine",
    )(pos, ys, x1, route, g, b)


def _to_rows_kernel(xp_ref, xs_ref, o_ref, t3_ref):
    i = pl.program_id(0)

    @pl.when(i < P_TILES)
    def _():
        for b in range(BATCH):
            t3_ref[:, b, :] = xp_ref[b]
        o_ref[...] = t3_ref[...].reshape(ROW_TILE, D_MODEL)

    @pl.when(i >= P_TILES)
    def _():
        for t in range(DEC_SEQ):
            o_ref[t * DEC_BATCH:(t + 1) * DEC_BATCH, :] = xs_ref[:, t, :]


def _to_rows(x_prompt, x_sample):
    return pl.pallas_call(
        _to_rows_kernel,
        grid=(N_TILES,),
        in_specs=[pl.BlockSpec((BATCH, P_TC, D_MODEL), lambda i: (0, jnp.minimum(i, P_TILES - 1), 0)),
                  pl.BlockSpec((DEC_BATCH, DEC_SEQ, D_MODEL), lambda i: (0, 0, 0))],
        out_specs=pl.BlockSpec((ROW_TILE, D_MODEL), lambda i: (i, 0)),
        out_shape=jax.ShapeDtypeStruct((N_TOK, D_MODEL), f32),
        scratch_shapes=[pltpu.VMEM((P_TC, BATCH, D_MODEL), f32)],
        compiler_params=_cparams(("arbitrary",)),
        name="to_rows",
    )(x_prompt, x_sample)


def _lookup(table, idx):
    ar = jnp.arange(MOE_EXPERTS, dtype=jnp.int32)
    return jnp.sum(jnp.where(idx[..., None] == ar, table, 0), axis=-1)


def _dispatch_plan(route, cnt):
    i32 = jnp.int32
    e = route[:, 0:2].astype(i32)
    rank = route[:, 4:6].astype(i32)
    counts = cnt[0, :MOE_EXPERTS].astype(i32)
    ends = jnp.cumsum(counts)
    starts = ends - counts
    pos = (_lookup(starts, e) + rank).reshape(N_PAIRS)

    first_tile = starts // MOE_TM
    last_tile = (ends - 1) // MOE_TM
    ntiles = jnp.where(counts > 0, last_tile - first_tile + 1, 0)
    item_end = jnp.cumsum(ntiles)
    n_items = item_end[-1]
    w = jnp.minimum(jnp.arange(MOE_MAX_ITEMS, dtype=i32), n_items - 1)
    we = jnp.sum((item_end[None, :] <= w[:, None]).astype(i32), axis=-1)
    wt = _lookup(first_tile, we) + w - _lookup(item_end - ntiles, we)
    wlo = jnp.maximum(_lookup(starts, we) - wt * MOE_TM, 0)
    whi = jnp.minimum(_lookup(ends, we) - wt * MOE_TM, MOE_TM)
    wfirst = jnp.concatenate([jnp.ones((1,), i32), (wt[1:] != wt[:-1]).astype(i32)])
    return pos, (wt, we, wlo, whi, wfirst, n_items.reshape(1))


def _block_diag(w):
    n, a, b = w.shape
    eye = jnp.eye(n, dtype=w.dtype)
    return (w[:, :, None, :] * eye[:, None, :, None]).reshape(n * a, n * b)


def _ssm_params(a_re, a_im, log_dt, b_re, b_im, c_re, c_im):
    ar, ai = a_re, a_im
    dt = jnp.exp(log_dt)[:, None]
    mag = jnp.exp(ar * dt)
    lb_re = mag * jnp.cos(ai * dt)
    lb_im = mag * jnp.sin(ai * dt)
    den = ar * ar + ai * ai
    nr = lb_re - 1.0
    coef_re = (nr * ar + lb_im * ai) / den
    coef_im = (lb_im * ar - nr * ai) / den
    bb_re = coef_re[..., None] * b_re - coef_im[..., None] * b_im
    bb_im = coef_re[..., None] * b_im + coef_im[..., None] * b_re
    gk = SSM_GROUPS // SSM_KB

    def diag_blocks(w):
        return jnp.stack([_block_diag(w[k * gk:(k + 1) * gk]) for k in range(SSM_KB)]).astype(bf16)

    bbre = diag_blocks(bb_re.transpose(0, 2, 1))
    bbim = diag_blocks(bb_im.transpose(0, 2, 1))
    ccre = diag_blocks(c_re.transpose(0, 2, 1))
    ccim = diag_blocks(c_im.transpose(0, 2, 1))
    return (lb_re.reshape(1, SSM_LANES), lb_im.reshape(1, SSM_LANES), bbre, bbim, ccre, ccim)


def kernel(x_prompt, x_sample, state_ret, state_ssm_re, state_ssm_im, state_lru, state_conv, w_in, ret_gn_g, ret_gn_b, ssm_a_re, ssm_a_im, ssm_log_dt, ssm_b_re, ssm_b_im, ssm_c_re, ssm_c_im, ssm_d, ssm_w_glu, lru_conv_w, lru_conv_b, lru_wa, lru_ba, lru_wx, lru_bx, lru_lambda, w_branch, w_out, ln1_g, ln1_b, moe_w_group, moe_b_group, moe_w_expert, moe_b_expert, moe_w1, moe_w3, moe_w2, ln2_g, ln2_b):
    x = _to_rows(x_prompt, x_sample)
    rope = _rope_tables()
    tabs_p = _ret_tables(RET_SUB_T)
    tabs_s = _ret_tables(DEC_SEQ)
    row = lambda v: v.reshape(1, -1)

    outs = [[] for _ in range(10)]
    for l in range(DEPTH):
        z = _inproj(x, w_in[l].astype(bf16))

        s0 = state_ret[l].reshape(DEC_BATCH, 2, 2 * RET_DK, RET_DV)
        y_ret, ret_p, ret_s = _retention(z, rope, s0, tabs_p, tabs_s,
                                         row(ret_gn_g[l]), row(ret_gn_b[l]))

        sp = _ssm_params(ssm_a_re[l], ssm_a_im[l], ssm_log_dt[l], ssm_b_re[l], ssm_b_im[l],
                         ssm_c_re[l], ssm_c_im[l])
        y_ssm, re_p, im_p, re_s, im_s = _ssm(
            z, state_ssm_re[l].reshape(DEC_BATCH, SSM_LANES), state_ssm_im[l].reshape(DEC_BATCH, SSM_LANES),
            *sp, row(ssm_d[l]), ssm_w_glu[l].astype(bf16))

        conv0 = state_conv[l].transpose(1, 0, 2).reshape(LRU_HIST_S, BRANCH_W)
        y_lru, lru_p, conv_p, lru_s, conv_s = _lru(
            z, state_lru[l], conv0, lru_conv_w[l], row(lru_conv_b[l]),
            _block_diag(lru_wa[l]).astype(bf16), row(lru_ba[l]),
            _block_diag(lru_wx[l]).astype(bf16), row(lru_bx[l]), row(lru_lambda[l]))

        wr = jnp.zeros((D_MODEL, ROUTE_LANES), f32)
        wr = wr.at[:, 0:MOE_GROUPS].set(moe_w_group[l]).at[:, MOE_GROUPS:MOE_GROUPS + MOE_EXPERTS].set(moe_w_expert[l])
        br = jnp.zeros((1, ROUTE_LANES), f32)
        br = br.at[0, 0:MOE_GROUPS].set(moe_b_group[l]).at[0, MOE_GROUPS:MOE_GROUPS + MOE_EXPERTS].set(moe_b_expert[l])
        x1, route, cnt = _merge(y_ret, y_ssm, y_lru, z, x, w_branch[l].astype(bf16), w_out[l].astype(bf16),
                                row(ln1_g[l]), row(ln1_b[l]), wr.astype(bf16), br)

        pos, plan = _dispatch_plan(route, cnt)
        xs = _dispatch(pos, x1)
        ys = _moe(plan, xs, moe_w1[l], moe_w3[l], moe_w2[l])
        x = _combine(pos, ys, x1, route, row(ln2_g[l]), row(ln2_b[l]), final=(l == DEPTH - 1))

        outs[0].append(ret_p.reshape(BATCH, RET_HEADS, RET_DK, RET_DV))
        outs[1].append(re_p.reshape(BATCH, SSM_GROUPS, SSM_STATE))
        outs[2].append(im_p.reshape(BATCH, SSM_GROUPS, SSM_STATE))
        outs[3].append(lru_p)
        outs[4].append(conv_p.reshape(CONV_W - 1, BATCH, BRANCH_W).transpose(1, 0, 2))
        outs[5].append(ret_s.reshape(DEC_BATCH, RET_HEADS, RET_DK, RET_DV))
        outs[6].append(re_s.reshape(DEC_BATCH, SSM_GROUPS, SSM_STATE))
        outs[7].append(im_s.reshape(DEC_BATCH, SSM_GROUPS, SSM_STATE))
        outs[8].append(lru_s)
        outs[9].append(conv_s.reshape(CONV_W - 1, DEC_BATCH, BRANCH_W).transpose(1, 0, 2))

    y_prompt, y_sample = x
    return (y_prompt, y_sample) + tuple(jnp.stack(o) for o in outs)
```

```python
import functools

import jax
import jax.numpy as jnp
import numpy as np
from jax import lax
from jax.experimental import pallas as pl
from jax.experimental.pallas import tpu as pltpu

f32 = jnp.float32
bf16 = jnp.bfloat16

D_MODEL = 1024
BATCH = 8
SEQ = 2048
DEPTH = 2
DEC_BATCH = 128
DEC_SEQ = 8
PAST_LEN = 16384
BRANCH_W = 512
N_BRANCH = 3
RET_HEADS = 4
RET_DK = 64
RET_DV = 128
ROPE_BASE = 10000.0
SSM_GROUP = 16
SSM_GROUPS = 32
SSM_STATE = 64
SSM_LANES = SSM_GROUPS * SSM_STATE
LRU_BLOCKS = 8
LRU_BW = 64
CONV_W = 4
LRU_C = 8.0
MOE_GROUPS = 4
MOE_PER_GROUP = 8
MOE_EXPERTS = 32
MOE_TOPK = 2
MOE_HIDDEN = 512
DN_ALPHA = (2.0 * DEPTH) ** 0.25
LN_EPS = 1e-5
D_IN = 6144

V7X_SUBLANES = 8
V7X_LANES = 128
V7X_VMEM_LIMIT = 56 * 1024 * 1024

N_P = BATCH * SEQ
N_S = DEC_BATCH * DEC_SEQ
N_TOK = N_P + N_S
ROW_TILE = 1024
P_TILES = N_P // ROW_TILE
N_TILES = N_TOK // ROW_TILE
P_TC = ROW_TILE // BATCH
RET_SUB_T = 32
RET_SUB_R = RET_SUB_T * BATCH
RET_SUBS = ROW_TILE // RET_SUB_R
S_BLOCKS = DEC_BATCH // BATCH
S_BLOCK_R = DEC_SEQ * BATCH
MERGE_TILE = 512
MOE_TM = 256
N_PAIRS = N_TOK * MOE_TOPK
MOE_MAX_ITEMS = N_PAIRS // MOE_TM + MOE_EXPERTS - 1
DSP_TILE = 512
CMB_TILE = 256
CMB_P_STEPS = N_P // CMB_TILE
ROUTE_LANES = 128


PROJ_RC = 256
RET_COLS, RET_BLK = 1536, 0
SSM_COLS, SSM_BLK = 512, 3
LRU_COLS, LRU_BLK = 1024, 2
GATE_COLS, GATE_BLK = 3072, 1


def _cparams(sem):
    return pltpu.CompilerParams(dimension_semantics=sem, vmem_limit_bytes=V7X_VMEM_LIMIT)


def _const_spec(block_shape, index):
    return pl.BlockSpec(block_shape, lambda *_: index, pipeline_mode=pl.Buffered(1))


def _dot(a, b):
    return jnp.dot(a, b, preferred_element_type=f32)


def _dot_nt(a, b):
    return lax.dot_general(a, b, (((1,), (1,)), ((), ())), preferred_element_type=f32)


def _dot_tn(a, b):
    return lax.dot_general(a, b, (((0,), (0,)), ((), ())), preferred_element_type=f32)


def _layer_norm_rows(x, g, b):
    mu = jnp.mean(x, -1, keepdims=True)
    xc = x - mu
    var = jnp.mean(xc * xc, -1, keepdims=True)
    return xc * lax.rsqrt(var + LN_EPS) * g + b


def _project_rows(xb_ref, w_ref, z_ref):
    for rc in range(ROW_TILE // PROJ_RC):
        rs = slice(rc * PROJ_RC, (rc + 1) * PROJ_RC)
        z_ref[rs, :] = _dot(xb_ref[rs, :], w_ref[...])


def _ret_block(q, k, v, g, cosb, sinb, mask_ref, qdec_ref, kdec_ref, cdec_ref, scat_ref, gng, gnb):
    rows = q.shape[0]
    lane_qk = lax.broadcasted_iota(jnp.int32, (rows, 2 * V7X_LANES), 1)
    first_half = (lane_qk & (RET_DK - 1)) < (RET_DK // 2)

    def rope(x):
        partner = jnp.where(first_half, pltpu.roll(x, 2 * V7X_LANES - RET_DK // 2, 1),
                            pltpu.roll(x, RET_DK // 2, 1))
        return x * cosb + partner * sinb

    q = rope(q)
    k = rope(k) * (RET_DK ** -0.5)
    kd = k * kdec_ref[...]
    lane = lax.broadcasted_iota(jnp.int32, (rows, V7X_LANES), 1)
    row_b = lax.broadcasted_iota(jnp.int32, (rows, V7X_LANES), 0) & (BATCH - 1)
    outs = []
    for p in range(2):
        qp = q[:, p * V7X_LANES:(p + 1) * V7X_LANES]
        kp = k[:, p * V7X_LANES:(p + 1) * V7X_LANES].astype(bf16)
        kdp = kd[:, p * V7X_LANES:(p + 1) * V7X_LANES]
        s_old = scat_ref[p]
        s_bf = s_old.astype(bf16)
        s_new = s_old * jnp.concatenate([cdec_ref[p]] * BATCH, axis=1)
        for hh in range(2):
            h = 2 * p + hh
            head_lanes = (lane >= RET_DK) if hh else (lane < RET_DK)
            qh = jnp.where(head_lanes, qp, 0.0).astype(bf16)
            kdh = jnp.where(head_lanes, kdp, 0.0).astype(bf16)
            vh = v[:, h * RET_DV:(h + 1) * RET_DV]
            vh_bf = vh.astype(bf16)
            sc = _dot_nt(qh, kp) * mask_ref[h]
            o = _dot(sc.astype(bf16), vh_bf)
            cross = _dot(qh, s_bf)
            oc = jnp.zeros((rows, RET_DV), f32)
            for b in range(BATCH):
                oc = oc + jnp.where(row_b == b, cross[:, b * RET_DV:(b + 1) * RET_DV], 0.0)
            o = o + oc * qdec_ref[h]
            vcat = jnp.concatenate([jnp.where(row_b == b, vh_bf, jnp.zeros_like(vh_bf))
                                    for b in range(BATCH)], axis=1)
            s_new = s_new + _dot_tn(kdh, vcat)
            mu = jnp.mean(o, -1, keepdims=True)
            oc2 = o - mu
            var = jnp.mean(oc2 * oc2, -1, keepdims=True)
            outs.append(oc2 * lax.rsqrt(var + LN_EPS))
        scat_ref[p] = s_new
    o = jnp.concatenate(outs, axis=1) * gng + gnb
    return jax.nn.silu(g) * o


def _per_step_rows(tab_ref, t0, steps):
    return jnp.concatenate(
        [jnp.broadcast_to(tab_ref[t0 + t:t0 + t + 1, :], (BATCH, tab_ref.shape[1])) for t in range(steps)], axis=0)


def _ret_kernel(xb_ref, w_ref, cos_ref, sin_ref, cos_s_ref, sin_s_ref, s0_ref,
                mask_p_ref, qdec_p_ref, kdec_p_ref, cdec_p_ref,
                mask_s_ref, qdec_s_ref, kdec_s_ref, cdec_s_ref,
                gng_ref, gnb_ref,
                y_ref, retp_ref, rets_ref, scat_ref, z_ref):
    i = pl.program_id(0)
    gng = gng_ref[...]
    gnb = gnb_ref[...]
    q_cols, k_cols = slice(0, 256), slice(256, 512)
    v_cols, g_cols = slice(512, 1024), slice(1024, 1536)

    @pl.when(i == 0)
    def _():
        scat_ref[...] = jnp.zeros_like(scat_ref)

    @pl.when(i <= P_TILES)
    def _():
        _project_rows(xb_ref, w_ref, z_ref)

    @pl.when(i < P_TILES)
    def _():
        for sc in range(RET_SUBS):
            r0 = sc * RET_SUB_R
            rs = slice(r0, r0 + RET_SUB_R)
            y_ref[rs, :] = _ret_block(
                z_ref[rs, q_cols], z_ref[rs, k_cols], z_ref[rs, v_cols], z_ref[rs, g_cols],
                _per_step_rows(cos_ref, sc * RET_SUB_T, RET_SUB_T), _per_step_rows(sin_ref, sc * RET_SUB_T, RET_SUB_T),
                mask_p_ref, qdec_p_ref, kdec_p_ref, cdec_p_ref, scat_ref, gng, gnb)

    @pl.when(i == P_TILES - 1)
    def _():
        for b in range(BATCH):
            for p in range(2):
                retp_ref[b, p] = scat_ref[p, :, b * RET_DV:(b + 1) * RET_DV]

    @pl.when(i >= P_TILES)
    def _():
        bb = i - P_TILES
        for b in range(BATCH):
            for p in range(2):
                scat_ref[p, :, b * RET_DV:(b + 1) * RET_DV] = s0_ref[b, p]

        def rows_of(cols):
            return jnp.concatenate(
                [z_ref[pl.ds(pl.multiple_of(t * DEC_BATCH + bb * BATCH, BATCH), BATCH), cols]
                 for t in range(DEC_SEQ)], axis=0)

        y = _ret_block(
            rows_of(q_cols), rows_of(k_cols), rows_of(v_cols),
            rows_of(g_cols), _per_step_rows(cos_s_ref, 0, DEC_SEQ), _per_step_rows(sin_s_ref, 0, DEC_SEQ),
            mask_s_ref, qdec_s_ref, kdec_s_ref, cdec_s_ref, scat_ref, gng, gnb)
        for t in range(DEC_SEQ):
            y_ref[pl.ds(pl.multiple_of(t * DEC_BATCH + bb * BATCH, BATCH), BATCH), :] = (
                y[t * BATCH:(t + 1) * BATCH, :])
        for b in range(BATCH):
            for p in range(2):
                rets_ref[b, p] = scat_ref[p, :, b * RET_DV:(b + 1) * RET_DV]


def _ret_tables(tc):
    rows = tc * BATCH
    log_g = jnp.log1p(-jnp.exp2(-5.0 - jnp.arange(RET_HEADS, dtype=f32)))
    t_idx = (jnp.arange(rows) // BATCH).astype(f32)
    b_idx = jnp.arange(rows) % BATCH
    rel = t_idx[:, None] - t_idx[None, :]
    same = b_idx[:, None] == b_idx[None, :]
    decay = jnp.exp(log_g[:, None, None] * jnp.maximum(rel, 0.0))
    mask = jnp.where((rel >= 0) & same, decay, 0.0)
    qdec = jnp.exp(log_g[:, None] * (t_idx[None, :] + 1.0))
    qdec = jnp.broadcast_to(qdec[:, :, None], (RET_HEADS, rows, RET_DV))
    kdec = jnp.exp(log_g[:, None] * (tc - 1.0 - t_idx[None, :]))
    kdec = jnp.broadcast_to(kdec.T[:, :, None], (rows, RET_HEADS, RET_DK)).reshape(rows, RET_HEADS * RET_DK)
    cdec = jnp.exp(log_g * tc)
    cdec = jnp.broadcast_to(cdec[:, None, None], (RET_HEADS, RET_DK, RET_DV)).reshape(2, 2 * RET_DK, RET_DV)
    return mask.astype(f32), qdec.astype(f32), kdec.astype(f32), cdec.astype(f32)


def _rope_tables():
    half = RET_DK // 2
    inv = ROPE_BASE ** (-jnp.arange(half, dtype=f32) / half)
    pos_p = jnp.arange(SEQ, dtype=f32)
    pos_s = PAST_LEN + jnp.arange(DEC_SEQ, dtype=f32)

    def tab(pos):
        ang = pos[:, None] * inv[None, :]
        cos = jnp.cos(ang)
        sin = jnp.sin(ang)
        cos_h = jnp.concatenate([cos, cos], axis=1)
        sin_h = jnp.concatenate([-sin, sin], axis=1)
        return jnp.tile(cos_h, (1, RET_HEADS)), jnp.tile(sin_h, (1, RET_HEADS))

    return tab(pos_p) + tab(pos_s)


def _retention(xb, w_bf, rope, s0_s, tabs_p, tabs_s, gng, gnb):
    n_steps = P_TILES + S_BLOCKS
    tile = lambda i: jnp.minimum(i, P_TILES)
    sblk = lambda i: jnp.maximum(i - P_TILES, 0)
    full = lambda a: pl.BlockSpec(a.shape, lambda i, _n=a.ndim: (0,) * _n)
    state_blk = (BATCH, 2, 2 * RET_DK, RET_DV)
    cos_p, sin_p, cos_s, sin_s = rope
    ptile = lambda i: jnp.minimum(i, P_TILES - 1)
    ins = [xb, w_bf, cos_p, sin_p, cos_s, sin_s, s0_s, *tabs_p, *tabs_s, gng, gnb]
    in_specs = [
        pl.BlockSpec((ROW_TILE, D_MODEL), lambda i: (tile(i), 0)),
        _const_spec((D_MODEL, RET_COLS), (0, RET_BLK)),
        pl.BlockSpec((P_TC, 256), lambda i: (ptile(i), 0)),
        pl.BlockSpec((P_TC, 256), lambda i: (ptile(i), 0)),
        full(cos_s), full(sin_s),
        pl.BlockSpec(state_blk, lambda i: (sblk(i), 0, 0, 0)),
    ] + [full(a) for a in (*tabs_p, *tabs_s, gng, gnb)]
    return pl.pallas_call(
        _ret_kernel,
        grid=(n_steps,),
        in_specs=in_specs,
        out_specs=[pl.BlockSpec((ROW_TILE, BRANCH_W), lambda i: (tile(i), 0)),
                   pl.BlockSpec(state_blk, lambda i: (0, 0, 0, 0)),
                   pl.BlockSpec(state_blk, lambda i: (sblk(i), 0, 0, 0))],
        out_shape=[jax.ShapeDtypeStruct((N_TOK, BRANCH_W), f32),
                   jax.ShapeDtypeStruct((BATCH, 2, 2 * RET_DK, RET_DV), f32),
                   jax.ShapeDtypeStruct((DEC_BATCH, 2, 2 * RET_DK, RET_DV), f32)],
        scratch_shapes=[pltpu.VMEM((2, 2 * RET_DK, BATCH * RET_DV), f32),
                        pltpu.VMEM((ROW_TILE, RET_COLS), f32)],
        compiler_params=_cparams(("arbitrary",)),
        name="retention",
    )(*ins)


SSM_LB = 512
SSM_RC = 256
SSM_KB = 2
SSM_KB_U = BRANCH_W // SSM_KB
SSM_KB_H = SSM_LANES // SSM_KB


def _ssm_scan(bre_ref, bim_ref, lre_ref, lim_ref, h_re0, h_im0, row0, nb_rows, steps, lb, unroll):
    ls = slice(lb * SSM_LB, (lb + 1) * SSM_LB)
    a_re = jnp.broadcast_to(lre_ref[:, ls], (V7X_SUBLANES, SSM_LB))
    a_im = jnp.broadcast_to(lim_ref[:, ls], (V7X_SUBLANES, SSM_LB))

    def step(t, carry):
        h_re, h_im = carry
        r = pl.multiple_of(row0 + t * nb_rows, V7X_SUBLANES)
        n_re = a_re * h_re - a_im * h_im + bre_ref[pl.ds(r, V7X_SUBLANES), ls]
        n_im = a_re * h_im + a_im * h_re + bim_ref[pl.ds(r, V7X_SUBLANES), ls]
        bre_ref[pl.ds(r, V7X_SUBLANES), ls] = n_re
        bim_ref[pl.ds(r, V7X_SUBLANES), ls] = n_im
        return n_re, n_im

    return lax.fori_loop(0, steps, step, (h_re0, h_im0), unroll=unroll)


def _ssm_kernel(xb_ref, w_ref, h0re_ref, h0im_ref, lre_ref, lim_ref, bbre_ref, bbim_ref, ccre_ref, ccim_ref,
                d_ref, wglu_ref,
                y_ref, pre_ref, pim_ref, sre_ref, sim_ref,
                bre_ref, bim_ref, hre_ref, him_ref, zs_ref):
    i = pl.program_id(0)
    _project_rows(xb_ref, w_ref, zs_ref)
    for rc in range(ROW_TILE // SSM_RC):
        rs = slice(rc * SSM_RC, (rc + 1) * SSM_RC)
        ub = zs_ref[rs, :].astype(bf16)
        for k in range(SSM_KB):
            uk = ub[:, k * SSM_KB_U:(k + 1) * SSM_KB_U]
            hs = slice(k * SSM_KB_H, (k + 1) * SSM_KB_H)
            bre_ref[rs, hs] = _dot(uk, bbre_ref[k])
            bim_ref[rs, hs] = _dot(uk, bbim_ref[k])

    @pl.when(i == 0)
    def _():
        hre_ref[...] = jnp.zeros_like(hre_ref)
        him_ref[...] = jnp.zeros_like(him_ref)

    @pl.when(i < P_TILES)
    def _():
        for lb in range(SSM_LANES // SSM_LB):
            ls = slice(lb * SSM_LB, (lb + 1) * SSM_LB)
            h_re, h_im = _ssm_scan(bre_ref, bim_ref, lre_ref, lim_ref, hre_ref[:, ls], him_ref[:, ls],
                                   0, BATCH, P_TC, lb, 8)
            hre_ref[:, ls] = h_re
            him_ref[:, ls] = h_im
        pre_ref[...] = hre_ref[...]
        pim_ref[...] = him_ref[...]

    @pl.when(i >= P_TILES)
    def _():
        def per_row_tile(rt, c):
            r0 = pl.multiple_of(rt * V7X_SUBLANES, V7X_SUBLANES)
            for lb in range(SSM_LANES // SSM_LB):
                ls = slice(lb * SSM_LB, (lb + 1) * SSM_LB)
                _ssm_scan(bre_ref, bim_ref, lre_ref, lim_ref,
                          h0re_ref[pl.ds(r0, V7X_SUBLANES), ls], h0im_ref[pl.ds(r0, V7X_SUBLANES), ls],
                          r0, DEC_BATCH, DEC_SEQ, lb, True)
            return c

        lax.fori_loop(0, DEC_BATCH // V7X_SUBLANES, per_row_tile, 0)
        last = (DEC_SEQ - 1) * DEC_BATCH
        sre_ref[...] = bre_ref[last:last + DEC_BATCH, :]
        sim_ref[...] = bim_ref[last:last + DEC_BATCH, :]

    for rc in range(ROW_TILE // SSM_RC):
        rs = slice(rc * SSM_RC, (rc + 1) * SSM_RC)
        ch = []
        for k in range(SSM_KB):
            hs = slice(k * SSM_KB_H, (k + 1) * SSM_KB_H)
            ch.append(_dot(bre_ref[rs, hs].astype(bf16), ccre_ref[k]) - _dot(bim_ref[rs, hs].astype(bf16), ccim_ref[k]))
        y = jnp.concatenate(ch, axis=1) + d_ref[...] * zs_ref[rs, :]
        zz = jax.nn.gelu(y)
        y_ref[rs, :] = zz * jax.nn.sigmoid(_dot(zz.astype(bf16), wglu_ref[...]))


def _ssm(xb, w_bf, h0re, h0im, lre, lim, bbre, bbim, ccre, ccim, dvec, wglu):
    full = lambda a: _const_spec(a.shape, (0,) * a.ndim)
    consts = (h0re, h0im, lre, lim, bbre, bbim, ccre, ccim, dvec, wglu)
    return pl.pallas_call(
        _ssm_kernel,
        grid=(N_TILES,),
        in_specs=[pl.BlockSpec((ROW_TILE, D_MODEL), lambda i: (i, 0)),
                  _const_spec((D_MODEL, SSM_COLS), (0, SSM_BLK))] + [full(a) for a in consts],
        out_specs=[pl.BlockSpec((ROW_TILE, BRANCH_W), lambda i: (i, 0)),
                   pl.BlockSpec((BATCH, SSM_LANES), lambda i: (0, 0)),
                   pl.BlockSpec((BATCH, SSM_LANES), lambda i: (0, 0)),
                   pl.BlockSpec((DEC_BATCH, SSM_LANES), lambda i: (0, 0)),
                   pl.BlockSpec((DEC_BATCH, SSM_LANES), lambda i: (0, 0))],
        out_shape=[jax.ShapeDtypeStruct((N_TOK, BRANCH_W), f32),
                   jax.ShapeDtypeStruct((BATCH, SSM_LANES), f32),
                   jax.ShapeDtypeStruct((BATCH, SSM_LANES), f32),
                   jax.ShapeDtypeStruct((DEC_BATCH, SSM_LANES), f32),
                   jax.ShapeDtypeStruct((DEC_BATCH, SSM_LANES), f32)],
        scratch_shapes=[pltpu.VMEM((ROW_TILE, SSM_LANES), f32), pltpu.VMEM((ROW_TILE, SSM_LANES), f32),
                        pltpu.VMEM((BATCH, SSM_LANES), f32), pltpu.VMEM((BATCH, SSM_LANES), f32),
                        pltpu.VMEM((ROW_TILE, SSM_COLS), f32)],
        compiler_params=_cparams(("arbitrary",)),
        name="ssm",
    )(xb, w_bf, *consts)


LRU_HIST_P = (CONV_W - 1) * BATCH
LRU_HIST_S = (CONV_W - 1) * DEC_BATCH


def _lru_gates(xe_ref, nb_rows, cw_ref, cb_ref, wa_ref, ba_ref, wx_ref, bx_ref, lam_ref):
    xc = cb_ref[...] + xe_ref[0:ROW_TILE, :] * cw_ref[0:1, :]
    for j in range(1, CONV_W):
        xc = xc + xe_ref[j * nb_rows:j * nb_rows + ROW_TILE, :] * cw_ref[j:j + 1, :]
    xcb = xc.astype(bf16)
    r = jax.nn.sigmoid(_dot(xcb, wa_ref[...]) + ba_ref[...])
    ig = jax.nn.sigmoid(_dot(xcb, wx_ref[...]) + bx_ref[...])
    log_a = -LRU_C * r * jax.nn.softplus(-lam_ref[...])
    a = jnp.exp(log_a)
    b = jnp.sqrt(-jnp.tanh(log_a) * (jnp.exp(2.0 * log_a) + 1.0)) * (ig * xc)
    return a, b


def _lru_kernel(xb_ref, w_ref, h0_ref, conv0_ref, cw_ref, cb_ref, wa_ref, ba_ref, wx_ref, bx_ref, lam_ref,
                y_ref, hp_ref, convp_ref, hs_ref, convs_ref,
                xe_ref, a_ref, b_ref, hc_ref, z_ref):
    i = pl.program_id(0)
    params = (cw_ref, cb_ref, wa_ref, ba_ref, wx_ref, bx_ref, lam_ref)
    _project_rows(xb_ref, w_ref, z_ref)
    zx_ref = z_ref.at[:, 0:BRANCH_W]
    zg_ref = z_ref.at[:, BRANCH_W:2 * BRANCH_W]

    @pl.when(i == 0)
    def _():
        xe_ref[0:LRU_HIST_P, :] = jnp.zeros((LRU_HIST_P, BRANCH_W), f32)
        hc_ref[...] = jnp.zeros_like(hc_ref)

    @pl.when(i < P_TILES)
    def _():
        xe_ref[LRU_HIST_P:LRU_HIST_P + ROW_TILE, :] = zx_ref[...]
        a, b = _lru_gates(xe_ref, BATCH, *params)
        a_ref[...] = a
        b_ref[...] = b
        hist = xe_ref[ROW_TILE:ROW_TILE + LRU_HIST_P, :]
        xe_ref[0:LRU_HIST_P, :] = hist
        convp_ref[...] = hist

        def step(t, h):
            r = pl.multiple_of(t * BATCH, BATCH)
            h = a_ref[pl.ds(r, BATCH), :] * h + b_ref[pl.ds(r, BATCH), :]
            b_ref[pl.ds(r, BATCH), :] = h
            return h

        h = lax.fori_loop(0, P_TC, step, hc_ref[...], unroll=8)
        hc_ref[...] = h
        hp_ref[...] = h

    @pl.when(i >= P_TILES)
    def _():
        xe_ref[0:LRU_HIST_S, :] = conv0_ref[...]
        xe_ref[LRU_HIST_S:LRU_HIST_S + ROW_TILE, :] = zx_ref[...]
        a, b = _lru_gates(xe_ref, DEC_BATCH, *params)
        a_ref[...] = a
        b_ref[...] = b
        convs_ref[...] = xe_ref[ROW_TILE:ROW_TILE + LRU_HIST_S, :]

        def per_row_tile(rt, c):
            r0 = pl.multiple_of(rt * V7X_SUBLANES, V7X_SUBLANES)
            h = h0_ref[pl.ds(r0, V7X_SUBLANES), :]
            for t in range(DEC_SEQ):
                r = pl.multiple_of(t * DEC_BATCH + r0, V7X_SUBLANES)
                h = a_ref[pl.ds(r, V7X_SUBLANES), :] * h + b_ref[pl.ds(r, V7X_SUBLANES), :]
                b_ref[pl.ds(r, V7X_SUBLANES), :] = h
            return c

        lax.fori_loop(0, DEC_BATCH // V7X_SUBLANES, per_row_tile, 0)
        last = (DEC_SEQ - 1) * DEC_BATCH
        hs_ref[...] = b_ref[last:last + DEC_BATCH, :]

    y_ref[...] = b_ref[...] * jax.nn.gelu(zg_ref[...])


def _lru(xb, w_bf, h0, conv0, cw, cb, wa, ba, wx, bx, lam):
    full = lambda a: _const_spec(a.shape, (0,) * a.ndim)
    consts = (h0, conv0, cw, cb, wa, ba, wx, bx, lam)
    return pl.pallas_call(
        _lru_kernel,
        grid=(N_TILES,),
        in_specs=[pl.BlockSpec((ROW_TILE, D_MODEL), lambda i: (i, 0)),
                  _const_spec((D_MODEL, LRU_COLS), (0, LRU_BLK))] + [full(a) for a in consts],
        out_specs=[pl.BlockSpec((ROW_TILE, BRANCH_W), lambda i: (i, 0)),
                   pl.BlockSpec((BATCH, BRANCH_W), lambda i: (0, 0)),
                   pl.BlockSpec((LRU_HIST_P, BRANCH_W), lambda i: (0, 0)),
                   pl.BlockSpec((DEC_BATCH, BRANCH_W), lambda i: (0, 0)),
                   pl.BlockSpec((LRU_HIST_S, BRANCH_W), lambda i: (0, 0))],
        out_shape=[jax.ShapeDtypeStruct((N_TOK, BRANCH_W), f32),
                   jax.ShapeDtypeStruct((BATCH, BRANCH_W), f32),
                   jax.ShapeDtypeStruct((LRU_HIST_P, BRANCH_W), f32),
                   jax.ShapeDtypeStruct((DEC_BATCH, BRANCH_W), f32),
                   jax.ShapeDtypeStruct((LRU_HIST_S, BRANCH_W), f32)],
        scratch_shapes=[pltpu.VMEM((ROW_TILE + LRU_HIST_S, BRANCH_W), f32),
                        pltpu.VMEM((ROW_TILE, BRANCH_W), f32), pltpu.VMEM((ROW_TILE, BRANCH_W), f32),
                        pltpu.VMEM((BATCH, BRANCH_W), f32),
                        pltpu.VMEM((ROW_TILE, LRU_COLS), f32)],
        compiler_params=_cparams(("arbitrary",)),
        name="lru",
    )(xb, w_bf, *consts)


def _merge_kernel(yr_ref, ys_ref, yl_ref, xb_ref, wg_ref, x_ref, wb_ref, wo_ref, g_ref, b_ref, wr_ref, br_ref,
                  x1_ref, route_ref, cnt_out_ref, cnt_ref):
    merged = jnp.zeros((MERGE_TILE, D_MODEL), f32)
    xb = xb_ref[...]
    for n, y_ref in enumerate((yr_ref, ys_ref, yl_ref)):
        proj = _dot(y_ref[...].astype(bf16), wb_ref[n])
        gate = jax.nn.sigmoid(_dot(xb, wg_ref[:, n * D_MODEL:(n + 1) * D_MODEL]))
        merged = merged + gate * proj
    mix = _dot(merged.astype(bf16), wo_ref[...])
    x1 = _layer_norm_rows(DN_ALPHA * x_ref[...] + mix, g_ref[...], b_ref[...])
    x1_ref[...] = x1

    logits = _dot(x1.astype(bf16), wr_ref[...]) + br_ref[...]
    lane = lax.broadcasted_iota(jnp.int32, (MERGE_TILE, ROUTE_LANES), 1).astype(f32)
    big = jnp.float32(ROUTE_LANES)
    neg = jnp.float32(-jnp.inf)
    is_g = lane < MOE_GROUPS
    lg = jnp.where(is_g, logits, neg)
    mg = jnp.max(lg, -1, keepdims=True)
    gsel = jnp.min(jnp.where(lg == mg, lane, big), -1, keepdims=True)
    sum_g = jnp.sum(jnp.where(is_g, jnp.exp(lg - mg), 0.0), -1, keepdims=True)
    pg_sel = 1.0 / sum_g
    lo = MOE_GROUPS + gsel * MOE_PER_GROUP
    is_e = jnp.abs(lane - lo - 0.5 * (MOE_PER_GROUP - 1)) < 0.5 * MOE_PER_GROUP
    le = jnp.where(is_e, logits, neg)
    me = jnp.max(le, -1, keepdims=True)
    ex = jnp.where(is_e, jnp.exp(le - me), 0.0)
    pe = jnp.where(is_e, ex / jnp.sum(ex, -1, keepdims=True), -1.0)
    v1 = jnp.max(pe, -1, keepdims=True)
    i1 = jnp.min(jnp.where(pe == v1, lane, big), -1, keepdims=True)
    pe2 = jnp.where(lane == i1, -1.0, pe)
    v2 = jnp.max(pe2, -1, keepdims=True)
    i2 = jnp.min(jnp.where(pe2 == v2, lane, big), -1, keepdims=True)
    vsum = v1 + v2
    w1 = pg_sel * v1 / vsum
    w2 = pg_sel * v2 / vsum
    e1 = i1 - MOE_GROUPS
    e2 = i2 - MOE_GROUPS

    @pl.when(pl.program_id(0) == 0)
    def _():
        cnt_ref[...] = jnp.zeros_like(cnt_ref)

    oh1 = lane == e1
    oh2 = lane == e2
    ohs = jnp.where(oh1, 1.0, jnp.where(oh2, 1.0, 0.0))
    r_i = lax.broadcasted_iota(jnp.int32, (MERGE_TILE, MERGE_TILE), 0)
    c_i = lax.broadcasted_iota(jnp.int32, (MERGE_TILE, MERGE_TILE), 1)
    strict_lower = jnp.where(c_i < r_i, 1.0, 0.0).astype(bf16)
    before = _dot(strict_lower, ohs.astype(bf16)) + cnt_ref[0:1, :]
    rank1 = jnp.sum(jnp.where(oh1, before, 0.0), -1, keepdims=True)
    rank2 = jnp.sum(jnp.where(oh2, before, 0.0), -1, keepdims=True)
    cnt_ref[0:1, :] = cnt_ref[0:1, :] + jnp.sum(ohs, 0, keepdims=True)
    cnt_out_ref[...] = cnt_ref[...]

    route = jnp.zeros((MERGE_TILE, ROUTE_LANES), f32)
    for k, val in enumerate((e1, e2, w1, w2, rank1, rank2)):
        route = jnp.where(lane == k, val, route)
    route_ref[...] = route


def _merge(y_ret, y_ssm, y_lru, xb, w_bf, x, wb, wo, g, b, wr, br):
    full = lambda a: _const_spec(a.shape, (0,) * a.ndim)
    row = lambda w: pl.BlockSpec((MERGE_TILE, w), lambda i: (i, 0))
    consts = (wb, wo, g, b, wr, br)
    return pl.pallas_call(
        _merge_kernel,
        grid=(N_TOK // MERGE_TILE,),
        in_specs=[row(BRANCH_W), row(BRANCH_W), row(BRANCH_W), row(D_MODEL),
                  _const_spec((D_MODEL, GATE_COLS), (0, GATE_BLK)),
                  row(D_MODEL)] + [full(a) for a in consts],
        out_specs=[row(D_MODEL), row(ROUTE_LANES),
                   pl.BlockSpec((V7X_SUBLANES, ROUTE_LANES), lambda i: (0, 0))],
        out_shape=[jax.ShapeDtypeStruct((N_TOK, D_MODEL), f32),
                   jax.ShapeDtypeStruct((N_TOK, ROUTE_LANES), f32),
                   jax.ShapeDtypeStruct((V7X_SUBLANES, ROUTE_LANES), f32)],
        scratch_shapes=[pltpu.VMEM((V7X_SUBLANES, ROUTE_LANES), f32)],
        compiler_params=_cparams(("arbitrary",)),
        name="merge",
    )(y_ret, y_ssm, y_lru, xb, w_bf, x, *consts)


def _dispatch_kernel(pos_ref, x1_ref, xs_hbm, sem):
    base = pl.program_id(0) * DSP_TILE

    def row_copy(r, dst_row):
        return pltpu.make_async_copy(x1_ref.at[pl.ds(r, 1), :], xs_hbm.at[pl.ds(dst_row, 1), :], sem.at[0])

    def issue(r, c):
        n = base + r
        row_copy(r, pos_ref[2 * n]).start()
        row_copy(r, pos_ref[2 * n + 1]).start()
        return c

    lax.fori_loop(0, DSP_TILE, issue, 0, unroll=8)
    for _ in range(MOE_TOPK):
        pltpu.make_async_copy(x1_ref, xs_hbm.at[pl.ds(0, DSP_TILE), :], sem.at[0]).wait()


def _dispatch(pos, x1):
    grid_spec = pltpu.PrefetchScalarGridSpec(
        num_scalar_prefetch=1,
        grid=(N_TOK // DSP_TILE,),
        in_specs=[pl.BlockSpec((DSP_TILE, D_MODEL), lambda s, pos: (s, 0))],
        out_specs=pl.BlockSpec(memory_space=pl.ANY),
        scratch_shapes=[pltpu.SemaphoreType.DMA((1,))],
    )
    return pl.pallas_call(
        _dispatch_kernel,
        grid_spec=grid_spec,
        out_shape=jax.ShapeDtypeStruct((N_PAIRS, D_MODEL), f32),
        compiler_params=_cparams(("arbitrary",)),
        name="dispatch",
    )(pos, x1)


def _moe_kernel(wt_ref, we_ref, wlo_ref, whi_ref, wfirst_ref, nw_ref,
                xs_ref, w1_ref, w3_ref, w2_ref, ys_ref, w1b, w3b, w2b):
    w = pl.program_id(0)

    @pl.when(w < nw_ref[0])
    def _():
        prev = we_ref[jnp.maximum(w - 1, 0)]

        @pl.when((w == 0) | (we_ref[w] != prev))
        def _():
            w1b[...] = w1_ref[0, 0].astype(bf16)
            w3b[...] = w3_ref[0, 0].astype(bf16)
            w2b[...] = w2_ref[0, 0].astype(bf16)

        xt = xs_ref[...].astype(bf16)
        h = jax.nn.silu(_dot(xt, w1b[...])) * _dot(xt, w3b[...])
        res = _dot(h.astype(bf16), w2b[...])
        row = lax.broadcasted_iota(jnp.int32, (MOE_TM, D_MODEL), 0)
        mine = jnp.where(row >= wlo_ref[w], row, MOE_TM) < whi_ref[w]

        @pl.when(wfirst_ref[w] == 1)
        def _():
            ys_ref[...] = jnp.where(mine, res, 0.0)

        @pl.when(wfirst_ref[w] == 0)
        def _():
            ys_ref[...] = jnp.where(mine, res, ys_ref[...])


def _moe(plan, xs, w1, w3, w2, layer):
    wspec = lambda shp: pl.BlockSpec((1, 1) + shp, lambda w, wt, we, *_: (layer, we[w], 0, 0))
    grid_spec = pltpu.PrefetchScalarGridSpec(
        num_scalar_prefetch=6,
        grid=(MOE_MAX_ITEMS,),
        in_specs=[pl.BlockSpec((MOE_TM, D_MODEL), lambda w, wt, *_: (wt[w], 0)),
                  wspec((D_MODEL, MOE_HIDDEN)), wspec((D_MODEL, MOE_HIDDEN)), wspec((MOE_HIDDEN, D_MODEL))],
        out_specs=pl.BlockSpec((MOE_TM, D_MODEL), lambda w, wt, *_: (wt[w], 0)),
        scratch_shapes=[pltpu.VMEM((D_MODEL, MOE_HIDDEN), bf16), pltpu.VMEM((D_MODEL, MOE_HIDDEN), bf16),
                        pltpu.VMEM((MOE_HIDDEN, D_MODEL), bf16)],
    )
    return pl.pallas_call(
        _moe_kernel,
        grid_spec=grid_spec,
        out_shape=jax.ShapeDtypeStruct((N_PAIRS, D_MODEL), f32),
        compiler_params=_cparams(("arbitrary",)),
        name="moe",
    )(*plan, xs, w1, w3, w2)


def _combine_kernel(final, pos_ref, ys_hbm, x1_ref, route_ref, g_ref, b_ref, *rest):
    if final:
        yp_ref, ysm_ref, buf, sem, t3_ref = rest
    else:
        o_ref, ob_ref, buf, sem = rest
    s = pl.program_id(0)
    nsteps = pl.num_programs(0)

    def gather_copy(row, slot, k, r):
        return pltpu.make_async_copy(ys_hbm.at[pl.ds(row, 1), :], buf.at[slot, k, pl.ds(r, 1), :], sem.at[slot])

    def issue(tile, slot):
        def body(r, c):
            n = tile * CMB_TILE + r
            gather_copy(pos_ref[2 * n], slot, 0, r).start()
            gather_copy(pos_ref[2 * n + 1], slot, 1, r).start()
            return c

        lax.fori_loop(0, CMB_TILE, body, 0, unroll=8)

    @pl.when(s == 0)
    def _():
        issue(0, 0)

    @pl.when(s + 1 < nsteps)
    def _():
        issue(s + 1, (s + 1) % 2)

    slot = s % 2
    for k in range(MOE_TOPK):
        pltpu.make_async_copy(ys_hbm.at[pl.ds(0, CMB_TILE), :], buf.at[slot, k], sem.at[slot]).wait()
    route = route_ref[...]
    moe = route[:, 2:3] * buf[slot, 0] + route[:, 3:4] * buf[slot, 1]
    y = _layer_norm_rows(DN_ALPHA * x1_ref[...] + moe, g_ref[...], b_ref[...])
    if not final:
        o_ref[...] = y
        ob_ref[...] = y.astype(bf16)
        return

    @pl.when(s < CMB_P_STEPS)
    def _():
        t3_ref[...] = y.reshape(CMB_TILE // BATCH, BATCH, D_MODEL)
        for b in range(BATCH):
            yp_ref[b] = t3_ref[:, b, :]

    steps_per_tile = CMB_TILE // DEC_BATCH
    for q in range(N_S // CMB_TILE):
        @pl.when(s == CMB_P_STEPS + q)
        def _():
            for h in range(steps_per_tile):
                ysm_ref[:, q * steps_per_tile + h, :] = y[h * DEC_BATCH:(h + 1) * DEC_BATCH, :]


def _combine(pos, ys, x1, route, g, b, final):
    if final:
        out_specs = [pl.BlockSpec((BATCH, CMB_TILE // BATCH, D_MODEL),
                                  lambda s, pos: (0, jnp.minimum(s, CMB_P_STEPS - 1), 0)),
                     pl.BlockSpec((DEC_BATCH, DEC_SEQ, D_MODEL), lambda s, pos: (0, 0, 0))]
        out_shape = [jax.ShapeDtypeStruct((BATCH, SEQ, D_MODEL), f32),
                     jax.ShapeDtypeStruct((DEC_BATCH, DEC_SEQ, D_MODEL), f32)]
        extra = [pltpu.VMEM((CMB_TILE // BATCH, BATCH, D_MODEL), f32)]
    else:
        out_specs = [pl.BlockSpec((CMB_TILE, D_MODEL), lambda s, pos: (s, 0))] * 2
        out_shape = [jax.ShapeDtypeStruct((N_TOK, D_MODEL), f32), jax.ShapeDtypeStruct((N_TOK, D_MODEL), bf16)]
        extra = []
    grid_spec = pltpu.PrefetchScalarGridSpec(
        num_scalar_prefetch=1,
        grid=(N_TOK // CMB_TILE,),
        in_specs=[pl.BlockSpec(memory_space=pl.ANY),
                  pl.BlockSpec((CMB_TILE, D_MODEL), lambda s, pos: (s, 0)),
                  pl.BlockSpec((CMB_TILE, ROUTE_LANES), lambda s, pos: (s, 0)),
                  pl.BlockSpec((1, D_MODEL), lambda s, pos: (0, 0)),
                  pl.BlockSpec((1, D_MODEL), lambda s, pos: (0, 0))],
        out_specs=out_specs,
        scratch_shapes=[pltpu.VMEM((2, 2, CMB_TILE, D_MODEL), f32), pltpu.SemaphoreType.DMA((2,))] + extra,
    )
    return pl.pallas_call(
        functools.partial(_combine_kernel, final),
        grid_spec=grid_spec,
        out_shape=out_shape,
        compiler_params=_cparams(("arbitrary",)),
        name="combine_out" if final else "combine",
    )(pos, ys, x1, route, g, b)


def _to_rows_kernel(xp_ref, xs_ref, o_ref, ob_ref, t3_ref):
    i = pl.program_id(0)

    @pl.when(i < P_TILES)
    def _():
        for b in range(BATCH):
            t3_ref[:, b, :] = xp_ref[b]
        rows = t3_ref[...].reshape(ROW_TILE, D_MODEL)
        o_ref[...] = rows
        ob_ref[...] = rows.astype(bf16)

    @pl.when(i >= P_TILES)
    def _():
        for t in range(DEC_SEQ):
            rows = xs_ref[:, t, :]
            o_ref[t * DEC_BATCH:(t + 1) * DEC_BATCH, :] = rows
            ob_ref[t * DEC_BATCH:(t + 1) * DEC_BATCH, :] = rows.astype(bf16)


def _to_rows(x_prompt, x_sample):
    return pl.pallas_call(
        _to_rows_kernel,
        grid=(N_TILES,),
        in_specs=[pl.BlockSpec((BATCH, P_TC, D_MODEL), lambda i: (0, jnp.minimum(i, P_TILES - 1), 0)),
                  pl.BlockSpec((DEC_BATCH, DEC_SEQ, D_MODEL), lambda i: (0, 0, 0))],
        out_specs=[pl.BlockSpec((ROW_TILE, D_MODEL), lambda i: (i, 0))] * 2,
        out_shape=[jax.ShapeDtypeStruct((N_TOK, D_MODEL), f32), jax.ShapeDtypeStruct((N_TOK, D_MODEL), bf16)],
        scratch_shapes=[pltpu.VMEM((P_TC, BATCH, D_MODEL), f32)],
        compiler_params=_cparams(("arbitrary",)),
        name="to_rows",
    )(x_prompt, x_sample)


def _lookup(table, idx):
    ar = jnp.arange(MOE_EXPERTS, dtype=jnp.int32)
    return jnp.sum(jnp.where(idx[..., None] == ar, table, 0), axis=-1)


def _dispatch_plan(route, cnt):
    i32 = jnp.int32
    e = route[:, 0:2].astype(i32)
    rank = route[:, 4:6].astype(i32)
    counts = cnt[0, :MOE_EXPERTS].astype(i32)
    ends = jnp.cumsum(counts)
    starts = ends - counts
    pos = (_lookup(starts, e) + rank).reshape(N_PAIRS)

    first_tile = starts // MOE_TM
    last_tile = (ends - 1) // MOE_TM
    ntiles = jnp.where(counts > 0, last_tile - first_tile + 1, 0)
    item_end = jnp.cumsum(ntiles)
    n_items = item_end[-1]
    w = jnp.minimum(jnp.arange(MOE_MAX_ITEMS, dtype=i32), n_items - 1)
    we = jnp.sum((item_end[None, :] <= w[:, None]).astype(i32), axis=-1)
    wt = _lookup(first_tile, we) + w - _lookup(item_end - ntiles, we)
    wlo = jnp.maximum(_lookup(starts, we) - wt * MOE_TM, 0)
    whi = jnp.minimum(_lookup(ends, we) - wt * MOE_TM, MOE_TM)
    wfirst = jnp.concatenate([jnp.ones((1,), i32), (wt[1:] != wt[:-1]).astype(i32)])
    return pos, (wt, we, wlo, whi, wfirst, n_items.reshape(1))


def _block_diag(w):
    n, a, b = w.shape
    eye = jnp.eye(n, dtype=w.dtype)
    return (w[:, :, None, :] * eye[:, None, :, None]).reshape(n * a, n * b)


def _ssm_params(a_re, a_im, log_dt, b_re, b_im, c_re, c_im):
    ar, ai = a_re, a_im
    dt = jnp.exp(log_dt)[:, None]
    mag = jnp.exp(ar * dt)
    lb_re = mag * jnp.cos(ai * dt)
    lb_im = mag * jnp.sin(ai * dt)
    den = ar * ar + ai * ai
    nr = lb_re - 1.0
    coef_re = (nr * ar + lb_im * ai) / den
    coef_im = (lb_im * ar - nr * ai) / den
    bb_re = coef_re[..., None] * b_re - coef_im[..., None] * b_im
    bb_im = coef_re[..., None] * b_im + coef_im[..., None] * b_re
    gk = SSM_GROUPS // SSM_KB

    def diag_blocks(w):
        return jnp.stack([_block_diag(w[k * gk:(k + 1) * gk]) for k in range(SSM_KB)]).astype(bf16)

    bbre = diag_blocks(bb_re.transpose(0, 2, 1))
    bbim = diag_blocks(bb_im.transpose(0, 2, 1))
    ccre = diag_blocks(c_re.transpose(0, 2, 1))
    ccim = diag_blocks(c_im.transpose(0, 2, 1))
    return (lb_re.reshape(1, SSM_LANES), lb_im.reshape(1, SSM_LANES), bbre, bbim, ccre, ccim)


def kernel(x_prompt, x_sample, state_ret, state_ssm_re, state_ssm_im, state_lru, state_conv, w_in, ret_gn_g, ret_gn_b, ssm_a_re, ssm_a_im, ssm_log_dt, ssm_b_re, ssm_b_im, ssm_c_re, ssm_c_im, ssm_d, ssm_w_glu, lru_conv_w, lru_conv_b, lru_wa, lru_ba, lru_wx, lru_bx, lru_lambda, w_branch, w_out, ln1_g, ln1_b, moe_w_group, moe_b_group, moe_w_expert, moe_b_expert, moe_w1, moe_w3, moe_w2, ln2_g, ln2_b):
    x, xb = _to_rows(x_prompt, x_sample)
    rope = _rope_tables()
    tabs_p = _ret_tables(RET_SUB_T)
    tabs_s = _ret_tables(DEC_SEQ)
    row = lambda v: v.reshape(1, -1)

    outs = [[] for _ in range(10)]
    for l in range(DEPTH):
        w_bf = w_in[l].astype(bf16)

        s0 = state_ret[l].reshape(DEC_BATCH, 2, 2 * RET_DK, RET_DV)
        y_ret, ret_p, ret_s = _retention(xb, w_bf, rope, s0, tabs_p, tabs_s,
                                         row(ret_gn_g[l]), row(ret_gn_b[l]))

        sp = _ssm_params(ssm_a_re[l], ssm_a_im[l], ssm_log_dt[l], ssm_b_re[l], ssm_b_im[l],
                         ssm_c_re[l], ssm_c_im[l])
        y_ssm, re_p, im_p, re_s, im_s = _ssm(
            xb, w_bf, state_ssm_re[l].reshape(DEC_BATCH, SSM_LANES), state_ssm_im[l].reshape(DEC_BATCH, SSM_LANES),
            *sp, row(ssm_d[l]), ssm_w_glu[l].astype(bf16))

        conv0 = state_conv[l].transpose(1, 0, 2).reshape(LRU_HIST_S, BRANCH_W)
        y_lru, lru_p, conv_p, lru_s, conv_s = _lru(
            xb, w_bf, state_lru[l], conv0, lru_conv_w[l], row(lru_conv_b[l]),
            _block_diag(lru_wa[l]).astype(bf16), row(lru_ba[l]),
            _block_diag(lru_wx[l]).astype(bf16), row(lru_bx[l]), row(lru_lambda[l]))

        wr = jnp.zeros((D_MODEL, ROUTE_LANES), f32)
        wr = wr.at[:, 0:MOE_GROUPS].set(moe_w_group[l]).at[:, MOE_GROUPS:MOE_GROUPS + MOE_EXPERTS].set(moe_w_expert[l])
        br = jnp.zeros((1, ROUTE_LANES), f32)
        br = br.at[0, 0:MOE_GROUPS].set(moe_b_group[l]).at[0, MOE_GROUPS:MOE_GROUPS + MOE_EXPERTS].set(moe_b_expert[l])
        x1, route, cnt = _merge(y_ret, y_ssm, y_lru, xb, w_bf, x, w_branch[l].astype(bf16), w_out[l].astype(bf16),
                                row(ln1_g[l]), row(ln1_b[l]), wr.astype(bf16), br)

        pos, plan = _dispatch_plan(route, cnt)
        xs = _dispatch(pos, x1)
        ys = _moe(plan, xs, moe_w1, moe_w3, moe_w2, l)
        x, xb = _combine(pos, ys, x1, route, row(ln2_g[l]), row(ln2_b[l]), final=(l == DEPTH - 1))

        outs[0].append(ret_p.reshape(BATCH, RET_HEADS, RET_DK, RET_DV))
        outs[1].append(re_p.reshape(BATCH, SSM_GROUPS, SSM_STATE))
        outs[2].append(im_p.reshape(BATCH, SSM_GROUPS, SSM_STATE))
        outs[3].append(lru_p)
        outs[4].append(conv_p.reshape(CONV_W - 1, BATCH, BRANCH_W).transpose(1, 0, 2))
        outs[5].append(ret_s.reshape(DEC_BATCH, RET_HEADS, RET_DK, RET_DV))
        outs[6].append(re_s.reshape(DEC_BATCH, SSM_GROUPS, SSM_STATE))
        outs[7].append(im_s.reshape(DEC_BATCH, SSM_GROUPS, SSM_STATE))
        outs[8].append(lru_s)
        outs[9].append(conv_s.reshape(CONV_W - 1, DEC_BATCH, BRANCH_W).transpose(1, 0, 2))

    y_prompt, y_sample = x, xb
    return (y_prompt, y_sample) + tuple(jnp.stack(o) for o in outs)
```

```python
import functools

import jax
import jax.numpy as jnp
import numpy as np
from jax import lax
from jax.experimental import pallas as pl
from jax.experimental.pallas import tpu as pltpu

f32 = jnp.float32
bf16 = jnp.bfloat16

D_MODEL = 1024
BATCH = 8
SEQ = 2048
DEPTH = 2
DEC_BATCH = 128
DEC_SEQ = 8
PAST_LEN = 16384
BRANCH_W = 512
N_BRANCH = 3
RET_HEADS = 4
RET_DK = 64
RET_DV = 128
ROPE_BASE = 10000.0
SSM_GROUP = 16
SSM_GROUPS = 32
SSM_STATE = 64
SSM_LANES = SSM_GROUPS * SSM_STATE
LRU_BLOCKS = 8
LRU_BW = 64
CONV_W = 4
LRU_C = 8.0
MOE_GROUPS = 4
MOE_PER_GROUP = 8
MOE_EXPERTS = 32
MOE_TOPK = 2
MOE_HIDDEN = 512
DN_ALPHA = (2.0 * DEPTH) ** 0.25
LN_EPS = 1e-5
D_IN = 6144

V7X_SUBLANES = 8
V7X_LANES = 128
V7X_VMEM_LIMIT = 56 * 1024 * 1024

N_P = BATCH * SEQ
N_S = DEC_BATCH * DEC_SEQ
N_TOK = N_P + N_S
ROW_TILE = 1024
P_TILES = N_P // ROW_TILE
N_TILES = N_TOK // ROW_TILE
P_TC = ROW_TILE // BATCH
RET_SUB_T = 32
RET_SUB_R = RET_SUB_T * BATCH
RET_SUBS = ROW_TILE // RET_SUB_R
S_BLOCKS = DEC_BATCH // BATCH
S_BLOCK_R = DEC_SEQ * BATCH
MERGE_TILE = 512
MOE_TM = 256
N_PAIRS = N_TOK * MOE_TOPK
MOE_MAX_ITEMS = N_PAIRS // MOE_TM + MOE_EXPERTS - 1
DSP_TILE = 512
CMB_TILE = 256
CMB_P_STEPS = N_P // CMB_TILE
ROUTE_LANES = 128


PROJ_RC = 256
RET_COLS, RET_BLK = 1536, 0
SSM_COLS, SSM_BLK = 512, 3
LRU_COLS, LRU_BLK = 1024, 2
GATE_COLS, GATE_BLK = 3072, 1


def _cparams(sem):
    return pltpu.CompilerParams(dimension_semantics=sem, vmem_limit_bytes=V7X_VMEM_LIMIT)


N_PLANES = D_MODEL // V7X_LANES


def _to_planes(ref, rows):
    for c in range(N_PLANES):
        ref[c] = rows[:, c * V7X_LANES:(c + 1) * V7X_LANES]


def _from_planes(ref):
    return jnp.concatenate([ref[c] for c in range(N_PLANES)], axis=1)


def _const_spec(block_shape, index):
    return pl.BlockSpec(block_shape, lambda *_: index, pipeline_mode=pl.Buffered(1))


def _dot(a, b):
    return jnp.dot(a, b, preferred_element_type=f32)


def _dot_nt(a, b):
    return lax.dot_general(a, b, (((1,), (1,)), ((), ())), preferred_element_type=f32)


def _dot_tn(a, b):
    return lax.dot_general(a, b, (((0,), (0,)), ((), ())), preferred_element_type=f32)


def _layer_norm_rows(x, g, b):
    mu = jnp.mean(x, -1, keepdims=True)
    xc = x - mu
    var = jnp.mean(xc * xc, -1, keepdims=True)
    return xc * lax.rsqrt(var + LN_EPS) * g + b


def _project_rows(xb_ref, w_ref, z_ref):
    for rc in range(ROW_TILE // PROJ_RC):
        rs = slice(rc * PROJ_RC, (rc + 1) * PROJ_RC)
        z_ref[rs, :] = _dot(xb_ref[rs, :], w_ref[...])


def _ret_block(q, k, v, g, cosb, sinb, mask_ref, qdec_ref, kdec_ref, cdec_ref, scat_ref, gng, gnb):
    rows = q.shape[0]
    lane_qk = lax.broadcasted_iota(jnp.int32, (rows, 2 * V7X_LANES), 1)
    first_half = (lane_qk & (RET_DK - 1)) < (RET_DK // 2)

    def rope(x):
        partner = jnp.where(first_half, pltpu.roll(x, 2 * V7X_LANES - RET_DK // 2, 1),
                            pltpu.roll(x, RET_DK // 2, 1))
        return x * cosb + partner * sinb

    q = rope(q)
    k = rope(k) * (RET_DK ** -0.5)
    kd = k * kdec_ref[...]
    lane = lax.broadcasted_iota(jnp.int32, (rows, V7X_LANES), 1)
    row_b = lax.broadcasted_iota(jnp.int32, (rows, V7X_LANES), 0) & (BATCH - 1)
    outs = []
    for p in range(2):
        qp = q[:, p * V7X_LANES:(p + 1) * V7X_LANES]
        kp = k[:, p * V7X_LANES:(p + 1) * V7X_LANES].astype(bf16)
        kdp = kd[:, p * V7X_LANES:(p + 1) * V7X_LANES]
        s_old = scat_ref[p]
        s_bf = s_old.astype(bf16)
        s_new = s_old * jnp.concatenate([cdec_ref[p]] * BATCH, axis=1)
        for hh in range(2):
            h = 2 * p + hh
            head_lanes = (lane >= RET_DK) if hh else (lane < RET_DK)
            qh = jnp.where(head_lanes, qp, 0.0).astype(bf16)
            kdh = jnp.where(head_lanes, kdp, 0.0).astype(bf16)
            vh = v[:, h * RET_DV:(h + 1) * RET_DV]
            vh_bf = vh.astype(bf16)
            sc = _dot_nt(qh, kp) * mask_ref[h]
            o = _dot(sc.astype(bf16), vh_bf)
            cross = _dot(qh, s_bf)
            oc = jnp.zeros((rows, RET_DV), f32)
            for b in range(BATCH):
                oc = oc + jnp.where(row_b == b, cross[:, b * RET_DV:(b + 1) * RET_DV], 0.0)
            o = o + oc * qdec_ref[h]
            vcat = jnp.concatenate([jnp.where(row_b == b, vh_bf, jnp.zeros_like(vh_bf))
                                    for b in range(BATCH)], axis=1)
            s_new = s_new + _dot_tn(kdh, vcat)
            mu = jnp.mean(o, -1, keepdims=True)
            oc2 = o - mu
            var = jnp.mean(oc2 * oc2, -1, keepdims=True)
            outs.append(oc2 * lax.rsqrt(var + LN_EPS))
        scat_ref[p] = s_new
    o = jnp.concatenate(outs, axis=1) * gng + gnb
    return jax.nn.silu(g) * o


def _per_step_rows(tab_ref, t0, steps):
    return jnp.concatenate(
        [jnp.broadcast_to(tab_ref[t0 + t:t0 + t + 1, :], (BATCH, tab_ref.shape[1])) for t in range(steps)], axis=0)


def _ret_kernel(xb_ref, w_ref, cos_ref, sin_ref, cos_s_ref, sin_s_ref, s0_ref,
                mask_p_ref, qdec_p_ref, kdec_p_ref, cdec_p_ref,
                mask_s_ref, qdec_s_ref, kdec_s_ref, cdec_s_ref,
                gng_ref, gnb_ref,
                y_ref, retp_ref, rets_ref, scat_ref, z_ref):
    i = pl.program_id(0)
    gng = gng_ref[...]
    gnb = gnb_ref[...]
    q_cols, k_cols = slice(0, 256), slice(256, 512)
    v_cols, g_cols = slice(512, 1024), slice(1024, 1536)

    @pl.when(i == 0)
    def _():
        scat_ref[...] = jnp.zeros_like(scat_ref)

    @pl.when(i <= P_TILES)
    def _():
        _project_rows(xb_ref, w_ref, z_ref)

    @pl.when(i < P_TILES)
    def _():
        for sc in range(RET_SUBS):
            r0 = sc * RET_SUB_R
            rs = slice(r0, r0 + RET_SUB_R)
            y_ref[rs, :] = _ret_block(
                z_ref[rs, q_cols], z_ref[rs, k_cols], z_ref[rs, v_cols], z_ref[rs, g_cols],
                _per_step_rows(cos_ref, sc * RET_SUB_T, RET_SUB_T), _per_step_rows(sin_ref, sc * RET_SUB_T, RET_SUB_T),
                mask_p_ref, qdec_p_ref, kdec_p_ref, cdec_p_ref, scat_ref, gng, gnb)

    @pl.when(i == P_TILES - 1)
    def _():
        for b in range(BATCH):
            for p in range(2):
                retp_ref[b, p] = scat_ref[p, :, b * RET_DV:(b + 1) * RET_DV]

    @pl.when(i >= P_TILES)
    def _():
        bb = i - P_TILES
        for b in range(BATCH):
            for p in range(2):
                scat_ref[p, :, b * RET_DV:(b + 1) * RET_DV] = s0_ref[b, p]

        def rows_of(cols):
            return jnp.concatenate(
                [z_ref[pl.ds(pl.multiple_of(t * DEC_BATCH + bb * BATCH, BATCH), BATCH), cols]
                 for t in range(DEC_SEQ)], axis=0)

        y = _ret_block(
            rows_of(q_cols), rows_of(k_cols), rows_of(v_cols),
            rows_of(g_cols), _per_step_rows(cos_s_ref, 0, DEC_SEQ), _per_step_rows(sin_s_ref, 0, DEC_SEQ),
            mask_s_ref, qdec_s_ref, kdec_s_ref, cdec_s_ref, scat_ref, gng, gnb)
        for t in range(DEC_SEQ):
            y_ref[pl.ds(pl.multiple_of(t * DEC_BATCH + bb * BATCH, BATCH), BATCH), :] = (
                y[t * BATCH:(t + 1) * BATCH, :])
        for b in range(BATCH):
            for p in range(2):
                rets_ref[b, p] = scat_ref[p, :, b * RET_DV:(b + 1) * RET_DV]


def _ret_tables(tc):
    rows = tc * BATCH
    log_g = jnp.log1p(-jnp.exp2(-5.0 - jnp.arange(RET_HEADS, dtype=f32)))
    t_idx = (jnp.arange(rows) // BATCH).astype(f32)
    b_idx = jnp.arange(rows) % BATCH
    rel = t_idx[:, None] - t_idx[None, :]
    same = b_idx[:, None] == b_idx[None, :]
    decay = jnp.exp(log_g[:, None, None] * jnp.maximum(rel, 0.0))
    mask = jnp.where((rel >= 0) & same, decay, 0.0)
    qdec = jnp.exp(log_g[:, None] * (t_idx[None, :] + 1.0))
    qdec = jnp.broadcast_to(qdec[:, :, None], (RET_HEADS, rows, RET_DV))
    kdec = jnp.exp(log_g[:, None] * (tc - 1.0 - t_idx[None, :]))
    kdec = jnp.broadcast_to(kdec.T[:, :, None], (rows, RET_HEADS, RET_DK)).reshape(rows, RET_HEADS * RET_DK)
    cdec = jnp.exp(log_g * tc)
    cdec = jnp.broadcast_to(cdec[:, None, None], (RET_HEADS, RET_DK, RET_DV)).reshape(2, 2 * RET_DK, RET_DV)
    return mask.astype(f32), qdec.astype(f32), kdec.astype(f32), cdec.astype(f32)


def _rope_tables():
    half = RET_DK // 2
    inv = ROPE_BASE ** (-jnp.arange(half, dtype=f32) / half)
    pos_p = jnp.arange(SEQ, dtype=f32)
    pos_s = PAST_LEN + jnp.arange(DEC_SEQ, dtype=f32)

    def tab(pos):
        ang = pos[:, None] * inv[None, :]
        cos = jnp.cos(ang)
        sin = jnp.sin(ang)
        cos_h = jnp.concatenate([cos, cos], axis=1)
        sin_h = jnp.concatenate([-sin, sin], axis=1)
        return jnp.tile(cos_h, (1, RET_HEADS)), jnp.tile(sin_h, (1, RET_HEADS))

    return tab(pos_p) + tab(pos_s)


def _retention(xb, w_bf, rope, s0_s, tabs_p, tabs_s, gng, gnb):
    n_steps = P_TILES + S_BLOCKS
    tile = lambda i: jnp.minimum(i, P_TILES)
    sblk = lambda i: jnp.maximum(i - P_TILES, 0)
    full = lambda a: pl.BlockSpec(a.shape, lambda i, _n=a.ndim: (0,) * _n)
    state_blk = (BATCH, 2, 2 * RET_DK, RET_DV)
    cos_p, sin_p, cos_s, sin_s = rope
    ptile = lambda i: jnp.minimum(i, P_TILES - 1)
    ins = [xb, w_bf, cos_p, sin_p, cos_s, sin_s, s0_s, *tabs_p, *tabs_s, gng, gnb]
    in_specs = [
        pl.BlockSpec((ROW_TILE, D_MODEL), lambda i: (tile(i), 0)),
        _const_spec((D_MODEL, RET_COLS), (0, RET_BLK)),
        pl.BlockSpec((P_TC, 256), lambda i: (ptile(i), 0)),
        pl.BlockSpec((P_TC, 256), lambda i: (ptile(i), 0)),
        full(cos_s), full(sin_s),
        pl.BlockSpec(state_blk, lambda i: (sblk(i), 0, 0, 0)),
    ] + [full(a) for a in (*tabs_p, *tabs_s, gng, gnb)]
    return pl.pallas_call(
        _ret_kernel,
        grid=(n_steps,),
        in_specs=in_specs,
        out_specs=[pl.BlockSpec((ROW_TILE, BRANCH_W), lambda i: (tile(i), 0)),
                   pl.BlockSpec(state_blk, lambda i: (0, 0, 0, 0)),
                   pl.BlockSpec(state_blk, lambda i: (sblk(i), 0, 0, 0))],
        out_shape=[jax.ShapeDtypeStruct((N_TOK, BRANCH_W), f32),
                   jax.ShapeDtypeStruct((BATCH, 2, 2 * RET_DK, RET_DV), f32),
                   jax.ShapeDtypeStruct((DEC_BATCH, 2, 2 * RET_DK, RET_DV), f32)],
        scratch_shapes=[pltpu.VMEM((2, 2 * RET_DK, BATCH * RET_DV), f32),
                        pltpu.VMEM((ROW_TILE, RET_COLS), f32)],
        compiler_params=_cparams(("arbitrary",)),
        name="retention",
    )(*ins)


SSM_LB = 512
SSM_RC = 256
SSM_KB = 2
SSM_KB_U = BRANCH_W // SSM_KB
SSM_KB_H = SSM_LANES // SSM_KB


def _ssm_scan(bre_ref, bim_ref, lre_ref, lim_ref, h_re0, h_im0, row0, nb_rows, steps, lb, unroll):
    ls = slice(lb * SSM_LB, (lb + 1) * SSM_LB)
    a_re = jnp.broadcast_to(lre_ref[:, ls], (V7X_SUBLANES, SSM_LB))
    a_im = jnp.broadcast_to(lim_ref[:, ls], (V7X_SUBLANES, SSM_LB))

    def step(t, carry):
        h_re, h_im = carry
        r = pl.multiple_of(row0 + t * nb_rows, V7X_SUBLANES)
        n_re = a_re * h_re - a_im * h_im + bre_ref[pl.ds(r, V7X_SUBLANES), ls]
        n_im = a_re * h_im + a_im * h_re + bim_ref[pl.ds(r, V7X_SUBLANES), ls]
        bre_ref[pl.ds(r, V7X_SUBLANES), ls] = n_re
        bim_ref[pl.ds(r, V7X_SUBLANES), ls] = n_im
        return n_re, n_im

    return lax.fori_loop(0, steps, step, (h_re0, h_im0), unroll=unroll)


def _ssm_kernel(xb_ref, w_ref, h0re_ref, h0im_ref, lre_ref, lim_ref, bbre_ref, bbim_ref, ccre_ref, ccim_ref,
                d_ref, wglu_ref,
                y_ref, pre_ref, pim_ref, sre_ref, sim_ref,
                bre_ref, bim_ref, hre_ref, him_ref, zs_ref):
    i = pl.program_id(0)
    _project_rows(xb_ref, w_ref, zs_ref)
    for rc in range(ROW_TILE // SSM_RC):
        rs = slice(rc * SSM_RC, (rc + 1) * SSM_RC)
        ub = zs_ref[rs, :].astype(bf16)
        for k in range(SSM_KB):
            uk = ub[:, k * SSM_KB_U:(k + 1) * SSM_KB_U]
            hs = slice(k * SSM_KB_H, (k + 1) * SSM_KB_H)
            bre_ref[rs, hs] = _dot(uk, bbre_ref[k])
            bim_ref[rs, hs] = _dot(uk, bbim_ref[k])

    @pl.when(i == 0)
    def _():
        hre_ref[...] = jnp.zeros_like(hre_ref)
        him_ref[...] = jnp.zeros_like(him_ref)

    @pl.when(i < P_TILES)
    def _():
        for lb in range(SSM_LANES // SSM_LB):
            ls = slice(lb * SSM_LB, (lb + 1) * SSM_LB)
            h_re, h_im = _ssm_scan(bre_ref, bim_ref, lre_ref, lim_ref, hre_ref[:, ls], him_ref[:, ls],
                                   0, BATCH, P_TC, lb, 8)
            hre_ref[:, ls] = h_re
            him_ref[:, ls] = h_im
        pre_ref[...] = hre_ref[...]
        pim_ref[...] = him_ref[...]

    @pl.when(i >= P_TILES)
    def _():
        def per_row_tile(rt, c):
            r0 = pl.multiple_of(rt * V7X_SUBLANES, V7X_SUBLANES)
            for lb in range(SSM_LANES // SSM_LB):
                ls = slice(lb * SSM_LB, (lb + 1) * SSM_LB)
                _ssm_scan(bre_ref, bim_ref, lre_ref, lim_ref,
                          h0re_ref[pl.ds(r0, V7X_SUBLANES), ls], h0im_ref[pl.ds(r0, V7X_SUBLANES), ls],
                          r0, DEC_BATCH, DEC_SEQ, lb, True)
            return c

        lax.fori_loop(0, DEC_BATCH // V7X_SUBLANES, per_row_tile, 0)
        last = (DEC_SEQ - 1) * DEC_BATCH
        sre_ref[...] = bre_ref[last:last + DEC_BATCH, :]
        sim_ref[...] = bim_ref[last:last + DEC_BATCH, :]

    for rc in range(ROW_TILE // SSM_RC):
        rs = slice(rc * SSM_RC, (rc + 1) * SSM_RC)
        ch = []
        for k in range(SSM_KB):
            hs = slice(k * SSM_KB_H, (k + 1) * SSM_KB_H)
            ch.append(_dot(bre_ref[rs, hs].astype(bf16), ccre_ref[k]) - _dot(bim_ref[rs, hs].astype(bf16), ccim_ref[k]))
        y = jnp.concatenate(ch, axis=1) + d_ref[...] * zs_ref[rs, :]
        zz = jax.nn.gelu(y)
        y_ref[rs, :] = zz * jax.nn.sigmoid(_dot(zz.astype(bf16), wglu_ref[...]))


def _ssm(xb, w_bf, h0re, h0im, lre, lim, bbre, bbim, ccre, ccim, dvec, wglu):
    full = lambda a: _const_spec(a.shape, (0,) * a.ndim)
    consts = (h0re, h0im, lre, lim, bbre, bbim, ccre, ccim, dvec, wglu)
    return pl.pallas_call(
        _ssm_kernel,
        grid=(N_TILES,),
        in_specs=[pl.BlockSpec((ROW_TILE, D_MODEL), lambda i: (i, 0)),
                  _const_spec((D_MODEL, SSM_COLS), (0, SSM_BLK))] + [full(a) for a in consts],
        out_specs=[pl.BlockSpec((ROW_TILE, BRANCH_W), lambda i: (i, 0)),
                   pl.BlockSpec((BATCH, SSM_LANES), lambda i: (0, 0)),
                   pl.BlockSpec((BATCH, SSM_LANES), lambda i: (0, 0)),
                   pl.BlockSpec((DEC_BATCH, SSM_LANES), lambda i: (0, 0)),
                   pl.BlockSpec((DEC_BATCH, SSM_LANES), lambda i: (0, 0))],
        out_shape=[jax.ShapeDtypeStruct((N_TOK, BRANCH_W), f32),
                   jax.ShapeDtypeStruct((BATCH, SSM_LANES), f32),
                   jax.ShapeDtypeStruct((BATCH, SSM_LANES), f32),
                   jax.ShapeDtypeStruct((DEC_BATCH, SSM_LANES), f32),
                   jax.ShapeDtypeStruct((DEC_BATCH, SSM_LANES), f32)],
        scratch_shapes=[pltpu.VMEM((ROW_TILE, SSM_LANES), f32), pltpu.VMEM((ROW_TILE, SSM_LANES), f32),
                        pltpu.VMEM((BATCH, SSM_LANES), f32), pltpu.VMEM((BATCH, SSM_LANES), f32),
                        pltpu.VMEM((ROW_TILE, SSM_COLS), f32)],
        compiler_params=_cparams(("arbitrary",)),
        name="ssm",
    )(xb, w_bf, *consts)


LRU_HIST_P = (CONV_W - 1) * BATCH
LRU_HIST_S = (CONV_W - 1) * DEC_BATCH


def _lru_gates(xe_ref, nb_rows, cw_ref, cb_ref, wa_ref, ba_ref, wx_ref, bx_ref, lam_ref):
    xc = cb_ref[...] + xe_ref[0:ROW_TILE, :] * cw_ref[0:1, :]
    for j in range(1, CONV_W):
        xc = xc + xe_ref[j * nb_rows:j * nb_rows + ROW_TILE, :] * cw_ref[j:j + 1, :]
    xcb = xc.astype(bf16)
    r = jax.nn.sigmoid(_dot(xcb, wa_ref[...]) + ba_ref[...])
    ig = jax.nn.sigmoid(_dot(xcb, wx_ref[...]) + bx_ref[...])
    log_a = -LRU_C * r * jax.nn.softplus(-lam_ref[...])
    a = jnp.exp(log_a)
    b = jnp.sqrt(-jnp.tanh(log_a) * (jnp.exp(2.0 * log_a) + 1.0)) * (ig * xc)
    return a, b


def _lru_kernel(xb_ref, w_ref, h0_ref, conv0_ref, cw_ref, cb_ref, wa_ref, ba_ref, wx_ref, bx_ref, lam_ref,
                y_ref, hp_ref, convp_ref, hs_ref, convs_ref,
                xe_ref, a_ref, b_ref, hc_ref, z_ref):
    i = pl.program_id(0)
    params = (cw_ref, cb_ref, wa_ref, ba_ref, wx_ref, bx_ref, lam_ref)
    _project_rows(xb_ref, w_ref, z_ref)
    zx_ref = z_ref.at[:, 0:BRANCH_W]
    zg_ref = z_ref.at[:, BRANCH_W:2 * BRANCH_W]

    @pl.when(i == 0)
    def _():
        xe_ref[0:LRU_HIST_P, :] = jnp.zeros((LRU_HIST_P, BRANCH_W), f32)
        hc_ref[...] = jnp.zeros_like(hc_ref)

    @pl.when(i < P_TILES)
    def _():
        xe_ref[LRU_HIST_P:LRU_HIST_P + ROW_TILE, :] = zx_ref[...]
        a, b = _lru_gates(xe_ref, BATCH, *params)
        a_ref[...] = a
        b_ref[...] = b
        hist = xe_ref[ROW_TILE:ROW_TILE + LRU_HIST_P, :]
        xe_ref[0:LRU_HIST_P, :] = hist
        convp_ref[...] = hist

        def step(t, h):
            r = pl.multiple_of(t * BATCH, BATCH)
            h = a_ref[pl.ds(r, BATCH), :] * h + b_ref[pl.ds(r, BATCH), :]
            b_ref[pl.ds(r, BATCH), :] = h
            return h

        h = lax.fori_loop(0, P_TC, step, hc_ref[...], unroll=8)
        hc_ref[...] = h
        hp_ref[...] = h

    @pl.when(i >= P_TILES)
    def _():
        xe_ref[0:LRU_HIST_S, :] = conv0_ref[...]
        xe_ref[LRU_HIST_S:LRU_HIST_S + ROW_TILE, :] = zx_ref[...]
        a, b = _lru_gates(xe_ref, DEC_BATCH, *params)
        a_ref[...] = a
        b_ref[...] = b
        convs_ref[...] = xe_ref[ROW_TILE:ROW_TILE + LRU_HIST_S, :]

        def per_row_tile(rt, c):
            r0 = pl.multiple_of(rt * V7X_SUBLANES, V7X_SUBLANES)
            h = h0_ref[pl.ds(r0, V7X_SUBLANES), :]
            for t in range(DEC_SEQ):
                r = pl.multiple_of(t * DEC_BATCH + r0, V7X_SUBLANES)
                h = a_ref[pl.ds(r, V7X_SUBLANES), :] * h + b_ref[pl.ds(r, V7X_SUBLANES), :]
                b_ref[pl.ds(r, V7X_SUBLANES), :] = h
            return c

        lax.fori_loop(0, DEC_BATCH // V7X_SUBLANES, per_row_tile, 0)
        last = (DEC_SEQ - 1) * DEC_BATCH
        hs_ref[...] = b_ref[last:last + DEC_BATCH, :]

    y_ref[...] = b_ref[...] * jax.nn.gelu(zg_ref[...])


def _lru(xb, w_bf, h0, conv0, cw, cb, wa, ba, wx, bx, lam):
    full = lambda a: _const_spec(a.shape, (0,) * a.ndim)
    consts = (h0, conv0, cw, cb, wa, ba, wx, bx, lam)
    return pl.pallas_call(
        _lru_kernel,
        grid=(N_TILES,),
        in_specs=[pl.BlockSpec((ROW_TILE, D_MODEL), lambda i: (i, 0)),
                  _const_spec((D_MODEL, LRU_COLS), (0, LRU_BLK))] + [full(a) for a in consts],
        out_specs=[pl.BlockSpec((ROW_TILE, BRANCH_W), lambda i: (i, 0)),
                   pl.BlockSpec((BATCH, BRANCH_W), lambda i: (0, 0)),
                   pl.BlockSpec((LRU_HIST_P, BRANCH_W), lambda i: (0, 0)),
                   pl.BlockSpec((DEC_BATCH, BRANCH_W), lambda i: (0, 0)),
                   pl.BlockSpec((LRU_HIST_S, BRANCH_W), lambda i: (0, 0))],
        out_shape=[jax.ShapeDtypeStruct((N_TOK, BRANCH_W), f32),
                   jax.ShapeDtypeStruct((BATCH, BRANCH_W), f32),
                   jax.ShapeDtypeStruct((LRU_HIST_P, BRANCH_W), f32),
                   jax.ShapeDtypeStruct((DEC_BATCH, BRANCH_W), f32),
                   jax.ShapeDtypeStruct((LRU_HIST_S, BRANCH_W), f32)],
        scratch_shapes=[pltpu.VMEM((ROW_TILE + LRU_HIST_S, BRANCH_W), f32),
                        pltpu.VMEM((ROW_TILE, BRANCH_W), f32), pltpu.VMEM((ROW_TILE, BRANCH_W), f32),
                        pltpu.VMEM((BATCH, BRANCH_W), f32),
                        pltpu.VMEM((ROW_TILE, LRU_COLS), f32)],
        compiler_params=_cparams(("arbitrary",)),
        name="lru",
    )(xb, w_bf, *consts)


def _merge_kernel(yr_ref, ys_ref, yl_ref, xb_ref, wg_ref, x_ref, wb_ref, wo_ref, g_ref, b_ref, wr_ref, br_ref,
                  x1_ref, route_ref, cnt_out_ref, cnt_ref):
    merged = jnp.zeros((MERGE_TILE, D_MODEL), f32)
    xb = xb_ref[...]
    for n, y_ref in enumerate((yr_ref, ys_ref, yl_ref)):
        proj = _dot(y_ref[...].astype(bf16), wb_ref[n])
        gate = jax.nn.sigmoid(_dot(xb, wg_ref[:, n * D_MODEL:(n + 1) * D_MODEL]))
        merged = merged + gate * proj
    mix = _dot(merged.astype(bf16), wo_ref[...])
    x1 = _layer_norm_rows(DN_ALPHA * x_ref[...] + mix, g_ref[...], b_ref[...])
    _to_planes(x1_ref, x1)

    logits = _dot(x1.astype(bf16), wr_ref[...]) + br_ref[...]
    lane = lax.broadcasted_iota(jnp.int32, (MERGE_TILE, ROUTE_LANES), 1).astype(f32)
    big = jnp.float32(ROUTE_LANES)
    neg = jnp.float32(-jnp.inf)
    is_g = lane < MOE_GROUPS
    lg = jnp.where(is_g, logits, neg)
    mg = jnp.max(lg, -1, keepdims=True)
    gsel = jnp.min(jnp.where(lg == mg, lane, big), -1, keepdims=True)
    sum_g = jnp.sum(jnp.where(is_g, jnp.exp(lg - mg), 0.0), -1, keepdims=True)
    pg_sel = 1.0 / sum_g
    lo = MOE_GROUPS + gsel * MOE_PER_GROUP
    is_e = jnp.abs(lane - lo - 0.5 * (MOE_PER_GROUP - 1)) < 0.5 * MOE_PER_GROUP
    le = jnp.where(is_e, logits, neg)
    me = jnp.max(le, -1, keepdims=True)
    ex = jnp.where(is_e, jnp.exp(le - me), 0.0)
    pe = jnp.where(is_e, ex / jnp.sum(ex, -1, keepdims=True), -1.0)
    v1 = jnp.max(pe, -1, keepdims=True)
    i1 = jnp.min(jnp.where(pe == v1, lane, big), -1, keepdims=True)
    pe2 = jnp.where(lane == i1, -1.0, pe)
    v2 = jnp.max(pe2, -1, keepdims=True)
    i2 = jnp.min(jnp.where(pe2 == v2, lane, big), -1, keepdims=True)
    vsum = v1 + v2
    w1 = pg_sel * v1 / vsum
    w2 = pg_sel * v2 / vsum
    e1 = i1 - MOE_GROUPS
    e2 = i2 - MOE_GROUPS

    @pl.when(pl.program_id(0) == 0)
    def _():
        cnt_ref[...] = jnp.zeros_like(cnt_ref)

    oh1 = lane == e1
    oh2 = lane == e2
    ohs = jnp.where(oh1, 1.0, jnp.where(oh2, 1.0, 0.0))
    r_i = lax.broadcasted_iota(jnp.int32, (MERGE_TILE, MERGE_TILE), 0)
    c_i = lax.broadcasted_iota(jnp.int32, (MERGE_TILE, MERGE_TILE), 1)
    strict_lower = jnp.where(c_i < r_i, 1.0, 0.0).astype(bf16)
    before = _dot(strict_lower, ohs.astype(bf16)) + cnt_ref[0:1, :]
    rank1 = jnp.sum(jnp.where(oh1, before, 0.0), -1, keepdims=True)
    rank2 = jnp.sum(jnp.where(oh2, before, 0.0), -1, keepdims=True)
    cnt_ref[0:1, :] = cnt_ref[0:1, :] + jnp.sum(ohs, 0, keepdims=True)
    cnt_out_ref[...] = cnt_ref[...]

    route = jnp.zeros((MERGE_TILE, ROUTE_LANES), f32)
    for k, val in enumerate((e1, e2, w1, w2, rank1, rank2)):
        route = jnp.where(lane == k, val, route)
    route_ref[...] = route


def _merge(y_ret, y_ssm, y_lru, xb, w_bf, x, wb, wo, g, b, wr, br):
    full = lambda a: _const_spec(a.shape, (0,) * a.ndim)
    row = lambda w: pl.BlockSpec((MERGE_TILE, w), lambda i: (i, 0))
    consts = (wb, wo, g, b, wr, br)
    return pl.pallas_call(
        _merge_kernel,
        grid=(N_TOK // MERGE_TILE,),
        in_specs=[row(BRANCH_W), row(BRANCH_W), row(BRANCH_W), row(D_MODEL),
                  _const_spec((D_MODEL, GATE_COLS), (0, GATE_BLK)),
                  row(D_MODEL)] + [full(a) for a in consts],
        out_specs=[pl.BlockSpec((N_PLANES, MERGE_TILE, V7X_LANES), lambda i: (0, i, 0)), row(ROUTE_LANES),
                   pl.BlockSpec((V7X_SUBLANES, ROUTE_LANES), lambda i: (0, 0))],
        out_shape=[jax.ShapeDtypeStruct((N_PLANES, N_TOK, V7X_LANES), f32),
                   jax.ShapeDtypeStruct((N_TOK, ROUTE_LANES), f32),
                   jax.ShapeDtypeStruct((V7X_SUBLANES, ROUTE_LANES), f32)],
        scratch_shapes=[pltpu.VMEM((V7X_SUBLANES, ROUTE_LANES), f32)],
        compiler_params=_cparams(("arbitrary",)),
        name="merge",
    )(y_ret, y_ssm, y_lru, xb, w_bf, x, *consts)


def _dispatch_kernel(pos_ref, x1_ref, xs_hbm, sem):
    base = pl.program_id(0) * DSP_TILE

    def row_copy(r, dst_row):
        return pltpu.make_async_copy(x1_ref.at[:, r, :], xs_hbm.at[dst_row], sem.at[0])

    def issue(r, c):
        n = base + r
        row_copy(r, pos_ref[2 * n]).start()
        row_copy(r, pos_ref[2 * n + 1]).start()
        return c

    lax.fori_loop(0, DSP_TILE, issue, 0, unroll=8)
    for _ in range(MOE_TOPK):
        pltpu.make_async_copy(x1_ref, x1_ref, sem.at[0]).wait()


def _dispatch(pos, x1p):
    grid_spec = pltpu.PrefetchScalarGridSpec(
        num_scalar_prefetch=1,
        grid=(N_TOK // DSP_TILE,),
        in_specs=[pl.BlockSpec((N_PLANES, DSP_TILE, V7X_LANES), lambda s, pos: (0, s, 0))],
        out_specs=pl.BlockSpec(memory_space=pl.ANY),
        scratch_shapes=[pltpu.SemaphoreType.DMA((1,))],
    )
    return pl.pallas_call(
        _dispatch_kernel,
        grid_spec=grid_spec,
        out_shape=jax.ShapeDtypeStruct((N_PAIRS, N_PLANES, V7X_LANES), f32),
        compiler_params=_cparams(("arbitrary",)),
        name="dispatch",
    )(pos, x1p)


def _moe_kernel(wt_ref, we_ref, wlo_ref, whi_ref, wfirst_ref, wlast_ref, nw_ref,
                xs_hbm, w1_ref, w3_ref, w2_ref, ys_hbm,
                xin, yout, sem_in, sem_out, w1b, w3b, w2b):
    w = pl.program_id(0)
    n_items = nw_ref[0]

    def in_copy(item, slot, c):
        r0 = pl.multiple_of(wt_ref[item] * MOE_TM, MOE_TM)
        return pltpu.make_async_copy(xs_hbm.at[pl.ds(r0, MOE_TM), c, :], xin.at[slot, c], sem_in.at[slot])

    def out_copy(tile, slot, c):
        r0 = pl.multiple_of(tile * MOE_TM, MOE_TM)
        return pltpu.make_async_copy(yout.at[slot, c], ys_hbm.at[pl.ds(r0, MOE_TM), c, :], sem_out.at[slot])

    @pl.when(w == 0)
    def _():
        for c in range(N_PLANES):
            in_copy(0, 0, c).start()

    @pl.when(w + 1 < n_items)
    def _():
        for c in range(N_PLANES):
            in_copy(w + 1, (w + 1) % 2, c).start()

    @pl.when(w < n_items)
    def _():
        slot = w % 2
        tile = wt_ref[w]
        oslot = tile % 2
        for c in range(N_PLANES):
            in_copy(w, slot, c).wait()
        prev = we_ref[jnp.maximum(w - 1, 0)]

        @pl.when((w == 0) | (we_ref[w] != prev))
        def _():
            w1b[...] = w1_ref[0, 0].astype(bf16)
            w3b[...] = w3_ref[0, 0].astype(bf16)
            w2b[...] = w2_ref[0, 0].astype(bf16)

        xt = _from_planes(xin.at[slot]).astype(bf16)
        h = jax.nn.silu(_dot(xt, w1b[...])) * _dot(xt, w3b[...])
        res = _dot(h.astype(bf16), w2b[...])
        row = lax.broadcasted_iota(jnp.int32, (MOE_TM, D_MODEL), 0)
        mine = jnp.where(row >= wlo_ref[w], row, MOE_TM) < whi_ref[w]

        @pl.when(wfirst_ref[w] == 1)
        def _():
            @pl.when(tile >= 2)
            def _():
                for c in range(N_PLANES):
                    out_copy(0, oslot, c).wait()

            _to_planes(yout.at[oslot], jnp.where(mine, res, 0.0))

        @pl.when(wfirst_ref[w] == 0)
        def _():
            _to_planes(yout.at[oslot], jnp.where(mine, res, _from_planes(yout.at[oslot])))

        @pl.when(wlast_ref[w] == 1)
        def _():
            for c in range(N_PLANES):
                out_copy(tile, oslot, c).start()

        @pl.when(w == n_items - 1)
        def _():
            for c in range(N_PLANES):
                out_copy(0, oslot, c).wait()

            @pl.when(tile >= 1)
            def _():
                for c in range(N_PLANES):
                    out_copy(0, 1 - oslot, c).wait()


def _moe(plan, xs, w1, w3, w2, layer):
    wspec = lambda shp: pl.BlockSpec((1, 1) + shp, lambda w, wt, we, *_: (layer, we[w], 0, 0))
    grid_spec = pltpu.PrefetchScalarGridSpec(
        num_scalar_prefetch=7,
        grid=(MOE_MAX_ITEMS,),
        in_specs=[pl.BlockSpec(memory_space=pl.ANY),
                  wspec((D_MODEL, MOE_HIDDEN)), wspec((D_MODEL, MOE_HIDDEN)), wspec((MOE_HIDDEN, D_MODEL))],
        out_specs=pl.BlockSpec(memory_space=pl.ANY),
        scratch_shapes=[pltpu.VMEM((2, N_PLANES, MOE_TM, V7X_LANES), f32),
                        pltpu.VMEM((2, N_PLANES, MOE_TM, V7X_LANES), f32),
                        pltpu.SemaphoreType.DMA((2,)), pltpu.SemaphoreType.DMA((2,)),
                        pltpu.VMEM((D_MODEL, MOE_HIDDEN), bf16), pltpu.VMEM((D_MODEL, MOE_HIDDEN), bf16),
                        pltpu.VMEM((MOE_HIDDEN, D_MODEL), bf16)],
    )
    return pl.pallas_call(
        _moe_kernel,
        grid_spec=grid_spec,
        out_shape=jax.ShapeDtypeStruct((N_PAIRS, N_PLANES, V7X_LANES), f32),
        compiler_params=_cparams(("arbitrary",)),
        name="moe",
    )(*plan, xs, w1, w3, w2)


def _combine_kernel(final, pos_ref, ys_hbm, x1_ref, route_ref, g_ref, b_ref, *rest):
    if final:
        yp_ref, ysm_ref, buf, sem, t3_ref = rest
    else:
        o_ref, ob_ref, buf, sem = rest
    s = pl.program_id(0)
    nsteps = pl.num_programs(0)

    def gather_copy(row, slot, k, r):
        return pltpu.make_async_copy(ys_hbm.at[row], buf.at[slot, k, :, r, :], sem.at[slot])

    def issue(tile, slot):
        def body(r, c):
            n = tile * CMB_TILE + r
            gather_copy(pos_ref[2 * n], slot, 0, r).start()
            gather_copy(pos_ref[2 * n + 1], slot, 1, r).start()
            return c

        lax.fori_loop(0, CMB_TILE, body, 0, unroll=8)

    @pl.when(s == 0)
    def _():
        issue(0, 0)

    @pl.when(s + 1 < nsteps)
    def _():
        issue(s + 1, (s + 1) % 2)

    slot = s % 2
    for k in range(MOE_TOPK):
        pltpu.make_async_copy(buf.at[slot, k], buf.at[slot, k], sem.at[slot]).wait()
    route = route_ref[...]
    moe = route[:, 2:3] * _from_planes(buf.at[slot, 0]) + route[:, 3:4] * _from_planes(buf.at[slot, 1])
    y = _layer_norm_rows(DN_ALPHA * _from_planes(x1_ref) + moe, g_ref[...], b_ref[...])
    if not final:
        o_ref[...] = y
        ob_ref[...] = y.astype(bf16)
        return

    @pl.when(s < CMB_P_STEPS)
    def _():
        t3_ref[...] = y.reshape(CMB_TILE // BATCH, BATCH, D_MODEL)
        for b in range(BATCH):
            yp_ref[b] = t3_ref[:, b, :]

    steps_per_tile = CMB_TILE // DEC_BATCH
    for q in range(N_S // CMB_TILE):
        @pl.when(s == CMB_P_STEPS + q)
        def _():
            for h in range(steps_per_tile):
                ysm_ref[:, q * steps_per_tile + h, :] = y[h * DEC_BATCH:(h + 1) * DEC_BATCH, :]


def _combine(pos, ys, x1, route, g, b, final):
    if final:
        out_specs = [pl.BlockSpec((BATCH, CMB_TILE // BATCH, D_MODEL),
                                  lambda s, pos: (0, jnp.minimum(s, CMB_P_STEPS - 1), 0)),
                     pl.BlockSpec((DEC_BATCH, DEC_SEQ, D_MODEL), lambda s, pos: (0, 0, 0))]
        out_shape = [jax.ShapeDtypeStruct((BATCH, SEQ, D_MODEL), f32),
                     jax.ShapeDtypeStruct((DEC_BATCH, DEC_SEQ, D_MODEL), f32)]
        extra = [pltpu.VMEM((CMB_TILE // BATCH, BATCH, D_MODEL), f32)]
    else:
        out_specs = [pl.BlockSpec((CMB_TILE, D_MODEL), lambda s, pos: (s, 0))] * 2
        out_shape = [jax.ShapeDtypeStruct((N_TOK, D_MODEL), f32), jax.ShapeDtypeStruct((N_TOK, D_MODEL), bf16)]
        extra = []
    grid_spec = pltpu.PrefetchScalarGridSpec(
        num_scalar_prefetch=1,
        grid=(N_TOK // CMB_TILE,),
        in_specs=[pl.BlockSpec(memory_space=pl.ANY),
                  pl.BlockSpec((N_PLANES, CMB_TILE, V7X_LANES), lambda s, pos: (0, s, 0)),
                  pl.BlockSpec((CMB_TILE, ROUTE_LANES), lambda s, pos: (s, 0)),
                  pl.BlockSpec((1, D_MODEL), lambda s, pos: (0, 0)),
                  pl.BlockSpec((1, D_MODEL), lambda s, pos: (0, 0))],
        out_specs=out_specs,
        scratch_shapes=[pltpu.VMEM((2, MOE_TOPK, N_PLANES, CMB_TILE, V7X_LANES), f32),
                        pltpu.SemaphoreType.DMA((2,))] + extra,
    )
    return pl.pallas_call(
        functools.partial(_combine_kernel, final),
        grid_spec=grid_spec,
        out_shape=out_shape,
        compiler_params=_cparams(("arbitrary",)),
        name="combine_out" if final else "combine",
    )(pos, ys, x1, route, g, b)


def _to_rows_kernel(xp_ref, xs_ref, o_ref, ob_ref, t3_ref):
    i = pl.program_id(0)

    @pl.when(i < P_TILES)
    def _():
        for b in range(BATCH):
            t3_ref[:, b, :] = xp_ref[b]
        rows = t3_ref[...].reshape(ROW_TILE, D_MODEL)
        o_ref[...] = rows
        ob_ref[...] = rows.astype(bf16)

    @pl.when(i >= P_TILES)
    def _():
        for t in range(DEC_SEQ):
            rows = xs_ref[:, t, :]
            o_ref[t * DEC_BATCH:(t + 1) * DEC_BATCH, :] = rows
            ob_ref[t * DEC_BATCH:(t + 1) * DEC_BATCH, :] = rows.astype(bf16)


def _to_rows(x_prompt, x_sample):
    return pl.pallas_call(
        _to_rows_kernel,
        grid=(N_TILES,),
        in_specs=[pl.BlockSpec((BATCH, P_TC, D_MODEL), lambda i: (0, jnp.minimum(i, P_TILES - 1), 0)),
                  pl.BlockSpec((DEC_BATCH, DEC_SEQ, D_MODEL), lambda i: (0, 0, 0))],
        out_specs=[pl.BlockSpec((ROW_TILE, D_MODEL), lambda i: (i, 0))] * 2,
        out_shape=[jax.ShapeDtypeStruct((N_TOK, D_MODEL), f32), jax.ShapeDtypeStruct((N_TOK, D_MODEL), bf16)],
        scratch_shapes=[pltpu.VMEM((P_TC, BATCH, D_MODEL), f32)],
        compiler_params=_cparams(("arbitrary",)),
        name="to_rows",
    )(x_prompt, x_sample)


def _lookup(table, idx):
    ar = jnp.arange(MOE_EXPERTS, dtype=jnp.int32)
    return jnp.sum(jnp.where(idx[..., None] == ar, table, 0), axis=-1)


def _dispatch_plan(route, cnt):
    i32 = jnp.int32
    e = route[:, 0:2].astype(i32)
    rank = route[:, 4:6].astype(i32)
    counts = cnt[0, :MOE_EXPERTS].astype(i32)
    ends = jnp.cumsum(counts)
    starts = ends - counts
    pos = (_lookup(starts, e) + rank).reshape(N_PAIRS)

    first_tile = starts // MOE_TM
    last_tile = (ends - 1) // MOE_TM
    ntiles = jnp.where(counts > 0, last_tile - first_tile + 1, 0)
    item_end = jnp.cumsum(ntiles)
    n_items = item_end[-1]
    w = jnp.minimum(jnp.arange(MOE_MAX_ITEMS, dtype=i32), n_items - 1)
    we = jnp.sum((item_end[None, :] <= w[:, None]).astype(i32), axis=-1)
    wt = _lookup(first_tile, we) + w - _lookup(item_end - ntiles, we)
    wlo = jnp.maximum(_lookup(starts, we) - wt * MOE_TM, 0)
    whi = jnp.minimum(_lookup(ends, we) - wt * MOE_TM, MOE_TM)
    changes = (wt[1:] != wt[:-1]).astype(i32)
    wfirst = jnp.concatenate([jnp.ones((1,), i32), changes])
    wlast = jnp.concatenate([changes, jnp.ones((1,), i32)])
    wlast = jnp.where(jnp.arange(MOE_MAX_ITEMS, dtype=i32) == n_items - 1, 1, wlast)
    return pos, (wt, we, wlo, whi, wfirst, wlast, n_items.reshape(1))


def _block_diag(w):
    n, a, b = w.shape
    eye = jnp.eye(n, dtype=w.dtype)
    return (w[:, :, None, :] * eye[:, None, :, None]).reshape(n * a, n * b)


def _ssm_params(a_re, a_im, log_dt, b_re, b_im, c_re, c_im):
    ar, ai = a_re, a_im
    dt = jnp.exp(log_dt)[:, None]
    mag = jnp.exp(ar * dt)
    lb_re = mag * jnp.cos(ai * dt)
    lb_im = mag * jnp.sin(ai * dt)
    den = ar * ar + ai * ai
    nr = lb_re - 1.0
    coef_re = (nr * ar + lb_im * ai) / den
    coef_im = (lb_im * ar - nr * ai) / den
    bb_re = coef_re[..., None] * b_re - coef_im[..., None] * b_im
    bb_im = coef_re[..., None] * b_im + coef_im[..., None] * b_re
    gk = SSM_GROUPS // SSM_KB

    def diag_blocks(w):
        return jnp.stack([_block_diag(w[k * gk:(k + 1) * gk]) for k in range(SSM_KB)]).astype(bf16)

    bbre = diag_blocks(bb_re.transpose(0, 2, 1))
    bbim = diag_blocks(bb_im.transpose(0, 2, 1))
    ccre = diag_blocks(c_re.transpose(0, 2, 1))
    ccim = diag_blocks(c_im.transpose(0, 2, 1))
    return (lb_re.reshape(1, SSM_LANES), lb_im.reshape(1, SSM_LANES), bbre, bbim, ccre, ccim)


def kernel(x_prompt, x_sample, state_ret, state_ssm_re, state_ssm_im, state_lru, state_conv, w_in, ret_gn_g, ret_gn_b, ssm_a_re, ssm_a_im, ssm_log_dt, ssm_b_re, ssm_b_im, ssm_c_re, ssm_c_im, ssm_d, ssm_w_glu, lru_conv_w, lru_conv_b, lru_wa, lru_ba, lru_wx, lru_bx, lru_lambda, w_branch, w_out, ln1_g, ln1_b, moe_w_group, moe_b_group, moe_w_expert, moe_b_expert, moe_w1, moe_w3, moe_w2, ln2_g, ln2_b):
    x, xb = _to_rows(x_prompt, x_sample)
    rope = _rope_tables()
    tabs_p = _ret_tables(RET_SUB_T)
    tabs_s = _ret_tables(DEC_SEQ)
    row = lambda v: v.reshape(1, -1)

    outs = [[] for _ in range(10)]
    for l in range(DEPTH):
        w_bf = w_in[l].astype(bf16)

        s0 = state_ret[l].reshape(DEC_BATCH, 2, 2 * RET_DK, RET_DV)
        y_ret, ret_p, ret_s = _retention(xb, w_bf, rope, s0, tabs_p, tabs_s,
                                         row(ret_gn_g[l]), row(ret_gn_b[l]))

        sp = _ssm_params(ssm_a_re[l], ssm_a_im[l], ssm_log_dt[l], ssm_b_re[l], ssm_b_im[l],
                         ssm_c_re[l], ssm_c_im[l])
        y_ssm, re_p, im_p, re_s, im_s = _ssm(
            xb, w_bf, state_ssm_re[l].reshape(DEC_BATCH, SSM_LANES), state_ssm_im[l].reshape(DEC_BATCH, SSM_LANES),
            *sp, row(ssm_d[l]), ssm_w_glu[l].astype(bf16))

        conv0 = state_conv[l].transpose(1, 0, 2).reshape(LRU_HIST_S, BRANCH_W)
        y_lru, lru_p, conv_p, lru_s, conv_s = _lru(
            xb, w_bf, state_lru[l], conv0, lru_conv_w[l], row(lru_conv_b[l]),
            _block_diag(lru_wa[l]).astype(bf16), row(lru_ba[l]),
            _block_diag(lru_wx[l]).astype(bf16), row(lru_bx[l]), row(lru_lambda[l]))

        wr = jnp.zeros((D_MODEL, ROUTE_LANES), f32)
        wr = wr.at[:, 0:MOE_GROUPS].set(moe_w_group[l]).at[:, MOE_GROUPS:MOE_GROUPS + MOE_EXPERTS].set(moe_w_expert[l])
        br = jnp.zeros((1, ROUTE_LANES), f32)
        br = br.at[0, 0:MOE_GROUPS].set(moe_b_group[l]).at[0, MOE_GROUPS:MOE_GROUPS + MOE_EXPERTS].set(moe_b_expert[l])
        x1, route, cnt = _merge(y_ret, y_ssm, y_lru, xb, w_bf, x, w_branch[l].astype(bf16), w_out[l].astype(bf16),
                                row(ln1_g[l]), row(ln1_b[l]), wr.astype(bf16), br)

        pos, plan = _dispatch_plan(route, cnt)
        xs = _dispatch(pos, x1)
        ys = _moe(plan, xs, moe_w1, moe_w3, moe_w2, l)
        x, xb = _combine(pos, ys, x1, route, row(ln2_g[l]), row(ln2_b[l]), final=(l == DEPTH - 1))

        outs[0].append(ret_p.reshape(BATCH, RET_HEADS, RET_DK, RET_DV))
        outs[1].append(re_p.reshape(BATCH, SSM_GROUPS, SSM_STATE))
        outs[2].append(im_p.reshape(BATCH, SSM_GROUPS, SSM_STATE))
        outs[3].append(lru_p)
        outs[4].append(conv_p.reshape(CONV_W - 1, BATCH, BRANCH_W).transpose(1, 0, 2))
        outs[5].append(ret_s.reshape(DEC_BATCH, RET_HEADS, RET_DK, RET_DV))
        outs[6].append(re_s.reshape(DEC_BATCH, SSM_GROUPS, SSM_STATE))
        outs[7].append(im_s.reshape(DEC_BATCH, SSM_GROUPS, SSM_STATE))
        outs[8].append(lru_s)
        outs[9].append(conv_s.reshape(CONV_W - 1, DEC_BATCH, BRANCH_W).transpose(1, 0, 2))

    y_prompt, y_sample = x, xb
    return (y_prompt, y_sample) + tuple(jnp.stack(o) for o in outs)
```

```python
import functools

import jax
import jax.numpy as jnp
import numpy as np
from jax import lax
from jax.experimental import pallas as pl
from jax.experimental.pallas import tpu as pltpu

f32 = jnp.float32
bf16 = jnp.bfloat16

D_MODEL = 1024
BATCH = 8
SEQ = 2048
DEPTH = 2
DEC_BATCH = 128
DEC_SEQ = 8
PAST_LEN = 16384
BRANCH_W = 512
N_BRANCH = 3
RET_HEADS = 4
RET_DK = 64
RET_DV = 128
ROPE_BASE = 10000.0
SSM_GROUP = 16
SSM_GROUPS = 32
SSM_STATE = 64
SSM_LANES = SSM_GROUPS * SSM_STATE
LRU_BLOCKS = 8
LRU_BW = 64
CONV_W = 4
LRU_C = 8.0
MOE_GROUPS = 4
MOE_PER_GROUP = 8
MOE_EXPERTS = 32
MOE_TOPK = 2
MOE_HIDDEN = 512
DN_ALPHA = (2.0 * DEPTH) ** 0.25
LN_EPS = 1e-5
D_IN = 6144

V7X_SUBLANES = 8
V7X_LANES = 128
V7X_VMEM_LIMIT = 56 * 1024 * 1024

N_P = BATCH * SEQ
N_S = DEC_BATCH * DEC_SEQ
N_TOK = N_P + N_S
ROW_TILE = 1024
P_TILES = N_P // ROW_TILE
N_TILES = N_TOK // ROW_TILE
P_TC = ROW_TILE // BATCH
RET_SUB_T = 32
RET_SUB_R = RET_SUB_T * BATCH
RET_SUBS = ROW_TILE // RET_SUB_R
S_BLOCKS = DEC_BATCH // BATCH
S_BLOCK_R = DEC_SEQ * BATCH
MERGE_TILE = 512
MOE_TM = 256
N_PAIRS = N_TOK * MOE_TOPK
MOE_MAX_ITEMS = N_PAIRS // MOE_TM + MOE_EXPERTS - 1
DSP_TILE = 512
CMB_TILE = 256
CMB_P_STEPS = N_P // CMB_TILE
ROUTE_LANES = 128


PROJ_RC = 256
RET_COLS, RET_BLK = 1536, 0
SSM_COLS, SSM_BLK = 512, 3
LRU_COLS, LRU_BLK = 1024, 2
GATE_COLS, GATE_BLK = 3072, 1


def _cparams(sem):
    return pltpu.CompilerParams(dimension_semantics=sem, vmem_limit_bytes=V7X_VMEM_LIMIT)


N_PLANES = D_MODEL // V7X_LANES


def _to_planes(ref, rows):
    for c in range(N_PLANES):
        ref[c] = rows[:, c * V7X_LANES:(c + 1) * V7X_LANES]


def _from_planes(ref):
    return jnp.concatenate([ref[c] for c in range(N_PLANES)], axis=1)


def _const_spec(block_shape, index):
    return pl.BlockSpec(block_shape, lambda *_: index, pipeline_mode=pl.Buffered(1))


def _dot(a, b):
    return jnp.dot(a, b, preferred_element_type=f32)


def _dot_nt(a, b):
    return lax.dot_general(a, b, (((1,), (1,)), ((), ())), preferred_element_type=f32)


def _dot_tn(a, b):
    return lax.dot_general(a, b, (((0,), (0,)), ((), ())), preferred_element_type=f32)


def _layer_norm_rows(x, g, b):
    mu = jnp.mean(x, -1, keepdims=True)
    xc = x - mu
    var = jnp.mean(xc * xc, -1, keepdims=True)
    return xc * lax.rsqrt(var + LN_EPS) * g + b


def _project_rows(xb_ref, w_ref, z_ref):
    for rc in range(ROW_TILE // PROJ_RC):
        rs = slice(rc * PROJ_RC, (rc + 1) * PROJ_RC)
        z_ref[rs, :] = _dot(xb_ref[rs, :], w_ref[...])


def _ret_block(q, k, v, g, cosb, sinb, mask_ref, qdec_ref, kdec_ref, cdec_ref, scat_ref, gng, gnb):
    rows = q.shape[0]
    lane_qk = lax.broadcasted_iota(jnp.int32, (rows, 2 * V7X_LANES), 1)
    first_half = (lane_qk & (RET_DK - 1)) < (RET_DK // 2)

    def rope(x):
        partner = jnp.where(first_half, pltpu.roll(x, 2 * V7X_LANES - RET_DK // 2, 1),
                            pltpu.roll(x, RET_DK // 2, 1))
        return x * cosb + partner * sinb

    q = rope(q)
    k = rope(k) * (RET_DK ** -0.5)
    kd = k * kdec_ref[...]
    lane = lax.broadcasted_iota(jnp.int32, (rows, V7X_LANES), 1)
    row_b = lax.broadcasted_iota(jnp.int32, (rows, V7X_LANES), 0) & (BATCH - 1)
    outs = []
    for p in range(2):
        qp = q[:, p * V7X_LANES:(p + 1) * V7X_LANES]
        kp = k[:, p * V7X_LANES:(p + 1) * V7X_LANES].astype(bf16)
        kdp = kd[:, p * V7X_LANES:(p + 1) * V7X_LANES]
        s_old = scat_ref[p]
        s_bf = s_old.astype(bf16)
        s_new = s_old * jnp.concatenate([cdec_ref[p]] * BATCH, axis=1)
        for hh in range(2):
            h = 2 * p + hh
            head_lanes = (lane >= RET_DK) if hh else (lane < RET_DK)
            qh = jnp.where(head_lanes, qp, 0.0).astype(bf16)
            kdh = jnp.where(head_lanes, kdp, 0.0).astype(bf16)
            vh = v[:, h * RET_DV:(h + 1) * RET_DV]
            vh_bf = vh.astype(bf16)
            sc = _dot_nt(qh, kp) * mask_ref[h]
            o = _dot(sc.astype(bf16), vh_bf)
            cross = _dot(qh, s_bf)
            oc = jnp.zeros((rows, RET_DV), f32)
            for b in range(BATCH):
                oc = oc + jnp.where(row_b == b, cross[:, b * RET_DV:(b + 1) * RET_DV], 0.0)
            o = o + oc * qdec_ref[h]
            vcat = jnp.concatenate([jnp.where(row_b == b, vh_bf, jnp.zeros_like(vh_bf))
                                    for b in range(BATCH)], axis=1)
            s_new = s_new + _dot_tn(kdh, vcat)
            mu = jnp.mean(o, -1, keepdims=True)
            oc2 = o - mu
            var = jnp.mean(oc2 * oc2, -1, keepdims=True)
            outs.append(oc2 * lax.rsqrt(var + LN_EPS))
        scat_ref[p] = s_new
    o = jnp.concatenate(outs, axis=1) * gng + gnb
    return jax.nn.silu(g) * o


def _per_step_rows(tab_ref, t0, steps):
    return jnp.concatenate(
        [jnp.broadcast_to(tab_ref[t0 + t:t0 + t + 1, :], (BATCH, tab_ref.shape[1])) for t in range(steps)], axis=0)


def _ret_kernel(xb_ref, w_ref, cos_ref, sin_ref, cos_s_ref, sin_s_ref, s0_ref,
                mask_p_ref, qdec_p_ref, kdec_p_ref, cdec_p_ref,
                mask_s_ref, qdec_s_ref, kdec_s_ref, cdec_s_ref,
                gng_ref, gnb_ref,
                y_ref, retp_ref, rets_ref, scat_ref, z_ref):
    i = pl.program_id(0)
    gng = gng_ref[...]
    gnb = gnb_ref[...]
    q_cols, k_cols = slice(0, 256), slice(256, 512)
    v_cols, g_cols = slice(512, 1024), slice(1024, 1536)

    @pl.when(i == 0)
    def _():
        scat_ref[...] = jnp.zeros_like(scat_ref)

    @pl.when(i <= P_TILES)
    def _():
        _project_rows(xb_ref, w_ref, z_ref)

    @pl.when(i < P_TILES)
    def _():
        for sc in range(RET_SUBS):
            r0 = sc * RET_SUB_R
            rs = slice(r0, r0 + RET_SUB_R)
            y_ref[rs, :] = _ret_block(
                z_ref[rs, q_cols], z_ref[rs, k_cols], z_ref[rs, v_cols], z_ref[rs, g_cols],
                _per_step_rows(cos_ref, sc * RET_SUB_T, RET_SUB_T), _per_step_rows(sin_ref, sc * RET_SUB_T, RET_SUB_T),
                mask_p_ref, qdec_p_ref, kdec_p_ref, cdec_p_ref, scat_ref, gng, gnb)

    @pl.when(i == P_TILES - 1)
    def _():
        for b in range(BATCH):
            for p in range(2):
                retp_ref[b, p] = scat_ref[p, :, b * RET_DV:(b + 1) * RET_DV]

    @pl.when(i >= P_TILES)
    def _():
        bb = i - P_TILES
        for b in range(BATCH):
            for p in range(2):
                scat_ref[p, :, b * RET_DV:(b + 1) * RET_DV] = s0_ref[0, b, p]

        def rows_of(cols):
            return jnp.concatenate(
                [z_ref[pl.ds(pl.multiple_of(t * DEC_BATCH + bb * BATCH, BATCH), BATCH), cols]
                 for t in range(DEC_SEQ)], axis=0)

        y = _ret_block(
            rows_of(q_cols), rows_of(k_cols), rows_of(v_cols),
            rows_of(g_cols), _per_step_rows(cos_s_ref, 0, DEC_SEQ), _per_step_rows(sin_s_ref, 0, DEC_SEQ),
            mask_s_ref, qdec_s_ref, kdec_s_ref, cdec_s_ref, scat_ref, gng, gnb)
        for t in range(DEC_SEQ):
            y_ref[pl.ds(pl.multiple_of(t * DEC_BATCH + bb * BATCH, BATCH), BATCH), :] = (
                y[t * BATCH:(t + 1) * BATCH, :])
        for b in range(BATCH):
            for p in range(2):
                rets_ref[b, p] = scat_ref[p, :, b * RET_DV:(b + 1) * RET_DV]


def _ret_tables(tc):
    rows = tc * BATCH
    nf = np.float32
    log_g = np.log1p(-np.exp2(nf(-5.0) - np.arange(RET_HEADS, dtype=nf))).astype(nf)
    t_idx = (np.arange(rows) // BATCH).astype(nf)
    b_idx = np.arange(rows) % BATCH
    rel = t_idx[:, None] - t_idx[None, :]
    same = b_idx[:, None] == b_idx[None, :]
    decay = np.exp(log_g[:, None, None] * np.maximum(rel, nf(0.0)))
    mask = np.where((rel >= 0) & same, decay, nf(0.0))
    qdec = np.exp(log_g[:, None] * (t_idx[None, :] + nf(1.0)))
    qdec = np.broadcast_to(qdec[:, :, None], (RET_HEADS, rows, RET_DV))
    kdec = np.exp(log_g[:, None] * (nf(tc - 1.0) - t_idx[None, :]))
    kdec = np.broadcast_to(kdec.T[:, :, None], (rows, RET_HEADS, RET_DK)).reshape(rows, RET_HEADS * RET_DK)
    cdec = np.exp(log_g * nf(tc))
    cdec = np.broadcast_to(cdec[:, None, None], (RET_HEADS, RET_DK, RET_DV)).reshape(2, 2 * RET_DK, RET_DV)
    return tuple(jnp.asarray(np.ascontiguousarray(a), dtype=f32) for a in (mask, qdec, kdec, cdec))


def _rope_tables():
    half = RET_DK // 2
    inv = ROPE_BASE ** (-jnp.arange(half, dtype=f32) / half)
    pos_p = jnp.arange(SEQ, dtype=f32)
    pos_s = PAST_LEN + jnp.arange(DEC_SEQ, dtype=f32)

    def tab(pos):
        ang = pos[:, None] * inv[None, :]
        cos = jnp.cos(ang)
        sin = jnp.sin(ang)
        cos_h = jnp.concatenate([cos, cos], axis=1)
        sin_h = jnp.concatenate([-sin, sin], axis=1)
        return jnp.tile(cos_h, (1, RET_HEADS)), jnp.tile(sin_h, (1, RET_HEADS))

    return tab(pos_p) + tab(pos_s)


def _retention(xb, w_bf, rope, s0_s, layer, tabs_p, tabs_s, gng, gnb):
    n_steps = P_TILES + S_BLOCKS
    tile = lambda i: jnp.minimum(i, P_TILES)
    sblk = lambda i: jnp.maximum(i - P_TILES, 0)
    full = lambda a: pl.BlockSpec(a.shape, lambda i, _n=a.ndim: (0,) * _n)
    state_blk = (BATCH, 2, 2 * RET_DK, RET_DV)
    cos_p, sin_p, cos_s, sin_s = rope
    ptile = lambda i: jnp.minimum(i, P_TILES - 1)
    ins = [xb, w_bf, cos_p, sin_p, cos_s, sin_s, s0_s, *tabs_p, *tabs_s, gng, gnb]
    in_specs = [
        pl.BlockSpec((ROW_TILE, D_MODEL), lambda i: (tile(i), 0)),
        _const_spec((D_MODEL, RET_COLS), (0, RET_BLK)),
        pl.BlockSpec((P_TC, 256), lambda i: (ptile(i), 0)),
        pl.BlockSpec((P_TC, 256), lambda i: (ptile(i), 0)),
        full(cos_s), full(sin_s),
        pl.BlockSpec((1,) + state_blk, lambda i: (layer, sblk(i), 0, 0, 0)),
    ] + [full(a) for a in (*tabs_p, *tabs_s, gng, gnb)]
    return pl.pallas_call(
        _ret_kernel,
        grid=(n_steps,),
        in_specs=in_specs,
        out_specs=[pl.BlockSpec((ROW_TILE, BRANCH_W), lambda i: (tile(i), 0)),
                   pl.BlockSpec(state_blk, lambda i: (0, 0, 0, 0)),
                   pl.BlockSpec(state_blk, lambda i: (sblk(i), 0, 0, 0))],
        out_shape=[jax.ShapeDtypeStruct((N_TOK, BRANCH_W), f32),
                   jax.ShapeDtypeStruct((BATCH, 2, 2 * RET_DK, RET_DV), f32),
                   jax.ShapeDtypeStruct((DEC_BATCH, 2, 2 * RET_DK, RET_DV), f32)],
        scratch_shapes=[pltpu.VMEM((2, 2 * RET_DK, BATCH * RET_DV), f32),
                        pltpu.VMEM((ROW_TILE, RET_COLS), f32)],
        compiler_params=_cparams(("arbitrary",)),
        name="retention",
    )(*ins)


SSM_LB = 512
SSM_RC = 256
SSM_KB = 2
SSM_KB_U = BRANCH_W // SSM_KB
SSM_KB_H = SSM_LANES // SSM_KB


def _ssm_scan(bre_ref, bim_ref, lre_ref, lim_ref, h_re0, h_im0, row0, nb_rows, steps, lb, unroll):
    ls = slice(lb * SSM_LB, (lb + 1) * SSM_LB)
    a_re = jnp.broadcast_to(lre_ref[:, ls], (V7X_SUBLANES, SSM_LB))
    a_im = jnp.broadcast_to(lim_ref[:, ls], (V7X_SUBLANES, SSM_LB))

    def step(t, carry):
        h_re, h_im = carry
        r = pl.multiple_of(row0 + t * nb_rows, V7X_SUBLANES)
        n_re = a_re * h_re - a_im * h_im + bre_ref[pl.ds(r, V7X_SUBLANES), ls]
        n_im = a_re * h_im + a_im * h_re + bim_ref[pl.ds(r, V7X_SUBLANES), ls]
        bre_ref[pl.ds(r, V7X_SUBLANES), ls] = n_re
        bim_ref[pl.ds(r, V7X_SUBLANES), ls] = n_im
        return n_re, n_im

    return lax.fori_loop(0, steps, step, (h_re0, h_im0), unroll=unroll)


def _ssm_kernel(xb_ref, w_ref, h0re_ref, h0im_ref, lre_ref, lim_ref, bbre_ref, bbim_ref, ccre_ref, ccim_ref,
                d_ref, wglu_ref,
                y_ref, pre_ref, pim_ref, sre_ref, sim_ref,
                bre_ref, bim_ref, hre_ref, him_ref, zs_ref):
    i = pl.program_id(0)
    _project_rows(xb_ref, w_ref, zs_ref)
    for rc in range(ROW_TILE // SSM_RC):
        rs = slice(rc * SSM_RC, (rc + 1) * SSM_RC)
        ub = zs_ref[rs, :].astype(bf16)
        for k in range(SSM_KB):
            uk = ub[:, k * SSM_KB_U:(k + 1) * SSM_KB_U]
            hs = slice(k * SSM_KB_H, (k + 1) * SSM_KB_H)
            bre_ref[rs, hs] = _dot(uk, bbre_ref[k])
            bim_ref[rs, hs] = _dot(uk, bbim_ref[k])

    @pl.when(i == 0)
    def _():
        hre_ref[...] = jnp.zeros_like(hre_ref)
        him_ref[...] = jnp.zeros_like(him_ref)

    @pl.when(i < P_TILES)
    def _():
        for lb in range(SSM_LANES // SSM_LB):
            ls = slice(lb * SSM_LB, (lb + 1) * SSM_LB)
            h_re, h_im = _ssm_scan(bre_ref, bim_ref, lre_ref, lim_ref, hre_ref[:, ls], him_ref[:, ls],
                                   0, BATCH, P_TC, lb, 8)
            hre_ref[:, ls] = h_re
            him_ref[:, ls] = h_im
        pre_ref[...] = hre_ref[...]
        pim_ref[...] = him_ref[...]

    @pl.when(i >= P_TILES)
    def _():
        def per_row_tile(rt, c):
            r0 = pl.multiple_of(rt * V7X_SUBLANES, V7X_SUBLANES)
            for lb in range(SSM_LANES // SSM_LB):
                ls = slice(lb * SSM_LB, (lb + 1) * SSM_LB)
                _ssm_scan(bre_ref, bim_ref, lre_ref, lim_ref,
                          h0re_ref[pl.ds(r0, V7X_SUBLANES), ls], h0im_ref[pl.ds(r0, V7X_SUBLANES), ls],
                          r0, DEC_BATCH, DEC_SEQ, lb, True)
            return c

        lax.fori_loop(0, DEC_BATCH // V7X_SUBLANES, per_row_tile, 0)
        last = (DEC_SEQ - 1) * DEC_BATCH
        sre_ref[...] = bre_ref[last:last + DEC_BATCH, :]
        sim_ref[...] = bim_ref[last:last + DEC_BATCH, :]

    for rc in range(ROW_TILE // SSM_RC):
        rs = slice(rc * SSM_RC, (rc + 1) * SSM_RC)
        ch = []
        for k in range(SSM_KB):
            hs = slice(k * SSM_KB_H, (k + 1) * SSM_KB_H)
            ch.append(_dot(bre_ref[rs, hs].astype(bf16), ccre_ref[k]) - _dot(bim_ref[rs, hs].astype(bf16), ccim_ref[k]))
        y = jnp.concatenate(ch, axis=1) + d_ref[...] * zs_ref[rs, :]
        zz = jax.nn.gelu(y)
        y_ref[rs, :] = zz * jax.nn.sigmoid(_dot(zz.astype(bf16), wglu_ref[...]))


def _ssm(xb, w_bf, h0re, h0im, lre, lim, bbre, bbim, ccre, ccim, dvec, wglu):
    full = lambda a: _const_spec(a.shape, (0,) * a.ndim)
    consts = (h0re, h0im, lre, lim, bbre, bbim, ccre, ccim, dvec, wglu)
    return pl.pallas_call(
        _ssm_kernel,
        grid=(N_TILES,),
        in_specs=[pl.BlockSpec((ROW_TILE, D_MODEL), lambda i: (i, 0)),
                  _const_spec((D_MODEL, SSM_COLS), (0, SSM_BLK))] + [full(a) for a in consts],
        out_specs=[pl.BlockSpec((ROW_TILE, BRANCH_W), lambda i: (i, 0)),
                   pl.BlockSpec((BATCH, SSM_LANES), lambda i: (0, 0)),
                   pl.BlockSpec((BATCH, SSM_LANES), lambda i: (0, 0)),
                   pl.BlockSpec((DEC_BATCH, SSM_LANES), lambda i: (0, 0)),
                   pl.BlockSpec((DEC_BATCH, SSM_LANES), lambda i: (0, 0))],
        out_shape=[jax.ShapeDtypeStruct((N_TOK, BRANCH_W), f32),
                   jax.ShapeDtypeStruct((BATCH, SSM_LANES), f32),
                   jax.ShapeDtypeStruct((BATCH, SSM_LANES), f32),
                   jax.ShapeDtypeStruct((DEC_BATCH, SSM_LANES), f32),
                   jax.ShapeDtypeStruct((DEC_BATCH, SSM_LANES), f32)],
        scratch_shapes=[pltpu.VMEM((ROW_TILE, SSM_LANES), f32), pltpu.VMEM((ROW_TILE, SSM_LANES), f32),
                        pltpu.VMEM((BATCH, SSM_LANES), f32), pltpu.VMEM((BATCH, SSM_LANES), f32),
                        pltpu.VMEM((ROW_TILE, SSM_COLS), f32)],
        compiler_params=_cparams(("arbitrary",)),
        name="ssm",
    )(xb, w_bf, *consts)


LRU_HIST_P = (CONV_W - 1) * BATCH
LRU_HIST_S = (CONV_W - 1) * DEC_BATCH


def _lru_gates(xe_ref, nb_rows, cw_ref, cb_ref, wa_ref, ba_ref, wx_ref, bx_ref, lam_ref):
    xc = cb_ref[...] + xe_ref[0:ROW_TILE, :] * cw_ref[0:1, :]
    for j in range(1, CONV_W):
        xc = xc + xe_ref[j * nb_rows:j * nb_rows + ROW_TILE, :] * cw_ref[j:j + 1, :]
    xcb = xc.astype(bf16)
    r = jax.nn.sigmoid(_dot(xcb, wa_ref[...]) + ba_ref[...])
    ig = jax.nn.sigmoid(_dot(xcb, wx_ref[...]) + bx_ref[...])
    log_a = -LRU_C * r * jax.nn.softplus(-lam_ref[...])
    a = jnp.exp(log_a)
    b = jnp.sqrt(-jnp.tanh(log_a) * (jnp.exp(2.0 * log_a) + 1.0)) * (ig * xc)
    return a, b


def _lru_kernel(xb_ref, w_ref, h0_ref, conv0_ref, cw_ref, cb_ref, wa_ref, ba_ref, wx_ref, bx_ref, lam_ref,
                y_ref, hp_ref, convp_ref, hs_ref, convs_ref,
                xe_ref, a_ref, b_ref, hc_ref, z_ref):
    i = pl.program_id(0)
    params = (cw_ref, cb_ref, wa_ref, ba_ref, wx_ref, bx_ref, lam_ref)
    _project_rows(xb_ref, w_ref, z_ref)
    zx_ref = z_ref.at[:, 0:BRANCH_W]
    zg_ref = z_ref.at[:, BRANCH_W:2 * BRANCH_W]

    @pl.when(i == 0)
    def _():
        xe_ref[0:LRU_HIST_P, :] = jnp.zeros((LRU_HIST_P, BRANCH_W), f32)
        hc_ref[...] = jnp.zeros_like(hc_ref)

    @pl.when(i < P_TILES)
    def _():
        xe_ref[LRU_HIST_P:LRU_HIST_P + ROW_TILE, :] = zx_ref[...]
        a, b = _lru_gates(xe_ref, BATCH, *params)
        a_ref[...] = a
        b_ref[...] = b
        hist = xe_ref[ROW_TILE:ROW_TILE + LRU_HIST_P, :]
        xe_ref[0:LRU_HIST_P, :] = hist
        convp_ref[...] = hist

        def step(t, h):
            r = pl.multiple_of(t * BATCH, BATCH)
            h = a_ref[pl.ds(r, BATCH), :] * h + b_ref[pl.ds(r, BATCH), :]
            b_ref[pl.ds(r, BATCH), :] = h
            return h

        h = lax.fori_loop(0, P_TC, step, hc_ref[...], unroll=8)
        hc_ref[...] = h
        hp_ref[...] = h

    @pl.when(i >= P_TILES)
    def _():
        xe_ref[0:LRU_HIST_S, :] = conv0_ref[...]
        xe_ref[LRU_HIST_S:LRU_HIST_S + ROW_TILE, :] = zx_ref[...]
        a, b = _lru_gates(xe_ref, DEC_BATCH, *params)
        a_ref[...] = a
        b_ref[...] = b
        convs_ref[...] = xe_ref[ROW_TILE:ROW_TILE + LRU_HIST_S, :]

        def per_row_tile(rt, c):
            r0 = pl.multiple_of(rt * V7X_SUBLANES, V7X_SUBLANES)
            h = h0_ref[pl.ds(r0, V7X_SUBLANES), :]
            for t in range(DEC_SEQ):
                r = pl.multiple_of(t * DEC_BATCH + r0, V7X_SUBLANES)
                h = a_ref[pl.ds(r, V7X_SUBLANES), :] * h + b_ref[pl.ds(r, V7X_SUBLANES), :]
                b_ref[pl.ds(r, V7X_SUBLANES), :] = h
            return c

        lax.fori_loop(0, DEC_BATCH // V7X_SUBLANES, per_row_tile, 0)
        last = (DEC_SEQ - 1) * DEC_BATCH
        hs_ref[...] = b_ref[last:last + DEC_BATCH, :]

    y_ref[...] = b_ref[...] * jax.nn.gelu(zg_ref[...])


def _lru(xb, w_bf, h0, conv0, cw, cb, wa, ba, wx, bx, lam):
    full = lambda a: _const_spec(a.shape, (0,) * a.ndim)
    consts = (h0, conv0, cw, cb, wa, ba, wx, bx, lam)
    return pl.pallas_call(
        _lru_kernel,
        grid=(N_TILES,),
        in_specs=[pl.BlockSpec((ROW_TILE, D_MODEL), lambda i: (i, 0)),
                  _const_spec((D_MODEL, LRU_COLS), (0, LRU_BLK))] + [full(a) for a in consts],
        out_specs=[pl.BlockSpec((ROW_TILE, BRANCH_W), lambda i: (i, 0)),
                   pl.BlockSpec((BATCH, BRANCH_W), lambda i: (0, 0)),
                   pl.BlockSpec((LRU_HIST_P, BRANCH_W), lambda i: (0, 0)),
                   pl.BlockSpec((DEC_BATCH, BRANCH_W), lambda i: (0, 0)),
                   pl.BlockSpec((LRU_HIST_S, BRANCH_W), lambda i: (0, 0))],
        out_shape=[jax.ShapeDtypeStruct((N_TOK, BRANCH_W), f32),
                   jax.ShapeDtypeStruct((BATCH, BRANCH_W), f32),
                   jax.ShapeDtypeStruct((LRU_HIST_P, BRANCH_W), f32),
                   jax.ShapeDtypeStruct((DEC_BATCH, BRANCH_W), f32),
                   jax.ShapeDtypeStruct((LRU_HIST_S, BRANCH_W), f32)],
        scratch_shapes=[pltpu.VMEM((ROW_TILE + LRU_HIST_S, BRANCH_W), f32),
                        pltpu.VMEM((ROW_TILE, BRANCH_W), f32), pltpu.VMEM((ROW_TILE, BRANCH_W), f32),
                        pltpu.VMEM((BATCH, BRANCH_W), f32),
                        pltpu.VMEM((ROW_TILE, LRU_COLS), f32)],
        compiler_params=_cparams(("arbitrary",)),
        name="lru",
    )(xb, w_bf, *consts)


def _merge_kernel(yr_ref, ys_ref, yl_ref, xb_ref, wg_ref, x_ref, wb_ref, wo_ref, g_ref, b_ref, wr_ref, br_ref,
                  x1_ref, route_ref, cnt_out_ref, cnt_ref):
    merged = jnp.zeros((MERGE_TILE, D_MODEL), f32)
    xb = xb_ref[...]
    for n, y_ref in enumerate((yr_ref, ys_ref, yl_ref)):
        proj = _dot(y_ref[...].astype(bf16), wb_ref[n])
        gate = jax.nn.sigmoid(_dot(xb, wg_ref[:, n * D_MODEL:(n + 1) * D_MODEL]))
        merged = merged + gate * proj
    mix = _dot(merged.astype(bf16), wo_ref[...])
    x1 = _layer_norm_rows(DN_ALPHA * x_ref[...] + mix, g_ref[...], b_ref[...])
    _to_planes(x1_ref, x1)

    logits = _dot(x1.astype(bf16), wr_ref[...]) + br_ref[...]
    lane = lax.broadcasted_iota(jnp.int32, (MERGE_TILE, ROUTE_LANES), 1).astype(f32)
    big = jnp.float32(ROUTE_LANES)
    neg = jnp.float32(-jnp.inf)
    is_g = lane < MOE_GROUPS
    lg = jnp.where(is_g, logits, neg)
    mg = jnp.max(lg, -1, keepdims=True)
    gsel = jnp.min(jnp.where(lg == mg, lane, big), -1, keepdims=True)
    sum_g = jnp.sum(jnp.where(is_g, jnp.exp(lg - mg), 0.0), -1, keepdims=True)
    pg_sel = 1.0 / sum_g
    lo = MOE_GROUPS + gsel * MOE_PER_GROUP
    is_e = jnp.abs(lane - lo - 0.5 * (MOE_PER_GROUP - 1)) < 0.5 * MOE_PER_GROUP
    le = jnp.where(is_e, logits, neg)
    me = jnp.max(le, -1, keepdims=True)
    ex = jnp.where(is_e, jnp.exp(le - me), 0.0)
    pe = jnp.where(is_e, ex / jnp.sum(ex, -1, keepdims=True), -1.0)
    v1 = jnp.max(pe, -1, keepdims=True)
    i1 = jnp.min(jnp.where(pe == v1, lane, big), -1, keepdims=True)
    pe2 = jnp.where(lane == i1, -1.0, pe)
    v2 = jnp.max(pe2, -1, keepdims=True)
    i2 = jnp.min(jnp.where(pe2 == v2, lane, big), -1, keepdims=True)
    vsum = v1 + v2
    w1 = pg_sel * v1 / vsum
    w2 = pg_sel * v2 / vsum
    e1 = i1 - MOE_GROUPS
    e2 = i2 - MOE_GROUPS

    @pl.when(pl.program_id(0) == 0)
    def _():
        cnt_ref[...] = jnp.zeros_like(cnt_ref)

    oh1 = lane == e1
    oh2 = lane == e2
    ohs = jnp.where(oh1, 1.0, jnp.where(oh2, 1.0, 0.0))
    r_i = lax.broadcasted_iota(jnp.int32, (MERGE_TILE, MERGE_TILE), 0)
    c_i = lax.broadcasted_iota(jnp.int32, (MERGE_TILE, MERGE_TILE), 1)
    strict_lower = jnp.where(c_i < r_i, 1.0, 0.0).astype(bf16)
    before = _dot(strict_lower, ohs.astype(bf16)) + cnt_ref[0:1, :]
    rank1 = jnp.sum(jnp.where(oh1, before, 0.0), -1, keepdims=True)
    rank2 = jnp.sum(jnp.where(oh2, before, 0.0), -1, keepdims=True)
    cnt_ref[0:1, :] = cnt_ref[0:1, :] + jnp.sum(ohs, 0, keepdims=True)
    cnt_out_ref[...] = cnt_ref[...]

    route = jnp.zeros((MERGE_TILE, ROUTE_LANES), f32)
    for k, val in enumerate((e1, e2, w1, w2, rank1, rank2)):
        route = jnp.where(lane == k, val, route)
    route_ref[...] = route


def _merge(y_ret, y_ssm, y_lru, xb, w_bf, x, wb, wo, g, b, wr, br):
    full = lambda a: _const_spec(a.shape, (0,) * a.ndim)
    row = lambda w: pl.BlockSpec((MERGE_TILE, w), lambda i: (i, 0))
    consts = (wb, wo, g, b, wr, br)
    return pl.pallas_call(
        _merge_kernel,
        grid=(N_TOK // MERGE_TILE,),
        in_specs=[row(BRANCH_W), row(BRANCH_W), row(BRANCH_W), row(D_MODEL),
                  _const_spec((D_MODEL, GATE_COLS), (0, GATE_BLK)),
                  row(D_MODEL)] + [full(a) for a in consts],
        out_specs=[pl.BlockSpec((N_PLANES, MERGE_TILE, V7X_LANES), lambda i: (0, i, 0)), row(ROUTE_LANES),
                   pl.BlockSpec((V7X_SUBLANES, ROUTE_LANES), lambda i: (0, 0))],
        out_shape=[jax.ShapeDtypeStruct((N_PLANES, N_TOK, V7X_LANES), f32),
                   jax.ShapeDtypeStruct((N_TOK, ROUTE_LANES), f32),
                   jax.ShapeDtypeStruct((V7X_SUBLANES, ROUTE_LANES), f32)],
        scratch_shapes=[pltpu.VMEM((V7X_SUBLANES, ROUTE_LANES), f32)],
        compiler_params=_cparams(("arbitrary",)),
        name="merge",
    )(y_ret, y_ssm, y_lru, xb, w_bf, x, *consts)


def _dispatch_kernel(pos_ref, x1_ref, xs_hbm, sem):
    base = pl.program_id(0) * DSP_TILE

    def row_copy(r, dst_row):
        return pltpu.make_async_copy(x1_ref.at[:, r, :], xs_hbm.at[dst_row], sem.at[0])

    def issue(r, c):
        n = base + r
        for k in range(MOE_TOPK):
            row_copy(r, pos_ref[2 * n + k]).start(priority=k % 2)
        return c

    lax.fori_loop(0, DSP_TILE, issue, 0, unroll=8)
    for _ in range(MOE_TOPK):
        pltpu.make_async_copy(x1_ref, x1_ref, sem.at[0]).wait()


def _dispatch(pos, x1p):
    grid_spec = pltpu.PrefetchScalarGridSpec(
        num_scalar_prefetch=1,
        grid=(N_TOK // DSP_TILE,),
        in_specs=[pl.BlockSpec((N_PLANES, DSP_TILE, V7X_LANES), lambda s, pos: (0, s, 0))],
        out_specs=pl.BlockSpec(memory_space=pl.ANY),
        scratch_shapes=[pltpu.SemaphoreType.DMA((1,))],
    )
    return pl.pallas_call(
        _dispatch_kernel,
        grid_spec=grid_spec,
        out_shape=jax.ShapeDtypeStruct((N_PAIRS, N_PLANES, V7X_LANES), f32),
        compiler_params=_cparams(("arbitrary",)),
        name="dispatch",
    )(pos, x1p)


def _moe_kernel(wt_ref, we_ref, wlo_ref, whi_ref, wfirst_ref, wlast_ref, nw_ref,
                xs_hbm, w1_ref, w3_ref, w2_ref, ys_hbm,
                xin, yout, sem_in, sem_out, w1b, w3b, w2b):
    w = pl.program_id(0)
    n_items = nw_ref[0]

    def in_copy(item, slot, c):
        r0 = pl.multiple_of(wt_ref[item] * MOE_TM, MOE_TM)
        return pltpu.make_async_copy(xs_hbm.at[pl.ds(r0, MOE_TM), c, :], xin.at[slot, c], sem_in.at[slot])

    def out_copy(tile, slot, c):
        r0 = pl.multiple_of(tile * MOE_TM, MOE_TM)
        return pltpu.make_async_copy(yout.at[slot, c], ys_hbm.at[pl.ds(r0, MOE_TM), c, :], sem_out.at[slot])

    @pl.when(w == 0)
    def _():
        for c in range(N_PLANES):
            in_copy(0, 0, c).start(priority=1)

    @pl.when(w + 1 < n_items)
    def _():
        for c in range(N_PLANES):
            in_copy(w + 1, (w + 1) % 2, c).start(priority=1)

    @pl.when(w < n_items)
    def _():
        slot = w % 2
        tile = wt_ref[w]
        oslot = tile % 2
        for c in range(N_PLANES):
            in_copy(w, slot, c).wait()
        prev = we_ref[jnp.maximum(w - 1, 0)]

        @pl.when((w == 0) | (we_ref[w] != prev))
        def _():
            w1b[...] = w1_ref[0, 0].astype(bf16)
            w3b[...] = w3_ref[0, 0].astype(bf16)
            w2b[...] = w2_ref[0, 0].astype(bf16)

        xt = _from_planes(xin.at[slot]).astype(bf16)
        h = jax.nn.silu(_dot(xt, w1b[...])) * _dot(xt, w3b[...])
        res = _dot(h.astype(bf16), w2b[...])
        row = lax.broadcasted_iota(jnp.int32, (MOE_TM, D_MODEL), 0)
        mine = jnp.where(row >= wlo_ref[w], row, MOE_TM) < whi_ref[w]

        @pl.when(wfirst_ref[w] == 1)
        def _():
            @pl.when(tile >= 2)
            def _():
                for c in range(N_PLANES):
                    out_copy(0, oslot, c).wait()

            _to_planes(yout.at[oslot], jnp.where(mine, res, 0.0))

        @pl.when(wfirst_ref[w] == 0)
        def _():
            _to_planes(yout.at[oslot], jnp.where(mine, res, _from_planes(yout.at[oslot])))

        @pl.when(wlast_ref[w] == 1)
        def _():
            for c in range(N_PLANES):
                out_copy(tile, oslot, c).start()

        @pl.when(w == n_items - 1)
        def _():
            for c in range(N_PLANES):
                out_copy(0, oslot, c).wait()

            @pl.when(tile >= 1)
            def _():
                for c in range(N_PLANES):
                    out_copy(0, 1 - oslot, c).wait()


def _moe(plan, xs, w1, w3, w2, layer):
    wspec = lambda shp: pl.BlockSpec((1, 1) + shp, lambda w, wt, we, *_: (layer, we[w], 0, 0))
    grid_spec = pltpu.PrefetchScalarGridSpec(
        num_scalar_prefetch=7,
        grid=(MOE_MAX_ITEMS,),
        in_specs=[pl.BlockSpec(memory_space=pl.ANY),
                  wspec((D_MODEL, MOE_HIDDEN)), wspec((D_MODEL, MOE_HIDDEN)), wspec((MOE_HIDDEN, D_MODEL))],
        out_specs=pl.BlockSpec(memory_space=pl.ANY),
        scratch_shapes=[pltpu.VMEM((2, N_PLANES, MOE_TM, V7X_LANES), f32),
                        pltpu.VMEM((2, N_PLANES, MOE_TM, V7X_LANES), f32),
                        pltpu.SemaphoreType.DMA((2,)), pltpu.SemaphoreType.DMA((2,)),
                        pltpu.VMEM((D_MODEL, MOE_HIDDEN), bf16), pltpu.VMEM((D_MODEL, MOE_HIDDEN), bf16),
                        pltpu.VMEM((MOE_HIDDEN, D_MODEL), bf16)],
    )
    return pl.pallas_call(
        _moe_kernel,
        grid_spec=grid_spec,
        out_shape=jax.ShapeDtypeStruct((N_PAIRS, N_PLANES, V7X_LANES), f32),
        compiler_params=_cparams(("arbitrary",)),
        name="moe",
    )(*plan, xs, w1, w3, w2)


def _combine_kernel(final, pos_ref, ys_hbm, x1_ref, route_ref, g_ref, b_ref, *rest):
    if final:
        yp_ref, ysm_ref, buf, sem, t3_ref = rest
    else:
        o_ref, ob_ref, buf, sem = rest
    s = pl.program_id(0)
    nsteps = pl.num_programs(0)

    def gather_copy(row, slot, k, r):
        return pltpu.make_async_copy(ys_hbm.at[row], buf.at[slot, k, :, r, :], sem.at[slot])

    def issue(tile, slot):
        def body(r, c):
            n = tile * CMB_TILE + r
            for k in range(MOE_TOPK):
                gather_copy(pos_ref[2 * n + k], slot, k, r).start(priority=k % 2)
            return c

        lax.fori_loop(0, CMB_TILE, body, 0, unroll=8)

    @pl.when(s == 0)
    def _():
        issue(0, 0)

    @pl.when(s + 1 < nsteps)
    def _():
        issue(s + 1, (s + 1) % 2)

    slot = s % 2
    for k in range(MOE_TOPK):
        pltpu.make_async_copy(buf.at[slot, k], buf.at[slot, k], sem.at[slot]).wait()
    route = route_ref[...]
    moe = route[:, 2:3] * _from_planes(buf.at[slot, 0]) + route[:, 3:4] * _from_planes(buf.at[slot, 1])
    y = _layer_norm_rows(DN_ALPHA * _from_planes(x1_ref) + moe, g_ref[...], b_ref[...])
    if not final:
        o_ref[...] = y
        ob_ref[...] = y.astype(bf16)
        return

    @pl.when(s < CMB_P_STEPS)
    def _():
        t3_ref[...] = y.reshape(CMB_TILE // BATCH, BATCH, D_MODEL)
        for b in range(BATCH):
            yp_ref[b] = t3_ref[:, b, :]

    steps_per_tile = CMB_TILE // DEC_BATCH
    for q in range(N_S // CMB_TILE):
        @pl.when(s == CMB_P_STEPS + q)
        def _():
            for h in range(steps_per_tile):
                ysm_ref[:, q * steps_per_tile + h, :] = y[h * DEC_BATCH:(h + 1) * DEC_BATCH, :]


def _combine(pos, ys, x1, route, g, b, final):
    if final:
        out_specs = [pl.BlockSpec((BATCH, CMB_TILE // BATCH, D_MODEL),
                                  lambda s, pos: (0, jnp.minimum(s, CMB_P_STEPS - 1), 0)),
                     pl.BlockSpec((DEC_BATCH, DEC_SEQ, D_MODEL), lambda s, pos: (0, 0, 0))]
        out_shape = [jax.ShapeDtypeStruct((BATCH, SEQ, D_MODEL), f32),
                     jax.ShapeDtypeStruct((DEC_BATCH, DEC_SEQ, D_MODEL), f32)]
        extra = [pltpu.VMEM((CMB_TILE // BATCH, BATCH, D_MODEL), f32)]
    else:
        out_specs = [pl.BlockSpec((CMB_TILE, D_MODEL), lambda s, pos: (s, 0))] * 2
        out_shape = [jax.ShapeDtypeStruct((N_TOK, D_MODEL), f32), jax.ShapeDtypeStruct((N_TOK, D_MODEL), bf16)]
        extra = []
    grid_spec = pltpu.PrefetchScalarGridSpec(
        num_scalar_prefetch=1,
        grid=(N_TOK // CMB_TILE,),
        in_specs=[pl.BlockSpec(memory_space=pl.ANY),
                  pl.BlockSpec((N_PLANES, CMB_TILE, V7X_LANES), lambda s, pos: (0, s, 0)),
                  pl.BlockSpec((CMB_TILE, ROUTE_LANES), lambda s, pos: (s, 0)),
                  pl.BlockSpec((1, D_MODEL), lambda s, pos: (0, 0)),
                  pl.BlockSpec((1, D_MODEL), lambda s, pos: (0, 0))],
        out_specs=out_specs,
        scratch_shapes=[pltpu.VMEM((2, MOE_TOPK, N_PLANES, CMB_TILE, V7X_LANES), f32),
                        pltpu.SemaphoreType.DMA((2,))] + extra,
    )
    return pl.pallas_call(
        functools.partial(_combine_kernel, final),
        grid_spec=grid_spec,
        out_shape=out_shape,
        compiler_params=_cparams(("arbitrary",)),
        name="combine_out" if final else "combine",
    )(pos, ys, x1, route, g, b)


def _to_rows_kernel(xp_ref, xs_ref, o_ref, ob_ref, t3_ref):
    i = pl.program_id(0)

    @pl.when(i < P_TILES)
    def _():
        for b in range(BATCH):
            t3_ref[:, b, :] = xp_ref[b]
        rows = t3_ref[...].reshape(ROW_TILE, D_MODEL)
        o_ref[...] = rows
        ob_ref[...] = rows.astype(bf16)

    @pl.when(i >= P_TILES)
    def _():
        for t in range(DEC_SEQ):
            rows = xs_ref[:, t, :]
            o_ref[t * DEC_BATCH:(t + 1) * DEC_BATCH, :] = rows
            ob_ref[t * DEC_BATCH:(t + 1) * DEC_BATCH, :] = rows.astype(bf16)


def _to_rows(x_prompt, x_sample):
    return pl.pallas_call(
        _to_rows_kernel,
        grid=(N_TILES,),
        in_specs=[pl.BlockSpec((BATCH, P_TC, D_MODEL), lambda i: (0, jnp.minimum(i, P_TILES - 1), 0)),
                  pl.BlockSpec((DEC_BATCH, DEC_SEQ, D_MODEL), lambda i: (0, 0, 0))],
        out_specs=[pl.BlockSpec((ROW_TILE, D_MODEL), lambda i: (i, 0))] * 2,
        out_shape=[jax.ShapeDtypeStruct((N_TOK, D_MODEL), f32), jax.ShapeDtypeStruct((N_TOK, D_MODEL), bf16)],
        scratch_shapes=[pltpu.VMEM((P_TC, BATCH, D_MODEL), f32)],
        compiler_params=_cparams(("arbitrary",)),
        name="to_rows",
    )(x_prompt, x_sample)


def _lookup(table, idx):
    ar = jnp.arange(MOE_EXPERTS, dtype=jnp.int32)
    return jnp.sum(jnp.where(idx[..., None] == ar, table, 0), axis=-1)


def _dispatch_plan(route, cnt):
    i32 = jnp.int32
    e = route[:, 0:2].astype(i32)
    rank = route[:, 4:6].astype(i32)
    counts = cnt[0, :MOE_EXPERTS].astype(i32)
    ends = jnp.cumsum(counts)
    starts = ends - counts
    pos = (_lookup(starts, e) + rank).reshape(N_PAIRS)

    first_tile = starts // MOE_TM
    last_tile = (ends - 1) // MOE_TM
    ntiles = jnp.where(counts > 0, last_tile - first_tile + 1, 0)
    item_end = jnp.cumsum(ntiles)
    n_items = item_end[-1]
    w = jnp.minimum(jnp.arange(MOE_MAX_ITEMS, dtype=i32), n_items - 1)
    we = jnp.sum((item_end[None, :] <= w[:, None]).astype(i32), axis=-1)
    wt = _lookup(first_tile, we) + w - _lookup(item_end - ntiles, we)
    wlo = jnp.maximum(_lookup(starts, we) - wt * MOE_TM, 0)
    whi = jnp.minimum(_lookup(ends, we) - wt * MOE_TM, MOE_TM)
    changes = (wt[1:] != wt[:-1]).astype(i32)
    wfirst = jnp.concatenate([jnp.ones((1,), i32), changes])
    wlast = jnp.concatenate([changes, jnp.ones((1,), i32)])
    wlast = jnp.where(jnp.arange(MOE_MAX_ITEMS, dtype=i32) == n_items - 1, 1, wlast)
    return pos, (wt, we, wlo, whi, wfirst, wlast, n_items.reshape(1))


def _block_diag(w):
    n, a, b = w.shape
    eye = jnp.eye(n, dtype=w.dtype)
    return (w[:, :, None, :] * eye[:, None, :, None]).reshape(n * a, n * b)


def _ssm_params(a_re, a_im, log_dt, b_re, b_im, c_re, c_im):
    ar, ai = a_re, a_im
    dt = jnp.exp(log_dt)[:, None]
    mag = jnp.exp(ar * dt)
    lb_re = mag * jnp.cos(ai * dt)
    lb_im = mag * jnp.sin(ai * dt)
    den = ar * ar + ai * ai
    nr = lb_re - 1.0
    coef_re = (nr * ar + lb_im * ai) / den
    coef_im = (lb_im * ar - nr * ai) / den
    bb_re = coef_re[..., None] * b_re - coef_im[..., None] * b_im
    bb_im = coef_re[..., None] * b_im + coef_im[..., None] * b_re
    gk = SSM_GROUPS // SSM_KB

    def diag_blocks(w):
        return jnp.stack([_block_diag(w[k * gk:(k + 1) * gk]) for k in range(SSM_KB)]).astype(bf16)

    bbre = diag_blocks(bb_re.transpose(0, 2, 1))
    bbim = diag_blocks(bb_im.transpose(0, 2, 1))
    ccre = diag_blocks(c_re.transpose(0, 2, 1))
    ccim = diag_blocks(c_im.transpose(0, 2, 1))
    return (lb_re.reshape(1, SSM_LANES), lb_im.reshape(1, SSM_LANES), bbre, bbim, ccre, ccim)


def kernel(x_prompt, x_sample, state_ret, state_ssm_re, state_ssm_im, state_lru, state_conv, w_in, ret_gn_g, ret_gn_b, ssm_a_re, ssm_a_im, ssm_log_dt, ssm_b_re, ssm_b_im, ssm_c_re, ssm_c_im, ssm_d, ssm_w_glu, lru_conv_w, lru_conv_b, lru_wa, lru_ba, lru_wx, lru_bx, lru_lambda, w_branch, w_out, ln1_g, ln1_b, moe_w_group, moe_b_group, moe_w_expert, moe_b_expert, moe_w1, moe_w3, moe_w2, ln2_g, ln2_b):
    x, xb = _to_rows(x_prompt, x_sample)
    rope = _rope_tables()
    tabs_p = _ret_tables(RET_SUB_T)
    tabs_s = _ret_tables(DEC_SEQ)
    row = lambda v: v.reshape(1, -1)

    outs = [[] for _ in range(10)]
    for l in range(DEPTH):
        w_bf = w_in[l].astype(bf16)

        s0 = state_ret.reshape(DEPTH, DEC_BATCH, 2, 2 * RET_DK, RET_DV)
        y_ret, ret_p, ret_s = _retention(xb, w_bf, rope, s0, l, tabs_p, tabs_s,
                                         row(ret_gn_g[l]), row(ret_gn_b[l]))

        sp = _ssm_params(ssm_a_re[l], ssm_a_im[l], ssm_log_dt[l], ssm_b_re[l], ssm_b_im[l],
                         ssm_c_re[l], ssm_c_im[l])
        y_ssm, re_p, im_p, re_s, im_s = _ssm(
            xb, w_bf, state_ssm_re[l].reshape(DEC_BATCH, SSM_LANES), state_ssm_im[l].reshape(DEC_BATCH, SSM_LANES),
            *sp, row(ssm_d[l]), ssm_w_glu[l].astype(bf16))

        conv0 = state_conv[l].transpose(1, 0, 2).reshape(LRU_HIST_S, BRANCH_W)
        y_lru, lru_p, conv_p, lru_s, conv_s = _lru(
            xb, w_bf, state_lru[l], conv0, lru_conv_w[l], row(lru_conv_b[l]),
            _block_diag(lru_wa[l]).astype(bf16), row(lru_ba[l]),
            _block_diag(lru_wx[l]).astype(bf16), row(lru_bx[l]), row(lru_lambda[l]))

        wr = jnp.zeros((D_MODEL, ROUTE_LANES), f32)
        wr = wr.at[:, 0:MOE_GROUPS].set(moe_w_group[l]).at[:, MOE_GROUPS:MOE_GROUPS + MOE_EXPERTS].set(moe_w_expert[l])
        br = jnp.zeros((1, ROUTE_LANES), f32)
        br = br.at[0, 0:MOE_GROUPS].set(moe_b_group[l]).at[0, MOE_GROUPS:MOE_GROUPS + MOE_EXPERTS].set(moe_b_expert[l])
        x1, route, cnt = _merge(y_ret, y_ssm, y_lru, xb, w_bf, x, w_branch[l].astype(bf16), w_out[l].astype(bf16),
                                row(ln1_g[l]), row(ln1_b[l]), wr.astype(bf16), br)

        pos, plan = _dispatch_plan(route, cnt)
        xs = _dispatch(pos, x1)
        ys = _moe(plan, xs, moe_w1, moe_w3, moe_w2, l)
        x, xb = _combine(pos, ys, x1, route, row(ln2_g[l]), row(ln2_b[l]), final=(l == DEPTH - 1))

        outs[0].append(ret_p.reshape(BATCH, RET_HEADS, RET_DK, RET_DV))
        outs[1].append(re_p.reshape(BATCH, SSM_GROUPS, SSM_STATE))
        outs[2].append(im_p.reshape(BATCH, SSM_GROUPS, SSM_STATE))
        outs[3].append(lru_p)
        outs[4].append(conv_p.reshape(CONV_W - 1, BATCH, BRANCH_W).transpose(1, 0, 2))
        outs[5].append(ret_s.reshape(DEC_BATCH, RET_HEADS, RET_DK, RET_DV))
        outs[6].append(re_s.reshape(DEC_BATCH, SSM_GROUPS, SSM_STATE))
        outs[7].append(im_s.reshape(DEC_BATCH, SSM_GROUPS, SSM_STATE))
        outs[8].append(lru_s)
        outs[9].append(conv_s.reshape(CONV_W - 1, DEC_BATCH, BRANCH_W).transpose(1, 0, 2))

    y_prompt, y_sample = x, xb
    return (y_prompt, y_sample) + tuple(jnp.stack(o) for o in outs)
```

```python
import functools

import jax
import jax.numpy as jnp
import numpy as np
from jax import lax
from jax.experimental import pallas as pl
from jax.experimental.pallas import tpu as pltpu

f32 = jnp.float32
bf16 = jnp.bfloat16

D_MODEL = 1024
BATCH = 8
SEQ = 2048
DEPTH = 2
DEC_BATCH = 128
DEC_SEQ = 8
PAST_LEN = 16384
BRANCH_W = 512
N_BRANCH = 3
RET_HEADS = 4
RET_DK = 64
RET_DV = 128
ROPE_BASE = 10000.0
SSM_GROUP = 16
SSM_GROUPS = 32
SSM_STATE = 64
SSM_LANES = SSM_GROUPS * SSM_STATE
LRU_BLOCKS = 8
LRU_BW = 64
CONV_W = 4
LRU_C = 8.0
MOE_GROUPS = 4
MOE_PER_GROUP = 8
MOE_EXPERTS = 32
MOE_TOPK = 2
MOE_HIDDEN = 512
DN_ALPHA = (2.0 * DEPTH) ** 0.25
LN_EPS = 1e-5
D_IN = 6144

V7X_SUBLANES = 8
V7X_LANES = 128
V7X_VMEM_LIMIT = 56 * 1024 * 1024

N_P = BATCH * SEQ
N_S = DEC_BATCH * DEC_SEQ
N_TOK = N_P + N_S
ROW_TILE = 1024
P_TILES = N_P // ROW_TILE
N_TILES = N_TOK // ROW_TILE
P_TC = ROW_TILE // BATCH
RET_SUB_T = 32
RET_SUB_R = RET_SUB_T * BATCH
RET_SUBS = ROW_TILE // RET_SUB_R
S_BLOCKS = DEC_BATCH // BATCH
S_BLOCK_R = DEC_SEQ * BATCH
MERGE_TILE = 512
MOE_TM = 256
N_PAIRS = N_TOK * MOE_TOPK
MOE_MAX_ITEMS = N_PAIRS // MOE_TM + MOE_EXPERTS - 1
DSP_TILE = 512
CMB_TILE = 256
CMB_P_STEPS = N_P // CMB_TILE
ROUTE_LANES = 128


PROJ_RC = 256
RET_COLS, RET_BLK = 1536, 0
SSM_COLS, SSM_BLK = 512, 3
LRU_COLS, LRU_BLK = 1024, 2
GATE_COLS, GATE_BLK = 3072, 1


def _cparams(sem):
    return pltpu.CompilerParams(dimension_semantics=sem, vmem_limit_bytes=V7X_VMEM_LIMIT)


N_PLANES = D_MODEL // V7X_LANES


def _to_planes(ref, rows):
    for c in range(N_PLANES):
        ref[c] = rows[:, c * V7X_LANES:(c + 1) * V7X_LANES]


def _from_planes(ref):
    return jnp.concatenate([ref[c] for c in range(N_PLANES)], axis=1)


def _const_spec(block_shape, index):
    return pl.BlockSpec(block_shape, lambda *_: index, pipeline_mode=pl.Buffered(1))


def _dot(a, b):
    return jnp.dot(a, b, preferred_element_type=f32)


def _dot_nt(a, b):
    return lax.dot_general(a, b, (((1,), (1,)), ((), ())), preferred_element_type=f32)


def _dot_tn(a, b):
    return lax.dot_general(a, b, (((0,), (0,)), ((), ())), preferred_element_type=f32)


def _layer_norm_rows(x, g, b):
    mu = jnp.mean(x, -1, keepdims=True)
    xc = x - mu
    var = jnp.mean(xc * xc, -1, keepdims=True)
    return xc * lax.rsqrt(var + LN_EPS) * g + b


def _project_rows(xb_ref, w_ref, z_ref):
    for rc in range(ROW_TILE // PROJ_RC):
        rs = slice(rc * PROJ_RC, (rc + 1) * PROJ_RC)
        z_ref[rs, :] = _dot(xb_ref[rs, :], w_ref[0])


def _ret_block(q, k, v, g, cosb, sinb, mask_ref, qdec_ref, kdec_ref, cdec_ref, scat_ref, gng, gnb):
    rows = q.shape[0]
    lane_qk = lax.broadcasted_iota(jnp.int32, (rows, 2 * V7X_LANES), 1)
    first_half = (lane_qk & (RET_DK - 1)) < (RET_DK // 2)

    def rope(x):
        partner = jnp.where(first_half, pltpu.roll(x, 2 * V7X_LANES - RET_DK // 2, 1),
                            pltpu.roll(x, RET_DK // 2, 1))
        return x * cosb + partner * sinb

    q = rope(q)
    k = rope(k) * (RET_DK ** -0.5)
    kd = k * kdec_ref[...]
    lane = lax.broadcasted_iota(jnp.int32, (rows, V7X_LANES), 1)
    row_b = lax.broadcasted_iota(jnp.int32, (rows, V7X_LANES), 0) & (BATCH - 1)
    outs = []
    for p in range(2):
        qp = q[:, p * V7X_LANES:(p + 1) * V7X_LANES]
        kp = k[:, p * V7X_LANES:(p + 1) * V7X_LANES].astype(bf16)
        kdp = kd[:, p * V7X_LANES:(p + 1) * V7X_LANES]
        s_old = scat_ref[p]
        s_bf = s_old.astype(bf16)
        s_new = s_old * jnp.concatenate([cdec_ref[p]] * BATCH, axis=1)
        for hh in range(2):
            h = 2 * p + hh
            head_lanes = (lane >= RET_DK) if hh else (lane < RET_DK)
            qh = jnp.where(head_lanes, qp, 0.0).astype(bf16)
            kdh = jnp.where(head_lanes, kdp, 0.0).astype(bf16)
            vh = v[:, h * RET_DV:(h + 1) * RET_DV]
            vh_bf = vh.astype(bf16)
            sc = _dot_nt(qh, kp) * mask_ref[h]
            o = _dot(sc.astype(bf16), vh_bf)
            cross = _dot(qh, s_bf)
            oc = jnp.zeros((rows, RET_DV), f32)
            for b in range(BATCH):
                oc = oc + jnp.where(row_b == b, cross[:, b * RET_DV:(b + 1) * RET_DV], 0.0)
            o = o + oc * qdec_ref[h]
            vcat = jnp.concatenate([jnp.where(row_b == b, vh_bf, jnp.zeros_like(vh_bf))
                                    for b in range(BATCH)], axis=1)
            s_new = s_new + _dot_tn(kdh, vcat)
            mu = jnp.mean(o, -1, keepdims=True)
            oc2 = o - mu
            var = jnp.mean(oc2 * oc2, -1, keepdims=True)
            outs.append(oc2 * lax.rsqrt(var + LN_EPS))
        scat_ref[p] = s_new
    o = jnp.concatenate(outs, axis=1) * gng + gnb
    return jax.nn.silu(g) * o


def _per_step_rows(tab_ref, t0, steps):
    return jnp.concatenate(
        [jnp.broadcast_to(tab_ref[t0 + t:t0 + t + 1, :], (BATCH, tab_ref.shape[1])) for t in range(steps)], axis=0)


def _ret_kernel(xb_ref, w_ref, cos_ref, sin_ref, cos_s_ref, sin_s_ref, s0_ref,
                mask_p_ref, qdec_p_ref, kdec_p_ref, cdec_p_ref,
                mask_s_ref, qdec_s_ref, kdec_s_ref, cdec_s_ref,
                gng_ref, gnb_ref,
                y_ref, retp_ref, rets_ref, scat_ref, z_ref):
    i = pl.program_id(0)
    gng = gng_ref[...]
    gnb = gnb_ref[...]
    q_cols, k_cols = slice(0, 256), slice(256, 512)
    v_cols, g_cols = slice(512, 1024), slice(1024, 1536)

    @pl.when(i == 0)
    def _():
        scat_ref[...] = jnp.zeros_like(scat_ref)

    @pl.when(i <= P_TILES)
    def _():
        _project_rows(xb_ref, w_ref, z_ref)

    @pl.when(i < P_TILES)
    def _():
        for sc in range(RET_SUBS):
            r0 = sc * RET_SUB_R
            rs = slice(r0, r0 + RET_SUB_R)
            y_ref[rs, :] = _ret_block(
                z_ref[rs, q_cols], z_ref[rs, k_cols], z_ref[rs, v_cols], z_ref[rs, g_cols],
                _per_step_rows(cos_ref, sc * RET_SUB_T, RET_SUB_T), _per_step_rows(sin_ref, sc * RET_SUB_T, RET_SUB_T),
                mask_p_ref, qdec_p_ref, kdec_p_ref, cdec_p_ref, scat_ref, gng, gnb)

    @pl.when(i == P_TILES - 1)
    def _():
        for b in range(BATCH):
            for p in range(2):
                retp_ref[b, p] = scat_ref[p, :, b * RET_DV:(b + 1) * RET_DV]

    @pl.when(i >= P_TILES)
    def _():
        bb = i - P_TILES
        for b in range(BATCH):
            for p in range(2):
                scat_ref[p, :, b * RET_DV:(b + 1) * RET_DV] = s0_ref[0, b, p]

        def rows_of(cols):
            return jnp.concatenate(
                [z_ref[pl.ds(pl.multiple_of(t * DEC_BATCH + bb * BATCH, BATCH), BATCH), cols]
                 for t in range(DEC_SEQ)], axis=0)

        y = _ret_block(
            rows_of(q_cols), rows_of(k_cols), rows_of(v_cols),
            rows_of(g_cols), _per_step_rows(cos_s_ref, 0, DEC_SEQ), _per_step_rows(sin_s_ref, 0, DEC_SEQ),
            mask_s_ref, qdec_s_ref, kdec_s_ref, cdec_s_ref, scat_ref, gng, gnb)
        for t in range(DEC_SEQ):
            y_ref[pl.ds(pl.multiple_of(t * DEC_BATCH + bb * BATCH, BATCH), BATCH), :] = (
                y[t * BATCH:(t + 1) * BATCH, :])
        for b in range(BATCH):
            for p in range(2):
                rets_ref[b, p] = scat_ref[p, :, b * RET_DV:(b + 1) * RET_DV]


def _ret_tables(tc):
    rows = tc * BATCH
    nf = np.float32
    log_g = np.log1p(-np.exp2(nf(-5.0) - np.arange(RET_HEADS, dtype=nf))).astype(nf)
    t_idx = (np.arange(rows) // BATCH).astype(nf)
    b_idx = np.arange(rows) % BATCH
    rel = t_idx[:, None] - t_idx[None, :]
    same = b_idx[:, None] == b_idx[None, :]
    decay = np.exp(log_g[:, None, None] * np.maximum(rel, nf(0.0)))
    mask = np.where((rel >= 0) & same, decay, nf(0.0))
    qdec = np.exp(log_g[:, None] * (t_idx[None, :] + nf(1.0)))
    qdec = np.broadcast_to(qdec[:, :, None], (RET_HEADS, rows, RET_DV))
    kdec = np.exp(log_g[:, None] * (nf(tc - 1.0) - t_idx[None, :]))
    kdec = np.broadcast_to(kdec.T[:, :, None], (rows, RET_HEADS, RET_DK)).reshape(rows, RET_HEADS * RET_DK)
    cdec = np.exp(log_g * nf(tc))
    cdec = np.broadcast_to(cdec[:, None, None], (RET_HEADS, RET_DK, RET_DV)).reshape(2, 2 * RET_DK, RET_DV)
    return tuple(jnp.asarray(np.ascontiguousarray(a), dtype=f32) for a in (mask, qdec, kdec, cdec))


def _rope_tables():
    half = RET_DK // 2
    inv = ROPE_BASE ** (-jnp.arange(half, dtype=f32) / half)
    pos_p = jnp.arange(SEQ, dtype=f32)
    pos_s = PAST_LEN + jnp.arange(DEC_SEQ, dtype=f32)

    def tab(pos):
        ang = pos[:, None] * inv[None, :]
        cos = jnp.cos(ang)
        sin = jnp.sin(ang)
        cos_h = jnp.concatenate([cos, cos], axis=1)
        sin_h = jnp.concatenate([-sin, sin], axis=1)
        return jnp.tile(cos_h, (1, RET_HEADS)), jnp.tile(sin_h, (1, RET_HEADS))

    return tab(pos_p) + tab(pos_s)


def _retention(xb, w_bf, rope, s0_s, layer, tabs_p, tabs_s, gng, gnb):
    n_steps = P_TILES + S_BLOCKS
    tile = lambda i: jnp.minimum(i, P_TILES)
    sblk = lambda i: jnp.maximum(i - P_TILES, 0)
    full = lambda a: pl.BlockSpec(a.shape, lambda i, _n=a.ndim: (0,) * _n)
    state_blk = (BATCH, 2, 2 * RET_DK, RET_DV)
    cos_p, sin_p, cos_s, sin_s = rope
    ptile = lambda i: jnp.minimum(i, P_TILES - 1)
    ins = [xb, w_bf, cos_p, sin_p, cos_s, sin_s, s0_s, *tabs_p, *tabs_s, gng, gnb]
    in_specs = [
        pl.BlockSpec((ROW_TILE, D_MODEL), lambda i: (tile(i), 0)),
        _const_spec((1, D_MODEL, RET_COLS), (layer, 0, RET_BLK)),
        pl.BlockSpec((P_TC, 256), lambda i: (ptile(i), 0)),
        pl.BlockSpec((P_TC, 256), lambda i: (ptile(i), 0)),
        full(cos_s), full(sin_s),
        pl.BlockSpec((1,) + state_blk, lambda i: (layer, sblk(i), 0, 0, 0)),
    ] + [full(a) for a in (*tabs_p, *tabs_s, gng, gnb)]
    return pl.pallas_call(
        _ret_kernel,
        grid=(n_steps,),
        in_specs=in_specs,
        out_specs=[pl.BlockSpec((ROW_TILE, BRANCH_W), lambda i: (tile(i), 0)),
                   pl.BlockSpec(state_blk, lambda i: (0, 0, 0, 0)),
                   pl.BlockSpec(state_blk, lambda i: (sblk(i), 0, 0, 0))],
        out_shape=[jax.ShapeDtypeStruct((N_TOK, BRANCH_W), f32),
                   jax.ShapeDtypeStruct((BATCH, 2, 2 * RET_DK, RET_DV), f32),
                   jax.ShapeDtypeStruct((DEC_BATCH, 2, 2 * RET_DK, RET_DV), f32)],
        scratch_shapes=[pltpu.VMEM((2, 2 * RET_DK, BATCH * RET_DV), f32),
                        pltpu.VMEM((ROW_TILE, RET_COLS), f32)],
        compiler_params=_cparams(("arbitrary",)),
        name="retention",
    )(*ins)


SSM_LB = 512
SSM_RC = 256
SSM_KB = 2
SSM_KB_U = BRANCH_W // SSM_KB
SSM_KB_H = SSM_LANES // SSM_KB


def _ssm_scan(bre_ref, bim_ref, lre_ref, lim_ref, h_re0, h_im0, row0, nb_rows, steps, lb, unroll):
    ls = slice(lb * SSM_LB, (lb + 1) * SSM_LB)
    a_re = jnp.broadcast_to(lre_ref[:, ls], (V7X_SUBLANES, SSM_LB))
    a_im = jnp.broadcast_to(lim_ref[:, ls], (V7X_SUBLANES, SSM_LB))

    def step(t, carry):
        h_re, h_im = carry
        r = pl.multiple_of(row0 + t * nb_rows, V7X_SUBLANES)
        n_re = a_re * h_re - a_im * h_im + bre_ref[pl.ds(r, V7X_SUBLANES), ls]
        n_im = a_re * h_im + a_im * h_re + bim_ref[pl.ds(r, V7X_SUBLANES), ls]
        bre_ref[pl.ds(r, V7X_SUBLANES), ls] = n_re
        bim_ref[pl.ds(r, V7X_SUBLANES), ls] = n_im
        return n_re, n_im

    return lax.fori_loop(0, steps, step, (h_re0, h_im0), unroll=unroll)


def _ssm_kernel(xb_ref, w_ref, h0re_ref, h0im_ref, lre_ref, lim_ref, bbre_ref, bbim_ref, ccre_ref, ccim_ref,
                d_ref, wglu_ref,
                y_ref, pre_ref, pim_ref, sre_ref, sim_ref,
                bre_ref, bim_ref, hre_ref, him_ref, zs_ref):
    i = pl.program_id(0)
    _project_rows(xb_ref, w_ref, zs_ref)
    for rc in range(ROW_TILE // SSM_RC):
        rs = slice(rc * SSM_RC, (rc + 1) * SSM_RC)
        ub = zs_ref[rs, :].astype(bf16)
        for k in range(SSM_KB):
            uk = ub[:, k * SSM_KB_U:(k + 1) * SSM_KB_U]
            hs = slice(k * SSM_KB_H, (k + 1) * SSM_KB_H)
            bre_ref[rs, hs] = _dot(uk, bbre_ref[k])
            bim_ref[rs, hs] = _dot(uk, bbim_ref[k])

    @pl.when(i == 0)
    def _():
        hre_ref[...] = jnp.zeros_like(hre_ref)
        him_ref[...] = jnp.zeros_like(him_ref)

    @pl.when(i < P_TILES)
    def _():
        for lb in range(SSM_LANES // SSM_LB):
            ls = slice(lb * SSM_LB, (lb + 1) * SSM_LB)
            h_re, h_im = _ssm_scan(bre_ref, bim_ref, lre_ref, lim_ref, hre_ref[:, ls], him_ref[:, ls],
                                   0, BATCH, P_TC, lb, 8)
            hre_ref[:, ls] = h_re
            him_ref[:, ls] = h_im
        pre_ref[...] = hre_ref[...]
        pim_ref[...] = him_ref[...]

    @pl.when(i >= P_TILES)
    def _():
        def per_row_tile(rt, c):
            r0 = pl.multiple_of(rt * V7X_SUBLANES, V7X_SUBLANES)
            for lb in range(SSM_LANES // SSM_LB):
                ls = slice(lb * SSM_LB, (lb + 1) * SSM_LB)
                _ssm_scan(bre_ref, bim_ref, lre_ref, lim_ref,
                          h0re_ref[pl.ds(r0, V7X_SUBLANES), ls], h0im_ref[pl.ds(r0, V7X_SUBLANES), ls],
                          r0, DEC_BATCH, DEC_SEQ, lb, True)
            return c

        lax.fori_loop(0, DEC_BATCH // V7X_SUBLANES, per_row_tile, 0)
        last = (DEC_SEQ - 1) * DEC_BATCH
        sre_ref[...] = bre_ref[last:last + DEC_BATCH, :]
        sim_ref[...] = bim_ref[last:last + DEC_BATCH, :]

    for rc in range(ROW_TILE // SSM_RC):
        rs = slice(rc * SSM_RC, (rc + 1) * SSM_RC)
        ch = []
        for k in range(SSM_KB):
            hs = slice(k * SSM_KB_H, (k + 1) * SSM_KB_H)
            ch.append(_dot(bre_ref[rs, hs].astype(bf16), ccre_ref[k]) - _dot(bim_ref[rs, hs].astype(bf16), ccim_ref[k]))
        y = jnp.concatenate(ch, axis=1) + d_ref[...] * zs_ref[rs, :]
        zz = jax.nn.gelu(y)
        y_ref[rs, :] = zz * jax.nn.sigmoid(_dot(zz.astype(bf16), wglu_ref[...]))


def _ssm(xb, w_bf, layer, h0re, h0im, lre, lim, bbre, bbim, ccre, ccim, dvec, wglu):
    full = lambda a: _const_spec(a.shape, (0,) * a.ndim)
    consts = (h0re, h0im, lre, lim, bbre, bbim, ccre, ccim, dvec, wglu)
    return pl.pallas_call(
        _ssm_kernel,
        grid=(N_TILES,),
        in_specs=[pl.BlockSpec((ROW_TILE, D_MODEL), lambda i: (i, 0)),
                  _const_spec((1, D_MODEL, SSM_COLS), (layer, 0, SSM_BLK))] + [full(a) for a in consts],
        out_specs=[pl.BlockSpec((ROW_TILE, BRANCH_W), lambda i: (i, 0)),
                   pl.BlockSpec((BATCH, SSM_LANES), lambda i: (0, 0)),
                   pl.BlockSpec((BATCH, SSM_LANES), lambda i: (0, 0)),
                   pl.BlockSpec((DEC_BATCH, SSM_LANES), lambda i: (0, 0)),
                   pl.BlockSpec((DEC_BATCH, SSM_LANES), lambda i: (0, 0))],
        out_shape=[jax.ShapeDtypeStruct((N_TOK, BRANCH_W), f32),
                   jax.ShapeDtypeStruct((BATCH, SSM_LANES), f32),
                   jax.ShapeDtypeStruct((BATCH, SSM_LANES), f32),
                   jax.ShapeDtypeStruct((DEC_BATCH, SSM_LANES), f32),
                   jax.ShapeDtypeStruct((DEC_BATCH, SSM_LANES), f32)],
        scratch_shapes=[pltpu.VMEM((ROW_TILE, SSM_LANES), f32), pltpu.VMEM((ROW_TILE, SSM_LANES), f32),
                        pltpu.VMEM((BATCH, SSM_LANES), f32), pltpu.VMEM((BATCH, SSM_LANES), f32),
                        pltpu.VMEM((ROW_TILE, SSM_COLS), f32)],
        compiler_params=_cparams(("arbitrary",)),
        name="ssm",
    )(xb, w_bf, *consts)


LRU_HIST_P = (CONV_W - 1) * BATCH
LRU_HIST_S = (CONV_W - 1) * DEC_BATCH


def _lru_gates(xe_ref, nb_rows, cw_ref, cb_ref, wa_ref, ba_ref, wx_ref, bx_ref, lam_ref):
    xc = cb_ref[...] + xe_ref[0:ROW_TILE, :] * cw_ref[0:1, :]
    for j in range(1, CONV_W):
        xc = xc + xe_ref[j * nb_rows:j * nb_rows + ROW_TILE, :] * cw_ref[j:j + 1, :]
    xcb = xc.astype(bf16)
    r = jax.nn.sigmoid(_dot(xcb, wa_ref[...]) + ba_ref[...])
    ig = jax.nn.sigmoid(_dot(xcb, wx_ref[...]) + bx_ref[...])
    log_a = -LRU_C * r * jax.nn.softplus(-lam_ref[...])
    a = jnp.exp(log_a)
    b = jnp.sqrt(-jnp.tanh(log_a) * (jnp.exp(2.0 * log_a) + 1.0)) * (ig * xc)
    return a, b


def _lru_kernel(xb_ref, w_ref, h0_ref, conv0_ref, cw_ref, cb_ref, wa_ref, ba_ref, wx_ref, bx_ref, lam_ref,
                y_ref, hp_ref, convp_ref, hs_ref, convs_ref,
                xe_ref, a_ref, b_ref, hc_ref, z_ref):
    i = pl.program_id(0)
    params = (cw_ref, cb_ref, wa_ref, ba_ref, wx_ref, bx_ref, lam_ref)
    _project_rows(xb_ref, w_ref, z_ref)
    zx_ref = z_ref.at[:, 0:BRANCH_W]
    zg_ref = z_ref.at[:, BRANCH_W:2 * BRANCH_W]

    @pl.when(i == 0)
    def _():
        xe_ref[0:LRU_HIST_P, :] = jnp.zeros((LRU_HIST_P, BRANCH_W), f32)
        hc_ref[...] = jnp.zeros_like(hc_ref)

    @pl.when(i < P_TILES)
    def _():
        xe_ref[LRU_HIST_P:LRU_HIST_P + ROW_TILE, :] = zx_ref[...]
        a, b = _lru_gates(xe_ref, BATCH, *params)
        a_ref[...] = a
        b_ref[...] = b
        hist = xe_ref[ROW_TILE:ROW_TILE + LRU_HIST_P, :]
        xe_ref[0:LRU_HIST_P, :] = hist
        convp_ref[...] = hist

        def step(t, h):
            r = pl.multiple_of(t * BATCH, BATCH)
            h = a_ref[pl.ds(r, BATCH), :] * h + b_ref[pl.ds(r, BATCH), :]
            b_ref[pl.ds(r, BATCH), :] = h
            return h

        h = lax.fori_loop(0, P_TC, step, hc_ref[...], unroll=8)
        hc_ref[...] = h
        hp_ref[...] = h

    @pl.when(i >= P_TILES)
    def _():
        xe_ref[0:LRU_HIST_S, :] = conv0_ref[...]
        xe_ref[LRU_HIST_S:LRU_HIST_S + ROW_TILE, :] = zx_ref[...]
        a, b = _lru_gates(xe_ref, DEC_BATCH, *params)
        a_ref[...] = a
        b_ref[...] = b
        convs_ref[...] = xe_ref[ROW_TILE:ROW_TILE + LRU_HIST_S, :]

        def per_row_tile(rt, c):
            r0 = pl.multiple_of(rt * V7X_SUBLANES, V7X_SUBLANES)
            h = h0_ref[pl.ds(r0, V7X_SUBLANES), :]
            for t in range(DEC_SEQ):
                r = pl.multiple_of(t * DEC_BATCH + r0, V7X_SUBLANES)
                h = a_ref[pl.ds(r, V7X_SUBLANES), :] * h + b_ref[pl.ds(r, V7X_SUBLANES), :]
                b_ref[pl.ds(r, V7X_SUBLANES), :] = h
            return c

        lax.fori_loop(0, DEC_BATCH // V7X_SUBLANES, per_row_tile, 0)
        last = (DEC_SEQ - 1) * DEC_BATCH
        hs_ref[...] = b_ref[last:last + DEC_BATCH, :]

    y_ref[...] = b_ref[...] * jax.nn.gelu(zg_ref[...])


def _lru(xb, w_bf, layer, h0, conv0, cw, cb, wa, ba, wx, bx, lam):
    full = lambda a: _const_spec(a.shape, (0,) * a.ndim)
    consts = (h0, conv0, cw, cb, wa, ba, wx, bx, lam)
    return pl.pallas_call(
        _lru_kernel,
        grid=(N_TILES,),
        in_specs=[pl.BlockSpec((ROW_TILE, D_MODEL), lambda i: (i, 0)),
                  _const_spec((1, D_MODEL, LRU_COLS), (layer, 0, LRU_BLK))] + [full(a) for a in consts],
        out_specs=[pl.BlockSpec((ROW_TILE, BRANCH_W), lambda i: (i, 0)),
                   pl.BlockSpec((BATCH, BRANCH_W), lambda i: (0, 0)),
                   pl.BlockSpec((LRU_HIST_P, BRANCH_W), lambda i: (0, 0)),
                   pl.BlockSpec((DEC_BATCH, BRANCH_W), lambda i: (0, 0)),
                   pl.BlockSpec((LRU_HIST_S, BRANCH_W), lambda i: (0, 0))],
        out_shape=[jax.ShapeDtypeStruct((N_TOK, BRANCH_W), f32),
                   jax.ShapeDtypeStruct((BATCH, BRANCH_W), f32),
                   jax.ShapeDtypeStruct((LRU_HIST_P, BRANCH_W), f32),
                   jax.ShapeDtypeStruct((DEC_BATCH, BRANCH_W), f32),
                   jax.ShapeDtypeStruct((LRU_HIST_S, BRANCH_W), f32)],
        scratch_shapes=[pltpu.VMEM((ROW_TILE + LRU_HIST_S, BRANCH_W), f32),
                        pltpu.VMEM((ROW_TILE, BRANCH_W), f32), pltpu.VMEM((ROW_TILE, BRANCH_W), f32),
                        pltpu.VMEM((BATCH, BRANCH_W), f32),
                        pltpu.VMEM((ROW_TILE, LRU_COLS), f32)],
        compiler_params=_cparams(("arbitrary",)),
        name="lru",
    )(xb, w_bf, *consts)


def _merge_kernel(yr_ref, ys_ref, yl_ref, xb_ref, wg_ref, x_ref, wb_ref, wo_ref, g_ref, b_ref, wr_ref, br_ref,
                  x1_ref, route_ref, route_t_ref, cnt_out_ref, cnt_ref):
    merged = jnp.zeros((MERGE_TILE, D_MODEL), f32)
    xb = xb_ref[...]
    for n, y_ref in enumerate((yr_ref, ys_ref, yl_ref)):
        proj = _dot(y_ref[...].astype(bf16), wb_ref[n])
        gate = jax.nn.sigmoid(_dot(xb, wg_ref[0, :, n * D_MODEL:(n + 1) * D_MODEL]))
        merged = merged + gate * proj
    mix = _dot(merged.astype(bf16), wo_ref[...])
    x1 = _layer_norm_rows(DN_ALPHA * x_ref[...] + mix, g_ref[...], b_ref[...])
    _to_planes(x1_ref, x1)

    logits = _dot(x1.astype(bf16), wr_ref[...]) + br_ref[...]
    lane = lax.broadcasted_iota(jnp.int32, (MERGE_TILE, ROUTE_LANES), 1).astype(f32)
    big = jnp.float32(ROUTE_LANES)
    neg = jnp.float32(-jnp.inf)
    is_g = lane < MOE_GROUPS
    lg = jnp.where(is_g, logits, neg)
    mg = jnp.max(lg, -1, keepdims=True)
    gsel = jnp.min(jnp.where(lg == mg, lane, big), -1, keepdims=True)
    sum_g = jnp.sum(jnp.where(is_g, jnp.exp(lg - mg), 0.0), -1, keepdims=True)
    pg_sel = 1.0 / sum_g
    lo = MOE_GROUPS + gsel * MOE_PER_GROUP
    is_e = jnp.abs(lane - lo - 0.5 * (MOE_PER_GROUP - 1)) < 0.5 * MOE_PER_GROUP
    le = jnp.where(is_e, logits, neg)
    me = jnp.max(le, -1, keepdims=True)
    ex = jnp.where(is_e, jnp.exp(le - me), 0.0)
    pe = jnp.where(is_e, ex / jnp.sum(ex, -1, keepdims=True), -1.0)
    v1 = jnp.max(pe, -1, keepdims=True)
    i1 = jnp.min(jnp.where(pe == v1, lane, big), -1, keepdims=True)
    pe2 = jnp.where(lane == i1, -1.0, pe)
    v2 = jnp.max(pe2, -1, keepdims=True)
    i2 = jnp.min(jnp.where(pe2 == v2, lane, big), -1, keepdims=True)
    vsum = v1 + v2
    w1 = pg_sel * v1 / vsum
    w2 = pg_sel * v2 / vsum
    e1 = i1 - MOE_GROUPS
    e2 = i2 - MOE_GROUPS

    @pl.when(pl.program_id(0) == 0)
    def _():
        cnt_ref[...] = jnp.zeros_like(cnt_ref)

    oh1 = lane == e1
    oh2 = lane == e2
    ohs = jnp.where(oh1, 1.0, jnp.where(oh2, 1.0, 0.0))
    r_i = lax.broadcasted_iota(jnp.int32, (MERGE_TILE, MERGE_TILE), 0)
    c_i = lax.broadcasted_iota(jnp.int32, (MERGE_TILE, MERGE_TILE), 1)
    strict_lower = jnp.where(c_i < r_i, 1.0, 0.0).astype(bf16)
    before = _dot(strict_lower, ohs.astype(bf16)) + cnt_ref[0:1, :]
    rank1 = jnp.sum(jnp.where(oh1, before, 0.0), -1, keepdims=True)
    rank2 = jnp.sum(jnp.where(oh2, before, 0.0), -1, keepdims=True)
    cnt_ref[0:1, :] = cnt_ref[0:1, :] + jnp.sum(ohs, 0, keepdims=True)
    cnt_out_ref[...] = cnt_ref[...]

    route = jnp.zeros((MERGE_TILE, ROUTE_LANES), f32)
    for k, val in enumerate((e1, e2, w1, w2, rank1, rank2)):
        route = jnp.where(lane == k, val, route)
    route_ref[...] = route
    route_t_ref[...] = route.T[0:V7X_SUBLANES, :]


def _merge(y_ret, y_ssm, y_lru, xb, w_bf, layer, x, wb, wo, g, b, wr, br):
    full = lambda a: _const_spec(a.shape, (0,) * a.ndim)
    row = lambda w: pl.BlockSpec((MERGE_TILE, w), lambda i: (i, 0))
    consts = (wb, wo, g, b, wr, br)
    return pl.pallas_call(
        _merge_kernel,
        grid=(N_TOK // MERGE_TILE,),
        in_specs=[row(BRANCH_W), row(BRANCH_W), row(BRANCH_W), row(D_MODEL),
                  _const_spec((1, D_MODEL, GATE_COLS), (layer, 0, GATE_BLK)),
                  row(D_MODEL)] + [full(a) for a in consts],
        out_specs=[pl.BlockSpec((N_PLANES, MERGE_TILE, V7X_LANES), lambda i: (0, i, 0)), row(ROUTE_LANES),
                   pl.BlockSpec((V7X_SUBLANES, MERGE_TILE), lambda i: (0, i)),
                   pl.BlockSpec((V7X_SUBLANES, ROUTE_LANES), lambda i: (0, 0))],
        out_shape=[jax.ShapeDtypeStruct((N_PLANES, N_TOK, V7X_LANES), f32),
                   jax.ShapeDtypeStruct((N_TOK, ROUTE_LANES), f32),
                   jax.ShapeDtypeStruct((V7X_SUBLANES, N_TOK), f32),
                   jax.ShapeDtypeStruct((V7X_SUBLANES, ROUTE_LANES), f32)],
        scratch_shapes=[pltpu.VMEM((V7X_SUBLANES, ROUTE_LANES), f32)],
        compiler_params=_cparams(("arbitrary",)),
        name="merge",
    )(y_ret, y_ssm, y_lru, xb, w_bf, x, *consts)


def _dispatch_kernel(pos0_ref, pos1_ref, x1_ref, xs_hbm, sem):
    base = pl.program_id(0) * DSP_TILE

    def row_copy(r, dst_row):
        return pltpu.make_async_copy(x1_ref.at[:, r, :], xs_hbm.at[dst_row], sem.at[0])

    def issue(r, c):
        n = base + r
        for k in range(MOE_TOPK):
            row_copy(r, (pos0_ref, pos1_ref)[k][n]).start(priority=k % 2)
        return c

    lax.fori_loop(0, DSP_TILE, issue, 0, unroll=8)
    for _ in range(MOE_TOPK):
        pltpu.make_async_copy(x1_ref, x1_ref, sem.at[0]).wait()


def _dispatch(pos, x1p):
    grid_spec = pltpu.PrefetchScalarGridSpec(
        num_scalar_prefetch=MOE_TOPK,
        grid=(N_TOK // DSP_TILE,),
        in_specs=[pl.BlockSpec((N_PLANES, DSP_TILE, V7X_LANES), lambda s, *_: (0, s, 0))],
        out_specs=pl.BlockSpec(memory_space=pl.ANY),
        scratch_shapes=[pltpu.SemaphoreType.DMA((1,))],
    )
    return pl.pallas_call(
        _dispatch_kernel,
        grid_spec=grid_spec,
        out_shape=jax.ShapeDtypeStruct((N_PAIRS, N_PLANES, V7X_LANES), f32),
        compiler_params=_cparams(("arbitrary",)),
        name="dispatch",
    )(*pos, x1p)


def _moe_kernel(layer, wt_ref, we_ref, wlo_ref, whi_ref, wfirst_ref, wlast_ref, wefirst_ref, weslot_ref,
                wenext_ref, nw_ref,
                xs_hbm, w1_hbm, w3_hbm, w2_hbm, ys_hbm,
                xin, yout, sem_in, sem_out, wst1, wst3, wst2, sem_w, w1b, w3b, w2b):
    w = pl.program_id(0)
    n_items = nw_ref[0]

    def weight_copies(expert, slot):
        return [pltpu.make_async_copy(src.at[layer, expert], dst.at[slot], sem_w.at[slot])
                for src, dst in ((w1_hbm, wst1), (w3_hbm, wst3), (w2_hbm, wst2))]

    def in_copy(item, slot, c):
        r0 = pl.multiple_of(wt_ref[item] * MOE_TM, MOE_TM)
        return pltpu.make_async_copy(xs_hbm.at[pl.ds(r0, MOE_TM), c, :], xin.at[slot, c], sem_in.at[slot])

    def out_copy(tile, slot, c):
        r0 = pl.multiple_of(tile * MOE_TM, MOE_TM)
        return pltpu.make_async_copy(yout.at[slot, c], ys_hbm.at[pl.ds(r0, MOE_TM), c, :], sem_out.at[slot])

    @pl.when(w == 0)
    def _():
        for cp in weight_copies(we_ref[0], 0):
            cp.start()
        for c in range(N_PLANES):
            in_copy(0, 0, c).start(priority=1)

    @pl.when(w + 1 < n_items)
    def _():
        for c in range(N_PLANES):
            in_copy(w + 1, (w + 1) % 2, c).start(priority=1)

    @pl.when(w < n_items)
    def _():
        slot = w % 2
        tile = wt_ref[w]
        oslot = tile % 2
        for c in range(N_PLANES):
            in_copy(w, slot, c).wait()

        @pl.when(wefirst_ref[w] == 1)
        def _():
            wslot = weslot_ref[w]
            for cp in weight_copies(we_ref[w], wslot):
                cp.wait()
            w1b[...] = wst1[wslot].astype(bf16)
            w3b[...] = wst3[wslot].astype(bf16)
            w2b[...] = wst2[wslot].astype(bf16)

            @pl.when(wenext_ref[w] >= 0)
            def _():
                for cp in weight_copies(wenext_ref[w], 1 - wslot):
                    cp.start()

        xt = _from_planes(xin.at[slot]).astype(bf16)
        h = jax.nn.silu(_dot(xt, w1b[...])) * _dot(xt, w3b[...])
        res = _dot(h.astype(bf16), w2b[...])
        row = lax.broadcasted_iota(jnp.int32, (MOE_TM, D_MODEL), 0)
        mine = jnp.where(row >= wlo_ref[w], row, MOE_TM) < whi_ref[w]

        @pl.when(wfirst_ref[w] == 1)
        def _():
            @pl.when(tile >= 2)
            def _():
                for c in range(N_PLANES):
                    out_copy(0, oslot, c).wait()

            _to_planes(yout.at[oslot], jnp.where(mine, res, 0.0))

        @pl.when(wfirst_ref[w] == 0)
        def _():
            _to_planes(yout.at[oslot], jnp.where(mine, res, _from_planes(yout.at[oslot])))

        @pl.when(wlast_ref[w] == 1)
        def _():
            for c in range(N_PLANES):
                out_copy(tile, oslot, c).start()

        @pl.when(w == n_items - 1)
        def _():
            for c in range(N_PLANES):
                out_copy(0, oslot, c).wait()

            @pl.when(tile >= 1)
            def _():
                for c in range(N_PLANES):
                    out_copy(0, 1 - oslot, c).wait()


def _moe(plan, xs, w1, w3, w2, layer):
    grid_spec = pltpu.PrefetchScalarGridSpec(
        num_scalar_prefetch=len(plan),
        grid=(MOE_MAX_ITEMS,),
        in_specs=[pl.BlockSpec(memory_space=pl.ANY)] * 4,
        out_specs=pl.BlockSpec(memory_space=pl.ANY),
        scratch_shapes=[pltpu.VMEM((2, N_PLANES, MOE_TM, V7X_LANES), f32),
                        pltpu.VMEM((2, N_PLANES, MOE_TM, V7X_LANES), f32),
                        pltpu.SemaphoreType.DMA((2,)), pltpu.SemaphoreType.DMA((2,)),
                        pltpu.VMEM((2, D_MODEL, MOE_HIDDEN), f32), pltpu.VMEM((2, D_MODEL, MOE_HIDDEN), f32),
                        pltpu.VMEM((2, MOE_HIDDEN, D_MODEL), f32), pltpu.SemaphoreType.DMA((2,)),
                        pltpu.VMEM((D_MODEL, MOE_HIDDEN), bf16), pltpu.VMEM((D_MODEL, MOE_HIDDEN), bf16),
                        pltpu.VMEM((MOE_HIDDEN, D_MODEL), bf16)],
    )
    return pl.pallas_call(
        functools.partial(_moe_kernel, layer),
        grid_spec=grid_spec,
        out_shape=jax.ShapeDtypeStruct((N_PAIRS, N_PLANES, V7X_LANES), f32),
        compiler_params=_cparams(("arbitrary",)),
        name="moe",
    )(*plan, xs, w1, w3, w2)


def _combine_kernel(final, pos0_ref, pos1_ref, ys_hbm, x1_ref, route_ref, g_ref, b_ref, *rest):
    if final:
        yp_ref, ysm_ref, buf, sem, t3_ref = rest
    else:
        o_ref, ob_ref, buf, sem = rest
    s = pl.program_id(0)
    nsteps = pl.num_programs(0)

    def gather_copy(row, slot, k, r):
        return pltpu.make_async_copy(ys_hbm.at[row], buf.at[slot, k, :, r, :], sem.at[slot])

    def issue(tile, slot):
        def body(r, c):
            n = tile * CMB_TILE + r
            for k in range(MOE_TOPK):
                gather_copy((pos0_ref, pos1_ref)[k][n], slot, k, r).start(priority=k % 2)
            return c

        lax.fori_loop(0, CMB_TILE, body, 0, unroll=8)

    @pl.when(s == 0)
    def _():
        issue(0, 0)

    @pl.when(s + 1 < nsteps)
    def _():
        issue(s + 1, (s + 1) % 2)

    slot = s % 2
    for k in range(MOE_TOPK):
        pltpu.make_async_copy(buf.at[slot, k], buf.at[slot, k], sem.at[slot]).wait()
    route = route_ref[...]
    moe = route[:, 2:3] * _from_planes(buf.at[slot, 0]) + route[:, 3:4] * _from_planes(buf.at[slot, 1])
    y = _layer_norm_rows(DN_ALPHA * _from_planes(x1_ref) + moe, g_ref[...], b_ref[...])
    if not final:
        o_ref[...] = y
        ob_ref[...] = y.astype(bf16)
        return

    @pl.when(s < CMB_P_STEPS)
    def _():
        t3_ref[...] = y.reshape(CMB_TILE // BATCH, BATCH, D_MODEL)
        for b in range(BATCH):
            yp_ref[b] = t3_ref[:, b, :]

    steps_per_tile = CMB_TILE // DEC_BATCH
    for q in range(N_S // CMB_TILE):
        @pl.when(s == CMB_P_STEPS + q)
        def _():
            for h in range(steps_per_tile):
                ysm_ref[:, q * steps_per_tile + h, :] = y[h * DEC_BATCH:(h + 1) * DEC_BATCH, :]


def _combine(pos, ys, x1, route, g, b, final):
    if final:
        out_specs = [pl.BlockSpec((BATCH, CMB_TILE // BATCH, D_MODEL),
                                  lambda s, *_: (0, jnp.minimum(s, CMB_P_STEPS - 1), 0)),
                     pl.BlockSpec((DEC_BATCH, DEC_SEQ, D_MODEL), lambda s, *_: (0, 0, 0))]
        out_shape = [jax.ShapeDtypeStruct((BATCH, SEQ, D_MODEL), f32),
                     jax.ShapeDtypeStruct((DEC_BATCH, DEC_SEQ, D_MODEL), f32)]
        extra = [pltpu.VMEM((CMB_TILE // BATCH, BATCH, D_MODEL), f32)]
    else:
        out_specs = [pl.BlockSpec((CMB_TILE, D_MODEL), lambda s, *_: (s, 0))] * 2
        out_shape = [jax.ShapeDtypeStruct((N_TOK, D_MODEL), f32), jax.ShapeDtypeStruct((N_TOK, D_MODEL), bf16)]
        extra = []
    grid_spec = pltpu.PrefetchScalarGridSpec(
        num_scalar_prefetch=MOE_TOPK,
        grid=(N_TOK // CMB_TILE,),
        in_specs=[pl.BlockSpec(memory_space=pl.ANY),
                  pl.BlockSpec((N_PLANES, CMB_TILE, V7X_LANES), lambda s, *_: (0, s, 0)),
                  pl.BlockSpec((CMB_TILE, ROUTE_LANES), lambda s, *_: (s, 0)),
                  pl.BlockSpec((1, D_MODEL), lambda s, *_: (0, 0)),
                  pl.BlockSpec((1, D_MODEL), lambda s, *_: (0, 0))],
        out_specs=out_specs,
        scratch_shapes=[pltpu.VMEM((2, MOE_TOPK, N_PLANES, CMB_TILE, V7X_LANES), f32),
                        pltpu.SemaphoreType.DMA((2,))] + extra,
    )
    return pl.pallas_call(
        functools.partial(_combine_kernel, final),
        grid_spec=grid_spec,
        out_shape=out_shape,
        compiler_params=_cparams(("arbitrary",)),
        name="combine_out" if final else "combine",
    )(*pos, ys, x1, route, g, b)


def _to_rows_kernel(xp_ref, xs_ref, o_ref, ob_ref, t3_ref):
    i = pl.program_id(0)

    @pl.when(i < P_TILES)
    def _():
        for b in range(BATCH):
            t3_ref[:, b, :] = xp_ref[b]
        rows = t3_ref[...].reshape(ROW_TILE, D_MODEL)
        o_ref[...] = rows
        ob_ref[...] = rows.astype(bf16)

    @pl.when(i >= P_TILES)
    def _():
        for t in range(DEC_SEQ):
            rows = xs_ref[:, t, :]
            o_ref[t * DEC_BATCH:(t + 1) * DEC_BATCH, :] = rows
            ob_ref[t * DEC_BATCH:(t + 1) * DEC_BATCH, :] = rows.astype(bf16)


def _to_rows(x_prompt, x_sample):
    return pl.pallas_call(
        _to_rows_kernel,
        grid=(N_TILES,),
        in_specs=[pl.BlockSpec((BATCH, P_TC, D_MODEL), lambda i: (0, jnp.minimum(i, P_TILES - 1), 0)),
                  pl.BlockSpec((DEC_BATCH, DEC_SEQ, D_MODEL), lambda i: (0, 0, 0))],
        out_specs=[pl.BlockSpec((ROW_TILE, D_MODEL), lambda i: (i, 0))] * 2,
        out_shape=[jax.ShapeDtypeStruct((N_TOK, D_MODEL), f32), jax.ShapeDtypeStruct((N_TOK, D_MODEL), bf16)],
        scratch_shapes=[pltpu.VMEM((P_TC, BATCH, D_MODEL), f32)],
        compiler_params=_cparams(("arbitrary",)),
        name="to_rows",
    )(x_prompt, x_sample)


def _lookup(table, idx):
    ar = jnp.arange(MOE_EXPERTS, dtype=jnp.int32).reshape((MOE_EXPERTS,) + (1,) * idx.ndim)
    table = table.reshape(ar.shape)
    return jnp.sum(jnp.where(idx[None] == ar, table, 0), axis=0)


def _dispatch_plan(route_t, cnt):
    i32 = jnp.int32
    e = route_t[0:2].astype(i32)
    rank = route_t[4:6].astype(i32)
    counts = cnt[0, :MOE_EXPERTS].astype(i32)
    ends = jnp.cumsum(counts)
    starts = ends - counts
    pos = _lookup(starts, e) + rank

    first_tile = starts // MOE_TM
    last_tile = (ends - 1) // MOE_TM
    ntiles = jnp.where(counts > 0, last_tile - first_tile + 1, 0)
    item_end = jnp.cumsum(ntiles)
    n_items = item_end[-1]
    w = jnp.minimum(jnp.arange(MOE_MAX_ITEMS, dtype=i32), n_items - 1)
    we = jnp.sum((item_end[None, :] <= w[:, None]).astype(i32), axis=-1)
    wt = _lookup(first_tile, we) + w - _lookup(item_end - ntiles, we)
    wlo = jnp.maximum(_lookup(starts, we) - wt * MOE_TM, 0)
    whi = jnp.minimum(_lookup(ends, we) - wt * MOE_TM, MOE_TM)
    changes = (wt[1:] != wt[:-1]).astype(i32)
    wfirst = jnp.concatenate([jnp.ones((1,), i32), changes])
    wlast = jnp.concatenate([changes, jnp.ones((1,), i32)])
    wlast = jnp.where(jnp.arange(MOE_MAX_ITEMS, dtype=i32) == n_items - 1, 1, wlast)
    wefirst = jnp.concatenate([jnp.ones((1,), i32), (we[1:] != we[:-1]).astype(i32)])
    weslot = (jnp.cumsum(wefirst) - 1) % 2
    ar = jnp.arange(MOE_EXPERTS, dtype=i32)
    later = (ar[None, :] > ar[:, None]) & (counts[None, :] > 0)
    next_expert = jnp.min(jnp.where(later, ar[None, :], MOE_EXPERTS), axis=1)
    wenext = _lookup(jnp.where(next_expert < MOE_EXPERTS, next_expert, -1), we)
    return (pos[0], pos[1]), (wt, we, wlo, whi, wfirst, wlast, wefirst, weslot, wenext, n_items.reshape(1))


def _block_diag(w):
    n, a, b = w.shape
    eye = jnp.eye(n, dtype=w.dtype)
    return (w[:, :, None, :] * eye[:, None, :, None]).reshape(n * a, n * b)


def _ssm_params(a_re, a_im, log_dt, b_re, b_im, c_re, c_im):
    ar, ai = a_re, a_im
    dt = jnp.exp(log_dt)[:, None]
    mag = jnp.exp(ar * dt)
    lb_re = mag * jnp.cos(ai * dt)
    lb_im = mag * jnp.sin(ai * dt)
    den = ar * ar + ai * ai
    nr = lb_re - 1.0
    coef_re = (nr * ar + lb_im * ai) / den
    coef_im = (lb_im * ar - nr * ai) / den
    bb_re = coef_re[..., None] * b_re - coef_im[..., None] * b_im
    bb_im = coef_re[..., None] * b_im + coef_im[..., None] * b_re
    gk = SSM_GROUPS // SSM_KB

    def diag_blocks(w):
        return jnp.stack([_block_diag(w[k * gk:(k + 1) * gk]) for k in range(SSM_KB)]).astype(bf16)

    bbre = diag_blocks(bb_re.transpose(0, 2, 1))
    bbim = diag_blocks(bb_im.transpose(0, 2, 1))
    ccre = diag_blocks(c_re.transpose(0, 2, 1))
    ccim = diag_blocks(c_im.transpose(0, 2, 1))
    return (lb_re.reshape(1, SSM_LANES), lb_im.reshape(1, SSM_LANES), bbre, bbim, ccre, ccim)


def kernel(x_prompt, x_sample, state_ret, state_ssm_re, state_ssm_im, state_lru, state_conv, w_in, ret_gn_g, ret_gn_b, ssm_a_re, ssm_a_im, ssm_log_dt, ssm_b_re, ssm_b_im, ssm_c_re, ssm_c_im, ssm_d, ssm_w_glu, lru_conv_w, lru_conv_b, lru_wa, lru_ba, lru_wx, lru_bx, lru_lambda, w_branch, w_out, ln1_g, ln1_b, moe_w_group, moe_b_group, moe_w_expert, moe_b_expert, moe_w1, moe_w3, moe_w2, ln2_g, ln2_b):
    x, xb = _to_rows(x_prompt, x_sample)
    rope = _rope_tables()
    tabs_p = _ret_tables(RET_SUB_T)
    tabs_s = _ret_tables(DEC_SEQ)
    row = lambda v: v.reshape(1, -1)
    w_bf = w_in.astype(bf16)

    outs = [[] for _ in range(10)]
    for l in range(DEPTH):

        s0 = state_ret.reshape(DEPTH, DEC_BATCH, 2, 2 * RET_DK, RET_DV)
        y_ret, ret_p, ret_s = _retention(xb, w_bf, rope, s0, l, tabs_p, tabs_s,
                                         row(ret_gn_g[l]), row(ret_gn_b[l]))

        sp = _ssm_params(ssm_a_re[l], ssm_a_im[l], ssm_log_dt[l], ssm_b_re[l], ssm_b_im[l],
                         ssm_c_re[l], ssm_c_im[l])
        y_ssm, re_p, im_p, re_s, im_s = _ssm(
            xb, w_bf, l, state_ssm_re[l].reshape(DEC_BATCH, SSM_LANES), state_ssm_im[l].reshape(DEC_BATCH, SSM_LANES),
            *sp, row(ssm_d[l]), ssm_w_glu[l].astype(bf16))

        conv0 = state_conv[l].transpose(1, 0, 2).reshape(LRU_HIST_S, BRANCH_W)
        y_lru, lru_p, conv_p, lru_s, conv_s = _lru(
            xb, w_bf, l, state_lru[l], conv0, lru_conv_w[l], row(lru_conv_b[l]),
            _block_diag(lru_wa[l]).astype(bf16), row(lru_ba[l]),
            _block_diag(lru_wx[l]).astype(bf16), row(lru_bx[l]), row(lru_lambda[l]))

        wr = jnp.zeros((D_MODEL, ROUTE_LANES), f32)
        wr = wr.at[:, 0:MOE_GROUPS].set(moe_w_group[l]).at[:, MOE_GROUPS:MOE_GROUPS + MOE_EXPERTS].set(moe_w_expert[l])
        br = jnp.zeros((1, ROUTE_LANES), f32)
        br = br.at[0, 0:MOE_GROUPS].set(moe_b_group[l]).at[0, MOE_GROUPS:MOE_GROUPS + MOE_EXPERTS].set(moe_b_expert[l])
        x1, route, route_t, cnt = _merge(y_ret, y_ssm, y_lru, xb, w_bf, l, x, w_branch[l].astype(bf16), w_out[l].astype(bf16),
                                row(ln1_g[l]), row(ln1_b[l]), wr.astype(bf16), br)

        pos, plan = _dispatch_plan(route_t, cnt)
        xs = _dispatch(pos, x1)
        ys = _moe(plan, xs, moe_w1, moe_w3, moe_w2, l)
        x, xb = _combine(pos, ys, x1, route, row(ln2_g[l]), row(ln2_b[l]), final=(l == DEPTH - 1))

        outs[0].append(ret_p.reshape(BATCH, RET_HEADS, RET_DK, RET_DV))
        outs[1].append(re_p.reshape(BATCH, SSM_GROUPS, SSM_STATE))
        outs[2].append(im_p.reshape(BATCH, SSM_GROUPS, SSM_STATE))
        outs[3].append(lru_p)
        outs[4].append(conv_p.reshape(CONV_W - 1, BATCH, BRANCH_W).transpose(1, 0, 2))
        outs[5].append(ret_s.reshape(DEC_BATCH, RET_HEADS, RET_DK, RET_DV))
        outs[6].append(re_s.reshape(DEC_BATCH, SSM_GROUPS, SSM_STATE))
        outs[7].append(im_s.reshape(DEC_BATCH, SSM_GROUPS, SSM_STATE))
        outs[8].append(lru_s)
        outs[9].append(conv_s.reshape(CONV_W - 1, DEC_BATCH, BRANCH_W).transpose(1, 0, 2))

    y_prompt, y_sample = x, xb
    return (y_prompt, y_sample) + tuple(jnp.stack(o) for o in outs)
```

```python
import functools

import jax
import jax.numpy as jnp
import numpy as np
from jax import lax
from jax.experimental import pallas as pl
from jax.experimental.pallas import tpu as pltpu

f32 = jnp.float32
bf16 = jnp.bfloat16

D_MODEL = 1024
BATCH = 8
SEQ = 2048
DEPTH = 2
DEC_BATCH = 128
DEC_SEQ = 8
PAST_LEN = 16384
BRANCH_W = 512
N_BRANCH = 3
RET_HEADS = 4
RET_DK = 64
RET_DV = 128
ROPE_BASE = 10000.0
SSM_GROUP = 16
SSM_GROUPS = 32
SSM_STATE = 64
SSM_LANES = SSM_GROUPS * SSM_STATE
LRU_BLOCKS = 8
LRU_BW = 64
CONV_W = 4
LRU_C = 8.0
MOE_GROUPS = 4
MOE_PER_GROUP = 8
MOE_EXPERTS = 32
MOE_TOPK = 2
MOE_HIDDEN = 512
DN_ALPHA = (2.0 * DEPTH) ** 0.25
LN_EPS = 1e-5
D_IN = 6144

V7X_SUBLANES = 8
V7X_LANES = 128
V7X_VMEM_LIMIT = 56 * 1024 * 1024

N_P = BATCH * SEQ
N_S = DEC_BATCH * DEC_SEQ
N_TOK = N_P + N_S
ROW_TILE = 1024
P_TILES = N_P // ROW_TILE
N_TILES = N_TOK // ROW_TILE
P_TC = ROW_TILE // BATCH
RET_SUB_T = 32
RET_SUB_R = RET_SUB_T * BATCH
RET_SUBS = ROW_TILE // RET_SUB_R
S_BLOCKS = DEC_BATCH // BATCH
S_BLOCK_R = DEC_SEQ * BATCH
MERGE_TILE = 512
MOE_TM = 512
N_PAIRS = N_TOK * MOE_TOPK
MOE_MAX_ITEMS = N_PAIRS // MOE_TM + MOE_EXPERTS - 1
DSP_TILE = 1024
CMB_TILE = 512
CMB_P_STEPS = N_P // CMB_TILE
ROUTE_LANES = 128


PROJ_RC = 256
RET_COLS, RET_BLK = 1536, 0
SSM_COLS, SSM_BLK = 512, 3
LRU_COLS, LRU_BLK = 1024, 2
GATE_COLS, GATE_BLK = 3072, 1


def _cparams(sem):
    return pltpu.CompilerParams(dimension_semantics=sem, vmem_limit_bytes=V7X_VMEM_LIMIT)


N_PLANES = D_MODEL // V7X_LANES


def _to_planes(ref, rows):
    for c in range(N_PLANES):
        ref[c] = rows[:, c * V7X_LANES:(c + 1) * V7X_LANES]


def _from_planes(ref):
    return jnp.concatenate([ref[c] for c in range(N_PLANES)], axis=1)


def _const_spec(block_shape, index):
    return pl.BlockSpec(block_shape, lambda *_: index, pipeline_mode=pl.Buffered(1))


def _dot(a, b):
    return jnp.dot(a, b, preferred_element_type=f32)


def _dot_nt(a, b):
    return lax.dot_general(a, b, (((1,), (1,)), ((), ())), preferred_element_type=f32)


def _dot_tn(a, b):
    return lax.dot_general(a, b, (((0,), (0,)), ((), ())), preferred_element_type=f32)


def _layer_norm_rows(x, g, b):
    mu = jnp.mean(x, -1, keepdims=True)
    xc = x - mu
    var = jnp.mean(xc * xc, -1, keepdims=True)
    return xc * lax.rsqrt(var + LN_EPS) * g + b


def _project_rows(xb_ref, w_ref, z_ref):
    for rc in range(ROW_TILE // PROJ_RC):
        rs = slice(rc * PROJ_RC, (rc + 1) * PROJ_RC)
        z_ref[rs, :] = _dot(xb_ref[rs, :], w_ref[0])


def _ret_block(q, k, v, g, cosb, sinb, mask_ref, qdec_ref, kdec_ref, cdec_ref, scat_ref, gng, gnb):
    rows = q.shape[0]
    lane_qk = lax.broadcasted_iota(jnp.int32, (rows, 2 * V7X_LANES), 1)
    first_half = (lane_qk & (RET_DK - 1)) < (RET_DK // 2)

    def rope(x):
        partner = jnp.where(first_half, pltpu.roll(x, 2 * V7X_LANES - RET_DK // 2, 1),
                            pltpu.roll(x, RET_DK // 2, 1))
        return x * cosb + partner * sinb

    q = rope(q)
    k = rope(k) * (RET_DK ** -0.5)
    kd = k * kdec_ref[...]
    lane = lax.broadcasted_iota(jnp.int32, (rows, V7X_LANES), 1)
    row_b = lax.broadcasted_iota(jnp.int32, (rows, V7X_LANES), 0) & (BATCH - 1)
    outs = []
    for p in range(2):
        qp = q[:, p * V7X_LANES:(p + 1) * V7X_LANES]
        kp = k[:, p * V7X_LANES:(p + 1) * V7X_LANES].astype(bf16)
        kdp = kd[:, p * V7X_LANES:(p + 1) * V7X_LANES]
        s_old = scat_ref[p]
        s_bf = s_old.astype(bf16)
        s_new = s_old * jnp.concatenate([cdec_ref[p]] * BATCH, axis=1)
        for hh in range(2):
            h = 2 * p + hh
            head_lanes = (lane >= RET_DK) if hh else (lane < RET_DK)
            qh = jnp.where(head_lanes, qp, 0.0).astype(bf16)
            kdh = jnp.where(head_lanes, kdp, 0.0).astype(bf16)
            vh = v[:, h * RET_DV:(h + 1) * RET_DV]
            vh_bf = vh.astype(bf16)
            sc = _dot_nt(qh, kp) * mask_ref[h]
            o = _dot(sc.astype(bf16), vh_bf)
            cross = _dot(qh, s_bf)
            oc = jnp.zeros((rows, RET_DV), f32)
            for b in range(BATCH):
                oc = oc + jnp.where(row_b == b, cross[:, b * RET_DV:(b + 1) * RET_DV], 0.0)
            o = o + oc * qdec_ref[h]
            vcat = jnp.concatenate([jnp.where(row_b == b, vh_bf, jnp.zeros_like(vh_bf))
                                    for b in range(BATCH)], axis=1)
            s_new = s_new + _dot_tn(kdh, vcat)
            mu = jnp.mean(o, -1, keepdims=True)
            oc2 = o - mu
            var = jnp.mean(oc2 * oc2, -1, keepdims=True)
            outs.append(oc2 * lax.rsqrt(var + LN_EPS))
        scat_ref[p] = s_new
    o = jnp.concatenate(outs, axis=1) * gng + gnb
    return jax.nn.silu(g) * o


def _per_step_rows(tab_ref, t0, steps):
    return jnp.concatenate(
        [jnp.broadcast_to(tab_ref[t0 + t:t0 + t + 1, :], (BATCH, tab_ref.shape[1])) for t in range(steps)], axis=0)


def _ret_kernel(xb_ref, w_ref, cos_ref, sin_ref, cos_s_ref, sin_s_ref, s0_ref,
                mask_p_ref, qdec_p_ref, kdec_p_ref, cdec_p_ref,
                mask_s_ref, qdec_s_ref, kdec_s_ref, cdec_s_ref,
                gng_ref, gnb_ref,
                y_ref, retp_ref, rets_ref, scat_ref, z_ref):
    i = pl.program_id(0)
    gng = gng_ref[...]
    gnb = gnb_ref[...]
    q_cols, k_cols = slice(0, 256), slice(256, 512)
    v_cols, g_cols = slice(512, 1024), slice(1024, 1536)

    @pl.when(i == 0)
    def _():
        scat_ref[...] = jnp.zeros_like(scat_ref)

    @pl.when(i <= P_TILES)
    def _():
        _project_rows(xb_ref, w_ref, z_ref)

    @pl.when(i < P_TILES)
    def _():
        for sc in range(RET_SUBS):
            r0 = sc * RET_SUB_R
            rs = slice(r0, r0 + RET_SUB_R)
            y_ref[rs, :] = _ret_block(
                z_ref[rs, q_cols], z_ref[rs, k_cols], z_ref[rs, v_cols], z_ref[rs, g_cols],
                _per_step_rows(cos_ref, sc * RET_SUB_T, RET_SUB_T), _per_step_rows(sin_ref, sc * RET_SUB_T, RET_SUB_T),
                mask_p_ref, qdec_p_ref, kdec_p_ref, cdec_p_ref, scat_ref, gng, gnb)

    @pl.when(i == P_TILES - 1)
    def _():
        for b in range(BATCH):
            for p in range(2):
                retp_ref[b, p] = scat_ref[p, :, b * RET_DV:(b + 1) * RET_DV]

    @pl.when(i >= P_TILES)
    def _():
        bb = i - P_TILES
        for b in range(BATCH):
            for p in range(2):
                scat_ref[p, :, b * RET_DV:(b + 1) * RET_DV] = s0_ref[0, b, p]

        def rows_of(cols):
            return jnp.concatenate(
                [z_ref[pl.ds(pl.multiple_of(t * DEC_BATCH + bb * BATCH, BATCH), BATCH), cols]
                 for t in range(DEC_SEQ)], axis=0)

        y = _ret_block(
            rows_of(q_cols), rows_of(k_cols), rows_of(v_cols),
            rows_of(g_cols), _per_step_rows(cos_s_ref, 0, DEC_SEQ), _per_step_rows(sin_s_ref, 0, DEC_SEQ),
            mask_s_ref, qdec_s_ref, kdec_s_ref, cdec_s_ref, scat_ref, gng, gnb)
        for t in range(DEC_SEQ):
            y_ref[pl.ds(pl.multiple_of(t * DEC_BATCH + bb * BATCH, BATCH), BATCH), :] = (
                y[t * BATCH:(t + 1) * BATCH, :])
        for b in range(BATCH):
            for p in range(2):
                rets_ref[b, p] = scat_ref[p, :, b * RET_DV:(b + 1) * RET_DV]


def _ret_tables(tc):
    rows = tc * BATCH
    nf = np.float32
    log_g = np.log1p(-np.exp2(nf(-5.0) - np.arange(RET_HEADS, dtype=nf))).astype(nf)
    t_idx = (np.arange(rows) // BATCH).astype(nf)
    b_idx = np.arange(rows) % BATCH
    rel = t_idx[:, None] - t_idx[None, :]
    same = b_idx[:, None] == b_idx[None, :]
    decay = np.exp(log_g[:, None, None] * np.maximum(rel, nf(0.0)))
    mask = np.where((rel >= 0) & same, decay, nf(0.0))
    qdec = np.exp(log_g[:, None] * (t_idx[None, :] + nf(1.0)))
    qdec = np.broadcast_to(qdec[:, :, None], (RET_HEADS, rows, RET_DV))
    kdec = np.exp(log_g[:, None] * (nf(tc - 1.0) - t_idx[None, :]))
    kdec = np.broadcast_to(kdec.T[:, :, None], (rows, RET_HEADS, RET_DK)).reshape(rows, RET_HEADS * RET_DK)
    cdec = np.exp(log_g * nf(tc))
    cdec = np.broadcast_to(cdec[:, None, None], (RET_HEADS, RET_DK, RET_DV)).reshape(2, 2 * RET_DK, RET_DV)
    return tuple(jnp.asarray(np.ascontiguousarray(a), dtype=f32) for a in (mask, qdec, kdec, cdec))


def _rope_tables():
    half = RET_DK // 2
    inv = ROPE_BASE ** (-jnp.arange(half, dtype=f32) / half)
    pos_p = jnp.arange(SEQ, dtype=f32)
    pos_s = PAST_LEN + jnp.arange(DEC_SEQ, dtype=f32)

    def tab(pos):
        ang = pos[:, None] * inv[None, :]
        cos = jnp.cos(ang)
        sin = jnp.sin(ang)
        cos_h = jnp.concatenate([cos, cos], axis=1)
        sin_h = jnp.concatenate([-sin, sin], axis=1)
        return jnp.tile(cos_h, (1, RET_HEADS)), jnp.tile(sin_h, (1, RET_HEADS))

    return tab(pos_p) + tab(pos_s)


def _retention(xb, w_bf, rope, s0_s, layer, tabs_p, tabs_s, gng, gnb):
    n_steps = P_TILES + S_BLOCKS
    tile = lambda i: jnp.minimum(i, P_TILES)
    sblk = lambda i: jnp.maximum(i - P_TILES, 0)
    full = lambda a: pl.BlockSpec(a.shape, lambda i, _n=a.ndim: (0,) * _n)
    state_blk = (BATCH, 2, 2 * RET_DK, RET_DV)
    cos_p, sin_p, cos_s, sin_s = rope
    ptile = lambda i: jnp.minimum(i, P_TILES - 1)
    ins = [xb, w_bf, cos_p, sin_p, cos_s, sin_s, s0_s, *tabs_p, *tabs_s, gng, gnb]
    in_specs = [
        pl.BlockSpec((ROW_TILE, D_MODEL), lambda i: (tile(i), 0)),
        _const_spec((1, D_MODEL, RET_COLS), (layer, 0, RET_BLK)),
        pl.BlockSpec((P_TC, 256), lambda i: (ptile(i), 0)),
        pl.BlockSpec((P_TC, 256), lambda i: (ptile(i), 0)),
        full(cos_s), full(sin_s),
        pl.BlockSpec((1,) + state_blk, lambda i: (layer, sblk(i), 0, 0, 0)),
    ] + [full(a) for a in (*tabs_p, *tabs_s, gng, gnb)]
    return pl.pallas_call(
        _ret_kernel,
        grid=(n_steps,),
        in_specs=in_specs,
        out_specs=[pl.BlockSpec((ROW_TILE, BRANCH_W), lambda i: (tile(i), 0)),
                   pl.BlockSpec(state_blk, lambda i: (0, 0, 0, 0)),
                   pl.BlockSpec(state_blk, lambda i: (sblk(i), 0, 0, 0))],
        out_shape=[jax.ShapeDtypeStruct((N_TOK, BRANCH_W), f32),
                   jax.ShapeDtypeStruct((BATCH, 2, 2 * RET_DK, RET_DV), f32),
                   jax.ShapeDtypeStruct((DEC_BATCH, 2, 2 * RET_DK, RET_DV), f32)],
        scratch_shapes=[pltpu.VMEM((2, 2 * RET_DK, BATCH * RET_DV), f32),
                        pltpu.VMEM((ROW_TILE, RET_COLS), f32)],
        compiler_params=_cparams(("arbitrary",)),
        name="retention",
    )(*ins)


SSM_LB = 512
SSM_RC = 256
SSM_KB = 2
SSM_KB_U = BRANCH_W // SSM_KB
SSM_KB_H = SSM_LANES // SSM_KB


def _ssm_scan(bre_ref, bim_ref, lre_ref, lim_ref, h_re0, h_im0, row0, nb_rows, steps, lb, unroll):
    ls = slice(lb * SSM_LB, (lb + 1) * SSM_LB)
    a_re = jnp.broadcast_to(lre_ref[:, ls], (V7X_SUBLANES, SSM_LB))
    a_im = jnp.broadcast_to(lim_ref[:, ls], (V7X_SUBLANES, SSM_LB))

    def step(t, carry):
        h_re, h_im = carry
        r = pl.multiple_of(row0 + t * nb_rows, V7X_SUBLANES)
        n_re = a_re * h_re - a_im * h_im + bre_ref[pl.ds(r, V7X_SUBLANES), ls]
        n_im = a_re * h_im + a_im * h_re + bim_ref[pl.ds(r, V7X_SUBLANES), ls]
        bre_ref[pl.ds(r, V7X_SUBLANES), ls] = n_re
        bim_ref[pl.ds(r, V7X_SUBLANES), ls] = n_im
        return n_re, n_im

    return lax.fori_loop(0, steps, step, (h_re0, h_im0), unroll=unroll)


def _ssm_kernel(xb_ref, w_ref, h0re_ref, h0im_ref, lre_ref, lim_ref, bbre_ref, bbim_ref, ccre_ref, ccim_ref,
                d_ref, wglu_ref,
                y_ref, pre_ref, pim_ref, sre_ref, sim_ref,
                bre_ref, bim_ref, hre_ref, him_ref, zs_ref):
    i = pl.program_id(0)
    _project_rows(xb_ref, w_ref, zs_ref)
    for rc in range(ROW_TILE // SSM_RC):
        rs = slice(rc * SSM_RC, (rc + 1) * SSM_RC)
        ub = zs_ref[rs, :].astype(bf16)
        for k in range(SSM_KB):
            uk = ub[:, k * SSM_KB_U:(k + 1) * SSM_KB_U]
            hs = slice(k * SSM_KB_H, (k + 1) * SSM_KB_H)
            bre_ref[rs, hs] = _dot(uk, bbre_ref[k])
            bim_ref[rs, hs] = _dot(uk, bbim_ref[k])

    @pl.when(i == 0)
    def _():
        hre_ref[...] = jnp.zeros_like(hre_ref)
        him_ref[...] = jnp.zeros_like(him_ref)

    @pl.when(i < P_TILES)
    def _():
        for lb in range(SSM_LANES // SSM_LB):
            ls = slice(lb * SSM_LB, (lb + 1) * SSM_LB)
            h_re, h_im = _ssm_scan(bre_ref, bim_ref, lre_ref, lim_ref, hre_ref[:, ls], him_ref[:, ls],
                                   0, BATCH, P_TC, lb, 8)
            hre_ref[:, ls] = h_re
            him_ref[:, ls] = h_im
        pre_ref[...] = hre_ref[...]
        pim_ref[...] = him_ref[...]

    @pl.when(i >= P_TILES)
    def _():
        def per_row_tile(rt, c):
            r0 = pl.multiple_of(rt * V7X_SUBLANES, V7X_SUBLANES)
            for lb in range(SSM_LANES // SSM_LB):
                ls = slice(lb * SSM_LB, (lb + 1) * SSM_LB)
                _ssm_scan(bre_ref, bim_ref, lre_ref, lim_ref,
                          h0re_ref[pl.ds(r0, V7X_SUBLANES), ls], h0im_ref[pl.ds(r0, V7X_SUBLANES), ls],
                          r0, DEC_BATCH, DEC_SEQ, lb, True)
            return c

        lax.fori_loop(0, DEC_BATCH // V7X_SUBLANES, per_row_tile, 0)
        last = (DEC_SEQ - 1) * DEC_BATCH
        sre_ref[...] = bre_ref[last:last + DEC_BATCH, :]
        sim_ref[...] = bim_ref[last:last + DEC_BATCH, :]

    for rc in range(ROW_TILE // SSM_RC):
        rs = slice(rc * SSM_RC, (rc + 1) * SSM_RC)
        ch = []
        for k in range(SSM_KB):
            hs = slice(k * SSM_KB_H, (k + 1) * SSM_KB_H)
            ch.append(_dot(bre_ref[rs, hs].astype(bf16), ccre_ref[k]) - _dot(bim_ref[rs, hs].astype(bf16), ccim_ref[k]))
        y = jnp.concatenate(ch, axis=1) + d_ref[...] * zs_ref[rs, :]
        zz = jax.nn.gelu(y)
        y_ref[rs, :] = zz * jax.nn.sigmoid(_dot(zz.astype(bf16), wglu_ref[...]))


def _ssm(xb, w_bf, layer, h0re, h0im, lre, lim, bbre, bbim, ccre, ccim, dvec, wglu):
    full = lambda a: _const_spec(a.shape, (0,) * a.ndim)
    consts = (h0re, h0im, lre, lim, bbre, bbim, ccre, ccim, dvec, wglu)
    return pl.pallas_call(
        _ssm_kernel,
        grid=(N_TILES,),
        in_specs=[pl.BlockSpec((ROW_TILE, D_MODEL), lambda i: (i, 0)),
                  _const_spec((1, D_MODEL, SSM_COLS), (layer, 0, SSM_BLK))] + [full(a) for a in consts],
        out_specs=[pl.BlockSpec((ROW_TILE, BRANCH_W), lambda i: (i, 0)),
                   pl.BlockSpec((BATCH, SSM_LANES), lambda i: (0, 0)),
                   pl.BlockSpec((BATCH, SSM_LANES), lambda i: (0, 0)),
                   pl.BlockSpec((DEC_BATCH, SSM_LANES), lambda i: (0, 0)),
                   pl.BlockSpec((DEC_BATCH, SSM_LANES), lambda i: (0, 0))],
        out_shape=[jax.ShapeDtypeStruct((N_TOK, BRANCH_W), f32),
                   jax.ShapeDtypeStruct((BATCH, SSM_LANES), f32),
                   jax.ShapeDtypeStruct((BATCH, SSM_LANES), f32),
                   jax.ShapeDtypeStruct((DEC_BATCH, SSM_LANES), f32),
                   jax.ShapeDtypeStruct((DEC_BATCH, SSM_LANES), f32)],
        scratch_shapes=[pltpu.VMEM((ROW_TILE, SSM_LANES), f32), pltpu.VMEM((ROW_TILE, SSM_LANES), f32),
                        pltpu.VMEM((BATCH, SSM_LANES), f32), pltpu.VMEM((BATCH, SSM_LANES), f32),
                        pltpu.VMEM((ROW_TILE, SSM_COLS), f32)],
        compiler_params=_cparams(("arbitrary",)),
        name="ssm",
    )(xb, w_bf, *consts)


LRU_HIST_P = (CONV_W - 1) * BATCH
LRU_HIST_S = (CONV_W - 1) * DEC_BATCH


def _lru_gates(xe_ref, nb_rows, cw_ref, cb_ref, wa_ref, ba_ref, wx_ref, bx_ref, lam_ref):
    xc = cb_ref[...] + xe_ref[0:ROW_TILE, :] * cw_ref[0:1, :]
    for j in range(1, CONV_W):
        xc = xc + xe_ref[j * nb_rows:j * nb_rows + ROW_TILE, :] * cw_ref[j:j + 1, :]
    xcb = xc.astype(bf16)
    r = jax.nn.sigmoid(_dot(xcb, wa_ref[...]) + ba_ref[...])
    ig = jax.nn.sigmoid(_dot(xcb, wx_ref[...]) + bx_ref[...])
    log_a = -LRU_C * r * jax.nn.softplus(-lam_ref[...])
    a = jnp.exp(log_a)
    b = jnp.sqrt(-jnp.tanh(log_a) * (jnp.exp(2.0 * log_a) + 1.0)) * (ig * xc)
    return a, b


def _lru_kernel(xb_ref, w_ref, h0_ref, conv0_ref, cw_ref, cb_ref, wa_ref, ba_ref, wx_ref, bx_ref, lam_ref,
                y_ref, hp_ref, convp_ref, hs_ref, convs_ref,
                xe_ref, a_ref, b_ref, hc_ref, z_ref):
    i = pl.program_id(0)
    params = (cw_ref, cb_ref, wa_ref, ba_ref, wx_ref, bx_ref, lam_ref)
    _project_rows(xb_ref, w_ref, z_ref)
    zx_ref = z_ref.at[:, 0:BRANCH_W]
    zg_ref = z_ref.at[:, BRANCH_W:2 * BRANCH_W]

    @pl.when(i == 0)
    def _():
        xe_ref[0:LRU_HIST_P, :] = jnp.zeros((LRU_HIST_P, BRANCH_W), f32)
        hc_ref[...] = jnp.zeros_like(hc_ref)

    @pl.when(i < P_TILES)
    def _():
        xe_ref[LRU_HIST_P:LRU_HIST_P + ROW_TILE, :] = zx_ref[...]
        a, b = _lru_gates(xe_ref, BATCH, *params)
        a_ref[...] = a
        b_ref[...] = b
        hist = xe_ref[ROW_TILE:ROW_TILE + LRU_HIST_P, :]
        xe_ref[0:LRU_HIST_P, :] = hist
        convp_ref[...] = hist

        def step(t, h):
            r = pl.multiple_of(t * BATCH, BATCH)
            h = a_ref[pl.ds(r, BATCH), :] * h + b_ref[pl.ds(r, BATCH), :]
            b_ref[pl.ds(r, BATCH), :] = h
            return h

        h = lax.fori_loop(0, P_TC, step, hc_ref[...], unroll=8)
        hc_ref[...] = h
        hp_ref[...] = h

    @pl.when(i >= P_TILES)
    def _():
        xe_ref[0:LRU_HIST_S, :] = conv0_ref[...]
        xe_ref[LRU_HIST_S:LRU_HIST_S + ROW_TILE, :] = zx_ref[...]
        a, b = _lru_gates(xe_ref, DEC_BATCH, *params)
        a_ref[...] = a
        b_ref[...] = b
        convs_ref[...] = xe_ref[ROW_TILE:ROW_TILE + LRU_HIST_S, :]

        def per_row_tile(rt, c):
            r0 = pl.multiple_of(rt * V7X_SUBLANES, V7X_SUBLANES)
            h = h0_ref[pl.ds(r0, V7X_SUBLANES), :]
            for t in range(DEC_SEQ):
                r = pl.multiple_of(t * DEC_BATCH + r0, V7X_SUBLANES)
                h = a_ref[pl.ds(r, V7X_SUBLANES), :] * h + b_ref[pl.ds(r, V7X_SUBLANES), :]
                b_ref[pl.ds(r, V7X_SUBLANES), :] = h
            return c

        lax.fori_loop(0, DEC_BATCH // V7X_SUBLANES, per_row_tile, 0)
        last = (DEC_SEQ - 1) * DEC_BATCH
        hs_ref[...] = b_ref[last:last + DEC_BATCH, :]

    y_ref[...] = b_ref[...] * jax.nn.gelu(zg_ref[...])


def _lru(xb, w_bf, layer, h0, conv0, cw, cb, wa, ba, wx, bx, lam):
    full = lambda a: _const_spec(a.shape, (0,) * a.ndim)
    consts = (h0, conv0, cw, cb, wa, ba, wx, bx, lam)
    return pl.pallas_call(
        _lru_kernel,
        grid=(N_TILES,),
        in_specs=[pl.BlockSpec((ROW_TILE, D_MODEL), lambda i: (i, 0)),
                  _const_spec((1, D_MODEL, LRU_COLS), (layer, 0, LRU_BLK))] + [full(a) for a in consts],
        out_specs=[pl.BlockSpec((ROW_TILE, BRANCH_W), lambda i: (i, 0)),
                   pl.BlockSpec((BATCH, BRANCH_W), lambda i: (0, 0)),
                   pl.BlockSpec((LRU_HIST_P, BRANCH_W), lambda i: (0, 0)),
                   pl.BlockSpec((DEC_BATCH, BRANCH_W), lambda i: (0, 0)),
                   pl.BlockSpec((LRU_HIST_S, BRANCH_W), lambda i: (0, 0))],
        out_shape=[jax.ShapeDtypeStruct((N_TOK, BRANCH_W), f32),
                   jax.ShapeDtypeStruct((BATCH, BRANCH_W), f32),
                   jax.ShapeDtypeStruct((LRU_HIST_P, BRANCH_W), f32),
                   jax.ShapeDtypeStruct((DEC_BATCH, BRANCH_W), f32),
                   jax.ShapeDtypeStruct((LRU_HIST_S, BRANCH_W), f32)],
        scratch_shapes=[pltpu.VMEM((ROW_TILE + LRU_HIST_S, BRANCH_W), f32),
                        pltpu.VMEM((ROW_TILE, BRANCH_W), f32), pltpu.VMEM((ROW_TILE, BRANCH_W), f32),
                        pltpu.VMEM((BATCH, BRANCH_W), f32),
                        pltpu.VMEM((ROW_TILE, LRU_COLS), f32)],
        compiler_params=_cparams(("arbitrary",)),
        name="lru",
    )(xb, w_bf, *consts)


def _merge_kernel(yr_ref, ys_ref, yl_ref, xb_ref, wg_ref, x_ref, wb_ref, wo_ref, g_ref, b_ref, wr_ref, br_ref,
                  x1_ref, route_ref, route_t_ref, cnt_out_ref, cnt_ref):
    merged = jnp.zeros((MERGE_TILE, D_MODEL), f32)
    xb = xb_ref[...]
    for n, y_ref in enumerate((yr_ref, ys_ref, yl_ref)):
        proj = _dot(y_ref[...].astype(bf16), wb_ref[n])
        gate = jax.nn.sigmoid(_dot(xb, wg_ref[0, :, n * D_MODEL:(n + 1) * D_MODEL]))
        merged = merged + gate * proj
    mix = _dot(merged.astype(bf16), wo_ref[...])
    x1 = _layer_norm_rows(DN_ALPHA * x_ref[...] + mix, g_ref[...], b_ref[...])
    _to_planes(x1_ref, x1)

    logits = _dot(x1.astype(bf16), wr_ref[...]) + br_ref[...]
    lane = lax.broadcasted_iota(jnp.int32, (MERGE_TILE, ROUTE_LANES), 1).astype(f32)
    big = jnp.float32(ROUTE_LANES)
    neg = jnp.float32(-jnp.inf)
    is_g = lane < MOE_GROUPS
    lg = jnp.where(is_g, logits, neg)
    mg = jnp.max(lg, -1, keepdims=True)
    gsel = jnp.min(jnp.where(lg == mg, lane, big), -1, keepdims=True)
    sum_g = jnp.sum(jnp.where(is_g, jnp.exp(lg - mg), 0.0), -1, keepdims=True)
    pg_sel = 1.0 / sum_g
    lo = MOE_GROUPS + gsel * MOE_PER_GROUP
    is_e = jnp.abs(lane - lo - 0.5 * (MOE_PER_GROUP - 1)) < 0.5 * MOE_PER_GROUP
    le = jnp.where(is_e, logits, neg)
    me = jnp.max(le, -1, keepdims=True)
    ex = jnp.where(is_e, jnp.exp(le - me), 0.0)
    pe = jnp.where(is_e, ex / jnp.sum(ex, -1, keepdims=True), -1.0)
    v1 = jnp.max(pe, -1, keepdims=True)
    i1 = jnp.min(jnp.where(pe == v1, lane, big), -1, keepdims=True)
    pe2 = jnp.where(lane == i1, -1.0, pe)
    v2 = jnp.max(pe2, -1, keepdims=True)
    i2 = jnp.min(jnp.where(pe2 == v2, lane, big), -1, keepdims=True)
    vsum = v1 + v2
    w1 = pg_sel * v1 / vsum
    w2 = pg_sel * v2 / vsum
    e1 = i1 - MOE_GROUPS
    e2 = i2 - MOE_GROUPS

    @pl.when(pl.program_id(0) == 0)
    def _():
        cnt_ref[...] = jnp.zeros_like(cnt_ref)

    oh1 = lane == e1
    oh2 = lane == e2
    ohs = jnp.where(oh1, 1.0, jnp.where(oh2, 1.0, 0.0))
    r_i = lax.broadcasted_iota(jnp.int32, (MERGE_TILE, MERGE_TILE), 0)
    c_i = lax.broadcasted_iota(jnp.int32, (MERGE_TILE, MERGE_TILE), 1)
    strict_lower = jnp.where(c_i < r_i, 1.0, 0.0).astype(bf16)
    before = _dot(strict_lower, ohs.astype(bf16)) + cnt_ref[0:1, :]
    rank1 = jnp.sum(jnp.where(oh1, before, 0.0), -1, keepdims=True)
    rank2 = jnp.sum(jnp.where(oh2, before, 0.0), -1, keepdims=True)
    cnt_ref[0:1, :] = cnt_ref[0:1, :] + jnp.sum(ohs, 0, keepdims=True)
    cnt_out_ref[...] = cnt_ref[...]

    route = jnp.zeros((MERGE_TILE, ROUTE_LANES), f32)
    for k, val in enumerate((e1, e2, w1, w2, rank1, rank2)):
        route = jnp.where(lane == k, val, route)
    route_ref[...] = route
    route_t_ref[...] = route.T[0:V7X_SUBLANES, :]


def _merge(y_ret, y_ssm, y_lru, xb, w_bf, layer, x, wb, wo, g, b, wr, br):
    full = lambda a: _const_spec(a.shape, (0,) * a.ndim)
    row = lambda w: pl.BlockSpec((MERGE_TILE, w), lambda i: (i, 0))
    consts = (wb, wo, g, b, wr, br)
    return pl.pallas_call(
        _merge_kernel,
        grid=(N_TOK // MERGE_TILE,),
        in_specs=[row(BRANCH_W), row(BRANCH_W), row(BRANCH_W), row(D_MODEL),
                  _const_spec((1, D_MODEL, GATE_COLS), (layer, 0, GATE_BLK)),
                  row(D_MODEL)] + [full(a) for a in consts],
        out_specs=[pl.BlockSpec((N_PLANES, MERGE_TILE, V7X_LANES), lambda i: (0, i, 0)), row(ROUTE_LANES),
                   pl.BlockSpec((V7X_SUBLANES, MERGE_TILE), lambda i: (0, i)),
                   pl.BlockSpec((V7X_SUBLANES, ROUTE_LANES), lambda i: (0, 0))],
        out_shape=[jax.ShapeDtypeStruct((N_PLANES, N_TOK, V7X_LANES), f32),
                   jax.ShapeDtypeStruct((N_TOK, ROUTE_LANES), f32),
                   jax.ShapeDtypeStruct((V7X_SUBLANES, N_TOK), f32),
                   jax.ShapeDtypeStruct((V7X_SUBLANES, ROUTE_LANES), f32)],
        scratch_shapes=[pltpu.VMEM((V7X_SUBLANES, ROUTE_LANES), f32)],
        compiler_params=_cparams(("arbitrary",)),
        name="merge",
    )(y_ret, y_ssm, y_lru, xb, w_bf, x, *consts)


def _dispatch_kernel(pos0_ref, pos1_ref, x1_ref, xs_hbm, sem):
    base = pl.program_id(0) * DSP_TILE

    def row_copy(r, dst_row):
        return pltpu.make_async_copy(x1_ref.at[:, r, :], xs_hbm.at[dst_row], sem.at[0])

    def issue(r, c):
        n = base + r
        for k in range(MOE_TOPK):
            row_copy(r, (pos0_ref, pos1_ref)[k][n]).start(priority=k % 2)
        return c

    lax.fori_loop(0, DSP_TILE, issue, 0, unroll=8)
    for _ in range(MOE_TOPK):
        pltpu.make_async_copy(x1_ref, x1_ref, sem.at[0]).wait()


def _dispatch(pos, x1p):
    grid_spec = pltpu.PrefetchScalarGridSpec(
        num_scalar_prefetch=MOE_TOPK,
        grid=(N_TOK // DSP_TILE,),
        in_specs=[pl.BlockSpec((N_PLANES, DSP_TILE, V7X_LANES), lambda s, *_: (0, s, 0))],
        out_specs=pl.BlockSpec(memory_space=pl.ANY),
        scratch_shapes=[pltpu.SemaphoreType.DMA((1,))],
    )
    return pl.pallas_call(
        _dispatch_kernel,
        grid_spec=grid_spec,
        out_shape=jax.ShapeDtypeStruct((N_PAIRS, N_PLANES, V7X_LANES), f32),
        compiler_params=_cparams(("arbitrary",)),
        name="dispatch",
    )(*pos, x1p)


def _moe_kernel(layer, wt_ref, we_ref, wlo_ref, whi_ref, wfirst_ref, wlast_ref, wefirst_ref, weslot_ref,
                wenext_ref, nw_ref,
                xs_hbm, w1_hbm, w3_hbm, w2_hbm, ys_hbm,
                xin, yout, sem_in, sem_out, wst1, wst3, wst2, sem_w, w1b, w3b, w2b):
    w = pl.program_id(0)
    n_items = nw_ref[0]

    def weight_copies(expert, slot):
        return [pltpu.make_async_copy(src.at[layer, expert], dst.at[slot], sem_w.at[slot])
                for src, dst in ((w1_hbm, wst1), (w3_hbm, wst3), (w2_hbm, wst2))]

    def in_copy(item, slot, c):
        r0 = pl.multiple_of(wt_ref[item] * MOE_TM, MOE_TM)
        return pltpu.make_async_copy(xs_hbm.at[pl.ds(r0, MOE_TM), c, :], xin.at[slot, c], sem_in.at[slot])

    def out_copy(tile, slot, c):
        r0 = pl.multiple_of(tile * MOE_TM, MOE_TM)
        return pltpu.make_async_copy(yout.at[slot, c], ys_hbm.at[pl.ds(r0, MOE_TM), c, :], sem_out.at[slot])

    @pl.when(w == 0)
    def _():
        for cp in weight_copies(we_ref[0], 0):
            cp.start()
        for c in range(N_PLANES):
            in_copy(0, 0, c).start(priority=1)

    @pl.when(w + 1 < n_items)
    def _():
        for c in range(N_PLANES):
            in_copy(w + 1, (w + 1) % 2, c).start(priority=1)

    @pl.when(w < n_items)
    def _():
        slot = w % 2
        tile = wt_ref[w]
        oslot = tile % 2
        for c in range(N_PLANES):
            in_copy(w, slot, c).wait()

        @pl.when(wefirst_ref[w] == 1)
        def _():
            wslot = weslot_ref[w]
            for cp in weight_copies(we_ref[w], wslot):
                cp.wait()
            w1b[...] = wst1[wslot].astype(bf16)
            w3b[...] = wst3[wslot].astype(bf16)
            w2b[...] = wst2[wslot].astype(bf16)

            @pl.when(wenext_ref[w] >= 0)
            def _():
                for cp in weight_copies(wenext_ref[w], 1 - wslot):
                    cp.start()

        xt = _from_planes(xin.at[slot]).astype(bf16)
        h = jax.nn.silu(_dot(xt, w1b[...])) * _dot(xt, w3b[...])
        res = _dot(h.astype(bf16), w2b[...])
        row = lax.broadcasted_iota(jnp.int32, (MOE_TM, D_MODEL), 0)
        mine = jnp.where(row >= wlo_ref[w], row, MOE_TM) < whi_ref[w]

        @pl.when(wfirst_ref[w] == 1)
        def _():
            @pl.when(tile >= 2)
            def _():
                for c in range(N_PLANES):
                    out_copy(0, oslot, c).wait()

            _to_planes(yout.at[oslot], jnp.where(mine, res, 0.0))

        @pl.when(wfirst_ref[w] == 0)
        def _():
            _to_planes(yout.at[oslot], jnp.where(mine, res, _from_planes(yout.at[oslot])))

        @pl.when(wlast_ref[w] == 1)
        def _():
            for c in range(N_PLANES):
                out_copy(tile, oslot, c).start()

        @pl.when(w == n_items - 1)
        def _():
            for c in range(N_PLANES):
                out_copy(0, oslot, c).wait()

            @pl.when(tile >= 1)
            def _():
                for c in range(N_PLANES):
                    out_copy(0, 1 - oslot, c).wait()


def _moe(plan, xs, w1, w3, w2, layer):
    grid_spec = pltpu.PrefetchScalarGridSpec(
        num_scalar_prefetch=len(plan),
        grid=(MOE_MAX_ITEMS,),
        in_specs=[pl.BlockSpec(memory_space=pl.ANY)] * 4,
        out_specs=pl.BlockSpec(memory_space=pl.ANY),
        scratch_shapes=[pltpu.VMEM((2, N_PLANES, MOE_TM, V7X_LANES), f32),
                        pltpu.VMEM((2, N_PLANES, MOE_TM, V7X_LANES), f32),
                        pltpu.SemaphoreType.DMA((2,)), pltpu.SemaphoreType.DMA((2,)),
                        pltpu.VMEM((2, D_MODEL, MOE_HIDDEN), f32), pltpu.VMEM((2, D_MODEL, MOE_HIDDEN), f32),
                        pltpu.VMEM((2, MOE_HIDDEN, D_MODEL), f32), pltpu.SemaphoreType.DMA((2,)),
                        pltpu.VMEM((D_MODEL, MOE_HIDDEN), bf16), pltpu.VMEM((D_MODEL, MOE_HIDDEN), bf16),
                        pltpu.VMEM((MOE_HIDDEN, D_MODEL), bf16)],
    )
    return pl.pallas_call(
        functools.partial(_moe_kernel, layer),
        grid_spec=grid_spec,
        out_shape=jax.ShapeDtypeStruct((N_PAIRS, N_PLANES, V7X_LANES), f32),
        compiler_params=_cparams(("arbitrary",)),
        name="moe",
    )(*plan, xs, w1, w3, w2)


def _combine_kernel(final, pos0_ref, pos1_ref, ys_hbm, x1_ref, route_ref, g_ref, b_ref, *rest):
    if final:
        yp_ref, ysm_ref, buf, sem, t3_ref = rest
    else:
        o_ref, ob_ref, buf, sem = rest
    s = pl.program_id(0)
    nsteps = pl.num_programs(0)

    def gather_copy(row, slot, k, r):
        return pltpu.make_async_copy(ys_hbm.at[row], buf.at[slot, k, :, r, :], sem.at[slot])

    def issue(tile, slot):
        def body(r, c):
            n = tile * CMB_TILE + r
            for k in range(MOE_TOPK):
                gather_copy((pos0_ref, pos1_ref)[k][n], slot, k, r).start(priority=k % 2)
            return c

        lax.fori_loop(0, CMB_TILE, body, 0, unroll=8)

    @pl.when(s == 0)
    def _():
        issue(0, 0)

    @pl.when(s + 1 < nsteps)
    def _():
        issue(s + 1, (s + 1) % 2)

    slot = s % 2
    for k in range(MOE_TOPK):
        pltpu.make_async_copy(buf.at[slot, k], buf.at[slot, k], sem.at[slot]).wait()
    route = route_ref[...]
    moe = route[:, 2:3] * _from_planes(buf.at[slot, 0]) + route[:, 3:4] * _from_planes(buf.at[slot, 1])
    y = _layer_norm_rows(DN_ALPHA * _from_planes(x1_ref) + moe, g_ref[...], b_ref[...])
    if not final:
        o_ref[...] = y
        ob_ref[...] = y.astype(bf16)
        return

    @pl.when(s < CMB_P_STEPS)
    def _():
        t3_ref[...] = y.reshape(CMB_TILE // BATCH, BATCH, D_MODEL)
        for b in range(BATCH):
            yp_ref[b] = t3_ref[:, b, :]

    steps_per_tile = CMB_TILE // DEC_BATCH
    for q in range(N_S // CMB_TILE):
        @pl.when(s == CMB_P_STEPS + q)
        def _():
            for h in range(steps_per_tile):
                ysm_ref[:, q * steps_per_tile + h, :] = y[h * DEC_BATCH:(h + 1) * DEC_BATCH, :]


def _combine(pos, ys, x1, route, g, b, final):
    if final:
        out_specs = [pl.BlockSpec((BATCH, CMB_TILE // BATCH, D_MODEL),
                                  lambda s, *_: (0, jnp.minimum(s, CMB_P_STEPS - 1), 0)),
                     pl.BlockSpec((DEC_BATCH, DEC_SEQ, D_MODEL), lambda s, *_: (0, 0, 0))]
        out_shape = [jax.ShapeDtypeStruct((BATCH, SEQ, D_MODEL), f32),
                     jax.ShapeDtypeStruct((DEC_BATCH, DEC_SEQ, D_MODEL), f32)]
        extra = [pltpu.VMEM((CMB_TILE // BATCH, BATCH, D_MODEL), f32)]
    else:
        out_specs = [pl.BlockSpec((CMB_TILE, D_MODEL), lambda s, *_: (s, 0))] * 2
        out_shape = [jax.ShapeDtypeStruct((N_TOK, D_MODEL), f32), jax.ShapeDtypeStruct((N_TOK, D_MODEL), bf16)]
        extra = []
    grid_spec = pltpu.PrefetchScalarGridSpec(
        num_scalar_prefetch=MOE_TOPK,
        grid=(N_TOK // CMB_TILE,),
        in_specs=[pl.BlockSpec(memory_space=pl.ANY),
                  pl.BlockSpec((N_PLANES, CMB_TILE, V7X_LANES), lambda s, *_: (0, s, 0)),
                  pl.BlockSpec((CMB_TILE, ROUTE_LANES), lambda s, *_: (s, 0)),
                  pl.BlockSpec((1, D_MODEL), lambda s, *_: (0, 0)),
                  pl.BlockSpec((1, D_MODEL), lambda s, *_: (0, 0))],
        out_specs=out_specs,
        scratch_shapes=[pltpu.VMEM((2, MOE_TOPK, N_PLANES, CMB_TILE, V7X_LANES), f32),
                        pltpu.SemaphoreType.DMA((2,))] + extra,
    )
    return pl.pallas_call(
        functools.partial(_combine_kernel, final),
        grid_spec=grid_spec,
        out_shape=out_shape,
        compiler_params=_cparams(("arbitrary",)),
        name="combine_out" if final else "combine",
    )(*pos, ys, x1, route, g, b)


def _to_rows_kernel(xp_ref, xs_ref, o_ref, ob_ref, t3_ref):
    i = pl.program_id(0)

    @pl.when(i < P_TILES)
    def _():
        for b in range(BATCH):
            t3_ref[:, b, :] = xp_ref[b]
        rows = t3_ref[...].reshape(ROW_TILE, D_MODEL)
        o_ref[...] = rows
        ob_ref[...] = rows.astype(bf16)

    @pl.when(i >= P_TILES)
    def _():
        for t in range(DEC_SEQ):
            rows = xs_ref[:, t, :]
            o_ref[t * DEC_BATCH:(t + 1) * DEC_BATCH, :] = rows
            ob_ref[t * DEC_BATCH:(t + 1) * DEC_BATCH, :] = rows.astype(bf16)


def _to_rows(x_prompt, x_sample):
    return pl.pallas_call(
        _to_rows_kernel,
        grid=(N_TILES,),
        in_specs=[pl.BlockSpec((BATCH, P_TC, D_MODEL), lambda i: (0, jnp.minimum(i, P_TILES - 1), 0)),
                  pl.BlockSpec((DEC_BATCH, DEC_SEQ, D_MODEL), lambda i: (0, 0, 0))],
        out_specs=[pl.BlockSpec((ROW_TILE, D_MODEL), lambda i: (i, 0))] * 2,
        out_shape=[jax.ShapeDtypeStruct((N_TOK, D_MODEL), f32), jax.ShapeDtypeStruct((N_TOK, D_MODEL), bf16)],
        scratch_shapes=[pltpu.VMEM((P_TC, BATCH, D_MODEL), f32)],
        compiler_params=_cparams(("arbitrary",)),
        name="to_rows",
    )(x_prompt, x_sample)


def _lookup(table, idx):
    ar = jnp.arange(MOE_EXPERTS, dtype=jnp.int32).reshape((MOE_EXPERTS,) + (1,) * idx.ndim)
    table = table.reshape(ar.shape)
    return jnp.sum(jnp.where(idx[None] == ar, table, 0), axis=0)


def _dispatch_plan(route_t, cnt):
    i32 = jnp.int32
    e = route_t[0:2].astype(i32)
    rank = route_t[4:6].astype(i32)
    counts = cnt[0, :MOE_EXPERTS].astype(i32)
    ends = jnp.cumsum(counts)
    starts = ends - counts
    pos = _lookup(starts, e) + rank

    first_tile = starts // MOE_TM
    last_tile = (ends - 1) // MOE_TM
    ntiles = jnp.where(counts > 0, last_tile - first_tile + 1, 0)
    item_end = jnp.cumsum(ntiles)
    n_items = item_end[-1]
    w = jnp.minimum(jnp.arange(MOE_MAX_ITEMS, dtype=i32), n_items - 1)
    we = jnp.sum((item_end[None, :] <= w[:, None]).astype(i32), axis=-1)
    wt = _lookup(first_tile, we) + w - _lookup(item_end - ntiles, we)
    wlo = jnp.maximum(_lookup(starts, we) - wt * MOE_TM, 0)
    whi = jnp.minimum(_lookup(ends, we) - wt * MOE_TM, MOE_TM)
    changes = (wt[1:] != wt[:-1]).astype(i32)
    wfirst = jnp.concatenate([jnp.ones((1,), i32), changes])
    wlast = jnp.concatenate([changes, jnp.ones((1,), i32)])
    wlast = jnp.where(jnp.arange(MOE_MAX_ITEMS, dtype=i32) == n_items - 1, 1, wlast)
    wefirst = jnp.concatenate([jnp.ones((1,), i32), (we[1:] != we[:-1]).astype(i32)])
    weslot = (jnp.cumsum(wefirst) - 1) % 2
    ar = jnp.arange(MOE_EXPERTS, dtype=i32)
    later = (ar[None, :] > ar[:, None]) & (counts[None, :] > 0)
    next_expert = jnp.min(jnp.where(later, ar[None, :], MOE_EXPERTS), axis=1)
    wenext = _lookup(jnp.where(next_expert < MOE_EXPERTS, next_expert, -1), we)
    return (pos[0], pos[1]), (wt, we, wlo, whi, wfirst, wlast, wefirst, weslot, wenext, n_items.reshape(1))


def _block_diag(w):
    n, a, b = w.shape
    eye = jnp.eye(n, dtype=w.dtype)
    return (w[:, :, None, :] * eye[:, None, :, None]).reshape(n * a, n * b)


def _ssm_params(a_re, a_im, log_dt, b_re, b_im, c_re, c_im):
    ar, ai = a_re, a_im
    dt = jnp.exp(log_dt)[:, None]
    mag = jnp.exp(ar * dt)
    lb_re = mag * jnp.cos(ai * dt)
    lb_im = mag * jnp.sin(ai * dt)
    den = ar * ar + ai * ai
    nr = lb_re - 1.0
    coef_re = (nr * ar + lb_im * ai) / den
    coef_im = (lb_im * ar - nr * ai) / den
    bb_re = coef_re[..., None] * b_re - coef_im[..., None] * b_im
    bb_im = coef_re[..., None] * b_im + coef_im[..., None] * b_re
    gk = SSM_GROUPS // SSM_KB

    def diag_blocks(w):
        return jnp.stack([_block_diag(w[k * gk:(k + 1) * gk]) for k in range(SSM_KB)]).astype(bf16)

    bbre = diag_blocks(bb_re.transpose(0, 2, 1))
    bbim = diag_blocks(bb_im.transpose(0, 2, 1))
    ccre = diag_blocks(c_re.transpose(0, 2, 1))
    ccim = diag_blocks(c_im.transpose(0, 2, 1))
    return (lb_re.reshape(1, SSM_LANES), lb_im.reshape(1, SSM_LANES), bbre, bbim, ccre, ccim)


def kernel(x_prompt, x_sample, state_ret, state_ssm_re, state_ssm_im, state_lru, state_conv, w_in, ret_gn_g, ret_gn_b, ssm_a_re, ssm_a_im, ssm_log_dt, ssm_b_re, ssm_b_im, ssm_c_re, ssm_c_im, ssm_d, ssm_w_glu, lru_conv_w, lru_conv_b, lru_wa, lru_ba, lru_wx, lru_bx, lru_lambda, w_branch, w_out, ln1_g, ln1_b, moe_w_group, moe_b_group, moe_w_expert, moe_b_expert, moe_w1, moe_w3, moe_w2, ln2_g, ln2_b):
    x, xb = _to_rows(x_prompt, x_sample)
    rope = _rope_tables()
    tabs_p = _ret_tables(RET_SUB_T)
    tabs_s = _ret_tables(DEC_SEQ)
    row = lambda v: v.reshape(1, -1)
    w_bf = w_in.astype(bf16)

    outs = [[] for _ in range(10)]
    for l in range(DEPTH):

        s0 = state_ret.reshape(DEPTH, DEC_BATCH, 2, 2 * RET_DK, RET_DV)
        y_ret, ret_p, ret_s = _retention(xb, w_bf, rope, s0, l, tabs_p, tabs_s,
                                         row(ret_gn_g[l]), row(ret_gn_b[l]))

        sp = _ssm_params(ssm_a_re[l], ssm_a_im[l], ssm_log_dt[l], ssm_b_re[l], ssm_b_im[l],
                         ssm_c_re[l], ssm_c_im[l])
        y_ssm, re_p, im_p, re_s, im_s = _ssm(
            xb, w_bf, l, state_ssm_re[l].reshape(DEC_BATCH, SSM_LANES), state_ssm_im[l].reshape(DEC_BATCH, SSM_LANES),
            *sp, row(ssm_d[l]), ssm_w_glu[l].astype(bf16))

        conv0 = state_conv[l].transpose(1, 0, 2).reshape(LRU_HIST_S, BRANCH_W)
        y_lru, lru_p, conv_p, lru_s, conv_s = _lru(
            xb, w_bf, l, state_lru[l], conv0, lru_conv_w[l], row(lru_conv_b[l]),
            _block_diag(lru_wa[l]).astype(bf16), row(lru_ba[l]),
            _block_diag(lru_wx[l]).astype(bf16), row(lru_bx[l]), row(lru_lambda[l]))

        wr = jnp.zeros((D_MODEL, ROUTE_LANES), f32)
        wr = wr.at[:, 0:MOE_GROUPS].set(moe_w_group[l]).at[:, MOE_GROUPS:MOE_GROUPS + MOE_EXPERTS].set(moe_w_expert[l])
        br = jnp.zeros((1, ROUTE_LANES), f32)
        br = br.at[0, 0:MOE_GROUPS].set(moe_b_group[l]).at[0, MOE_GROUPS:MOE_GROUPS + MOE_EXPERTS].set(moe_b_expert[l])
        x1, route, route_t, cnt = _merge(y_ret, y_ssm, y_lru, xb, w_bf, l, x, w_branch[l].astype(bf16), w_out[l].astype(bf16),
                                row(ln1_g[l]), row(ln1_b[l]), wr.astype(bf16), br)

        pos, plan = _dispatch_plan(route_t, cnt)
        xs = _dispatch(pos, x1)
        ys = _moe(plan, xs, moe_w1, moe_w3, moe_w2, l)
        x, xb = _combine(pos, ys, x1, route, row(ln2_g[l]), row(ln2_b[l]), final=(l == DEPTH - 1))

        outs[0].append(ret_p.reshape(BATCH, RET_HEADS, RET_DK, RET_DV))
        outs[1].append(re_p.reshape(BATCH, SSM_GROUPS, SSM_STATE))
        outs[2].append(im_p.reshape(BATCH, SSM_GROUPS, SSM_STATE))
        outs[3].append(lru_p)
        outs[4].append(conv_p.reshape(CONV_W - 1, BATCH, BRANCH_W).transpose(1, 0, 2))
        outs[5].append(ret_s.reshape(DEC_BATCH, RET_HEADS, RET_DK, RET_DV))
        outs[6].append(re_s.reshape(DEC_BATCH, SSM_GROUPS, SSM_STATE))
        outs[7].append(im_s.reshape(DEC_BATCH, SSM_GROUPS, SSM_STATE))
        outs[8].append(lru_s)
        outs[9].append(conv_s.reshape(CONV_W - 1, DEC_BATCH, BRANCH_W).transpose(1, 0, 2))

    y_prompt, y_sample = x, xb
    return (y_prompt, y_sample) + tuple(jnp.stack(o) for o in outs)
```

```python
import functools

import jax
import jax.numpy as jnp
import numpy as np
from jax import lax
from jax.experimental import pallas as pl
from jax.experimental.pallas import tpu as pltpu

f32 = jnp.float32
bf16 = jnp.bfloat16

D_MODEL = 1024
BATCH = 8
SEQ = 2048
DEPTH = 2
DEC_BATCH = 128
DEC_SEQ = 8
PAST_LEN = 16384
BRANCH_W = 512
N_BRANCH = 3
RET_HEADS = 4
RET_DK = 64
RET_DV = 128
ROPE_BASE = 10000.0
SSM_GROUP = 16
SSM_GROUPS = 32
SSM_STATE = 64
SSM_LANES = SSM_GROUPS * SSM_STATE
LRU_BLOCKS = 8
LRU_BW = 64
CONV_W = 4
LRU_C = 8.0
MOE_GROUPS = 4
MOE_PER_GROUP = 8
MOE_EXPERTS = 32
MOE_TOPK = 2
MOE_HIDDEN = 512
DN_ALPHA = (2.0 * DEPTH) ** 0.25
LN_EPS = 1e-5
D_IN = 6144

V7X_SUBLANES = 8
V7X_LANES = 128
V7X_VMEM_LIMIT = 56 * 1024 * 1024

N_P = BATCH * SEQ
N_S = DEC_BATCH * DEC_SEQ
N_TOK = N_P + N_S
ROW_TILE = 1024
P_TILES = N_P // ROW_TILE
N_TILES = N_TOK // ROW_TILE
P_TC = ROW_TILE // BATCH
RET_SUB_T = 32
RET_SUB_R = RET_SUB_T * BATCH
RET_SUBS = ROW_TILE // RET_SUB_R
S_BLOCKS = DEC_BATCH // BATCH
S_BLOCK_R = DEC_SEQ * BATCH
MERGE_TILE = 512
MOE_TM = 512
N_PAIRS = N_TOK * MOE_TOPK
MOE_MAX_ITEMS = N_PAIRS // MOE_TM + MOE_EXPERTS - 1
DSP_TILE = 1024
CMB_TILE = 256
CMB_P_STEPS = N_P // CMB_TILE
ROUTE_LANES = 128


PROJ_RC = 256
RET_COLS, RET_BLK = 1536, 0
SSM_COLS, SSM_BLK = 512, 3
LRU_COLS, LRU_BLK = 1024, 2
GATE_COLS, GATE_BLK = 3072, 1


def _cparams(sem):
    return pltpu.CompilerParams(dimension_semantics=sem, vmem_limit_bytes=V7X_VMEM_LIMIT)


N_PLANES = D_MODEL // V7X_LANES


def _to_planes(ref, rows):
    for c in range(N_PLANES):
        ref[c] = rows[:, c * V7X_LANES:(c + 1) * V7X_LANES]


def _from_planes(ref):
    return jnp.concatenate([ref[c] for c in range(N_PLANES)], axis=1)


def _const_spec(block_shape, index):
    return pl.BlockSpec(block_shape, lambda *_: index, pipeline_mode=pl.Buffered(1))


def _dot(a, b):
    return jnp.dot(a, b, preferred_element_type=f32)


def _dot_nt(a, b):
    return lax.dot_general(a, b, (((1,), (1,)), ((), ())), preferred_element_type=f32)


def _dot_tn(a, b):
    return lax.dot_general(a, b, (((0,), (0,)), ((), ())), preferred_element_type=f32)


def _layer_norm_rows(x, g, b):
    mu = jnp.mean(x, -1, keepdims=True)
    xc = x - mu
    var = jnp.mean(xc * xc, -1, keepdims=True)
    return xc * lax.rsqrt(var + LN_EPS) * g + b


def _project_rows(xb_ref, w_ref, z_ref):
    for rc in range(ROW_TILE // PROJ_RC):
        rs = slice(rc * PROJ_RC, (rc + 1) * PROJ_RC)
        z_ref[rs, :] = _dot(xb_ref[rs, :], w_ref[0])


def _ret_block(q, k, v, g, cosb, sinb, mask_ref, qdec_ref, kdec_ref, cdec_ref, scat_ref, gng, gnb):
    rows = q.shape[0]
    lane_qk = lax.broadcasted_iota(jnp.int32, (rows, 2 * V7X_LANES), 1)
    first_half = (lane_qk & (RET_DK - 1)) < (RET_DK // 2)

    def rope(x):
        partner = jnp.where(first_half, pltpu.roll(x, 2 * V7X_LANES - RET_DK // 2, 1),
                            pltpu.roll(x, RET_DK // 2, 1))
        return x * cosb + partner * sinb

    q = rope(q)
    k = rope(k) * (RET_DK ** -0.5)
    kd = k * kdec_ref[...]
    lane = lax.broadcasted_iota(jnp.int32, (rows, V7X_LANES), 1)
    row_b = lax.broadcasted_iota(jnp.int32, (rows, V7X_LANES), 0) & (BATCH - 1)
    outs = []
    for p in range(2):
        qp = q[:, p * V7X_LANES:(p + 1) * V7X_LANES]
        kp = k[:, p * V7X_LANES:(p + 1) * V7X_LANES].astype(bf16)
        kdp = kd[:, p * V7X_LANES:(p + 1) * V7X_LANES]
        s_old = scat_ref[p]
        s_bf = s_old.astype(bf16)
        s_new = s_old * jnp.concatenate([cdec_ref[p]] * BATCH, axis=1)
        for hh in range(2):
            h = 2 * p + hh
            head_lanes = (lane >= RET_DK) if hh else (lane < RET_DK)
            qh = jnp.where(head_lanes, qp, 0.0).astype(bf16)
            kdh = jnp.where(head_lanes, kdp, 0.0).astype(bf16)
            vh = v[:, h * RET_DV:(h + 1) * RET_DV]
            vh_bf = vh.astype(bf16)
            sc = _dot_nt(qh, kp) * mask_ref[h]
            o = _dot(sc.astype(bf16), vh_bf)
            cross = _dot(qh, s_bf)
            oc = jnp.zeros((rows, RET_DV), f32)
            for b in range(BATCH):
                oc = oc + jnp.where(row_b == b, cross[:, b * RET_DV:(b + 1) * RET_DV], 0.0)
            o = o + oc * qdec_ref[h]
            vcat = jnp.concatenate([jnp.where(row_b == b, vh_bf, jnp.zeros_like(vh_bf))
                                    for b in range(BATCH)], axis=1)
            s_new = s_new + _dot_tn(kdh, vcat)
            mu = jnp.mean(o, -1, keepdims=True)
            oc2 = o - mu
            var = jnp.mean(oc2 * oc2, -1, keepdims=True)
            outs.append(oc2 * lax.rsqrt(var + LN_EPS))
        scat_ref[p] = s_new
    o = jnp.concatenate(outs, axis=1) * gng + gnb
    return jax.nn.silu(g) * o


def _per_step_rows(tab_ref, t0, steps):
    return jnp.concatenate(
        [jnp.broadcast_to(tab_ref[t0 + t:t0 + t + 1, :], (BATCH, tab_ref.shape[1])) for t in range(steps)], axis=0)


def _ret_kernel(xb_ref, w_ref, cos_ref, sin_ref, cos_s_ref, sin_s_ref, s0_ref,
                mask_p_ref, qdec_p_ref, kdec_p_ref, cdec_p_ref,
                mask_s_ref, qdec_s_ref, kdec_s_ref, cdec_s_ref,
                gng_ref, gnb_ref,
                y_ref, retp_ref, rets_ref, scat_ref, z_ref):
    i = pl.program_id(0)
    gng = gng_ref[...]
    gnb = gnb_ref[...]
    q_cols, k_cols = slice(0, 256), slice(256, 512)
    v_cols, g_cols = slice(512, 1024), slice(1024, 1536)

    @pl.when(i == 0)
    def _():
        scat_ref[...] = jnp.zeros_like(scat_ref)

    @pl.when(i <= P_TILES)
    def _():
        _project_rows(xb_ref, w_ref, z_ref)

    @pl.when(i < P_TILES)
    def _():
        for sc in range(RET_SUBS):
            r0 = sc * RET_SUB_R
            rs = slice(r0, r0 + RET_SUB_R)
            y_ref[rs, :] = _ret_block(
                z_ref[rs, q_cols], z_ref[rs, k_cols], z_ref[rs, v_cols], z_ref[rs, g_cols],
                _per_step_rows(cos_ref, sc * RET_SUB_T, RET_SUB_T), _per_step_rows(sin_ref, sc * RET_SUB_T, RET_SUB_T),
                mask_p_ref, qdec_p_ref, kdec_p_ref, cdec_p_ref, scat_ref, gng, gnb)

    @pl.when(i == P_TILES - 1)
    def _():
        for b in range(BATCH):
            for p in range(2):
                retp_ref[b, p] = scat_ref[p, :, b * RET_DV:(b + 1) * RET_DV]

    @pl.when(i >= P_TILES)
    def _():
        bb = i - P_TILES
        for b in range(BATCH):
            for p in range(2):
                scat_ref[p, :, b * RET_DV:(b + 1) * RET_DV] = s0_ref[0, b, p]

        def rows_of(cols):
            return jnp.concatenate(
                [z_ref[pl.ds(pl.multiple_of(t * DEC_BATCH + bb * BATCH, BATCH), BATCH), cols]
                 for t in range(DEC_SEQ)], axis=0)

        y = _ret_block(
            rows_of(q_cols), rows_of(k_cols), rows_of(v_cols),
            rows_of(g_cols), _per_step_rows(cos_s_ref, 0, DEC_SEQ), _per_step_rows(sin_s_ref, 0, DEC_SEQ),
            mask_s_ref, qdec_s_ref, kdec_s_ref, cdec_s_ref, scat_ref, gng, gnb)
        for t in range(DEC_SEQ):
            y_ref[pl.ds(pl.multiple_of(t * DEC_BATCH + bb * BATCH, BATCH), BATCH), :] = (
                y[t * BATCH:(t + 1) * BATCH, :])
        for b in range(BATCH):
            for p in range(2):
                rets_ref[b, p] = scat_ref[p, :, b * RET_DV:(b + 1) * RET_DV]


def _ret_tables(tc):
    rows = tc * BATCH
    nf = np.float32
    log_g = np.log1p(-np.exp2(nf(-5.0) - np.arange(RET_HEADS, dtype=nf))).astype(nf)
    t_idx = (np.arange(rows) // BATCH).astype(nf)
    b_idx = np.arange(rows) % BATCH
    rel = t_idx[:, None] - t_idx[None, :]
    same = b_idx[:, None] == b_idx[None, :]
    decay = np.exp(log_g[:, None, None] * np.maximum(rel, nf(0.0)))
    mask = np.where((rel >= 0) & same, decay, nf(0.0))
    qdec = np.exp(log_g[:, None] * (t_idx[None, :] + nf(1.0)))
    qdec = np.broadcast_to(qdec[:, :, None], (RET_HEADS, rows, RET_DV))
    kdec = np.exp(log_g[:, None] * (nf(tc - 1.0) - t_idx[None, :]))
    kdec = np.broadcast_to(kdec.T[:, :, None], (rows, RET_HEADS, RET_DK)).reshape(rows, RET_HEADS * RET_DK)
    cdec = np.exp(log_g * nf(tc))
    cdec = np.broadcast_to(cdec[:, None, None], (RET_HEADS, RET_DK, RET_DV)).reshape(2, 2 * RET_DK, RET_DV)
    return tuple(jnp.asarray(np.ascontiguousarray(a), dtype=f32) for a in (mask, qdec, kdec, cdec))


def _rope_tables():
    half = RET_DK // 2
    inv = ROPE_BASE ** (-jnp.arange(half, dtype=f32) / half)
    pos_p = jnp.arange(SEQ, dtype=f32)
    pos_s = PAST_LEN + jnp.arange(DEC_SEQ, dtype=f32)

    def tab(pos):
        ang = pos[:, None] * inv[None, :]
        cos = jnp.cos(ang)
        sin = jnp.sin(ang)
        cos_h = jnp.concatenate([cos, cos], axis=1)
        sin_h = jnp.concatenate([-sin, sin], axis=1)
        return jnp.tile(cos_h, (1, RET_HEADS)), jnp.tile(sin_h, (1, RET_HEADS))

    return tab(pos_p) + tab(pos_s)


def _retention(xb, w_bf, rope, s0_s, layer, tabs_p, tabs_s, gng, gnb):
    n_steps = P_TILES + S_BLOCKS
    tile = lambda i: jnp.minimum(i, P_TILES)
    sblk = lambda i: jnp.maximum(i - P_TILES, 0)
    full = lambda a: pl.BlockSpec(a.shape, lambda i, _n=a.ndim: (0,) * _n)
    state_blk = (BATCH, 2, 2 * RET_DK, RET_DV)
    cos_p, sin_p, cos_s, sin_s = rope
    ptile = lambda i: jnp.minimum(i, P_TILES - 1)
    ins = [xb, w_bf, cos_p, sin_p, cos_s, sin_s, s0_s, *tabs_p, *tabs_s, gng, gnb]
    in_specs = [
        pl.BlockSpec((ROW_TILE, D_MODEL), lambda i: (tile(i), 0)),
        _const_spec((1, D_MODEL, RET_COLS), (layer, 0, RET_BLK)),
        pl.BlockSpec((P_TC, 256), lambda i: (ptile(i), 0)),
        pl.BlockSpec((P_TC, 256), lambda i: (ptile(i), 0)),
        full(cos_s), full(sin_s),
        pl.BlockSpec((1,) + state_blk, lambda i: (layer, sblk(i), 0, 0, 0)),
    ] + [full(a) for a in (*tabs_p, *tabs_s, gng, gnb)]
    return pl.pallas_call(
        _ret_kernel,
        grid=(n_steps,),
        in_specs=in_specs,
        out_specs=[pl.BlockSpec((ROW_TILE, BRANCH_W), lambda i: (tile(i), 0)),
                   pl.BlockSpec(state_blk, lambda i: (0, 0, 0, 0)),
                   pl.BlockSpec(state_blk, lambda i: (sblk(i), 0, 0, 0))],
        out_shape=[jax.ShapeDtypeStruct((N_TOK, BRANCH_W), f32),
                   jax.ShapeDtypeStruct((BATCH, 2, 2 * RET_DK, RET_DV), f32),
                   jax.ShapeDtypeStruct((DEC_BATCH, 2, 2 * RET_DK, RET_DV), f32)],
        scratch_shapes=[pltpu.VMEM((2, 2 * RET_DK, BATCH * RET_DV), f32),
                        pltpu.VMEM((ROW_TILE, RET_COLS), f32)],
        compiler_params=_cparams(("arbitrary",)),
        name="retention",
    )(*ins)


SSM_LB = 512
SSM_RC = 256
SSM_KB = 2
SSM_KB_U = BRANCH_W // SSM_KB
SSM_KB_H = SSM_LANES // SSM_KB


def _ssm_scan(bre_ref, bim_ref, lre_ref, lim_ref, h_re0, h_im0, row0, nb_rows, steps, lb, unroll):
    ls = slice(lb * SSM_LB, (lb + 1) * SSM_LB)
    a_re = jnp.broadcast_to(lre_ref[:, ls], (V7X_SUBLANES, SSM_LB))
    a_im = jnp.broadcast_to(lim_ref[:, ls], (V7X_SUBLANES, SSM_LB))

    def step(t, carry):
        h_re, h_im = carry
        r = pl.multiple_of(row0 + t * nb_rows, V7X_SUBLANES)
        n_re = a_re * h_re - a_im * h_im + bre_ref[pl.ds(r, V7X_SUBLANES), ls]
        n_im = a_re * h_im + a_im * h_re + bim_ref[pl.ds(r, V7X_SUBLANES), ls]
        bre_ref[pl.ds(r, V7X_SUBLANES), ls] = n_re
        bim_ref[pl.ds(r, V7X_SUBLANES), ls] = n_im
        return n_re, n_im

    return lax.fori_loop(0, steps, step, (h_re0, h_im0), unroll=unroll)


def _ssm_kernel(xb_ref, w_ref, h0re_ref, h0im_ref, lre_ref, lim_ref, bbre_ref, bbim_ref, ccre_ref, ccim_ref,
                d_ref, wglu_ref,
                y_ref, pre_ref, pim_ref, sre_ref, sim_ref,
                bre_ref, bim_ref, hre_ref, him_ref, zs_ref):
    i = pl.program_id(0)
    _project_rows(xb_ref, w_ref, zs_ref)
    for rc in range(ROW_TILE // SSM_RC):
        rs = slice(rc * SSM_RC, (rc + 1) * SSM_RC)
        ub = zs_ref[rs, :].astype(bf16)
        for k in range(SSM_KB):
            uk = ub[:, k * SSM_KB_U:(k + 1) * SSM_KB_U]
            hs = slice(k * SSM_KB_H, (k + 1) * SSM_KB_H)
            bre_ref[rs, hs] = _dot(uk, bbre_ref[k])
            bim_ref[rs, hs] = _dot(uk, bbim_ref[k])

    @pl.when(i == 0)
    def _():
        hre_ref[...] = jnp.zeros_like(hre_ref)
        him_ref[...] = jnp.zeros_like(him_ref)

    @pl.when(i < P_TILES)
    def _():
        for lb in range(SSM_LANES // SSM_LB):
            ls = slice(lb * SSM_LB, (lb + 1) * SSM_LB)
            h_re, h_im = _ssm_scan(bre_ref, bim_ref, lre_ref, lim_ref, hre_ref[:, ls], him_ref[:, ls],
                                   0, BATCH, P_TC, lb, 8)
            hre_ref[:, ls] = h_re
            him_ref[:, ls] = h_im
        pre_ref[...] = hre_ref[...]
        pim_ref[...] = him_ref[...]

    @pl.when(i >= P_TILES)
    def _():
        def per_row_tile(rt, c):
            r0 = pl.multiple_of(rt * V7X_SUBLANES, V7X_SUBLANES)
            for lb in range(SSM_LANES // SSM_LB):
                ls = slice(lb * SSM_LB, (lb + 1) * SSM_LB)
                _ssm_scan(bre_ref, bim_ref, lre_ref, lim_ref,
                          h0re_ref[pl.ds(r0, V7X_SUBLANES), ls], h0im_ref[pl.ds(r0, V7X_SUBLANES), ls],
                          r0, DEC_BATCH, DEC_SEQ, lb, True)
            return c

        lax.fori_loop(0, DEC_BATCH // V7X_SUBLANES, per_row_tile, 0)
        last = (DEC_SEQ - 1) * DEC_BATCH
        sre_ref[...] = bre_ref[last:last + DEC_BATCH, :]
        sim_ref[...] = bim_ref[last:last + DEC_BATCH, :]

    for rc in range(ROW_TILE // SSM_RC):
        rs = slice(rc * SSM_RC, (rc + 1) * SSM_RC)
        ch = []
        for k in range(SSM_KB):
            hs = slice(k * SSM_KB_H, (k + 1) * SSM_KB_H)
            ch.append(_dot(bre_ref[rs, hs].astype(bf16), ccre_ref[k]) - _dot(bim_ref[rs, hs].astype(bf16), ccim_ref[k]))
        y = jnp.concatenate(ch, axis=1) + d_ref[...] * zs_ref[rs, :]
        zz = jax.nn.gelu(y)
        y_ref[rs, :] = zz * jax.nn.sigmoid(_dot(zz.astype(bf16), wglu_ref[...]))


def _ssm(xb, w_bf, layer, h0re, h0im, lre, lim, bbre, bbim, ccre, ccim, dvec, wglu):
    full = lambda a: _const_spec(a.shape, (0,) * a.ndim)
    consts = (h0re, h0im, lre, lim, bbre, bbim, ccre, ccim, dvec, wglu)
    return pl.pallas_call(
        _ssm_kernel,
        grid=(N_TILES,),
        in_specs=[pl.BlockSpec((ROW_TILE, D_MODEL), lambda i: (i, 0)),
                  _const_spec((1, D_MODEL, SSM_COLS), (layer, 0, SSM_BLK))] + [full(a) for a in consts],
        out_specs=[pl.BlockSpec((ROW_TILE, BRANCH_W), lambda i: (i, 0)),
                   pl.BlockSpec((BATCH, SSM_LANES), lambda i: (0, 0)),
                   pl.BlockSpec((BATCH, SSM_LANES), lambda i: (0, 0)),
                   pl.BlockSpec((DEC_BATCH, SSM_LANES), lambda i: (0, 0)),
                   pl.BlockSpec((DEC_BATCH, SSM_LANES), lambda i: (0, 0))],
        out_shape=[jax.ShapeDtypeStruct((N_TOK, BRANCH_W), f32),
                   jax.ShapeDtypeStruct((BATCH, SSM_LANES), f32),
                   jax.ShapeDtypeStruct((BATCH, SSM_LANES), f32),
                   jax.ShapeDtypeStruct((DEC_BATCH, SSM_LANES), f32),
                   jax.ShapeDtypeStruct((DEC_BATCH, SSM_LANES), f32)],
        scratch_shapes=[pltpu.VMEM((ROW_TILE, SSM_LANES), f32), pltpu.VMEM((ROW_TILE, SSM_LANES), f32),
                        pltpu.VMEM((BATCH, SSM_LANES), f32), pltpu.VMEM((BATCH, SSM_LANES), f32),
                        pltpu.VMEM((ROW_TILE, SSM_COLS), f32)],
        compiler_params=_cparams(("arbitrary",)),
        name="ssm",
    )(xb, w_bf, *consts)


LRU_HIST_P = (CONV_W - 1) * BATCH
LRU_HIST_S = (CONV_W - 1) * DEC_BATCH


def _lru_gates(xe_ref, nb_rows, cw_ref, cb_ref, wa_ref, ba_ref, wx_ref, bx_ref, lam_ref):
    xc = cb_ref[...] + xe_ref[0:ROW_TILE, :] * cw_ref[0:1, :]
    for j in range(1, CONV_W):
        xc = xc + xe_ref[j * nb_rows:j * nb_rows + ROW_TILE, :] * cw_ref[j:j + 1, :]
    xcb = xc.astype(bf16)
    r = jax.nn.sigmoid(_dot(xcb, wa_ref[...]) + ba_ref[...])
    ig = jax.nn.sigmoid(_dot(xcb, wx_ref[...]) + bx_ref[...])
    log_a = -LRU_C * r * jax.nn.softplus(-lam_ref[...])
    a = jnp.exp(log_a)
    b = jnp.sqrt(-jnp.tanh(log_a) * (a * a + 1.0)) * (ig * xc)
    return a, b


def _lru_kernel(xb_ref, w_ref, h0_ref, conv0_ref, cw_ref, cb_ref, wa_ref, ba_ref, wx_ref, bx_ref, lam_ref,
                y_ref, hp_ref, convp_ref, hs_ref, convs_ref,
                xe_ref, a_ref, b_ref, hc_ref, z_ref):
    i = pl.program_id(0)
    params = (cw_ref, cb_ref, wa_ref, ba_ref, wx_ref, bx_ref, lam_ref)
    _project_rows(xb_ref, w_ref, z_ref)
    zx_ref = z_ref.at[:, 0:BRANCH_W]
    zg_ref = z_ref.at[:, BRANCH_W:2 * BRANCH_W]

    @pl.when(i == 0)
    def _():
        xe_ref[0:LRU_HIST_P, :] = jnp.zeros((LRU_HIST_P, BRANCH_W), f32)
        hc_ref[...] = jnp.zeros_like(hc_ref)

    @pl.when(i < P_TILES)
    def _():
        xe_ref[LRU_HIST_P:LRU_HIST_P + ROW_TILE, :] = zx_ref[...]
        a, b = _lru_gates(xe_ref, BATCH, *params)
        a_ref[...] = a
        b_ref[...] = b
        hist = xe_ref[ROW_TILE:ROW_TILE + LRU_HIST_P, :]
        xe_ref[0:LRU_HIST_P, :] = hist
        convp_ref[...] = hist

        def step(t, h):
            r = pl.multiple_of(t * BATCH, BATCH)
            h = a_ref[pl.ds(r, BATCH), :] * h + b_ref[pl.ds(r, BATCH), :]
            b_ref[pl.ds(r, BATCH), :] = h
            return h

        h = lax.fori_loop(0, P_TC, step, hc_ref[...], unroll=8)
        hc_ref[...] = h
        hp_ref[...] = h

    @pl.when(i >= P_TILES)
    def _():
        xe_ref[0:LRU_HIST_S, :] = conv0_ref[...]
        xe_ref[LRU_HIST_S:LRU_HIST_S + ROW_TILE, :] = zx_ref[...]
        a, b = _lru_gates(xe_ref, DEC_BATCH, *params)
        a_ref[...] = a
        b_ref[...] = b
        convs_ref[...] = xe_ref[ROW_TILE:ROW_TILE + LRU_HIST_S, :]

        def per_row_tile(rt, c):
            r0 = pl.multiple_of(rt * V7X_SUBLANES, V7X_SUBLANES)
            h = h0_ref[pl.ds(r0, V7X_SUBLANES), :]
            for t in range(DEC_SEQ):
                r = pl.multiple_of(t * DEC_BATCH + r0, V7X_SUBLANES)
                h = a_ref[pl.ds(r, V7X_SUBLANES), :] * h + b_ref[pl.ds(r, V7X_SUBLANES), :]
                b_ref[pl.ds(r, V7X_SUBLANES), :] = h
            return c

        lax.fori_loop(0, DEC_BATCH // V7X_SUBLANES, per_row_tile, 0)
        last = (DEC_SEQ - 1) * DEC_BATCH
        hs_ref[...] = b_ref[last:last + DEC_BATCH, :]

    y_ref[...] = b_ref[...] * jax.nn.gelu(zg_ref[...])


def _lru(xb, w_bf, layer, h0, conv0, cw, cb, wa, ba, wx, bx, lam):
    full = lambda a: _const_spec(a.shape, (0,) * a.ndim)
    consts = (h0, conv0, cw, cb, wa, ba, wx, bx, lam)
    return pl.pallas_call(
        _lru_kernel,
        grid=(N_TILES,),
        in_specs=[pl.BlockSpec((ROW_TILE, D_MODEL), lambda i: (i, 0)),
                  _const_spec((1, D_MODEL, LRU_COLS), (layer, 0, LRU_BLK))] + [full(a) for a in consts],
        out_specs=[pl.BlockSpec((ROW_TILE, BRANCH_W), lambda i: (i, 0)),
                   pl.BlockSpec((BATCH, BRANCH_W), lambda i: (0, 0)),
                   pl.BlockSpec((LRU_HIST_P, BRANCH_W), lambda i: (0, 0)),
                   pl.BlockSpec((DEC_BATCH, BRANCH_W), lambda i: (0, 0)),
                   pl.BlockSpec((LRU_HIST_S, BRANCH_W), lambda i: (0, 0))],
        out_shape=[jax.ShapeDtypeStruct((N_TOK, BRANCH_W), f32),
                   jax.ShapeDtypeStruct((BATCH, BRANCH_W), f32),
                   jax.ShapeDtypeStruct((LRU_HIST_P, BRANCH_W), f32),
                   jax.ShapeDtypeStruct((DEC_BATCH, BRANCH_W), f32),
                   jax.ShapeDtypeStruct((LRU_HIST_S, BRANCH_W), f32)],
        scratch_shapes=[pltpu.VMEM((ROW_TILE + LRU_HIST_S, BRANCH_W), f32),
                        pltpu.VMEM((ROW_TILE, BRANCH_W), f32), pltpu.VMEM((ROW_TILE, BRANCH_W), f32),
                        pltpu.VMEM((BATCH, BRANCH_W), f32),
                        pltpu.VMEM((ROW_TILE, LRU_COLS), f32)],
        compiler_params=_cparams(("arbitrary",)),
        name="lru",
    )(xb, w_bf, *consts)


def _merge_kernel(yr_ref, ys_ref, yl_ref, xb_ref, wg_ref, x_ref, wb_ref, wo_ref, g_ref, b_ref, wr_ref, br_ref,
                  x1_ref, route_ref, route_t_ref, cnt_out_ref, cnt_ref):
    merged = jnp.zeros((MERGE_TILE, D_MODEL), f32)
    xb = xb_ref[...]
    for n, y_ref in enumerate((yr_ref, ys_ref, yl_ref)):
        proj = _dot(y_ref[...].astype(bf16), wb_ref[n])
        gate = jax.nn.sigmoid(_dot(xb, wg_ref[0, :, n * D_MODEL:(n + 1) * D_MODEL]))
        merged = merged + gate * proj
    mix = _dot(merged.astype(bf16), wo_ref[...])
    x1 = _layer_norm_rows(DN_ALPHA * x_ref[...] + mix, g_ref[...], b_ref[...])
    _to_planes(x1_ref, x1)

    logits = _dot(x1.astype(bf16), wr_ref[...]) + br_ref[...]
    lane = lax.broadcasted_iota(jnp.int32, (MERGE_TILE, ROUTE_LANES), 1).astype(f32)
    big = jnp.float32(ROUTE_LANES)
    neg = jnp.float32(-jnp.inf)
    is_g = lane < MOE_GROUPS
    lg = jnp.where(is_g, logits, neg)
    mg = jnp.max(lg, -1, keepdims=True)
    gsel = jnp.min(jnp.where(lg == mg, lane, big), -1, keepdims=True)
    sum_g = jnp.sum(jnp.where(is_g, jnp.exp(lg - mg), 0.0), -1, keepdims=True)
    pg_sel = 1.0 / sum_g
    lo = MOE_GROUPS + gsel * MOE_PER_GROUP
    is_e = jnp.abs(lane - lo - 0.5 * (MOE_PER_GROUP - 1)) < 0.5 * MOE_PER_GROUP
    le = jnp.where(is_e, logits, neg)
    me = jnp.max(le, -1, keepdims=True)
    ex = jnp.where(is_e, jnp.exp(le - me), 0.0)
    pe = jnp.where(is_e, ex / jnp.sum(ex, -1, keepdims=True), -1.0)
    v1 = jnp.max(pe, -1, keepdims=True)
    i1 = jnp.min(jnp.where(pe == v1, lane, big), -1, keepdims=True)
    pe2 = jnp.where(lane == i1, -1.0, pe)
    v2 = jnp.max(pe2, -1, keepdims=True)
    i2 = jnp.min(jnp.where(pe2 == v2, lane, big), -1, keepdims=True)
    vsum = v1 + v2
    w1 = pg_sel * v1 / vsum
    w2 = pg_sel * v2 / vsum
    e1 = i1 - MOE_GROUPS
    e2 = i2 - MOE_GROUPS

    @pl.when(pl.program_id(0) == 0)
    def _():
        cnt_ref[...] = jnp.zeros_like(cnt_ref)

    oh1 = lane == e1
    oh2 = lane == e2
    ohs = jnp.where(oh1, 1.0, jnp.where(oh2, 1.0, 0.0))
    r_i = lax.broadcasted_iota(jnp.int32, (MERGE_TILE, MERGE_TILE), 0)
    c_i = lax.broadcasted_iota(jnp.int32, (MERGE_TILE, MERGE_TILE), 1)
    strict_lower = jnp.where(c_i < r_i, 1.0, 0.0).astype(bf16)
    before = _dot(strict_lower, ohs.astype(bf16)) + cnt_ref[0:1, :]
    rank1 = jnp.sum(jnp.where(oh1, before, 0.0), -1, keepdims=True)
    rank2 = jnp.sum(jnp.where(oh2, before, 0.0), -1, keepdims=True)
    cnt_ref[0:1, :] = cnt_ref[0:1, :] + jnp.sum(ohs, 0, keepdims=True)
    cnt_out_ref[...] = cnt_ref[...]

    route = jnp.zeros((MERGE_TILE, ROUTE_LANES), f32)
    for k, val in enumerate((e1, e2, w1, w2, rank1, rank2)):
        route = jnp.where(lane == k, val, route)
    route_ref[...] = route
    route_t_ref[...] = route.T[0:V7X_SUBLANES, :]


def _merge(y_ret, y_ssm, y_lru, xb, w_bf, layer, x, wb, wo, g, b, wr, br):
    full = lambda a: _const_spec(a.shape, (0,) * a.ndim)
    row = lambda w: pl.BlockSpec((MERGE_TILE, w), lambda i: (i, 0))
    consts = (wb, wo, g, b, wr, br)
    return pl.pallas_call(
        _merge_kernel,
        grid=(N_TOK // MERGE_TILE,),
        in_specs=[row(BRANCH_W), row(BRANCH_W), row(BRANCH_W), row(D_MODEL),
                  _const_spec((1, D_MODEL, GATE_COLS), (layer, 0, GATE_BLK)),
                  row(D_MODEL)] + [full(a) for a in consts],
        out_specs=[pl.BlockSpec((N_PLANES, MERGE_TILE, V7X_LANES), lambda i: (0, i, 0)), row(ROUTE_LANES),
                   pl.BlockSpec((V7X_SUBLANES, MERGE_TILE), lambda i: (0, i)),
                   pl.BlockSpec((V7X_SUBLANES, ROUTE_LANES), lambda i: (0, 0))],
        out_shape=[jax.ShapeDtypeStruct((N_PLANES, N_TOK, V7X_LANES), f32),
                   jax.ShapeDtypeStruct((N_TOK, ROUTE_LANES), f32),
                   jax.ShapeDtypeStruct((V7X_SUBLANES, N_TOK), f32),
                   jax.ShapeDtypeStruct((V7X_SUBLANES, ROUTE_LANES), f32)],
        scratch_shapes=[pltpu.VMEM((V7X_SUBLANES, ROUTE_LANES), f32)],
        compiler_params=_cparams(("arbitrary",)),
        name="merge",
    )(y_ret, y_ssm, y_lru, xb, w_bf, x, *consts)


def _dispatch_kernel(pos0_ref, pos1_ref, x1_ref, xs_hbm, sem):
    base = pl.program_id(0) * DSP_TILE

    def row_copy(r, dst_row):
        return pltpu.make_async_copy(x1_ref.at[:, r, :], xs_hbm.at[dst_row], sem.at[0])

    def issue(r, c):
        n = base + r
        for k in range(MOE_TOPK):
            row_copy(r, (pos0_ref, pos1_ref)[k][n]).start(priority=k % 2)
        return c

    lax.fori_loop(0, DSP_TILE, issue, 0, unroll=8)
    for _ in range(MOE_TOPK):
        pltpu.make_async_copy(x1_ref, x1_ref, sem.at[0]).wait()


def _dispatch(pos, x1p):
    grid_spec = pltpu.PrefetchScalarGridSpec(
        num_scalar_prefetch=MOE_TOPK,
        grid=(N_TOK // DSP_TILE,),
        in_specs=[pl.BlockSpec((N_PLANES, DSP_TILE, V7X_LANES), lambda s, *_: (0, s, 0))],
        out_specs=pl.BlockSpec(memory_space=pl.ANY),
        scratch_shapes=[pltpu.SemaphoreType.DMA((1,))],
    )
    return pl.pallas_call(
        _dispatch_kernel,
        grid_spec=grid_spec,
        out_shape=jax.ShapeDtypeStruct((N_PAIRS, N_PLANES, V7X_LANES), f32),
        compiler_params=_cparams(("arbitrary",)),
        name="dispatch",
    )(*pos, x1p)


def _moe_kernel(layer, wt_ref, we_ref, wlo_ref, whi_ref, wfirst_ref, wlast_ref, wefirst_ref, weslot_ref,
                wenext_ref, nw_ref,
                xs_hbm, w1_hbm, w3_hbm, w2_hbm, ys_hbm,
                xin, yout, sem_in, sem_out, wst1, wst3, wst2, sem_w, w1b, w3b, w2b):
    w = pl.program_id(0)
    n_items = nw_ref[0]

    def weight_copies(expert, slot):
        return [pltpu.make_async_copy(src.at[layer, expert], dst.at[slot], sem_w.at[slot])
                for src, dst in ((w1_hbm, wst1), (w3_hbm, wst3), (w2_hbm, wst2))]

    def in_copy(item, slot, c):
        r0 = pl.multiple_of(wt_ref[item] * MOE_TM, MOE_TM)
        return pltpu.make_async_copy(xs_hbm.at[pl.ds(r0, MOE_TM), c, :], xin.at[slot, c], sem_in.at[slot])

    def out_copy(tile, slot, c):
        r0 = pl.multiple_of(tile * MOE_TM, MOE_TM)
        return pltpu.make_async_copy(yout.at[slot, c], ys_hbm.at[pl.ds(r0, MOE_TM), c, :], sem_out.at[slot])

    @pl.when(w == 0)
    def _():
        for cp in weight_copies(we_ref[0], 0):
            cp.start()
        for c in range(N_PLANES):
            in_copy(0, 0, c).start(priority=1)

    @pl.when(w + 1 < n_items)
    def _():
        for c in range(N_PLANES):
            in_copy(w + 1, (w + 1) % 2, c).start(priority=1)

    @pl.when(w < n_items)
    def _():
        slot = w % 2
        tile = wt_ref[w]
        oslot = tile % 2
        for c in range(N_PLANES):
            in_copy(w, slot, c).wait()

        @pl.when(wefirst_ref[w] == 1)
        def _():
            wslot = weslot_ref[w]
            for cp in weight_copies(we_ref[w], wslot):
                cp.wait()
            w1b[...] = wst1[wslot].astype(bf16)
            w3b[...] = wst3[wslot].astype(bf16)
            w2b[...] = wst2[wslot].astype(bf16)

            @pl.when(wenext_ref[w] >= 0)
            def _():
                for cp in weight_copies(wenext_ref[w], 1 - wslot):
                    cp.start()

        xt = _from_planes(xin.at[slot]).astype(bf16)
        h = jax.nn.silu(_dot(xt, w1b[...])) * _dot(xt, w3b[...])
        res = _dot(h.astype(bf16), w2b[...])
        row = lax.broadcasted_iota(jnp.int32, (MOE_TM, D_MODEL), 0)
        mine = jnp.where(row >= wlo_ref[w], row, MOE_TM) < whi_ref[w]

        @pl.when(wfirst_ref[w] == 1)
        def _():
            @pl.when(tile >= 2)
            def _():
                for c in range(N_PLANES):
                    out_copy(0, oslot, c).wait()

            _to_planes(yout.at[oslot], jnp.where(mine, res, 0.0))

        @pl.when(wfirst_ref[w] == 0)
        def _():
            _to_planes(yout.at[oslot], jnp.where(mine, res, _from_planes(yout.at[oslot])))

        @pl.when(wlast_ref[w] == 1)
        def _():
            for c in range(N_PLANES):
                out_copy(tile, oslot, c).start()

        @pl.when(w == n_items - 1)
        def _():
            for c in range(N_PLANES):
                out_copy(0, oslot, c).wait()

            @pl.when(tile >= 1)
            def _():
                for c in range(N_PLANES):
                    out_copy(0, 1 - oslot, c).wait()


def _moe(plan, xs, w1, w3, w2, layer):
    grid_spec = pltpu.PrefetchScalarGridSpec(
        num_scalar_prefetch=len(plan),
        grid=(MOE_MAX_ITEMS,),
        in_specs=[pl.BlockSpec(memory_space=pl.ANY)] * 4,
        out_specs=pl.BlockSpec(memory_space=pl.ANY),
        scratch_shapes=[pltpu.VMEM((2, N_PLANES, MOE_TM, V7X_LANES), f32),
                        pltpu.VMEM((2, N_PLANES, MOE_TM, V7X_LANES), f32),
                        pltpu.SemaphoreType.DMA((2,)), pltpu.SemaphoreType.DMA((2,)),
                        pltpu.VMEM((2, D_MODEL, MOE_HIDDEN), f32), pltpu.VMEM((2, D_MODEL, MOE_HIDDEN), f32),
                        pltpu.VMEM((2, MOE_HIDDEN, D_MODEL), f32), pltpu.SemaphoreType.DMA((2,)),
                        pltpu.VMEM((D_MODEL, MOE_HIDDEN), bf16), pltpu.VMEM((D_MODEL, MOE_HIDDEN), bf16),
                        pltpu.VMEM((MOE_HIDDEN, D_MODEL), bf16)],
    )
    return pl.pallas_call(
        functools.partial(_moe_kernel, layer),
        grid_spec=grid_spec,
        out_shape=jax.ShapeDtypeStruct((N_PAIRS, N_PLANES, V7X_LANES), f32),
        compiler_params=_cparams(("arbitrary",)),
        name="moe",
    )(*plan, xs, w1, w3, w2)


def _combine_kernel(final, pos0_ref, pos1_ref, ys_hbm, x1_ref, route_ref, g_ref, b_ref, *rest):
    if final:
        yp_ref, ysm_ref, buf, sem, t3_ref = rest
    else:
        o_ref, ob_ref, buf, sem = rest
    s = pl.program_id(0)
    nsteps = pl.num_programs(0)

    def gather_copy(row, slot, k, r):
        return pltpu.make_async_copy(ys_hbm.at[row], buf.at[slot, k, :, r, :], sem.at[slot])

    def issue(tile, slot):
        def body(r, c):
            n = tile * CMB_TILE + r
            for k in range(MOE_TOPK):
                gather_copy((pos0_ref, pos1_ref)[k][n], slot, k, r).start(priority=k % 2)
            return c

        lax.fori_loop(0, CMB_TILE, body, 0, unroll=8)

    @pl.when(s == 0)
    def _():
        issue(0, 0)

    @pl.when(s + 1 < nsteps)
    def _():
        issue(s + 1, (s + 1) % 2)

    slot = s % 2
    for k in range(MOE_TOPK):
        pltpu.make_async_copy(buf.at[slot, k], buf.at[slot, k], sem.at[slot]).wait()
    route = route_ref[...]
    moe = route[:, 2:3] * _from_planes(buf.at[slot, 0]) + route[:, 3:4] * _from_planes(buf.at[slot, 1])
    y = _layer_norm_rows(DN_ALPHA * _from_planes(x1_ref) + moe, g_ref[...], b_ref[...])
    if not final:
        o_ref[...] = y
        ob_ref[...] = y.astype(bf16)
        return

    @pl.when(s < CMB_P_STEPS)
    def _():
        t3_ref[...] = y.reshape(CMB_TILE // BATCH, BATCH, D_MODEL)
        for b in range(BATCH):
            yp_ref[b] = t3_ref[:, b, :]

    steps_per_tile = CMB_TILE // DEC_BATCH
    for q in range(N_S // CMB_TILE):
        @pl.when(s == CMB_P_STEPS + q)
        def _():
            for h in range(steps_per_tile):
                ysm_ref[:, q * steps_per_tile + h, :] = y[h * DEC_BATCH:(h + 1) * DEC_BATCH, :]


def _combine(pos, ys, x1, route, g, b, final):
    if final:
        out_specs = [pl.BlockSpec((BATCH, CMB_TILE // BATCH, D_MODEL),
                                  lambda s, *_: (0, jnp.minimum(s, CMB_P_STEPS - 1), 0)),
                     pl.BlockSpec((DEC_BATCH, DEC_SEQ, D_MODEL), lambda s, *_: (0, 0, 0))]
        out_shape = [jax.ShapeDtypeStruct((BATCH, SEQ, D_MODEL), f32),
                     jax.ShapeDtypeStruct((DEC_BATCH, DEC_SEQ, D_MODEL), f32)]
        extra = [pltpu.VMEM((CMB_TILE // BATCH, BATCH, D_MODEL), f32)]
    else:
        out_specs = [pl.BlockSpec((CMB_TILE, D_MODEL), lambda s, *_: (s, 0))] * 2
        out_shape = [jax.ShapeDtypeStruct((N_TOK, D_MODEL), f32), jax.ShapeDtypeStruct((N_TOK, D_MODEL), bf16)]
        extra = []
    grid_spec = pltpu.PrefetchScalarGridSpec(
        num_scalar_prefetch=MOE_TOPK,
        grid=(N_TOK // CMB_TILE,),
        in_specs=[pl.BlockSpec(memory_space=pl.ANY),
                  pl.BlockSpec((N_PLANES, CMB_TILE, V7X_LANES), lambda s, *_: (0, s, 0)),
                  pl.BlockSpec((CMB_TILE, ROUTE_LANES), lambda s, *_: (s, 0)),
                  pl.BlockSpec((1, D_MODEL), lambda s, *_: (0, 0)),
                  pl.BlockSpec((1, D_MODEL), lambda s, *_: (0, 0))],
        out_specs=out_specs,
        scratch_shapes=[pltpu.VMEM((2, MOE_TOPK, N_PLANES, CMB_TILE, V7X_LANES), f32),
                        pltpu.SemaphoreType.DMA((2,))] + extra,
    )
    return pl.pallas_call(
        functools.partial(_combine_kernel, final),
        grid_spec=grid_spec,
        out_shape=out_shape,
        compiler_params=_cparams(("arbitrary",)),
        name="combine_out" if final else "combine",
    )(*pos, ys, x1, route, g, b)


def _to_rows_kernel(xp_ref, xs_ref, o_ref, ob_ref, t3_ref):
    i = pl.program_id(0)

    @pl.when(i < P_TILES)
    def _():
        for b in range(BATCH):
            t3_ref[:, b, :] = xp_ref[b]
        rows = t3_ref[...].reshape(ROW_TILE, D_MODEL)
        o_ref[...] = rows
        ob_ref[...] = rows.astype(bf16)

    @pl.when(i >= P_TILES)
    def _():
        for t in range(DEC_SEQ):
            rows = xs_ref[:, t, :]
            o_ref[t * DEC_BATCH:(t + 1) * DEC_BATCH, :] = rows
            ob_ref[t * DEC_BATCH:(t + 1) * DEC_BATCH, :] = rows.astype(bf16)


def _to_rows(x_prompt, x_sample):
    return pl.pallas_call(
        _to_rows_kernel,
        grid=(N_TILES,),
        in_specs=[pl.BlockSpec((BATCH, P_TC, D_MODEL), lambda i: (0, jnp.minimum(i, P_TILES - 1), 0)),
                  pl.BlockSpec((DEC_BATCH, DEC_SEQ, D_MODEL), lambda i: (0, 0, 0))],
        out_specs=[pl.BlockSpec((ROW_TILE, D_MODEL), lambda i: (i, 0))] * 2,
        out_shape=[jax.ShapeDtypeStruct((N_TOK, D_MODEL), f32), jax.ShapeDtypeStruct((N_TOK, D_MODEL), bf16)],
        scratch_shapes=[pltpu.VMEM((P_TC, BATCH, D_MODEL), f32)],
        compiler_params=_cparams(("arbitrary",)),
        name="to_rows",
    )(x_prompt, x_sample)


def _lookup(table, idx):
    ar = jnp.arange(MOE_EXPERTS, dtype=jnp.int32).reshape((MOE_EXPERTS,) + (1,) * idx.ndim)
    table = table.reshape(ar.shape)
    return jnp.sum(jnp.where(idx[None] == ar, table, 0), axis=0)


def _dispatch_plan(route_t, cnt):
    i32 = jnp.int32
    e = route_t[0:2].astype(i32)
    rank = route_t[4:6].astype(i32)
    counts = cnt[0, :MOE_EXPERTS].astype(i32)
    ends = jnp.cumsum(counts)
    starts = ends - counts
    pos = _lookup(starts, e) + rank

    first_tile = starts // MOE_TM
    last_tile = (ends - 1) // MOE_TM
    ntiles = jnp.where(counts > 0, last_tile - first_tile + 1, 0)
    item_end = jnp.cumsum(ntiles)
    n_items = item_end[-1]
    w = jnp.minimum(jnp.arange(MOE_MAX_ITEMS, dtype=i32), n_items - 1)
    we = jnp.sum((item_end[None, :] <= w[:, None]).astype(i32), axis=-1)
    wt = _lookup(first_tile, we) + w - _lookup(item_end - ntiles, we)
    wlo = jnp.maximum(_lookup(starts, we) - wt * MOE_TM, 0)
    whi = jnp.minimum(_lookup(ends, we) - wt * MOE_TM, MOE_TM)
    changes = (wt[1:] != wt[:-1]).astype(i32)
    wfirst = jnp.concatenate([jnp.ones((1,), i32), changes])
    wlast = jnp.concatenate([changes, jnp.ones((1,), i32)])
    wlast = jnp.where(jnp.arange(MOE_MAX_ITEMS, dtype=i32) == n_items - 1, 1, wlast)
    wefirst = jnp.concatenate([jnp.ones((1,), i32), (we[1:] != we[:-1]).astype(i32)])
    weslot = (jnp.cumsum(wefirst) - 1) % 2
    ar = jnp.arange(MOE_EXPERTS, dtype=i32)
    later = (ar[None, :] > ar[:, None]) & (counts[None, :] > 0)
    next_expert = jnp.min(jnp.where(later, ar[None, :], MOE_EXPERTS), axis=1)
    wenext = _lookup(jnp.where(next_expert < MOE_EXPERTS, next_expert, -1), we)
    return (pos[0], pos[1]), (wt, we, wlo, whi, wfirst, wlast, wefirst, weslot, wenext, n_items.reshape(1))


def _block_diag(w):
    n, a, b = w.shape
    eye = jnp.eye(n, dtype=w.dtype)
    return (w[:, :, None, :] * eye[:, None, :, None]).reshape(n * a, n * b)


def _ssm_params(a_re, a_im, log_dt, b_re, b_im, c_re, c_im):
    ar, ai = a_re, a_im
    dt = jnp.exp(log_dt)[:, None]
    mag = jnp.exp(ar * dt)
    lb_re = mag * jnp.cos(ai * dt)
    lb_im = mag * jnp.sin(ai * dt)
    den = ar * ar + ai * ai
    nr = lb_re - 1.0
    coef_re = (nr * ar + lb_im * ai) / den
    coef_im = (lb_im * ar - nr * ai) / den
    bb_re = coef_re[..., None] * b_re - coef_im[..., None] * b_im
    bb_im = coef_re[..., None] * b_im + coef_im[..., None] * b_re
    gk = SSM_GROUPS // SSM_KB

    def diag_blocks(w):
        return jnp.stack([_block_diag(w[k * gk:(k + 1) * gk]) for k in range(SSM_KB)]).astype(bf16)

    bbre = diag_blocks(bb_re.transpose(0, 2, 1))
    bbim = diag_blocks(bb_im.transpose(0, 2, 1))
    ccre = diag_blocks(c_re.transpose(0, 2, 1))
    ccim = diag_blocks(c_im.transpose(0, 2, 1))
    return (lb_re.reshape(1, SSM_LANES), lb_im.reshape(1, SSM_LANES), bbre, bbim, ccre, ccim)


def kernel(x_prompt, x_sample, state_ret, state_ssm_re, state_ssm_im, state_lru, state_conv, w_in, ret_gn_g, ret_gn_b, ssm_a_re, ssm_a_im, ssm_log_dt, ssm_b_re, ssm_b_im, ssm_c_re, ssm_c_im, ssm_d, ssm_w_glu, lru_conv_w, lru_conv_b, lru_wa, lru_ba, lru_wx, lru_bx, lru_lambda, w_branch, w_out, ln1_g, ln1_b, moe_w_group, moe_b_group, moe_w_expert, moe_b_expert, moe_w1, moe_w3, moe_w2, ln2_g, ln2_b):
    x, xb = _to_rows(x_prompt, x_sample)
    rope = _rope_tables()
    tabs_p = _ret_tables(RET_SUB_T)
    tabs_s = _ret_tables(DEC_SEQ)
    row = lambda v: v.reshape(1, -1)
    w_bf = w_in.astype(bf16)

    outs = [[] for _ in range(10)]
    for l in range(DEPTH):

        s0 = state_ret.reshape(DEPTH, DEC_BATCH, 2, 2 * RET_DK, RET_DV)
        y_ret, ret_p, ret_s = _retention(xb, w_bf, rope, s0, l, tabs_p, tabs_s,
                                         row(ret_gn_g[l]), row(ret_gn_b[l]))

        sp = _ssm_params(ssm_a_re[l], ssm_a_im[l], ssm_log_dt[l], ssm_b_re[l], ssm_b_im[l],
                         ssm_c_re[l], ssm_c_im[l])
        y_ssm, re_p, im_p, re_s, im_s = _ssm(
            xb, w_bf, l, state_ssm_re[l].reshape(DEC_BATCH, SSM_LANES), state_ssm_im[l].reshape(DEC_BATCH, SSM_LANES),
            *sp, row(ssm_d[l]), ssm_w_glu[l].astype(bf16))

        conv0 = state_conv[l].transpose(1, 0, 2).reshape(LRU_HIST_S, BRANCH_W)
        y_lru, lru_p, conv_p, lru_s, conv_s = _lru(
            xb, w_bf, l, state_lru[l], conv0, lru_conv_w[l], row(lru_conv_b[l]),
            _block_diag(lru_wa[l]).astype(bf16), row(lru_ba[l]),
            _block_diag(lru_wx[l]).astype(bf16), row(lru_bx[l]), row(lru_lambda[l]))

        wr = jnp.zeros((D_MODEL, ROUTE_LANES), f32)
        wr = wr.at[:, 0:MOE_GROUPS].set(moe_w_group[l]).at[:, MOE_GROUPS:MOE_GROUPS + MOE_EXPERTS].set(moe_w_expert[l])
        br = jnp.zeros((1, ROUTE_LANES), f32)
        br = br.at[0, 0:MOE_GROUPS].set(moe_b_group[l]).at[0, MOE_GROUPS:MOE_GROUPS + MOE_EXPERTS].set(moe_b_expert[l])
        x1, route, route_t, cnt = _merge(y_ret, y_ssm, y_lru, xb, w_bf, l, x, w_branch[l].astype(bf16), w_out[l].astype(bf16),
                                row(ln1_g[l]), row(ln1_b[l]), wr.astype(bf16), br)

        pos, plan = _dispatch_plan(route_t, cnt)
        xs = _dispatch(pos, x1)
        ys = _moe(plan, xs, moe_w1, moe_w3, moe_w2, l)
        x, xb = _combine(pos, ys, x1, route, row(ln2_g[l]), row(ln2_b[l]), final=(l == DEPTH - 1))

        outs[0].append(ret_p.reshape(BATCH, RET_HEADS, RET_DK, RET_DV))
        outs[1].append(re_p.reshape(BATCH, SSM_GROUPS, SSM_STATE))
        outs[2].append(im_p.reshape(BATCH, SSM_GROUPS, SSM_STATE))
        outs[3].append(lru_p)
        outs[4].append(conv_p.reshape(CONV_W - 1, BATCH, BRANCH_W).transpose(1, 0, 2))
        outs[5].append(ret_s.reshape(DEC_BATCH, RET_HEADS, RET_DK, RET_DV))
        outs[6].append(re_s.reshape(DEC_BATCH, SSM_GROUPS, SSM_STATE))
        outs[7].append(im_s.reshape(DEC_BATCH, SSM_GROUPS, SSM_STATE))
        outs[8].append(lru_s)
        outs[9].append(conv_s.reshape(CONV_W - 1, DEC_BATCH, BRANCH_W).transpose(1, 0, 2))

    y_prompt, y_sample = x, xb
    return (y_prompt, y_sample) + tuple(jnp.stack(o) for o in outs)
```

```python
import functools

import jax
import jax.numpy as jnp
import numpy as np
from jax import lax
from jax.experimental import pallas as pl
from jax.experimental.pallas import tpu as pltpu

f32 = jnp.float32
bf16 = jnp.bfloat16

D_MODEL = 1024
BATCH = 8
SEQ = 2048
DEPTH = 2
DEC_BATCH = 128
DEC_SEQ = 8
PAST_LEN = 16384
BRANCH_W = 512
N_BRANCH = 3
RET_HEADS = 4
RET_DK = 64
RET_DV = 128
ROPE_BASE = 10000.0
SSM_GROUP = 16
SSM_GROUPS = 32
SSM_STATE = 64
SSM_LANES = SSM_GROUPS * SSM_STATE
LRU_BLOCKS = 8
LRU_BW = 64
CONV_W = 4
LRU_C = 8.0
MOE_GROUPS = 4
MOE_PER_GROUP = 8
MOE_EXPERTS = 32
MOE_TOPK = 2
MOE_HIDDEN = 512
DN_ALPHA = (2.0 * DEPTH) ** 0.25
LN_EPS = 1e-5
D_IN = 6144

V7X_SUBLANES = 8
V7X_LANES = 128
V7X_VMEM_LIMIT = 56 * 1024 * 1024

N_P = BATCH * SEQ
N_S = DEC_BATCH * DEC_SEQ
N_TOK = N_P + N_S
ROW_TILE = 1024
P_TILES = N_P // ROW_TILE
N_TILES = N_TOK // ROW_TILE
P_TC = ROW_TILE // BATCH
RET_SUB_T = 32
RET_SUB_R = RET_SUB_T * BATCH
RET_SUBS = ROW_TILE // RET_SUB_R
S_BLOCKS = DEC_BATCH // BATCH
S_BLOCK_R = DEC_SEQ * BATCH
MERGE_TILE = 512
MOE_TM = 512
N_PAIRS = N_TOK * MOE_TOPK
MOE_MAX_ITEMS = N_PAIRS // MOE_TM + MOE_EXPERTS - 1
DSP_TILE = 512
CMB_TILE = 256
CMB_P_STEPS = N_P // CMB_TILE
ROUTE_LANES = 128


PROJ_RC = 256
RET_COLS, RET_BLK = 1536, 0
SSM_COLS, SSM_BLK = 512, 3
LRU_COLS, LRU_BLK = 1024, 2
GATE_COLS, GATE_BLK = 3072, 1


def _cparams(sem):
    return pltpu.CompilerParams(dimension_semantics=sem, vmem_limit_bytes=V7X_VMEM_LIMIT)


N_PLANES = D_MODEL // V7X_LANES


def _to_planes(ref, rows):
    for c in range(N_PLANES):
        ref[c] = rows[:, c * V7X_LANES:(c + 1) * V7X_LANES]


def _from_planes(ref):
    return jnp.concatenate([ref[c] for c in range(N_PLANES)], axis=1)


def _const_spec(block_shape, index):
    return pl.BlockSpec(block_shape, lambda *_: index, pipeline_mode=pl.Buffered(1))


def _dot(a, b):
    return jnp.dot(a, b, preferred_element_type=f32)


def _dot_nt(a, b):
    return lax.dot_general(a, b, (((1,), (1,)), ((), ())), preferred_element_type=f32)


def _dot_tn(a, b):
    return lax.dot_general(a, b, (((0,), (0,)), ((), ())), preferred_element_type=f32)


def _layer_norm_rows(x, g, b):
    mu = jnp.mean(x, -1, keepdims=True)
    xc = x - mu
    var = jnp.mean(xc * xc, -1, keepdims=True)
    return xc * lax.rsqrt(var + LN_EPS) * g + b


def _project_rows(xb_ref, w_ref, z_ref):
    for rc in range(ROW_TILE // PROJ_RC):
        rs = slice(rc * PROJ_RC, (rc + 1) * PROJ_RC)
        z_ref[rs, :] = _dot(xb_ref[rs, :], w_ref[0])


def _ret_block(q, k, v, g, cosb, sinb, mask_ref, qdec_ref, kdec_ref, cdec_ref, scat_ref, gng, gnb):
    rows = q.shape[0]
    lane_qk = lax.broadcasted_iota(jnp.int32, (rows, 2 * V7X_LANES), 1)
    first_half = (lane_qk & (RET_DK - 1)) < (RET_DK // 2)

    def rope(x):
        partner = jnp.where(first_half, pltpu.roll(x, 2 * V7X_LANES - RET_DK // 2, 1),
                            pltpu.roll(x, RET_DK // 2, 1))
        return x * cosb + partner * sinb

    q = rope(q)
    k = rope(k) * (RET_DK ** -0.5)
    kd = k * kdec_ref[...]
    lane = lax.broadcasted_iota(jnp.int32, (rows, V7X_LANES), 1)
    row_b = lax.broadcasted_iota(jnp.int32, (rows, V7X_LANES), 0) & (BATCH - 1)
    outs = []
    for p in range(2):
        qp = q[:, p * V7X_LANES:(p + 1) * V7X_LANES]
        kp = k[:, p * V7X_LANES:(p + 1) * V7X_LANES].astype(bf16)
        kdp = kd[:, p * V7X_LANES:(p + 1) * V7X_LANES]
        s_old = scat_ref[p]
        s_bf = s_old.astype(bf16)
        s_new = s_old * jnp.concatenate([cdec_ref[p]] * BATCH, axis=1)
        for hh in range(2):
            h = 2 * p + hh
            head_lanes = (lane >= RET_DK) if hh else (lane < RET_DK)
            qh = jnp.where(head_lanes, qp, 0.0).astype(bf16)
            kdh = jnp.where(head_lanes, kdp, 0.0).astype(bf16)
            vh = v[:, h * RET_DV:(h + 1) * RET_DV]
            vh_bf = vh.astype(bf16)
            sc = _dot_nt(qh, kp) * mask_ref[h]
            o = _dot(sc.astype(bf16), vh_bf)
            cross = _dot(qh, s_bf)
            oc = jnp.zeros((rows, RET_DV), f32)
            for b in range(BATCH):
                oc = oc + jnp.where(row_b == b, cross[:, b * RET_DV:(b + 1) * RET_DV], 0.0)
            o = o + oc * qdec_ref[h]
            vcat = jnp.concatenate([jnp.where(row_b == b, vh_bf, jnp.zeros_like(vh_bf))
                                    for b in range(BATCH)], axis=1)
            s_new = s_new + _dot_tn(kdh, vcat)
            mu = jnp.mean(o, -1, keepdims=True)
            oc2 = o - mu
            var = jnp.mean(oc2 * oc2, -1, keepdims=True)
            outs.append(oc2 * lax.rsqrt(var + LN_EPS))
        scat_ref[p] = s_new
    o = jnp.concatenate(outs, axis=1) * gng + gnb
    return jax.nn.silu(g) * o


def _per_step_rows(tab_ref, t0, steps):
    return jnp.concatenate(
        [jnp.broadcast_to(tab_ref[t0 + t:t0 + t + 1, :], (BATCH, tab_ref.shape[1])) for t in range(steps)], axis=0)


def _ret_kernel(xb_ref, w_ref, cos_ref, sin_ref, cos_s_ref, sin_s_ref, s0_ref,
                mask_p_ref, qdec_p_ref, kdec_p_ref, cdec_p_ref,
                mask_s_ref, qdec_s_ref, kdec_s_ref, cdec_s_ref,
                gng_ref, gnb_ref,
                y_ref, retp_ref, rets_ref, scat_ref, z_ref):
    i = pl.program_id(0)
    gng = gng_ref[...]
    gnb = gnb_ref[...]
    q_cols, k_cols = slice(0, 256), slice(256, 512)
    v_cols, g_cols = slice(512, 1024), slice(1024, 1536)

    @pl.when(i == 0)
    def _():
        scat_ref[...] = jnp.zeros_like(scat_ref)

    @pl.when(i <= P_TILES)
    def _():
        _project_rows(xb_ref, w_ref, z_ref)

    @pl.when(i < P_TILES)
    def _():
        for sc in range(RET_SUBS):
            r0 = sc * RET_SUB_R
            rs = slice(r0, r0 + RET_SUB_R)
            y_ref[rs, :] = _ret_block(
                z_ref[rs, q_cols], z_ref[rs, k_cols], z_ref[rs, v_cols], z_ref[rs, g_cols],
                _per_step_rows(cos_ref, sc * RET_SUB_T, RET_SUB_T), _per_step_rows(sin_ref, sc * RET_SUB_T, RET_SUB_T),
                mask_p_ref, qdec_p_ref, kdec_p_ref, cdec_p_ref, scat_ref, gng, gnb)

    @pl.when(i == P_TILES - 1)
    def _():
        for b in range(BATCH):
            for p in range(2):
                retp_ref[b, p] = scat_ref[p, :, b * RET_DV:(b + 1) * RET_DV]

    @pl.when(i >= P_TILES)
    def _():
        bb = i - P_TILES
        for b in range(BATCH):
            for p in range(2):
                scat_ref[p, :, b * RET_DV:(b + 1) * RET_DV] = s0_ref[0, b, p]

        def rows_of(cols):
            return jnp.concatenate(
                [z_ref[pl.ds(pl.multiple_of(t * DEC_BATCH + bb * BATCH, BATCH), BATCH), cols]
                 for t in range(DEC_SEQ)], axis=0)

        y = _ret_block(
            rows_of(q_cols), rows_of(k_cols), rows_of(v_cols),
            rows_of(g_cols), _per_step_rows(cos_s_ref, 0, DEC_SEQ), _per_step_rows(sin_s_ref, 0, DEC_SEQ),
            mask_s_ref, qdec_s_ref, kdec_s_ref, cdec_s_ref, scat_ref, gng, gnb)
        for t in range(DEC_SEQ):
            y_ref[pl.ds(pl.multiple_of(t * DEC_BATCH + bb * BATCH, BATCH), BATCH), :] = (
                y[t * BATCH:(t + 1) * BATCH, :])
        for b in range(BATCH):
            for p in range(2):
                rets_ref[b, p] = scat_ref[p, :, b * RET_DV:(b + 1) * RET_DV]


def _ret_tables(tc):
    rows = tc * BATCH
    nf = np.float32
    log_g = np.log1p(-np.exp2(nf(-5.0) - np.arange(RET_HEADS, dtype=nf))).astype(nf)
    t_idx = (np.arange(rows) // BATCH).astype(nf)
    b_idx = np.arange(rows) % BATCH
    rel = t_idx[:, None] - t_idx[None, :]
    same = b_idx[:, None] == b_idx[None, :]
    decay = np.exp(log_g[:, None, None] * np.maximum(rel, nf(0.0)))
    mask = np.where((rel >= 0) & same, decay, nf(0.0))
    qdec = np.exp(log_g[:, None] * (t_idx[None, :] + nf(1.0)))
    qdec = np.broadcast_to(qdec[:, :, None], (RET_HEADS, rows, RET_DV))
    kdec = np.exp(log_g[:, None] * (nf(tc - 1.0) - t_idx[None, :]))
    kdec = np.broadcast_to(kdec.T[:, :, None], (rows, RET_HEADS, RET_DK)).reshape(rows, RET_HEADS * RET_DK)
    cdec = np.exp(log_g * nf(tc))
    cdec = np.broadcast_to(cdec[:, None, None], (RET_HEADS, RET_DK, RET_DV)).reshape(2, 2 * RET_DK, RET_DV)
    return tuple(jnp.asarray(np.ascontiguousarray(a), dtype=f32) for a in (mask, qdec, kdec, cdec))


def _rope_tables():
    half = RET_DK // 2
    inv = ROPE_BASE ** (-jnp.arange(half, dtype=f32) / half)
    pos_p = jnp.arange(SEQ, dtype=f32)
    pos_s = PAST_LEN + jnp.arange(DEC_SEQ, dtype=f32)

    def tab(pos):
        ang = pos[:, None] * inv[None, :]
        cos = jnp.cos(ang)
        sin = jnp.sin(ang)
        cos_h = jnp.concatenate([cos, cos], axis=1)
        sin_h = jnp.concatenate([-sin, sin], axis=1)
        return jnp.tile(cos_h, (1, RET_HEADS)), jnp.tile(sin_h, (1, RET_HEADS))

    return tab(pos_p) + tab(pos_s)


def _retention(xb, w_bf, rope, s0_s, layer, tabs_p, tabs_s, gng, gnb):
    n_steps = P_TILES + S_BLOCKS
    tile = lambda i: jnp.minimum(i, P_TILES)
    sblk = lambda i: jnp.maximum(i - P_TILES, 0)
    full = lambda a: pl.BlockSpec(a.shape, lambda i, _n=a.ndim: (0,) * _n)
    state_blk = (BATCH, 2, 2 * RET_DK, RET_DV)
    cos_p, sin_p, cos_s, sin_s = rope
    ptile = lambda i: jnp.minimum(i, P_TILES - 1)
    ins = [xb, w_bf, cos_p, sin_p, cos_s, sin_s, s0_s, *tabs_p, *tabs_s, gng, gnb]
    in_specs = [
        pl.BlockSpec((ROW_TILE, D_MODEL), lambda i: (tile(i), 0)),
        _const_spec((1, D_MODEL, RET_COLS), (layer, 0, RET_BLK)),
        pl.BlockSpec((P_TC, 256), lambda i: (ptile(i), 0)),
        pl.BlockSpec((P_TC, 256), lambda i: (ptile(i), 0)),
        full(cos_s), full(sin_s),
        pl.BlockSpec((1,) + state_blk, lambda i: (layer, sblk(i), 0, 0, 0)),
    ] + [full(a) for a in (*tabs_p, *tabs_s, gng, gnb)]
    return pl.pallas_call(
        _ret_kernel,
        grid=(n_steps,),
        in_specs=in_specs,
        out_specs=[pl.BlockSpec((ROW_TILE, BRANCH_W), lambda i: (tile(i), 0)),
                   pl.BlockSpec(state_blk, lambda i: (0, 0, 0, 0)),
                   pl.BlockSpec(state_blk, lambda i: (sblk(i), 0, 0, 0))],
        out_shape=[jax.ShapeDtypeStruct((N_TOK, BRANCH_W), f32),
                   jax.ShapeDtypeStruct((BATCH, 2, 2 * RET_DK, RET_DV), f32),
                   jax.ShapeDtypeStruct((DEC_BATCH, 2, 2 * RET_DK, RET_DV), f32)],
        scratch_shapes=[pltpu.VMEM((2, 2 * RET_DK, BATCH * RET_DV), f32),
                        pltpu.VMEM((ROW_TILE, RET_COLS), f32)],
        compiler_params=_cparams(("arbitrary",)),
        name="retention",
    )(*ins)


SSM_LB = 512
SSM_RC = 256
SSM_KB = 2
SSM_KB_U = BRANCH_W // SSM_KB
SSM_KB_H = SSM_LANES // SSM_KB


def _ssm_scan(bre_ref, bim_ref, lre_ref, lim_ref, h_re0, h_im0, row0, nb_rows, steps, lb, unroll):
    ls = slice(lb * SSM_LB, (lb + 1) * SSM_LB)
    a_re = jnp.broadcast_to(lre_ref[:, ls], (V7X_SUBLANES, SSM_LB))
    a_im = jnp.broadcast_to(lim_ref[:, ls], (V7X_SUBLANES, SSM_LB))

    def step(t, carry):
        h_re, h_im = carry
        r = pl.multiple_of(row0 + t * nb_rows, V7X_SUBLANES)
        n_re = a_re * h_re - a_im * h_im + bre_ref[pl.ds(r, V7X_SUBLANES), ls]
        n_im = a_re * h_im + a_im * h_re + bim_ref[pl.ds(r, V7X_SUBLANES), ls]
        bre_ref[pl.ds(r, V7X_SUBLANES), ls] = n_re
        bim_ref[pl.ds(r, V7X_SUBLANES), ls] = n_im
        return n_re, n_im

    return lax.fori_loop(0, steps, step, (h_re0, h_im0), unroll=unroll)


def _ssm_kernel(xb_ref, w_ref, h0re_ref, h0im_ref, lre_ref, lim_ref, bbre_ref, bbim_ref, ccre_ref, ccim_ref,
                d_ref, wglu_ref,
                y_ref, pre_ref, pim_ref, sre_ref, sim_ref,
                bre_ref, bim_ref, hre_ref, him_ref, zs_ref):
    i = pl.program_id(0)
    _project_rows(xb_ref, w_ref, zs_ref)
    for rc in range(ROW_TILE // SSM_RC):
        rs = slice(rc * SSM_RC, (rc + 1) * SSM_RC)
        ub = zs_ref[rs, :].astype(bf16)
        for k in range(SSM_KB):
            uk = ub[:, k * SSM_KB_U:(k + 1) * SSM_KB_U]
            hs = slice(k * SSM_KB_H, (k + 1) * SSM_KB_H)
            bre_ref[rs, hs] = _dot(uk, bbre_ref[k])
            bim_ref[rs, hs] = _dot(uk, bbim_ref[k])

    @pl.when(i == 0)
    def _():
        hre_ref[...] = jnp.zeros_like(hre_ref)
        him_ref[...] = jnp.zeros_like(him_ref)

    @pl.when(i < P_TILES)
    def _():
        for lb in range(SSM_LANES // SSM_LB):
            ls = slice(lb * SSM_LB, (lb + 1) * SSM_LB)
            h_re, h_im = _ssm_scan(bre_ref, bim_ref, lre_ref, lim_ref, hre_ref[:, ls], him_ref[:, ls],
                                   0, BATCH, P_TC, lb, 8)
            hre_ref[:, ls] = h_re
            him_ref[:, ls] = h_im
        pre_ref[...] = hre_ref[...]
        pim_ref[...] = him_ref[...]

    @pl.when(i >= P_TILES)
    def _():
        def per_row_tile(rt, c):
            r0 = pl.multiple_of(rt * V7X_SUBLANES, V7X_SUBLANES)
            for lb in range(SSM_LANES // SSM_LB):
                ls = slice(lb * SSM_LB, (lb + 1) * SSM_LB)
                _ssm_scan(bre_ref, bim_ref, lre_ref, lim_ref,
                          h0re_ref[pl.ds(r0, V7X_SUBLANES), ls], h0im_ref[pl.ds(r0, V7X_SUBLANES), ls],
                          r0, DEC_BATCH, DEC_SEQ, lb, True)
            return c

        lax.fori_loop(0, DEC_BATCH // V7X_SUBLANES, per_row_tile, 0)
        last = (DEC_SEQ - 1) * DEC_BATCH
        sre_ref[...] = bre_ref[last:last + DEC_BATCH, :]
        sim_ref[...] = bim_ref[last:last + DEC_BATCH, :]

    for rc in range(ROW_TILE // SSM_RC):
        rs = slice(rc * SSM_RC, (rc + 1) * SSM_RC)
        ch = []
        for k in range(SSM_KB):
            hs = slice(k * SSM_KB_H, (k + 1) * SSM_KB_H)
            ch.append(_dot(bre_ref[rs, hs].astype(bf16), ccre_ref[k]) - _dot(bim_ref[rs, hs].astype(bf16), ccim_ref[k]))
        y = jnp.concatenate(ch, axis=1) + d_ref[...] * zs_ref[rs, :]
        zz = jax.nn.gelu(y)
        y_ref[rs, :] = zz * jax.nn.sigmoid(_dot(zz.astype(bf16), wglu_ref[...]))


def _ssm(xb, w_bf, layer, h0re, h0im, lre, lim, bbre, bbim, ccre, ccim, dvec, wglu):
    full = lambda a: _const_spec(a.shape, (0,) * a.ndim)
    consts = (h0re, h0im, lre, lim, bbre, bbim, ccre, ccim, dvec, wglu)
    return pl.pallas_call(
        _ssm_kernel,
        grid=(N_TILES,),
        in_specs=[pl.BlockSpec((ROW_TILE, D_MODEL), lambda i: (i, 0)),
                  _const_spec((1, D_MODEL, SSM_COLS), (layer, 0, SSM_BLK))] + [full(a) for a in consts],
        out_specs=[pl.BlockSpec((ROW_TILE, BRANCH_W), lambda i: (i, 0)),
                   pl.BlockSpec((BATCH, SSM_LANES), lambda i: (0, 0)),
                   pl.BlockSpec((BATCH, SSM_LANES), lambda i: (0, 0)),
                   pl.BlockSpec((DEC_BATCH, SSM_LANES), lambda i: (0, 0)),
                   pl.BlockSpec((DEC_BATCH, SSM_LANES), lambda i: (0, 0))],
        out_shape=[jax.ShapeDtypeStruct((N_TOK, BRANCH_W), f32),
                   jax.ShapeDtypeStruct((BATCH, SSM_LANES), f32),
                   jax.ShapeDtypeStruct((BATCH, SSM_LANES), f32),
                   jax.ShapeDtypeStruct((DEC_BATCH, SSM_LANES), f32),
                   jax.ShapeDtypeStruct((DEC_BATCH, SSM_LANES), f32)],
        scratch_shapes=[pltpu.VMEM((ROW_TILE, SSM_LANES), f32), pltpu.VMEM((ROW_TILE, SSM_LANES), f32),
                        pltpu.VMEM((BATCH, SSM_LANES), f32), pltpu.VMEM((BATCH, SSM_LANES), f32),
                        pltpu.VMEM((ROW_TILE, SSM_COLS), f32)],
        compiler_params=_cparams(("arbitrary",)),
        name="ssm",
    )(xb, w_bf, *consts)


LRU_HIST_P = (CONV_W - 1) * BATCH
LRU_HIST_S = (CONV_W - 1) * DEC_BATCH


def _lru_gates(xe_ref, nb_rows, cw_ref, cb_ref, wa_ref, ba_ref, wx_ref, bx_ref, lam_ref):
    xc = cb_ref[...] + xe_ref[0:ROW_TILE, :] * cw_ref[0:1, :]
    for j in range(1, CONV_W):
        xc = xc + xe_ref[j * nb_rows:j * nb_rows + ROW_TILE, :] * cw_ref[j:j + 1, :]
    xcb = xc.astype(bf16)
    r = jax.nn.sigmoid(_dot(xcb, wa_ref[...]) + ba_ref[...])
    ig = jax.nn.sigmoid(_dot(xcb, wx_ref[...]) + bx_ref[...])
    log_a = -LRU_C * r * jax.nn.softplus(-lam_ref[...])
    a = jnp.exp(log_a)
    b = jnp.sqrt(-jnp.tanh(log_a) * (a * a + 1.0)) * (ig * xc)
    return a, b


def _lru_kernel(xb_ref, w_ref, h0_ref, conv0_ref, cw_ref, cb_ref, wa_ref, ba_ref, wx_ref, bx_ref, lam_ref,
                y_ref, hp_ref, convp_ref, hs_ref, convs_ref,
                xe_ref, a_ref, b_ref, hc_ref, z_ref):
    i = pl.program_id(0)
    params = (cw_ref, cb_ref, wa_ref, ba_ref, wx_ref, bx_ref, lam_ref)
    _project_rows(xb_ref, w_ref, z_ref)
    zx_ref = z_ref.at[:, 0:BRANCH_W]
    zg_ref = z_ref.at[:, BRANCH_W:2 * BRANCH_W]

    @pl.when(i == 0)
    def _():
        xe_ref[0:LRU_HIST_P, :] = jnp.zeros((LRU_HIST_P, BRANCH_W), f32)
        hc_ref[...] = jnp.zeros_like(hc_ref)

    @pl.when(i < P_TILES)
    def _():
        xe_ref[LRU_HIST_P:LRU_HIST_P + ROW_TILE, :] = zx_ref[...]
        a, b = _lru_gates(xe_ref, BATCH, *params)
        a_ref[...] = a
        b_ref[...] = b
        hist = xe_ref[ROW_TILE:ROW_TILE + LRU_HIST_P, :]
        xe_ref[0:LRU_HIST_P, :] = hist
        convp_ref[...] = hist

        def step(t, h):
            r = pl.multiple_of(t * BATCH, BATCH)
            h = a_ref[pl.ds(r, BATCH), :] * h + b_ref[pl.ds(r, BATCH), :]
            b_ref[pl.ds(r, BATCH), :] = h
            return h

        h = lax.fori_loop(0, P_TC, step, hc_ref[...], unroll=8)
        hc_ref[...] = h
        hp_ref[...] = h

    @pl.when(i >= P_TILES)
    def _():
        xe_ref[0:LRU_HIST_S, :] = conv0_ref[...]
        xe_ref[LRU_HIST_S:LRU_HIST_S + ROW_TILE, :] = zx_ref[...]
        a, b = _lru_gates(xe_ref, DEC_BATCH, *params)
        a_ref[...] = a
        b_ref[...] = b
        convs_ref[...] = xe_ref[ROW_TILE:ROW_TILE + LRU_HIST_S, :]

        def per_row_tile(rt, c):
            r0 = pl.multiple_of(rt * V7X_SUBLANES, V7X_SUBLANES)
            h = h0_ref[pl.ds(r0, V7X_SUBLANES), :]
            for t in range(DEC_SEQ):
                r = pl.multiple_of(t * DEC_BATCH + r0, V7X_SUBLANES)
                h = a_ref[pl.ds(r, V7X_SUBLANES), :] * h + b_ref[pl.ds(r, V7X_SUBLANES), :]
                b_ref[pl.ds(r, V7X_SUBLANES), :] = h
            return c

        lax.fori_loop(0, DEC_BATCH // V7X_SUBLANES, per_row_tile, 0)
        last = (DEC_SEQ - 1) * DEC_BATCH
        hs_ref[...] = b_ref[last:last + DEC_BATCH, :]

    y_ref[...] = b_ref[...] * jax.nn.gelu(zg_ref[...])


def _lru(xb, w_bf, layer, h0, conv0, cw, cb, wa, ba, wx, bx, lam):
    full = lambda a: _const_spec(a.shape, (0,) * a.ndim)
    consts = (h0, conv0, cw, cb, wa, ba, wx, bx, lam)
    return pl.pallas_call(
        _lru_kernel,
        grid=(N_TILES,),
        in_specs=[pl.BlockSpec((ROW_TILE, D_MODEL), lambda i: (i, 0)),
                  _const_spec((1, D_MODEL, LRU_COLS), (layer, 0, LRU_BLK))] + [full(a) for a in consts],
        out_specs=[pl.BlockSpec((ROW_TILE, BRANCH_W), lambda i: (i, 0)),
                   pl.BlockSpec((BATCH, BRANCH_W), lambda i: (0, 0)),
                   pl.BlockSpec((LRU_HIST_P, BRANCH_W), lambda i: (0, 0)),
                   pl.BlockSpec((DEC_BATCH, BRANCH_W), lambda i: (0, 0)),
                   pl.BlockSpec((LRU_HIST_S, BRANCH_W), lambda i: (0, 0))],
        out_shape=[jax.ShapeDtypeStruct((N_TOK, BRANCH_W), f32),
                   jax.ShapeDtypeStruct((BATCH, BRANCH_W), f32),
                   jax.ShapeDtypeStruct((LRU_HIST_P, BRANCH_W), f32),
                   jax.ShapeDtypeStruct((DEC_BATCH, BRANCH_W), f32),
                   jax.ShapeDtypeStruct((LRU_HIST_S, BRANCH_W), f32)],
        scratch_shapes=[pltpu.VMEM((ROW_TILE + LRU_HIST_S, BRANCH_W), f32),
                        pltpu.VMEM((ROW_TILE, BRANCH_W), f32), pltpu.VMEM((ROW_TILE, BRANCH_W), f32),
                        pltpu.VMEM((BATCH, BRANCH_W), f32),
                        pltpu.VMEM((ROW_TILE, LRU_COLS), f32)],
        compiler_params=_cparams(("arbitrary",)),
        name="lru",
    )(xb, w_bf, *consts)


def _merge_kernel(yr_ref, ys_ref, yl_ref, xb_ref, wg_ref, x_ref, wb_ref, wo_ref, g_ref, b_ref, wr_ref, br_ref,
                  x1_ref, route_ref, route_t_ref, cnt_out_ref, cnt_ref):
    merged = jnp.zeros((MERGE_TILE, D_MODEL), f32)
    xb = xb_ref[...]
    for n, y_ref in enumerate((yr_ref, ys_ref, yl_ref)):
        proj = _dot(y_ref[...].astype(bf16), wb_ref[n])
        gate = jax.nn.sigmoid(_dot(xb, wg_ref[0, :, n * D_MODEL:(n + 1) * D_MODEL]))
        merged = merged + gate * proj
    mix = _dot(merged.astype(bf16), wo_ref[...])
    x1 = _layer_norm_rows(DN_ALPHA * x_ref[...] + mix, g_ref[...], b_ref[...])
    _to_planes(x1_ref, x1)

    logits = _dot(x1.astype(bf16), wr_ref[...]) + br_ref[...]
    lane = lax.broadcasted_iota(jnp.int32, (MERGE_TILE, ROUTE_LANES), 1).astype(f32)
    big = jnp.float32(ROUTE_LANES)
    neg = jnp.float32(-jnp.inf)
    is_g = lane < MOE_GROUPS
    lg = jnp.where(is_g, logits, neg)
    mg = jnp.max(lg, -1, keepdims=True)
    gsel = jnp.min(jnp.where(lg == mg, lane, big), -1, keepdims=True)
    sum_g = jnp.sum(jnp.where(is_g, jnp.exp(lg - mg), 0.0), -1, keepdims=True)
    pg_sel = 1.0 / sum_g
    lo = MOE_GROUPS + gsel * MOE_PER_GROUP
    is_e = jnp.abs(lane - lo - 0.5 * (MOE_PER_GROUP - 1)) < 0.5 * MOE_PER_GROUP
    le = jnp.where(is_e, logits, neg)
    me = jnp.max(le, -1, keepdims=True)
    ex = jnp.where(is_e, jnp.exp(le - me), 0.0)
    pe = jnp.where(is_e, ex / jnp.sum(ex, -1, keepdims=True), -1.0)
    v1 = jnp.max(pe, -1, keepdims=True)
    i1 = jnp.min(jnp.where(pe == v1, lane, big), -1, keepdims=True)
    pe2 = jnp.where(lane == i1, -1.0, pe)
    v2 = jnp.max(pe2, -1, keepdims=True)
    i2 = jnp.min(jnp.where(pe2 == v2, lane, big), -1, keepdims=True)
    vsum = v1 + v2
    w1 = pg_sel * v1 / vsum
    w2 = pg_sel * v2 / vsum
    e1 = i1 - MOE_GROUPS
    e2 = i2 - MOE_GROUPS

    @pl.when(pl.program_id(0) == 0)
    def _():
        cnt_ref[...] = jnp.zeros_like(cnt_ref)

    oh1 = lane == e1
    oh2 = lane == e2
    ohs = jnp.where(oh1, 1.0, jnp.where(oh2, 1.0, 0.0))
    r_i = lax.broadcasted_iota(jnp.int32, (MERGE_TILE, MERGE_TILE), 0)
    c_i = lax.broadcasted_iota(jnp.int32, (MERGE_TILE, MERGE_TILE), 1)
    strict_lower = jnp.where(c_i < r_i, 1.0, 0.0).astype(bf16)
    before = _dot(strict_lower, ohs.astype(bf16)) + cnt_ref[0:1, :]
    rank1 = jnp.sum(jnp.where(oh1, before, 0.0), -1, keepdims=True)
    rank2 = jnp.sum(jnp.where(oh2, before, 0.0), -1, keepdims=True)
    cnt_ref[0:1, :] = cnt_ref[0:1, :] + jnp.sum(ohs, 0, keepdims=True)
    cnt_out_ref[...] = cnt_ref[...]

    route = jnp.zeros((MERGE_TILE, ROUTE_LANES), f32)
    for k, val in enumerate((e1, e2, w1, w2, rank1, rank2)):
        route = jnp.where(lane == k, val, route)
    route_ref[...] = route
    route_t_ref[...] = route.T[0:V7X_SUBLANES, :]


def _merge(y_ret, y_ssm, y_lru, xb, w_bf, layer, x, wb, wo, g, b, wr, br):
    full = lambda a: _const_spec(a.shape, (0,) * a.ndim)
    row = lambda w: pl.BlockSpec((MERGE_TILE, w), lambda i: (i, 0))
    consts = (wb, wo, g, b, wr, br)
    return pl.pallas_call(
        _merge_kernel,
        grid=(N_TOK // MERGE_TILE,),
        in_specs=[row(BRANCH_W), row(BRANCH_W), row(BRANCH_W), row(D_MODEL),
                  _const_spec((1, D_MODEL, GATE_COLS), (layer, 0, GATE_BLK)),
                  row(D_MODEL)] + [full(a) for a in consts],
        out_specs=[pl.BlockSpec((N_PLANES, MERGE_TILE, V7X_LANES), lambda i: (0, i, 0)), row(ROUTE_LANES),
                   pl.BlockSpec((V7X_SUBLANES, MERGE_TILE), lambda i: (0, i)),
                   pl.BlockSpec((V7X_SUBLANES, ROUTE_LANES), lambda i: (0, 0))],
        out_shape=[jax.ShapeDtypeStruct((N_PLANES, N_TOK, V7X_LANES), f32),
                   jax.ShapeDtypeStruct((N_TOK, ROUTE_LANES), f32),
                   jax.ShapeDtypeStruct((V7X_SUBLANES, N_TOK), f32),
                   jax.ShapeDtypeStruct((V7X_SUBLANES, ROUTE_LANES), f32)],
        scratch_shapes=[pltpu.VMEM((V7X_SUBLANES, ROUTE_LANES), f32)],
        compiler_params=_cparams(("arbitrary",)),
        name="merge",
    )(y_ret, y_ssm, y_lru, xb, w_bf, x, *consts)


def _dispatch_kernel(pos0_ref, pos1_ref, x1_ref, xs_hbm, sem):
    base = pl.program_id(0) * DSP_TILE

    def row_copy(r, dst_row):
        return pltpu.make_async_copy(x1_ref.at[:, r, :], xs_hbm.at[dst_row], sem.at[0])

    for r in range(DSP_TILE):
        for k in range(MOE_TOPK):
            row_copy(r, (pos0_ref, pos1_ref)[k][base + r]).start(priority=k % 2)
    for _ in range(MOE_TOPK):
        pltpu.make_async_copy(x1_ref, x1_ref, sem.at[0]).wait()


def _dispatch(pos, x1p):
    grid_spec = pltpu.PrefetchScalarGridSpec(
        num_scalar_prefetch=MOE_TOPK,
        grid=(N_TOK // DSP_TILE,),
        in_specs=[pl.BlockSpec((N_PLANES, DSP_TILE, V7X_LANES), lambda s, *_: (0, s, 0))],
        out_specs=pl.BlockSpec(memory_space=pl.ANY),
        scratch_shapes=[pltpu.SemaphoreType.DMA((1,))],
    )
    return pl.pallas_call(
        _dispatch_kernel,
        grid_spec=grid_spec,
        out_shape=jax.ShapeDtypeStruct((N_PAIRS, N_PLANES, V7X_LANES), f32),
        compiler_params=_cparams(("arbitrary",)),
        name="dispatch",
    )(*pos, x1p)


def _moe_kernel(layer, wt_ref, we_ref, wlo_ref, whi_ref, wfirst_ref, wlast_ref, wefirst_ref, weslot_ref,
                wenext_ref, nw_ref,
                xs_hbm, w1_hbm, w3_hbm, w2_hbm, ys_hbm,
                xin, yout, sem_in, sem_out, wst1, wst3, wst2, sem_w, w1b, w3b, w2b):
    w = pl.program_id(0)
    n_items = nw_ref[0]

    def weight_copies(expert, slot):
        return [pltpu.make_async_copy(src.at[layer, expert], dst.at[slot], sem_w.at[slot])
                for src, dst in ((w1_hbm, wst1), (w3_hbm, wst3), (w2_hbm, wst2))]

    def in_copy(item, slot, c):
        r0 = pl.multiple_of(wt_ref[item] * MOE_TM, MOE_TM)
        return pltpu.make_async_copy(xs_hbm.at[pl.ds(r0, MOE_TM), c, :], xin.at[slot, c], sem_in.at[slot])

    def out_copy(tile, slot, c):
        r0 = pl.multiple_of(tile * MOE_TM, MOE_TM)
        return pltpu.make_async_copy(yout.at[slot, c], ys_hbm.at[pl.ds(r0, MOE_TM), c, :], sem_out.at[slot])

    @pl.when(w == 0)
    def _():
        for cp in weight_copies(we_ref[0], 0):
            cp.start()
        for c in range(N_PLANES):
            in_copy(0, 0, c).start(priority=1)

    @pl.when(w + 1 < n_items)
    def _():
        for c in range(N_PLANES):
            in_copy(w + 1, (w + 1) % 2, c).start(priority=1)

    @pl.when(w < n_items)
    def _():
        slot = w % 2
        tile = wt_ref[w]
        oslot = tile % 2
        for c in range(N_PLANES):
            in_copy(w, slot, c).wait()

        @pl.when(wefirst_ref[w] == 1)
        def _():
            wslot = weslot_ref[w]
            for cp in weight_copies(we_ref[w], wslot):
                cp.wait()
            w1b[...] = wst1[wslot].astype(bf16)
            w3b[...] = wst3[wslot].astype(bf16)
            w2b[...] = wst2[wslot].astype(bf16)

            @pl.when(wenext_ref[w] >= 0)
            def _():
                for cp in weight_copies(wenext_ref[w], 1 - wslot):
                    cp.start()

        xt = _from_planes(xin.at[slot]).astype(bf16)
        h = jax.nn.silu(_dot(xt, w1b[...])) * _dot(xt, w3b[...])
        res = _dot(h.astype(bf16), w2b[...])
        row = lax.broadcasted_iota(jnp.int32, (MOE_TM, D_MODEL), 0)
        mine = jnp.where(row >= wlo_ref[w], row, MOE_TM) < whi_ref[w]

        @pl.when(wfirst_ref[w] == 1)
        def _():
            @pl.when(tile >= 2)
            def _():
                for c in range(N_PLANES):
                    out_copy(0, oslot, c).wait()

            _to_planes(yout.at[oslot], jnp.where(mine, res, 0.0))

        @pl.when(wfirst_ref[w] == 0)
        def _():
            _to_planes(yout.at[oslot], jnp.where(mine, res, _from_planes(yout.at[oslot])))

        @pl.when(wlast_ref[w] == 1)
        def _():
            for c in range(N_PLANES):
                out_copy(tile, oslot, c).start()

        @pl.when(w == n_items - 1)
        def _():
            for c in range(N_PLANES):
                out_copy(0, oslot, c).wait()

            @pl.when(tile >= 1)
            def _():
                for c in range(N_PLANES):
                    out_copy(0, 1 - oslot, c).wait()


def _moe(plan, xs, w1, w3, w2, layer):
    grid_spec = pltpu.PrefetchScalarGridSpec(
        num_scalar_prefetch=len(plan),
        grid=(MOE_MAX_ITEMS,),
        in_specs=[pl.BlockSpec(memory_space=pl.ANY)] * 4,
        out_specs=pl.BlockSpec(memory_space=pl.ANY),
        scratch_shapes=[pltpu.VMEM((2, N_PLANES, MOE_TM, V7X_LANES), f32),
                        pltpu.VMEM((2, N_PLANES, MOE_TM, V7X_LANES), f32),
                        pltpu.SemaphoreType.DMA((2,)), pltpu.SemaphoreType.DMA((2,)),
                        pltpu.VMEM((2, D_MODEL, MOE_HIDDEN), f32), pltpu.VMEM((2, D_MODEL, MOE_HIDDEN), f32),
                        pltpu.VMEM((2, MOE_HIDDEN, D_MODEL), f32), pltpu.SemaphoreType.DMA((2,)),
                        pltpu.VMEM((D_MODEL, MOE_HIDDEN), bf16), pltpu.VMEM((D_MODEL, MOE_HIDDEN), bf16),
                        pltpu.VMEM((MOE_HIDDEN, D_MODEL), bf16)],
    )
    return pl.pallas_call(
        functools.partial(_moe_kernel, layer),
        grid_spec=grid_spec,
        out_shape=jax.ShapeDtypeStruct((N_PAIRS, N_PLANES, V7X_LANES), f32),
        compiler_params=_cparams(("arbitrary",)),
        name="moe",
    )(*plan, xs, w1, w3, w2)


def _combine_kernel(final, pos0_ref, pos1_ref, ys_hbm, x1_ref, route_ref, g_ref, b_ref, *rest):
    if final:
        yp_ref, ysm_ref, buf0, buf1, sem, t3_ref = rest
    else:
        o_ref, ob_ref, buf0, buf1, sem = rest
    bufs = (buf0, buf1)
    pos = (pos0_ref, pos1_ref)
    s = pl.program_id(0)
    nsteps = pl.num_programs(0)

    def issue(tile, slot):
        base = tile * CMB_TILE
        for r in range(CMB_TILE):
            for k in range(MOE_TOPK):
                pltpu.make_async_copy(ys_hbm.at[pos[k][base + r]], bufs[slot].at[k, :, r, :],
                                      sem.at[slot]).start(priority=k % 2)

    def drain(slot):
        for k in range(MOE_TOPK):
            pltpu.make_async_copy(bufs[slot].at[k], bufs[slot].at[k], sem.at[slot]).wait()

    @pl.when(s == 0)
    def _():
        issue(0, 0)

    def step(slot):
        drain(slot)
        issue(jnp.minimum(s + 1, nsteps - 1), 1 - slot)
        route = route_ref[...]
        moe = route[:, 2:3] * _from_planes(bufs[slot].at[0]) + route[:, 3:4] * _from_planes(bufs[slot].at[1])
        y = _layer_norm_rows(DN_ALPHA * _from_planes(x1_ref) + moe, g_ref[...], b_ref[...])
        if not final:
            o_ref[...] = y
            ob_ref[...] = y.astype(bf16)
        else:
            @pl.when(s < CMB_P_STEPS)
            def _():
                t3_ref[...] = y.reshape(CMB_TILE // BATCH, BATCH, D_MODEL)
                for b in range(BATCH):
                    yp_ref[b] = t3_ref[:, b, :]

            steps_per_tile = CMB_TILE // DEC_BATCH
            for q in range(N_S // CMB_TILE):
                @pl.when(s == CMB_P_STEPS + q)
                def _():
                    for h in range(steps_per_tile):
                        ysm_ref[:, q * steps_per_tile + h, :] = y[h * DEC_BATCH:(h + 1) * DEC_BATCH, :]

        @pl.when(s == nsteps - 1)
        def _():
            drain(1 - slot)

    for slot in range(2):
        @pl.when(s % 2 == slot)
        def _():
            step(slot)


def _combine(pos, ys, x1, route, g, b, final):
    if final:
        out_specs = [pl.BlockSpec((BATCH, CMB_TILE // BATCH, D_MODEL),
                                  lambda s, *_: (0, jnp.minimum(s, CMB_P_STEPS - 1), 0)),
                     pl.BlockSpec((DEC_BATCH, DEC_SEQ, D_MODEL), lambda s, *_: (0, 0, 0))]
        out_shape = [jax.ShapeDtypeStruct((BATCH, SEQ, D_MODEL), f32),
                     jax.ShapeDtypeStruct((DEC_BATCH, DEC_SEQ, D_MODEL), f32)]
        extra = [pltpu.VMEM((CMB_TILE // BATCH, BATCH, D_MODEL), f32)]
    else:
        out_specs = [pl.BlockSpec((CMB_TILE, D_MODEL), lambda s, *_: (s, 0))] * 2
        out_shape = [jax.ShapeDtypeStruct((N_TOK, D_MODEL), f32), jax.ShapeDtypeStruct((N_TOK, D_MODEL), bf16)]
        extra = []
    grid_spec = pltpu.PrefetchScalarGridSpec(
        num_scalar_prefetch=MOE_TOPK,
        grid=(N_TOK // CMB_TILE,),
        in_specs=[pl.BlockSpec(memory_space=pl.ANY),
                  pl.BlockSpec((N_PLANES, CMB_TILE, V7X_LANES), lambda s, *_: (0, s, 0)),
                  pl.BlockSpec((CMB_TILE, ROUTE_LANES), lambda s, *_: (s, 0)),
                  pl.BlockSpec((1, D_MODEL), lambda s, *_: (0, 0)),
                  pl.BlockSpec((1, D_MODEL), lambda s, *_: (0, 0))],
        out_specs=out_specs,
        scratch_shapes=[pltpu.VMEM((MOE_TOPK, N_PLANES, CMB_TILE, V7X_LANES), f32),
                        pltpu.VMEM((MOE_TOPK, N_PLANES, CMB_TILE, V7X_LANES), f32),
                        pltpu.SemaphoreType.DMA((2,))] + extra,
    )
    return pl.pallas_call(
        functools.partial(_combine_kernel, final),
        grid_spec=grid_spec,
        out_shape=out_shape,
        compiler_params=_cparams(("arbitrary",)),
        name="combine_out" if final else "combine",
    )(*pos, ys, x1, route, g, b)


def _to_rows_kernel(xp_ref, xs_ref, o_ref, ob_ref, t3_ref):
    i = pl.program_id(0)

    @pl.when(i < P_TILES)
    def _():
        for b in range(BATCH):
            t3_ref[:, b, :] = xp_ref[b]
        rows = t3_ref[...].reshape(ROW_TILE, D_MODEL)
        o_ref[...] = rows
        ob_ref[...] = rows.astype(bf16)

    @pl.when(i >= P_TILES)
    def _():
        for t in range(DEC_SEQ):
            rows = xs_ref[:, t, :]
            o_ref[t * DEC_BATCH:(t + 1) * DEC_BATCH, :] = rows
            ob_ref[t * DEC_BATCH:(t + 1) * DEC_BATCH, :] = rows.astype(bf16)


def _to_rows(x_prompt, x_sample):
    return pl.pallas_call(
        _to_rows_kernel,
        grid=(N_TILES,),
        in_specs=[pl.BlockSpec((BATCH, P_TC, D_MODEL), lambda i: (0, jnp.minimum(i, P_TILES - 1), 0)),
                  pl.BlockSpec((DEC_BATCH, DEC_SEQ, D_MODEL), lambda i: (0, 0, 0))],
        out_specs=[pl.BlockSpec((ROW_TILE, D_MODEL), lambda i: (i, 0))] * 2,
        out_shape=[jax.ShapeDtypeStruct((N_TOK, D_MODEL), f32), jax.ShapeDtypeStruct((N_TOK, D_MODEL), bf16)],
        scratch_shapes=[pltpu.VMEM((P_TC, BATCH, D_MODEL), f32)],
        compiler_params=_cparams(("arbitrary",)),
        name="to_rows",
    )(x_prompt, x_sample)


def _lookup(table, idx):
    ar = jnp.arange(MOE_EXPERTS, dtype=jnp.int32).reshape((MOE_EXPERTS,) + (1,) * idx.ndim)
    table = table.reshape(ar.shape)
    return jnp.sum(jnp.where(idx[None] == ar, table, 0), axis=0)


def _dispatch_plan(route_t, cnt):
    i32 = jnp.int32
    e = route_t[0:2].astype(i32)
    rank = route_t[4:6].astype(i32)
    counts = cnt[0, :MOE_EXPERTS].astype(i32)
    ends = jnp.cumsum(counts)
    starts = ends - counts
    pos = _lookup(starts, e) + rank

    first_tile = starts // MOE_TM
    last_tile = (ends - 1) // MOE_TM
    ntiles = jnp.where(counts > 0, last_tile - first_tile + 1, 0)
    item_end = jnp.cumsum(ntiles)
    n_items = item_end[-1]
    w = jnp.minimum(jnp.arange(MOE_MAX_ITEMS, dtype=i32), n_items - 1)
    we = jnp.sum((item_end[None, :] <= w[:, None]).astype(i32), axis=-1)
    wt = _lookup(first_tile, we) + w - _lookup(item_end - ntiles, we)
    wlo = jnp.maximum(_lookup(starts, we) - wt * MOE_TM, 0)
    whi = jnp.minimum(_lookup(ends, we) - wt * MOE_TM, MOE_TM)
    changes = (wt[1:] != wt[:-1]).astype(i32)
    wfirst = jnp.concatenate([jnp.ones((1,), i32), changes])
    wlast = jnp.concatenate([changes, jnp.ones((1,), i32)])
    wlast = jnp.where(jnp.arange(MOE_MAX_ITEMS, dtype=i32) == n_items - 1, 1, wlast)
    wefirst = jnp.concatenate([jnp.ones((1,), i32), (we[1:] != we[:-1]).astype(i32)])
    weslot = (jnp.cumsum(wefirst) - 1) % 2
    ar = jnp.arange(MOE_EXPERTS, dtype=i32)
    later = (ar[None, :] > ar[:, None]) & (counts[None, :] > 0)
    next_expert = jnp.min(jnp.where(later, ar[None, :], MOE_EXPERTS), axis=1)
    wenext = _lookup(jnp.where(next_expert < MOE_EXPERTS, next_expert, -1), we)
    return (pos[0], pos[1]), (wt, we, wlo, whi, wfirst, wlast, wefirst, weslot, wenext, n_items.reshape(1))


def _block_diag(w):
    n, a, b = w.shape
    eye = jnp.eye(n, dtype=w.dtype)
    return (w[:, :, None, :] * eye[:, None, :, None]).reshape(n * a, n * b)


def _ssm_params(a_re, a_im, log_dt, b_re, b_im, c_re, c_im):
    ar, ai = a_re, a_im
    dt = jnp.exp(log_dt)[:, None]
    mag = jnp.exp(ar * dt)
    lb_re = mag * jnp.cos(ai * dt)
    lb_im = mag * jnp.sin(ai * dt)
    den = ar * ar + ai * ai
    nr = lb_re - 1.0
    coef_re = (nr * ar + lb_im * ai) / den
    coef_im = (lb_im * ar - nr * ai) / den
    bb_re = coef_re[..., None] * b_re - coef_im[..., None] * b_im
    bb_im = coef_re[..., None] * b_im + coef_im[..., None] * b_re
    gk = SSM_GROUPS // SSM_KB

    def diag_blocks(w):
        return jnp.stack([_block_diag(w[k * gk:(k + 1) * gk]) for k in range(SSM_KB)]).astype(bf16)

    bbre = diag_blocks(bb_re.transpose(0, 2, 1))
    bbim = diag_blocks(bb_im.transpose(0, 2, 1))
    ccre = diag_blocks(c_re.transpose(0, 2, 1))
    ccim = diag_blocks(c_im.transpose(0, 2, 1))
    return (lb_re.reshape(1, SSM_LANES), lb_im.reshape(1, SSM_LANES), bbre, bbim, ccre, ccim)


def kernel(x_prompt, x_sample, state_ret, state_ssm_re, state_ssm_im, state_lru, state_conv, w_in, ret_gn_g, ret_gn_b, ssm_a_re, ssm_a_im, ssm_log_dt, ssm_b_re, ssm_b_im, ssm_c_re, ssm_c_im, ssm_d, ssm_w_glu, lru_conv_w, lru_conv_b, lru_wa, lru_ba, lru_wx, lru_bx, lru_lambda, w_branch, w_out, ln1_g, ln1_b, moe_w_group, moe_b_group, moe_w_expert, moe_b_expert, moe_w1, moe_w3, moe_w2, ln2_g, ln2_b):
    x, xb = _to_rows(x_prompt, x_sample)
    rope = _rope_tables()
    tabs_p = _ret_tables(RET_SUB_T)
    tabs_s = _ret_tables(DEC_SEQ)
    row = lambda v: v.reshape(1, -1)
    w_bf = w_in.astype(bf16)

    outs = [[] for _ in range(10)]
    for l in range(DEPTH):

        s0 = state_ret.reshape(DEPTH, DEC_BATCH, 2, 2 * RET_DK, RET_DV)
        y_ret, ret_p, ret_s = _retention(xb, w_bf, rope, s0, l, tabs_p, tabs_s,
                                         row(ret_gn_g[l]), row(ret_gn_b[l]))

        sp = _ssm_params(ssm_a_re[l], ssm_a_im[l], ssm_log_dt[l], ssm_b_re[l], ssm_b_im[l],
                         ssm_c_re[l], ssm_c_im[l])
        y_ssm, re_p, im_p, re_s, im_s = _ssm(
            xb, w_bf, l, state_ssm_re[l].reshape(DEC_BATCH, SSM_LANES), state_ssm_im[l].reshape(DEC_BATCH, SSM_LANES),
            *sp, row(ssm_d[l]), ssm_w_glu[l].astype(bf16))

        conv0 = state_conv[l].transpose(1, 0, 2).reshape(LRU_HIST_S, BRANCH_W)
        y_lru, lru_p, conv_p, lru_s, conv_s = _lru(
            xb, w_bf, l, state_lru[l], conv0, lru_conv_w[l], row(lru_conv_b[l]),
            _block_diag(lru_wa[l]).astype(bf16), row(lru_ba[l]),
            _block_diag(lru_wx[l]).astype(bf16), row(lru_bx[l]), row(lru_lambda[l]))

        wr = jnp.zeros((D_MODEL, ROUTE_LANES), f32)
        wr = wr.at[:, 0:MOE_GROUPS].set(moe_w_group[l]).at[:, MOE_GROUPS:MOE_GROUPS + MOE_EXPERTS].set(moe_w_expert[l])
        br = jnp.zeros((1, ROUTE_LANES), f32)
        br = br.at[0, 0:MOE_GROUPS].set(moe_b_group[l]).at[0, MOE_GROUPS:MOE_GROUPS + MOE_EXPERTS].set(moe_b_expert[l])
        x1, route, route_t, cnt = _merge(y_ret, y_ssm, y_lru, xb, w_bf, l, x, w_branch[l].astype(bf16), w_out[l].astype(bf16),
                                row(ln1_g[l]), row(ln1_b[l]), wr.astype(bf16), br)

        pos, plan = _dispatch_plan(route_t, cnt)
        xs = _dispatch(pos, x1)
        ys = _moe(plan, xs, moe_w1, moe_w3, moe_w2, l)
        x, xb = _combine(pos, ys, x1, route, row(ln2_g[l]), row(ln2_b[l]), final=(l == DEPTH - 1))

        outs[0].append(ret_p.reshape(BATCH, RET_HEADS, RET_DK, RET_DV))
        outs[1].append(re_p.reshape(BATCH, SSM_GROUPS, SSM_STATE))
        outs[2].append(im_p.reshape(BATCH, SSM_GROUPS, SSM_STATE))
        outs[3].append(lru_p)
        outs[4].append(conv_p.reshape(CONV_W - 1, BATCH, BRANCH_W).transpose(1, 0, 2))
        outs[5].append(ret_s.reshape(DEC_BATCH, RET_HEADS, RET_DK, RET_DV))
        outs[6].append(re_s.reshape(DEC_BATCH, SSM_GROUPS, SSM_STATE))
        outs[7].append(im_s.reshape(DEC_BATCH, SSM_GROUPS, SSM_STATE))
        outs[8].append(lru_s)
        outs[9].append(conv_s.reshape(CONV_W - 1, DEC_BATCH, BRANCH_W).transpose(1, 0, 2))

    y_prompt, y_sample = x, xb
    return (y_prompt, y_sample) + tuple(jnp.stack(o) for o in outs)
```

```python
import functools

import jax
import jax.numpy as jnp
import numpy as np
from jax import lax
from jax.experimental import pallas as pl
from jax.experimental.pallas import tpu as pltpu

f32 = jnp.float32
bf16 = jnp.bfloat16

D_MODEL = 1024
BATCH = 8
SEQ = 2048
DEPTH = 2
DEC_BATCH = 128
DEC_SEQ = 8
PAST_LEN = 16384
BRANCH_W = 512
N_BRANCH = 3
RET_HEADS = 4
RET_DK = 64
RET_DV = 128
ROPE_BASE = 10000.0
SSM_GROUP = 16
SSM_GROUPS = 32
SSM_STATE = 64
SSM_LANES = SSM_GROUPS * SSM_STATE
LRU_BLOCKS = 8
LRU_BW = 64
CONV_W = 4
LRU_C = 8.0
MOE_GROUPS = 4
MOE_PER_GROUP = 8
MOE_EXPERTS = 32
MOE_TOPK = 2
MOE_HIDDEN = 512
DN_ALPHA = (2.0 * DEPTH) ** 0.25
LN_EPS = 1e-5
D_IN = 6144

V7X_SUBLANES = 8
V7X_LANES = 128
V7X_VMEM_LIMIT = 56 * 1024 * 1024

N_P = BATCH * SEQ
N_S = DEC_BATCH * DEC_SEQ
N_TOK = N_P + N_S
ROW_TILE = 1024
P_TILES = N_P // ROW_TILE
N_TILES = N_TOK // ROW_TILE
P_TC = ROW_TILE // BATCH
RET_SUB_T = 32
RET_SUB_R = RET_SUB_T * BATCH
RET_SUBS = ROW_TILE // RET_SUB_R
S_BLOCKS = DEC_BATCH // BATCH
S_BLOCK_R = DEC_SEQ * BATCH
MERGE_TILE = 512
MOE_TM = 512
MOE_SUB = 128
N_PAIRS = N_TOK * MOE_TOPK
MOE_MAX_ITEMS = N_PAIRS // MOE_TM + MOE_EXPERTS - 1
DSP_TILE = 1024
CMB_TILE = 256
CMB_P_STEPS = N_P // CMB_TILE
ROUTE_LANES = 128


PROJ_RC = 256
RET_COLS, RET_BLK = 1536, 0
SSM_COLS, SSM_BLK = 512, 3
LRU_COLS, LRU_BLK = 1024, 2
GATE_COLS, GATE_BLK = 3072, 1


def _cparams(sem):
    return pltpu.CompilerParams(dimension_semantics=sem, vmem_limit_bytes=V7X_VMEM_LIMIT)


N_PLANES = D_MODEL // V7X_LANES


def _to_planes(ref, rows):
    for c in range(N_PLANES):
        ref[c] = rows[:, c * V7X_LANES:(c + 1) * V7X_LANES]


def _from_planes(ref):
    return jnp.concatenate([ref[c] for c in range(N_PLANES)], axis=1)


def _const_spec(block_shape, index):
    return pl.BlockSpec(block_shape, lambda *_: index, pipeline_mode=pl.Buffered(1))


def _dot(a, b):
    return jnp.dot(a, b, preferred_element_type=f32)


def _dot_nt(a, b):
    return lax.dot_general(a, b, (((1,), (1,)), ((), ())), preferred_element_type=f32)


def _dot_tn(a, b):
    return lax.dot_general(a, b, (((0,), (0,)), ((), ())), preferred_element_type=f32)


def _layer_norm_rows(x, g, b):
    mu = jnp.mean(x, -1, keepdims=True)
    xc = x - mu
    var = jnp.mean(xc * xc, -1, keepdims=True)
    return xc * lax.rsqrt(var + LN_EPS) * g + b


def _project_rows(xb_ref, w_ref, z_ref):
    for rc in range(ROW_TILE // PROJ_RC):
        rs = slice(rc * PROJ_RC, (rc + 1) * PROJ_RC)
        z_ref[rs, :] = _dot(xb_ref[rs, :], w_ref[0])


def _ret_block(q, k, v, g, cosb, sinb, mask_ref, qdec_ref, kdec_ref, cdec_ref, scat_ref, gng, gnb):
    rows = q.shape[0]
    lane_qk = lax.broadcasted_iota(jnp.int32, (rows, 2 * V7X_LANES), 1)
    first_half = (lane_qk & (RET_DK - 1)) < (RET_DK // 2)

    def rope(x):
        partner = jnp.where(first_half, pltpu.roll(x, 2 * V7X_LANES - RET_DK // 2, 1),
                            pltpu.roll(x, RET_DK // 2, 1))
        return x * cosb + partner * sinb

    q = rope(q)
    k = rope(k) * (RET_DK ** -0.5)
    kd = k * kdec_ref[...]
    lane = lax.broadcasted_iota(jnp.int32, (rows, V7X_LANES), 1)
    row_b = lax.broadcasted_iota(jnp.int32, (rows, V7X_LANES), 0) & (BATCH - 1)
    outs = []
    for p in range(2):
        qp = q[:, p * V7X_LANES:(p + 1) * V7X_LANES]
        kp = k[:, p * V7X_LANES:(p + 1) * V7X_LANES].astype(bf16)
        kdp = kd[:, p * V7X_LANES:(p + 1) * V7X_LANES]
        s_old = scat_ref[p]
        s_bf = s_old.astype(bf16)
        s_new = s_old * jnp.concatenate([cdec_ref[p]] * BATCH, axis=1)
        for hh in range(2):
            h = 2 * p + hh
            head_lanes = (lane >= RET_DK) if hh else (lane < RET_DK)
            qh = jnp.where(head_lanes, qp, 0.0).astype(bf16)
            kdh = jnp.where(head_lanes, kdp, 0.0).astype(bf16)
            vh = v[:, h * RET_DV:(h + 1) * RET_DV]
            vh_bf = vh.astype(bf16)
            sc = _dot_nt(qh, kp) * mask_ref[h]
            o = _dot(sc.astype(bf16), vh_bf)
            cross = _dot(qh, s_bf)
            oc = jnp.zeros((rows, RET_DV), f32)
            for b in range(BATCH):
                oc = oc + jnp.where(row_b == b, cross[:, b * RET_DV:(b + 1) * RET_DV], 0.0)
            o = o + oc * qdec_ref[h]
            vcat = jnp.concatenate([jnp.where(row_b == b, vh_bf, jnp.zeros_like(vh_bf))
                                    for b in range(BATCH)], axis=1)
            s_new = s_new + _dot_tn(kdh, vcat)
            mu = jnp.mean(o, -1, keepdims=True)
            oc2 = o - mu
            var = jnp.mean(oc2 * oc2, -1, keepdims=True)
            outs.append(oc2 * lax.rsqrt(var + LN_EPS))
        scat_ref[p] = s_new
    o = jnp.concatenate(outs, axis=1) * gng + gnb
    return jax.nn.silu(g) * o


def _per_step_rows(tab_ref, t0, steps):
    return jnp.concatenate(
        [jnp.broadcast_to(tab_ref[t0 + t:t0 + t + 1, :], (BATCH, tab_ref.shape[1])) for t in range(steps)], axis=0)


def _ret_kernel(xb_ref, w_ref, cos_ref, sin_ref, cos_s_ref, sin_s_ref, s0_ref,
                mask_p_ref, qdec_p_ref, kdec_p_ref, cdec_p_ref,
                mask_s_ref, qdec_s_ref, kdec_s_ref, cdec_s_ref,
                gng_ref, gnb_ref,
                y_ref, retp_ref, rets_ref, scat_ref, z_ref):
    i = pl.program_id(0)
    gng = gng_ref[...]
    gnb = gnb_ref[...]
    q_cols, k_cols = slice(0, 256), slice(256, 512)
    v_cols, g_cols = slice(512, 1024), slice(1024, 1536)

    @pl.when(i == 0)
    def _():
        scat_ref[...] = jnp.zeros_like(scat_ref)

    @pl.when(i <= P_TILES)
    def _():
        _project_rows(xb_ref, w_ref, z_ref)

    @pl.when(i < P_TILES)
    def _():
        for sc in range(RET_SUBS):
            r0 = sc * RET_SUB_R
            rs = slice(r0, r0 + RET_SUB_R)
            y_ref[rs, :] = _ret_block(
                z_ref[rs, q_cols], z_ref[rs, k_cols], z_ref[rs, v_cols], z_ref[rs, g_cols],
                _per_step_rows(cos_ref, sc * RET_SUB_T, RET_SUB_T), _per_step_rows(sin_ref, sc * RET_SUB_T, RET_SUB_T),
                mask_p_ref, qdec_p_ref, kdec_p_ref, cdec_p_ref, scat_ref, gng, gnb)

    @pl.when(i == P_TILES - 1)
    def _():
        for b in range(BATCH):
            for p in range(2):
                retp_ref[b, p] = scat_ref[p, :, b * RET_DV:(b + 1) * RET_DV]

    @pl.when(i >= P_TILES)
    def _():
        bb = i - P_TILES
        for b in range(BATCH):
            for p in range(2):
                scat_ref[p, :, b * RET_DV:(b + 1) * RET_DV] = s0_ref[0, b, p]

        def rows_of(cols):
            return jnp.concatenate(
                [z_ref[pl.ds(pl.multiple_of(t * DEC_BATCH + bb * BATCH, BATCH), BATCH), cols]
                 for t in range(DEC_SEQ)], axis=0)

        y = _ret_block(
            rows_of(q_cols), rows_of(k_cols), rows_of(v_cols),
            rows_of(g_cols), _per_step_rows(cos_s_ref, 0, DEC_SEQ), _per_step_rows(sin_s_ref, 0, DEC_SEQ),
            mask_s_ref, qdec_s_ref, kdec_s_ref, cdec_s_ref, scat_ref, gng, gnb)
        for t in range(DEC_SEQ):
            y_ref[pl.ds(pl.multiple_of(t * DEC_BATCH + bb * BATCH, BATCH), BATCH), :] = (
                y[t * BATCH:(t + 1) * BATCH, :])
        for b in range(BATCH):
            for p in range(2):
                rets_ref[b, p] = scat_ref[p, :, b * RET_DV:(b + 1) * RET_DV]


def _ret_tables(tc):
    rows = tc * BATCH
    nf = np.float32
    log_g = np.log1p(-np.exp2(nf(-5.0) - np.arange(RET_HEADS, dtype=nf))).astype(nf)
    t_idx = (np.arange(rows) // BATCH).astype(nf)
    b_idx = np.arange(rows) % BATCH
    rel = t_idx[:, None] - t_idx[None, :]
    same = b_idx[:, None] == b_idx[None, :]
    decay = np.exp(log_g[:, None, None] * np.maximum(rel, nf(0.0)))
    mask = np.where((rel >= 0) & same, decay, nf(0.0))
    qdec = np.exp(log_g[:, None] * (t_idx[None, :] + nf(1.0)))
    qdec = np.broadcast_to(qdec[:, :, None], (RET_HEADS, rows, RET_DV))
    kdec = np.exp(log_g[:, None] * (nf(tc - 1.0) - t_idx[None, :]))
    kdec = np.broadcast_to(kdec.T[:, :, None], (rows, RET_HEADS, RET_DK)).reshape(rows, RET_HEADS * RET_DK)
    cdec = np.exp(log_g * nf(tc))
    cdec = np.broadcast_to(cdec[:, None, None], (RET_HEADS, RET_DK, RET_DV)).reshape(2, 2 * RET_DK, RET_DV)
    return tuple(jnp.asarray(np.ascontiguousarray(a), dtype=f32) for a in (mask, qdec, kdec, cdec))


def _rope_tables():
    half = RET_DK // 2
    inv = ROPE_BASE ** (-jnp.arange(half, dtype=f32) / half)
    pos_p = jnp.arange(SEQ, dtype=f32)
    pos_s = PAST_LEN + jnp.arange(DEC_SEQ, dtype=f32)

    def tab(pos):
        ang = pos[:, None] * inv[None, :]
        cos = jnp.cos(ang)
        sin = jnp.sin(ang)
        cos_h = jnp.concatenate([cos, cos], axis=1)
        sin_h = jnp.concatenate([-sin, sin], axis=1)
        return jnp.tile(cos_h, (1, RET_HEADS)), jnp.tile(sin_h, (1, RET_HEADS))

    return tab(pos_p) + tab(pos_s)


def _retention(xb, w_bf, rope, s0_s, layer, tabs_p, tabs_s, gng, gnb):
    n_steps = P_TILES + S_BLOCKS
    tile = lambda i: jnp.minimum(i, P_TILES)
    sblk = lambda i: jnp.maximum(i - P_TILES, 0)
    full = lambda a: pl.BlockSpec(a.shape, lambda i, _n=a.ndim: (0,) * _n)
    state_blk = (BATCH, 2, 2 * RET_DK, RET_DV)
    cos_p, sin_p, cos_s, sin_s = rope
    ptile = lambda i: jnp.minimum(i, P_TILES - 1)
    ins = [xb, w_bf, cos_p, sin_p, cos_s, sin_s, s0_s, *tabs_p, *tabs_s, gng, gnb]
    in_specs = [
        pl.BlockSpec((ROW_TILE, D_MODEL), lambda i: (tile(i), 0)),
        _const_spec((1, D_MODEL, RET_COLS), (layer, 0, RET_BLK)),
        pl.BlockSpec((P_TC, 256), lambda i: (ptile(i), 0)),
        pl.BlockSpec((P_TC, 256), lambda i: (ptile(i), 0)),
        full(cos_s), full(sin_s),
        pl.BlockSpec((1,) + state_blk, lambda i: (layer, sblk(i), 0, 0, 0)),
    ] + [full(a) for a in (*tabs_p, *tabs_s, gng, gnb)]
    return pl.pallas_call(
        _ret_kernel,
        grid=(n_steps,),
        in_specs=in_specs,
        out_specs=[pl.BlockSpec((ROW_TILE, BRANCH_W), lambda i: (tile(i), 0)),
                   pl.BlockSpec(state_blk, lambda i: (0, 0, 0, 0)),
                   pl.BlockSpec(state_blk, lambda i: (sblk(i), 0, 0, 0))],
        out_shape=[jax.ShapeDtypeStruct((N_TOK, BRANCH_W), f32),
                   jax.ShapeDtypeStruct((BATCH, 2, 2 * RET_DK, RET_DV), f32),
                   jax.ShapeDtypeStruct((DEC_BATCH, 2, 2 * RET_DK, RET_DV), f32)],
        scratch_shapes=[pltpu.VMEM((2, 2 * RET_DK, BATCH * RET_DV), f32),
                        pltpu.VMEM((ROW_TILE, RET_COLS), f32)],
        compiler_params=_cparams(("arbitrary",)),
        name="retention",
    )(*ins)


SSM_LB = 512
SSM_RC = 256
SSM_KB = 2
SSM_KB_U = BRANCH_W // SSM_KB
SSM_KB_H = SSM_LANES // SSM_KB


def _ssm_scan(bre_ref, bim_ref, lre_ref, lim_ref, h_re0, h_im0, row0, nb_rows, steps, lb, unroll):
    ls = slice(lb * SSM_LB, (lb + 1) * SSM_LB)
    a_re = jnp.broadcast_to(lre_ref[:, ls], (V7X_SUBLANES, SSM_LB))
    a_im = jnp.broadcast_to(lim_ref[:, ls], (V7X_SUBLANES, SSM_LB))

    def step(t, carry):
        h_re, h_im = carry
        r = pl.multiple_of(row0 + t * nb_rows, V7X_SUBLANES)
        n_re = a_re * h_re - a_im * h_im + bre_ref[pl.ds(r, V7X_SUBLANES), ls]
        n_im = a_re * h_im + a_im * h_re + bim_ref[pl.ds(r, V7X_SUBLANES), ls]
        bre_ref[pl.ds(r, V7X_SUBLANES), ls] = n_re
        bim_ref[pl.ds(r, V7X_SUBLANES), ls] = n_im
        return n_re, n_im

    return lax.fori_loop(0, steps, step, (h_re0, h_im0), unroll=unroll)


def _ssm_kernel(xb_ref, w_ref, h0re_ref, h0im_ref, lre_ref, lim_ref, bbre_ref, bbim_ref, ccre_ref, ccim_ref,
                d_ref, wglu_ref,
                y_ref, pre_ref, pim_ref, sre_ref, sim_ref,
                bre_ref, bim_ref, hre_ref, him_ref, zs_ref):
    i = pl.program_id(0)
    _project_rows(xb_ref, w_ref, zs_ref)
    for rc in range(ROW_TILE // SSM_RC):
        rs = slice(rc * SSM_RC, (rc + 1) * SSM_RC)
        ub = zs_ref[rs, :].astype(bf16)
        for k in range(SSM_KB):
            uk = ub[:, k * SSM_KB_U:(k + 1) * SSM_KB_U]
            hs = slice(k * SSM_KB_H, (k + 1) * SSM_KB_H)
            bre_ref[rs, hs] = _dot(uk, bbre_ref[k])
            bim_ref[rs, hs] = _dot(uk, bbim_ref[k])

    @pl.when(i == 0)
    def _():
        hre_ref[...] = jnp.zeros_like(hre_ref)
        him_ref[...] = jnp.zeros_like(him_ref)

    @pl.when(i < P_TILES)
    def _():
        for lb in range(SSM_LANES // SSM_LB):
            ls = slice(lb * SSM_LB, (lb + 1) * SSM_LB)
            h_re, h_im = _ssm_scan(bre_ref, bim_ref, lre_ref, lim_ref, hre_ref[:, ls], him_ref[:, ls],
                                   0, BATCH, P_TC, lb, 8)
            hre_ref[:, ls] = h_re
            him_ref[:, ls] = h_im
        pre_ref[...] = hre_ref[...]
        pim_ref[...] = him_ref[...]

    @pl.when(i >= P_TILES)
    def _():
        def per_row_tile(rt, c):
            r0 = pl.multiple_of(rt * V7X_SUBLANES, V7X_SUBLANES)
            for lb in range(SSM_LANES // SSM_LB):
                ls = slice(lb * SSM_LB, (lb + 1) * SSM_LB)
                _ssm_scan(bre_ref, bim_ref, lre_ref, lim_ref,
                          h0re_ref[pl.ds(r0, V7X_SUBLANES), ls], h0im_ref[pl.ds(r0, V7X_SUBLANES), ls],
                          r0, DEC_BATCH, DEC_SEQ, lb, True)
            return c

        lax.fori_loop(0, DEC_BATCH // V7X_SUBLANES, per_row_tile, 0)
        last = (DEC_SEQ - 1) * DEC_BATCH
        sre_ref[...] = bre_ref[last:last + DEC_BATCH, :]
        sim_ref[...] = bim_ref[last:last + DEC_BATCH, :]

    for rc in range(ROW_TILE // SSM_RC):
        rs = slice(rc * SSM_RC, (rc + 1) * SSM_RC)
        ch = []
        for k in range(SSM_KB):
            hs = slice(k * SSM_KB_H, (k + 1) * SSM_KB_H)
            ch.append(_dot(bre_ref[rs, hs].astype(bf16), ccre_ref[k]) - _dot(bim_ref[rs, hs].astype(bf16), ccim_ref[k]))
        y = jnp.concatenate(ch, axis=1) + d_ref[...] * zs_ref[rs, :]
        zz = jax.nn.gelu(y)
        y_ref[rs, :] = zz * jax.nn.sigmoid(_dot(zz.astype(bf16), wglu_ref[...]))


def _ssm(xb, w_bf, layer, h0re, h0im, lre, lim, bbre, bbim, ccre, ccim, dvec, wglu):
    full = lambda a: _const_spec(a.shape, (0,) * a.ndim)
    consts = (h0re, h0im, lre, lim, bbre, bbim, ccre, ccim, dvec, wglu)
    return pl.pallas_call(
        _ssm_kernel,
        grid=(N_TILES,),
        in_specs=[pl.BlockSpec((ROW_TILE, D_MODEL), lambda i: (i, 0)),
                  _const_spec((1, D_MODEL, SSM_COLS), (layer, 0, SSM_BLK))] + [full(a) for a in consts],
        out_specs=[pl.BlockSpec((ROW_TILE, BRANCH_W), lambda i: (i, 0)),
                   pl.BlockSpec((BATCH, SSM_LANES), lambda i: (0, 0)),
                   pl.BlockSpec((BATCH, SSM_LANES), lambda i: (0, 0)),
                   pl.BlockSpec((DEC_BATCH, SSM_LANES), lambda i: (0, 0)),
                   pl.BlockSpec((DEC_BATCH, SSM_LANES), lambda i: (0, 0))],
        out_shape=[jax.ShapeDtypeStruct((N_TOK, BRANCH_W), f32),
                   jax.ShapeDtypeStruct((BATCH, SSM_LANES), f32),
                   jax.ShapeDtypeStruct((BATCH, SSM_LANES), f32),
                   jax.ShapeDtypeStruct((DEC_BATCH, SSM_LANES), f32),
                   jax.ShapeDtypeStruct((DEC_BATCH, SSM_LANES), f32)],
        scratch_shapes=[pltpu.VMEM((ROW_TILE, SSM_LANES), f32), pltpu.VMEM((ROW_TILE, SSM_LANES), f32),
                        pltpu.VMEM((BATCH, SSM_LANES), f32), pltpu.VMEM((BATCH, SSM_LANES), f32),
                        pltpu.VMEM((ROW_TILE, SSM_COLS), f32)],
        compiler_params=_cparams(("arbitrary",)),
        name="ssm",
    )(xb, w_bf, *consts)


LRU_HIST_P = (CONV_W - 1) * BATCH
LRU_HIST_S = (CONV_W - 1) * DEC_BATCH


def _lru_gates(xe_ref, nb_rows, cw_ref, cb_ref, wa_ref, ba_ref, wx_ref, bx_ref, lam_ref):
    xc = cb_ref[...] + xe_ref[0:ROW_TILE, :] * cw_ref[0:1, :]
    for j in range(1, CONV_W):
        xc = xc + xe_ref[j * nb_rows:j * nb_rows + ROW_TILE, :] * cw_ref[j:j + 1, :]
    xcb = xc.astype(bf16)
    r = jax.nn.sigmoid(_dot(xcb, wa_ref[...]) + ba_ref[...])
    ig = jax.nn.sigmoid(_dot(xcb, wx_ref[...]) + bx_ref[...])
    log_a = -LRU_C * r * jax.nn.softplus(-lam_ref[...])
    a = jnp.exp(log_a)
    b = jnp.sqrt(-jnp.tanh(log_a) * (a * a + 1.0)) * (ig * xc)
    return a, b


def _lru_kernel(xb_ref, w_ref, h0_ref, conv0_ref, cw_ref, cb_ref, wa_ref, ba_ref, wx_ref, bx_ref, lam_ref,
                y_ref, hp_ref, convp_ref, hs_ref, convs_ref,
                xe_ref, a_ref, b_ref, hc_ref, z_ref):
    i = pl.program_id(0)
    params = (cw_ref, cb_ref, wa_ref, ba_ref, wx_ref, bx_ref, lam_ref)
    _project_rows(xb_ref, w_ref, z_ref)
    zx_ref = z_ref.at[:, 0:BRANCH_W]
    zg_ref = z_ref.at[:, BRANCH_W:2 * BRANCH_W]

    @pl.when(i == 0)
    def _():
        xe_ref[0:LRU_HIST_P, :] = jnp.zeros((LRU_HIST_P, BRANCH_W), f32)
        hc_ref[...] = jnp.zeros_like(hc_ref)

    @pl.when(i < P_TILES)
    def _():
        xe_ref[LRU_HIST_P:LRU_HIST_P + ROW_TILE, :] = zx_ref[...]
        a, b = _lru_gates(xe_ref, BATCH, *params)
        a_ref[...] = a
        b_ref[...] = b
        hist = xe_ref[ROW_TILE:ROW_TILE + LRU_HIST_P, :]
        xe_ref[0:LRU_HIST_P, :] = hist
        convp_ref[...] = hist

        def step(t, h):
            r = pl.multiple_of(t * BATCH, BATCH)
            h = a_ref[pl.ds(r, BATCH), :] * h + b_ref[pl.ds(r, BATCH), :]
            b_ref[pl.ds(r, BATCH), :] = h
            return h

        h = lax.fori_loop(0, P_TC, step, hc_ref[...], unroll=8)
        hc_ref[...] = h
        hp_ref[...] = h

    @pl.when(i >= P_TILES)
    def _():
        xe_ref[0:LRU_HIST_S, :] = conv0_ref[...]
        xe_ref[LRU_HIST_S:LRU_HIST_S + ROW_TILE, :] = zx_ref[...]
        a, b = _lru_gates(xe_ref, DEC_BATCH, *params)
        a_ref[...] = a
        b_ref[...] = b
        convs_ref[...] = xe_ref[ROW_TILE:ROW_TILE + LRU_HIST_S, :]

        def per_row_tile(rt, c):
            r0 = pl.multiple_of(rt * V7X_SUBLANES, V7X_SUBLANES)
            h = h0_ref[pl.ds(r0, V7X_SUBLANES), :]
            for t in range(DEC_SEQ):
                r = pl.multiple_of(t * DEC_BATCH + r0, V7X_SUBLANES)
                h = a_ref[pl.ds(r, V7X_SUBLANES), :] * h + b_ref[pl.ds(r, V7X_SUBLANES), :]
                b_ref[pl.ds(r, V7X_SUBLANES), :] = h
            return c

        lax.fori_loop(0, DEC_BATCH // V7X_SUBLANES, per_row_tile, 0)
        last = (DEC_SEQ - 1) * DEC_BATCH
        hs_ref[...] = b_ref[last:last + DEC_BATCH, :]

    y_ref[...] = b_ref[...] * jax.nn.gelu(zg_ref[...])


def _lru(xb, w_bf, layer, h0, conv0, cw, cb, wa, ba, wx, bx, lam):
    full = lambda a: _const_spec(a.shape, (0,) * a.ndim)
    consts = (h0, conv0, cw, cb, wa, ba, wx, bx, lam)
    return pl.pallas_call(
        _lru_kernel,
        grid=(N_TILES,),
        in_specs=[pl.BlockSpec((ROW_TILE, D_MODEL), lambda i: (i, 0)),
                  _const_spec((1, D_MODEL, LRU_COLS), (layer, 0, LRU_BLK))] + [full(a) for a in consts],
        out_specs=[pl.BlockSpec((ROW_TILE, BRANCH_W), lambda i: (i, 0)),
                   pl.BlockSpec((BATCH, BRANCH_W), lambda i: (0, 0)),
                   pl.BlockSpec((LRU_HIST_P, BRANCH_W), lambda i: (0, 0)),
                   pl.BlockSpec((DEC_BATCH, BRANCH_W), lambda i: (0, 0)),
                   pl.BlockSpec((LRU_HIST_S, BRANCH_W), lambda i: (0, 0))],
        out_shape=[jax.ShapeDtypeStruct((N_TOK, BRANCH_W), f32),
                   jax.ShapeDtypeStruct((BATCH, BRANCH_W), f32),
                   jax.ShapeDtypeStruct((LRU_HIST_P, BRANCH_W), f32),
                   jax.ShapeDtypeStruct((DEC_BATCH, BRANCH_W), f32),
                   jax.ShapeDtypeStruct((LRU_HIST_S, BRANCH_W), f32)],
        scratch_shapes=[pltpu.VMEM((ROW_TILE + LRU_HIST_S, BRANCH_W), f32),
                        pltpu.VMEM((ROW_TILE, BRANCH_W), f32), pltpu.VMEM((ROW_TILE, BRANCH_W), f32),
                        pltpu.VMEM((BATCH, BRANCH_W), f32),
                        pltpu.VMEM((ROW_TILE, LRU_COLS), f32)],
        compiler_params=_cparams(("arbitrary",)),
        name="lru",
    )(xb, w_bf, *consts)


def _merge_kernel(yr_ref, ys_ref, yl_ref, xb_ref, wg_ref, x_ref, wb_ref, wo_ref, g_ref, b_ref, wr_ref, br_ref,
                  x1_ref, route_ref, route_t_ref, cnt_out_ref, cnt_ref):
    merged = jnp.zeros((MERGE_TILE, D_MODEL), f32)
    xb = xb_ref[...]
    for n, y_ref in enumerate((yr_ref, ys_ref, yl_ref)):
        proj = _dot(y_ref[...].astype(bf16), wb_ref[n])
        gate = jax.nn.sigmoid(_dot(xb, wg_ref[0, :, n * D_MODEL:(n + 1) * D_MODEL]))
        merged = merged + gate * proj
    mix = _dot(merged.astype(bf16), wo_ref[...])
    x1 = _layer_norm_rows(DN_ALPHA * x_ref[...] + mix, g_ref[...], b_ref[...])
    _to_planes(x1_ref, x1)

    logits = _dot(x1.astype(bf16), wr_ref[...]) + br_ref[...]
    lane = lax.broadcasted_iota(jnp.int32, (MERGE_TILE, ROUTE_LANES), 1).astype(f32)
    big = jnp.float32(ROUTE_LANES)
    neg = jnp.float32(-jnp.inf)
    is_g = lane < MOE_GROUPS
    lg = jnp.where(is_g, logits, neg)
    mg = jnp.max(lg, -1, keepdims=True)
    gsel = jnp.min(jnp.where(lg == mg, lane, big), -1, keepdims=True)
    sum_g = jnp.sum(jnp.where(is_g, jnp.exp(lg - mg), 0.0), -1, keepdims=True)
    pg_sel = 1.0 / sum_g
    lo = MOE_GROUPS + gsel * MOE_PER_GROUP
    is_e = jnp.abs(lane - lo - 0.5 * (MOE_PER_GROUP - 1)) < 0.5 * MOE_PER_GROUP
    le = jnp.where(is_e, logits, neg)
    me = jnp.max(le, -1, keepdims=True)
    ex = jnp.where(is_e, jnp.exp(le - me), 0.0)
    pe = jnp.where(is_e, ex / jnp.sum(ex, -1, keepdims=True), -1.0)
    v1 = jnp.max(pe, -1, keepdims=True)
    i1 = jnp.min(jnp.where(pe == v1, lane, big), -1, keepdims=True)
    pe2 = jnp.where(lane == i1, -1.0, pe)
    v2 = jnp.max(pe2, -1, keepdims=True)
    i2 = jnp.min(jnp.where(pe2 == v2, lane, big), -1, keepdims=True)
    vsum = v1 + v2
    w1 = pg_sel * v1 / vsum
    w2 = pg_sel * v2 / vsum
    e1 = i1 - MOE_GROUPS
    e2 = i2 - MOE_GROUPS

    @pl.when(pl.program_id(0) == 0)
    def _():
        cnt_ref[...] = jnp.zeros_like(cnt_ref)

    oh1 = lane == e1
    oh2 = lane == e2
    ohs = jnp.where(oh1, 1.0, jnp.where(oh2, 1.0, 0.0))
    r_i = lax.broadcasted_iota(jnp.int32, (MERGE_TILE, MERGE_TILE), 0)
    c_i = lax.broadcasted_iota(jnp.int32, (MERGE_TILE, MERGE_TILE), 1)
    strict_lower = jnp.where(c_i < r_i, 1.0, 0.0).astype(bf16)
    before = _dot(strict_lower, ohs.astype(bf16)) + cnt_ref[0:1, :]
    rank1 = jnp.sum(jnp.where(oh1, before, 0.0), -1, keepdims=True)
    rank2 = jnp.sum(jnp.where(oh2, before, 0.0), -1, keepdims=True)
    cnt_ref[0:1, :] = cnt_ref[0:1, :] + jnp.sum(ohs, 0, keepdims=True)
    cnt_out_ref[...] = cnt_ref[...]

    route = jnp.zeros((MERGE_TILE, ROUTE_LANES), f32)
    for k, val in enumerate((e1, e2, w1, w2, rank1, rank2)):
        route = jnp.where(lane == k, val, route)
    route_ref[...] = route
    route_t_ref[...] = route.T[0:V7X_SUBLANES, :]


def _merge(y_ret, y_ssm, y_lru, xb, w_bf, layer, x, wb, wo, g, b, wr, br):
    full = lambda a: _const_spec(a.shape, (0,) * a.ndim)
    row = lambda w: pl.BlockSpec((MERGE_TILE, w), lambda i: (i, 0))
    consts = (wb, wo, g, b, wr, br)
    return pl.pallas_call(
        _merge_kernel,
        grid=(N_TOK // MERGE_TILE,),
        in_specs=[row(BRANCH_W), row(BRANCH_W), row(BRANCH_W), row(D_MODEL),
                  _const_spec((1, D_MODEL, GATE_COLS), (layer, 0, GATE_BLK)),
                  row(D_MODEL)] + [full(a) for a in consts],
        out_specs=[pl.BlockSpec((N_PLANES, MERGE_TILE, V7X_LANES), lambda i: (0, i, 0)), row(ROUTE_LANES),
                   pl.BlockSpec((V7X_SUBLANES, MERGE_TILE), lambda i: (0, i)),
                   pl.BlockSpec((V7X_SUBLANES, ROUTE_LANES), lambda i: (0, 0))],
        out_shape=[jax.ShapeDtypeStruct((N_PLANES, N_TOK, V7X_LANES), f32),
                   jax.ShapeDtypeStruct((N_TOK, ROUTE_LANES), f32),
                   jax.ShapeDtypeStruct((V7X_SUBLANES, N_TOK), f32),
                   jax.ShapeDtypeStruct((V7X_SUBLANES, ROUTE_LANES), f32)],
        scratch_shapes=[pltpu.VMEM((V7X_SUBLANES, ROUTE_LANES), f32)],
        compiler_params=_cparams(("arbitrary",)),
        name="merge",
    )(y_ret, y_ssm, y_lru, xb, w_bf, x, *consts)


def _dispatch_kernel(pos0_ref, pos1_ref, x1_ref, xs_hbm, sem):
    base = pl.program_id(0) * DSP_TILE

    def row_copy(r, dst_row):
        return pltpu.make_async_copy(x1_ref.at[:, r, :], xs_hbm.at[dst_row], sem.at[0])

    def issue(r, c):
        n = base + r
        for k in range(MOE_TOPK):
            row_copy(r, (pos0_ref, pos1_ref)[k][n]).start(priority=k % 2)
        return c

    lax.fori_loop(0, DSP_TILE, issue, 0, unroll=8)
    for _ in range(MOE_TOPK):
        pltpu.make_async_copy(x1_ref, x1_ref, sem.at[0]).wait()


def _dispatch(pos, x1p):
    grid_spec = pltpu.PrefetchScalarGridSpec(
        num_scalar_prefetch=MOE_TOPK,
        grid=(N_TOK // DSP_TILE,),
        in_specs=[pl.BlockSpec((N_PLANES, DSP_TILE, V7X_LANES), lambda s, *_: (0, s, 0))],
        out_specs=pl.BlockSpec(memory_space=pl.ANY),
        scratch_shapes=[pltpu.SemaphoreType.DMA((1,))],
    )
    return pl.pallas_call(
        _dispatch_kernel,
        grid_spec=grid_spec,
        out_shape=jax.ShapeDtypeStruct((N_PAIRS, N_PLANES, V7X_LANES), f32),
        compiler_params=_cparams(("arbitrary",)),
        name="dispatch",
    )(*pos, x1p)


def _moe_kernel(layer, wt_ref, we_ref, wlo_ref, whi_ref, wfirst_ref, wlast_ref, wefirst_ref, weslot_ref,
                wenext_ref, nw_ref,
                xs_hbm, w1_hbm, w3_hbm, w2_hbm, ys_hbm,
                xin, yout, sem_in, sem_out, wst1, wst3, wst2, sem_w, w1b, w3b, w2b):
    w = pl.program_id(0)
    n_items = nw_ref[0]

    def weight_copies(expert, slot):
        return [pltpu.make_async_copy(src.at[layer, expert], dst.at[slot], sem_w.at[slot])
                for src, dst in ((w1_hbm, wst1), (w3_hbm, wst3), (w2_hbm, wst2))]

    def in_copy(item, slot, c):
        r0 = pl.multiple_of(wt_ref[item] * MOE_TM, MOE_TM)
        return pltpu.make_async_copy(xs_hbm.at[pl.ds(r0, MOE_TM), c, :], xin.at[slot, c], sem_in.at[slot])

    def out_copy(tile, slot, c):
        r0 = pl.multiple_of(tile * MOE_TM, MOE_TM)
        return pltpu.make_async_copy(yout.at[slot, c], ys_hbm.at[pl.ds(r0, MOE_TM), c, :], sem_out.at[slot])

    @pl.when(w == 0)
    def _():
        for cp in weight_copies(we_ref[0], 0):
            cp.start()
        for c in range(N_PLANES):
            in_copy(0, 0, c).start(priority=1)

    @pl.when(w + 1 < n_items)
    def _():
        for c in range(N_PLANES):
            in_copy(w + 1, (w + 1) % 2, c).start(priority=1)

    @pl.when(w < n_items)
    def _():
        slot = w % 2
        tile = wt_ref[w]
        oslot = tile % 2
        for c in range(N_PLANES):
            in_copy(w, slot, c).wait()

        @pl.when(wefirst_ref[w] == 1)
        def _():
            wslot = weslot_ref[w]
            for cp in weight_copies(we_ref[w], wslot):
                cp.wait()
            w1b[...] = wst1[wslot].astype(bf16)
            w3b[...] = wst3[wslot].astype(bf16)
            w2b[...] = wst2[wslot].astype(bf16)

            @pl.when(wenext_ref[w] >= 0)
            def _():
                for cp in weight_copies(wenext_ref[w], 1 - wslot):
                    cp.start()

        lo = wlo_ref[w]
        hi = whi_ref[w]
        first = wfirst_ref[w] == 1

        @pl.when(first & (tile >= 2))
        def _():
            for c in range(N_PLANES):
                out_copy(0, oslot, c).wait()

        def ffn(rows):
            xt = jnp.concatenate([xin[slot, c, rows, :] for c in range(N_PLANES)], axis=1).astype(bf16)
            h = jax.nn.silu(_dot(xt, w1b[...])) * _dot(xt, w3b[...])
            return _dot(h.astype(bf16), w2b[...])

        def store(rows, res):
            row = rows.start + lax.broadcasted_iota(jnp.int32, (rows.stop - rows.start, V7X_LANES), 0)
            mine = jnp.where(row >= lo, row, MOE_TM) < hi
            for c in range(N_PLANES):
                res_c = res[:, c * V7X_LANES:(c + 1) * V7X_LANES]

                @pl.when(first)
                def _():
                    yout[oslot, c, rows, :] = jnp.where(mine, res_c, 0.0)

                @pl.when(jnp.logical_not(first))
                def _():
                    yout[oslot, c, rows, :] = jnp.where(mine, res_c, yout[oslot, c, rows, :])

        whole = (lo == 0) & (hi == MOE_TM)

        @pl.when(whole)
        def _():
            rows = slice(0, MOE_TM)
            store(rows, ffn(rows))

        @pl.when(jnp.logical_not(whole))
        def _():
            for q in range(MOE_TM // MOE_SUB):
                rows = slice(q * MOE_SUB, (q + 1) * MOE_SUB)
                touched = (lo < rows.stop) & (hi > rows.start)

                @pl.when(touched)
                def _():
                    store(rows, ffn(rows))

                @pl.when(jnp.logical_not(touched) & first)
                def _():
                    for c in range(N_PLANES):
                        yout[oslot, c, rows, :] = jnp.zeros((MOE_SUB, V7X_LANES), f32)

        @pl.when(wlast_ref[w] == 1)
        def _():
            for c in range(N_PLANES):
                out_copy(tile, oslot, c).start()

        @pl.when(w == n_items - 1)
        def _():
            for c in range(N_PLANES):
                out_copy(0, oslot, c).wait()

            @pl.when(tile >= 1)
            def _():
                for c in range(N_PLANES):
                    out_copy(0, 1 - oslot, c).wait()


def _moe(plan, xs, w1, w3, w2, layer):
    grid_spec = pltpu.PrefetchScalarGridSpec(
        num_scalar_prefetch=len(plan),
        grid=(MOE_MAX_ITEMS,),
        in_specs=[pl.BlockSpec(memory_space=pl.ANY)] * 4,
        out_specs=pl.BlockSpec(memory_space=pl.ANY),
        scratch_shapes=[pltpu.VMEM((2, N_PLANES, MOE_TM, V7X_LANES), f32),
                        pltpu.VMEM((2, N_PLANES, MOE_TM, V7X_LANES), f32),
                        pltpu.SemaphoreType.DMA((2,)), pltpu.SemaphoreType.DMA((2,)),
                        pltpu.VMEM((2, D_MODEL, MOE_HIDDEN), f32), pltpu.VMEM((2, D_MODEL, MOE_HIDDEN), f32),
                        pltpu.VMEM((2, MOE_HIDDEN, D_MODEL), f32), pltpu.SemaphoreType.DMA((2,)),
                        pltpu.VMEM((D_MODEL, MOE_HIDDEN), bf16), pltpu.VMEM((D_MODEL, MOE_HIDDEN), bf16),
                        pltpu.VMEM((MOE_HIDDEN, D_MODEL), bf16)],
    )
    return pl.pallas_call(
        functools.partial(_moe_kernel, layer),
        grid_spec=grid_spec,
        out_shape=jax.ShapeDtypeStruct((N_PAIRS, N_PLANES, V7X_LANES), f32),
        compiler_params=_cparams(("arbitrary",)),
        name="moe",
    )(*plan, xs, w1, w3, w2)


def _combine_kernel(final, pos0_ref, pos1_ref, ys_hbm, x1_ref, route_ref, g_ref, b_ref, *rest):
    if final:
        yp_ref, ysm_ref, buf, sem, t3_ref = rest
    else:
        o_ref, ob_ref, buf, sem = rest
    s = pl.program_id(0)
    nsteps = pl.num_programs(0)

    def gather_copy(row, slot, k, r):
        return pltpu.make_async_copy(ys_hbm.at[row], buf.at[slot, k, :, r, :], sem.at[slot])

    def issue(tile, slot):
        def body(r, c):
            n = tile * CMB_TILE + r
            for k in range(MOE_TOPK):
                gather_copy((pos0_ref, pos1_ref)[k][n], slot, k, r).start(priority=k % 2)
            return c

        lax.fori_loop(0, CMB_TILE, body, 0, unroll=8)

    @pl.when(s == 0)
    def _():
        issue(0, 0)

    @pl.when(s + 1 < nsteps)
    def _():
        issue(s + 1, (s + 1) % 2)

    slot = s % 2
    for k in range(MOE_TOPK):
        pltpu.make_async_copy(buf.at[slot, k], buf.at[slot, k], sem.at[slot]).wait()
    route = route_ref[...]
    moe = route[:, 2:3] * _from_planes(buf.at[slot, 0]) + route[:, 3:4] * _from_planes(buf.at[slot, 1])
    y = _layer_norm_rows(DN_ALPHA * _from_planes(x1_ref) + moe, g_ref[...], b_ref[...])
    if not final:
        o_ref[...] = y
        ob_ref[...] = y.astype(bf16)
        return

    @pl.when(s < CMB_P_STEPS)
    def _():
        t3_ref[...] = y.reshape(CMB_TILE // BATCH, BATCH, D_MODEL)
        for b in range(BATCH):
            yp_ref[b] = t3_ref[:, b, :]

    steps_per_tile = CMB_TILE // DEC_BATCH
    for q in range(N_S // CMB_TILE):
        @pl.when(s == CMB_P_STEPS + q)
        def _():
            for h in range(steps_per_tile):
                ysm_ref[:, q * steps_per_tile + h, :] = y[h * DEC_BATCH:(h + 1) * DEC_BATCH, :]


def _combine(pos, ys, x1, route, g, b, final):
    if final:
        out_specs = [pl.BlockSpec((BATCH, CMB_TILE // BATCH, D_MODEL),
                                  lambda s, *_: (0, jnp.minimum(s, CMB_P_STEPS - 1), 0)),
                     pl.BlockSpec((DEC_BATCH, DEC_SEQ, D_MODEL), lambda s, *_: (0, 0, 0))]
        out_shape = [jax.ShapeDtypeStruct((BATCH, SEQ, D_MODEL), f32),
                     jax.ShapeDtypeStruct((DEC_BATCH, DEC_SEQ, D_MODEL), f32)]
        extra = [pltpu.VMEM((CMB_TILE // BATCH, BATCH, D_MODEL), f32)]
    else:
        out_specs = [pl.BlockSpec((CMB_TILE, D_MODEL), lambda s, *_: (s, 0))] * 2
        out_shape = [jax.ShapeDtypeStruct((N_TOK, D_MODEL), f32), jax.ShapeDtypeStruct((N_TOK, D_MODEL), bf16)]
        extra = []
    grid_spec = pltpu.PrefetchScalarGridSpec(
        num_scalar_prefetch=MOE_TOPK,
        grid=(N_TOK // CMB_TILE,),
        in_specs=[pl.BlockSpec(memory_space=pl.ANY),
                  pl.BlockSpec((N_PLANES, CMB_TILE, V7X_LANES), lambda s, *_: (0, s, 0)),
                  pl.BlockSpec((CMB_TILE, ROUTE_LANES), lambda s, *_: (s, 0)),
                  pl.BlockSpec((1, D_MODEL), lambda s, *_: (0, 0)),
                  pl.BlockSpec((1, D_MODEL), lambda s, *_: (0, 0))],
        out_specs=out_specs,
        scratch_shapes=[pltpu.VMEM((2, MOE_TOPK, N_PLANES, CMB_TILE, V7X_LANES), f32),
                        pltpu.SemaphoreType.DMA((2,))] + extra,
    )
    return pl.pallas_call(
        functools.partial(_combine_kernel, final),
        grid_spec=grid_spec,
        out_shape=out_shape,
        compiler_params=_cparams(("arbitrary",)),
        name="combine_out" if final else "combine",
    )(*pos, ys, x1, route, g, b)


def _to_rows_kernel(xp_ref, xs_ref, o_ref, ob_ref, t3_ref):
    i = pl.program_id(0)

    @pl.when(i < P_TILES)
    def _():
        for b in range(BATCH):
            t3_ref[:, b, :] = xp_ref[b]
        rows = t3_ref[...].reshape(ROW_TILE, D_MODEL)
        o_ref[...] = rows
        ob_ref[...] = rows.astype(bf16)

    @pl.when(i >= P_TILES)
    def _():
        for t in range(DEC_SEQ):
            rows = xs_ref[:, t, :]
            o_ref[t * DEC_BATCH:(t + 1) * DEC_BATCH, :] = rows
            ob_ref[t * DEC_BATCH:(t + 1) * DEC_BATCH, :] = rows.astype(bf16)


def _to_rows(x_prompt, x_sample):
    return pl.pallas_call(
        _to_rows_kernel,
        grid=(N_TILES,),
        in_specs=[pl.BlockSpec((BATCH, P_TC, D_MODEL), lambda i: (0, jnp.minimum(i, P_TILES - 1), 0)),
                  pl.BlockSpec((DEC_BATCH, DEC_SEQ, D_MODEL), lambda i: (0, 0, 0))],
        out_specs=[pl.BlockSpec((ROW_TILE, D_MODEL), lambda i: (i, 0))] * 2,
        out_shape=[jax.ShapeDtypeStruct((N_TOK, D_MODEL), f32), jax.ShapeDtypeStruct((N_TOK, D_MODEL), bf16)],
        scratch_shapes=[pltpu.VMEM((P_TC, BATCH, D_MODEL), f32)],
        compiler_params=_cparams(("arbitrary",)),
        name="to_rows",
    )(x_prompt, x_sample)


def _lookup(table, idx):
    ar = jnp.arange(MOE_EXPERTS, dtype=jnp.int32).reshape((MOE_EXPERTS,) + (1,) * idx.ndim)
    table = table.reshape(ar.shape)
    return jnp.sum(jnp.where(idx[None] == ar, table, 0), axis=0)


def _dispatch_plan(route_t, cnt):
    i32 = jnp.int32
    e = route_t[0:2].astype(i32)
    rank = route_t[4:6].astype(i32)
    counts = cnt[0, :MOE_EXPERTS].astype(i32)
    ends = jnp.cumsum(counts)
    starts = ends - counts
    pos = _lookup(starts, e) + rank

    first_tile = starts // MOE_TM
    last_tile = (ends - 1) // MOE_TM
    ntiles = jnp.where(counts > 0, last_tile - first_tile + 1, 0)
    item_end = jnp.cumsum(ntiles)
    n_items = item_end[-1]
    w = jnp.minimum(jnp.arange(MOE_MAX_ITEMS, dtype=i32), n_items - 1)
    we = jnp.sum((item_end[None, :] <= w[:, None]).astype(i32), axis=-1)
    wt = _lookup(first_tile, we) + w - _lookup(item_end - ntiles, we)
    wlo = jnp.maximum(_lookup(starts, we) - wt * MOE_TM, 0)
    whi = jnp.minimum(_lookup(ends, we) - wt * MOE_TM, MOE_TM)
    changes = (wt[1:] != wt[:-1]).astype(i32)
    wfirst = jnp.concatenate([jnp.ones((1,), i32), changes])
    wlast = jnp.concatenate([changes, jnp.ones((1,), i32)])
    wlast = jnp.where(jnp.arange(MOE_MAX_ITEMS, dtype=i32) == n_items - 1, 1, wlast)
    wefirst = jnp.concatenate([jnp.ones((1,), i32), (we[1:] != we[:-1]).astype(i32)])
    weslot = (jnp.cumsum(wefirst) - 1) % 2
    ar = jnp.arange(MOE_EXPERTS, dtype=i32)
    later = (ar[None, :] > ar[:, None]) & (counts[None, :] > 0)
    next_expert = jnp.min(jnp.where(later, ar[None, :], MOE_EXPERTS), axis=1)
    wenext = _lookup(jnp.where(next_expert < MOE_EXPERTS, next_expert, -1), we)
    return (pos[0], pos[1]), (wt, we, wlo, whi, wfirst, wlast, wefirst, weslot, wenext, n_items.reshape(1))


def _block_diag(w):
    n, a, b = w.shape
    eye = jnp.eye(n, dtype=w.dtype)
    return (w[:, :, None, :] * eye[:, None, :, None]).reshape(n * a, n * b)


def _ssm_params(a_re, a_im, log_dt, b_re, b_im, c_re, c_im):
    ar, ai = a_re, a_im
    dt = jnp.exp(log_dt)[:, None]
    mag = jnp.exp(ar * dt)
    lb_re = mag * jnp.cos(ai * dt)
    lb_im = mag * jnp.sin(ai * dt)
    den = ar * ar + ai * ai
    nr = lb_re - 1.0
    coef_re = (nr * ar + lb_im * ai) / den
    coef_im = (lb_im * ar - nr * ai) / den
    bb_re = coef_re[..., None] * b_re - coef_im[..., None] * b_im
    bb_im = coef_re[..., None] * b_im + coef_im[..., None] * b_re
    gk = SSM_GROUPS // SSM_KB

    def diag_blocks(w):
        return jnp.stack([_block_diag(w[k * gk:(k + 1) * gk]) for k in range(SSM_KB)]).astype(bf16)

    bbre = diag_blocks(bb_re.transpose(0, 2, 1))
    bbim = diag_blocks(bb_im.transpose(0, 2, 1))
    ccre = diag_blocks(c_re.transpose(0, 2, 1))
    ccim = diag_blocks(c_im.transpose(0, 2, 1))
    return (lb_re.reshape(1, SSM_LANES), lb_im.reshape(1, SSM_LANES), bbre, bbim, ccre, ccim)


def kernel(x_prompt, x_sample, state_ret, state_ssm_re, state_ssm_im, state_lru, state_conv, w_in, ret_gn_g, ret_gn_b, ssm_a_re, ssm_a_im, ssm_log_dt, ssm_b_re, ssm_b_im, ssm_c_re, ssm_c_im, ssm_d, ssm_w_glu, lru_conv_w, lru_conv_b, lru_wa, lru_ba, lru_wx, lru_bx, lru_lambda, w_branch, w_out, ln1_g, ln1_b, moe_w_group, moe_b_group, moe_w_expert, moe_b_expert, moe_w1, moe_w3, moe_w2, ln2_g, ln2_b):
    x, xb = _to_rows(x_prompt, x_sample)
    rope = _rope_tables()
    tabs_p = _ret_tables(RET_SUB_T)
    tabs_s = _ret_tables(DEC_SEQ)
    row = lambda v: v.reshape(1, -1)
    w_bf = w_in.astype(bf16)

    outs = [[] for _ in range(10)]
    for l in range(DEPTH):

        s0 = state_ret.reshape(DEPTH, DEC_BATCH, 2, 2 * RET_DK, RET_DV)
        y_ret, ret_p, ret_s = _retention(xb, w_bf, rope, s0, l, tabs_p, tabs_s,
                                         row(ret_gn_g[l]), row(ret_gn_b[l]))

        sp = _ssm_params(ssm_a_re[l], ssm_a_im[l], ssm_log_dt[l], ssm_b_re[l], ssm_b_im[l],
                         ssm_c_re[l], ssm_c_im[l])
        y_ssm, re_p, im_p, re_s, im_s = _ssm(
            xb, w_bf, l, state_ssm_re[l].reshape(DEC_BATCH, SSM_LANES), state_ssm_im[l].reshape(DEC_BATCH, SSM_LANES),
            *sp, row(ssm_d[l]), ssm_w_glu[l].astype(bf16))

        conv0 = state_conv[l].transpose(1, 0, 2).reshape(LRU_HIST_S, BRANCH_W)
        y_lru, lru_p, conv_p, lru_s, conv_s = _lru(
            xb, w_bf, l, state_lru[l], conv0, lru_conv_w[l], row(lru_conv_b[l]),
            _block_diag(lru_wa[l]).astype(bf16), row(lru_ba[l]),
            _block_diag(lru_wx[l]).astype(bf16), row(lru_bx[l]), row(lru_lambda[l]))

        wr = jnp.zeros((D_MODEL, ROUTE_LANES), f32)
        wr = wr.at[:, 0:MOE_GROUPS].set(moe_w_group[l]).at[:, MOE_GROUPS:MOE_GROUPS + MOE_EXPERTS].set(moe_w_expert[l])
        br = jnp.zeros((1, ROUTE_LANES), f32)
        br = br.at[0, 0:MOE_GROUPS].set(moe_b_group[l]).at[0, MOE_GROUPS:MOE_GROUPS + MOE_EXPERTS].set(moe_b_expert[l])
        x1, route, route_t, cnt = _merge(y_ret, y_ssm, y_lru, xb, w_bf, l, x, w_branch[l].astype(bf16), w_out[l].astype(bf16),
                                row(ln1_g[l]), row(ln1_b[l]), wr.astype(bf16), br)

        pos, plan = _dispatch_plan(route_t, cnt)
        xs = _dispatch(pos, x1)
        ys = _moe(plan, xs, moe_w1, moe_w3, moe_w2, l)
        x, xb = _combine(pos, ys, x1, route, row(ln2_g[l]), row(ln2_b[l]), final=(l == DEPTH - 1))

        outs[0].append(ret_p.reshape(BATCH, RET_HEADS, RET_DK, RET_DV))
        outs[1].append(re_p.reshape(BATCH, SSM_GROUPS, SSM_STATE))
        outs[2].append(im_p.reshape(BATCH, SSM_GROUPS, SSM_STATE))
        outs[3].append(lru_p)
        outs[4].append(conv_p.reshape(CONV_W - 1, BATCH, BRANCH_W).transpose(1, 0, 2))
        outs[5].append(ret_s.reshape(DEC_BATCH, RET_HEADS, RET_DK, RET_DV))
        outs[6].append(re_s.reshape(DEC_BATCH, SSM_GROUPS, SSM_STATE))
        outs[7].append(im_s.reshape(DEC_BATCH, SSM_GROUPS, SSM_STATE))
        outs[8].append(lru_s)
        outs[9].append(conv_s.reshape(CONV_W - 1, DEC_BATCH, BRANCH_W).transpose(1, 0, 2))

    y_prompt, y_sample = x, xb
    return (y_prompt, y_sample) + tuple(jnp.stack(o) for o in outs)
```

```python
import functools

import jax
import jax.numpy as jnp
import numpy as np
from jax import lax
from jax.experimental import pallas as pl
from jax.experimental.pallas import tpu as pltpu

f32 = jnp.float32
bf16 = jnp.bfloat16

D_MODEL = 1024
BATCH = 8
SEQ = 2048
DEPTH = 2
DEC_BATCH = 128
DEC_SEQ = 8
PAST_LEN = 16384
BRANCH_W = 512
N_BRANCH = 3
RET_HEADS = 4
RET_DK = 64
RET_DV = 128
ROPE_BASE = 10000.0
SSM_GROUP = 16
SSM_GROUPS = 32
SSM_STATE = 64
SSM_LANES = SSM_GROUPS * SSM_STATE
LRU_BLOCKS = 8
LRU_BW = 64
CONV_W = 4
LRU_C = 8.0
MOE_GROUPS = 4
MOE_PER_GROUP = 8
MOE_EXPERTS = 32
MOE_TOPK = 2
MOE_HIDDEN = 512
DN_ALPHA = (2.0 * DEPTH) ** 0.25
LN_EPS = 1e-5
D_IN = 6144

V7X_SUBLANES = 8
V7X_LANES = 128
V7X_VMEM_LIMIT = 56 * 1024 * 1024

N_P = BATCH * SEQ
N_S = DEC_BATCH * DEC_SEQ
N_TOK = N_P + N_S
ROW_TILE = 1024
P_TILES = N_P // ROW_TILE
N_TILES = N_TOK // ROW_TILE
P_TC = ROW_TILE // BATCH
RET_SUB_T = 32
RET_SUB_R = RET_SUB_T * BATCH
RET_SUBS = ROW_TILE // RET_SUB_R
S_BLOCKS = DEC_BATCH // BATCH
S_BLOCK_R = DEC_SEQ * BATCH
MERGE_TILE = 512
MOE_TM = 512
N_PAIRS = N_TOK * MOE_TOPK
MOE_MAX_ITEMS = N_PAIRS // MOE_TM + MOE_EXPERTS - 1
DSP_TILE = 1024
CMB_TILE = 256
CMB_P_STEPS = N_P // CMB_TILE
ROUTE_LANES = 128


PROJ_RC = 256
RET_COLS, RET_BLK = 1536, 0
SSM_COLS, SSM_BLK = 512, 3
LRU_COLS, LRU_BLK = 1024, 2
GATE_COLS, GATE_BLK = 3072, 1


def _cparams(sem):
    return pltpu.CompilerParams(dimension_semantics=sem, vmem_limit_bytes=V7X_VMEM_LIMIT)


N_PLANES = D_MODEL // V7X_LANES


def _to_planes(ref, rows):
    for c in range(N_PLANES):
        ref[c] = rows[:, c * V7X_LANES:(c + 1) * V7X_LANES]


def _from_planes(ref):
    return jnp.concatenate([ref[c] for c in range(N_PLANES)], axis=1)


def _layer_spec(stacked, layer):
    shape = (None,) + stacked.shape[1:]
    return pl.BlockSpec(shape, lambda *_: (layer,) + (0,) * (stacked.ndim - 1), pipeline_mode=pl.Buffered(1))


def _const_spec(block_shape, index):
    return pl.BlockSpec(block_shape, lambda *_: index, pipeline_mode=pl.Buffered(1))


def _dot(a, b):
    return jnp.dot(a, b, preferred_element_type=f32)


def _dot_nt(a, b):
    return lax.dot_general(a, b, (((1,), (1,)), ((), ())), preferred_element_type=f32)


def _dot_tn(a, b):
    return lax.dot_general(a, b, (((0,), (0,)), ((), ())), preferred_element_type=f32)


def _layer_norm_rows(x, g, b):
    mu = jnp.mean(x, -1, keepdims=True)
    xc = x - mu
    var = jnp.mean(xc * xc, -1, keepdims=True)
    return xc * lax.rsqrt(var + LN_EPS) * g + b


def _project_rows(xb_ref, w_ref, z_ref):
    for rc in range(ROW_TILE // PROJ_RC):
        rs = slice(rc * PROJ_RC, (rc + 1) * PROJ_RC)
        z_ref[rs, :] = _dot(xb_ref[rs, :], w_ref[0])


def _ret_block(q, k, v, g, cosb, sinb, mask_ref, qdec_ref, kdec_ref, cdec_ref, scat_ref, gng, gnb):
    rows = q.shape[0]
    lane_qk = lax.broadcasted_iota(jnp.int32, (rows, 2 * V7X_LANES), 1)
    first_half = (lane_qk & (RET_DK - 1)) < (RET_DK // 2)

    def rope(x):
        partner = jnp.where(first_half, pltpu.roll(x, 2 * V7X_LANES - RET_DK // 2, 1),
                            pltpu.roll(x, RET_DK // 2, 1))
        return x * cosb + partner * sinb

    q = rope(q)
    k = rope(k) * (RET_DK ** -0.5)
    kd = k * kdec_ref[...]
    lane = lax.broadcasted_iota(jnp.int32, (rows, V7X_LANES), 1)
    row_b = lax.broadcasted_iota(jnp.int32, (rows, V7X_LANES), 0) & (BATCH - 1)
    outs = []
    for p in range(2):
        qp = q[:, p * V7X_LANES:(p + 1) * V7X_LANES]
        kp = k[:, p * V7X_LANES:(p + 1) * V7X_LANES].astype(bf16)
        kdp = kd[:, p * V7X_LANES:(p + 1) * V7X_LANES]
        s_old = scat_ref[p]
        s_bf = s_old.astype(bf16)
        s_new = s_old * jnp.concatenate([cdec_ref[p]] * BATCH, axis=1)
        for hh in range(2):
            h = 2 * p + hh
            head_lanes = (lane >= RET_DK) if hh else (lane < RET_DK)
            qh = jnp.where(head_lanes, qp, 0.0).astype(bf16)
            kdh = jnp.where(head_lanes, kdp, 0.0).astype(bf16)
            vh = v[:, h * RET_DV:(h + 1) * RET_DV]
            vh_bf = vh.astype(bf16)
            sc = _dot_nt(qh, kp) * mask_ref[h]
            o = _dot(sc.astype(bf16), vh_bf)
            cross = _dot(qh, s_bf)
            oc = jnp.zeros((rows, RET_DV), f32)
            for b in range(BATCH):
                oc = oc + jnp.where(row_b == b, cross[:, b * RET_DV:(b + 1) * RET_DV], 0.0)
            o = o + oc * qdec_ref[h]
            vcat = jnp.concatenate([jnp.where(row_b == b, vh_bf, jnp.zeros_like(vh_bf))
                                    for b in range(BATCH)], axis=1)
            s_new = s_new + _dot_tn(kdh, vcat)
            mu = jnp.mean(o, -1, keepdims=True)
            oc2 = o - mu
            var = jnp.mean(oc2 * oc2, -1, keepdims=True)
            outs.append(oc2 * lax.rsqrt(var + LN_EPS))
        scat_ref[p] = s_new
    o = jnp.concatenate(outs, axis=1) * gng + gnb
    return jax.nn.silu(g) * o


def _per_step_rows(tab_ref, t0, steps):
    return jnp.concatenate(
        [jnp.broadcast_to(tab_ref[t0 + t:t0 + t + 1, :], (BATCH, tab_ref.shape[1])) for t in range(steps)], axis=0)


def _ret_kernel(xb_ref, w_ref, cos_ref, sin_ref, cos_s_ref, sin_s_ref, s0_ref,
                mask_p_ref, qdec_p_ref, kdec_p_ref, cdec_p_ref,
                mask_s_ref, qdec_s_ref, kdec_s_ref, cdec_s_ref,
                gng_ref, gnb_ref,
                y_ref, retp_ref, rets_ref, scat_ref, z_ref):
    i = pl.program_id(0)
    gng = gng_ref[...]
    gnb = gnb_ref[...]
    q_cols, k_cols = slice(0, 256), slice(256, 512)
    v_cols, g_cols = slice(512, 1024), slice(1024, 1536)

    @pl.when(i == 0)
    def _():
        scat_ref[...] = jnp.zeros_like(scat_ref)

    @pl.when(i <= P_TILES)
    def _():
        _project_rows(xb_ref, w_ref, z_ref)

    @pl.when(i < P_TILES)
    def _():
        for sc in range(RET_SUBS):
            r0 = sc * RET_SUB_R
            rs = slice(r0, r0 + RET_SUB_R)
            y_ref[rs, :] = _ret_block(
                z_ref[rs, q_cols], z_ref[rs, k_cols], z_ref[rs, v_cols], z_ref[rs, g_cols],
                _per_step_rows(cos_ref, sc * RET_SUB_T, RET_SUB_T), _per_step_rows(sin_ref, sc * RET_SUB_T, RET_SUB_T),
                mask_p_ref, qdec_p_ref, kdec_p_ref, cdec_p_ref, scat_ref, gng, gnb)

    @pl.when(i == P_TILES - 1)
    def _():
        for b in range(BATCH):
            for p in range(2):
                retp_ref[b, p] = scat_ref[p, :, b * RET_DV:(b + 1) * RET_DV]

    @pl.when(i >= P_TILES)
    def _():
        bb = i - P_TILES
        for b in range(BATCH):
            for p in range(2):
                scat_ref[p, :, b * RET_DV:(b + 1) * RET_DV] = s0_ref[0, b, p]

        def rows_of(cols):
            return jnp.concatenate(
                [z_ref[pl.ds(pl.multiple_of(t * DEC_BATCH + bb * BATCH, BATCH), BATCH), cols]
                 for t in range(DEC_SEQ)], axis=0)

        y = _ret_block(
            rows_of(q_cols), rows_of(k_cols), rows_of(v_cols),
            rows_of(g_cols), _per_step_rows(cos_s_ref, 0, DEC_SEQ), _per_step_rows(sin_s_ref, 0, DEC_SEQ),
            mask_s_ref, qdec_s_ref, kdec_s_ref, cdec_s_ref, scat_ref, gng, gnb)
        for t in range(DEC_SEQ):
            y_ref[pl.ds(pl.multiple_of(t * DEC_BATCH + bb * BATCH, BATCH), BATCH), :] = (
                y[t * BATCH:(t + 1) * BATCH, :])
        for b in range(BATCH):
            for p in range(2):
                rets_ref[b, p] = scat_ref[p, :, b * RET_DV:(b + 1) * RET_DV]


def _ret_tables(tc):
    rows = tc * BATCH
    nf = np.float32
    log_g = np.log1p(-np.exp2(nf(-5.0) - np.arange(RET_HEADS, dtype=nf))).astype(nf)
    t_idx = (np.arange(rows) // BATCH).astype(nf)
    b_idx = np.arange(rows) % BATCH
    rel = t_idx[:, None] - t_idx[None, :]
    same = b_idx[:, None] == b_idx[None, :]
    decay = np.exp(log_g[:, None, None] * np.maximum(rel, nf(0.0)))
    mask = np.where((rel >= 0) & same, decay, nf(0.0))
    qdec = np.exp(log_g[:, None] * (t_idx[None, :] + nf(1.0)))
    qdec = np.broadcast_to(qdec[:, :, None], (RET_HEADS, rows, RET_DV))
    kdec = np.exp(log_g[:, None] * (nf(tc - 1.0) - t_idx[None, :]))
    kdec = np.broadcast_to(kdec.T[:, :, None], (rows, RET_HEADS, RET_DK)).reshape(rows, RET_HEADS * RET_DK)
    cdec = np.exp(log_g * nf(tc))
    cdec = np.broadcast_to(cdec[:, None, None], (RET_HEADS, RET_DK, RET_DV)).reshape(2, 2 * RET_DK, RET_DV)
    return tuple(jnp.asarray(np.ascontiguousarray(a), dtype=f32) for a in (mask, qdec, kdec, cdec))


def _rope_tables():
    half = RET_DK // 2
    inv = ROPE_BASE ** (-jnp.arange(half, dtype=f32) / half)
    pos_p = jnp.arange(SEQ, dtype=f32)
    pos_s = PAST_LEN + jnp.arange(DEC_SEQ, dtype=f32)

    def tab(pos):
        ang = pos[:, None] * inv[None, :]
        cos = jnp.cos(ang)
        sin = jnp.sin(ang)
        cos_h = jnp.concatenate([cos, cos], axis=1)
        sin_h = jnp.concatenate([-sin, sin], axis=1)
        return jnp.tile(cos_h, (1, RET_HEADS)), jnp.tile(sin_h, (1, RET_HEADS))

    return tab(pos_p) + tab(pos_s)


def _retention(xb, w_bf, rope, s0_s, layer, tabs_p, tabs_s, gng, gnb):
    n_steps = P_TILES + S_BLOCKS
    tile = lambda i: jnp.minimum(i, P_TILES)
    sblk = lambda i: jnp.maximum(i - P_TILES, 0)
    full = lambda a: pl.BlockSpec(a.shape, lambda i, _n=a.ndim: (0,) * _n)
    state_blk = (BATCH, 2, 2 * RET_DK, RET_DV)
    cos_p, sin_p, cos_s, sin_s = rope
    ptile = lambda i: jnp.minimum(i, P_TILES - 1)
    ins = [xb, w_bf, cos_p, sin_p, cos_s, sin_s, s0_s, *tabs_p, *tabs_s, gng, gnb]
    in_specs = [
        pl.BlockSpec((ROW_TILE, D_MODEL), lambda i: (tile(i), 0)),
        _const_spec((1, D_MODEL, RET_COLS), (layer, 0, RET_BLK)),
        pl.BlockSpec((P_TC, 256), lambda i: (ptile(i), 0)),
        pl.BlockSpec((P_TC, 256), lambda i: (ptile(i), 0)),
        full(cos_s), full(sin_s),
        pl.BlockSpec((1,) + state_blk, lambda i: (layer, sblk(i), 0, 0, 0)),
    ] + [full(a) for a in (*tabs_p, *tabs_s)] + [_layer_spec(gng, layer), _layer_spec(gnb, layer)]
    return pl.pallas_call(
        _ret_kernel,
        grid=(n_steps,),
        in_specs=in_specs,
        out_specs=[pl.BlockSpec((ROW_TILE, BRANCH_W), lambda i: (tile(i), 0)),
                   pl.BlockSpec(state_blk, lambda i: (0, 0, 0, 0)),
                   pl.BlockSpec(state_blk, lambda i: (sblk(i), 0, 0, 0))],
        out_shape=[jax.ShapeDtypeStruct((N_TOK, BRANCH_W), f32),
                   jax.ShapeDtypeStruct((BATCH, 2, 2 * RET_DK, RET_DV), f32),
                   jax.ShapeDtypeStruct((DEC_BATCH, 2, 2 * RET_DK, RET_DV), f32)],
        scratch_shapes=[pltpu.VMEM((2, 2 * RET_DK, BATCH * RET_DV), f32),
                        pltpu.VMEM((ROW_TILE, RET_COLS), f32)],
        compiler_params=_cparams(("arbitrary",)),
        name="retention",
    )(*ins)


SSM_LB = 512
SSM_RC = 256
SSM_KB = 2
SSM_KB_U = BRANCH_W // SSM_KB
SSM_KB_H = SSM_LANES // SSM_KB


def _ssm_scan(bre_ref, bim_ref, lre_ref, lim_ref, h_re0, h_im0, row0, nb_rows, steps, lb, unroll):
    ls = slice(lb * SSM_LB, (lb + 1) * SSM_LB)
    a_re = jnp.broadcast_to(lre_ref[:, ls], (V7X_SUBLANES, SSM_LB))
    a_im = jnp.broadcast_to(lim_ref[:, ls], (V7X_SUBLANES, SSM_LB))

    def step(t, carry):
        h_re, h_im = carry
        r = pl.multiple_of(row0 + t * nb_rows, V7X_SUBLANES)
        n_re = a_re * h_re - a_im * h_im + bre_ref[pl.ds(r, V7X_SUBLANES), ls]
        n_im = a_re * h_im + a_im * h_re + bim_ref[pl.ds(r, V7X_SUBLANES), ls]
        bre_ref[pl.ds(r, V7X_SUBLANES), ls] = n_re
        bim_ref[pl.ds(r, V7X_SUBLANES), ls] = n_im
        return n_re, n_im

    return lax.fori_loop(0, steps, step, (h_re0, h_im0), unroll=unroll)


def _ssm_kernel(xb_ref, w_ref, h0re_ref, h0im_ref, lre_ref, lim_ref, bbre_ref, bbim_ref, ccre_ref, ccim_ref,
                d_ref, wglu_ref,
                y_ref, pre_ref, pim_ref, sre_ref, sim_ref,
                bre_ref, bim_ref, hre_ref, him_ref, zs_ref):
    i = pl.program_id(0)
    _project_rows(xb_ref, w_ref, zs_ref)
    for rc in range(ROW_TILE // SSM_RC):
        rs = slice(rc * SSM_RC, (rc + 1) * SSM_RC)
        ub = zs_ref[rs, :].astype(bf16)
        for k in range(SSM_KB):
            uk = ub[:, k * SSM_KB_U:(k + 1) * SSM_KB_U]
            hs = slice(k * SSM_KB_H, (k + 1) * SSM_KB_H)
            bre_ref[rs, hs] = _dot(uk, bbre_ref[k])
            bim_ref[rs, hs] = _dot(uk, bbim_ref[k])

    @pl.when(i == 0)
    def _():
        hre_ref[...] = jnp.zeros_like(hre_ref)
        him_ref[...] = jnp.zeros_like(him_ref)

    @pl.when(i < P_TILES)
    def _():
        for lb in range(SSM_LANES // SSM_LB):
            ls = slice(lb * SSM_LB, (lb + 1) * SSM_LB)
            h_re, h_im = _ssm_scan(bre_ref, bim_ref, lre_ref, lim_ref, hre_ref[:, ls], him_ref[:, ls],
                                   0, BATCH, P_TC, lb, 8)
            hre_ref[:, ls] = h_re
            him_ref[:, ls] = h_im
        pre_ref[...] = hre_ref[...]
        pim_ref[...] = him_ref[...]

    @pl.when(i >= P_TILES)
    def _():
        def per_row_tile(rt, c):
            r0 = pl.multiple_of(rt * V7X_SUBLANES, V7X_SUBLANES)
            for lb in range(SSM_LANES // SSM_LB):
                ls = slice(lb * SSM_LB, (lb + 1) * SSM_LB)
                _ssm_scan(bre_ref, bim_ref, lre_ref, lim_ref,
                          h0re_ref[pl.ds(r0, V7X_SUBLANES), ls], h0im_ref[pl.ds(r0, V7X_SUBLANES), ls],
                          r0, DEC_BATCH, DEC_SEQ, lb, True)
            return c

        lax.fori_loop(0, DEC_BATCH // V7X_SUBLANES, per_row_tile, 0)
        last = (DEC_SEQ - 1) * DEC_BATCH
        sre_ref[...] = bre_ref[last:last + DEC_BATCH, :]
        sim_ref[...] = bim_ref[last:last + DEC_BATCH, :]

    for rc in range(ROW_TILE // SSM_RC):
        rs = slice(rc * SSM_RC, (rc + 1) * SSM_RC)
        ch = []
        for k in range(SSM_KB):
            hs = slice(k * SSM_KB_H, (k + 1) * SSM_KB_H)
            ch.append(_dot(bre_ref[rs, hs].astype(bf16), ccre_ref[k]) - _dot(bim_ref[rs, hs].astype(bf16), ccim_ref[k]))
        y = jnp.concatenate(ch, axis=1) + d_ref[...] * zs_ref[rs, :]
        zz = jax.nn.gelu(y)
        y_ref[rs, :] = zz * jax.nn.sigmoid(_dot(zz.astype(bf16), wglu_ref[...]))


def _ssm(xb, w_bf, layer, h0re, h0im, lre, lim, bbre, bbim, ccre, ccim, dvec, wglu):
    full = lambda a: _layer_spec(a, layer)
    consts = (h0re, h0im, lre, lim, bbre, bbim, ccre, ccim, dvec, wglu)
    return pl.pallas_call(
        _ssm_kernel,
        grid=(N_TILES,),
        in_specs=[pl.BlockSpec((ROW_TILE, D_MODEL), lambda i: (i, 0)),
                  _const_spec((1, D_MODEL, SSM_COLS), (layer, 0, SSM_BLK))] + [full(a) for a in consts],
        out_specs=[pl.BlockSpec((ROW_TILE, BRANCH_W), lambda i: (i, 0)),
                   pl.BlockSpec((BATCH, SSM_LANES), lambda i: (0, 0)),
                   pl.BlockSpec((BATCH, SSM_LANES), lambda i: (0, 0)),
                   pl.BlockSpec((DEC_BATCH, SSM_LANES), lambda i: (0, 0)),
                   pl.BlockSpec((DEC_BATCH, SSM_LANES), lambda i: (0, 0))],
        out_shape=[jax.ShapeDtypeStruct((N_TOK, BRANCH_W), f32),
                   jax.ShapeDtypeStruct((BATCH, SSM_LANES), f32),
                   jax.ShapeDtypeStruct((BATCH, SSM_LANES), f32),
                   jax.ShapeDtypeStruct((DEC_BATCH, SSM_LANES), f32),
                   jax.ShapeDtypeStruct((DEC_BATCH, SSM_LANES), f32)],
        scratch_shapes=[pltpu.VMEM((ROW_TILE, SSM_LANES), f32), pltpu.VMEM((ROW_TILE, SSM_LANES), f32),
                        pltpu.VMEM((BATCH, SSM_LANES), f32), pltpu.VMEM((BATCH, SSM_LANES), f32),
                        pltpu.VMEM((ROW_TILE, SSM_COLS), f32)],
        compiler_params=_cparams(("arbitrary",)),
        name="ssm",
    )(xb, w_bf, *consts)


LRU_HIST_P = (CONV_W - 1) * BATCH
LRU_HIST_S = (CONV_W - 1) * DEC_BATCH


def _lru_gates(xe_ref, nb_rows, cw_ref, cb_ref, wa_ref, ba_ref, wx_ref, bx_ref, lam_ref):
    xc = cb_ref[...] + xe_ref[0:ROW_TILE, :] * cw_ref[0:1, :]
    for j in range(1, CONV_W):
        xc = xc + xe_ref[j * nb_rows:j * nb_rows + ROW_TILE, :] * cw_ref[j:j + 1, :]
    xcb = xc.astype(bf16)
    r = jax.nn.sigmoid(_dot(xcb, wa_ref[...]) + ba_ref[...])
    ig = jax.nn.sigmoid(_dot(xcb, wx_ref[...]) + bx_ref[...])
    log_a = -LRU_C * r * jax.nn.softplus(-lam_ref[...])
    a = jnp.exp(log_a)
    b = jnp.sqrt(-jnp.tanh(log_a) * (a * a + 1.0)) * (ig * xc)
    return a, b


def _lru_kernel(xb_ref, w_ref, h0_ref, conv0_ref, cw_ref, cb_ref, wa_ref, ba_ref, wx_ref, bx_ref, lam_ref,
                y_ref, hp_ref, convp_ref, hs_ref, convs_ref,
                xe_ref, a_ref, b_ref, hc_ref, z_ref):
    i = pl.program_id(0)
    params = (cw_ref, cb_ref, wa_ref, ba_ref, wx_ref, bx_ref, lam_ref)
    _project_rows(xb_ref, w_ref, z_ref)
    zx_ref = z_ref.at[:, 0:BRANCH_W]
    zg_ref = z_ref.at[:, BRANCH_W:2 * BRANCH_W]

    @pl.when(i == 0)
    def _():
        xe_ref[0:LRU_HIST_P, :] = jnp.zeros((LRU_HIST_P, BRANCH_W), f32)
        hc_ref[...] = jnp.zeros_like(hc_ref)

    @pl.when(i < P_TILES)
    def _():
        xe_ref[LRU_HIST_P:LRU_HIST_P + ROW_TILE, :] = zx_ref[...]
        a, b = _lru_gates(xe_ref, BATCH, *params)
        a_ref[...] = a
        b_ref[...] = b
        hist = xe_ref[ROW_TILE:ROW_TILE + LRU_HIST_P, :]
        xe_ref[0:LRU_HIST_P, :] = hist
        convp_ref[...] = hist

        def step(t, h):
            r = pl.multiple_of(t * BATCH, BATCH)
            h = a_ref[pl.ds(r, BATCH), :] * h + b_ref[pl.ds(r, BATCH), :]
            b_ref[pl.ds(r, BATCH), :] = h
            return h

        h = lax.fori_loop(0, P_TC, step, hc_ref[...], unroll=8)
        hc_ref[...] = h
        hp_ref[...] = h

    @pl.when(i >= P_TILES)
    def _():
        xe_ref[0:LRU_HIST_S, :] = conv0_ref[...]
        xe_ref[LRU_HIST_S:LRU_HIST_S + ROW_TILE, :] = zx_ref[...]
        a, b = _lru_gates(xe_ref, DEC_BATCH, *params)
        a_ref[...] = a
        b_ref[...] = b
        convs_ref[...] = xe_ref[ROW_TILE:ROW_TILE + LRU_HIST_S, :]

        def per_row_tile(rt, c):
            r0 = pl.multiple_of(rt * V7X_SUBLANES, V7X_SUBLANES)
            h = h0_ref[pl.ds(r0, V7X_SUBLANES), :]
            for t in range(DEC_SEQ):
                r = pl.multiple_of(t * DEC_BATCH + r0, V7X_SUBLANES)
                h = a_ref[pl.ds(r, V7X_SUBLANES), :] * h + b_ref[pl.ds(r, V7X_SUBLANES), :]
                b_ref[pl.ds(r, V7X_SUBLANES), :] = h
            return c

        lax.fori_loop(0, DEC_BATCH // V7X_SUBLANES, per_row_tile, 0)
        last = (DEC_SEQ - 1) * DEC_BATCH
        hs_ref[...] = b_ref[last:last + DEC_BATCH, :]

    y_ref[...] = b_ref[...] * jax.nn.gelu(zg_ref[...])


def _lru(xb, w_bf, layer, h0, conv0, cw, cb, wa, ba, wx, bx, lam):
    full = lambda a: _layer_spec(a, layer)
    consts = (h0, conv0, cw, cb, wa, ba, wx, bx, lam)
    return pl.pallas_call(
        _lru_kernel,
        grid=(N_TILES,),
        in_specs=[pl.BlockSpec((ROW_TILE, D_MODEL), lambda i: (i, 0)),
                  _const_spec((1, D_MODEL, LRU_COLS), (layer, 0, LRU_BLK))] + [full(a) for a in consts],
        out_specs=[pl.BlockSpec((ROW_TILE, BRANCH_W), lambda i: (i, 0)),
                   pl.BlockSpec((BATCH, BRANCH_W), lambda i: (0, 0)),
                   pl.BlockSpec((LRU_HIST_P, BRANCH_W), lambda i: (0, 0)),
                   pl.BlockSpec((DEC_BATCH, BRANCH_W), lambda i: (0, 0)),
                   pl.BlockSpec((LRU_HIST_S, BRANCH_W), lambda i: (0, 0))],
        out_shape=[jax.ShapeDtypeStruct((N_TOK, BRANCH_W), f32),
                   jax.ShapeDtypeStruct((BATCH, BRANCH_W), f32),
                   jax.ShapeDtypeStruct((LRU_HIST_P, BRANCH_W), f32),
                   jax.ShapeDtypeStruct((DEC_BATCH, BRANCH_W), f32),
                   jax.ShapeDtypeStruct((LRU_HIST_S, BRANCH_W), f32)],
        scratch_shapes=[pltpu.VMEM((ROW_TILE + LRU_HIST_S, BRANCH_W), f32),
                        pltpu.VMEM((ROW_TILE, BRANCH_W), f32), pltpu.VMEM((ROW_TILE, BRANCH_W), f32),
                        pltpu.VMEM((BATCH, BRANCH_W), f32),
                        pltpu.VMEM((ROW_TILE, LRU_COLS), f32)],
        compiler_params=_cparams(("arbitrary",)),
        name="lru",
    )(xb, w_bf, *consts)


def _merge_kernel(yr_ref, ys_ref, yl_ref, xb_ref, wg_ref, x_ref, wb_ref, wo_ref, g_ref, b_ref, wr_ref, br_ref,
                  x1_ref, route_ref, route_t_ref, cnt_out_ref, cnt_ref):
    merged = jnp.zeros((MERGE_TILE, D_MODEL), f32)
    xb = xb_ref[...]
    for n, y_ref in enumerate((yr_ref, ys_ref, yl_ref)):
        proj = _dot(y_ref[...].astype(bf16), wb_ref[n])
        gate = jax.nn.sigmoid(_dot(xb, wg_ref[0, :, n * D_MODEL:(n + 1) * D_MODEL]))
        merged = merged + gate * proj
    mix = _dot(merged.astype(bf16), wo_ref[...])
    x1 = _layer_norm_rows(DN_ALPHA * x_ref[...] + mix, g_ref[...], b_ref[...])
    _to_planes(x1_ref, x1)

    logits = _dot(x1.astype(bf16), wr_ref[...]) + br_ref[...]
    lane = lax.broadcasted_iota(jnp.int32, (MERGE_TILE, ROUTE_LANES), 1).astype(f32)
    big = jnp.float32(ROUTE_LANES)
    neg = jnp.float32(-jnp.inf)
    is_g = lane < MOE_GROUPS
    lg = jnp.where(is_g, logits, neg)
    mg = jnp.max(lg, -1, keepdims=True)
    gsel = jnp.min(jnp.where(lg == mg, lane, big), -1, keepdims=True)
    sum_g = jnp.sum(jnp.where(is_g, jnp.exp(lg - mg), 0.0), -1, keepdims=True)
    pg_sel = 1.0 / sum_g
    lo = MOE_GROUPS + gsel * MOE_PER_GROUP
    is_e = jnp.abs(lane - lo - 0.5 * (MOE_PER_GROUP - 1)) < 0.5 * MOE_PER_GROUP
    le = jnp.where(is_e, logits, neg)
    me = jnp.max(le, -1, keepdims=True)
    ex = jnp.where(is_e, jnp.exp(le - me), 0.0)
    pe = jnp.where(is_e, ex / jnp.sum(ex, -1, keepdims=True), -1.0)
    v1 = jnp.max(pe, -1, keepdims=True)
    i1 = jnp.min(jnp.where(pe == v1, lane, big), -1, keepdims=True)
    pe2 = jnp.where(lane == i1, -1.0, pe)
    v2 = jnp.max(pe2, -1, keepdims=True)
    i2 = jnp.min(jnp.where(pe2 == v2, lane, big), -1, keepdims=True)
    vsum = v1 + v2
    w1 = pg_sel * v1 / vsum
    w2 = pg_sel * v2 / vsum
    e1 = i1 - MOE_GROUPS
    e2 = i2 - MOE_GROUPS

    @pl.when(pl.program_id(0) == 0)
    def _():
        cnt_ref[...] = jnp.zeros_like(cnt_ref)

    oh1 = lane == e1
    oh2 = lane == e2
    ohs = jnp.where(oh1, 1.0, jnp.where(oh2, 1.0, 0.0))
    r_i = lax.broadcasted_iota(jnp.int32, (MERGE_TILE, MERGE_TILE), 0)
    c_i = lax.broadcasted_iota(jnp.int32, (MERGE_TILE, MERGE_TILE), 1)
    strict_lower = jnp.where(c_i < r_i, 1.0, 0.0).astype(bf16)
    before = _dot(strict_lower, ohs.astype(bf16)) + cnt_ref[0:1, :]
    rank1 = jnp.sum(jnp.where(oh1, before, 0.0), -1, keepdims=True)
    rank2 = jnp.sum(jnp.where(oh2, before, 0.0), -1, keepdims=True)
    cnt_ref[0:1, :] = cnt_ref[0:1, :] + jnp.sum(ohs, 0, keepdims=True)
    cnt_out_ref[...] = cnt_ref[...]

    route = jnp.zeros((MERGE_TILE, ROUTE_LANES), f32)
    for k, val in enumerate((e1, e2, w1, w2, rank1, rank2)):
        route = jnp.where(lane == k, val, route)
    route_ref[...] = route
    route_t_ref[...] = route.T[0:V7X_SUBLANES, :]


def _merge(y_ret, y_ssm, y_lru, xb, w_bf, layer, x, wb, wo, g, b, wr, br):
    full = lambda a: _layer_spec(a, layer)
    row = lambda w: pl.BlockSpec((MERGE_TILE, w), lambda i: (i, 0))
    consts = (wb, wo, g, b, wr, br)
    return pl.pallas_call(
        _merge_kernel,
        grid=(N_TOK // MERGE_TILE,),
        in_specs=[row(BRANCH_W), row(BRANCH_W), row(BRANCH_W), row(D_MODEL),
                  _const_spec((1, D_MODEL, GATE_COLS), (layer, 0, GATE_BLK)),
                  row(D_MODEL)] + [full(a) for a in consts],
        out_specs=[pl.BlockSpec((N_PLANES, MERGE_TILE, V7X_LANES), lambda i: (0, i, 0)), row(ROUTE_LANES),
                   pl.BlockSpec((V7X_SUBLANES, MERGE_TILE), lambda i: (0, i)),
                   pl.BlockSpec((V7X_SUBLANES, ROUTE_LANES), lambda i: (0, 0))],
        out_shape=[jax.ShapeDtypeStruct((N_PLANES, N_TOK, V7X_LANES), f32),
                   jax.ShapeDtypeStruct((N_TOK, ROUTE_LANES), f32),
                   jax.ShapeDtypeStruct((V7X_SUBLANES, N_TOK), f32),
                   jax.ShapeDtypeStruct((V7X_SUBLANES, ROUTE_LANES), f32)],
        scratch_shapes=[pltpu.VMEM((V7X_SUBLANES, ROUTE_LANES), f32)],
        compiler_params=_cparams(("arbitrary",)),
        name="merge",
    )(y_ret, y_ssm, y_lru, xb, w_bf, x, *consts)


def _dispatch_kernel(pos0_ref, pos1_ref, x1_ref, xs_hbm, sem):
    base = pl.program_id(0) * DSP_TILE

    def row_copy(r, dst_row):
        return pltpu.make_async_copy(x1_ref.at[:, r, :], xs_hbm.at[dst_row], sem.at[0])

    def issue(r, c):
        n = base + r
        for k in range(MOE_TOPK):
            row_copy(r, (pos0_ref, pos1_ref)[k][n]).start(priority=k % 2)
        return c

    lax.fori_loop(0, DSP_TILE, issue, 0, unroll=8)
    for _ in range(MOE_TOPK):
        pltpu.make_async_copy(x1_ref, x1_ref, sem.at[0]).wait()


def _dispatch(pos, x1p):
    grid_spec = pltpu.PrefetchScalarGridSpec(
        num_scalar_prefetch=MOE_TOPK,
        grid=(N_TOK // DSP_TILE,),
        in_specs=[pl.BlockSpec((N_PLANES, DSP_TILE, V7X_LANES), lambda s, *_: (0, s, 0))],
        out_specs=pl.BlockSpec(memory_space=pl.ANY),
        scratch_shapes=[pltpu.SemaphoreType.DMA((1,))],
    )
    return pl.pallas_call(
        _dispatch_kernel,
        grid_spec=grid_spec,
        out_shape=jax.ShapeDtypeStruct((N_PAIRS, N_PLANES, V7X_LANES), f32),
        compiler_params=_cparams(("arbitrary",)),
        name="dispatch",
    )(*pos, x1p)


def _moe_kernel(layer, wt_ref, we_ref, wlo_ref, whi_ref, wfirst_ref, wlast_ref, wefirst_ref, weslot_ref,
                wenext_ref, nw_ref,
                xs_hbm, w1_hbm, w3_hbm, w2_hbm, ys_hbm,
                xin, yout, sem_in, sem_out, wst1, wst3, wst2, sem_w, w1b, w3b, w2b):
    w = pl.program_id(0)
    n_items = nw_ref[0]

    def weight_copies(expert, slot):
        return [pltpu.make_async_copy(src.at[layer, expert], dst.at[slot], sem_w.at[slot])
                for src, dst in ((w1_hbm, wst1), (w3_hbm, wst3), (w2_hbm, wst2))]

    def in_copy(item, slot, c):
        r0 = pl.multiple_of(wt_ref[item] * MOE_TM, MOE_TM)
        return pltpu.make_async_copy(xs_hbm.at[pl.ds(r0, MOE_TM), c, :], xin.at[slot, c], sem_in.at[slot])

    def out_copy(tile, slot, c):
        r0 = pl.multiple_of(tile * MOE_TM, MOE_TM)
        return pltpu.make_async_copy(yout.at[slot, c], ys_hbm.at[pl.ds(r0, MOE_TM), c, :], sem_out.at[slot])

    @pl.when(w == 0)
    def _():
        for cp in weight_copies(we_ref[0], 0):
            cp.start()
        for c in range(N_PLANES):
            in_copy(0, 0, c).start(priority=1)

    @pl.when(w + 1 < n_items)
    def _():
        for c in range(N_PLANES):
            in_copy(w + 1, (w + 1) % 2, c).start(priority=1)

    @pl.when(w < n_items)
    def _():
        slot = w % 2
        tile = wt_ref[w]
        oslot = tile % 2
        for c in range(N_PLANES):
            in_copy(w, slot, c).wait()

        @pl.when(wefirst_ref[w] == 1)
        def _():
            wslot = weslot_ref[w]
            for cp in weight_copies(we_ref[w], wslot):
                cp.wait()
            w1b[...] = wst1[wslot].astype(bf16)
            w3b[...] = wst3[wslot].astype(bf16)
            w2b[...] = wst2[wslot].astype(bf16)

            @pl.when(wenext_ref[w] >= 0)
            def _():
                for cp in weight_copies(wenext_ref[w], 1 - wslot):
                    cp.start()

        xt = _from_planes(xin.at[slot]).astype(bf16)
        h = jax.nn.silu(_dot(xt, w1b[...])) * _dot(xt, w3b[...])
        res = _dot(h.astype(bf16), w2b[...])
        row = lax.broadcasted_iota(jnp.int32, (MOE_TM, D_MODEL), 0)
        mine = jnp.where(row >= wlo_ref[w], row, MOE_TM) < whi_ref[w]

        @pl.when(wfirst_ref[w] == 1)
        def _():
            @pl.when(tile >= 2)
            def _():
                for c in range(N_PLANES):
                    out_copy(0, oslot, c).wait()

            _to_planes(yout.at[oslot], jnp.where(mine, res, 0.0))

        @pl.when(wfirst_ref[w] == 0)
        def _():
            _to_planes(yout.at[oslot], jnp.where(mine, res, _from_planes(yout.at[oslot])))

        @pl.when(wlast_ref[w] == 1)
        def _():
            for c in range(N_PLANES):
                out_copy(tile, oslot, c).start()

        @pl.when(w == n_items - 1)
        def _():
            for c in range(N_PLANES):
                out_copy(0, oslot, c).wait()

            @pl.when(tile >= 1)
            def _():
                for c in range(N_PLANES):
                    out_copy(0, 1 - oslot, c).wait()


def _moe(plan, xs, w1, w3, w2, layer):
    grid_spec = pltpu.PrefetchScalarGridSpec(
        num_scalar_prefetch=len(plan),
        grid=(MOE_MAX_ITEMS,),
        in_specs=[pl.BlockSpec(memory_space=pl.ANY)] * 4,
        out_specs=pl.BlockSpec(memory_space=pl.ANY),
        scratch_shapes=[pltpu.VMEM((2, N_PLANES, MOE_TM, V7X_LANES), f32),
                        pltpu.VMEM((2, N_PLANES, MOE_TM, V7X_LANES), f32),
                        pltpu.SemaphoreType.DMA((2,)), pltpu.SemaphoreType.DMA((2,)),
                        pltpu.VMEM((2, D_MODEL, MOE_HIDDEN), f32), pltpu.VMEM((2, D_MODEL, MOE_HIDDEN), f32),
                        pltpu.VMEM((2, MOE_HIDDEN, D_MODEL), f32), pltpu.SemaphoreType.DMA((2,)),
                        pltpu.VMEM((D_MODEL, MOE_HIDDEN), bf16), pltpu.VMEM((D_MODEL, MOE_HIDDEN), bf16),
                        pltpu.VMEM((MOE_HIDDEN, D_MODEL), bf16)],
    )
    return pl.pallas_call(
        functools.partial(_moe_kernel, layer),
        grid_spec=grid_spec,
        out_shape=jax.ShapeDtypeStruct((N_PAIRS, N_PLANES, V7X_LANES), f32),
        compiler_params=_cparams(("arbitrary",)),
        name="moe",
    )(*plan, xs, w1, w3, w2)


def _combine_kernel(final, pos0_ref, pos1_ref, ys_hbm, x1_ref, route_ref, g_ref, b_ref, *rest):
    if final:
        yp_ref, ysm_ref, buf, sem, t3_ref = rest
    else:
        o_ref, ob_ref, buf, sem = rest
    s = pl.program_id(0)
    nsteps = pl.num_programs(0)

    def gather_copy(row, slot, k, r):
        return pltpu.make_async_copy(ys_hbm.at[row], buf.at[slot, k, :, r, :], sem.at[slot])

    def issue(tile, slot):
        def body(r, c):
            n = tile * CMB_TILE + r
            for k in range(MOE_TOPK):
                gather_copy((pos0_ref, pos1_ref)[k][n], slot, k, r).start(priority=k % 2)
            return c

        lax.fori_loop(0, CMB_TILE, body, 0, unroll=8)

    @pl.when(s == 0)
    def _():
        issue(0, 0)

    @pl.when(s + 1 < nsteps)
    def _():
        issue(s + 1, (s + 1) % 2)

    slot = s % 2
    for k in range(MOE_TOPK):
        pltpu.make_async_copy(buf.at[slot, k], buf.at[slot, k], sem.at[slot]).wait()
    route = route_ref[...]
    moe = route[:, 2:3] * _from_planes(buf.at[slot, 0]) + route[:, 3:4] * _from_planes(buf.at[slot, 1])
    y = _layer_norm_rows(DN_ALPHA * _from_planes(x1_ref) + moe, g_ref[...], b_ref[...])
    if not final:
        o_ref[...] = y
        ob_ref[...] = y.astype(bf16)
        return

    @pl.when(s < CMB_P_STEPS)
    def _():
        t3_ref[...] = y.reshape(CMB_TILE // BATCH, BATCH, D_MODEL)
        for b in range(BATCH):
            yp_ref[b] = t3_ref[:, b, :]

    steps_per_tile = CMB_TILE // DEC_BATCH
    for q in range(N_S // CMB_TILE):
        @pl.when(s == CMB_P_STEPS + q)
        def _():
            for h in range(steps_per_tile):
                ysm_ref[:, q * steps_per_tile + h, :] = y[h * DEC_BATCH:(h + 1) * DEC_BATCH, :]


def _combine(pos, ys, x1, route, g, b, layer, final):
    if final:
        out_specs = [pl.BlockSpec((BATCH, CMB_TILE // BATCH, D_MODEL),
                                  lambda s, *_: (0, jnp.minimum(s, CMB_P_STEPS - 1), 0)),
                     pl.BlockSpec((DEC_BATCH, DEC_SEQ, D_MODEL), lambda s, *_: (0, 0, 0))]
        out_shape = [jax.ShapeDtypeStruct((BATCH, SEQ, D_MODEL), f32),
                     jax.ShapeDtypeStruct((DEC_BATCH, DEC_SEQ, D_MODEL), f32)]
        extra = [pltpu.VMEM((CMB_TILE // BATCH, BATCH, D_MODEL), f32)]
    else:
        out_specs = [pl.BlockSpec((CMB_TILE, D_MODEL), lambda s, *_: (s, 0))] * 2
        out_shape = [jax.ShapeDtypeStruct((N_TOK, D_MODEL), f32), jax.ShapeDtypeStruct((N_TOK, D_MODEL), bf16)]
        extra = []
    grid_spec = pltpu.PrefetchScalarGridSpec(
        num_scalar_prefetch=MOE_TOPK,
        grid=(N_TOK // CMB_TILE,),
        in_specs=[pl.BlockSpec(memory_space=pl.ANY),
                  pl.BlockSpec((N_PLANES, CMB_TILE, V7X_LANES), lambda s, *_: (0, s, 0)),
                  pl.BlockSpec((CMB_TILE, ROUTE_LANES), lambda s, *_: (s, 0)),
                  _layer_spec(g, layer), _layer_spec(b, layer)],
        out_specs=out_specs,
        scratch_shapes=[pltpu.VMEM((2, MOE_TOPK, N_PLANES, CMB_TILE, V7X_LANES), f32),
                        pltpu.SemaphoreType.DMA((2,))] + extra,
    )
    return pl.pallas_call(
        functools.partial(_combine_kernel, final),
        grid_spec=grid_spec,
        out_shape=out_shape,
        compiler_params=_cparams(("arbitrary",)),
        name="combine_out" if final else "combine",
    )(*pos, ys, x1, route, g, b)


def _to_rows_kernel(xp_ref, xs_ref, o_ref, ob_ref, t3_ref):
    i = pl.program_id(0)

    @pl.when(i < P_TILES)
    def _():
        for b in range(BATCH):
            t3_ref[:, b, :] = xp_ref[b]
        rows = t3_ref[...].reshape(ROW_TILE, D_MODEL)
        o_ref[...] = rows
        ob_ref[...] = rows.astype(bf16)

    @pl.when(i >= P_TILES)
    def _():
        for t in range(DEC_SEQ):
            rows = xs_ref[:, t, :]
            o_ref[t * DEC_BATCH:(t + 1) * DEC_BATCH, :] = rows
            ob_ref[t * DEC_BATCH:(t + 1) * DEC_BATCH, :] = rows.astype(bf16)


def _to_rows(x_prompt, x_sample):
    return pl.pallas_call(
        _to_rows_kernel,
        grid=(N_TILES,),
        in_specs=[pl.BlockSpec((BATCH, P_TC, D_MODEL), lambda i: (0, jnp.minimum(i, P_TILES - 1), 0)),
                  pl.BlockSpec((DEC_BATCH, DEC_SEQ, D_MODEL), lambda i: (0, 0, 0))],
        out_specs=[pl.BlockSpec((ROW_TILE, D_MODEL), lambda i: (i, 0))] * 2,
        out_shape=[jax.ShapeDtypeStruct((N_TOK, D_MODEL), f32), jax.ShapeDtypeStruct((N_TOK, D_MODEL), bf16)],
        scratch_shapes=[pltpu.VMEM((P_TC, BATCH, D_MODEL), f32)],
        compiler_params=_cparams(("arbitrary",)),
        name="to_rows",
    )(x_prompt, x_sample)


def _lookup(table, idx):
    ar = jnp.arange(MOE_EXPERTS, dtype=jnp.int32).reshape((MOE_EXPERTS,) + (1,) * idx.ndim)
    table = table.reshape(ar.shape)
    return jnp.sum(jnp.where(idx[None] == ar, table, 0), axis=0)


def _dispatch_plan(route_t, cnt):
    i32 = jnp.int32
    e = route_t[0:2].astype(i32)
    rank = route_t[4:6].astype(i32)
    counts = cnt[0, :MOE_EXPERTS].astype(i32)
    ends = jnp.cumsum(counts)
    starts = ends - counts
    pos = _lookup(starts, e) + rank

    first_tile = starts // MOE_TM
    last_tile = (ends - 1) // MOE_TM
    ntiles = jnp.where(counts > 0, last_tile - first_tile + 1, 0)
    item_end = jnp.cumsum(ntiles)
    n_items = item_end[-1]
    w = jnp.minimum(jnp.arange(MOE_MAX_ITEMS, dtype=i32), n_items - 1)
    we = jnp.sum((item_end[None, :] <= w[:, None]).astype(i32), axis=-1)
    wt = _lookup(first_tile, we) + w - _lookup(item_end - ntiles, we)
    wlo = jnp.maximum(_lookup(starts, we) - wt * MOE_TM, 0)
    whi = jnp.minimum(_lookup(ends, we) - wt * MOE_TM, MOE_TM)
    changes = (wt[1:] != wt[:-1]).astype(i32)
    wfirst = jnp.concatenate([jnp.ones((1,), i32), changes])
    wlast = jnp.concatenate([changes, jnp.ones((1,), i32)])
    wlast = jnp.where(jnp.arange(MOE_MAX_ITEMS, dtype=i32) == n_items - 1, 1, wlast)
    wefirst = jnp.concatenate([jnp.ones((1,), i32), (we[1:] != we[:-1]).astype(i32)])
    weslot = (jnp.cumsum(wefirst) - 1) % 2
    ar = jnp.arange(MOE_EXPERTS, dtype=i32)
    later = (ar[None, :] > ar[:, None]) & (counts[None, :] > 0)
    next_expert = jnp.min(jnp.where(later, ar[None, :], MOE_EXPERTS), axis=1)
    wenext = _lookup(jnp.where(next_expert < MOE_EXPERTS, next_expert, -1), we)
    return (pos[0], pos[1]), (wt, we, wlo, whi, wfirst, wlast, wefirst, weslot, wenext, n_items.reshape(1))


def _block_diag(w):
    n, a, b = w.shape
    eye = jnp.eye(n, dtype=w.dtype)
    return (w[:, :, None, :] * eye[:, None, :, None]).reshape(n * a, n * b)


def _ssm_params(a_re, a_im, log_dt, b_re, b_im, c_re, c_im):
    ar, ai = a_re, a_im
    dt = jnp.exp(log_dt)[:, None]
    mag = jnp.exp(ar * dt)
    lb_re = mag * jnp.cos(ai * dt)
    lb_im = mag * jnp.sin(ai * dt)
    den = ar * ar + ai * ai
    nr = lb_re - 1.0
    coef_re = (nr * ar + lb_im * ai) / den
    coef_im = (lb_im * ar - nr * ai) / den
    bb_re = coef_re[..., None] * b_re - coef_im[..., None] * b_im
    bb_im = coef_re[..., None] * b_im + coef_im[..., None] * b_re
    gk = SSM_GROUPS // SSM_KB

    def diag_blocks(w):
        return jnp.stack([_block_diag(w[k * gk:(k + 1) * gk]) for k in range(SSM_KB)]).astype(bf16)

    bbre = diag_blocks(bb_re.transpose(0, 2, 1))
    bbim = diag_blocks(bb_im.transpose(0, 2, 1))
    ccre = diag_blocks(c_re.transpose(0, 2, 1))
    ccim = diag_blocks(c_im.transpose(0, 2, 1))
    return (lb_re.reshape(1, SSM_LANES), lb_im.reshape(1, SSM_LANES), bbre, bbim, ccre, ccim)


def kernel(x_prompt, x_sample, state_ret, state_ssm_re, state_ssm_im, state_lru, state_conv, w_in, ret_gn_g, ret_gn_b, ssm_a_re, ssm_a_im, ssm_log_dt, ssm_b_re, ssm_b_im, ssm_c_re, ssm_c_im, ssm_d, ssm_w_glu, lru_conv_w, lru_conv_b, lru_wa, lru_ba, lru_wx, lru_bx, lru_lambda, w_branch, w_out, ln1_g, ln1_b, moe_w_group, moe_b_group, moe_w_expert, moe_b_expert, moe_w1, moe_w3, moe_w2, ln2_g, ln2_b):
    x, xb = _to_rows(x_prompt, x_sample)
    rope = _rope_tables()
    tabs_p = _ret_tables(RET_SUB_T)
    tabs_s = _ret_tables(DEC_SEQ)
    rows = lambda v: v.reshape(DEPTH, 1, -1)
    w_bf = w_in.astype(bf16)
    s0 = state_ret.reshape(DEPTH, DEC_BATCH, 2, 2 * RET_DK, RET_DV)
    ssm_p = jax.vmap(_ssm_params)(ssm_a_re, ssm_a_im, ssm_log_dt, ssm_b_re, ssm_b_im, ssm_c_re, ssm_c_im)
    h0re = state_ssm_re.reshape(DEPTH, DEC_BATCH, SSM_LANES)
    h0im = state_ssm_im.reshape(DEPTH, DEC_BATCH, SSM_LANES)
    conv0 = state_conv.transpose(0, 2, 1, 3).reshape(DEPTH, LRU_HIST_S, BRANCH_W)
    lru_wa_bd = jax.vmap(_block_diag)(lru_wa).astype(bf16)
    lru_wx_bd = jax.vmap(_block_diag)(lru_wx).astype(bf16)
    pad_w = jnp.zeros((DEPTH, D_MODEL, ROUTE_LANES - MOE_GROUPS - MOE_EXPERTS), f32)
    wr = jnp.concatenate([moe_w_group, moe_w_expert, pad_w], axis=-1).astype(bf16)
    br = jnp.concatenate([moe_b_group, moe_b_expert, pad_w[:, 0, :]], axis=-1).reshape(DEPTH, 1, ROUTE_LANES)
    wb_bf, wo_bf, wglu_bf = w_branch.astype(bf16), w_out.astype(bf16), ssm_w_glu.astype(bf16)

    outs = [[] for _ in range(10)]
    for l in range(DEPTH):
        y_ret, ret_p, ret_s = _retention(xb, w_bf, rope, s0, l, tabs_p, tabs_s, rows(ret_gn_g), rows(ret_gn_b))
        y_ssm, re_p, im_p, re_s, im_s = _ssm(xb, w_bf, l, h0re, h0im, *ssm_p, rows(ssm_d), wglu_bf)
        y_lru, lru_p, conv_p, lru_s, conv_s = _lru(
            xb, w_bf, l, state_lru, conv0, lru_conv_w, rows(lru_conv_b),
            lru_wa_bd, rows(lru_ba), lru_wx_bd, rows(lru_bx), rows(lru_lambda))
        x1, route, route_t, cnt = _merge(y_ret, y_ssm, y_lru, xb, w_bf, l, x, wb_bf, wo_bf,
                                         rows(ln1_g), rows(ln1_b), wr, br)

        pos, plan = _dispatch_plan(route_t, cnt)
        xs = _dispatch(pos, x1)
        ys = _moe(plan, xs, moe_w1, moe_w3, moe_w2, l)
        x, xb = _combine(pos, ys, x1, route, rows(ln2_g), rows(ln2_b), l, final=(l == DEPTH - 1))

        outs[0].append(ret_p.reshape(BATCH, RET_HEADS, RET_DK, RET_DV))
        outs[1].append(re_p.reshape(BATCH, SSM_GROUPS, SSM_STATE))
        outs[2].append(im_p.reshape(BATCH, SSM_GROUPS, SSM_STATE))
        outs[3].append(lru_p)
        outs[4].append(conv_p.reshape(CONV_W - 1, BATCH, BRANCH_W).transpose(1, 0, 2))
        outs[5].append(ret_s.reshape(DEC_BATCH, RET_HEADS, RET_DK, RET_DV))
        outs[6].append(re_s.reshape(DEC_BATCH, SSM_GROUPS, SSM_STATE))
        outs[7].append(im_s.reshape(DEC_BATCH, SSM_GROUPS, SSM_STATE))
        outs[8].append(lru_s)
        outs[9].append(conv_s.reshape(CONV_W - 1, DEC_BATCH, BRANCH_W).transpose(1, 0, 2))

    y_prompt, y_sample = x, xb
    return (y_prompt, y_sample) + tuple(jnp.stack(o) for o in outs)
```

```python
import functools

import jax
import jax.numpy as jnp
import numpy as np
from jax import lax
from jax.experimental import pallas as pl
from jax.experimental.pallas import tpu as pltpu

f32 = jnp.float32
bf16 = jnp.bfloat16

D_MODEL = 1024
BATCH = 8
SEQ = 2048
DEPTH = 2
DEC_BATCH = 128
DEC_SEQ = 8
PAST_LEN = 16384
BRANCH_W = 512
N_BRANCH = 3
RET_HEADS = 4
RET_DK = 64
RET_DV = 128
ROPE_BASE = 10000.0
SSM_GROUP = 16
SSM_GROUPS = 32
SSM_STATE = 64
SSM_LANES = SSM_GROUPS * SSM_STATE
LRU_BLOCKS = 8
LRU_BW = 64
CONV_W = 4
LRU_C = 8.0
MOE_GROUPS = 4
MOE_PER_GROUP = 8
MOE_EXPERTS = 32
MOE_TOPK = 2
MOE_HIDDEN = 512
DN_ALPHA = (2.0 * DEPTH) ** 0.25
LN_EPS = 1e-5
D_IN = 6144

V7X_SUBLANES = 8
V7X_LANES = 128
V7X_VMEM_LIMIT = 56 * 1024 * 1024

N_P = BATCH * SEQ
N_S = DEC_BATCH * DEC_SEQ
N_TOK = N_P + N_S
ROW_TILE = 1024
P_TILES = N_P // ROW_TILE
N_TILES = N_TOK // ROW_TILE
P_TC = ROW_TILE // BATCH
RET_SUB_T = 32
RET_SUB_R = RET_SUB_T * BATCH
RET_SUBS = ROW_TILE // RET_SUB_R
S_BLOCKS = DEC_BATCH // BATCH
S_BLOCK_R = DEC_SEQ * BATCH
MERGE_TILE = 512
MOE_TM = 512
N_PAIRS = N_TOK * MOE_TOPK
MOE_MAX_ITEMS = N_PAIRS // MOE_TM + MOE_EXPERTS - 1
DSP_TILE = 1024
CMB_TILE = 256
CMB_P_STEPS = N_P // CMB_TILE
ROUTE_LANES = 128


PROJ_RC = 256
RET_COLS, RET_BLK = 1536, 0
SSM_COLS, SSM_BLK = 512, 3
LRU_COLS, LRU_BLK = 1024, 2
GATE_COLS, GATE_BLK = 3072, 1


def _cparams(sem):
    return pltpu.CompilerParams(dimension_semantics=sem, vmem_limit_bytes=V7X_VMEM_LIMIT)


N_PLANES = D_MODEL // V7X_LANES


def _to_planes(ref, rows):
    for c in range(N_PLANES):
        ref[c] = rows[:, c * V7X_LANES:(c + 1) * V7X_LANES]


def _from_planes(ref):
    return jnp.concatenate([ref[c] for c in range(N_PLANES)], axis=1)


def _layer_spec(stacked, layer):
    shape = (None,) + stacked.shape[1:]
    return pl.BlockSpec(shape, lambda *_: (layer,) + (0,) * (stacked.ndim - 1), pipeline_mode=pl.Buffered(1))


def _const_spec(block_shape, index):
    return pl.BlockSpec(block_shape, lambda *_: index, pipeline_mode=pl.Buffered(1))


def _dot(a, b):
    return jnp.dot(a, b, preferred_element_type=f32)


def _dot_nt(a, b):
    return lax.dot_general(a, b, (((1,), (1,)), ((), ())), preferred_element_type=f32)


def _dot_tn(a, b):
    return lax.dot_general(a, b, (((0,), (0,)), ((), ())), preferred_element_type=f32)


def _layer_norm_rows(x, g, b):
    mu = jnp.mean(x, -1, keepdims=True)
    xc = x - mu
    var = jnp.mean(xc * xc, -1, keepdims=True)
    return xc * lax.rsqrt(var + LN_EPS) * g + b


def _project_rows(xb_ref, w_ref, z_ref):
    for rc in range(ROW_TILE // PROJ_RC):
        rs = slice(rc * PROJ_RC, (rc + 1) * PROJ_RC)
        z_ref[rs, :] = _dot(xb_ref[rs, :], w_ref[0])


def _ret_block(q, k, v, g, cosb, sinb, mask_ref, qdec_ref, kdec_ref, cdec_ref, scat_ref, gng, gnb):
    rows = q.shape[0]
    lane_qk = lax.broadcasted_iota(jnp.int32, (rows, 2 * V7X_LANES), 1)
    first_half = (lane_qk & (RET_DK - 1)) < (RET_DK // 2)

    def rope(x):
        partner = jnp.where(first_half, pltpu.roll(x, 2 * V7X_LANES - RET_DK // 2, 1),
                            pltpu.roll(x, RET_DK // 2, 1))
        return x * cosb + partner * sinb

    q = rope(q)
    k = rope(k) * (RET_DK ** -0.5)
    kd = k * kdec_ref[...]
    lane = lax.broadcasted_iota(jnp.int32, (rows, V7X_LANES), 1)
    row_b = lax.broadcasted_iota(jnp.int32, (rows, V7X_LANES), 0) & (BATCH - 1)
    outs = []
    for p in range(2):
        qp = q[:, p * V7X_LANES:(p + 1) * V7X_LANES]
        kp = k[:, p * V7X_LANES:(p + 1) * V7X_LANES].astype(bf16)
        kdp = kd[:, p * V7X_LANES:(p + 1) * V7X_LANES]
        s_old = scat_ref[p]
        s_bf = s_old.astype(bf16)
        s_new = s_old * jnp.concatenate([cdec_ref[p]] * BATCH, axis=1)
        for hh in range(2):
            h = 2 * p + hh
            head_lanes = (lane >= RET_DK) if hh else (lane < RET_DK)
            qh = jnp.where(head_lanes, qp, 0.0).astype(bf16)
            kdh = jnp.where(head_lanes, kdp, 0.0).astype(bf16)
            vh = v[:, h * RET_DV:(h + 1) * RET_DV]
            vh_bf = vh.astype(bf16)
            sc = _dot_nt(qh, kp) * mask_ref[h]
            o = _dot(sc.astype(bf16), vh_bf)
            cross = _dot(qh, s_bf)
            oc = jnp.zeros((rows, RET_DV), f32)
            for b in range(BATCH):
                oc = oc + jnp.where(row_b == b, cross[:, b * RET_DV:(b + 1) * RET_DV], 0.0)
            o = o + oc * qdec_ref[h]
            vcat = jnp.concatenate([jnp.where(row_b == b, vh_bf, jnp.zeros_like(vh_bf))
                                    for b in range(BATCH)], axis=1)
            s_new = s_new + _dot_tn(kdh, vcat)
            mu = jnp.mean(o, -1, keepdims=True)
            oc2 = o - mu
            var = jnp.mean(oc2 * oc2, -1, keepdims=True)
            outs.append(oc2 * lax.rsqrt(var + LN_EPS))
        scat_ref[p] = s_new
    o = jnp.concatenate(outs, axis=1) * gng + gnb
    return jax.nn.silu(g) * o


def _per_step_rows(tab_ref, t0, steps):
    return jnp.concatenate(
        [jnp.broadcast_to(tab_ref[t0 + t:t0 + t + 1, :], (BATCH, tab_ref.shape[1])) for t in range(steps)], axis=0)


def _ret_kernel(xb_ref, w_ref, cos_ref, sin_ref, cos_s_ref, sin_s_ref, s0_ref,
                mask_p_ref, qdec_p_ref, kdec_p_ref, cdec_p_ref,
                mask_s_ref, qdec_s_ref, kdec_s_ref, cdec_s_ref,
                gng_ref, gnb_ref,
                y_ref, retp_ref, rets_ref, scat_ref, z_ref):
    i = pl.program_id(0)
    gng = gng_ref[...]
    gnb = gnb_ref[...]
    q_cols, k_cols = slice(0, 256), slice(256, 512)
    v_cols, g_cols = slice(512, 1024), slice(1024, 1536)

    @pl.when(i == 0)
    def _():
        scat_ref[...] = jnp.zeros_like(scat_ref)

    @pl.when(i <= P_TILES)
    def _():
        _project_rows(xb_ref, w_ref, z_ref)

    @pl.when(i < P_TILES)
    def _():
        for sc in range(RET_SUBS):
            r0 = sc * RET_SUB_R
            rs = slice(r0, r0 + RET_SUB_R)
            y_ref[rs, :] = _ret_block(
                z_ref[rs, q_cols], z_ref[rs, k_cols], z_ref[rs, v_cols], z_ref[rs, g_cols],
                _per_step_rows(cos_ref, sc * RET_SUB_T, RET_SUB_T), _per_step_rows(sin_ref, sc * RET_SUB_T, RET_SUB_T),
                mask_p_ref, qdec_p_ref, kdec_p_ref, cdec_p_ref, scat_ref, gng, gnb)

    @pl.when(i == P_TILES - 1)
    def _():
        for b in range(BATCH):
            for p in range(2):
                retp_ref[b, p] = scat_ref[p, :, b * RET_DV:(b + 1) * RET_DV]

    @pl.when(i >= P_TILES)
    def _():
        bb = i - P_TILES
        for b in range(BATCH):
            for p in range(2):
                scat_ref[p, :, b * RET_DV:(b + 1) * RET_DV] = s0_ref[0, b, p]

        def rows_of(cols):
            return jnp.concatenate(
                [z_ref[pl.ds(pl.multiple_of(t * DEC_BATCH + bb * BATCH, BATCH), BATCH), cols]
                 for t in range(DEC_SEQ)], axis=0)

        y = _ret_block(
            rows_of(q_cols), rows_of(k_cols), rows_of(v_cols),
            rows_of(g_cols), _per_step_rows(cos_s_ref, 0, DEC_SEQ), _per_step_rows(sin_s_ref, 0, DEC_SEQ),
            mask_s_ref, qdec_s_ref, kdec_s_ref, cdec_s_ref, scat_ref, gng, gnb)
        for t in range(DEC_SEQ):
            y_ref[pl.ds(pl.multiple_of(t * DEC_BATCH + bb * BATCH, BATCH), BATCH), :] = (
                y[t * BATCH:(t + 1) * BATCH, :])
        for b in range(BATCH):
            for p in range(2):
                rets_ref[b, p] = scat_ref[p, :, b * RET_DV:(b + 1) * RET_DV]


def _ret_tables(tc):
    rows = tc * BATCH
    nf = np.float32
    log_g = np.log1p(-np.exp2(nf(-5.0) - np.arange(RET_HEADS, dtype=nf))).astype(nf)
    t_idx = (np.arange(rows) // BATCH).astype(nf)
    b_idx = np.arange(rows) % BATCH
    rel = t_idx[:, None] - t_idx[None, :]
    same = b_idx[:, None] == b_idx[None, :]
    decay = np.exp(log_g[:, None, None] * np.maximum(rel, nf(0.0)))
    mask = np.where((rel >= 0) & same, decay, nf(0.0))
    qdec = np.exp(log_g[:, None] * (t_idx[None, :] + nf(1.0)))
    qdec = np.broadcast_to(qdec[:, :, None], (RET_HEADS, rows, RET_DV))
    kdec = np.exp(log_g[:, None] * (nf(tc - 1.0) - t_idx[None, :]))
    kdec = np.broadcast_to(kdec.T[:, :, None], (rows, RET_HEADS, RET_DK)).reshape(rows, RET_HEADS * RET_DK)
    cdec = np.exp(log_g * nf(tc))
    cdec = np.broadcast_to(cdec[:, None, None], (RET_HEADS, RET_DK, RET_DV)).reshape(2, 2 * RET_DK, RET_DV)
    return tuple(jnp.asarray(np.ascontiguousarray(a), dtype=f32) for a in (mask, qdec, kdec, cdec))


def _rope_tables():
    half = RET_DK // 2
    inv = ROPE_BASE ** (-jnp.arange(half, dtype=f32) / half)
    pos_p = jnp.arange(SEQ, dtype=f32)
    pos_s = PAST_LEN + jnp.arange(DEC_SEQ, dtype=f32)

    def tab(pos):
        ang = pos[:, None] * inv[None, :]
        cos = jnp.cos(ang)
        sin = jnp.sin(ang)
        cos_h = jnp.concatenate([cos, cos], axis=1)
        sin_h = jnp.concatenate([-sin, sin], axis=1)
        return jnp.tile(cos_h, (1, RET_HEADS)), jnp.tile(sin_h, (1, RET_HEADS))

    return tab(pos_p) + tab(pos_s)


def _retention(xb, w_bf, rope, s0_s, layer, tabs_p, tabs_s, gng, gnb):
    n_steps = P_TILES + S_BLOCKS
    tile = lambda i: jnp.minimum(i, P_TILES)
    sblk = lambda i: jnp.maximum(i - P_TILES, 0)
    full = lambda a: pl.BlockSpec(a.shape, lambda i, _n=a.ndim: (0,) * _n)
    state_blk = (BATCH, 2, 2 * RET_DK, RET_DV)
    cos_p, sin_p, cos_s, sin_s = rope
    ptile = lambda i: jnp.minimum(i, P_TILES - 1)
    ins = [xb, w_bf, cos_p, sin_p, cos_s, sin_s, s0_s, *tabs_p, *tabs_s, gng, gnb]
    in_specs = [
        pl.BlockSpec((ROW_TILE, D_MODEL), lambda i: (tile(i), 0)),
        _const_spec((1, D_MODEL, RET_COLS), (layer, 0, RET_BLK)),
        pl.BlockSpec((P_TC, 256), lambda i: (ptile(i), 0)),
        pl.BlockSpec((P_TC, 256), lambda i: (ptile(i), 0)),
        full(cos_s), full(sin_s),
        pl.BlockSpec((1,) + state_blk, lambda i: (layer, sblk(i), 0, 0, 0)),
    ] + [full(a) for a in (*tabs_p, *tabs_s)] + [_layer_spec(gng, layer), _layer_spec(gnb, layer)]
    return pl.pallas_call(
        _ret_kernel,
        grid=(n_steps,),
        in_specs=in_specs,
        out_specs=[pl.BlockSpec((ROW_TILE, BRANCH_W), lambda i: (tile(i), 0)),
                   pl.BlockSpec(state_blk, lambda i: (0, 0, 0, 0)),
                   pl.BlockSpec(state_blk, lambda i: (sblk(i), 0, 0, 0))],
        out_shape=[jax.ShapeDtypeStruct((N_TOK, BRANCH_W), f32),
                   jax.ShapeDtypeStruct((BATCH, 2, 2 * RET_DK, RET_DV), f32),
                   jax.ShapeDtypeStruct((DEC_BATCH, 2, 2 * RET_DK, RET_DV), f32)],
        scratch_shapes=[pltpu.VMEM((2, 2 * RET_DK, BATCH * RET_DV), f32),
                        pltpu.VMEM((ROW_TILE, RET_COLS), f32)],
        compiler_params=_cparams(("arbitrary",)),
        name="retention",
    )(*ins)


SSM_LB = 512
SSM_RC = 256
SSM_KB = 2
SSM_KB_U = BRANCH_W // SSM_KB
SSM_KB_H = SSM_LANES // SSM_KB


def _ssm_scan(bre_ref, bim_ref, lre_ref, lim_ref, h_re0, h_im0, row0, nb_rows, steps, lb, unroll):
    ls = slice(lb * SSM_LB, (lb + 1) * SSM_LB)
    a_re = jnp.broadcast_to(lre_ref[:, ls], (V7X_SUBLANES, SSM_LB))
    a_im = jnp.broadcast_to(lim_ref[:, ls], (V7X_SUBLANES, SSM_LB))

    def step(t, carry):
        h_re, h_im = carry
        r = pl.multiple_of(row0 + t * nb_rows, V7X_SUBLANES)
        n_re = a_re * h_re - a_im * h_im + bre_ref[pl.ds(r, V7X_SUBLANES), ls]
        n_im = a_re * h_im + a_im * h_re + bim_ref[pl.ds(r, V7X_SUBLANES), ls]
        bre_ref[pl.ds(r, V7X_SUBLANES), ls] = n_re
        bim_ref[pl.ds(r, V7X_SUBLANES), ls] = n_im
        return n_re, n_im

    return lax.fori_loop(0, steps, step, (h_re0, h_im0), unroll=unroll)


def _ssm_kernel(xb_ref, w_ref, h0re_ref, h0im_ref, lre_ref, lim_ref, bbre_ref, bbim_ref, ccre_ref, ccim_ref,
                d_ref, wglu_ref,
                y_ref, pre_ref, pim_ref, sre_ref, sim_ref,
                bre_ref, bim_ref, hre_ref, him_ref, zs_ref):
    i = pl.program_id(0)
    _project_rows(xb_ref, w_ref, zs_ref)
    for rc in range(ROW_TILE // SSM_RC):
        rs = slice(rc * SSM_RC, (rc + 1) * SSM_RC)
        ub = zs_ref[rs, :].astype(bf16)
        for k in range(SSM_KB):
            uk = ub[:, k * SSM_KB_U:(k + 1) * SSM_KB_U]
            hs = slice(k * SSM_KB_H, (k + 1) * SSM_KB_H)
            bre_ref[rs, hs] = _dot(uk, bbre_ref[k])
            bim_ref[rs, hs] = _dot(uk, bbim_ref[k])

    @pl.when(i == 0)
    def _():
        hre_ref[...] = jnp.zeros_like(hre_ref)
        him_ref[...] = jnp.zeros_like(him_ref)

    @pl.when(i < P_TILES)
    def _():
        for lb in range(SSM_LANES // SSM_LB):
            ls = slice(lb * SSM_LB, (lb + 1) * SSM_LB)
            h_re, h_im = _ssm_scan(bre_ref, bim_ref, lre_ref, lim_ref, hre_ref[:, ls], him_ref[:, ls],
                                   0, BATCH, P_TC, lb, 8)
            hre_ref[:, ls] = h_re
            him_ref[:, ls] = h_im
        pre_ref[...] = hre_ref[...]
        pim_ref[...] = him_ref[...]

    @pl.when(i >= P_TILES)
    def _():
        def per_row_tile(rt, c):
            r0 = pl.multiple_of(rt * V7X_SUBLANES, V7X_SUBLANES)
            for lb in range(SSM_LANES // SSM_LB):
                ls = slice(lb * SSM_LB, (lb + 1) * SSM_LB)
                _ssm_scan(bre_ref, bim_ref, lre_ref, lim_ref,
                          h0re_ref[pl.ds(r0, V7X_SUBLANES), ls], h0im_ref[pl.ds(r0, V7X_SUBLANES), ls],
                          r0, DEC_BATCH, DEC_SEQ, lb, True)
            return c

        lax.fori_loop(0, DEC_BATCH // V7X_SUBLANES, per_row_tile, 0)
        last = (DEC_SEQ - 1) * DEC_BATCH
        sre_ref[...] = bre_ref[last:last + DEC_BATCH, :]
        sim_ref[...] = bim_ref[last:last + DEC_BATCH, :]

    for rc in range(ROW_TILE // SSM_RC):
        rs = slice(rc * SSM_RC, (rc + 1) * SSM_RC)
        ch = []
        for k in range(SSM_KB):
            hs = slice(k * SSM_KB_H, (k + 1) * SSM_KB_H)
            ch.append(_dot(bre_ref[rs, hs].astype(bf16), ccre_ref[k]) - _dot(bim_ref[rs, hs].astype(bf16), ccim_ref[k]))
        y = jnp.concatenate(ch, axis=1) + d_ref[...] * zs_ref[rs, :]
        zz = jax.nn.gelu(y)
        y_ref[rs, :] = zz * jax.nn.sigmoid(_dot(zz.astype(bf16), wglu_ref[...]))


def _ssm(xb, w_bf, layer, h0re, h0im, lre, lim, bbre, bbim, ccre, ccim, dvec, wglu):
    full = lambda a: _layer_spec(a, layer)
    consts = (h0re, h0im, lre, lim, bbre, bbim, ccre, ccim, dvec, wglu)
    return pl.pallas_call(
        _ssm_kernel,
        grid=(N_TILES,),
        in_specs=[pl.BlockSpec((ROW_TILE, D_MODEL), lambda i: (i, 0)),
                  _const_spec((1, D_MODEL, SSM_COLS), (layer, 0, SSM_BLK))] + [full(a) for a in consts],
        out_specs=[pl.BlockSpec((ROW_TILE, BRANCH_W), lambda i: (i, 0)),
                   pl.BlockSpec((BATCH, SSM_LANES), lambda i: (0, 0)),
                   pl.BlockSpec((BATCH, SSM_LANES), lambda i: (0, 0)),
                   pl.BlockSpec((DEC_BATCH, SSM_LANES), lambda i: (0, 0)),
                   pl.BlockSpec((DEC_BATCH, SSM_LANES), lambda i: (0, 0))],
        out_shape=[jax.ShapeDtypeStruct((N_TOK, BRANCH_W), f32),
                   jax.ShapeDtypeStruct((BATCH, SSM_LANES), f32),
                   jax.ShapeDtypeStruct((BATCH, SSM_LANES), f32),
                   jax.ShapeDtypeStruct((DEC_BATCH, SSM_LANES), f32),
                   jax.ShapeDtypeStruct((DEC_BATCH, SSM_LANES), f32)],
        scratch_shapes=[pltpu.VMEM((ROW_TILE, SSM_LANES), f32), pltpu.VMEM((ROW_TILE, SSM_LANES), f32),
                        pltpu.VMEM((BATCH, SSM_LANES), f32), pltpu.VMEM((BATCH, SSM_LANES), f32),
                        pltpu.VMEM((ROW_TILE, SSM_COLS), f32)],
        compiler_params=_cparams(("arbitrary",)),
        name="ssm",
    )(xb, w_bf, *consts)


LRU_HIST_P = (CONV_W - 1) * BATCH
LRU_HIST_S = (CONV_W - 1) * DEC_BATCH


def _lru_gates(xe_ref, nb_rows, cw_ref, cb_ref, wa_ref, ba_ref, wx_ref, bx_ref, lam_ref):
    xc = cb_ref[...] + xe_ref[0:ROW_TILE, :] * cw_ref[0:1, :]
    for j in range(1, CONV_W):
        xc = xc + xe_ref[j * nb_rows:j * nb_rows + ROW_TILE, :] * cw_ref[j:j + 1, :]
    xcb = xc.astype(bf16)
    r = jax.nn.sigmoid(_dot(xcb, wa_ref[...]) + ba_ref[...])
    ig = jax.nn.sigmoid(_dot(xcb, wx_ref[...]) + bx_ref[...])
    log_a = -LRU_C * r * jax.nn.softplus(-lam_ref[...])
    a = jnp.exp(log_a)
    b = jnp.sqrt(-jnp.tanh(log_a) * (a * a + 1.0)) * (ig * xc)
    return a, b


def _lru_kernel(xb_ref, w_ref, h0_ref, conv0_ref, cw_ref, cb_ref, wa_ref, ba_ref, wx_ref, bx_ref, lam_ref,
                y_ref, hp_ref, convp_ref, hs_ref, convs_ref,
                xe_ref, a_ref, b_ref, hc_ref, z_ref):
    i = pl.program_id(0)
    params = (cw_ref, cb_ref, wa_ref, ba_ref, wx_ref, bx_ref, lam_ref)
    _project_rows(xb_ref, w_ref, z_ref)
    zx_ref = z_ref.at[:, 0:BRANCH_W]
    zg_ref = z_ref.at[:, BRANCH_W:2 * BRANCH_W]

    @pl.when(i == 0)
    def _():
        xe_ref[0:LRU_HIST_P, :] = jnp.zeros((LRU_HIST_P, BRANCH_W), f32)
        hc_ref[...] = jnp.zeros_like(hc_ref)

    @pl.when(i < P_TILES)
    def _():
        xe_ref[LRU_HIST_P:LRU_HIST_P + ROW_TILE, :] = zx_ref[...]
        a, b = _lru_gates(xe_ref, BATCH, *params)
        a_ref[...] = a
        b_ref[...] = b
        hist = xe_ref[ROW_TILE:ROW_TILE + LRU_HIST_P, :]
        xe_ref[0:LRU_HIST_P, :] = hist
        convp_ref[...] = hist

        def step(t, h):
            r = pl.multiple_of(t * BATCH, BATCH)
            h = a_ref[pl.ds(r, BATCH), :] * h + b_ref[pl.ds(r, BATCH), :]
            b_ref[pl.ds(r, BATCH), :] = h
            return h

        h = lax.fori_loop(0, P_TC, step, hc_ref[...], unroll=8)
        hc_ref[...] = h
        hp_ref[...] = h

    @pl.when(i >= P_TILES)
    def _():
        xe_ref[0:LRU_HIST_S, :] = conv0_ref[...]
        xe_ref[LRU_HIST_S:LRU_HIST_S + ROW_TILE, :] = zx_ref[...]
        a, b = _lru_gates(xe_ref, DEC_BATCH, *params)
        a_ref[...] = a
        b_ref[...] = b
        convs_ref[...] = xe_ref[ROW_TILE:ROW_TILE + LRU_HIST_S, :]

        def per_row_tile(rt, c):
            r0 = pl.multiple_of(rt * V7X_SUBLANES, V7X_SUBLANES)
            h = h0_ref[pl.ds(r0, V7X_SUBLANES), :]
            for t in range(DEC_SEQ):
                r = pl.multiple_of(t * DEC_BATCH + r0, V7X_SUBLANES)
                h = a_ref[pl.ds(r, V7X_SUBLANES), :] * h + b_ref[pl.ds(r, V7X_SUBLANES), :]
                b_ref[pl.ds(r, V7X_SUBLANES), :] = h
            return c

        lax.fori_loop(0, DEC_BATCH // V7X_SUBLANES, per_row_tile, 0)
        last = (DEC_SEQ - 1) * DEC_BATCH
        hs_ref[...] = b_ref[last:last + DEC_BATCH, :]

    y_ref[...] = b_ref[...] * jax.nn.gelu(zg_ref[...])


def _lru(xb, w_bf, layer, h0, conv0, cw, cb, wa, ba, wx, bx, lam):
    full = lambda a: _layer_spec(a, layer)
    consts = (h0, conv0, cw, cb, wa, ba, wx, bx, lam)
    return pl.pallas_call(
        _lru_kernel,
        grid=(N_TILES,),
        in_specs=[pl.BlockSpec((ROW_TILE, D_MODEL), lambda i: (i, 0)),
                  _const_spec((1, D_MODEL, LRU_COLS), (layer, 0, LRU_BLK))] + [full(a) for a in consts],
        out_specs=[pl.BlockSpec((ROW_TILE, BRANCH_W), lambda i: (i, 0)),
                   pl.BlockSpec((BATCH, BRANCH_W), lambda i: (0, 0)),
                   pl.BlockSpec((LRU_HIST_P, BRANCH_W), lambda i: (0, 0)),
                   pl.BlockSpec((DEC_BATCH, BRANCH_W), lambda i: (0, 0)),
                   pl.BlockSpec((LRU_HIST_S, BRANCH_W), lambda i: (0, 0))],
        out_shape=[jax.ShapeDtypeStruct((N_TOK, BRANCH_W), f32),
                   jax.ShapeDtypeStruct((BATCH, BRANCH_W), f32),
                   jax.ShapeDtypeStruct((LRU_HIST_P, BRANCH_W), f32),
                   jax.ShapeDtypeStruct((DEC_BATCH, BRANCH_W), f32),
                   jax.ShapeDtypeStruct((LRU_HIST_S, BRANCH_W), f32)],
        scratch_shapes=[pltpu.VMEM((ROW_TILE + LRU_HIST_S, BRANCH_W), f32),
                        pltpu.VMEM((ROW_TILE, BRANCH_W), f32), pltpu.VMEM((ROW_TILE, BRANCH_W), f32),
                        pltpu.VMEM((BATCH, BRANCH_W), f32),
                        pltpu.VMEM((ROW_TILE, LRU_COLS), f32)],
        compiler_params=_cparams(("arbitrary",)),
        name="lru",
    )(xb, w_bf, *consts)


def _merge_kernel(yr_ref, ys_ref, yl_ref, xb_ref, wg_ref, x_ref, wb_ref, wo_ref, g_ref, b_ref, wr_ref, br_ref,
                  x1_ref, route_ref, route_t_ref, cnt_out_ref, cnt_ref):
    merged = jnp.zeros((MERGE_TILE, D_MODEL), f32)
    xb = xb_ref[...]
    for n, y_ref in enumerate((yr_ref, ys_ref, yl_ref)):
        proj = _dot(y_ref[...].astype(bf16), wb_ref[n])
        gate = jax.nn.sigmoid(_dot(xb, wg_ref[0, :, n * D_MODEL:(n + 1) * D_MODEL]))
        merged = merged + gate * proj
    mix = _dot(merged.astype(bf16), wo_ref[...])
    x1 = _layer_norm_rows(DN_ALPHA * x_ref[...] + mix, g_ref[...], b_ref[...])
    _to_planes(x1_ref, x1)

    logits = _dot(x1.astype(bf16), wr_ref[...]) + br_ref[...]
    lane = lax.broadcasted_iota(jnp.int32, (MERGE_TILE, ROUTE_LANES), 1).astype(f32)
    big = jnp.float32(ROUTE_LANES)
    neg = jnp.float32(-jnp.inf)
    is_g = lane < MOE_GROUPS
    lg = jnp.where(is_g, logits, neg)
    mg = jnp.max(lg, -1, keepdims=True)
    gsel = jnp.min(jnp.where(lg == mg, lane, big), -1, keepdims=True)
    sum_g = jnp.sum(jnp.where(is_g, jnp.exp(lg - mg), 0.0), -1, keepdims=True)
    pg_sel = 1.0 / sum_g
    lo = MOE_GROUPS + gsel * MOE_PER_GROUP
    is_e = jnp.abs(lane - lo - 0.5 * (MOE_PER_GROUP - 1)) < 0.5 * MOE_PER_GROUP
    le = jnp.where(is_e, logits, neg)
    me = jnp.max(le, -1, keepdims=True)
    ex = jnp.where(is_e, jnp.exp(le - me), 0.0)
    pe = jnp.where(is_e, ex / jnp.sum(ex, -1, keepdims=True), -1.0)
    v1 = jnp.max(pe, -1, keepdims=True)
    i1 = jnp.min(jnp.where(pe == v1, lane, big), -1, keepdims=True)
    pe2 = jnp.where(lane == i1, -1.0, pe)
    v2 = jnp.max(pe2, -1, keepdims=True)
    i2 = jnp.min(jnp.where(pe2 == v2, lane, big), -1, keepdims=True)
    vsum = v1 + v2
    w1 = pg_sel * v1 / vsum
    w2 = pg_sel * v2 / vsum
    e1 = i1 - MOE_GROUPS
    e2 = i2 - MOE_GROUPS

    @pl.when(pl.program_id(0) == 0)
    def _():
        cnt_ref[...] = jnp.zeros_like(cnt_ref)

    oh1 = lane == e1
    oh2 = lane == e2
    ohs = jnp.where(oh1, 1.0, jnp.where(oh2, 1.0, 0.0))
    r_i = lax.broadcasted_iota(jnp.int32, (MERGE_TILE, MERGE_TILE), 0)
    c_i = lax.broadcasted_iota(jnp.int32, (MERGE_TILE, MERGE_TILE), 1)
    strict_lower = jnp.where(c_i < r_i, 1.0, 0.0).astype(bf16)
    before = _dot(strict_lower, ohs.astype(bf16)) + cnt_ref[0:1, :]
    rank1 = jnp.sum(jnp.where(oh1, before, 0.0), -1, keepdims=True)
    rank2 = jnp.sum(jnp.where(oh2, before, 0.0), -1, keepdims=True)
    cnt_ref[0:1, :] = cnt_ref[0:1, :] + jnp.sum(ohs, 0, keepdims=True)
    cnt_out_ref[...] = cnt_ref[...]

    route = jnp.zeros((MERGE_TILE, ROUTE_LANES), f32)
    for k, val in enumerate((e1, e2, w1, w2, rank1, rank2)):
        route = jnp.where(lane == k, val, route)
    route_ref[...] = route
    route_t_ref[...] = route.T[0:V7X_SUBLANES, :]


def _merge(y_ret, y_ssm, y_lru, xb, w_bf, layer, x, wb, wo, g, b, wr, br):
    full = lambda a: _layer_spec(a, layer)
    row = lambda w: pl.BlockSpec((MERGE_TILE, w), lambda i: (i, 0))
    consts = (wb, wo, g, b, wr, br)
    return pl.pallas_call(
        _merge_kernel,
        grid=(N_TOK // MERGE_TILE,),
        in_specs=[row(BRANCH_W), row(BRANCH_W), row(BRANCH_W), row(D_MODEL),
                  _const_spec((1, D_MODEL, GATE_COLS), (layer, 0, GATE_BLK)),
                  row(D_MODEL)] + [full(a) for a in consts],
        out_specs=[pl.BlockSpec((N_PLANES, MERGE_TILE, V7X_LANES), lambda i: (0, i, 0)), row(ROUTE_LANES),
                   pl.BlockSpec((V7X_SUBLANES, MERGE_TILE), lambda i: (0, i)),
                   pl.BlockSpec((V7X_SUBLANES, ROUTE_LANES), lambda i: (0, 0))],
        out_shape=[jax.ShapeDtypeStruct((N_PLANES, N_TOK, V7X_LANES), f32),
                   jax.ShapeDtypeStruct((N_TOK, ROUTE_LANES), f32),
                   jax.ShapeDtypeStruct((V7X_SUBLANES, N_TOK), f32),
                   jax.ShapeDtypeStruct((V7X_SUBLANES, ROUTE_LANES), f32)],
        scratch_shapes=[pltpu.VMEM((V7X_SUBLANES, ROUTE_LANES), f32)],
        compiler_params=_cparams(("arbitrary",)),
        name="merge",
    )(y_ret, y_ssm, y_lru, xb, w_bf, x, *consts)


def _dispatch_kernel(pos0_ref, pos1_ref, x1_ref, xs_hbm, sem):
    base = pl.program_id(0) * DSP_TILE

    def row_copy(r, dst_row):
        return pltpu.make_async_copy(x1_ref.at[:, r, :], xs_hbm.at[dst_row], sem.at[0])

    def issue(r, c):
        n = base + r
        for k in range(MOE_TOPK):
            row_copy(r, (pos0_ref, pos1_ref)[k][n]).start(priority=k % 2)
        return c

    lax.fori_loop(0, DSP_TILE, issue, 0, unroll=8)
    for _ in range(MOE_TOPK):
        pltpu.make_async_copy(x1_ref, x1_ref, sem.at[0]).wait()


def _dispatch(pos, x1p):
    grid_spec = pltpu.PrefetchScalarGridSpec(
        num_scalar_prefetch=MOE_TOPK,
        grid=(N_TOK // DSP_TILE,),
        in_specs=[pl.BlockSpec((N_PLANES, DSP_TILE, V7X_LANES), lambda s, *_: (0, s, 0))],
        out_specs=pl.BlockSpec(memory_space=pl.ANY),
        scratch_shapes=[pltpu.SemaphoreType.DMA((1,))],
    )
    return pl.pallas_call(
        _dispatch_kernel,
        grid_spec=grid_spec,
        out_shape=jax.ShapeDtypeStruct((N_PAIRS, N_PLANES, V7X_LANES), f32),
        compiler_params=_cparams(("arbitrary",)),
        name="dispatch",
    )(*pos, x1p)


def _moe_kernel(layer, wt_ref, we_ref, wlo_ref, whi_ref, wfirst_ref, wlast_ref, wefirst_ref, weslot_ref,
                wenext_ref, nw_ref,
                xs_hbm, w1_hbm, w3_hbm, w2_hbm, ys_hbm,
                xin, yout, sem_in, sem_out, wst1, wst3, wst2, sem_w, w1b, w3b, w2b):
    w = pl.program_id(0)
    n_items = nw_ref[0]

    def weight_copies(expert, slot):
        return [pltpu.make_async_copy(src.at[layer, expert], dst.at[slot], sem_w.at[slot])
                for src, dst in ((w1_hbm, wst1), (w3_hbm, wst3), (w2_hbm, wst2))]

    def in_copy(item, slot, c):
        r0 = pl.multiple_of(wt_ref[item] * MOE_TM, MOE_TM)
        return pltpu.make_async_copy(xs_hbm.at[pl.ds(r0, MOE_TM), c, :], xin.at[slot, c], sem_in.at[slot])

    def out_copy(tile, slot, c):
        r0 = pl.multiple_of(tile * MOE_TM, MOE_TM)
        return pltpu.make_async_copy(yout.at[slot, c], ys_hbm.at[pl.ds(r0, MOE_TM), c, :], sem_out.at[slot])

    @pl.when(w == 0)
    def _():
        for cp in weight_copies(we_ref[0], 0):
            cp.start()
        for c in range(N_PLANES):
            in_copy(0, 0, c).start(priority=1)

    @pl.when(w + 1 < n_items)
    def _():
        for c in range(N_PLANES):
            in_copy(w + 1, (w + 1) % 2, c).start(priority=1)

    @pl.when(w < n_items)
    def _():
        slot = w % 2
        tile = wt_ref[w]
        oslot = tile % 2
        for c in range(N_PLANES):
            in_copy(w, slot, c).wait()

        @pl.when(wefirst_ref[w] == 1)
        def _():
            wslot = weslot_ref[w]
            for cp in weight_copies(we_ref[w], wslot):
                cp.wait()
            w1b[...] = wst1[wslot].astype(bf16)
            w3b[...] = wst3[wslot].astype(bf16)
            w2b[...] = wst2[wslot].astype(bf16)

            @pl.when(wenext_ref[w] >= 0)
            def _():
                for cp in weight_copies(wenext_ref[w], 1 - wslot):
                    cp.start()

        xt = _from_planes(xin.at[slot]).astype(bf16)
        h = jax.nn.silu(_dot(xt, w1b[...])) * _dot(xt, w3b[...])
        res = _dot(h.astype(bf16), w2b[...])
        row = lax.broadcasted_iota(jnp.int32, (MOE_TM, D_MODEL), 0)
        mine = jnp.where(row >= wlo_ref[w], row, MOE_TM) < whi_ref[w]

        @pl.when(wfirst_ref[w] == 1)
        def _():
            @pl.when(tile >= 2)
            def _():
                for c in range(N_PLANES):
                    out_copy(0, oslot, c).wait()

            _to_planes(yout.at[oslot], jnp.where(mine, res, 0.0))

        @pl.when(wfirst_ref[w] == 0)
        def _():
            _to_planes(yout.at[oslot], jnp.where(mine, res, _from_planes(yout.at[oslot])))

        @pl.when(wlast_ref[w] == 1)
        def _():
            for c in range(N_PLANES):
                out_copy(tile, oslot, c).start()

        @pl.when(w == n_items - 1)
        def _():
            for c in range(N_PLANES):
                out_copy(0, oslot, c).wait()

            @pl.when(tile >= 1)
            def _():
                for c in range(N_PLANES):
                    out_copy(0, 1 - oslot, c).wait()


def _moe(plan, xs, w1, w3, w2, layer):
    grid_spec = pltpu.PrefetchScalarGridSpec(
        num_scalar_prefetch=len(plan),
        grid=(MOE_MAX_ITEMS,),
        in_specs=[pl.BlockSpec(memory_space=pl.ANY)] * 4,
        out_specs=pl.BlockSpec(memory_space=pl.ANY),
        scratch_shapes=[pltpu.VMEM((2, N_PLANES, MOE_TM, V7X_LANES), f32),
                        pltpu.VMEM((2, N_PLANES, MOE_TM, V7X_LANES), f32),
                        pltpu.SemaphoreType.DMA((2,)), pltpu.SemaphoreType.DMA((2,)),
                        pltpu.VMEM((2, D_MODEL, MOE_HIDDEN), f32), pltpu.VMEM((2, D_MODEL, MOE_HIDDEN), f32),
                        pltpu.VMEM((2, MOE_HIDDEN, D_MODEL), f32), pltpu.SemaphoreType.DMA((2,)),
                        pltpu.VMEM((D_MODEL, MOE_HIDDEN), bf16), pltpu.VMEM((D_MODEL, MOE_HIDDEN), bf16),
                        pltpu.VMEM((MOE_HIDDEN, D_MODEL), bf16)],
    )
    return pl.pallas_call(
        functools.partial(_moe_kernel, layer),
        grid_spec=grid_spec,
        out_shape=jax.ShapeDtypeStruct((N_PAIRS, N_PLANES, V7X_LANES), f32),
        compiler_params=_cparams(("arbitrary",)),
        name="moe",
    )(*plan, xs, w1, w3, w2)


def _combine_kernel(final, pos0_ref, pos1_ref, ys_hbm, x1_ref, route_ref, g_ref, b_ref, *rest):
    if final:
        yp_ref, ysm_ref, buf, sem, t3_ref = rest
    else:
        o_ref, ob_ref, buf, sem = rest
    s = pl.program_id(0)
    nsteps = pl.num_programs(0)

    def gather_copy(row, slot, k, r):
        return pltpu.make_async_copy(ys_hbm.at[row], buf.at[slot, k, :, r, :], sem.at[slot])

    def issue(tile, slot):
        def body(r, c):
            n = tile * CMB_TILE + r
            for k in range(MOE_TOPK):
                gather_copy((pos0_ref, pos1_ref)[k][n], slot, k, r).start(priority=k % 2)
            return c

        lax.fori_loop(0, CMB_TILE, body, 0, unroll=8)

    @pl.when(s == 0)
    def _():
        issue(0, 0)

    @pl.when(s + 1 < nsteps)
    def _():
        issue(s + 1, (s + 1) % 2)

    slot = s % 2
    for k in range(MOE_TOPK):
        pltpu.make_async_copy(buf.at[slot, k], buf.at[slot, k], sem.at[slot]).wait()
    route = route_ref[...]
    moe = route[:, 2:3] * _from_planes(buf.at[slot, 0]) + route[:, 3:4] * _from_planes(buf.at[slot, 1])
    y = _layer_norm_rows(DN_ALPHA * _from_planes(x1_ref) + moe, g_ref[...], b_ref[...])
    if not final:
        o_ref[...] = y
        ob_ref[...] = y.astype(bf16)
        return

    @pl.when(s < CMB_P_STEPS)
    def _():
        t3_ref[...] = y.reshape(CMB_TILE // BATCH, BATCH, D_MODEL)
        for b in range(BATCH):
            yp_ref[b] = t3_ref[:, b, :]

    steps_per_tile = CMB_TILE // DEC_BATCH
    for q in range(N_S // CMB_TILE):
        @pl.when(s == CMB_P_STEPS + q)
        def _():
            for h in range(steps_per_tile):
                ysm_ref[:, q * steps_per_tile + h, :] = y[h * DEC_BATCH:(h + 1) * DEC_BATCH, :]


def _combine(pos, ys, x1, route, g, b, layer, final):
    if final:
        out_specs = [pl.BlockSpec((BATCH, CMB_TILE // BATCH, D_MODEL),
                                  lambda s, *_: (0, jnp.minimum(s, CMB_P_STEPS - 1), 0)),
                     pl.BlockSpec((DEC_BATCH, DEC_SEQ, D_MODEL), lambda s, *_: (0, 0, 0))]
        out_shape = [jax.ShapeDtypeStruct((BATCH, SEQ, D_MODEL), f32),
                     jax.ShapeDtypeStruct((DEC_BATCH, DEC_SEQ, D_MODEL), f32)]
        extra = [pltpu.VMEM((CMB_TILE // BATCH, BATCH, D_MODEL), f32)]
    else:
        out_specs = [pl.BlockSpec((CMB_TILE, D_MODEL), lambda s, *_: (s, 0))] * 2
        out_shape = [jax.ShapeDtypeStruct((N_TOK, D_MODEL), f32), jax.ShapeDtypeStruct((N_TOK, D_MODEL), bf16)]
        extra = []
    grid_spec = pltpu.PrefetchScalarGridSpec(
        num_scalar_prefetch=MOE_TOPK,
        grid=(N_TOK // CMB_TILE,),
        in_specs=[pl.BlockSpec(memory_space=pl.ANY),
                  pl.BlockSpec((N_PLANES, CMB_TILE, V7X_LANES), lambda s, *_: (0, s, 0)),
                  pl.BlockSpec((CMB_TILE, ROUTE_LANES), lambda s, *_: (s, 0)),
                  _layer_spec(g, layer), _layer_spec(b, layer)],
        out_specs=out_specs,
        scratch_shapes=[pltpu.VMEM((2, MOE_TOPK, N_PLANES, CMB_TILE, V7X_LANES), f32),
                        pltpu.SemaphoreType.DMA((2,))] + extra,
    )
    return pl.pallas_call(
        functools.partial(_combine_kernel, final),
        grid_spec=grid_spec,
        out_shape=out_shape,
        compiler_params=_cparams(("arbitrary",)),
        name="combine_out" if final else "combine",
    )(*pos, ys, x1, route, g, b)


def _to_rows_kernel(xp_ref, xs_ref, o_ref, ob_ref, t3_ref):
    i = pl.program_id(0)

    @pl.when(i < P_TILES)
    def _():
        for b in range(BATCH):
            t3_ref[:, b, :] = xp_ref[b]
        rows = t3_ref[...].reshape(ROW_TILE, D_MODEL)
        o_ref[...] = rows
        ob_ref[...] = rows.astype(bf16)

    @pl.when(i >= P_TILES)
    def _():
        for t in range(DEC_SEQ):
            rows = xs_ref[:, t, :]
            o_ref[t * DEC_BATCH:(t + 1) * DEC_BATCH, :] = rows
            ob_ref[t * DEC_BATCH:(t + 1) * DEC_BATCH, :] = rows.astype(bf16)


def _to_rows(x_prompt, x_sample):
    return pl.pallas_call(
        _to_rows_kernel,
        grid=(N_TILES,),
        in_specs=[pl.BlockSpec((BATCH, P_TC, D_MODEL), lambda i: (0, jnp.minimum(i, P_TILES - 1), 0)),
                  pl.BlockSpec((DEC_BATCH, DEC_SEQ, D_MODEL), lambda i: (0, 0, 0))],
        out_specs=[pl.BlockSpec((ROW_TILE, D_MODEL), lambda i: (i, 0))] * 2,
        out_shape=[jax.ShapeDtypeStruct((N_TOK, D_MODEL), f32), jax.ShapeDtypeStruct((N_TOK, D_MODEL), bf16)],
        scratch_shapes=[pltpu.VMEM((P_TC, BATCH, D_MODEL), f32)],
        compiler_params=_cparams(("arbitrary",)),
        name="to_rows",
    )(x_prompt, x_sample)


def _lookup(table, idx):
    ar = jnp.arange(MOE_EXPERTS, dtype=jnp.int32).reshape((MOE_EXPERTS,) + (1,) * idx.ndim)
    table = table.reshape(ar.shape)
    return jnp.sum(jnp.where(idx[None] == ar, table, 0), axis=0)


def _dispatch_plan(route_t, cnt):
    i32 = jnp.int32
    e = route_t[0:2].astype(i32)
    rank = route_t[4:6].astype(i32)
    counts = cnt[0, :MOE_EXPERTS].astype(i32)
    ends = jnp.cumsum(counts)
    starts = ends - counts
    pos = _lookup(starts, e) + rank

    first_tile = starts // MOE_TM
    last_tile = (ends - 1) // MOE_TM
    ntiles = jnp.where(counts > 0, last_tile - first_tile + 1, 0)
    item_end = jnp.cumsum(ntiles)
    n_items = item_end[-1]
    w = jnp.minimum(jnp.arange(MOE_MAX_ITEMS, dtype=i32), n_items - 1)
    we = jnp.sum((item_end[None, :] <= w[:, None]).astype(i32), axis=-1)
    ar = jnp.arange(MOE_EXPERTS, dtype=i32)
    later = (ar[None, :] > ar[:, None]) & (counts[None, :] > 0)
    next_expert = jnp.min(jnp.where(later, ar[None, :], MOE_EXPERTS), axis=1)
    next_expert = jnp.where(next_expert < MOE_EXPERTS, next_expert, -1)
    fields = jnp.stack([first_tile - (item_end - ntiles), starts, ends, next_expert])
    per_item = jnp.sum(jnp.where(we[None, None, :] == ar[None, :, None], fields[:, :, None], 0), axis=1)
    wt = per_item[0] + w
    wlo = jnp.maximum(per_item[1] - wt * MOE_TM, 0)
    whi = jnp.minimum(per_item[2] - wt * MOE_TM, MOE_TM)
    wenext = per_item[3]
    changes = (wt[1:] != wt[:-1]).astype(i32)
    wfirst = jnp.concatenate([jnp.ones((1,), i32), changes])
    wlast = jnp.concatenate([changes, jnp.ones((1,), i32)])
    wlast = jnp.where(jnp.arange(MOE_MAX_ITEMS, dtype=i32) == n_items - 1, 1, wlast)
    wefirst = jnp.concatenate([jnp.ones((1,), i32), (we[1:] != we[:-1]).astype(i32)])
    weslot = (jnp.cumsum(wefirst) - 1) % 2
    return (pos[0], pos[1]), (wt, we, wlo, whi, wfirst, wlast, wefirst, weslot, wenext, n_items.reshape(1))


def _block_diag(w):
    n, a, b = w.shape
    eye = jnp.eye(n, dtype=w.dtype)
    return (w[:, :, None, :] * eye[:, None, :, None]).reshape(n * a, n * b)


def _ssm_params(a_re, a_im, log_dt, b_re, b_im, c_re, c_im):
    ar, ai = a_re, a_im
    dt = jnp.exp(log_dt)[:, None]
    mag = jnp.exp(ar * dt)
    lb_re = mag * jnp.cos(ai * dt)
    lb_im = mag * jnp.sin(ai * dt)
    den = ar * ar + ai * ai
    nr = lb_re - 1.0
    coef_re = (nr * ar + lb_im * ai) / den
    coef_im = (lb_im * ar - nr * ai) / den
    bb_re = coef_re[..., None] * b_re - coef_im[..., None] * b_im
    bb_im = coef_re[..., None] * b_im + coef_im[..., None] * b_re
    gk = SSM_GROUPS // SSM_KB

    def diag_blocks(w):
        _, a, b = w.shape
        eye = jnp.eye(gk, dtype=w.dtype)
        w4 = w.reshape(SSM_KB, gk, a, b)
        return (w4[:, :, :, None, :] * eye[None, :, None, :, None]).reshape(SSM_KB, gk * a, gk * b).astype(bf16)

    bbre = diag_blocks(bb_re.transpose(0, 2, 1))
    bbim = diag_blocks(bb_im.transpose(0, 2, 1))
    ccre = diag_blocks(c_re.transpose(0, 2, 1))
    ccim = diag_blocks(c_im.transpose(0, 2, 1))
    return (lb_re.reshape(1, SSM_LANES), lb_im.reshape(1, SSM_LANES), bbre, bbim, ccre, ccim)


def kernel(x_prompt, x_sample, state_ret, state_ssm_re, state_ssm_im, state_lru, state_conv, w_in, ret_gn_g, ret_gn_b, ssm_a_re, ssm_a_im, ssm_log_dt, ssm_b_re, ssm_b_im, ssm_c_re, ssm_c_im, ssm_d, ssm_w_glu, lru_conv_w, lru_conv_b, lru_wa, lru_ba, lru_wx, lru_bx, lru_lambda, w_branch, w_out, ln1_g, ln1_b, moe_w_group, moe_b_group, moe_w_expert, moe_b_expert, moe_w1, moe_w3, moe_w2, ln2_g, ln2_b):
    x, xb = _to_rows(x_prompt, x_sample)
    rope = _rope_tables()
    tabs_p = _ret_tables(RET_SUB_T)
    tabs_s = _ret_tables(DEC_SEQ)
    rows = lambda v: v.reshape(DEPTH, 1, -1)
    w_bf = w_in.astype(bf16)
    s0 = state_ret.reshape(DEPTH, DEC_BATCH, 2, 2 * RET_DK, RET_DV)
    ssm_p = jax.vmap(_ssm_params)(ssm_a_re, ssm_a_im, ssm_log_dt, ssm_b_re, ssm_b_im, ssm_c_re, ssm_c_im)
    h0re = state_ssm_re.reshape(DEPTH, DEC_BATCH, SSM_LANES)
    h0im = state_ssm_im.reshape(DEPTH, DEC_BATCH, SSM_LANES)
    conv0 = state_conv.transpose(0, 2, 1, 3).reshape(DEPTH, LRU_HIST_S, BRANCH_W)
    lru_wa_bd = jax.vmap(_block_diag)(lru_wa).astype(bf16)
    lru_wx_bd = jax.vmap(_block_diag)(lru_wx).astype(bf16)
    pad_w = jnp.zeros((DEPTH, D_MODEL, ROUTE_LANES - MOE_GROUPS - MOE_EXPERTS), f32)
    wr = jnp.concatenate([moe_w_group, moe_w_expert, pad_w], axis=-1).astype(bf16)
    br = jnp.concatenate([moe_b_group, moe_b_expert, pad_w[:, 0, :]], axis=-1).reshape(DEPTH, 1, ROUTE_LANES)
    wb_bf, wo_bf, wglu_bf = w_branch.astype(bf16), w_out.astype(bf16), ssm_w_glu.astype(bf16)

    outs = [[] for _ in range(10)]
    for l in range(DEPTH):
        y_ret, ret_p, ret_s = _retention(xb, w_bf, rope, s0, l, tabs_p, tabs_s, rows(ret_gn_g), rows(ret_gn_b))
        y_ssm, re_p, im_p, re_s, im_s = _ssm(xb, w_bf, l, h0re, h0im, *ssm_p, rows(ssm_d), wglu_bf)
        y_lru, lru_p, conv_p, lru_s, conv_s = _lru(
            xb, w_bf, l, state_lru, conv0, lru_conv_w, rows(lru_conv_b),
            lru_wa_bd, rows(lru_ba), lru_wx_bd, rows(lru_bx), rows(lru_lambda))
        x1, route, route_t, cnt = _merge(y_ret, y_ssm, y_lru, xb, w_bf, l, x, wb_bf, wo_bf,
                                         rows(ln1_g), rows(ln1_b), wr, br)

        pos, plan = _dispatch_plan(route_t, cnt)
        xs = _dispatch(pos, x1)
        ys = _moe(plan, xs, moe_w1, moe_w3, moe_w2, l)
        x, xb = _combine(pos, ys, x1, route, rows(ln2_g), rows(ln2_b), l, final=(l == DEPTH - 1))

        outs[0].append(ret_p.reshape(BATCH, RET_HEADS, RET_DK, RET_DV))
        outs[1].append(re_p.reshape(BATCH, SSM_GROUPS, SSM_STATE))
        outs[2].append(im_p.reshape(BATCH, SSM_GROUPS, SSM_STATE))
        outs[3].append(lru_p)
        outs[4].append(conv_p.reshape(CONV_W - 1, BATCH, BRANCH_W).transpose(1, 0, 2))
        outs[5].append(ret_s.reshape(DEC_BATCH, RET_HEADS, RET_DK, RET_DV))
        outs[6].append(re_s.reshape(DEC_BATCH, SSM_GROUPS, SSM_STATE))
        outs[7].append(im_s.reshape(DEC_BATCH, SSM_GROUPS, SSM_STATE))
        outs[8].append(lru_s)
        outs[9].append(conv_s.reshape(CONV_W - 1, DEC_BATCH, BRANCH_W).transpose(1, 0, 2))

    y_prompt, y_sample = x, xb
    return (y_prompt, y_sample) + tuple(jnp.stack(o) for o in outs)
```

```python
import functools

import jax
import jax.numpy as jnp
import numpy as np
from jax import lax
from jax.experimental import pallas as pl
from jax.experimental.pallas import tpu as pltpu

f32 = jnp.float32
bf16 = jnp.bfloat16

D_MODEL = 1024
BATCH = 8
SEQ = 2048
DEPTH = 2
DEC_BATCH = 128
DEC_SEQ = 8
PAST_LEN = 16384
BRANCH_W = 512
RET_HEADS = 4
RET_DK = 64
RET_DV = 128
ROPE_BASE = 10000.0
SSM_GROUPS = 32
SSM_STATE = 64
SSM_LANES = SSM_GROUPS * SSM_STATE
CONV_W = 4
LRU_C = 8.0
MOE_GROUPS = 4
MOE_PER_GROUP = 8
MOE_EXPERTS = 32
MOE_TOPK = 2
MOE_HIDDEN = 512
DN_ALPHA = (2.0 * DEPTH) ** 0.25
LN_EPS = 1e-5

V7X_SUBLANES = 8
V7X_LANES = 128
V7X_VMEM_LIMIT = 56 * 1024 * 1024

N_P = BATCH * SEQ
N_S = DEC_BATCH * DEC_SEQ
N_TOK = N_P + N_S
ROW_TILE = 1024
P_TILES = N_P // ROW_TILE
N_TILES = N_TOK // ROW_TILE
P_TC = ROW_TILE // BATCH
RET_SUB_T = 32
RET_SUB_R = RET_SUB_T * BATCH
RET_SUBS = ROW_TILE // RET_SUB_R
S_BLOCKS = DEC_BATCH // BATCH
MERGE_TILE = 512
MOE_TM = 512
N_PAIRS = N_TOK * MOE_TOPK
MOE_MAX_ITEMS = N_PAIRS // MOE_TM + MOE_EXPERTS - 1
DSP_TILE = 1024
CMB_TILE = 256
CMB_P_STEPS = N_P // CMB_TILE
ROUTE_LANES = 128


PROJ_RC = 256
RET_COLS, RET_BLK = 1536, 0
SSM_COLS, SSM_BLK = 512, 3
LRU_COLS, LRU_BLK = 1024, 2
GATE_COLS, GATE_BLK = 3072, 1


def _cparams(sem):
    return pltpu.CompilerParams(dimension_semantics=sem, vmem_limit_bytes=V7X_VMEM_LIMIT)


N_PLANES = D_MODEL // V7X_LANES


def _to_planes(ref, rows):
    for c in range(N_PLANES):
        ref[c] = rows[:, c * V7X_LANES:(c + 1) * V7X_LANES]


def _from_planes(ref):
    return jnp.concatenate([ref[c] for c in range(N_PLANES)], axis=1)


def _layer_spec(stacked, layer):
    shape = (None,) + stacked.shape[1:]
    return pl.BlockSpec(shape, lambda *_: (layer,) + (0,) * (stacked.ndim - 1), pipeline_mode=pl.Buffered(1))


def _const_spec(block_shape, index):
    return pl.BlockSpec(block_shape, lambda *_: index, pipeline_mode=pl.Buffered(1))


def _dot(a, b):
    return jnp.dot(a, b, preferred_element_type=f32)


def _dot_nt(a, b):
    return lax.dot_general(a, b, (((1,), (1,)), ((), ())), preferred_element_type=f32)


def _dot_tn(a, b):
    return lax.dot_general(a, b, (((0,), (0,)), ((), ())), preferred_element_type=f32)


def _layer_norm_rows(x, g, b):
    mu = jnp.mean(x, -1, keepdims=True)
    xc = x - mu
    var = jnp.mean(xc * xc, -1, keepdims=True)
    return xc * lax.rsqrt(var + LN_EPS) * g + b


def _project_rows(xb_ref, w_ref, z_ref):
    for rc in range(ROW_TILE // PROJ_RC):
        rs = slice(rc * PROJ_RC, (rc + 1) * PROJ_RC)
        z_ref[rs, :] = _dot(xb_ref[rs, :], w_ref[0])


def _ret_block(q, k, v, g, cosb, sinb, mask_ref, qdec_ref, kdec_ref, cdec_ref, scat_ref, gng, gnb):
    rows = q.shape[0]
    lane_qk = lax.broadcasted_iota(jnp.int32, (rows, 2 * V7X_LANES), 1)
    first_half = (lane_qk & (RET_DK - 1)) < (RET_DK // 2)

    def rope(x):
        partner = jnp.where(first_half, pltpu.roll(x, 2 * V7X_LANES - RET_DK // 2, 1),
                            pltpu.roll(x, RET_DK // 2, 1))
        return x * cosb + partner * sinb

    q = rope(q)
    k = rope(k) * (RET_DK ** -0.5)
    kd = k * kdec_ref[...]
    lane = lax.broadcasted_iota(jnp.int32, (rows, V7X_LANES), 1)
    row_b = lax.broadcasted_iota(jnp.int32, (rows, V7X_LANES), 0) & (BATCH - 1)
    outs = []
    for p in range(2):
        qp = q[:, p * V7X_LANES:(p + 1) * V7X_LANES]
        kp = k[:, p * V7X_LANES:(p + 1) * V7X_LANES].astype(bf16)
        kdp = kd[:, p * V7X_LANES:(p + 1) * V7X_LANES]
        s_old = scat_ref[p]
        s_bf = s_old.astype(bf16)
        s_new = s_old * jnp.concatenate([cdec_ref[p]] * BATCH, axis=1)
        for hh in range(2):
            h = 2 * p + hh
            head_lanes = (lane >= RET_DK) if hh else (lane < RET_DK)
            qh = jnp.where(head_lanes, qp, 0.0).astype(bf16)
            kdh = jnp.where(head_lanes, kdp, 0.0).astype(bf16)
            vh = v[:, h * RET_DV:(h + 1) * RET_DV]
            vh_bf = vh.astype(bf16)
            sc = _dot_nt(qh, kp) * mask_ref[h]
            o = _dot(sc.astype(bf16), vh_bf)
            cross = _dot(qh, s_bf)
            oc = jnp.zeros((rows, RET_DV), f32)
            for b in range(BATCH):
                oc = oc + jnp.where(row_b == b, cross[:, b * RET_DV:(b + 1) * RET_DV], 0.0)
            o = o + oc * qdec_ref[h]
            vcat = jnp.concatenate([jnp.where(row_b == b, vh_bf, jnp.zeros_like(vh_bf))
                                    for b in range(BATCH)], axis=1)
            s_new = s_new + _dot_tn(kdh, vcat)
            mu = jnp.mean(o, -1, keepdims=True)
            oc2 = o - mu
            var = jnp.mean(oc2 * oc2, -1, keepdims=True)
            outs.append(oc2 * lax.rsqrt(var + LN_EPS))
        scat_ref[p] = s_new
    o = jnp.concatenate(outs, axis=1) * gng + gnb
    return jax.nn.silu(g) * o


def _per_step_rows(tab_ref, t0, steps):
    return jnp.concatenate(
        [jnp.broadcast_to(tab_ref[t0 + t:t0 + t + 1, :], (BATCH, tab_ref.shape[1])) for t in range(steps)], axis=0)


def _ret_kernel(xb_ref, w_ref, cos_ref, sin_ref, cos_s_ref, sin_s_ref, s0_ref,
                mask_p_ref, qdec_p_ref, kdec_p_ref, cdec_p_ref,
                mask_s_ref, qdec_s_ref, kdec_s_ref, cdec_s_ref,
                gng_ref, gnb_ref,
                y_ref, retp_ref, rets_ref, scat_ref, z_ref):
    i = pl.program_id(0)
    gng = gng_ref[...]
    gnb = gnb_ref[...]
    q_cols, k_cols = slice(0, 256), slice(256, 512)
    v_cols, g_cols = slice(512, 1024), slice(1024, 1536)

    @pl.when(i == 0)
    def _():
        scat_ref[...] = jnp.zeros_like(scat_ref)

    @pl.when(i <= P_TILES)
    def _():
        _project_rows(xb_ref, w_ref, z_ref)

    @pl.when(i < P_TILES)
    def _():
        for sc in range(RET_SUBS):
            r0 = sc * RET_SUB_R
            rs = slice(r0, r0 + RET_SUB_R)
            y_ref[rs, :] = _ret_block(
                z_ref[rs, q_cols], z_ref[rs, k_cols], z_ref[rs, v_cols], z_ref[rs, g_cols],
                _per_step_rows(cos_ref, sc * RET_SUB_T, RET_SUB_T), _per_step_rows(sin_ref, sc * RET_SUB_T, RET_SUB_T),
                mask_p_ref, qdec_p_ref, kdec_p_ref, cdec_p_ref, scat_ref, gng, gnb)

    @pl.when(i == P_TILES - 1)
    def _():
        for b in range(BATCH):
            for p in range(2):
                retp_ref[b, p] = scat_ref[p, :, b * RET_DV:(b + 1) * RET_DV]

    @pl.when(i >= P_TILES)
    def _():
        bb = i - P_TILES
        for b in range(BATCH):
            for p in range(2):
                scat_ref[p, :, b * RET_DV:(b + 1) * RET_DV] = s0_ref[0, b, p]

        def rows_of(cols):
            return jnp.concatenate(
                [z_ref[pl.ds(pl.multiple_of(t * DEC_BATCH + bb * BATCH, BATCH), BATCH), cols]
                 for t in range(DEC_SEQ)], axis=0)

        y = _ret_block(
            rows_of(q_cols), rows_of(k_cols), rows_of(v_cols),
            rows_of(g_cols), _per_step_rows(cos_s_ref, 0, DEC_SEQ), _per_step_rows(sin_s_ref, 0, DEC_SEQ),
            mask_s_ref, qdec_s_ref, kdec_s_ref, cdec_s_ref, scat_ref, gng, gnb)
        for t in range(DEC_SEQ):
            y_ref[pl.ds(pl.multiple_of(t * DEC_BATCH + bb * BATCH, BATCH), BATCH), :] = (
                y[t * BATCH:(t + 1) * BATCH, :])
        for b in range(BATCH):
            for p in range(2):
                rets_ref[b, p] = scat_ref[p, :, b * RET_DV:(b + 1) * RET_DV]


def _ret_tables(tc):
    rows = tc * BATCH
    nf = np.float32
    log_g = np.log1p(-np.exp2(nf(-5.0) - np.arange(RET_HEADS, dtype=nf))).astype(nf)
    t_idx = (np.arange(rows) // BATCH).astype(nf)
    b_idx = np.arange(rows) % BATCH
    rel = t_idx[:, None] - t_idx[None, :]
    same = b_idx[:, None] == b_idx[None, :]
    decay = np.exp(log_g[:, None, None] * np.maximum(rel, nf(0.0)))
    mask = np.where((rel >= 0) & same, decay, nf(0.0))
    qdec = np.exp(log_g[:, None] * (t_idx[None, :] + nf(1.0)))
    qdec = np.broadcast_to(qdec[:, :, None], (RET_HEADS, rows, RET_DV))
    kdec = np.exp(log_g[:, None] * (nf(tc - 1.0) - t_idx[None, :]))
    kdec = np.broadcast_to(kdec.T[:, :, None], (rows, RET_HEADS, RET_DK)).reshape(rows, RET_HEADS * RET_DK)
    cdec = np.exp(log_g * nf(tc))
    cdec = np.broadcast_to(cdec[:, None, None], (RET_HEADS, RET_DK, RET_DV)).reshape(2, 2 * RET_DK, RET_DV)
    return tuple(jnp.asarray(np.ascontiguousarray(a), dtype=f32) for a in (mask, qdec, kdec, cdec))


def _rope_tables():
    half = RET_DK // 2
    inv = ROPE_BASE ** (-jnp.arange(half, dtype=f32) / half)
    pos_p = jnp.arange(SEQ, dtype=f32)
    pos_s = PAST_LEN + jnp.arange(DEC_SEQ, dtype=f32)

    def tab(pos):
        ang = pos[:, None] * inv[None, :]
        cos = jnp.cos(ang)
        sin = jnp.sin(ang)
        cos_h = jnp.concatenate([cos, cos], axis=1)
        sin_h = jnp.concatenate([-sin, sin], axis=1)
        return jnp.tile(cos_h, (1, RET_HEADS)), jnp.tile(sin_h, (1, RET_HEADS))

    return tab(pos_p) + tab(pos_s)


def _retention(xb, w_bf, rope, s0_s, layer, tabs_p, tabs_s, gng, gnb):
    n_steps = P_TILES + S_BLOCKS
    tile = lambda i: jnp.minimum(i, P_TILES)
    sblk = lambda i: jnp.maximum(i - P_TILES, 0)
    full = lambda a: pl.BlockSpec(a.shape, lambda i, _n=a.ndim: (0,) * _n)
    state_blk = (BATCH, 2, 2 * RET_DK, RET_DV)
    cos_p, sin_p, cos_s, sin_s = rope
    ptile = lambda i: jnp.minimum(i, P_TILES - 1)
    ins = [xb, w_bf, cos_p, sin_p, cos_s, sin_s, s0_s, *tabs_p, *tabs_s, gng, gnb]
    in_specs = [
        pl.BlockSpec((ROW_TILE, D_MODEL), lambda i: (tile(i), 0)),
        _const_spec((1, D_MODEL, RET_COLS), (layer, 0, RET_BLK)),
        pl.BlockSpec((P_TC, 256), lambda i: (ptile(i), 0)),
        pl.BlockSpec((P_TC, 256), lambda i: (ptile(i), 0)),
        full(cos_s), full(sin_s),
        pl.BlockSpec((1,) + state_blk, lambda i: (layer, sblk(i), 0, 0, 0)),
    ] + [full(a) for a in (*tabs_p, *tabs_s)] + [_layer_spec(gng, layer), _layer_spec(gnb, layer)]
    return pl.pallas_call(
        _ret_kernel,
        grid=(n_steps,),
        in_specs=in_specs,
        out_specs=[pl.BlockSpec((ROW_TILE, BRANCH_W), lambda i: (tile(i), 0)),
                   pl.BlockSpec(state_blk, lambda i: (0, 0, 0, 0)),
                   pl.BlockSpec(state_blk, lambda i: (sblk(i), 0, 0, 0))],
        out_shape=[jax.ShapeDtypeStruct((N_TOK, BRANCH_W), f32),
                   jax.ShapeDtypeStruct((BATCH, 2, 2 * RET_DK, RET_DV), f32),
                   jax.ShapeDtypeStruct((DEC_BATCH, 2, 2 * RET_DK, RET_DV), f32)],
        scratch_shapes=[pltpu.VMEM((2, 2 * RET_DK, BATCH * RET_DV), f32),
                        pltpu.VMEM((ROW_TILE, RET_COLS), f32)],
        compiler_params=_cparams(("arbitrary",)),
        name="retention",
    )(*ins)


SSM_LB = 512
SSM_RC = 256
SSM_KB = 2
SSM_KB_U = BRANCH_W // SSM_KB
SSM_KB_H = SSM_LANES // SSM_KB


def _ssm_scan(bre_ref, bim_ref, lre_ref, lim_ref, h_re0, h_im0, row0, nb_rows, steps, lb, unroll):
    ls = slice(lb * SSM_LB, (lb + 1) * SSM_LB)
    a_re = jnp.broadcast_to(lre_ref[:, ls], (V7X_SUBLANES, SSM_LB))
    a_im = jnp.broadcast_to(lim_ref[:, ls], (V7X_SUBLANES, SSM_LB))

    def step(t, carry):
        h_re, h_im = carry
        r = pl.multiple_of(row0 + t * nb_rows, V7X_SUBLANES)
        n_re = a_re * h_re - a_im * h_im + bre_ref[pl.ds(r, V7X_SUBLANES), ls]
        n_im = a_re * h_im + a_im * h_re + bim_ref[pl.ds(r, V7X_SUBLANES), ls]
        bre_ref[pl.ds(r, V7X_SUBLANES), ls] = n_re
        bim_ref[pl.ds(r, V7X_SUBLANES), ls] = n_im
        return n_re, n_im

    return lax.fori_loop(0, steps, step, (h_re0, h_im0), unroll=unroll)


def _ssm_kernel(xb_ref, w_ref, h0re_ref, h0im_ref, lre_ref, lim_ref, bbre_ref, bbim_ref, ccre_ref, ccim_ref,
                d_ref, wglu_ref,
                y_ref, pre_ref, pim_ref, sre_ref, sim_ref,
                bre_ref, bim_ref, hre_ref, him_ref, zs_ref):
    i = pl.program_id(0)
    _project_rows(xb_ref, w_ref, zs_ref)
    for rc in range(ROW_TILE // SSM_RC):
        rs = slice(rc * SSM_RC, (rc + 1) * SSM_RC)
        ub = zs_ref[rs, :].astype(bf16)
        for k in range(SSM_KB):
            uk = ub[:, k * SSM_KB_U:(k + 1) * SSM_KB_U]
            hs = slice(k * SSM_KB_H, (k + 1) * SSM_KB_H)
            bre_ref[rs, hs] = _dot(uk, bbre_ref[k])
            bim_ref[rs, hs] = _dot(uk, bbim_ref[k])

    @pl.when(i == 0)
    def _():
        hre_ref[...] = jnp.zeros_like(hre_ref)
        him_ref[...] = jnp.zeros_like(him_ref)

    @pl.when(i < P_TILES)
    def _():
        for lb in range(SSM_LANES // SSM_LB):
            ls = slice(lb * SSM_LB, (lb + 1) * SSM_LB)
            h_re, h_im = _ssm_scan(bre_ref, bim_ref, lre_ref, lim_ref, hre_ref[:, ls], him_ref[:, ls],
                                   0, BATCH, P_TC, lb, 8)
            hre_ref[:, ls] = h_re
            him_ref[:, ls] = h_im
        pre_ref[...] = hre_ref[...]
        pim_ref[...] = him_ref[...]

    @pl.when(i >= P_TILES)
    def _():
        def per_row_tile(rt, c):
            r0 = pl.multiple_of(rt * V7X_SUBLANES, V7X_SUBLANES)
            for lb in range(SSM_LANES // SSM_LB):
                ls = slice(lb * SSM_LB, (lb + 1) * SSM_LB)
                _ssm_scan(bre_ref, bim_ref, lre_ref, lim_ref,
                          h0re_ref[pl.ds(r0, V7X_SUBLANES), ls], h0im_ref[pl.ds(r0, V7X_SUBLANES), ls],
                          r0, DEC_BATCH, DEC_SEQ, lb, True)
            return c

        lax.fori_loop(0, DEC_BATCH // V7X_SUBLANES, per_row_tile, 0)
        last = (DEC_SEQ - 1) * DEC_BATCH
        sre_ref[...] = bre_ref[last:last + DEC_BATCH, :]
        sim_ref[...] = bim_ref[last:last + DEC_BATCH, :]

    for rc in range(ROW_TILE // SSM_RC):
        rs = slice(rc * SSM_RC, (rc + 1) * SSM_RC)
        ch = []
        for k in range(SSM_KB):
            hs = slice(k * SSM_KB_H, (k + 1) * SSM_KB_H)
            ch.append(_dot(bre_ref[rs, hs].astype(bf16), ccre_ref[k]) - _dot(bim_ref[rs, hs].astype(bf16), ccim_ref[k]))
        y = jnp.concatenate(ch, axis=1) + d_ref[...] * zs_ref[rs, :]
        zz = jax.nn.gelu(y)
        y_ref[rs, :] = zz * jax.nn.sigmoid(_dot(zz.astype(bf16), wglu_ref[...]))


def _ssm(xb, w_bf, layer, h0re, h0im, lre, lim, bbre, bbim, ccre, ccim, dvec, wglu):
    full = lambda a: _layer_spec(a, layer)
    consts = (h0re, h0im, lre, lim, bbre, bbim, ccre, ccim, dvec, wglu)
    return pl.pallas_call(
        _ssm_kernel,
        grid=(N_TILES,),
        in_specs=[pl.BlockSpec((ROW_TILE, D_MODEL), lambda i: (i, 0)),
                  _const_spec((1, D_MODEL, SSM_COLS), (layer, 0, SSM_BLK))] + [full(a) for a in consts],
        out_specs=[pl.BlockSpec((ROW_TILE, BRANCH_W), lambda i: (i, 0)),
                   pl.BlockSpec((BATCH, SSM_LANES), lambda i: (0, 0)),
                   pl.BlockSpec((BATCH, SSM_LANES), lambda i: (0, 0)),
                   pl.BlockSpec((DEC_BATCH, SSM_LANES), lambda i: (0, 0)),
                   pl.BlockSpec((DEC_BATCH, SSM_LANES), lambda i: (0, 0))],
        out_shape=[jax.ShapeDtypeStruct((N_TOK, BRANCH_W), f32),
                   jax.ShapeDtypeStruct((BATCH, SSM_LANES), f32),
                   jax.ShapeDtypeStruct((BATCH, SSM_LANES), f32),
                   jax.ShapeDtypeStruct((DEC_BATCH, SSM_LANES), f32),
                   jax.ShapeDtypeStruct((DEC_BATCH, SSM_LANES), f32)],
        scratch_shapes=[pltpu.VMEM((ROW_TILE, SSM_LANES), f32), pltpu.VMEM((ROW_TILE, SSM_LANES), f32),
                        pltpu.VMEM((BATCH, SSM_LANES), f32), pltpu.VMEM((BATCH, SSM_LANES), f32),
                        pltpu.VMEM((ROW_TILE, SSM_COLS), f32)],
        compiler_params=_cparams(("arbitrary",)),
        name="ssm",
    )(xb, w_bf, *consts)


LRU_HIST_P = (CONV_W - 1) * BATCH
LRU_HIST_S = (CONV_W - 1) * DEC_BATCH


def _lru_gates(xe_ref, nb_rows, cw_ref, cb_ref, wa_ref, ba_ref, wx_ref, bx_ref, lam_ref):
    xc = cb_ref[...] + xe_ref[0:ROW_TILE, :] * cw_ref[0:1, :]
    for j in range(1, CONV_W):
        xc = xc + xe_ref[j * nb_rows:j * nb_rows + ROW_TILE, :] * cw_ref[j:j + 1, :]
    xcb = xc.astype(bf16)
    r = jax.nn.sigmoid(_dot(xcb, wa_ref[...]) + ba_ref[...])
    ig = jax.nn.sigmoid(_dot(xcb, wx_ref[...]) + bx_ref[...])
    log_a = -LRU_C * r * jax.nn.softplus(-lam_ref[...])
    a = jnp.exp(log_a)
    b = jnp.sqrt(-jnp.tanh(log_a) * (a * a + 1.0)) * (ig * xc)
    return a, b


def _lru_kernel(xb_ref, w_ref, h0_ref, conv0_ref, cw_ref, cb_ref, wa_ref, ba_ref, wx_ref, bx_ref, lam_ref,
                y_ref, hp_ref, convp_ref, hs_ref, convs_ref,
                xe_ref, a_ref, b_ref, hc_ref, z_ref):
    i = pl.program_id(0)
    params = (cw_ref, cb_ref, wa_ref, ba_ref, wx_ref, bx_ref, lam_ref)
    _project_rows(xb_ref, w_ref, z_ref)
    zx_ref = z_ref.at[:, 0:BRANCH_W]
    zg_ref = z_ref.at[:, BRANCH_W:2 * BRANCH_W]

    @pl.when(i == 0)
    def _():
        xe_ref[0:LRU_HIST_P, :] = jnp.zeros((LRU_HIST_P, BRANCH_W), f32)
        hc_ref[...] = jnp.zeros_like(hc_ref)

    @pl.when(i < P_TILES)
    def _():
        xe_ref[LRU_HIST_P:LRU_HIST_P + ROW_TILE, :] = zx_ref[...]
        a, b = _lru_gates(xe_ref, BATCH, *params)
        a_ref[...] = a
        b_ref[...] = b
        hist = xe_ref[ROW_TILE:ROW_TILE + LRU_HIST_P, :]
        xe_ref[0:LRU_HIST_P, :] = hist
        convp_ref[...] = hist

        def step(t, h):
            r = pl.multiple_of(t * BATCH, BATCH)
            h = a_ref[pl.ds(r, BATCH), :] * h + b_ref[pl.ds(r, BATCH), :]
            b_ref[pl.ds(r, BATCH), :] = h
            return h

        h = lax.fori_loop(0, P_TC, step, hc_ref[...], unroll=8)
        hc_ref[...] = h
        hp_ref[...] = h

    @pl.when(i >= P_TILES)
    def _():
        xe_ref[0:LRU_HIST_S, :] = conv0_ref[...]
        xe_ref[LRU_HIST_S:LRU_HIST_S + ROW_TILE, :] = zx_ref[...]
        a, b = _lru_gates(xe_ref, DEC_BATCH, *params)
        a_ref[...] = a
        b_ref[...] = b
        convs_ref[...] = xe_ref[ROW_TILE:ROW_TILE + LRU_HIST_S, :]

        def per_row_tile(rt, c):
            r0 = pl.multiple_of(rt * V7X_SUBLANES, V7X_SUBLANES)
            h = h0_ref[pl.ds(r0, V7X_SUBLANES), :]
            for t in range(DEC_SEQ):
                r = pl.multiple_of(t * DEC_BATCH + r0, V7X_SUBLANES)
                h = a_ref[pl.ds(r, V7X_SUBLANES), :] * h + b_ref[pl.ds(r, V7X_SUBLANES), :]
                b_ref[pl.ds(r, V7X_SUBLANES), :] = h
            return c

        lax.fori_loop(0, DEC_BATCH // V7X_SUBLANES, per_row_tile, 0)
        last = (DEC_SEQ - 1) * DEC_BATCH
        hs_ref[...] = b_ref[last:last + DEC_BATCH, :]

    y_ref[...] = b_ref[...] * jax.nn.gelu(zg_ref[...])


def _lru(xb, w_bf, layer, h0, conv0, cw, cb, wa, ba, wx, bx, lam):
    full = lambda a: _layer_spec(a, layer)
    consts = (h0, conv0, cw, cb, wa, ba, wx, bx, lam)
    return pl.pallas_call(
        _lru_kernel,
        grid=(N_TILES,),
        in_specs=[pl.BlockSpec((ROW_TILE, D_MODEL), lambda i: (i, 0)),
                  _const_spec((1, D_MODEL, LRU_COLS), (layer, 0, LRU_BLK))] + [full(a) for a in consts],
        out_specs=[pl.BlockSpec((ROW_TILE, BRANCH_W), lambda i: (i, 0)),
                   pl.BlockSpec((BATCH, BRANCH_W), lambda i: (0, 0)),
                   pl.BlockSpec((LRU_HIST_P, BRANCH_W), lambda i: (0, 0)),
                   pl.BlockSpec((DEC_BATCH, BRANCH_W), lambda i: (0, 0)),
                   pl.BlockSpec((LRU_HIST_S, BRANCH_W), lambda i: (0, 0))],
        out_shape=[jax.ShapeDtypeStruct((N_TOK, BRANCH_W), f32),
                   jax.ShapeDtypeStruct((BATCH, BRANCH_W), f32),
                   jax.ShapeDtypeStruct((LRU_HIST_P, BRANCH_W), f32),
                   jax.ShapeDtypeStruct((DEC_BATCH, BRANCH_W), f32),
                   jax.ShapeDtypeStruct((LRU_HIST_S, BRANCH_W), f32)],
        scratch_shapes=[pltpu.VMEM((ROW_TILE + LRU_HIST_S, BRANCH_W), f32),
                        pltpu.VMEM((ROW_TILE, BRANCH_W), f32), pltpu.VMEM((ROW_TILE, BRANCH_W), f32),
                        pltpu.VMEM((BATCH, BRANCH_W), f32),
                        pltpu.VMEM((ROW_TILE, LRU_COLS), f32)],
        compiler_params=_cparams(("arbitrary",)),
        name="lru",
    )(xb, w_bf, *consts)


def _merge_kernel(yr_ref, ys_ref, yl_ref, xb_ref, wg_ref, x_ref, wb_ref, wo_ref, g_ref, b_ref, wr_ref, br_ref,
                  x1_ref, route_ref, route_t_ref, cnt_out_ref, cnt_ref):
    merged = jnp.zeros((MERGE_TILE, D_MODEL), f32)
    xb = xb_ref[...]
    for n, y_ref in enumerate((yr_ref, ys_ref, yl_ref)):
        proj = _dot(y_ref[...].astype(bf16), wb_ref[n])
        gate = jax.nn.sigmoid(_dot(xb, wg_ref[0, :, n * D_MODEL:(n + 1) * D_MODEL]))
        merged = merged + gate * proj
    mix = _dot(merged.astype(bf16), wo_ref[...])
    x1 = _layer_norm_rows(DN_ALPHA * x_ref[...] + mix, g_ref[...], b_ref[...])
    _to_planes(x1_ref, x1)

    logits = _dot(x1.astype(bf16), wr_ref[...]) + br_ref[...]
    lane = lax.broadcasted_iota(jnp.int32, (MERGE_TILE, ROUTE_LANES), 1).astype(f32)
    big = jnp.float32(ROUTE_LANES)
    neg = jnp.float32(-jnp.inf)
    is_g = lane < MOE_GROUPS
    lg = jnp.where(is_g, logits, neg)
    mg = jnp.max(lg, -1, keepdims=True)
    gsel = jnp.min(jnp.where(lg == mg, lane, big), -1, keepdims=True)
    sum_g = jnp.sum(jnp.where(is_g, jnp.exp(lg - mg), 0.0), -1, keepdims=True)
    pg_sel = 1.0 / sum_g
    lo = MOE_GROUPS + gsel * MOE_PER_GROUP
    is_e = jnp.abs(lane - lo - 0.5 * (MOE_PER_GROUP - 1)) < 0.5 * MOE_PER_GROUP
    le = jnp.where(is_e, logits, neg)
    me = jnp.max(le, -1, keepdims=True)
    ex = jnp.where(is_e, jnp.exp(le - me), 0.0)
    pe = jnp.where(is_e, ex / jnp.sum(ex, -1, keepdims=True), -1.0)
    v1 = jnp.max(pe, -1, keepdims=True)
    i1 = jnp.min(jnp.where(pe == v1, lane, big), -1, keepdims=True)
    pe2 = jnp.where(lane == i1, -1.0, pe)
    v2 = jnp.max(pe2, -1, keepdims=True)
    i2 = jnp.min(jnp.where(pe2 == v2, lane, big), -1, keepdims=True)
    vsum = v1 + v2
    w1 = pg_sel * v1 / vsum
    w2 = pg_sel * v2 / vsum
    e1 = i1 - MOE_GROUPS
    e2 = i2 - MOE_GROUPS

    @pl.when(pl.program_id(0) == 0)
    def _():
        cnt_ref[...] = jnp.zeros_like(cnt_ref)

    oh1 = lane == e1
    oh2 = lane == e2
    ohs = jnp.where(oh1, 1.0, jnp.where(oh2, 1.0, 0.0))
    r_i = lax.broadcasted_iota(jnp.int32, (MERGE_TILE, MERGE_TILE), 0)
    c_i = lax.broadcasted_iota(jnp.int32, (MERGE_TILE, MERGE_TILE), 1)
    strict_lower = jnp.where(c_i < r_i, 1.0, 0.0).astype(bf16)
    before = _dot(strict_lower, ohs.astype(bf16)) + cnt_ref[0:1, :]
    rank1 = jnp.sum(jnp.where(oh1, before, 0.0), -1, keepdims=True)
    rank2 = jnp.sum(jnp.where(oh2, before, 0.0), -1, keepdims=True)
    cnt_ref[0:1, :] = cnt_ref[0:1, :] + jnp.sum(ohs, 0, keepdims=True)
    cnt_out_ref[...] = cnt_ref[...]

    route = jnp.zeros((MERGE_TILE, ROUTE_LANES), f32)
    for k, val in enumerate((e1, e2, w1, w2, rank1, rank2)):
        route = jnp.where(lane == k, val, route)
    route_ref[...] = route
    route_t_ref[...] = route.T[0:V7X_SUBLANES, :]


def _merge(y_ret, y_ssm, y_lru, xb, w_bf, layer, x, wb, wo, g, b, wr, br):
    full = lambda a: _layer_spec(a, layer)
    row = lambda w: pl.BlockSpec((MERGE_TILE, w), lambda i: (i, 0))
    consts = (wb, wo, g, b, wr, br)
    return pl.pallas_call(
        _merge_kernel,
        grid=(N_TOK // MERGE_TILE,),
        in_specs=[row(BRANCH_W), row(BRANCH_W), row(BRANCH_W), row(D_MODEL),
                  _const_spec((1, D_MODEL, GATE_COLS), (layer, 0, GATE_BLK)),
                  row(D_MODEL)] + [full(a) for a in consts],
        out_specs=[pl.BlockSpec((N_PLANES, MERGE_TILE, V7X_LANES), lambda i: (0, i, 0)), row(ROUTE_LANES),
                   pl.BlockSpec((V7X_SUBLANES, MERGE_TILE), lambda i: (0, i)),
                   pl.BlockSpec((V7X_SUBLANES, ROUTE_LANES), lambda i: (0, 0))],
        out_shape=[jax.ShapeDtypeStruct((N_PLANES, N_TOK, V7X_LANES), f32),
                   jax.ShapeDtypeStruct((N_TOK, ROUTE_LANES), f32),
                   jax.ShapeDtypeStruct((V7X_SUBLANES, N_TOK), f32),
                   jax.ShapeDtypeStruct((V7X_SUBLANES, ROUTE_LANES), f32)],
        scratch_shapes=[pltpu.VMEM((V7X_SUBLANES, ROUTE_LANES), f32)],
        compiler_params=_cparams(("arbitrary",)),
        name="merge",
    )(y_ret, y_ssm, y_lru, xb, w_bf, x, *consts)


def _dispatch_kernel(pos0_ref, pos1_ref, x1_ref, xs_hbm, sem):
    base = pl.program_id(0) * DSP_TILE

    def row_copy(r, dst_row):
        return pltpu.make_async_copy(x1_ref.at[:, r, :], xs_hbm.at[dst_row], sem.at[0])

    def issue(r, c):
        n = base + r
        for k in range(MOE_TOPK):
            row_copy(r, (pos0_ref, pos1_ref)[k][n]).start(priority=k % 2)
        return c

    lax.fori_loop(0, DSP_TILE, issue, 0, unroll=8)
    for _ in range(MOE_TOPK):
        pltpu.make_async_copy(x1_ref, x1_ref, sem.at[0]).wait()


def _dispatch(pos, x1p):
    grid_spec = pltpu.PrefetchScalarGridSpec(
        num_scalar_prefetch=MOE_TOPK,
        grid=(N_TOK // DSP_TILE,),
        in_specs=[pl.BlockSpec((N_PLANES, DSP_TILE, V7X_LANES), lambda s, *_: (0, s, 0))],
        out_specs=pl.BlockSpec(memory_space=pl.ANY),
        scratch_shapes=[pltpu.SemaphoreType.DMA((1,))],
    )
    return pl.pallas_call(
        _dispatch_kernel,
        grid_spec=grid_spec,
        out_shape=jax.ShapeDtypeStruct((N_PAIRS, N_PLANES, V7X_LANES), f32),
        compiler_params=_cparams(("arbitrary",)),
        name="dispatch",
    )(*pos, x1p)


def _moe_kernel(layer, wt_ref, we_ref, wlo_ref, whi_ref, wfirst_ref, wlast_ref, wefirst_ref, weslot_ref,
                wenext_ref, nw_ref,
                xs_hbm, w1_hbm, w3_hbm, w2_hbm, ys_hbm,
                xin, yout, sem_in, sem_out, wst1, wst3, wst2, sem_w, w1b, w3b, w2b):
    w = pl.program_id(0)
    n_items = nw_ref[0]

    def weight_copies(expert, slot):
        return [pltpu.make_async_copy(src.at[layer, expert], dst.at[slot], sem_w.at[slot])
                for src, dst in ((w1_hbm, wst1), (w3_hbm, wst3), (w2_hbm, wst2))]

    def in_copy(item, slot, c):
        r0 = pl.multiple_of(wt_ref[item] * MOE_TM, MOE_TM)
        return pltpu.make_async_copy(xs_hbm.at[pl.ds(r0, MOE_TM), c, :], xin.at[slot, c], sem_in.at[slot])

    def out_copy(tile, slot, c):
        r0 = pl.multiple_of(tile * MOE_TM, MOE_TM)
        return pltpu.make_async_copy(yout.at[slot, c], ys_hbm.at[pl.ds(r0, MOE_TM), c, :], sem_out.at[slot])

    @pl.when(w == 0)
    def _():
        for cp in weight_copies(we_ref[0], 0):
            cp.start()
        for c in range(N_PLANES):
            in_copy(0, 0, c).start(priority=1)

    @pl.when(w + 1 < n_items)
    def _():
        for c in range(N_PLANES):
            in_copy(w + 1, (w + 1) % 2, c).start(priority=1)

    @pl.when(w < n_items)
    def _():
        slot = w % 2
        tile = wt_ref[w]
        oslot = tile % 2
        for c in range(N_PLANES):
            in_copy(w, slot, c).wait()

        @pl.when(wefirst_ref[w] == 1)
        def _():
            wslot = weslot_ref[w]
            for cp in weight_copies(we_ref[w], wslot):
                cp.wait()
            w1b[...] = wst1[wslot].astype(bf16)
            w3b[...] = wst3[wslot].astype(bf16)
            w2b[...] = wst2[wslot].astype(bf16)

            @pl.when(wenext_ref[w] >= 0)
            def _():
                for cp in weight_copies(wenext_ref[w], 1 - wslot):
                    cp.start()

        xt = _from_planes(xin.at[slot]).astype(bf16)
        h = jax.nn.silu(_dot(xt, w1b[...])) * _dot(xt, w3b[...])
        res = _dot(h.astype(bf16), w2b[...])
        row = lax.broadcasted_iota(jnp.int32, (MOE_TM, D_MODEL), 0)
        mine = jnp.where(row >= wlo_ref[w], row, MOE_TM) < whi_ref[w]

        @pl.when(wfirst_ref[w] == 1)
        def _():
            @pl.when(tile >= 2)
            def _():
                for c in range(N_PLANES):
                    out_copy(0, oslot, c).wait()

            _to_planes(yout.at[oslot], jnp.where(mine, res, 0.0))

        @pl.when(wfirst_ref[w] == 0)
        def _():
            _to_planes(yout.at[oslot], jnp.where(mine, res, _from_planes(yout.at[oslot])))

        @pl.when(wlast_ref[w] == 1)
        def _():
            for c in range(N_PLANES):
                out_copy(tile, oslot, c).start()

        @pl.when(w == n_items - 1)
        def _():
            for c in range(N_PLANES):
                out_copy(0, oslot, c).wait()

            @pl.when(tile >= 1)
            def _():
                for c in range(N_PLANES):
                    out_copy(0, 1 - oslot, c).wait()


def _moe(plan, xs, w1, w3, w2, layer):
    grid_spec = pltpu.PrefetchScalarGridSpec(
        num_scalar_prefetch=len(plan),
        grid=(MOE_MAX_ITEMS,),
        in_specs=[pl.BlockSpec(memory_space=pl.ANY)] * 4,
        out_specs=pl.BlockSpec(memory_space=pl.ANY),
        scratch_shapes=[pltpu.VMEM((2, N_PLANES, MOE_TM, V7X_LANES), f32),
                        pltpu.VMEM((2, N_PLANES, MOE_TM, V7X_LANES), f32),
                        pltpu.SemaphoreType.DMA((2,)), pltpu.SemaphoreType.DMA((2,)),
                        pltpu.VMEM((2, D_MODEL, MOE_HIDDEN), f32), pltpu.VMEM((2, D_MODEL, MOE_HIDDEN), f32),
                        pltpu.VMEM((2, MOE_HIDDEN, D_MODEL), f32), pltpu.SemaphoreType.DMA((2,)),
                        pltpu.VMEM((D_MODEL, MOE_HIDDEN), bf16), pltpu.VMEM((D_MODEL, MOE_HIDDEN), bf16),
                        pltpu.VMEM((MOE_HIDDEN, D_MODEL), bf16)],
    )
    return pl.pallas_call(
        functools.partial(_moe_kernel, layer),
        grid_spec=grid_spec,
        out_shape=jax.ShapeDtypeStruct((N_PAIRS, N_PLANES, V7X_LANES), f32),
        compiler_params=_cparams(("arbitrary",)),
        name="moe",
    )(*plan, xs, w1, w3, w2)


def _combine_kernel(final, pos0_ref, pos1_ref, ys_hbm, x1_ref, route_ref, g_ref, b_ref, *rest):
    if final:
        yp_hbm, ysm_ref, buf, sem, stage, sem_out = rest
    else:
        o_ref, ob_ref, buf, sem = rest
    s = pl.program_id(0)
    nsteps = pl.num_programs(0)

    def gather_copy(row, slot, k, r):
        return pltpu.make_async_copy(ys_hbm.at[row], buf.at[slot, k, :, r, :], sem.at[slot])

    def issue(tile, slot):
        def body(r, c):
            n = tile * CMB_TILE + r
            for k in range(MOE_TOPK):
                gather_copy((pos0_ref, pos1_ref)[k][n], slot, k, r).start(priority=k % 2)
            return c

        lax.fori_loop(0, CMB_TILE, body, 0, unroll=8)

    @pl.when(s == 0)
    def _():
        issue(0, 0)

    @pl.when(s + 1 < nsteps)
    def _():
        issue(s + 1, (s + 1) % 2)

    slot = s % 2
    for k in range(MOE_TOPK):
        pltpu.make_async_copy(buf.at[slot, k], buf.at[slot, k], sem.at[slot]).wait()
    route = route_ref[...]
    moe = route[:, 2:3] * _from_planes(buf.at[slot, 0]) + route[:, 3:4] * _from_planes(buf.at[slot, 1])
    y = _layer_norm_rows(DN_ALPHA * _from_planes(x1_ref) + moe, g_ref[...], b_ref[...])
    if not final:
        o_ref[...] = y
        ob_ref[...] = y.astype(bf16)
        return

    steps_t = CMB_TILE // BATCH

    def out_copy(step, oslot, b):
        t0 = pl.multiple_of(step * steps_t, steps_t)
        return pltpu.make_async_copy(stage.at[oslot, :, b, :], yp_hbm.at[b, pl.ds(t0, steps_t), :], sem_out.at[oslot])

    @pl.when(s < CMB_P_STEPS)
    def _():
        oslot = s % 2

        @pl.when(s >= 2)
        def _():
            for b in range(BATCH):
                out_copy(0, oslot, b).wait()

        stage[oslot] = y.reshape(steps_t, BATCH, D_MODEL)
        for b in range(BATCH):
            out_copy(s, oslot, b).start()

        @pl.when(s == CMB_P_STEPS - 1)
        def _():
            for b in range(BATCH):
                out_copy(0, oslot, b).wait()
                out_copy(0, 1 - oslot, b).wait()

    steps_per_tile = CMB_TILE // DEC_BATCH
    for q in range(N_S // CMB_TILE):
        @pl.when(s == CMB_P_STEPS + q)
        def _():
            for h in range(steps_per_tile):
                ysm_ref[:, q * steps_per_tile + h, :] = y[h * DEC_BATCH:(h + 1) * DEC_BATCH, :]


def _combine(pos, ys, x1, route, g, b, layer, final):
    if final:
        out_specs = [pl.BlockSpec(memory_space=pl.ANY),
                     pl.BlockSpec((DEC_BATCH, DEC_SEQ, D_MODEL), lambda s, *_: (0, 0, 0))]
        out_shape = [jax.ShapeDtypeStruct((BATCH, SEQ, D_MODEL), f32),
                     jax.ShapeDtypeStruct((DEC_BATCH, DEC_SEQ, D_MODEL), f32)]
        extra = [pltpu.VMEM((2, CMB_TILE // BATCH, BATCH, D_MODEL), f32), pltpu.SemaphoreType.DMA((2,))]
    else:
        out_specs = [pl.BlockSpec((CMB_TILE, D_MODEL), lambda s, *_: (s, 0))] * 2
        out_shape = [jax.ShapeDtypeStruct((N_TOK, D_MODEL), f32), jax.ShapeDtypeStruct((N_TOK, D_MODEL), bf16)]
        extra = []
    grid_spec = pltpu.PrefetchScalarGridSpec(
        num_scalar_prefetch=MOE_TOPK,
        grid=(N_TOK // CMB_TILE,),
        in_specs=[pl.BlockSpec(memory_space=pl.ANY),
                  pl.BlockSpec((N_PLANES, CMB_TILE, V7X_LANES), lambda s, *_: (0, s, 0)),
                  pl.BlockSpec((CMB_TILE, ROUTE_LANES), lambda s, *_: (s, 0)),
                  _layer_spec(g, layer), _layer_spec(b, layer)],
        out_specs=out_specs,
        scratch_shapes=[pltpu.VMEM((2, MOE_TOPK, N_PLANES, CMB_TILE, V7X_LANES), f32),
                        pltpu.SemaphoreType.DMA((2,))] + extra,
    )
    return pl.pallas_call(
        functools.partial(_combine_kernel, final),
        grid_spec=grid_spec,
        out_shape=out_shape,
        compiler_params=_cparams(("arbitrary",)),
        name="combine_out" if final else "combine",
    )(*pos, ys, x1, route, g, b)


def _to_rows_kernel(xp_ref, xs_ref, o_ref, ob_ref, t3_ref):
    i = pl.program_id(0)

    @pl.when(i < P_TILES)
    def _():
        for b in range(BATCH):
            t3_ref[:, b, :] = xp_ref[b]
        rows = t3_ref[...].reshape(ROW_TILE, D_MODEL)
        o_ref[...] = rows
        ob_ref[...] = rows.astype(bf16)

    @pl.when(i >= P_TILES)
    def _():
        for t in range(DEC_SEQ):
            rows = xs_ref[:, t, :]
            o_ref[t * DEC_BATCH:(t + 1) * DEC_BATCH, :] = rows
            ob_ref[t * DEC_BATCH:(t + 1) * DEC_BATCH, :] = rows.astype(bf16)


def _to_rows(x_prompt, x_sample):
    return pl.pallas_call(
        _to_rows_kernel,
        grid=(N_TILES,),
        in_specs=[pl.BlockSpec((BATCH, P_TC, D_MODEL), lambda i: (0, jnp.minimum(i, P_TILES - 1), 0)),
                  pl.BlockSpec((DEC_BATCH, DEC_SEQ, D_MODEL), lambda i: (0, 0, 0))],
        out_specs=[pl.BlockSpec((ROW_TILE, D_MODEL), lambda i: (i, 0))] * 2,
        out_shape=[jax.ShapeDtypeStruct((N_TOK, D_MODEL), f32), jax.ShapeDtypeStruct((N_TOK, D_MODEL), bf16)],
        scratch_shapes=[pltpu.VMEM((P_TC, BATCH, D_MODEL), f32)],
        compiler_params=_cparams(("arbitrary",)),
        name="to_rows",
    )(x_prompt, x_sample)


def _lookup(table, idx):
    ar = jnp.arange(MOE_EXPERTS, dtype=jnp.int32).reshape((MOE_EXPERTS,) + (1,) * idx.ndim)
    table = table.reshape(ar.shape)
    return jnp.sum(jnp.where(idx[None] == ar, table, 0), axis=0)


def _dispatch_plan(route_t, cnt):
    i32 = jnp.int32
    e = route_t[0:2].astype(i32)
    rank = route_t[4:6].astype(i32)
    counts = cnt[0, :MOE_EXPERTS].astype(i32)
    ends = jnp.cumsum(counts)
    starts = ends - counts
    pos = _lookup(starts, e) + rank

    first_tile = starts // MOE_TM
    last_tile = (ends - 1) // MOE_TM
    ntiles = jnp.where(counts > 0, last_tile - first_tile + 1, 0)
    item_end = jnp.cumsum(ntiles)
    n_items = item_end[-1]
    w = jnp.minimum(jnp.arange(MOE_MAX_ITEMS, dtype=i32), n_items - 1)
    we = jnp.sum((item_end[None, :] <= w[:, None]).astype(i32), axis=-1)
    wt = _lookup(first_tile, we) + w - _lookup(item_end - ntiles, we)
    wlo = jnp.maximum(_lookup(starts, we) - wt * MOE_TM, 0)
    whi = jnp.minimum(_lookup(ends, we) - wt * MOE_TM, MOE_TM)
    changes = (wt[1:] != wt[:-1]).astype(i32)
    wfirst = jnp.concatenate([jnp.ones((1,), i32), changes])
    wlast = jnp.concatenate([changes, jnp.ones((1,), i32)])
    wlast = jnp.where(jnp.arange(MOE_MAX_ITEMS, dtype=i32) == n_items - 1, 1, wlast)
    wefirst = jnp.concatenate([jnp.ones((1,), i32), (we[1:] != we[:-1]).astype(i32)])
    weslot = (jnp.cumsum(wefirst) - 1) % 2
    ar = jnp.arange(MOE_EXPERTS, dtype=i32)
    later = (ar[None, :] > ar[:, None]) & (counts[None, :] > 0)
    next_expert = jnp.min(jnp.where(later, ar[None, :], MOE_EXPERTS), axis=1)
    wenext = _lookup(jnp.where(next_expert < MOE_EXPERTS, next_expert, -1), we)
    return (pos[0], pos[1]), (wt, we, wlo, whi, wfirst, wlast, wefirst, weslot, wenext, n_items.reshape(1))


def _block_diag(w):
    n, a, b = w.shape
    eye = jnp.eye(n, dtype=w.dtype)
    return (w[:, :, None, :] * eye[:, None, :, None]).reshape(n * a, n * b)


def _ssm_params(a_re, a_im, log_dt, b_re, b_im, c_re, c_im):
    ar, ai = a_re, a_im
    dt = jnp.exp(log_dt)[:, None]
    mag = jnp.exp(ar * dt)
    lb_re = mag * jnp.cos(ai * dt)
    lb_im = mag * jnp.sin(ai * dt)
    den = ar * ar + ai * ai
    nr = lb_re - 1.0
    coef_re = (nr * ar + lb_im * ai) / den
    coef_im = (lb_im * ar - nr * ai) / den
    bb_re = coef_re[..., None] * b_re - coef_im[..., None] * b_im
    bb_im = coef_re[..., None] * b_im + coef_im[..., None] * b_re
    gk = SSM_GROUPS // SSM_KB

    def diag_blocks(w):
        return jnp.stack([_block_diag(w[k * gk:(k + 1) * gk]) for k in range(SSM_KB)]).astype(bf16)

    bbre = diag_blocks(bb_re.transpose(0, 2, 1))
    bbim = diag_blocks(bb_im.transpose(0, 2, 1))
    ccre = diag_blocks(c_re.transpose(0, 2, 1))
    ccim = diag_blocks(c_im.transpose(0, 2, 1))
    return (lb_re.reshape(1, SSM_LANES), lb_im.reshape(1, SSM_LANES), bbre, bbim, ccre, ccim)


def kernel(x_prompt, x_sample, state_ret, state_ssm_re, state_ssm_im, state_lru, state_conv, w_in, ret_gn_g, ret_gn_b, ssm_a_re, ssm_a_im, ssm_log_dt, ssm_b_re, ssm_b_im, ssm_c_re, ssm_c_im, ssm_d, ssm_w_glu, lru_conv_w, lru_conv_b, lru_wa, lru_ba, lru_wx, lru_bx, lru_lambda, w_branch, w_out, ln1_g, ln1_b, moe_w_group, moe_b_group, moe_w_expert, moe_b_expert, moe_w1, moe_w3, moe_w2, ln2_g, ln2_b):
    x, xb = _to_rows(x_prompt, x_sample)
    rope = _rope_tables()
    tabs_p = _ret_tables(RET_SUB_T)
    tabs_s = _ret_tables(DEC_SEQ)
    rows = lambda v: v.reshape(DEPTH, 1, -1)
    w_bf = w_in.astype(bf16)
    s0 = state_ret.reshape(DEPTH, DEC_BATCH, 2, 2 * RET_DK, RET_DV)
    ssm_p = jax.vmap(_ssm_params)(ssm_a_re, ssm_a_im, ssm_log_dt, ssm_b_re, ssm_b_im, ssm_c_re, ssm_c_im)
    h0re = state_ssm_re.reshape(DEPTH, DEC_BATCH, SSM_LANES)
    h0im = state_ssm_im.reshape(DEPTH, DEC_BATCH, SSM_LANES)
    conv0 = state_conv.transpose(0, 2, 1, 3).reshape(DEPTH, LRU_HIST_S, BRANCH_W)
    lru_wa_bd = jax.vmap(_block_diag)(lru_wa).astype(bf16)
    lru_wx_bd = jax.vmap(_block_diag)(lru_wx).astype(bf16)
    pad_w = jnp.zeros((DEPTH, D_MODEL, ROUTE_LANES - MOE_GROUPS - MOE_EXPERTS), f32)
    wr = jnp.concatenate([moe_w_group, moe_w_expert, pad_w], axis=-1).astype(bf16)
    br = jnp.concatenate([moe_b_group, moe_b_expert, pad_w[:, 0, :]], axis=-1).reshape(DEPTH, 1, ROUTE_LANES)
    wb_bf, wo_bf, wglu_bf = w_branch.astype(bf16), w_out.astype(bf16), ssm_w_glu.astype(bf16)

    outs = [[] for _ in range(10)]
    for l in range(DEPTH):
        y_ret, ret_p, ret_s = _retention(xb, w_bf, rope, s0, l, tabs_p, tabs_s, rows(ret_gn_g), rows(ret_gn_b))
        y_ssm, re_p, im_p, re_s, im_s = _ssm(xb, w_bf, l, h0re, h0im, *ssm_p, rows(ssm_d), wglu_bf)
        y_lru, lru_p, conv_p, lru_s, conv_s = _lru(
            xb, w_bf, l, state_lru, conv0, lru_conv_w, rows(lru_conv_b),
            lru_wa_bd, rows(lru_ba), lru_wx_bd, rows(lru_bx), rows(lru_lambda))
        x1, route, route_t, cnt = _merge(y_ret, y_ssm, y_lru, xb, w_bf, l, x, wb_bf, wo_bf,
                                         rows(ln1_g), rows(ln1_b), wr, br)

        pos, plan = _dispatch_plan(route_t, cnt)
        xs = _dispatch(pos, x1)
        ys = _moe(plan, xs, moe_w1, moe_w3, moe_w2, l)
        x, xb = _combine(pos, ys, x1, route, rows(ln2_g), rows(ln2_b), l, final=(l == DEPTH - 1))

        outs[0].append(ret_p.reshape(BATCH, RET_HEADS, RET_DK, RET_DV))
        outs[1].append(re_p.reshape(BATCH, SSM_GROUPS, SSM_STATE))
        outs[2].append(im_p.reshape(BATCH, SSM_GROUPS, SSM_STATE))
        outs[3].append(lru_p)
        outs[4].append(conv_p.reshape(CONV_W - 1, BATCH, BRANCH_W).transpose(1, 0, 2))
        outs[5].append(ret_s.reshape(DEC_BATCH, RET_HEADS, RET_DK, RET_DV))
        outs[6].append(re_s.reshape(DEC_BATCH, SSM_GROUPS, SSM_STATE))
        outs[7].append(im_s.reshape(DEC_BATCH, SSM_GROUPS, SSM_STATE))
        outs[8].append(lru_s)
        outs[9].append(conv_s.reshape(CONV_W - 1, DEC_BATCH, BRANCH_W).transpose(1, 0, 2))

    y_prompt, y_sample = x, xb
    return (y_prompt, y_sample) + tuple(jnp.stack(o) for o in outs)
```

```python
import functools

import jax
import jax.numpy as jnp
import numpy as np
from jax import lax
from jax.experimental import pallas as pl
from jax.experimental.pallas import tpu as pltpu

f32 = jnp.float32
bf16 = jnp.bfloat16

D_MODEL = 1024
BATCH = 8
SEQ = 2048
DEPTH = 2
DEC_BATCH = 128
DEC_SEQ = 8
PAST_LEN = 16384
BRANCH_W = 512
RET_HEADS = 4
RET_DK = 64
RET_DV = 128
ROPE_BASE = 10000.0
SSM_GROUPS = 32
SSM_STATE = 64
SSM_LANES = SSM_GROUPS * SSM_STATE
CONV_W = 4
LRU_C = 8.0
MOE_GROUPS = 4
MOE_PER_GROUP = 8
MOE_EXPERTS = 32
MOE_TOPK = 2
MOE_HIDDEN = 512
DN_ALPHA = (2.0 * DEPTH) ** 0.25
LN_EPS = 1e-5

V7X_SUBLANES = 8
V7X_LANES = 128
V7X_VMEM_LIMIT = 56 * 1024 * 1024

N_P = BATCH * SEQ
N_S = DEC_BATCH * DEC_SEQ
N_TOK = N_P + N_S
ROW_TILE = 1024
P_TILES = N_P // ROW_TILE
N_TILES = N_TOK // ROW_TILE
P_TC = ROW_TILE // BATCH
RET_SUB_T = 32
RET_SUB_R = RET_SUB_T * BATCH
RET_SUBS = ROW_TILE // RET_SUB_R
S_BLOCKS = DEC_BATCH // BATCH
MERGE_TILE = 512
MOE_TM = 512
N_PAIRS = N_TOK * MOE_TOPK
MOE_MAX_ITEMS = N_PAIRS // MOE_TM + MOE_EXPERTS - 1
DSP_TILE = 1024
CMB_TILE = 256
CMB_P_STEPS = N_P // CMB_TILE
ROUTE_LANES = 128


PROJ_RC = 256
RET_COLS, RET_BLK = 1536, 0
SSM_COLS, SSM_BLK = 512, 3
LRU_COLS, LRU_BLK = 1024, 2
GATE_COLS, GATE_BLK = 3072, 1


def _cparams(sem):
    return pltpu.CompilerParams(dimension_semantics=sem, vmem_limit_bytes=V7X_VMEM_LIMIT)


N_PLANES = D_MODEL // V7X_LANES


def _to_planes(ref, rows):
    for c in range(N_PLANES):
        ref[c] = rows[:, c * V7X_LANES:(c + 1) * V7X_LANES]


def _from_planes(ref):
    return jnp.concatenate([ref[c] for c in range(N_PLANES)], axis=1)


def _layer_spec(stacked, layer):
    shape = (None,) + stacked.shape[1:]
    return pl.BlockSpec(shape, lambda *_: (layer,) + (0,) * (stacked.ndim - 1), pipeline_mode=pl.Buffered(1))


def _const_spec(block_shape, index):
    return pl.BlockSpec(block_shape, lambda *_: index, pipeline_mode=pl.Buffered(1))


def _dot(a, b):
    return jnp.dot(a, b, preferred_element_type=f32)


def _dot_nt(a, b):
    return lax.dot_general(a, b, (((1,), (1,)), ((), ())), preferred_element_type=f32)


def _dot_tn(a, b):
    return lax.dot_general(a, b, (((0,), (0,)), ((), ())), preferred_element_type=f32)


def _layer_norm_rows(x, g, b):
    mu = jnp.mean(x, -1, keepdims=True)
    xc = x - mu
    var = jnp.mean(xc * xc, -1, keepdims=True)
    return xc * lax.rsqrt(var + LN_EPS) * g + b


def _project_rows(xb_ref, w_ref, z_ref):
    for rc in range(ROW_TILE // PROJ_RC):
        rs = slice(rc * PROJ_RC, (rc + 1) * PROJ_RC)
        z_ref[rs, :] = _dot(xb_ref[rs, :], w_ref[0])


def _ret_block(q, k, v, g, cosb, sinb, mask_ref, qdec_ref, kdec_ref, cdec_ref, scat_ref, gng, gnb):
    rows = q.shape[0]
    lane_qk = lax.broadcasted_iota(jnp.int32, (rows, 2 * V7X_LANES), 1)
    first_half = (lane_qk & (RET_DK - 1)) < (RET_DK // 2)

    def rope(x):
        partner = jnp.where(first_half, pltpu.roll(x, 2 * V7X_LANES - RET_DK // 2, 1),
                            pltpu.roll(x, RET_DK // 2, 1))
        return x * cosb + partner * sinb

    q = rope(q)
    k = rope(k) * (RET_DK ** -0.5)
    kd = k * kdec_ref[...]
    lane = lax.broadcasted_iota(jnp.int32, (rows, V7X_LANES), 1)
    row_b = lax.broadcasted_iota(jnp.int32, (rows, V7X_LANES), 0) & (BATCH - 1)
    outs = []
    for p in range(2):
        qp = q[:, p * V7X_LANES:(p + 1) * V7X_LANES]
        kp = k[:, p * V7X_LANES:(p + 1) * V7X_LANES].astype(bf16)
        kdp = kd[:, p * V7X_LANES:(p + 1) * V7X_LANES]
        s_old = scat_ref[p]
        s_bf = s_old.astype(bf16)
        s_new = s_old * jnp.concatenate([cdec_ref[p]] * BATCH, axis=1)
        for hh in range(2):
            h = 2 * p + hh
            head_lanes = (lane >= RET_DK) if hh else (lane < RET_DK)
            qh = jnp.where(head_lanes, qp, 0.0).astype(bf16)
            kdh = jnp.where(head_lanes, kdp, 0.0).astype(bf16)
            vh = v[:, h * RET_DV:(h + 1) * RET_DV]
            vh_bf = vh.astype(bf16)
            sc = _dot_nt(qh, kp) * mask_ref[h]
            o = _dot(sc.astype(bf16), vh_bf)
            cross = _dot(qh, s_bf)
            oc = jnp.zeros((rows, RET_DV), f32)
            for b in range(BATCH):
                oc = oc + jnp.where(row_b == b, cross[:, b * RET_DV:(b + 1) * RET_DV], 0.0)
            o = o + oc * qdec_ref[h]
            vcat = jnp.concatenate([jnp.where(row_b == b, vh_bf, jnp.zeros_like(vh_bf))
                                    for b in range(BATCH)], axis=1)
            s_new = s_new + _dot_tn(kdh, vcat)
            mu = jnp.mean(o, -1, keepdims=True)
            oc2 = o - mu
            var = jnp.mean(oc2 * oc2, -1, keepdims=True)
            outs.append(oc2 * lax.rsqrt(var + LN_EPS))
        scat_ref[p] = s_new
    o = jnp.concatenate(outs, axis=1) * gng + gnb
    return jax.nn.silu(g) * o


def _per_step_rows(tab_ref, t0, steps):
    return jnp.concatenate(
        [jnp.broadcast_to(tab_ref[t0 + t:t0 + t + 1, :], (BATCH, tab_ref.shape[1])) for t in range(steps)], axis=0)


def _ret_kernel(xb_ref, w_ref, cos_ref, sin_ref, cos_s_ref, sin_s_ref, s0_ref,
                mask_p_ref, qdec_p_ref, kdec_p_ref, cdec_p_ref,
                mask_s_ref, qdec_s_ref, kdec_s_ref, cdec_s_ref,
                gng_ref, gnb_ref,
                y_ref, retp_ref, rets_ref, scat_ref, z_ref):
    i = pl.program_id(0)
    gng = gng_ref[...]
    gnb = gnb_ref[...]
    q_cols, k_cols = slice(0, 256), slice(256, 512)
    v_cols, g_cols = slice(512, 1024), slice(1024, 1536)

    @pl.when(i == 0)
    def _():
        scat_ref[...] = jnp.zeros_like(scat_ref)

    @pl.when(i <= P_TILES)
    def _():
        _project_rows(xb_ref, w_ref, z_ref)

    @pl.when(i < P_TILES)
    def _():
        for sc in range(RET_SUBS):
            r0 = sc * RET_SUB_R
            rs = slice(r0, r0 + RET_SUB_R)
            y_ref[rs, :] = _ret_block(
                z_ref[rs, q_cols], z_ref[rs, k_cols], z_ref[rs, v_cols], z_ref[rs, g_cols],
                _per_step_rows(cos_ref, sc * RET_SUB_T, RET_SUB_T), _per_step_rows(sin_ref, sc * RET_SUB_T, RET_SUB_T),
                mask_p_ref, qdec_p_ref, kdec_p_ref, cdec_p_ref, scat_ref, gng, gnb)

    @pl.when(i == P_TILES - 1)
    def _():
        for b in range(BATCH):
            for p in range(2):
                retp_ref[b, p] = scat_ref[p, :, b * RET_DV:(b + 1) * RET_DV]

    @pl.when(i >= P_TILES)
    def _():
        bb = i - P_TILES
        for b in range(BATCH):
            for p in range(2):
                scat_ref[p, :, b * RET_DV:(b + 1) * RET_DV] = s0_ref[0, b, p]

        def rows_of(cols):
            return jnp.concatenate(
                [z_ref[pl.ds(pl.multiple_of(t * DEC_BATCH + bb * BATCH, BATCH), BATCH), cols]
                 for t in range(DEC_SEQ)], axis=0)

        y = _ret_block(
            rows_of(q_cols), rows_of(k_cols), rows_of(v_cols),
            rows_of(g_cols), _per_step_rows(cos_s_ref, 0, DEC_SEQ), _per_step_rows(sin_s_ref, 0, DEC_SEQ),
            mask_s_ref, qdec_s_ref, kdec_s_ref, cdec_s_ref, scat_ref, gng, gnb)
        for t in range(DEC_SEQ):
            y_ref[pl.ds(pl.multiple_of(t * DEC_BATCH + bb * BATCH, BATCH), BATCH), :] = (
                y[t * BATCH:(t + 1) * BATCH, :])
        for b in range(BATCH):
            for p in range(2):
                rets_ref[b, p] = scat_ref[p, :, b * RET_DV:(b + 1) * RET_DV]


def _ret_tables(tc):
    rows = tc * BATCH
    nf = np.float32
    log_g = np.log1p(-np.exp2(nf(-5.0) - np.arange(RET_HEADS, dtype=nf))).astype(nf)
    t_idx = (np.arange(rows) // BATCH).astype(nf)
    b_idx = np.arange(rows) % BATCH
    rel = t_idx[:, None] - t_idx[None, :]
    same = b_idx[:, None] == b_idx[None, :]
    decay = np.exp(log_g[:, None, None] * np.maximum(rel, nf(0.0)))
    mask = np.where((rel >= 0) & same, decay, nf(0.0))
    qdec = np.exp(log_g[:, None] * (t_idx[None, :] + nf(1.0)))
    qdec = np.broadcast_to(qdec[:, :, None], (RET_HEADS, rows, RET_DV))
    kdec = np.exp(log_g[:, None] * (nf(tc - 1.0) - t_idx[None, :]))
    kdec = np.broadcast_to(kdec.T[:, :, None], (rows, RET_HEADS, RET_DK)).reshape(rows, RET_HEADS * RET_DK)
    cdec = np.exp(log_g * nf(tc))
    cdec = np.broadcast_to(cdec[:, None, None], (RET_HEADS, RET_DK, RET_DV)).reshape(2, 2 * RET_DK, RET_DV)
    return tuple(jnp.asarray(np.ascontiguousarray(a), dtype=f32) for a in (mask, qdec, kdec, cdec))


def _rope_tables():
    half = RET_DK // 2
    inv = ROPE_BASE ** (-jnp.arange(half, dtype=f32) / half)
    pos_p = jnp.arange(SEQ, dtype=f32)
    pos_s = PAST_LEN + jnp.arange(DEC_SEQ, dtype=f32)

    def tab(pos):
        ang = pos[:, None] * inv[None, :]
        cos = jnp.cos(ang)
        sin = jnp.sin(ang)
        cos_h = jnp.concatenate([cos, cos], axis=1)
        sin_h = jnp.concatenate([-sin, sin], axis=1)
        return jnp.tile(cos_h, (1, RET_HEADS)), jnp.tile(sin_h, (1, RET_HEADS))

    return tab(pos_p) + tab(pos_s)


def _retention(xb, w_bf, rope, s0_s, layer, tabs_p, tabs_s, gng, gnb):
    n_steps = P_TILES + S_BLOCKS
    tile = lambda i: jnp.minimum(i, P_TILES)
    sblk = lambda i: jnp.maximum(i - P_TILES, 0)
    full = lambda a: pl.BlockSpec(a.shape, lambda i, _n=a.ndim: (0,) * _n)
    state_blk = (BATCH, 2, 2 * RET_DK, RET_DV)
    cos_p, sin_p, cos_s, sin_s = rope
    ptile = lambda i: jnp.minimum(i, P_TILES - 1)
    ins = [xb, w_bf, cos_p, sin_p, cos_s, sin_s, s0_s, *tabs_p, *tabs_s, gng, gnb]
    in_specs = [
        pl.BlockSpec((ROW_TILE, D_MODEL), lambda i: (tile(i), 0)),
        _const_spec((1, D_MODEL, RET_COLS), (layer, 0, RET_BLK)),
        pl.BlockSpec((P_TC, 256), lambda i: (ptile(i), 0)),
        pl.BlockSpec((P_TC, 256), lambda i: (ptile(i), 0)),
        full(cos_s), full(sin_s),
        pl.BlockSpec((1,) + state_blk, lambda i: (layer, sblk(i), 0, 0, 0)),
    ] + [full(a) for a in (*tabs_p, *tabs_s)] + [_layer_spec(gng, layer), _layer_spec(gnb, layer)]
    return pl.pallas_call(
        _ret_kernel,
        grid=(n_steps,),
        in_specs=in_specs,
        out_specs=[pl.BlockSpec((ROW_TILE, BRANCH_W), lambda i: (tile(i), 0)),
                   pl.BlockSpec(state_blk, lambda i: (0, 0, 0, 0)),
                   pl.BlockSpec(state_blk, lambda i: (sblk(i), 0, 0, 0))],
        out_shape=[jax.ShapeDtypeStruct((N_TOK, BRANCH_W), f32),
                   jax.ShapeDtypeStruct((BATCH, 2, 2 * RET_DK, RET_DV), f32),
                   jax.ShapeDtypeStruct((DEC_BATCH, 2, 2 * RET_DK, RET_DV), f32)],
        scratch_shapes=[pltpu.VMEM((2, 2 * RET_DK, BATCH * RET_DV), f32),
                        pltpu.VMEM((ROW_TILE, RET_COLS), f32)],
        compiler_params=_cparams(("arbitrary",)),
        name="retention",
    )(*ins)


SSM_LB = 512
SSM_RC = 256
SSM_KB = 2
SSM_KB_U = BRANCH_W // SSM_KB
SSM_KB_H = SSM_LANES // SSM_KB


def _ssm_scan(bre_ref, bim_ref, lre_ref, lim_ref, h_re0, h_im0, row0, nb_rows, steps, lb, unroll):
    ls = slice(lb * SSM_LB, (lb + 1) * SSM_LB)
    a_re = jnp.broadcast_to(lre_ref[:, ls], (V7X_SUBLANES, SSM_LB))
    a_im = jnp.broadcast_to(lim_ref[:, ls], (V7X_SUBLANES, SSM_LB))

    def step(t, carry):
        h_re, h_im = carry
        r = pl.multiple_of(row0 + t * nb_rows, V7X_SUBLANES)
        n_re = a_re * h_re - a_im * h_im + bre_ref[pl.ds(r, V7X_SUBLANES), ls]
        n_im = a_re * h_im + a_im * h_re + bim_ref[pl.ds(r, V7X_SUBLANES), ls]
        bre_ref[pl.ds(r, V7X_SUBLANES), ls] = n_re
        bim_ref[pl.ds(r, V7X_SUBLANES), ls] = n_im
        return n_re, n_im

    return lax.fori_loop(0, steps, step, (h_re0, h_im0), unroll=unroll)


def _ssm_kernel(xb_ref, w_ref, h0re_ref, h0im_ref, lre_ref, lim_ref, bbre_ref, bbim_ref, ccre_ref, ccim_ref,
                d_ref, wglu_ref,
                y_ref, pre_ref, pim_ref, sre_ref, sim_ref,
                bre_ref, bim_ref, hre_ref, him_ref, zs_ref):
    i = pl.program_id(0)
    _project_rows(xb_ref, w_ref, zs_ref)

    def inputs(rc):
        rs = slice(rc * SSM_RC, (rc + 1) * SSM_RC)
        ub = zs_ref[rs, :].astype(bf16)
        for k in range(SSM_KB):
            uk = ub[:, k * SSM_KB_U:(k + 1) * SSM_KB_U]
            hs = slice(k * SSM_KB_H, (k + 1) * SSM_KB_H)
            bre_ref[rs, hs] = _dot(uk, bbre_ref[k])
            bim_ref[rs, hs] = _dot(uk, bbim_ref[k])

    def outputs(rc):
        rs = slice(rc * SSM_RC, (rc + 1) * SSM_RC)
        ch = []
        for k in range(SSM_KB):
            hs = slice(k * SSM_KB_H, (k + 1) * SSM_KB_H)
            ch.append(_dot(bre_ref[rs, hs].astype(bf16), ccre_ref[k]) - _dot(bim_ref[rs, hs].astype(bf16), ccim_ref[k]))
        y = jnp.concatenate(ch, axis=1) + d_ref[...] * zs_ref[rs, :]
        zz = jax.nn.gelu(y)
        y_ref[rs, :] = zz * jax.nn.sigmoid(_dot(zz.astype(bf16), wglu_ref[...]))

    @pl.when(i == 0)
    def _():
        hre_ref[...] = jnp.zeros_like(hre_ref)
        him_ref[...] = jnp.zeros_like(him_ref)

    @pl.when(i < P_TILES)
    def _():
        steps = SSM_RC // BATCH
        for rc in range(ROW_TILE // SSM_RC):
            inputs(rc)
        for rc in range(ROW_TILE // SSM_RC):
            for lb in range(SSM_LANES // SSM_LB):
                ls = slice(lb * SSM_LB, (lb + 1) * SSM_LB)
                a_re = jnp.broadcast_to(lre_ref[:, ls], (BATCH, SSM_LB))
                a_im = jnp.broadcast_to(lim_ref[:, ls], (BATCH, SSM_LB))
                h_re, h_im = hre_ref[:, ls], him_ref[:, ls]
                for t in range(steps):
                    rows = slice(rc * SSM_RC + t * BATCH, rc * SSM_RC + (t + 1) * BATCH)
                    h_re, h_im = (a_re * h_re - a_im * h_im + bre_ref[rows, ls],
                                  a_re * h_im + a_im * h_re + bim_ref[rows, ls])
                    bre_ref[rows, ls] = h_re
                    bim_ref[rows, ls] = h_im
                hre_ref[:, ls] = h_re
                him_ref[:, ls] = h_im
            outputs(rc)
        pre_ref[...] = hre_ref[...]
        pim_ref[...] = him_ref[...]

    @pl.when(i >= P_TILES)
    def _():
        for rc in range(ROW_TILE // SSM_RC):
            inputs(rc)

        def per_row_tile(rt, c):
            r0 = pl.multiple_of(rt * V7X_SUBLANES, V7X_SUBLANES)
            for lb in range(SSM_LANES // SSM_LB):
                ls = slice(lb * SSM_LB, (lb + 1) * SSM_LB)
                _ssm_scan(bre_ref, bim_ref, lre_ref, lim_ref,
                          h0re_ref[pl.ds(r0, V7X_SUBLANES), ls], h0im_ref[pl.ds(r0, V7X_SUBLANES), ls],
                          r0, DEC_BATCH, DEC_SEQ, lb, True)
            return c

        lax.fori_loop(0, DEC_BATCH // V7X_SUBLANES, per_row_tile, 0)
        last = (DEC_SEQ - 1) * DEC_BATCH
        sre_ref[...] = bre_ref[last:last + DEC_BATCH, :]
        sim_ref[...] = bim_ref[last:last + DEC_BATCH, :]
        for rc in range(ROW_TILE // SSM_RC):
            outputs(rc)


def _ssm(xb, w_bf, layer, h0re, h0im, lre, lim, bbre, bbim, ccre, ccim, dvec, wglu):
    full = lambda a: _layer_spec(a, layer)
    consts = (h0re, h0im, lre, lim, bbre, bbim, ccre, ccim, dvec, wglu)
    return pl.pallas_call(
        _ssm_kernel,
        grid=(N_TILES,),
        in_specs=[pl.BlockSpec((ROW_TILE, D_MODEL), lambda i: (i, 0)),
                  _const_spec((1, D_MODEL, SSM_COLS), (layer, 0, SSM_BLK))] + [full(a) for a in consts],
        out_specs=[pl.BlockSpec((ROW_TILE, BRANCH_W), lambda i: (i, 0)),
                   pl.BlockSpec((BATCH, SSM_LANES), lambda i: (0, 0)),
                   pl.BlockSpec((BATCH, SSM_LANES), lambda i: (0, 0)),
                   pl.BlockSpec((DEC_BATCH, SSM_LANES), lambda i: (0, 0)),
                   pl.BlockSpec((DEC_BATCH, SSM_LANES), lambda i: (0, 0))],
        out_shape=[jax.ShapeDtypeStruct((N_TOK, BRANCH_W), f32),
                   jax.ShapeDtypeStruct((BATCH, SSM_LANES), f32),
                   jax.ShapeDtypeStruct((BATCH, SSM_LANES), f32),
                   jax.ShapeDtypeStruct((DEC_BATCH, SSM_LANES), f32),
                   jax.ShapeDtypeStruct((DEC_BATCH, SSM_LANES), f32)],
        scratch_shapes=[pltpu.VMEM((ROW_TILE, SSM_LANES), f32), pltpu.VMEM((ROW_TILE, SSM_LANES), f32),
                        pltpu.VMEM((BATCH, SSM_LANES), f32), pltpu.VMEM((BATCH, SSM_LANES), f32),
                        pltpu.VMEM((ROW_TILE, SSM_COLS), f32)],
        compiler_params=_cparams(("arbitrary",)),
        name="ssm",
    )(xb, w_bf, *consts)


LRU_HIST_P = (CONV_W - 1) * BATCH
LRU_HIST_S = (CONV_W - 1) * DEC_BATCH


def _lru_gates(xe_ref, nb_rows, cw_ref, cb_ref, wa_ref, ba_ref, wx_ref, bx_ref, lam_ref):
    xc = cb_ref[...] + xe_ref[0:ROW_TILE, :] * cw_ref[0:1, :]
    for j in range(1, CONV_W):
        xc = xc + xe_ref[j * nb_rows:j * nb_rows + ROW_TILE, :] * cw_ref[j:j + 1, :]
    xcb = xc.astype(bf16)
    r = jax.nn.sigmoid(_dot(xcb, wa_ref[...]) + ba_ref[...])
    ig = jax.nn.sigmoid(_dot(xcb, wx_ref[...]) + bx_ref[...])
    log_a = -LRU_C * r * jax.nn.softplus(-lam_ref[...])
    a = jnp.exp(log_a)
    b = jnp.sqrt(-jnp.tanh(log_a) * (a * a + 1.0)) * (ig * xc)
    return a, b


def _lru_kernel(xb_ref, w_ref, h0_ref, conv0_ref, cw_ref, cb_ref, wa_ref, ba_ref, wx_ref, bx_ref, lam_ref,
                y_ref, hp_ref, convp_ref, hs_ref, convs_ref,
                xe_ref, a_ref, b_ref, hc_ref, z_ref):
    i = pl.program_id(0)
    params = (cw_ref, cb_ref, wa_ref, ba_ref, wx_ref, bx_ref, lam_ref)
    _project_rows(xb_ref, w_ref, z_ref)
    zx_ref = z_ref.at[:, 0:BRANCH_W]
    zg_ref = z_ref.at[:, BRANCH_W:2 * BRANCH_W]

    @pl.when(i == 0)
    def _():
        xe_ref[0:LRU_HIST_P, :] = jnp.zeros((LRU_HIST_P, BRANCH_W), f32)
        hc_ref[...] = jnp.zeros_like(hc_ref)

    @pl.when(i < P_TILES)
    def _():
        xe_ref[LRU_HIST_P:LRU_HIST_P + ROW_TILE, :] = zx_ref[...]
        a, b = _lru_gates(xe_ref, BATCH, *params)
        a_ref[...] = a
        b_ref[...] = b
        hist = xe_ref[ROW_TILE:ROW_TILE + LRU_HIST_P, :]
        xe_ref[0:LRU_HIST_P, :] = hist
        convp_ref[...] = hist

        def step(t, h):
            r = pl.multiple_of(t * BATCH, BATCH)
            h = a_ref[pl.ds(r, BATCH), :] * h + b_ref[pl.ds(r, BATCH), :]
            b_ref[pl.ds(r, BATCH), :] = h
            return h

        h = lax.fori_loop(0, P_TC, step, hc_ref[...], unroll=8)
        hc_ref[...] = h
        hp_ref[...] = h

    @pl.when(i >= P_TILES)
    def _():
        xe_ref[0:LRU_HIST_S, :] = conv0_ref[...]
        xe_ref[LRU_HIST_S:LRU_HIST_S + ROW_TILE, :] = zx_ref[...]
        a, b = _lru_gates(xe_ref, DEC_BATCH, *params)
        a_ref[...] = a
        b_ref[...] = b
        convs_ref[...] = xe_ref[ROW_TILE:ROW_TILE + LRU_HIST_S, :]

        def per_row_tile(rt, c):
            r0 = pl.multiple_of(rt * V7X_SUBLANES, V7X_SUBLANES)
            h = h0_ref[pl.ds(r0, V7X_SUBLANES), :]
            for t in range(DEC_SEQ):
                r = pl.multiple_of(t * DEC_BATCH + r0, V7X_SUBLANES)
                h = a_ref[pl.ds(r, V7X_SUBLANES), :] * h + b_ref[pl.ds(r, V7X_SUBLANES), :]
                b_ref[pl.ds(r, V7X_SUBLANES), :] = h
            return c

        lax.fori_loop(0, DEC_BATCH // V7X_SUBLANES, per_row_tile, 0)
        last = (DEC_SEQ - 1) * DEC_BATCH
        hs_ref[...] = b_ref[last:last + DEC_BATCH, :]

    y_ref[...] = b_ref[...] * jax.nn.gelu(zg_ref[...])


def _lru(xb, w_bf, layer, h0, conv0, cw, cb, wa, ba, wx, bx, lam):
    full = lambda a: _layer_spec(a, layer)
    consts = (h0, conv0, cw, cb, wa, ba, wx, bx, lam)
    return pl.pallas_call(
        _lru_kernel,
        grid=(N_TILES,),
        in_specs=[pl.BlockSpec((ROW_TILE, D_MODEL), lambda i: (i, 0)),
                  _const_spec((1, D_MODEL, LRU_COLS), (layer, 0, LRU_BLK))] + [full(a) for a in consts],
        out_specs=[pl.BlockSpec((ROW_TILE, BRANCH_W), lambda i: (i, 0)),
                   pl.BlockSpec((BATCH, BRANCH_W), lambda i: (0, 0)),
                   pl.BlockSpec((LRU_HIST_P, BRANCH_W), lambda i: (0, 0)),
                   pl.BlockSpec((DEC_BATCH, BRANCH_W), lambda i: (0, 0)),
                   pl.BlockSpec((LRU_HIST_S, BRANCH_W), lambda i: (0, 0))],
        out_shape=[jax.ShapeDtypeStruct((N_TOK, BRANCH_W), f32),
                   jax.ShapeDtypeStruct((BATCH, BRANCH_W), f32),
                   jax.ShapeDtypeStruct((LRU_HIST_P, BRANCH_W), f32),
                   jax.ShapeDtypeStruct((DEC_BATCH, BRANCH_W), f32),
                   jax.ShapeDtypeStruct((LRU_HIST_S, BRANCH_W), f32)],
        scratch_shapes=[pltpu.VMEM((ROW_TILE + LRU_HIST_S, BRANCH_W), f32),
                        pltpu.VMEM((ROW_TILE, BRANCH_W), f32), pltpu.VMEM((ROW_TILE, BRANCH_W), f32),
                        pltpu.VMEM((BATCH, BRANCH_W), f32),
                        pltpu.VMEM((ROW_TILE, LRU_COLS), f32)],
        compiler_params=_cparams(("arbitrary",)),
        name="lru",
    )(xb, w_bf, *consts)


def _merge_kernel(yr_ref, ys_ref, yl_ref, xb_ref, wg_ref, x_ref, wb_ref, wo_ref, g_ref, b_ref, wr_ref, br_ref,
                  x1_ref, route_ref, route_t_ref, cnt_out_ref, cnt_ref):
    merged = jnp.zeros((MERGE_TILE, D_MODEL), f32)
    xb = xb_ref[...]
    for n, y_ref in enumerate((yr_ref, ys_ref, yl_ref)):
        proj = _dot(y_ref[...].astype(bf16), wb_ref[n])
        gate = jax.nn.sigmoid(_dot(xb, wg_ref[0, :, n * D_MODEL:(n + 1) * D_MODEL]))
        merged = merged + gate * proj
    mix = _dot(merged.astype(bf16), wo_ref[...])
    x1 = _layer_norm_rows(DN_ALPHA * x_ref[...] + mix, g_ref[...], b_ref[...])
    _to_planes(x1_ref, x1)

    logits = _dot(x1.astype(bf16), wr_ref[...]) + br_ref[...]
    lane = lax.broadcasted_iota(jnp.int32, (MERGE_TILE, ROUTE_LANES), 1).astype(f32)
    big = jnp.float32(ROUTE_LANES)
    neg = jnp.float32(-jnp.inf)
    is_g = lane < MOE_GROUPS
    lg = jnp.where(is_g, logits, neg)
    mg = jnp.max(lg, -1, keepdims=True)
    gsel = jnp.min(jnp.where(lg == mg, lane, big), -1, keepdims=True)
    sum_g = jnp.sum(jnp.where(is_g, jnp.exp(lg - mg), 0.0), -1, keepdims=True)
    pg_sel = 1.0 / sum_g
    lo = MOE_GROUPS + gsel * MOE_PER_GROUP
    is_e = jnp.abs(lane - lo - 0.5 * (MOE_PER_GROUP - 1)) < 0.5 * MOE_PER_GROUP
    le = jnp.where(is_e, logits, neg)
    me = jnp.max(le, -1, keepdims=True)
    ex = jnp.where(is_e, jnp.exp(le - me), 0.0)
    pe = jnp.where(is_e, ex / jnp.sum(ex, -1, keepdims=True), -1.0)
    v1 = jnp.max(pe, -1, keepdims=True)
    i1 = jnp.min(jnp.where(pe == v1, lane, big), -1, keepdims=True)
    pe2 = jnp.where(lane == i1, -1.0, pe)
    v2 = jnp.max(pe2, -1, keepdims=True)
    i2 = jnp.min(jnp.where(pe2 == v2, lane, big), -1, keepdims=True)
    vsum = v1 + v2
    w1 = pg_sel * v1 / vsum
    w2 = pg_sel * v2 / vsum
    e1 = i1 - MOE_GROUPS
    e2 = i2 - MOE_GROUPS

    @pl.when(pl.program_id(0) == 0)
    def _():
        cnt_ref[...] = jnp.zeros_like(cnt_ref)

    oh1 = lane == e1
    oh2 = lane == e2
    ohs = jnp.where(oh1, 1.0, jnp.where(oh2, 1.0, 0.0))
    r_i = lax.broadcasted_iota(jnp.int32, (MERGE_TILE, MERGE_TILE), 0)
    c_i = lax.broadcasted_iota(jnp.int32, (MERGE_TILE, MERGE_TILE), 1)
    strict_lower = jnp.where(c_i < r_i, 1.0, 0.0).astype(bf16)
    before = _dot(strict_lower, ohs.astype(bf16)) + cnt_ref[0:1, :]
    rank1 = jnp.sum(jnp.where(oh1, before, 0.0), -1, keepdims=True)
    rank2 = jnp.sum(jnp.where(oh2, before, 0.0), -1, keepdims=True)
    cnt_ref[0:1, :] = cnt_ref[0:1, :] + jnp.sum(ohs, 0, keepdims=True)
    cnt_out_ref[...] = cnt_ref[...]

    route = jnp.zeros((MERGE_TILE, ROUTE_LANES), f32)
    for k, val in enumerate((e1, e2, w1, w2, rank1, rank2)):
        route = jnp.where(lane == k, val, route)
    route_ref[...] = route
    route_t_ref[...] = route.T[0:V7X_SUBLANES, :]


def _merge(y_ret, y_ssm, y_lru, xb, w_bf, layer, x, wb, wo, g, b, wr, br):
    full = lambda a: _layer_spec(a, layer)
    row = lambda w: pl.BlockSpec((MERGE_TILE, w), lambda i: (i, 0))
    consts = (wb, wo, g, b, wr, br)
    return pl.pallas_call(
        _merge_kernel,
        grid=(N_TOK // MERGE_TILE,),
        in_specs=[row(BRANCH_W), row(BRANCH_W), row(BRANCH_W), row(D_MODEL),
                  _const_spec((1, D_MODEL, GATE_COLS), (layer, 0, GATE_BLK)),
                  row(D_MODEL)] + [full(a) for a in consts],
        out_specs=[pl.BlockSpec((N_PLANES, MERGE_TILE, V7X_LANES), lambda i: (0, i, 0)), row(ROUTE_LANES),
                   pl.BlockSpec((V7X_SUBLANES, MERGE_TILE), lambda i: (0, i)),
                   pl.BlockSpec((V7X_SUBLANES, ROUTE_LANES), lambda i: (0, 0))],
        out_shape=[jax.ShapeDtypeStruct((N_PLANES, N_TOK, V7X_LANES), f32),
                   jax.ShapeDtypeStruct((N_TOK, ROUTE_LANES), f32),
                   jax.ShapeDtypeStruct((V7X_SUBLANES, N_TOK), f32),
                   jax.ShapeDtypeStruct((V7X_SUBLANES, ROUTE_LANES), f32)],
        scratch_shapes=[pltpu.VMEM((V7X_SUBLANES, ROUTE_LANES), f32)],
        compiler_params=_cparams(("arbitrary",)),
        name="merge",
    )(y_ret, y_ssm, y_lru, xb, w_bf, x, *consts)


def _dispatch_kernel(pos0_ref, pos1_ref, x1_ref, xs_hbm, sem):
    base = pl.program_id(0) * DSP_TILE

    def row_copy(r, dst_row):
        return pltpu.make_async_copy(x1_ref.at[:, r, :], xs_hbm.at[dst_row], sem.at[0])

    def issue(r, c):
        n = base + r
        for k in range(MOE_TOPK):
            row_copy(r, (pos0_ref, pos1_ref)[k][n]).start(priority=k % 2)
        return c

    lax.fori_loop(0, DSP_TILE, issue, 0, unroll=8)
    for _ in range(MOE_TOPK):
        pltpu.make_async_copy(x1_ref, x1_ref, sem.at[0]).wait()


def _dispatch(pos, x1p):
    grid_spec = pltpu.PrefetchScalarGridSpec(
        num_scalar_prefetch=MOE_TOPK,
        grid=(N_TOK // DSP_TILE,),
        in_specs=[pl.BlockSpec((N_PLANES, DSP_TILE, V7X_LANES), lambda s, *_: (0, s, 0))],
        out_specs=pl.BlockSpec(memory_space=pl.ANY),
        scratch_shapes=[pltpu.SemaphoreType.DMA((1,))],
    )
    return pl.pallas_call(
        _dispatch_kernel,
        grid_spec=grid_spec,
        out_shape=jax.ShapeDtypeStruct((N_PAIRS, N_PLANES, V7X_LANES), f32),
        compiler_params=_cparams(("arbitrary",)),
        name="dispatch",
    )(*pos, x1p)


def _moe_kernel(layer, wt_ref, we_ref, wlo_ref, whi_ref, wfirst_ref, wlast_ref, wefirst_ref, weslot_ref,
                wenext_ref, nw_ref,
                xs_hbm, w1_hbm, w3_hbm, w2_hbm, ys_hbm,
                xin, yout, sem_in, sem_out, wst1, wst3, wst2, sem_w, w1b, w3b, w2b):
    w = pl.program_id(0)
    n_items = nw_ref[0]

    def weight_copies(expert, slot):
        return [pltpu.make_async_copy(src.at[layer, expert], dst.at[slot], sem_w.at[slot])
                for src, dst in ((w1_hbm, wst1), (w3_hbm, wst3), (w2_hbm, wst2))]

    def in_copy(item, slot, c):
        r0 = pl.multiple_of(wt_ref[item] * MOE_TM, MOE_TM)
        return pltpu.make_async_copy(xs_hbm.at[pl.ds(r0, MOE_TM), c, :], xin.at[slot, c], sem_in.at[slot])

    def out_copy(tile, slot, c):
        r0 = pl.multiple_of(tile * MOE_TM, MOE_TM)
        return pltpu.make_async_copy(yout.at[slot, c], ys_hbm.at[pl.ds(r0, MOE_TM), c, :], sem_out.at[slot])

    @pl.when(w == 0)
    def _():
        for cp in weight_copies(we_ref[0], 0):
            cp.start()
        for c in range(N_PLANES):
            in_copy(0, 0, c).start(priority=1)

    @pl.when(w + 1 < n_items)
    def _():
        for c in range(N_PLANES):
            in_copy(w + 1, (w + 1) % 2, c).start(priority=1)

    @pl.when(w < n_items)
    def _():
        slot = w % 2
        tile = wt_ref[w]
        oslot = tile % 2
        for c in range(N_PLANES):
            in_copy(w, slot, c).wait()

        @pl.when(wefirst_ref[w] == 1)
        def _():
            wslot = weslot_ref[w]
            for cp in weight_copies(we_ref[w], wslot):
                cp.wait()
            w1b[...] = wst1[wslot].astype(bf16)
            w3b[...] = wst3[wslot].astype(bf16)
            w2b[...] = wst2[wslot].astype(bf16)

            @pl.when(wenext_ref[w] >= 0)
            def _():
                for cp in weight_copies(wenext_ref[w], 1 - wslot):
                    cp.start()

        xt = _from_planes(xin.at[slot]).astype(bf16)
        h = jax.nn.silu(_dot(xt, w1b[...])) * _dot(xt, w3b[...])
        res = _dot(h.astype(bf16), w2b[...])
        row = lax.broadcasted_iota(jnp.int32, (MOE_TM, D_MODEL), 0)
        mine = jnp.where(row >= wlo_ref[w], row, MOE_TM) < whi_ref[w]

        @pl.when(wfirst_ref[w] == 1)
        def _():
            @pl.when(tile >= 2)
            def _():
                for c in range(N_PLANES):
                    out_copy(0, oslot, c).wait()

            _to_planes(yout.at[oslot], jnp.where(mine, res, 0.0))

        @pl.when(wfirst_ref[w] == 0)
        def _():
            _to_planes(yout.at[oslot], jnp.where(mine, res, _from_planes(yout.at[oslot])))

        @pl.when(wlast_ref[w] == 1)
        def _():
            for c in range(N_PLANES):
                out_copy(tile, oslot, c).start()

        @pl.when(w == n_items - 1)
        def _():
            for c in range(N_PLANES):
                out_copy(0, oslot, c).wait()

            @pl.when(tile >= 1)
            def _():
                for c in range(N_PLANES):
                    out_copy(0, 1 - oslot, c).wait()


def _moe(plan, xs, w1, w3, w2, layer):
    grid_spec = pltpu.PrefetchScalarGridSpec(
        num_scalar_prefetch=len(plan),
        grid=(MOE_MAX_ITEMS,),
        in_specs=[pl.BlockSpec(memory_space=pl.ANY)] * 4,
        out_specs=pl.BlockSpec(memory_space=pl.ANY),
        scratch_shapes=[pltpu.VMEM((2, N_PLANES, MOE_TM, V7X_LANES), f32),
                        pltpu.VMEM((2, N_PLANES, MOE_TM, V7X_LANES), f32),
                        pltpu.SemaphoreType.DMA((2,)), pltpu.SemaphoreType.DMA((2,)),
                        pltpu.VMEM((2, D_MODEL, MOE_HIDDEN), f32), pltpu.VMEM((2, D_MODEL, MOE_HIDDEN), f32),
                        pltpu.VMEM((2, MOE_HIDDEN, D_MODEL), f32), pltpu.SemaphoreType.DMA((2,)),
                        pltpu.VMEM((D_MODEL, MOE_HIDDEN), bf16), pltpu.VMEM((D_MODEL, MOE_HIDDEN), bf16),
                        pltpu.VMEM((MOE_HIDDEN, D_MODEL), bf16)],
    )
    return pl.pallas_call(
        functools.partial(_moe_kernel, layer),
        grid_spec=grid_spec,
        out_shape=jax.ShapeDtypeStruct((N_PAIRS, N_PLANES, V7X_LANES), f32),
        compiler_params=_cparams(("arbitrary",)),
        name="moe",
    )(*plan, xs, w1, w3, w2)


def _combine_kernel(final, pos0_ref, pos1_ref, ys_hbm, x1_ref, route_ref, g_ref, b_ref, *rest):
    if final:
        yp_hbm, ysm_ref, buf, sem, stage, sem_out = rest
    else:
        o_ref, ob_ref, buf, sem = rest
    s = pl.program_id(0)
    nsteps = pl.num_programs(0)

    def gather_copy(row, slot, k, r):
        return pltpu.make_async_copy(ys_hbm.at[row], buf.at[slot, k, :, r, :], sem.at[slot])

    def issue(tile, slot):
        def body(r, c):
            n = tile * CMB_TILE + r
            for k in range(MOE_TOPK):
                gather_copy((pos0_ref, pos1_ref)[k][n], slot, k, r).start(priority=k % 2)
            return c

        lax.fori_loop(0, CMB_TILE, body, 0, unroll=8)

    @pl.when(s == 0)
    def _():
        issue(0, 0)

    @pl.when(s + 1 < nsteps)
    def _():
        issue(s + 1, (s + 1) % 2)

    slot = s % 2
    for k in range(MOE_TOPK):
        pltpu.make_async_copy(buf.at[slot, k], buf.at[slot, k], sem.at[slot]).wait()
    route = route_ref[...]
    moe = route[:, 2:3] * _from_planes(buf.at[slot, 0]) + route[:, 3:4] * _from_planes(buf.at[slot, 1])
    y = _layer_norm_rows(DN_ALPHA * _from_planes(x1_ref) + moe, g_ref[...], b_ref[...])
    if not final:
        o_ref[...] = y
        ob_ref[...] = y.astype(bf16)
        return

    steps_t = CMB_TILE // BATCH

    def out_copy(step, oslot, b):
        t0 = pl.multiple_of(step * steps_t, steps_t)
        return pltpu.make_async_copy(stage.at[oslot, :, b, :], yp_hbm.at[b, pl.ds(t0, steps_t), :], sem_out.at[oslot])

    @pl.when(s < CMB_P_STEPS)
    def _():
        oslot = s % 2

        @pl.when(s >= 2)
        def _():
            for b in range(BATCH):
                out_copy(0, oslot, b).wait()

        stage[oslot] = y.reshape(steps_t, BATCH, D_MODEL)
        for b in range(BATCH):
            out_copy(s, oslot, b).start()

        @pl.when(s == CMB_P_STEPS - 1)
        def _():
            for b in range(BATCH):
                out_copy(0, oslot, b).wait()
                out_copy(0, 1 - oslot, b).wait()

    steps_per_tile = CMB_TILE // DEC_BATCH
    for q in range(N_S // CMB_TILE):
        @pl.when(s == CMB_P_STEPS + q)
        def _():
            for h in range(steps_per_tile):
                ysm_ref[:, q * steps_per_tile + h, :] = y[h * DEC_BATCH:(h + 1) * DEC_BATCH, :]


def _combine(pos, ys, x1, route, g, b, layer, final):
    if final:
        out_specs = [pl.BlockSpec(memory_space=pl.ANY),
                     pl.BlockSpec((DEC_BATCH, DEC_SEQ, D_MODEL), lambda s, *_: (0, 0, 0))]
        out_shape = [jax.ShapeDtypeStruct((BATCH, SEQ, D_MODEL), f32),
                     jax.ShapeDtypeStruct((DEC_BATCH, DEC_SEQ, D_MODEL), f32)]
        extra = [pltpu.VMEM((2, CMB_TILE // BATCH, BATCH, D_MODEL), f32), pltpu.SemaphoreType.DMA((2,))]
    else:
        out_specs = [pl.BlockSpec((CMB_TILE, D_MODEL), lambda s, *_: (s, 0))] * 2
        out_shape = [jax.ShapeDtypeStruct((N_TOK, D_MODEL), f32), jax.ShapeDtypeStruct((N_TOK, D_MODEL), bf16)]
        extra = []
    grid_spec = pltpu.PrefetchScalarGridSpec(
        num_scalar_prefetch=MOE_TOPK,
        grid=(N_TOK // CMB_TILE,),
        in_specs=[pl.BlockSpec(memory_space=pl.ANY),
                  pl.BlockSpec((N_PLANES, CMB_TILE, V7X_LANES), lambda s, *_: (0, s, 0)),
                  pl.BlockSpec((CMB_TILE, ROUTE_LANES), lambda s, *_: (s, 0)),
                  _layer_spec(g, layer), _layer_spec(b, layer)],
        out_specs=out_specs,
        scratch_shapes=[pltpu.VMEM((2, MOE_TOPK, N_PLANES, CMB_TILE, V7X_LANES), f32),
                        pltpu.SemaphoreType.DMA((2,))] + extra,
    )
    return pl.pallas_call(
        functools.partial(_combine_kernel, final),
        grid_spec=grid_spec,
        out_shape=out_shape,
        compiler_params=_cparams(("arbitrary",)),
        name="combine_out" if final else "combine",
    )(*pos, ys, x1, route, g, b)


def _to_rows_kernel(xp_ref, xs_ref, o_ref, ob_ref, t3_ref):
    i = pl.program_id(0)

    @pl.when(i < P_TILES)
    def _():
        for b in range(BATCH):
            t3_ref[:, b, :] = xp_ref[b]
        rows = t3_ref[...].reshape(ROW_TILE, D_MODEL)
        o_ref[...] = rows
        ob_ref[...] = rows.astype(bf16)

    @pl.when(i >= P_TILES)
    def _():
        for t in range(DEC_SEQ):
            rows = xs_ref[:, t, :]
            o_ref[t * DEC_BATCH:(t + 1) * DEC_BATCH, :] = rows
            ob_ref[t * DEC_BATCH:(t + 1) * DEC_BATCH, :] = rows.astype(bf16)


def _to_rows(x_prompt, x_sample):
    return pl.pallas_call(
        _to_rows_kernel,
        grid=(N_TILES,),
        in_specs=[pl.BlockSpec((BATCH, P_TC, D_MODEL), lambda i: (0, jnp.minimum(i, P_TILES - 1), 0)),
                  pl.BlockSpec((DEC_BATCH, DEC_SEQ, D_MODEL), lambda i: (0, 0, 0))],
        out_specs=[pl.BlockSpec((ROW_TILE, D_MODEL), lambda i: (i, 0))] * 2,
        out_shape=[jax.ShapeDtypeStruct((N_TOK, D_MODEL), f32), jax.ShapeDtypeStruct((N_TOK, D_MODEL), bf16)],
        scratch_shapes=[pltpu.VMEM((P_TC, BATCH, D_MODEL), f32)],
        compiler_params=_cparams(("arbitrary",)),
        name="to_rows",
    )(x_prompt, x_sample)


def _lookup(table, idx):
    ar = jnp.arange(MOE_EXPERTS, dtype=jnp.int32).reshape((MOE_EXPERTS,) + (1,) * idx.ndim)
    table = table.reshape(ar.shape)
    return jnp.sum(jnp.where(idx[None] == ar, table, 0), axis=0)


def _dispatch_plan(route_t, cnt):
    i32 = jnp.int32
    e = route_t[0:2].astype(i32)
    rank = route_t[4:6].astype(i32)
    counts = cnt[0, :MOE_EXPERTS].astype(i32)
    ends = jnp.cumsum(counts)
    starts = ends - counts
    pos = _lookup(starts, e) + rank

    first_tile = starts // MOE_TM
    last_tile = (ends - 1) // MOE_TM
    ntiles = jnp.where(counts > 0, last_tile - first_tile + 1, 0)
    item_end = jnp.cumsum(ntiles)
    n_items = item_end[-1]
    w = jnp.minimum(jnp.arange(MOE_MAX_ITEMS, dtype=i32), n_items - 1)
    we = jnp.sum((item_end[None, :] <= w[:, None]).astype(i32), axis=-1)
    wt = _lookup(first_tile, we) + w - _lookup(item_end - ntiles, we)
    wlo = jnp.maximum(_lookup(starts, we) - wt * MOE_TM, 0)
    whi = jnp.minimum(_lookup(ends, we) - wt * MOE_TM, MOE_TM)
    changes = (wt[1:] != wt[:-1]).astype(i32)
    wfirst = jnp.concatenate([jnp.ones((1,), i32), changes])
    wlast = jnp.concatenate([changes, jnp.ones((1,), i32)])
    wlast = jnp.where(jnp.arange(MOE_MAX_ITEMS, dtype=i32) == n_items - 1, 1, wlast)
    wefirst = jnp.concatenate([jnp.ones((1,), i32), (we[1:] != we[:-1]).astype(i32)])
    weslot = (jnp.cumsum(wefirst) - 1) % 2
    ar = jnp.arange(MOE_EXPERTS, dtype=i32)
    later = (ar[None, :] > ar[:, None]) & (counts[None, :] > 0)
    next_expert = jnp.min(jnp.where(later, ar[None, :], MOE_EXPERTS), axis=1)
    wenext = _lookup(jnp.where(next_expert < MOE_EXPERTS, next_expert, -1), we)
    return (pos[0], pos[1]), (wt, we, wlo, whi, wfirst, wlast, wefirst, weslot, wenext, n_items.reshape(1))


def _block_diag(w):
    n, a, b = w.shape
    eye = jnp.eye(n, dtype=w.dtype)
    return (w[:, :, None, :] * eye[:, None, :, None]).reshape(n * a, n * b)


def _ssm_params(a_re, a_im, log_dt, b_re, b_im, c_re, c_im):
    ar, ai = a_re, a_im
    dt = jnp.exp(log_dt)[:, None]
    mag = jnp.exp(ar * dt)
    lb_re = mag * jnp.cos(ai * dt)
    lb_im = mag * jnp.sin(ai * dt)
    den = ar * ar + ai * ai
    nr = lb_re - 1.0
    coef_re = (nr * ar + lb_im * ai) / den
    coef_im = (lb_im * ar - nr * ai) / den
    bb_re = coef_re[..., None] * b_re - coef_im[..., None] * b_im
    bb_im = coef_re[..., None] * b_im + coef_im[..., None] * b_re
    gk = SSM_GROUPS // SSM_KB

    def diag_blocks(w):
        return jnp.stack([_block_diag(w[k * gk:(k + 1) * gk]) for k in range(SSM_KB)]).astype(bf16)

    bbre = diag_blocks(bb_re.transpose(0, 2, 1))
    bbim = diag_blocks(bb_im.transpose(0, 2, 1))
    ccre = diag_blocks(c_re.transpose(0, 2, 1))
    ccim = diag_blocks(c_im.transpose(0, 2, 1))
    return (lb_re.reshape(1, SSM_LANES), lb_im.reshape(1, SSM_LANES), bbre, bbim, ccre, ccim)


def kernel(x_prompt, x_sample, state_ret, state_ssm_re, state_ssm_im, state_lru, state_conv, w_in, ret_gn_g, ret_gn_b, ssm_a_re, ssm_a_im, ssm_log_dt, ssm_b_re, ssm_b_im, ssm_c_re, ssm_c_im, ssm_d, ssm_w_glu, lru_conv_w, lru_conv_b, lru_wa, lru_ba, lru_wx, lru_bx, lru_lambda, w_branch, w_out, ln1_g, ln1_b, moe_w_group, moe_b_group, moe_w_expert, moe_b_expert, moe_w1, moe_w3, moe_w2, ln2_g, ln2_b):
    x, xb = _to_rows(x_prompt, x_sample)
    rope = _rope_tables()
    tabs_p = _ret_tables(RET_SUB_T)
    tabs_s = _ret_tables(DEC_SEQ)
    rows = lambda v: v.reshape(DEPTH, 1, -1)
    w_bf = w_in.astype(bf16)
    s0 = state_ret.reshape(DEPTH, DEC_BATCH, 2, 2 * RET_DK, RET_DV)
    ssm_p = jax.vmap(_ssm_params)(ssm_a_re, ssm_a_im, ssm_log_dt, ssm_b_re, ssm_b_im, ssm_c_re, ssm_c_im)
    h0re = state_ssm_re.reshape(DEPTH, DEC_BATCH, SSM_LANES)
    h0im = state_ssm_im.reshape(DEPTH, DEC_BATCH, SSM_LANES)
    conv0 = state_conv.transpose(0, 2, 1, 3).reshape(DEPTH, LRU_HIST_S, BRANCH_W)
    lru_wa_bd = jax.vmap(_block_diag)(lru_wa).astype(bf16)
    lru_wx_bd = jax.vmap(_block_diag)(lru_wx).astype(bf16)
    pad_w = jnp.zeros((DEPTH, D_MODEL, ROUTE_LANES - MOE_GROUPS - MOE_EXPERTS), f32)
    wr = jnp.concatenate([moe_w_group, moe_w_expert, pad_w], axis=-1).astype(bf16)
    br = jnp.concatenate([moe_b_group, moe_b_expert, pad_w[:, 0, :]], axis=-1).reshape(DEPTH, 1, ROUTE_LANES)
    wb_bf, wo_bf, wglu_bf = w_branch.astype(bf16), w_out.astype(bf16), ssm_w_glu.astype(bf16)

    outs = [[] for _ in range(10)]
    for l in range(DEPTH):
        y_ret, ret_p, ret_s = _retention(xb, w_bf, rope, s0, l, tabs_p, tabs_s, rows(ret_gn_g), rows(ret_gn_b))
        y_ssm, re_p, im_p, re_s, im_s = _ssm(xb, w_bf, l, h0re, h0im, *ssm_p, rows(ssm_d), wglu_bf)
        y_lru, lru_p, conv_p, lru_s, conv_s = _lru(
            xb, w_bf, l, state_lru, conv0, lru_conv_w, rows(lru_conv_b),
            lru_wa_bd, rows(lru_ba), lru_wx_bd, rows(lru_bx), rows(lru_lambda))
        x1, route, route_t, cnt = _merge(y_ret, y_ssm, y_lru, xb, w_bf, l, x, wb_bf, wo_bf,
                                         rows(ln1_g), rows(ln1_b), wr, br)

        pos, plan = _dispatch_plan(route_t, cnt)
        xs = _dispatch(pos, x1)
        ys = _moe(plan, xs, moe_w1, moe_w3, moe_w2, l)
        x, xb = _combine(pos, ys, x1, route, rows(ln2_g), rows(ln2_b), l, final=(l == DEPTH - 1))

        outs[0].append(ret_p.reshape(BATCH, RET_HEADS, RET_DK, RET_DV))
        outs[1].append(re_p.reshape(BATCH, SSM_GROUPS, SSM_STATE))
        outs[2].append(im_p.reshape(BATCH, SSM_GROUPS, SSM_STATE))
        outs[3].append(lru_p)
        outs[4].append(conv_p.reshape(CONV_W - 1, BATCH, BRANCH_W).transpose(1, 0, 2))
        outs[5].append(ret_s.reshape(DEC_BATCH, RET_HEADS, RET_DK, RET_DV))
        outs[6].append(re_s.reshape(DEC_BATCH, SSM_GROUPS, SSM_STATE))
        outs[7].append(im_s.reshape(DEC_BATCH, SSM_GROUPS, SSM_STATE))
        outs[8].append(lru_s)
        outs[9].append(conv_s.reshape(CONV_W - 1, DEC_BATCH, BRANCH_W).transpose(1, 0, 2))

    y_prompt, y_sample = x, xb
    return (y_prompt, y_sample) + tuple(jnp.stack(o) for o in outs)
```

```python
import functools

import jax
import jax.numpy as jnp
import numpy as np
from jax import lax
from jax.experimental import pallas as pl
from jax.experimental.pallas import tpu as pltpu

f32 = jnp.float32
bf16 = jnp.bfloat16

D_MODEL = 1024
BATCH = 8
SEQ = 2048
DEPTH = 2
DEC_BATCH = 128
DEC_SEQ = 8
PAST_LEN = 16384
BRANCH_W = 512
RET_HEADS = 4
RET_DK = 64
RET_DV = 128
ROPE_BASE = 10000.0
SSM_GROUPS = 32
SSM_STATE = 64
SSM_LANES = SSM_GROUPS * SSM_STATE
CONV_W = 4
LRU_C = 8.0
MOE_GROUPS = 4
MOE_PER_GROUP = 8
MOE_EXPERTS = 32
MOE_TOPK = 2
MOE_HIDDEN = 512
DN_ALPHA = (2.0 * DEPTH) ** 0.25
LN_EPS = 1e-5

V7X_SUBLANES = 8
V7X_LANES = 128
V7X_VMEM_LIMIT = 56 * 1024 * 1024

N_P = BATCH * SEQ
N_S = DEC_BATCH * DEC_SEQ
N_TOK = N_P + N_S
ROW_TILE = 1024
P_TILES = N_P // ROW_TILE
N_TILES = N_TOK // ROW_TILE
P_TC = ROW_TILE // BATCH
RET_SUB_T = 32
RET_SUB_R = RET_SUB_T * BATCH
RET_SUBS = ROW_TILE // RET_SUB_R
S_BLOCKS = DEC_BATCH // BATCH
MERGE_TILE = 512
MOE_TM = 512
N_PAIRS = N_TOK * MOE_TOPK
MOE_MAX_ITEMS = N_PAIRS // MOE_TM + MOE_EXPERTS - 1
DSP_TILE = 1024
CMB_TILE = 256
CMB_P_STEPS = N_P // CMB_TILE
ROUTE_LANES = 128


PROJ_RC = 256
RET_QK_W = RET_HEADS * RET_DK
RET_COLS, RET_BLK = 2 * RET_QK_W + 2 * BRANCH_W, 0
SSM_COLS, SSM_BLK = 512, 3
LRU_COLS, LRU_BLK = 1024, 2
GATE_COLS, GATE_BLK = 3072, 1


def _cparams(sem):
    return pltpu.CompilerParams(dimension_semantics=sem, vmem_limit_bytes=V7X_VMEM_LIMIT)


N_PLANES = D_MODEL // V7X_LANES


def _to_planes(ref, rows):
    for c in range(N_PLANES):
        ref[c] = rows[:, c * V7X_LANES:(c + 1) * V7X_LANES]


def _from_planes(ref):
    return jnp.concatenate([ref[c] for c in range(N_PLANES)], axis=1)


def _layer_spec(stacked, layer):
    shape = (None,) + stacked.shape[1:]
    return pl.BlockSpec(shape, lambda *_: (layer,) + (0,) * (stacked.ndim - 1), pipeline_mode=pl.Buffered(1))


def _const_spec(block_shape, index):
    return pl.BlockSpec(block_shape, lambda *_: index, pipeline_mode=pl.Buffered(1))


def _dot(a, b):
    return jnp.dot(a, b, preferred_element_type=f32)


def _dot_nt(a, b):
    return lax.dot_general(a, b, (((1,), (1,)), ((), ())), preferred_element_type=f32)


def _dot_tn(a, b):
    return lax.dot_general(a, b, (((0,), (0,)), ((), ())), preferred_element_type=f32)


def _layer_norm_rows(x, g, b):
    mu = jnp.mean(x, -1, keepdims=True)
    xc = x - mu
    var = jnp.mean(xc * xc, -1, keepdims=True)
    return xc * lax.rsqrt(var + LN_EPS) * g + b


def _project_rows(xb_ref, w_ref, z_ref):
    for rc in range(ROW_TILE // PROJ_RC):
        rs = slice(rc * PROJ_RC, (rc + 1) * PROJ_RC)
        z_ref[rs, :] = _dot(xb_ref[rs, :], w_ref[0])


def _ret_block(q, k, v, g, cosb, sinb, mask_ref, qdec_ref, kdec_ref, cdec_ref, scat_ref, gng, gnb):
    rows = q.shape[0]
    lane_qk = lax.broadcasted_iota(jnp.int32, (rows, 2 * V7X_LANES), 1)
    first_half = (lane_qk & (RET_DK - 1)) < (RET_DK // 2)

    def rope(x):
        partner = jnp.where(first_half, pltpu.roll(x, 2 * V7X_LANES - RET_DK // 2, 1),
                            pltpu.roll(x, RET_DK // 2, 1))
        return x * cosb + partner * sinb

    q = rope(q)
    k = rope(k) * (RET_DK ** -0.5)
    kd = k * kdec_ref[...]
    lane = lax.broadcasted_iota(jnp.int32, (rows, V7X_LANES), 1)
    row_b = lax.broadcasted_iota(jnp.int32, (rows, V7X_LANES), 0) & (BATCH - 1)
    outs = []
    for p in range(2):
        qp = q[:, p * V7X_LANES:(p + 1) * V7X_LANES]
        kp = k[:, p * V7X_LANES:(p + 1) * V7X_LANES].astype(bf16)
        kdp = kd[:, p * V7X_LANES:(p + 1) * V7X_LANES]
        s_old = scat_ref[p]
        s_bf = s_old.astype(bf16)
        s_new = s_old * jnp.concatenate([cdec_ref[p]] * BATCH, axis=1)
        for hh in range(2):
            h = 2 * p + hh
            head_lanes = (lane >= RET_DK) if hh else (lane < RET_DK)
            qh = jnp.where(head_lanes, qp, 0.0).astype(bf16)
            kdh = jnp.where(head_lanes, kdp, 0.0).astype(bf16)
            vh = v[:, h * RET_DV:(h + 1) * RET_DV]
            vh_bf = vh.astype(bf16)
            sc = _dot_nt(qh, kp) * mask_ref[h]
            o = _dot(sc.astype(bf16), vh_bf)
            cross = _dot(qh, s_bf)
            oc = jnp.zeros((rows, RET_DV), f32)
            for b in range(BATCH):
                oc = oc + jnp.where(row_b == b, cross[:, b * RET_DV:(b + 1) * RET_DV], 0.0)
            o = o + oc * qdec_ref[h]
            vcat = jnp.concatenate([jnp.where(row_b == b, vh_bf, jnp.zeros_like(vh_bf))
                                    for b in range(BATCH)], axis=1)
            s_new = s_new + _dot_tn(kdh, vcat)
            mu = jnp.mean(o, -1, keepdims=True)
            oc2 = o - mu
            var = jnp.mean(oc2 * oc2, -1, keepdims=True)
            outs.append(oc2 * lax.rsqrt(var + LN_EPS))
        scat_ref[p] = s_new
    o = jnp.concatenate(outs, axis=1) * gng + gnb
    return jax.nn.silu(g) * o


def _per_step_rows(tab_ref, t0, steps):
    return jnp.concatenate(
        [jnp.broadcast_to(tab_ref[t0 + t:t0 + t + 1, :], (BATCH, tab_ref.shape[1])) for t in range(steps)], axis=0)


def _ret_kernel(xb_ref, w_ref, cos_ref, sin_ref, cos_s_ref, sin_s_ref, s0_ref,
                mask_p_ref, qdec_p_ref, kdec_p_ref, cdec_p_ref,
                mask_s_ref, qdec_s_ref, kdec_s_ref, cdec_s_ref,
                gng_ref, gnb_ref,
                y_ref, retp_ref, rets_ref, scat_ref, z_ref):
    i = pl.program_id(0)
    gng = gng_ref[...]
    gnb = gnb_ref[...]
    q_cols, k_cols = slice(0, RET_QK_W), slice(RET_QK_W, 2 * RET_QK_W)
    v_cols = slice(2 * RET_QK_W, 2 * RET_QK_W + BRANCH_W)
    g_cols = slice(2 * RET_QK_W + BRANCH_W, RET_COLS)

    @pl.when(i == 0)
    def _():
        scat_ref[...] = jnp.zeros_like(scat_ref)

    @pl.when(i <= P_TILES)
    def _():
        _project_rows(xb_ref, w_ref, z_ref)

    @pl.when(i < P_TILES)
    def _():
        for sc in range(RET_SUBS):
            r0 = sc * RET_SUB_R
            rs = slice(r0, r0 + RET_SUB_R)
            y_ref[rs, :] = _ret_block(
                z_ref[rs, q_cols], z_ref[rs, k_cols], z_ref[rs, v_cols], z_ref[rs, g_cols],
                _per_step_rows(cos_ref, sc * RET_SUB_T, RET_SUB_T), _per_step_rows(sin_ref, sc * RET_SUB_T, RET_SUB_T),
                mask_p_ref, qdec_p_ref, kdec_p_ref, cdec_p_ref, scat_ref, gng, gnb)

    @pl.when(i == P_TILES - 1)
    def _():
        for b in range(BATCH):
            for p in range(2):
                retp_ref[b, p] = scat_ref[p, :, b * RET_DV:(b + 1) * RET_DV]

    @pl.when(i >= P_TILES)
    def _():
        bb = i - P_TILES
        for b in range(BATCH):
            for p in range(2):
                scat_ref[p, :, b * RET_DV:(b + 1) * RET_DV] = s0_ref[0, b, p]

        def rows_of(cols):
            return jnp.concatenate(
                [z_ref[pl.ds(pl.multiple_of(t * DEC_BATCH + bb * BATCH, BATCH), BATCH), cols]
                 for t in range(DEC_SEQ)], axis=0)

        y = _ret_block(
            rows_of(q_cols), rows_of(k_cols), rows_of(v_cols),
            rows_of(g_cols), _per_step_rows(cos_s_ref, 0, DEC_SEQ), _per_step_rows(sin_s_ref, 0, DEC_SEQ),
            mask_s_ref, qdec_s_ref, kdec_s_ref, cdec_s_ref, scat_ref, gng, gnb)
        for t in range(DEC_SEQ):
            y_ref[pl.ds(pl.multiple_of(t * DEC_BATCH + bb * BATCH, BATCH), BATCH), :] = (
                y[t * BATCH:(t + 1) * BATCH, :])
        for b in range(BATCH):
            for p in range(2):
                rets_ref[b, p] = scat_ref[p, :, b * RET_DV:(b + 1) * RET_DV]


def _ret_tables(tc):
    rows = tc * BATCH
    nf = np.float32
    log_g = np.log1p(-np.exp2(nf(-5.0) - np.arange(RET_HEADS, dtype=nf))).astype(nf)
    t_idx = (np.arange(rows) // BATCH).astype(nf)
    b_idx = np.arange(rows) % BATCH
    rel = t_idx[:, None] - t_idx[None, :]
    same = b_idx[:, None] == b_idx[None, :]
    decay = np.exp(log_g[:, None, None] * np.maximum(rel, nf(0.0)))
    mask = np.where((rel >= 0) & same, decay, nf(0.0))
    qdec = np.exp(log_g[:, None] * (t_idx[None, :] + nf(1.0)))
    qdec = np.broadcast_to(qdec[:, :, None], (RET_HEADS, rows, RET_DV))
    kdec = np.exp(log_g[:, None] * (nf(tc - 1.0) - t_idx[None, :]))
    kdec = np.broadcast_to(kdec.T[:, :, None], (rows, RET_HEADS, RET_DK)).reshape(rows, RET_HEADS * RET_DK)
    cdec = np.exp(log_g * nf(tc))
    cdec = np.broadcast_to(cdec[:, None, None], (RET_HEADS, RET_DK, RET_DV)).reshape(2, 2 * RET_DK, RET_DV)
    return tuple(jnp.asarray(np.ascontiguousarray(a), dtype=f32) for a in (mask, qdec, kdec, cdec))


def _rope_tables():
    half = RET_DK // 2
    inv = ROPE_BASE ** (-jnp.arange(half, dtype=f32) / half)
    pos_p = jnp.arange(SEQ, dtype=f32)
    pos_s = PAST_LEN + jnp.arange(DEC_SEQ, dtype=f32)

    def tab(pos):
        ang = pos[:, None] * inv[None, :]
        cos = jnp.cos(ang)
        sin = jnp.sin(ang)
        cos_h = jnp.concatenate([cos, cos], axis=1)
        sin_h = jnp.concatenate([-sin, sin], axis=1)
        return jnp.tile(cos_h, (1, RET_HEADS)), jnp.tile(sin_h, (1, RET_HEADS))

    return tab(pos_p) + tab(pos_s)


def _retention(xb, w_bf, rope, s0_s, layer, tabs_p, tabs_s, gng, gnb):
    n_steps = P_TILES + S_BLOCKS
    tile = lambda i: jnp.minimum(i, P_TILES)
    sblk = lambda i: jnp.maximum(i - P_TILES, 0)
    full = lambda a: pl.BlockSpec(a.shape, lambda i, _n=a.ndim: (0,) * _n)
    state_blk = (BATCH, 2, 2 * RET_DK, RET_DV)
    cos_p, sin_p, cos_s, sin_s = rope
    ptile = lambda i: jnp.minimum(i, P_TILES - 1)
    ins = [xb, w_bf, cos_p, sin_p, cos_s, sin_s, s0_s, *tabs_p, *tabs_s, gng, gnb]
    in_specs = [
        pl.BlockSpec((ROW_TILE, D_MODEL), lambda i: (tile(i), 0)),
        _const_spec((1, D_MODEL, RET_COLS), (layer, 0, RET_BLK)),
        pl.BlockSpec((P_TC, RET_QK_W), lambda i: (ptile(i), 0)),
        pl.BlockSpec((P_TC, RET_QK_W), lambda i: (ptile(i), 0)),
        full(cos_s), full(sin_s),
        pl.BlockSpec((1,) + state_blk, lambda i: (layer, sblk(i), 0, 0, 0)),
    ] + [full(a) for a in (*tabs_p, *tabs_s)] + [_layer_spec(gng, layer), _layer_spec(gnb, layer)]
    return pl.pallas_call(
        _ret_kernel,
        grid=(n_steps,),
        in_specs=in_specs,
        out_specs=[pl.BlockSpec((ROW_TILE, BRANCH_W), lambda i: (tile(i), 0)),
                   pl.BlockSpec(state_blk, lambda i: (0, 0, 0, 0)),
                   pl.BlockSpec(state_blk, lambda i: (sblk(i), 0, 0, 0))],
        out_shape=[jax.ShapeDtypeStruct((N_TOK, BRANCH_W), f32),
                   jax.ShapeDtypeStruct((BATCH, 2, 2 * RET_DK, RET_DV), f32),
                   jax.ShapeDtypeStruct((DEC_BATCH, 2, 2 * RET_DK, RET_DV), f32)],
        scratch_shapes=[pltpu.VMEM((2, 2 * RET_DK, BATCH * RET_DV), f32),
                        pltpu.VMEM((ROW_TILE, RET_COLS), f32)],
        compiler_params=_cparams(("arbitrary",)),
        name="retention",
    )(*ins)


SSM_LB = 512
SSM_RC = 256
SSM_KB = 2
SSM_KB_U = BRANCH_W // SSM_KB
SSM_KB_H = SSM_LANES // SSM_KB


def _ssm_scan(bre_ref, bim_ref, lre_ref, lim_ref, h_re0, h_im0, row0, nb_rows, steps, lb, unroll):
    ls = slice(lb * SSM_LB, (lb + 1) * SSM_LB)
    a_re = jnp.broadcast_to(lre_ref[:, ls], (V7X_SUBLANES, SSM_LB))
    a_im = jnp.broadcast_to(lim_ref[:, ls], (V7X_SUBLANES, SSM_LB))

    def step(t, carry):
        h_re, h_im = carry
        r = pl.multiple_of(row0 + t * nb_rows, V7X_SUBLANES)
        n_re = a_re * h_re - a_im * h_im + bre_ref[pl.ds(r, V7X_SUBLANES), ls]
        n_im = a_re * h_im + a_im * h_re + bim_ref[pl.ds(r, V7X_SUBLANES), ls]
        bre_ref[pl.ds(r, V7X_SUBLANES), ls] = n_re
        bim_ref[pl.ds(r, V7X_SUBLANES), ls] = n_im
        return n_re, n_im

    return lax.fori_loop(0, steps, step, (h_re0, h_im0), unroll=unroll)


def _ssm_kernel(xb_ref, w_ref, h0re_ref, h0im_ref, lre_ref, lim_ref, bbre_ref, bbim_ref, ccre_ref, ccim_ref,
                d_ref, wglu_ref,
                y_ref, pre_ref, pim_ref, sre_ref, sim_ref,
                bre_ref, bim_ref, hre_ref, him_ref, zs_ref):
    i = pl.program_id(0)
    _project_rows(xb_ref, w_ref, zs_ref)

    def inputs(rc):
        rs = slice(rc * SSM_RC, (rc + 1) * SSM_RC)
        ub = zs_ref[rs, :].astype(bf16)
        for k in range(SSM_KB):
            uk = ub[:, k * SSM_KB_U:(k + 1) * SSM_KB_U]
            hs = slice(k * SSM_KB_H, (k + 1) * SSM_KB_H)
            bre_ref[rs, hs] = _dot(uk, bbre_ref[k])
            bim_ref[rs, hs] = _dot(uk, bbim_ref[k])

    def outputs(rc):
        rs = slice(rc * SSM_RC, (rc + 1) * SSM_RC)
        ch = []
        for k in range(SSM_KB):
            hs = slice(k * SSM_KB_H, (k + 1) * SSM_KB_H)
            ch.append(_dot(bre_ref[rs, hs].astype(bf16), ccre_ref[k]) - _dot(bim_ref[rs, hs].astype(bf16), ccim_ref[k]))
        y = jnp.concatenate(ch, axis=1) + d_ref[...] * zs_ref[rs, :]
        zz = jax.nn.gelu(y)
        y_ref[rs, :] = zz * jax.nn.sigmoid(_dot(zz.astype(bf16), wglu_ref[...]))

    @pl.when(i == 0)
    def _():
        hre_ref[...] = jnp.zeros_like(hre_ref)
        him_ref[...] = jnp.zeros_like(him_ref)

    @pl.when(i < P_TILES)
    def _():
        steps = SSM_RC // BATCH
        for rc in range(ROW_TILE // SSM_RC):
            inputs(rc)
        for rc in range(ROW_TILE // SSM_RC):
            for lb in range(SSM_LANES // SSM_LB):
                ls = slice(lb * SSM_LB, (lb + 1) * SSM_LB)
                a_re = jnp.broadcast_to(lre_ref[:, ls], (BATCH, SSM_LB))
                a_im = jnp.broadcast_to(lim_ref[:, ls], (BATCH, SSM_LB))
                h_re, h_im = hre_ref[:, ls], him_ref[:, ls]
                for t in range(steps):
                    rows = slice(rc * SSM_RC + t * BATCH, rc * SSM_RC + (t + 1) * BATCH)
                    h_re, h_im = (a_re * h_re - a_im * h_im + bre_ref[rows, ls],
                                  a_re * h_im + a_im * h_re + bim_ref[rows, ls])
                    bre_ref[rows, ls] = h_re
                    bim_ref[rows, ls] = h_im
                hre_ref[:, ls] = h_re
                him_ref[:, ls] = h_im
            outputs(rc)
        pre_ref[...] = hre_ref[...]
        pim_ref[...] = him_ref[...]

    @pl.when(i >= P_TILES)
    def _():
        for rc in range(ROW_TILE // SSM_RC):
            inputs(rc)

        def per_row_tile(rt, c):
            r0 = pl.multiple_of(rt * V7X_SUBLANES, V7X_SUBLANES)
            for lb in range(SSM_LANES // SSM_LB):
                ls = slice(lb * SSM_LB, (lb + 1) * SSM_LB)
                _ssm_scan(bre_ref, bim_ref, lre_ref, lim_ref,
                          h0re_ref[pl.ds(r0, V7X_SUBLANES), ls], h0im_ref[pl.ds(r0, V7X_SUBLANES), ls],
                          r0, DEC_BATCH, DEC_SEQ, lb, True)
            return c

        lax.fori_loop(0, DEC_BATCH // V7X_SUBLANES, per_row_tile, 0)
        last = (DEC_SEQ - 1) * DEC_BATCH
        sre_ref[...] = bre_ref[last:last + DEC_BATCH, :]
        sim_ref[...] = bim_ref[last:last + DEC_BATCH, :]
        for rc in range(ROW_TILE // SSM_RC):
            outputs(rc)


def _ssm(xb, w_bf, layer, h0re, h0im, lre, lim, bbre, bbim, ccre, ccim, dvec, wglu):
    full = lambda a: _layer_spec(a, layer)
    consts = (h0re, h0im, lre, lim, bbre, bbim, ccre, ccim, dvec, wglu)
    return pl.pallas_call(
        _ssm_kernel,
        grid=(N_TILES,),
        in_specs=[pl.BlockSpec((ROW_TILE, D_MODEL), lambda i: (i, 0)),
                  _const_spec((1, D_MODEL, SSM_COLS), (layer, 0, SSM_BLK))] + [full(a) for a in consts],
        out_specs=[pl.BlockSpec((ROW_TILE, BRANCH_W), lambda i: (i, 0)),
                   pl.BlockSpec((BATCH, SSM_LANES), lambda i: (0, 0)),
                   pl.BlockSpec((BATCH, SSM_LANES), lambda i: (0, 0)),
                   pl.BlockSpec((DEC_BATCH, SSM_LANES), lambda i: (0, 0)),
                   pl.BlockSpec((DEC_BATCH, SSM_LANES), lambda i: (0, 0))],
        out_shape=[jax.ShapeDtypeStruct((N_TOK, BRANCH_W), f32),
                   jax.ShapeDtypeStruct((BATCH, SSM_LANES), f32),
                   jax.ShapeDtypeStruct((BATCH, SSM_LANES), f32),
                   jax.ShapeDtypeStruct((DEC_BATCH, SSM_LANES), f32),
                   jax.ShapeDtypeStruct((DEC_BATCH, SSM_LANES), f32)],
        scratch_shapes=[pltpu.VMEM((ROW_TILE, SSM_LANES), f32), pltpu.VMEM((ROW_TILE, SSM_LANES), f32),
                        pltpu.VMEM((BATCH, SSM_LANES), f32), pltpu.VMEM((BATCH, SSM_LANES), f32),
                        pltpu.VMEM((ROW_TILE, SSM_COLS), f32)],
        compiler_params=_cparams(("arbitrary",)),
        name="ssm",
    )(xb, w_bf, *consts)


LRU_HIST_P = (CONV_W - 1) * BATCH
LRU_HIST_S = (CONV_W - 1) * DEC_BATCH


def _lru_gates(xe_ref, nb_rows, cw_ref, cb_ref, wa_ref, ba_ref, wx_ref, bx_ref, lam_ref):
    xc = cb_ref[...] + xe_ref[0:ROW_TILE, :] * cw_ref[0:1, :]
    for j in range(1, CONV_W):
        xc = xc + xe_ref[j * nb_rows:j * nb_rows + ROW_TILE, :] * cw_ref[j:j + 1, :]
    xcb = xc.astype(bf16)
    r = jax.nn.sigmoid(_dot(xcb, wa_ref[...]) + ba_ref[...])
    ig = jax.nn.sigmoid(_dot(xcb, wx_ref[...]) + bx_ref[...])
    log_a = -LRU_C * r * jax.nn.softplus(-lam_ref[...])
    a = jnp.exp(log_a)
    b = jnp.sqrt(-jnp.tanh(log_a) * (a * a + 1.0)) * (ig * xc)
    return a, b


def _lru_kernel(xb_ref, w_ref, h0_ref, conv0_ref, cw_ref, cb_ref, wa_ref, ba_ref, wx_ref, bx_ref, lam_ref,
                y_ref, hp_ref, convp_ref, hs_ref, convs_ref,
                xe_ref, a_ref, b_ref, hc_ref, z_ref):
    i = pl.program_id(0)
    params = (cw_ref, cb_ref, wa_ref, ba_ref, wx_ref, bx_ref, lam_ref)
    _project_rows(xb_ref, w_ref, z_ref)
    zx_ref = z_ref.at[:, 0:BRANCH_W]
    zg_ref = z_ref.at[:, BRANCH_W:2 * BRANCH_W]

    @pl.when(i == 0)
    def _():
        xe_ref[0:LRU_HIST_P, :] = jnp.zeros((LRU_HIST_P, BRANCH_W), f32)
        hc_ref[...] = jnp.zeros_like(hc_ref)

    @pl.when(i < P_TILES)
    def _():
        xe_ref[LRU_HIST_P:LRU_HIST_P + ROW_TILE, :] = zx_ref[...]
        a, b = _lru_gates(xe_ref, BATCH, *params)
        a_ref[...] = a
        b_ref[...] = b
        hist = xe_ref[ROW_TILE:ROW_TILE + LRU_HIST_P, :]
        xe_ref[0:LRU_HIST_P, :] = hist
        convp_ref[...] = hist

        def step(t, h):
            r = pl.multiple_of(t * BATCH, BATCH)
            h = a_ref[pl.ds(r, BATCH), :] * h + b_ref[pl.ds(r, BATCH), :]
            b_ref[pl.ds(r, BATCH), :] = h
            return h

        h = lax.fori_loop(0, P_TC, step, hc_ref[...], unroll=8)
        hc_ref[...] = h
        hp_ref[...] = h

    @pl.when(i >= P_TILES)
    def _():
        xe_ref[0:LRU_HIST_S, :] = conv0_ref[...]
        xe_ref[LRU_HIST_S:LRU_HIST_S + ROW_TILE, :] = zx_ref[...]
        a, b = _lru_gates(xe_ref, DEC_BATCH, *params)
        a_ref[...] = a
        b_ref[...] = b
        convs_ref[...] = xe_ref[ROW_TILE:ROW_TILE + LRU_HIST_S, :]

        def per_row_tile(rt, c):
            r0 = pl.multiple_of(rt * V7X_SUBLANES, V7X_SUBLANES)
            h = h0_ref[pl.ds(r0, V7X_SUBLANES), :]
            for t in range(DEC_SEQ):
                r = pl.multiple_of(t * DEC_BATCH + r0, V7X_SUBLANES)
                h = a_ref[pl.ds(r, V7X_SUBLANES), :] * h + b_ref[pl.ds(r, V7X_SUBLANES), :]
                b_ref[pl.ds(r, V7X_SUBLANES), :] = h
            return c

        lax.fori_loop(0, DEC_BATCH // V7X_SUBLANES, per_row_tile, 0)
        last = (DEC_SEQ - 1) * DEC_BATCH
        hs_ref[...] = b_ref[last:last + DEC_BATCH, :]

    y_ref[...] = b_ref[...] * jax.nn.gelu(zg_ref[...])


def _lru(xb, w_bf, layer, h0, conv0, cw, cb, wa, ba, wx, bx, lam):
    full = lambda a: _layer_spec(a, layer)
    consts = (h0, conv0, cw, cb, wa, ba, wx, bx, lam)
    return pl.pallas_call(
        _lru_kernel,
        grid=(N_TILES,),
        in_specs=[pl.BlockSpec((ROW_TILE, D_MODEL), lambda i: (i, 0)),
                  _const_spec((1, D_MODEL, LRU_COLS), (layer, 0, LRU_BLK))] + [full(a) for a in consts],
        out_specs=[pl.BlockSpec((ROW_TILE, BRANCH_W), lambda i: (i, 0)),
                   pl.BlockSpec((BATCH, BRANCH_W), lambda i: (0, 0)),
                   pl.BlockSpec((LRU_HIST_P, BRANCH_W), lambda i: (0, 0)),
                   pl.BlockSpec((DEC_BATCH, BRANCH_W), lambda i: (0, 0)),
                   pl.BlockSpec((LRU_HIST_S, BRANCH_W), lambda i: (0, 0))],
        out_shape=[jax.ShapeDtypeStruct((N_TOK, BRANCH_W), f32),
                   jax.ShapeDtypeStruct((BATCH, BRANCH_W), f32),
                   jax.ShapeDtypeStruct((LRU_HIST_P, BRANCH_W), f32),
                   jax.ShapeDtypeStruct((DEC_BATCH, BRANCH_W), f32),
                   jax.ShapeDtypeStruct((LRU_HIST_S, BRANCH_W), f32)],
        scratch_shapes=[pltpu.VMEM((ROW_TILE + LRU_HIST_S, BRANCH_W), f32),
                        pltpu.VMEM((ROW_TILE, BRANCH_W), f32), pltpu.VMEM((ROW_TILE, BRANCH_W), f32),
                        pltpu.VMEM((BATCH, BRANCH_W), f32),
                        pltpu.VMEM((ROW_TILE, LRU_COLS), f32)],
        compiler_params=_cparams(("arbitrary",)),
        name="lru",
    )(xb, w_bf, *consts)


def _merge_kernel(yr_ref, ys_ref, yl_ref, xb_ref, wg_ref, x_ref, wb_ref, wo_ref, g_ref, b_ref, wr_ref, br_ref,
                  x1_ref, route_ref, route_t_ref, cnt_out_ref, cnt_ref):
    merged = jnp.zeros((MERGE_TILE, D_MODEL), f32)
    xb = xb_ref[...]
    for n, y_ref in enumerate((yr_ref, ys_ref, yl_ref)):
        proj = _dot(y_ref[...].astype(bf16), wb_ref[n])
        gate = jax.nn.sigmoid(_dot(xb, wg_ref[0, :, n * D_MODEL:(n + 1) * D_MODEL]))
        merged = merged + gate * proj
    mix = _dot(merged.astype(bf16), wo_ref[...])
    x1 = _layer_norm_rows(DN_ALPHA * x_ref[...] + mix, g_ref[...], b_ref[...])
    _to_planes(x1_ref, x1)

    logits = _dot(x1.astype(bf16), wr_ref[...]) + br_ref[...]
    lane = lax.broadcasted_iota(jnp.int32, (MERGE_TILE, ROUTE_LANES), 1).astype(f32)
    big = jnp.float32(ROUTE_LANES)
    neg = jnp.float32(-jnp.inf)
    is_g = lane < MOE_GROUPS
    lg = jnp.where(is_g, logits, neg)
    mg = jnp.max(lg, -1, keepdims=True)
    gsel = jnp.min(jnp.where(lg == mg, lane, big), -1, keepdims=True)
    sum_g = jnp.sum(jnp.where(is_g, jnp.exp(lg - mg), 0.0), -1, keepdims=True)
    pg_sel = 1.0 / sum_g
    lo = MOE_GROUPS + gsel * MOE_PER_GROUP
    is_e = jnp.abs(lane - lo - 0.5 * (MOE_PER_GROUP - 1)) < 0.5 * MOE_PER_GROUP
    le = jnp.where(is_e, logits, neg)
    me = jnp.max(le, -1, keepdims=True)
    ex = jnp.where(is_e, jnp.exp(le - me), 0.0)
    pe = jnp.where(is_e, ex / jnp.sum(ex, -1, keepdims=True), -1.0)
    v1 = jnp.max(pe, -1, keepdims=True)
    i1 = jnp.min(jnp.where(pe == v1, lane, big), -1, keepdims=True)
    pe2 = jnp.where(lane == i1, -1.0, pe)
    v2 = jnp.max(pe2, -1, keepdims=True)
    i2 = jnp.min(jnp.where(pe2 == v2, lane, big), -1, keepdims=True)
    vsum = v1 + v2
    w1 = pg_sel * v1 / vsum
    w2 = pg_sel * v2 / vsum
    e1 = i1 - MOE_GROUPS
    e2 = i2 - MOE_GROUPS

    @pl.when(pl.program_id(0) == 0)
    def _():
        cnt_ref[...] = jnp.zeros_like(cnt_ref)

    oh1 = lane == e1
    oh2 = lane == e2
    ohs = jnp.where(oh1, 1.0, jnp.where(oh2, 1.0, 0.0))
    r_i = lax.broadcasted_iota(jnp.int32, (MERGE_TILE, MERGE_TILE), 0)
    c_i = lax.broadcasted_iota(jnp.int32, (MERGE_TILE, MERGE_TILE), 1)
    strict_lower = jnp.where(c_i < r_i, 1.0, 0.0).astype(bf16)
    before = _dot(strict_lower, ohs.astype(bf16)) + cnt_ref[0:1, :]
    rank1 = jnp.sum(jnp.where(oh1, before, 0.0), -1, keepdims=True)
    rank2 = jnp.sum(jnp.where(oh2, before, 0.0), -1, keepdims=True)
    cnt_ref[0:1, :] = cnt_ref[0:1, :] + jnp.sum(ohs, 0, keepdims=True)
    cnt_out_ref[...] = cnt_ref[...]

    route = jnp.zeros((MERGE_TILE, ROUTE_LANES), f32)
    for k, val in enumerate((e1, e2, w1, w2, rank1, rank2)):
        route = jnp.where(lane == k, val, route)
    route_ref[...] = route
    route_t_ref[...] = route.T[0:V7X_SUBLANES, :]


def _merge(y_ret, y_ssm, y_lru, xb, w_bf, layer, x, wb, wo, g, b, wr, br):
    full = lambda a: _layer_spec(a, layer)
    row = lambda w: pl.BlockSpec((MERGE_TILE, w), lambda i: (i, 0))
    consts = (wb, wo, g, b, wr, br)
    return pl.pallas_call(
        _merge_kernel,
        grid=(N_TOK // MERGE_TILE,),
        in_specs=[row(BRANCH_W), row(BRANCH_W), row(BRANCH_W), row(D_MODEL),
                  _const_spec((1, D_MODEL, GATE_COLS), (layer, 0, GATE_BLK)),
                  row(D_MODEL)] + [full(a) for a in consts],
        out_specs=[pl.BlockSpec((N_PLANES, MERGE_TILE, V7X_LANES), lambda i: (0, i, 0)), row(ROUTE_LANES),
                   pl.BlockSpec((V7X_SUBLANES, MERGE_TILE), lambda i: (0, i)),
                   pl.BlockSpec((V7X_SUBLANES, ROUTE_LANES), lambda i: (0, 0))],
        out_shape=[jax.ShapeDtypeStruct((N_PLANES, N_TOK, V7X_LANES), f32),
                   jax.ShapeDtypeStruct((N_TOK, ROUTE_LANES), f32),
                   jax.ShapeDtypeStruct((V7X_SUBLANES, N_TOK), f32),
                   jax.ShapeDtypeStruct((V7X_SUBLANES, ROUTE_LANES), f32)],
        scratch_shapes=[pltpu.VMEM((V7X_SUBLANES, ROUTE_LANES), f32)],
        compiler_params=_cparams(("arbitrary",)),
        name="merge",
    )(y_ret, y_ssm, y_lru, xb, w_bf, x, *consts)


def _dispatch_kernel(pos0_ref, pos1_ref, x1_ref, xs_hbm, sem):
    base = pl.program_id(0) * DSP_TILE

    def row_copy(r, dst_row):
        return pltpu.make_async_copy(x1_ref.at[:, r, :], xs_hbm.at[dst_row], sem.at[0])

    def issue(r, c):
        n = base + r
        for k in range(MOE_TOPK):
            row_copy(r, (pos0_ref, pos1_ref)[k][n]).start(priority=k % 2)
        return c

    lax.fori_loop(0, DSP_TILE, issue, 0, unroll=8)
    for _ in range(MOE_TOPK):
        pltpu.make_async_copy(x1_ref, x1_ref, sem.at[0]).wait()


def _dispatch(pos, x1p):
    grid_spec = pltpu.PrefetchScalarGridSpec(
        num_scalar_prefetch=MOE_TOPK,
        grid=(N_TOK // DSP_TILE,),
        in_specs=[pl.BlockSpec((N_PLANES, DSP_TILE, V7X_LANES), lambda s, *_: (0, s, 0))],
        out_specs=pl.BlockSpec(memory_space=pl.ANY),
        scratch_shapes=[pltpu.SemaphoreType.DMA((1,))],
    )
    return pl.pallas_call(
        _dispatch_kernel,
        grid_spec=grid_spec,
        out_shape=jax.ShapeDtypeStruct((N_PAIRS, N_PLANES, V7X_LANES), f32),
        compiler_params=_cparams(("arbitrary",)),
        name="dispatch",
    )(*pos, x1p)


def _moe_kernel(layer, wt_ref, we_ref, wlo_ref, whi_ref, wfirst_ref, wlast_ref, wefirst_ref, weslot_ref,
                wenext_ref, nw_ref,
                xs_hbm, w1_hbm, w3_hbm, w2_hbm, ys_hbm,
                xin, yout, sem_in, sem_out, wst1, wst3, wst2, sem_w, w1b, w3b, w2b):
    w = pl.program_id(0)
    n_items = nw_ref[0]

    def weight_copies(expert, slot):
        return [pltpu.make_async_copy(src.at[layer, expert], dst.at[slot], sem_w.at[slot])
                for src, dst in ((w1_hbm, wst1), (w3_hbm, wst3), (w2_hbm, wst2))]

    def in_copy(item, slot, c):
        r0 = pl.multiple_of(wt_ref[item] * MOE_TM, MOE_TM)
        return pltpu.make_async_copy(xs_hbm.at[pl.ds(r0, MOE_TM), c, :], xin.at[slot, c], sem_in.at[slot])

    def out_copy(tile, slot, c):
        r0 = pl.multiple_of(tile * MOE_TM, MOE_TM)
        return pltpu.make_async_copy(yout.at[slot, c], ys_hbm.at[pl.ds(r0, MOE_TM), c, :], sem_out.at[slot])

    @pl.when(w == 0)
    def _():
        for cp in weight_copies(we_ref[0], 0):
            cp.start()
        for c in range(N_PLANES):
            in_copy(0, 0, c).start(priority=1)

    @pl.when(w + 1 < n_items)
    def _():
        for c in range(N_PLANES):
            in_copy(w + 1, (w + 1) % 2, c).start(priority=1)

    @pl.when(w < n_items)
    def _():
        slot = w % 2
        tile = wt_ref[w]
        oslot = tile % 2
        for c in range(N_PLANES):
            in_copy(w, slot, c).wait()

        @pl.when(wefirst_ref[w] == 1)
        def _():
            wslot = weslot_ref[w]
            for cp in weight_copies(we_ref[w], wslot):
                cp.wait()
            w1b[...] = wst1[wslot].astype(bf16)
            w3b[...] = wst3[wslot].astype(bf16)
            w2b[...] = wst2[wslot].astype(bf16)

            @pl.when(wenext_ref[w] >= 0)
            def _():
                for cp in weight_copies(wenext_ref[w], 1 - wslot):
                    cp.start()

        xt = _from_planes(xin.at[slot]).astype(bf16)
        h = jax.nn.silu(_dot(xt, w1b[...])) * _dot(xt, w3b[...])
        res = _dot(h.astype(bf16), w2b[...])
        row = lax.broadcasted_iota(jnp.int32, (MOE_TM, D_MODEL), 0)
        mine = jnp.where(row >= wlo_ref[w], row, MOE_TM) < whi_ref[w]

        @pl.when(wfirst_ref[w] == 1)
        def _():
            @pl.when(tile >= 2)
            def _():
                for c in range(N_PLANES):
                    out_copy(0, oslot, c).wait()

            _to_planes(yout.at[oslot], jnp.where(mine, res, 0.0))

        @pl.when(wfirst_ref[w] == 0)
        def _():
            _to_planes(yout.at[oslot], jnp.where(mine, res, _from_planes(yout.at[oslot])))

        @pl.when(wlast_ref[w] == 1)
        def _():
            for c in range(N_PLANES):
                out_copy(tile, oslot, c).start()

        @pl.when(w == n_items - 1)
        def _():
            for c in range(N_PLANES):
                out_copy(0, oslot, c).wait()

            @pl.when(tile >= 1)
            def _():
                for c in range(N_PLANES):
                    out_copy(0, 1 - oslot, c).wait()


def _moe(plan, xs, w1, w3, w2, layer):
    grid_spec = pltpu.PrefetchScalarGridSpec(
        num_scalar_prefetch=len(plan),
        grid=(MOE_MAX_ITEMS,),
        in_specs=[pl.BlockSpec(memory_space=pl.ANY)] * 4,
        out_specs=pl.BlockSpec(memory_space=pl.ANY),
        scratch_shapes=[pltpu.VMEM((2, N_PLANES, MOE_TM, V7X_LANES), f32),
                        pltpu.VMEM((2, N_PLANES, MOE_TM, V7X_LANES), f32),
                        pltpu.SemaphoreType.DMA((2,)), pltpu.SemaphoreType.DMA((2,)),
                        pltpu.VMEM((2, D_MODEL, MOE_HIDDEN), f32), pltpu.VMEM((2, D_MODEL, MOE_HIDDEN), f32),
                        pltpu.VMEM((2, MOE_HIDDEN, D_MODEL), f32), pltpu.SemaphoreType.DMA((2,)),
                        pltpu.VMEM((D_MODEL, MOE_HIDDEN), bf16), pltpu.VMEM((D_MODEL, MOE_HIDDEN), bf16),
                        pltpu.VMEM((MOE_HIDDEN, D_MODEL), bf16)],
    )
    return pl.pallas_call(
        functools.partial(_moe_kernel, layer),
        grid_spec=grid_spec,
        out_shape=jax.ShapeDtypeStruct((N_PAIRS, N_PLANES, V7X_LANES), f32),
        compiler_params=_cparams(("arbitrary",)),
        name="moe",
    )(*plan, xs, w1, w3, w2)


def _combine_kernel(final, pos0_ref, pos1_ref, ys_hbm, x1_ref, route_ref, g_ref, b_ref, *rest):
    if final:
        yp_hbm, ysm_ref, buf, sem, stage, sem_out = rest
    else:
        o_ref, ob_ref, buf, sem = rest
    s = pl.program_id(0)
    nsteps = pl.num_programs(0)

    def gather_copy(row, slot, k, r):
        return pltpu.make_async_copy(ys_hbm.at[row], buf.at[slot, k, :, r, :], sem.at[slot])

    def issue(tile, slot):
        def body(r, c):
            n = tile * CMB_TILE + r
            for k in range(MOE_TOPK):
                gather_copy((pos0_ref, pos1_ref)[k][n], slot, k, r).start(priority=k % 2)
            return c

        lax.fori_loop(0, CMB_TILE, body, 0, unroll=8)

    @pl.when(s == 0)
    def _():
        issue(0, 0)

    @pl.when(s + 1 < nsteps)
    def _():
        issue(s + 1, (s + 1) % 2)

    slot = s % 2
    for k in range(MOE_TOPK):
        pltpu.make_async_copy(buf.at[slot, k], buf.at[slot, k], sem.at[slot]).wait()
    route = route_ref[...]
    moe = route[:, 2:3] * _from_planes(buf.at[slot, 0]) + route[:, 3:4] * _from_planes(buf.at[slot, 1])
    y = _layer_norm_rows(DN_ALPHA * _from_planes(x1_ref) + moe, g_ref[...], b_ref[...])
    if not final:
        o_ref[...] = y
        ob_ref[...] = y.astype(bf16)
        return

    steps_t = CMB_TILE // BATCH

    def out_copy(step, oslot, b):
        t0 = pl.multiple_of(step * steps_t, steps_t)
        return pltpu.make_async_copy(stage.at[oslot, :, b, :], yp_hbm.at[b, pl.ds(t0, steps_t), :], sem_out.at[oslot])

    @pl.when(s < CMB_P_STEPS)
    def _():
        oslot = s % 2

        @pl.when(s >= 2)
        def _():
            for b in range(BATCH):
                out_copy(0, oslot, b).wait()

        stage[oslot] = y.reshape(steps_t, BATCH, D_MODEL)
        for b in range(BATCH):
            out_copy(s, oslot, b).start()

        @pl.when(s == CMB_P_STEPS - 1)
        def _():
            for b in range(BATCH):
                out_copy(0, oslot, b).wait()
                out_copy(0, 1 - oslot, b).wait()

    steps_per_tile = CMB_TILE // DEC_BATCH
    for q in range(N_S // CMB_TILE):
        @pl.when(s == CMB_P_STEPS + q)
        def _():
            for h in range(steps_per_tile):
                ysm_ref[:, q * steps_per_tile + h, :] = y[h * DEC_BATCH:(h + 1) * DEC_BATCH, :]


def _combine(pos, ys, x1, route, g, b, layer, final):
    if final:
        out_specs = [pl.BlockSpec(memory_space=pl.ANY),
                     pl.BlockSpec((DEC_BATCH, DEC_SEQ, D_MODEL), lambda s, *_: (0, 0, 0))]
        out_shape = [jax.ShapeDtypeStruct((BATCH, SEQ, D_MODEL), f32),
                     jax.ShapeDtypeStruct((DEC_BATCH, DEC_SEQ, D_MODEL), f32)]
        extra = [pltpu.VMEM((2, CMB_TILE // BATCH, BATCH, D_MODEL), f32), pltpu.SemaphoreType.DMA((2,))]
    else:
        out_specs = [pl.BlockSpec((CMB_TILE, D_MODEL), lambda s, *_: (s, 0))] * 2
        out_shape = [jax.ShapeDtypeStruct((N_TOK, D_MODEL), f32), jax.ShapeDtypeStruct((N_TOK, D_MODEL), bf16)]
        extra = []
    grid_spec = pltpu.PrefetchScalarGridSpec(
        num_scalar_prefetch=MOE_TOPK,
        grid=(N_TOK // CMB_TILE,),
        in_specs=[pl.BlockSpec(memory_space=pl.ANY),
                  pl.BlockSpec((N_PLANES, CMB_TILE, V7X_LANES), lambda s, *_: (0, s, 0)),
                  pl.BlockSpec((CMB_TILE, ROUTE_LANES), lambda s, *_: (s, 0)),
                  _layer_spec(g, layer), _layer_spec(b, layer)],
        out_specs=out_specs,
        scratch_shapes=[pltpu.VMEM((2, MOE_TOPK, N_PLANES, CMB_TILE, V7X_LANES), f32),
                        pltpu.SemaphoreType.DMA((2,))] + extra,
    )
    return pl.pallas_call(
        functools.partial(_combine_kernel, final),
        grid_spec=grid_spec,
        out_shape=out_shape,
        compiler_params=_cparams(("arbitrary",)),
        name="combine_out" if final else "combine",
    )(*pos, ys, x1, route, g, b)


def _to_rows_kernel(xp_ref, xs_ref, o_ref, ob_ref, t3_ref):
    i = pl.program_id(0)

    @pl.when(i < P_TILES)
    def _():
        for b in range(BATCH):
            t3_ref[:, b, :] = xp_ref[b]
        rows = t3_ref[...].reshape(ROW_TILE, D_MODEL)
        o_ref[...] = rows
        ob_ref[...] = rows.astype(bf16)

    @pl.when(i >= P_TILES)
    def _():
        for t in range(DEC_SEQ):
            rows = xs_ref[:, t, :]
            o_ref[t * DEC_BATCH:(t + 1) * DEC_BATCH, :] = rows
            ob_ref[t * DEC_BATCH:(t + 1) * DEC_BATCH, :] = rows.astype(bf16)


def _to_rows(x_prompt, x_sample):
    return pl.pallas_call(
        _to_rows_kernel,
        grid=(N_TILES,),
        in_specs=[pl.BlockSpec((BATCH, P_TC, D_MODEL), lambda i: (0, jnp.minimum(i, P_TILES - 1), 0)),
                  pl.BlockSpec((DEC_BATCH, DEC_SEQ, D_MODEL), lambda i: (0, 0, 0))],
        out_specs=[pl.BlockSpec((ROW_TILE, D_MODEL), lambda i: (i, 0))] * 2,
        out_shape=[jax.ShapeDtypeStruct((N_TOK, D_MODEL), f32), jax.ShapeDtypeStruct((N_TOK, D_MODEL), bf16)],
        scratch_shapes=[pltpu.VMEM((P_TC, BATCH, D_MODEL), f32)],
        compiler_params=_cparams(("arbitrary",)),
        name="to_rows",
    )(x_prompt, x_sample)


def _lookup(table, idx):
    ar = jnp.arange(MOE_EXPERTS, dtype=jnp.int32).reshape((MOE_EXPERTS,) + (1,) * idx.ndim)
    table = table.reshape(ar.shape)
    return jnp.sum(jnp.where(idx[None] == ar, table, 0), axis=0)


def _dispatch_plan(route_t, cnt):
    i32 = jnp.int32
    e = route_t[0:2].astype(i32)
    rank = route_t[4:6].astype(i32)
    counts = cnt[0, :MOE_EXPERTS].astype(i32)
    ends = jnp.cumsum(counts)
    starts = ends - counts
    pos = _lookup(starts, e) + rank

    first_tile = starts // MOE_TM
    last_tile = (ends - 1) // MOE_TM
    ntiles = jnp.where(counts > 0, last_tile - first_tile + 1, 0)
    item_end = jnp.cumsum(ntiles)
    n_items = item_end[-1]
    w = jnp.minimum(jnp.arange(MOE_MAX_ITEMS, dtype=i32), n_items - 1)
    we = jnp.sum((item_end[None, :] <= w[:, None]).astype(i32), axis=-1)
    wt = _lookup(first_tile, we) + w - _lookup(item_end - ntiles, we)
    wlo = jnp.maximum(_lookup(starts, we) - wt * MOE_TM, 0)
    whi = jnp.minimum(_lookup(ends, we) - wt * MOE_TM, MOE_TM)
    changes = (wt[1:] != wt[:-1]).astype(i32)
    wfirst = jnp.concatenate([jnp.ones((1,), i32), changes])
    wlast = jnp.concatenate([changes, jnp.ones((1,), i32)])
    wlast = jnp.where(jnp.arange(MOE_MAX_ITEMS, dtype=i32) == n_items - 1, 1, wlast)
    wefirst = jnp.concatenate([jnp.ones((1,), i32), (we[1:] != we[:-1]).astype(i32)])
    weslot = (jnp.cumsum(wefirst) - 1) % 2
    ar = jnp.arange(MOE_EXPERTS, dtype=i32)
    later = (ar[None, :] > ar[:, None]) & (counts[None, :] > 0)
    next_expert = jnp.min(jnp.where(later, ar[None, :], MOE_EXPERTS), axis=1)
    wenext = _lookup(jnp.where(next_expert < MOE_EXPERTS, next_expert, -1), we)
    return (pos[0], pos[1]), (wt, we, wlo, whi, wfirst, wlast, wefirst, weslot, wenext, n_items.reshape(1))


def _block_diag(w):
    n, a, b = w.shape
    eye = jnp.eye(n, dtype=w.dtype)
    return (w[:, :, None, :] * eye[:, None, :, None]).reshape(n * a, n * b)


def _ssm_params(a_re, a_im, log_dt, b_re, b_im, c_re, c_im):
    ar, ai = a_re, a_im
    dt = jnp.exp(log_dt)[:, None]
    mag = jnp.exp(ar * dt)
    lb_re = mag * jnp.cos(ai * dt)
    lb_im = mag * jnp.sin(ai * dt)
    den = ar * ar + ai * ai
    nr = lb_re - 1.0
    coef_re = (nr * ar + lb_im * ai) / den
    coef_im = (lb_im * ar - nr * ai) / den
    bb_re = coef_re[..., None] * b_re - coef_im[..., None] * b_im
    bb_im = coef_re[..., None] * b_im + coef_im[..., None] * b_re
    gk = SSM_GROUPS // SSM_KB

    def diag_blocks(w):
        return jnp.stack([_block_diag(w[k * gk:(k + 1) * gk]) for k in range(SSM_KB)]).astype(bf16)

    bbre = diag_blocks(bb_re.transpose(0, 2, 1))
    bbim = diag_blocks(bb_im.transpose(0, 2, 1))
    ccre = diag_blocks(c_re.transpose(0, 2, 1))
    ccim = diag_blocks(c_im.transpose(0, 2, 1))
    return (lb_re.reshape(1, SSM_LANES), lb_im.reshape(1, SSM_LANES), bbre, bbim, ccre, ccim)


def kernel(x_prompt, x_sample, state_ret, state_ssm_re, state_ssm_im, state_lru, state_conv, w_in, ret_gn_g, ret_gn_b, ssm_a_re, ssm_a_im, ssm_log_dt, ssm_b_re, ssm_b_im, ssm_c_re, ssm_c_im, ssm_d, ssm_w_glu, lru_conv_w, lru_conv_b, lru_wa, lru_ba, lru_wx, lru_bx, lru_lambda, w_branch, w_out, ln1_g, ln1_b, moe_w_group, moe_b_group, moe_w_expert, moe_b_expert, moe_w1, moe_w3, moe_w2, ln2_g, ln2_b):
    x, xb = _to_rows(x_prompt, x_sample)
    rope = _rope_tables()
    tabs_p = _ret_tables(RET_SUB_T)
    tabs_s = _ret_tables(DEC_SEQ)
    rows = lambda v: v.reshape(DEPTH, 1, -1)
    w_bf = w_in.astype(bf16)
    s0 = state_ret.reshape(DEPTH, DEC_BATCH, 2, 2 * RET_DK, RET_DV)
    ssm_p = jax.vmap(_ssm_params)(ssm_a_re, ssm_a_im, ssm_log_dt, ssm_b_re, ssm_b_im, ssm_c_re, ssm_c_im)
    h0re = state_ssm_re.reshape(DEPTH, DEC_BATCH, SSM_LANES)
    h0im = state_ssm_im.reshape(DEPTH, DEC_BATCH, SSM_LANES)
    conv0 = state_conv.transpose(0, 2, 1, 3).reshape(DEPTH, LRU_HIST_S, BRANCH_W)
    lru_wa_bd = jax.vmap(_block_diag)(lru_wa).astype(bf16)
    lru_wx_bd = jax.vmap(_block_diag)(lru_wx).astype(bf16)
    pad_w = jnp.zeros((DEPTH, D_MODEL, ROUTE_LANES - MOE_GROUPS - MOE_EXPERTS), f32)
    wr = jnp.concatenate([moe_w_group, moe_w_expert, pad_w], axis=-1).astype(bf16)
    br = jnp.concatenate([moe_b_group, moe_b_expert, pad_w[:, 0, :]], axis=-1).reshape(DEPTH, 1, ROUTE_LANES)
    wb_bf, wo_bf, wglu_bf = w_branch.astype(bf16), w_out.astype(bf16), ssm_w_glu.astype(bf16)

    outs = [[] for _ in range(10)]
    for l in range(DEPTH):
        y_ret, ret_p, ret_s = _retention(xb, w_bf, rope, s0, l, tabs_p, tabs_s, rows(ret_gn_g), rows(ret_gn_b))
        y_ssm, re_p, im_p, re_s, im_s = _ssm(xb, w_bf, l, h0re, h0im, *ssm_p, rows(ssm_d), wglu_bf)
        y_lru, lru_p, conv_p, lru_s, conv_s = _lru(
            xb, w_bf, l, state_lru, conv0, lru_conv_w, rows(lru_conv_b),
            lru_wa_bd, rows(lru_ba), lru_wx_bd, rows(lru_bx), rows(lru_lambda))
        x1, route, route_t, cnt = _merge(y_ret, y_ssm, y_lru, xb, w_bf, l, x, wb_bf, wo_bf,
                                         rows(ln1_g), rows(ln1_b), wr, br)

        pos, plan = _dispatch_plan(route_t, cnt)
        xs = _dispatch(pos, x1)
        ys = _moe(plan, xs, moe_w1, moe_w3, moe_w2, l)
        x, xb = _combine(pos, ys, x1, route, rows(ln2_g), rows(ln2_b), l, final=(l == DEPTH - 1))

        outs[0].append(ret_p.reshape(BATCH, RET_HEADS, RET_DK, RET_DV))
        outs[1].append(re_p.reshape(BATCH, SSM_GROUPS, SSM_STATE))
        outs[2].append(im_p.reshape(BATCH, SSM_GROUPS, SSM_STATE))
        outs[3].append(lru_p)
        outs[4].append(conv_p.reshape(CONV_W - 1, BATCH, BRANCH_W).transpose(1, 0, 2))
        outs[5].append(ret_s.reshape(DEC_BATCH, RET_HEADS, RET_DK, RET_DV))
        outs[6].append(re_s.reshape(DEC_BATCH, SSM_GROUPS, SSM_STATE))
        outs[7].append(im_s.reshape(DEC_BATCH, SSM_GROUPS, SSM_STATE))
        outs[8].append(lru_s)
        outs[9].append(conv_s.reshape(CONV_W - 1, DEC_BATCH, BRANCH_W).transpose(1, 0, 2))

    y_prompt, y_sample = x, xb
    return (y_prompt, y_sample) + tuple(jnp.stack(o) for o in outs)
```

```python
import functools

import jax
import jax.numpy as jnp
import numpy as np
from jax import lax
from jax.experimental import pallas as pl
from jax.experimental.pallas import tpu as pltpu

f32 = jnp.float32
bf16 = jnp.bfloat16

D_MODEL = 1024
BATCH = 8
SEQ = 2048
DEPTH = 2
DEC_BATCH = 128
DEC_SEQ = 8
PAST_LEN = 16384
BRANCH_W = 512
RET_HEADS = 4
RET_DK = 64
RET_DV = 128
ROPE_BASE = 10000.0
SSM_GROUPS = 32
SSM_STATE = 64
SSM_LANES = SSM_GROUPS * SSM_STATE
CONV_W = 4
LRU_C = 8.0
MOE_GROUPS = 4
MOE_PER_GROUP = 8
MOE_EXPERTS = 32
MOE_TOPK = 2
MOE_HIDDEN = 512
DN_ALPHA = (2.0 * DEPTH) ** 0.25
LN_EPS = 1e-5

V7X_SUBLANES = 8
V7X_LANES = 128
V7X_VMEM_LIMIT = 56 * 1024 * 1024

N_P = BATCH * SEQ
N_S = DEC_BATCH * DEC_SEQ
N_TOK = N_P + N_S
ROW_TILE = 1024
P_TILES = N_P // ROW_TILE
N_TILES = N_TOK // ROW_TILE
P_TC = ROW_TILE // BATCH
RET_SUB_T = 32
RET_SUB_R = RET_SUB_T * BATCH
RET_SUBS = ROW_TILE // RET_SUB_R
S_BLOCKS = DEC_BATCH // BATCH
MERGE_TILE = 512
MOE_TM = 512
N_PAIRS = N_TOK * MOE_TOPK
MOE_MAX_ITEMS = N_PAIRS // MOE_TM + MOE_EXPERTS - 1
DSP_TILE = 1024
CMB_TILE = 256
CMB_P_STEPS = N_P // CMB_TILE
ROUTE_LANES = 128


PROJ_RC = 256
RET_QK_W = RET_HEADS * RET_DK
RET_COLS, RET_BLK = 2 * RET_QK_W + 2 * BRANCH_W, 0
SSM_COLS, SSM_BLK = 512, 3
LRU_COLS, LRU_BLK = 1024, 2
GATE_COLS, GATE_BLK = 3072, 1


def _cparams(sem):
    return pltpu.CompilerParams(dimension_semantics=sem, vmem_limit_bytes=V7X_VMEM_LIMIT)


N_PLANES = D_MODEL // V7X_LANES


def _to_planes(ref, rows):
    for c in range(N_PLANES):
        ref[c] = rows[:, c * V7X_LANES:(c + 1) * V7X_LANES]


def _from_planes(ref):
    return jnp.concatenate([ref[c] for c in range(N_PLANES)], axis=1)


def _layer_spec(stacked, layer):
    shape = (None,) + stacked.shape[1:]
    return pl.BlockSpec(shape, lambda *_: (layer,) + (0,) * (stacked.ndim - 1), pipeline_mode=pl.Buffered(1))


def _const_spec(block_shape, index):
    return pl.BlockSpec(block_shape, lambda *_: index, pipeline_mode=pl.Buffered(1))


def _dot(a, b):
    return jnp.dot(a, b, preferred_element_type=f32)


def _dot_nt(a, b):
    return lax.dot_general(a, b, (((1,), (1,)), ((), ())), preferred_element_type=f32)


def _dot_tn(a, b):
    return lax.dot_general(a, b, (((0,), (0,)), ((), ())), preferred_element_type=f32)


def _layer_norm_rows(x, g, b):
    mu = jnp.mean(x, -1, keepdims=True)
    xc = x - mu
    var = jnp.mean(xc * xc, -1, keepdims=True)
    return xc * lax.rsqrt(var + LN_EPS) * g + b


def _project_rows(xb_ref, w_ref, z_ref):
    for rc in range(ROW_TILE // PROJ_RC):
        rs = slice(rc * PROJ_RC, (rc + 1) * PROJ_RC)
        z_ref[rs, :] = _dot(xb_ref[rs, :], w_ref[0])


def _ret_block(q, k, v, g, cosb, sinb, mask_ref, qdec_ref, kdec_ref, cdec_ref, scat_ref, gng, gnb):
    rows = q.shape[0]
    lane_qk = lax.broadcasted_iota(jnp.int32, (rows, 2 * V7X_LANES), 1)
    first_half = (lane_qk & (RET_DK - 1)) < (RET_DK // 2)

    def rope(x):
        partner = jnp.where(first_half, pltpu.roll(x, 2 * V7X_LANES - RET_DK // 2, 1),
                            pltpu.roll(x, RET_DK // 2, 1))
        return x * cosb + partner * sinb

    q = rope(q)
    k = rope(k) * (RET_DK ** -0.5)
    kd = k * kdec_ref[...]
    lane = lax.broadcasted_iota(jnp.int32, (rows, V7X_LANES), 1)
    row_b = lax.broadcasted_iota(jnp.int32, (rows, V7X_LANES), 0) & (BATCH - 1)
    outs = []
    for p in range(2):
        qp = q[:, p * V7X_LANES:(p + 1) * V7X_LANES]
        kp = k[:, p * V7X_LANES:(p + 1) * V7X_LANES].astype(bf16)
        kdp = kd[:, p * V7X_LANES:(p + 1) * V7X_LANES]
        s_old = scat_ref[p]
        s_bf = s_old.astype(bf16)
        s_new = s_old * jnp.concatenate([cdec_ref[p]] * BATCH, axis=1)
        for hh in range(2):
            h = 2 * p + hh
            head_lanes = (lane >= RET_DK) if hh else (lane < RET_DK)
            qh = jnp.where(head_lanes, qp, 0.0).astype(bf16)
            kdh = jnp.where(head_lanes, kdp, 0.0).astype(bf16)
            vh = v[:, h * RET_DV:(h + 1) * RET_DV]
            vh_bf = vh.astype(bf16)
            sc = _dot_nt(qh, kp) * mask_ref[h]
            o = _dot(sc.astype(bf16), vh_bf)
            cross = _dot(qh, s_bf)
            oc = jnp.zeros((rows, RET_DV), f32)
            for b in range(BATCH):
                oc = oc + jnp.where(row_b == b, cross[:, b * RET_DV:(b + 1) * RET_DV], 0.0)
            o = o + oc * qdec_ref[h]
            vcat = jnp.concatenate([jnp.where(row_b == b, vh_bf, jnp.zeros_like(vh_bf))
                                    for b in range(BATCH)], axis=1)
            s_new = s_new + _dot_tn(kdh, vcat)
            mu = jnp.mean(o, -1, keepdims=True)
            oc2 = o - mu
            var = jnp.mean(oc2 * oc2, -1, keepdims=True)
            outs.append(oc2 * lax.rsqrt(var + LN_EPS))
        scat_ref[p] = s_new
    o = jnp.concatenate(outs, axis=1) * gng + gnb
    return jax.nn.silu(g) * o


def _per_step_rows(tab_ref, t0, steps):
    return jnp.concatenate(
        [jnp.broadcast_to(tab_ref[t0 + t:t0 + t + 1, :], (BATCH, tab_ref.shape[1])) for t in range(steps)], axis=0)


def _ret_kernel(xb_ref, w_ref, cos_ref, sin_ref, cos_s_ref, sin_s_ref, s0_ref,
                mask_p_ref, qdec_p_ref, kdec_p_ref, cdec_p_ref,
                mask_s_ref, qdec_s_ref, kdec_s_ref, cdec_s_ref,
                gng_ref, gnb_ref,
                y_ref, retp_ref, rets_ref, scat_ref, z_ref):
    i = pl.program_id(0)
    gng = gng_ref[...]
    gnb = gnb_ref[...]
    q_cols, k_cols = slice(0, RET_QK_W), slice(RET_QK_W, 2 * RET_QK_W)
    v_cols = slice(2 * RET_QK_W, 2 * RET_QK_W + BRANCH_W)
    g_cols = slice(2 * RET_QK_W + BRANCH_W, RET_COLS)

    @pl.when(i == 0)
    def _():
        scat_ref[...] = jnp.zeros_like(scat_ref)

    @pl.when(i <= P_TILES)
    def _():
        _project_rows(xb_ref, w_ref, z_ref)

    @pl.when(i < P_TILES)
    def _():
        for sc in range(RET_SUBS):
            r0 = sc * RET_SUB_R
            rs = slice(r0, r0 + RET_SUB_R)
            y_ref[rs, :] = _ret_block(
                z_ref[rs, q_cols], z_ref[rs, k_cols], z_ref[rs, v_cols], z_ref[rs, g_cols],
                _per_step_rows(cos_ref, sc * RET_SUB_T, RET_SUB_T), _per_step_rows(sin_ref, sc * RET_SUB_T, RET_SUB_T),
                mask_p_ref, qdec_p_ref, kdec_p_ref, cdec_p_ref, scat_ref, gng, gnb)

    @pl.when(i == P_TILES - 1)
    def _():
        for b in range(BATCH):
            for p in range(2):
                retp_ref[b, p] = scat_ref[p, :, b * RET_DV:(b + 1) * RET_DV]

    @pl.when(i >= P_TILES)
    def _():
        bb = i - P_TILES
        for b in range(BATCH):
            for p in range(2):
                scat_ref[p, :, b * RET_DV:(b + 1) * RET_DV] = s0_ref[0, b, p]

        def rows_of(cols):
            return jnp.concatenate(
                [z_ref[pl.ds(pl.multiple_of(t * DEC_BATCH + bb * BATCH, BATCH), BATCH), cols]
                 for t in range(DEC_SEQ)], axis=0)

        y = _ret_block(
            rows_of(q_cols), rows_of(k_cols), rows_of(v_cols),
            rows_of(g_cols), _per_step_rows(cos_s_ref, 0, DEC_SEQ), _per_step_rows(sin_s_ref, 0, DEC_SEQ),
            mask_s_ref, qdec_s_ref, kdec_s_ref, cdec_s_ref, scat_ref, gng, gnb)
        for t in range(DEC_SEQ):
            y_ref[pl.ds(pl.multiple_of(t * DEC_BATCH + bb * BATCH, BATCH), BATCH), :] = (
                y[t * BATCH:(t + 1) * BATCH, :])
        for b in range(BATCH):
            for p in range(2):
                rets_ref[b, p] = scat_ref[p, :, b * RET_DV:(b + 1) * RET_DV]


def _ret_tables(tc):
    rows = tc * BATCH
    nf = np.float32
    log_g = np.log1p(-np.exp2(nf(-5.0) - np.arange(RET_HEADS, dtype=nf))).astype(nf)
    t_idx = (np.arange(rows) // BATCH).astype(nf)
    b_idx = np.arange(rows) % BATCH
    rel = t_idx[:, None] - t_idx[None, :]
    same = b_idx[:, None] == b_idx[None, :]
    decay = np.exp(log_g[:, None, None] * np.maximum(rel, nf(0.0)))
    mask = np.where((rel >= 0) & same, decay, nf(0.0))
    qdec = np.exp(log_g[:, None] * (t_idx[None, :] + nf(1.0)))
    qdec = np.broadcast_to(qdec[:, :, None], (RET_HEADS, rows, RET_DV))
    kdec = np.exp(log_g[:, None] * (nf(tc - 1.0) - t_idx[None, :]))
    kdec = np.broadcast_to(kdec.T[:, :, None], (rows, RET_HEADS, RET_DK)).reshape(rows, RET_HEADS * RET_DK)
    cdec = np.exp(log_g * nf(tc))
    cdec = np.broadcast_to(cdec[:, None, None], (RET_HEADS, RET_DK, RET_DV)).reshape(2, 2 * RET_DK, RET_DV)
    return tuple(jnp.asarray(np.ascontiguousarray(a), dtype=f32) for a in (mask, qdec, kdec, cdec))


def _rope_tables():
    half = RET_DK // 2
    inv = ROPE_BASE ** (-jnp.arange(half, dtype=f32) / half)
    pos_p = jnp.arange(SEQ, dtype=f32)
    pos_s = PAST_LEN + jnp.arange(DEC_SEQ, dtype=f32)

    def tab(pos):
        ang = pos[:, None] * inv[None, :]
        cos = jnp.cos(ang)
        sin = jnp.sin(ang)
        cos_h = jnp.concatenate([cos, cos], axis=1)
        sin_h = jnp.concatenate([-sin, sin], axis=1)
        return jnp.tile(cos_h, (1, RET_HEADS)), jnp.tile(sin_h, (1, RET_HEADS))

    return tab(pos_p) + tab(pos_s)


def _retention(xb, w_bf, rope, s0_s, layer, tabs_p, tabs_s, gng, gnb):
    n_steps = P_TILES + S_BLOCKS
    tile = lambda i: jnp.minimum(i, P_TILES)
    sblk = lambda i: jnp.maximum(i - P_TILES, 0)
    full = lambda a: pl.BlockSpec(a.shape, lambda i, _n=a.ndim: (0,) * _n)
    state_blk = (BATCH, 2, 2 * RET_DK, RET_DV)
    cos_p, sin_p, cos_s, sin_s = rope
    ptile = lambda i: jnp.minimum(i, P_TILES - 1)
    ins = [xb, w_bf, cos_p, sin_p, cos_s, sin_s, s0_s, *tabs_p, *tabs_s, gng, gnb]
    in_specs = [
        pl.BlockSpec((ROW_TILE, D_MODEL), lambda i: (tile(i), 0)),
        _const_spec((1, D_MODEL, RET_COLS), (layer, 0, RET_BLK)),
        pl.BlockSpec((P_TC, RET_QK_W), lambda i: (ptile(i), 0)),
        pl.BlockSpec((P_TC, RET_QK_W), lambda i: (ptile(i), 0)),
        full(cos_s), full(sin_s),
        pl.BlockSpec((1,) + state_blk, lambda i: (layer, sblk(i), 0, 0, 0)),
    ] + [full(a) for a in (*tabs_p, *tabs_s)] + [_layer_spec(gng, layer), _layer_spec(gnb, layer)]
    return pl.pallas_call(
        _ret_kernel,
        grid=(n_steps,),
        in_specs=in_specs,
        out_specs=[pl.BlockSpec((ROW_TILE, BRANCH_W), lambda i: (tile(i), 0)),
                   pl.BlockSpec(state_blk, lambda i: (0, 0, 0, 0)),
                   pl.BlockSpec(state_blk, lambda i: (sblk(i), 0, 0, 0))],
        out_shape=[jax.ShapeDtypeStruct((N_TOK, BRANCH_W), f32),
                   jax.ShapeDtypeStruct((BATCH, 2, 2 * RET_DK, RET_DV), f32),
                   jax.ShapeDtypeStruct((DEC_BATCH, 2, 2 * RET_DK, RET_DV), f32)],
        scratch_shapes=[pltpu.VMEM((2, 2 * RET_DK, BATCH * RET_DV), f32),
                        pltpu.VMEM((ROW_TILE, RET_COLS), f32)],
        compiler_params=_cparams(("arbitrary",)),
        name="retention",
    )(*ins)


SSM_LB = 512
SSM_RC = 256
SSM_KB = 2
SSM_KB_U = BRANCH_W // SSM_KB
SSM_KB_H = SSM_LANES // SSM_KB


def _ssm_scan(bre_ref, bim_ref, lre_ref, lim_ref, h_re0, h_im0, row0, nb_rows, steps, lb, unroll):
    ls = slice(lb * SSM_LB, (lb + 1) * SSM_LB)
    a_re = jnp.broadcast_to(lre_ref[:, ls], (V7X_SUBLANES, SSM_LB))
    a_im = jnp.broadcast_to(lim_ref[:, ls], (V7X_SUBLANES, SSM_LB))

    def step(t, carry):
        h_re, h_im = carry
        r = pl.multiple_of(row0 + t * nb_rows, V7X_SUBLANES)
        n_re = a_re * h_re - a_im * h_im + bre_ref[pl.ds(r, V7X_SUBLANES), ls]
        n_im = a_re * h_im + a_im * h_re + bim_ref[pl.ds(r, V7X_SUBLANES), ls]
        bre_ref[pl.ds(r, V7X_SUBLANES), ls] = n_re
        bim_ref[pl.ds(r, V7X_SUBLANES), ls] = n_im
        return n_re, n_im

    return lax.fori_loop(0, steps, step, (h_re0, h_im0), unroll=unroll)


def _ssm_kernel(xb_ref, w_ref, h0re_ref, h0im_ref, lre_ref, lim_ref, bbre_ref, bbim_ref, ccre_ref, ccim_ref,
                d_ref, wglu_ref,
                y_ref, pre_ref, pim_ref, sre_ref, sim_ref,
                bre_ref, bim_ref, hre_ref, him_ref, zs_ref):
    i = pl.program_id(0)
    _project_rows(xb_ref, w_ref, zs_ref)

    def inputs(rc):
        rs = slice(rc * SSM_RC, (rc + 1) * SSM_RC)
        ub = zs_ref[rs, :].astype(bf16)
        for k in range(SSM_KB):
            uk = ub[:, k * SSM_KB_U:(k + 1) * SSM_KB_U]
            hs = slice(k * SSM_KB_H, (k + 1) * SSM_KB_H)
            bre_ref[rs, hs] = _dot(uk, bbre_ref[k])
            bim_ref[rs, hs] = _dot(uk, bbim_ref[k])

    def outputs(rc):
        rs = slice(rc * SSM_RC, (rc + 1) * SSM_RC)
        ch = []
        for k in range(SSM_KB):
            hs = slice(k * SSM_KB_H, (k + 1) * SSM_KB_H)
            ch.append(_dot(bre_ref[rs, hs].astype(bf16), ccre_ref[k]) - _dot(bim_ref[rs, hs].astype(bf16), ccim_ref[k]))
        y = jnp.concatenate(ch, axis=1) + d_ref[...] * zs_ref[rs, :]
        zz = jax.nn.gelu(y)
        y_ref[rs, :] = zz * jax.nn.sigmoid(_dot(zz.astype(bf16), wglu_ref[...]))

    @pl.when(i == 0)
    def _():
        hre_ref[...] = jnp.zeros_like(hre_ref)
        him_ref[...] = jnp.zeros_like(him_ref)

    @pl.when(i < P_TILES)
    def _():
        steps = SSM_RC // BATCH
        for rc in range(ROW_TILE // SSM_RC):
            inputs(rc)
        for rc in range(ROW_TILE // SSM_RC):
            for lb in range(SSM_LANES // SSM_LB):
                ls = slice(lb * SSM_LB, (lb + 1) * SSM_LB)
                a_re = jnp.broadcast_to(lre_ref[:, ls], (BATCH, SSM_LB))
                a_im = jnp.broadcast_to(lim_ref[:, ls], (BATCH, SSM_LB))
                h_re, h_im = hre_ref[:, ls], him_ref[:, ls]
                for t in range(steps):
                    rows = slice(rc * SSM_RC + t * BATCH, rc * SSM_RC + (t + 1) * BATCH)
                    h_re, h_im = (a_re * h_re - a_im * h_im + bre_ref[rows, ls],
                                  a_re * h_im + a_im * h_re + bim_ref[rows, ls])
                    bre_ref[rows, ls] = h_re
                    bim_ref[rows, ls] = h_im
                hre_ref[:, ls] = h_re
                him_ref[:, ls] = h_im
            outputs(rc)
        pre_ref[...] = hre_ref[...]
        pim_ref[...] = him_ref[...]

    @pl.when(i >= P_TILES)
    def _():
        for rc in range(ROW_TILE // SSM_RC):
            inputs(rc)

        def per_row_tile(rt, c):
            r0 = pl.multiple_of(rt * V7X_SUBLANES, V7X_SUBLANES)
            for lb in range(SSM_LANES // SSM_LB):
                ls = slice(lb * SSM_LB, (lb + 1) * SSM_LB)
                _ssm_scan(bre_ref, bim_ref, lre_ref, lim_ref,
                          h0re_ref[pl.ds(r0, V7X_SUBLANES), ls], h0im_ref[pl.ds(r0, V7X_SUBLANES), ls],
                          r0, DEC_BATCH, DEC_SEQ, lb, True)
            return c

        lax.fori_loop(0, DEC_BATCH // V7X_SUBLANES, per_row_tile, 0)
        last = (DEC_SEQ - 1) * DEC_BATCH
        sre_ref[...] = bre_ref[last:last + DEC_BATCH, :]
        sim_ref[...] = bim_ref[last:last + DEC_BATCH, :]
        for rc in range(ROW_TILE // SSM_RC):
            outputs(rc)


def _ssm(xb, w_bf, layer, h0re, h0im, lre, lim, bbre, bbim, ccre, ccim, dvec, wglu):
    full = lambda a: _layer_spec(a, layer)
    consts = (h0re, h0im, lre, lim, bbre, bbim, ccre, ccim, dvec, wglu)
    return pl.pallas_call(
        _ssm_kernel,
        grid=(N_TILES,),
        in_specs=[pl.BlockSpec((ROW_TILE, D_MODEL), lambda i: (i, 0)),
                  _const_spec((1, D_MODEL, SSM_COLS), (layer, 0, SSM_BLK))] + [full(a) for a in consts],
        out_specs=[pl.BlockSpec((ROW_TILE, BRANCH_W), lambda i: (i, 0)),
                   pl.BlockSpec((BATCH, SSM_LANES), lambda i: (0, 0)),
                   pl.BlockSpec((BATCH, SSM_LANES), lambda i: (0, 0)),
                   pl.BlockSpec((DEC_BATCH, SSM_LANES), lambda i: (0, 0)),
                   pl.BlockSpec((DEC_BATCH, SSM_LANES), lambda i: (0, 0))],
        out_shape=[jax.ShapeDtypeStruct((N_TOK, BRANCH_W), f32),
                   jax.ShapeDtypeStruct((BATCH, SSM_LANES), f32),
                   jax.ShapeDtypeStruct((BATCH, SSM_LANES), f32),
                   jax.ShapeDtypeStruct((DEC_BATCH, SSM_LANES), f32),
                   jax.ShapeDtypeStruct((DEC_BATCH, SSM_LANES), f32)],
        scratch_shapes=[pltpu.VMEM((ROW_TILE, SSM_LANES), f32), pltpu.VMEM((ROW_TILE, SSM_LANES), f32),
                        pltpu.VMEM((BATCH, SSM_LANES), f32), pltpu.VMEM((BATCH, SSM_LANES), f32),
                        pltpu.VMEM((ROW_TILE, SSM_COLS), f32)],
        compiler_params=_cparams(("arbitrary",)),
        name="ssm",
    )(xb, w_bf, *consts)


LRU_HIST_P = (CONV_W - 1) * BATCH
LRU_HIST_S = (CONV_W - 1) * DEC_BATCH


def _lru_gates(xe_ref, nb_rows, cw_ref, cb_ref, wa_ref, ba_ref, wx_ref, bx_ref, lam_ref):
    xc = cb_ref[...] + xe_ref[0:ROW_TILE, :] * cw_ref[0:1, :]
    for j in range(1, CONV_W):
        xc = xc + xe_ref[j * nb_rows:j * nb_rows + ROW_TILE, :] * cw_ref[j:j + 1, :]
    xcb = xc.astype(bf16)
    r = jax.nn.sigmoid(_dot(xcb, wa_ref[...]) + ba_ref[...])
    ig = jax.nn.sigmoid(_dot(xcb, wx_ref[...]) + bx_ref[...])
    log_a = -LRU_C * r * jax.nn.softplus(-lam_ref[...])
    a = jnp.exp(log_a)
    b = jnp.sqrt(-jnp.tanh(log_a) * (a * a + 1.0)) * (ig * xc)
    return a, b


def _lru_kernel(xb_ref, w_ref, h0_ref, conv0_ref, cw_ref, cb_ref, wa_ref, ba_ref, wx_ref, bx_ref, lam_ref,
                y_ref, hp_ref, convp_ref, hs_ref, convs_ref,
                xe_ref, a_ref, b_ref, hc_ref, z_ref):
    i = pl.program_id(0)
    params = (cw_ref, cb_ref, wa_ref, ba_ref, wx_ref, bx_ref, lam_ref)
    _project_rows(xb_ref, w_ref, z_ref)
    zx_ref = z_ref.at[:, 0:BRANCH_W]
    zg_ref = z_ref.at[:, BRANCH_W:2 * BRANCH_W]

    @pl.when(i == 0)
    def _():
        xe_ref[0:LRU_HIST_P, :] = jnp.zeros((LRU_HIST_P, BRANCH_W), f32)
        hc_ref[...] = jnp.zeros_like(hc_ref)

    @pl.when(i < P_TILES)
    def _():
        xe_ref[LRU_HIST_P:LRU_HIST_P + ROW_TILE, :] = zx_ref[...]
        a, b = _lru_gates(xe_ref, BATCH, *params)
        a_ref[...] = a
        b_ref[...] = b
        hist = xe_ref[ROW_TILE:ROW_TILE + LRU_HIST_P, :]
        xe_ref[0:LRU_HIST_P, :] = hist
        convp_ref[...] = hist

        h = hc_ref[...]
        for t in range(P_TC):
            rows = slice(t * BATCH, (t + 1) * BATCH)
            h = a_ref[rows, :] * h + b_ref[rows, :]
            b_ref[rows, :] = h
        hc_ref[...] = h
        hp_ref[...] = h

    @pl.when(i >= P_TILES)
    def _():
        xe_ref[0:LRU_HIST_S, :] = conv0_ref[...]
        xe_ref[LRU_HIST_S:LRU_HIST_S + ROW_TILE, :] = zx_ref[...]
        a, b = _lru_gates(xe_ref, DEC_BATCH, *params)
        a_ref[...] = a
        b_ref[...] = b
        convs_ref[...] = xe_ref[ROW_TILE:ROW_TILE + LRU_HIST_S, :]

        def per_row_tile(rt, c):
            r0 = pl.multiple_of(rt * V7X_SUBLANES, V7X_SUBLANES)
            h = h0_ref[pl.ds(r0, V7X_SUBLANES), :]
            for t in range(DEC_SEQ):
                r = pl.multiple_of(t * DEC_BATCH + r0, V7X_SUBLANES)
                h = a_ref[pl.ds(r, V7X_SUBLANES), :] * h + b_ref[pl.ds(r, V7X_SUBLANES), :]
                b_ref[pl.ds(r, V7X_SUBLANES), :] = h
            return c

        lax.fori_loop(0, DEC_BATCH // V7X_SUBLANES, per_row_tile, 0)
        last = (DEC_SEQ - 1) * DEC_BATCH
        hs_ref[...] = b_ref[last:last + DEC_BATCH, :]

    y_ref[...] = b_ref[...] * jax.nn.gelu(zg_ref[...])


def _lru(xb, w_bf, layer, h0, conv0, cw, cb, wa, ba, wx, bx, lam):
    full = lambda a: _layer_spec(a, layer)
    consts = (h0, conv0, cw, cb, wa, ba, wx, bx, lam)
    return pl.pallas_call(
        _lru_kernel,
        grid=(N_TILES,),
        in_specs=[pl.BlockSpec((ROW_TILE, D_MODEL), lambda i: (i, 0)),
                  _const_spec((1, D_MODEL, LRU_COLS), (layer, 0, LRU_BLK))] + [full(a) for a in consts],
        out_specs=[pl.BlockSpec((ROW_TILE, BRANCH_W), lambda i: (i, 0)),
                   pl.BlockSpec((BATCH, BRANCH_W), lambda i: (0, 0)),
                   pl.BlockSpec((LRU_HIST_P, BRANCH_W), lambda i: (0, 0)),
                   pl.BlockSpec((DEC_BATCH, BRANCH_W), lambda i: (0, 0)),
                   pl.BlockSpec((LRU_HIST_S, BRANCH_W), lambda i: (0, 0))],
        out_shape=[jax.ShapeDtypeStruct((N_TOK, BRANCH_W), f32),
                   jax.ShapeDtypeStruct((BATCH, BRANCH_W), f32),
                   jax.ShapeDtypeStruct((LRU_HIST_P, BRANCH_W), f32),
                   jax.ShapeDtypeStruct((DEC_BATCH, BRANCH_W), f32),
                   jax.ShapeDtypeStruct((LRU_HIST_S, BRANCH_W), f32)],
        scratch_shapes=[pltpu.VMEM((ROW_TILE + LRU_HIST_S, BRANCH_W), f32),
                        pltpu.VMEM((ROW_TILE, BRANCH_W), f32), pltpu.VMEM((ROW_TILE, BRANCH_W), f32),
                        pltpu.VMEM((BATCH, BRANCH_W), f32),
                        pltpu.VMEM((ROW_TILE, LRU_COLS), f32)],
        compiler_params=_cparams(("arbitrary",)),
        name="lru",
    )(xb, w_bf, *consts)


def _merge_kernel(yr_ref, ys_ref, yl_ref, xb_ref, wg_ref, x_ref, wb_ref, wo_ref, g_ref, b_ref, wr_ref, br_ref,
                  x1_ref, route_ref, route_t_ref, cnt_out_ref, cnt_ref):
    merged = jnp.zeros((MERGE_TILE, D_MODEL), f32)
    xb = xb_ref[...]
    for n, y_ref in enumerate((yr_ref, ys_ref, yl_ref)):
        proj = _dot(y_ref[...].astype(bf16), wb_ref[n])
        gate = jax.nn.sigmoid(_dot(xb, wg_ref[0, :, n * D_MODEL:(n + 1) * D_MODEL]))
        merged = merged + gate * proj
    mix = _dot(merged.astype(bf16), wo_ref[...])
    x1 = _layer_norm_rows(DN_ALPHA * x_ref[...] + mix, g_ref[...], b_ref[...])
    _to_planes(x1_ref, x1)

    logits = _dot(x1.astype(bf16), wr_ref[...]) + br_ref[...]
    lane = lax.broadcasted_iota(jnp.int32, (MERGE_TILE, ROUTE_LANES), 1).astype(f32)
    big = jnp.float32(ROUTE_LANES)
    neg = jnp.float32(-jnp.inf)
    is_g = lane < MOE_GROUPS
    lg = jnp.where(is_g, logits, neg)
    mg = jnp.max(lg, -1, keepdims=True)
    gsel = jnp.min(jnp.where(lg == mg, lane, big), -1, keepdims=True)
    sum_g = jnp.sum(jnp.where(is_g, jnp.exp(lg - mg), 0.0), -1, keepdims=True)
    pg_sel = 1.0 / sum_g
    lo = MOE_GROUPS + gsel * MOE_PER_GROUP
    is_e = jnp.abs(lane - lo - 0.5 * (MOE_PER_GROUP - 1)) < 0.5 * MOE_PER_GROUP
    le = jnp.where(is_e, logits, neg)
    me = jnp.max(le, -1, keepdims=True)
    ex = jnp.where(is_e, jnp.exp(le - me), 0.0)
    pe = jnp.where(is_e, ex / jnp.sum(ex, -1, keepdims=True), -1.0)
    v1 = jnp.max(pe, -1, keepdims=True)
    i1 = jnp.min(jnp.where(pe == v1, lane, big), -1, keepdims=True)
    pe2 = jnp.where(lane == i1, -1.0, pe)
    v2 = jnp.max(pe2, -1, keepdims=True)
    i2 = jnp.min(jnp.where(pe2 == v2, lane, big), -1, keepdims=True)
    vsum = v1 + v2
    w1 = pg_sel * v1 / vsum
    w2 = pg_sel * v2 / vsum
    e1 = i1 - MOE_GROUPS
    e2 = i2 - MOE_GROUPS

    @pl.when(pl.program_id(0) == 0)
    def _():
        cnt_ref[...] = jnp.zeros_like(cnt_ref)

    oh1 = lane == e1
    oh2 = lane == e2
    ohs = jnp.where(oh1, 1.0, jnp.where(oh2, 1.0, 0.0))
    r_i = lax.broadcasted_iota(jnp.int32, (MERGE_TILE, MERGE_TILE), 0)
    c_i = lax.broadcasted_iota(jnp.int32, (MERGE_TILE, MERGE_TILE), 1)
    strict_lower = jnp.where(c_i < r_i, 1.0, 0.0).astype(bf16)
    before = _dot(strict_lower, ohs.astype(bf16)) + cnt_ref[0:1, :]
    rank1 = jnp.sum(jnp.where(oh1, before, 0.0), -1, keepdims=True)
    rank2 = jnp.sum(jnp.where(oh2, before, 0.0), -1, keepdims=True)
    cnt_ref[0:1, :] = cnt_ref[0:1, :] + jnp.sum(ohs, 0, keepdims=True)
    cnt_out_ref[...] = cnt_ref[...]

    route = jnp.zeros((MERGE_TILE, ROUTE_LANES), f32)
    for k, val in enumerate((e1, e2, w1, w2, rank1, rank2)):
        route = jnp.where(lane == k, val, route)
    route_ref[...] = route
    route_t_ref[...] = route.T[0:V7X_SUBLANES, :]


def _merge(y_ret, y_ssm, y_lru, xb, w_bf, layer, x, wb, wo, g, b, wr, br):
    full = lambda a: _layer_spec(a, layer)
    row = lambda w: pl.BlockSpec((MERGE_TILE, w), lambda i: (i, 0))
    consts = (wb, wo, g, b, wr, br)
    return pl.pallas_call(
        _merge_kernel,
        grid=(N_TOK // MERGE_TILE,),
        in_specs=[row(BRANCH_W), row(BRANCH_W), row(BRANCH_W), row(D_MODEL),
                  _const_spec((1, D_MODEL, GATE_COLS), (layer, 0, GATE_BLK)),
                  row(D_MODEL)] + [full(a) for a in consts],
        out_specs=[pl.BlockSpec((N_PLANES, MERGE_TILE, V7X_LANES), lambda i: (0, i, 0)), row(ROUTE_LANES),
                   pl.BlockSpec((V7X_SUBLANES, MERGE_TILE), lambda i: (0, i)),
                   pl.BlockSpec((V7X_SUBLANES, ROUTE_LANES), lambda i: (0, 0))],
        out_shape=[jax.ShapeDtypeStruct((N_PLANES, N_TOK, V7X_LANES), f32),
                   jax.ShapeDtypeStruct((N_TOK, ROUTE_LANES), f32),
                   jax.ShapeDtypeStruct((V7X_SUBLANES, N_TOK), f32),
                   jax.ShapeDtypeStruct((V7X_SUBLANES, ROUTE_LANES), f32)],
        scratch_shapes=[pltpu.VMEM((V7X_SUBLANES, ROUTE_LANES), f32)],
        compiler_params=_cparams(("arbitrary",)),
        name="merge",
    )(y_ret, y_ssm, y_lru, xb, w_bf, x, *consts)


def _dispatch_kernel(pos0_ref, pos1_ref, x1_ref, xs_hbm, sem):
    base = pl.program_id(0) * DSP_TILE

    def row_copy(r, dst_row):
        return pltpu.make_async_copy(x1_ref.at[:, r, :], xs_hbm.at[dst_row], sem.at[0])

    def issue(r, c):
        n = base + r
        for k in range(MOE_TOPK):
            row_copy(r, (pos0_ref, pos1_ref)[k][n]).start(priority=k % 2)
        return c

    lax.fori_loop(0, DSP_TILE, issue, 0, unroll=8)
    for _ in range(MOE_TOPK):
        pltpu.make_async_copy(x1_ref, x1_ref, sem.at[0]).wait()


def _dispatch(pos, x1p):
    grid_spec = pltpu.PrefetchScalarGridSpec(
        num_scalar_prefetch=MOE_TOPK,
        grid=(N_TOK // DSP_TILE,),
        in_specs=[pl.BlockSpec((N_PLANES, DSP_TILE, V7X_LANES), lambda s, *_: (0, s, 0))],
        out_specs=pl.BlockSpec(memory_space=pl.ANY),
        scratch_shapes=[pltpu.SemaphoreType.DMA((1,))],
    )
    return pl.pallas_call(
        _dispatch_kernel,
        grid_spec=grid_spec,
        out_shape=jax.ShapeDtypeStruct((N_PAIRS, N_PLANES, V7X_LANES), f32),
        compiler_params=_cparams(("arbitrary",)),
        name="dispatch",
    )(*pos, x1p)


def _moe_kernel(layer, wt_ref, we_ref, wlo_ref, whi_ref, wfirst_ref, wlast_ref, wefirst_ref, weslot_ref,
                wenext_ref, nw_ref,
                xs_hbm, w1_hbm, w3_hbm, w2_hbm, ys_hbm,
                xin, yout, sem_in, sem_out, wst1, wst3, wst2, sem_w, w1b, w3b, w2b):
    w = pl.program_id(0)
    n_items = nw_ref[0]

    def weight_copies(expert, slot):
        return [pltpu.make_async_copy(src.at[layer, expert], dst.at[slot], sem_w.at[slot])
                for src, dst in ((w1_hbm, wst1), (w3_hbm, wst3), (w2_hbm, wst2))]

    def in_copy(item, slot, c):
        r0 = pl.multiple_of(wt_ref[item] * MOE_TM, MOE_TM)
        return pltpu.make_async_copy(xs_hbm.at[pl.ds(r0, MOE_TM), c, :], xin.at[slot, c], sem_in.at[slot])

    def out_copy(tile, slot, c):
        r0 = pl.multiple_of(tile * MOE_TM, MOE_TM)
        return pltpu.make_async_copy(yout.at[slot, c], ys_hbm.at[pl.ds(r0, MOE_TM), c, :], sem_out.at[slot])

    @pl.when(w == 0)
    def _():
        for cp in weight_copies(we_ref[0], 0):
            cp.start()
        for c in range(N_PLANES):
            in_copy(0, 0, c).start(priority=1)

    @pl.when(w + 1 < n_items)
    def _():
        for c in range(N_PLANES):
            in_copy(w + 1, (w + 1) % 2, c).start(priority=1)

    @pl.when(w < n_items)
    def _():
        slot = w % 2
        tile = wt_ref[w]
        oslot = tile % 2
        for c in range(N_PLANES):
            in_copy(w, slot, c).wait()

        @pl.when(wefirst_ref[w] == 1)
        def _():
            wslot = weslot_ref[w]
            for cp in weight_copies(we_ref[w], wslot):
                cp.wait()
            w1b[...] = wst1[wslot].astype(bf16)
            w3b[...] = wst3[wslot].astype(bf16)
            w2b[...] = wst2[wslot].astype(bf16)

            @pl.when(wenext_ref[w] >= 0)
            def _():
                for cp in weight_copies(wenext_ref[w], 1 - wslot):
                    cp.start()

        xt = _from_planes(xin.at[slot]).astype(bf16)
        h = jax.nn.silu(_dot(xt, w1b[...])) * _dot(xt, w3b[...])
        res = _dot(h.astype(bf16), w2b[...])
        row = lax.broadcasted_iota(jnp.int32, (MOE_TM, D_MODEL), 0)
        mine = jnp.where(row >= wlo_ref[w], row, MOE_TM) < whi_ref[w]

        @pl.when(wfirst_ref[w] == 1)
        def _():
            @pl.when(tile >= 2)
            def _():
                for c in range(N_PLANES):
                    out_copy(0, oslot, c).wait()

            _to_planes(yout.at[oslot], jnp.where(mine, res, 0.0))

        @pl.when(wfirst_ref[w] == 0)
        def _():
            _to_planes(yout.at[oslot], jnp.where(mine, res, _from_planes(yout.at[oslot])))

        @pl.when(wlast_ref[w] == 1)
        def _():
            for c in range(N_PLANES):
                out_copy(tile, oslot, c).start()

        @pl.when(w == n_items - 1)
        def _():
            for c in range(N_PLANES):
                out_copy(0, oslot, c).wait()

            @pl.when(tile >= 1)
            def _():
                for c in range(N_PLANES):
                    out_copy(0, 1 - oslot, c).wait()


def _moe(plan, xs, w1, w3, w2, layer):
    grid_spec = pltpu.PrefetchScalarGridSpec(
        num_scalar_prefetch=len(plan),
        grid=(MOE_MAX_ITEMS,),
        in_specs=[pl.BlockSpec(memory_space=pl.ANY)] * 4,
        out_specs=pl.BlockSpec(memory_space=pl.ANY),
        scratch_shapes=[pltpu.VMEM((2, N_PLANES, MOE_TM, V7X_LANES), f32),
                        pltpu.VMEM((2, N_PLANES, MOE_TM, V7X_LANES), f32),
                        pltpu.SemaphoreType.DMA((2,)), pltpu.SemaphoreType.DMA((2,)),
                        pltpu.VMEM((2, D_MODEL, MOE_HIDDEN), f32), pltpu.VMEM((2, D_MODEL, MOE_HIDDEN), f32),
                        pltpu.VMEM((2, MOE_HIDDEN, D_MODEL), f32), pltpu.SemaphoreType.DMA((2,)),
                        pltpu.VMEM((D_MODEL, MOE_HIDDEN), bf16), pltpu.VMEM((D_MODEL, MOE_HIDDEN), bf16),
                        pltpu.VMEM((MOE_HIDDEN, D_MODEL), bf16)],
    )
    return pl.pallas_call(
        functools.partial(_moe_kernel, layer),
        grid_spec=grid_spec,
        out_shape=jax.ShapeDtypeStruct((N_PAIRS, N_PLANES, V7X_LANES), f32),
        compiler_params=_cparams(("arbitrary",)),
        name="moe",
    )(*plan, xs, w1, w3, w2)


def _combine_kernel(final, pos0_ref, pos1_ref, ys_hbm, x1_ref, route_ref, g_ref, b_ref, *rest):
    if final:
        yp_hbm, ysm_ref, buf, sem, stage, sem_out = rest
    else:
        o_ref, ob_ref, buf, sem = rest
    s = pl.program_id(0)
    nsteps = pl.num_programs(0)

    def gather_copy(row, slot, k, r):
        return pltpu.make_async_copy(ys_hbm.at[row], buf.at[slot, k, :, r, :], sem.at[slot])

    def issue(tile, slot):
        def body(r, c):
            n = tile * CMB_TILE + r
            for k in range(MOE_TOPK):
                gather_copy((pos0_ref, pos1_ref)[k][n], slot, k, r).start(priority=k % 2)
            return c

        lax.fori_loop(0, CMB_TILE, body, 0, unroll=8)

    @pl.when(s == 0)
    def _():
        issue(0, 0)

    @pl.when(s + 1 < nsteps)
    def _():
        issue(s + 1, (s + 1) % 2)

    slot = s % 2
    for k in range(MOE_TOPK):
        pltpu.make_async_copy(buf.at[slot, k], buf.at[slot, k], sem.at[slot]).wait()
    route = route_ref[...]
    moe = route[:, 2:3] * _from_planes(buf.at[slot, 0]) + route[:, 3:4] * _from_planes(buf.at[slot, 1])
    y = _layer_norm_rows(DN_ALPHA * _from_planes(x1_ref) + moe, g_ref[...], b_ref[...])
    if not final:
        o_ref[...] = y
        ob_ref[...] = y.astype(bf16)
        return

    steps_t = CMB_TILE // BATCH

    def out_copy(step, oslot, b):
        t0 = pl.multiple_of(step * steps_t, steps_t)
        return pltpu.make_async_copy(stage.at[oslot, :, b, :], yp_hbm.at[b, pl.ds(t0, steps_t), :], sem_out.at[oslot])

    @pl.when(s < CMB_P_STEPS)
    def _():
        oslot = s % 2

        @pl.when(s >= 2)
        def _():
            for b in range(BATCH):
                out_copy(0, oslot, b).wait()

        stage[oslot] = y.reshape(steps_t, BATCH, D_MODEL)
        for b in range(BATCH):
            out_copy(s, oslot, b).start()

        @pl.when(s == CMB_P_STEPS - 1)
        def _():
            for b in range(BATCH):
                out_copy(0, oslot, b).wait()
                out_copy(0, 1 - oslot, b).wait()

    steps_per_tile = CMB_TILE // DEC_BATCH
    for q in range(N_S // CMB_TILE):
        @pl.when(s == CMB_P_STEPS + q)
        def _():
            for h in range(steps_per_tile):
                ysm_ref[:, q * steps_per_tile + h, :] = y[h * DEC_BATCH:(h + 1) * DEC_BATCH, :]


def _combine(pos, ys, x1, route, g, b, layer, final):
    if final:
        out_specs = [pl.BlockSpec(memory_space=pl.ANY),
                     pl.BlockSpec((DEC_BATCH, DEC_SEQ, D_MODEL), lambda s, *_: (0, 0, 0))]
        out_shape = [jax.ShapeDtypeStruct((BATCH, SEQ, D_MODEL), f32),
                     jax.ShapeDtypeStruct((DEC_BATCH, DEC_SEQ, D_MODEL), f32)]
        extra = [pltpu.VMEM((2, CMB_TILE // BATCH, BATCH, D_MODEL), f32), pltpu.SemaphoreType.DMA((2,))]
    else:
        out_specs = [pl.BlockSpec((CMB_TILE, D_MODEL), lambda s, *_: (s, 0))] * 2
        out_shape = [jax.ShapeDtypeStruct((N_TOK, D_MODEL), f32), jax.ShapeDtypeStruct((N_TOK, D_MODEL), bf16)]
        extra = []
    grid_spec = pltpu.PrefetchScalarGridSpec(
        num_scalar_prefetch=MOE_TOPK,
        grid=(N_TOK // CMB_TILE,),
        in_specs=[pl.BlockSpec(memory_space=pl.ANY),
                  pl.BlockSpec((N_PLANES, CMB_TILE, V7X_LANES), lambda s, *_: (0, s, 0)),
                  pl.BlockSpec((CMB_TILE, ROUTE_LANES), lambda s, *_: (s, 0)),
                  _layer_spec(g, layer), _layer_spec(b, layer)],
        out_specs=out_specs,
        scratch_shapes=[pltpu.VMEM((2, MOE_TOPK, N_PLANES, CMB_TILE, V7X_LANES), f32),
                        pltpu.SemaphoreType.DMA((2,))] + extra,
    )
    return pl.pallas_call(
        functools.partial(_combine_kernel, final),
        grid_spec=grid_spec,
        out_shape=out_shape,
        compiler_params=_cparams(("arbitrary",)),
        name="combine_out" if final else "combine",
    )(*pos, ys, x1, route, g, b)


def _to_rows_kernel(xp_ref, xs_ref, o_ref, ob_ref, t3_ref):
    i = pl.program_id(0)

    @pl.when(i < P_TILES)
    def _():
        for b in range(BATCH):
            t3_ref[:, b, :] = xp_ref[b]
        rows = t3_ref[...].reshape(ROW_TILE, D_MODEL)
        o_ref[...] = rows
        ob_ref[...] = rows.astype(bf16)

    @pl.when(i >= P_TILES)
    def _():
        for t in range(DEC_SEQ):
            rows = xs_ref[:, t, :]
            o_ref[t * DEC_BATCH:(t + 1) * DEC_BATCH, :] = rows
            ob_ref[t * DEC_BATCH:(t + 1) * DEC_BATCH, :] = rows.astype(bf16)


def _to_rows(x_prompt, x_sample):
    return pl.pallas_call(
        _to_rows_kernel,
        grid=(N_TILES,),
        in_specs=[pl.BlockSpec((BATCH, P_TC, D_MODEL), lambda i: (0, jnp.minimum(i, P_TILES - 1), 0)),
                  pl.BlockSpec((DEC_BATCH, DEC_SEQ, D_MODEL), lambda i: (0, 0, 0))],
        out_specs=[pl.BlockSpec((ROW_TILE, D_MODEL), lambda i: (i, 0))] * 2,
        out_shape=[jax.ShapeDtypeStruct((N_TOK, D_MODEL), f32), jax.ShapeDtypeStruct((N_TOK, D_MODEL), bf16)],
        scratch_shapes=[pltpu.VMEM((P_TC, BATCH, D_MODEL), f32)],
        compiler_params=_cparams(("arbitrary",)),
        name="to_rows",
    )(x_prompt, x_sample)


def _lookup(table, idx):
    ar = jnp.arange(MOE_EXPERTS, dtype=jnp.int32).reshape((MOE_EXPERTS,) + (1,) * idx.ndim)
    table = table.reshape(ar.shape)
    return jnp.sum(jnp.where(idx[None] == ar, table, 0), axis=0)


def _dispatch_plan(route_t, cnt):
    i32 = jnp.int32
    e = route_t[0:2].astype(i32)
    rank = route_t[4:6].astype(i32)
    counts = cnt[0, :MOE_EXPERTS].astype(i32)
    ends = jnp.cumsum(counts)
    starts = ends - counts
    pos = _lookup(starts, e) + rank

    first_tile = starts // MOE_TM
    last_tile = (ends - 1) // MOE_TM
    ntiles = jnp.where(counts > 0, last_tile - first_tile + 1, 0)
    item_end = jnp.cumsum(ntiles)
    n_items = item_end[-1]
    w = jnp.minimum(jnp.arange(MOE_MAX_ITEMS, dtype=i32), n_items - 1)
    we = jnp.sum((item_end[None, :] <= w[:, None]).astype(i32), axis=-1)
    wt = _lookup(first_tile, we) + w - _lookup(item_end - ntiles, we)
    wlo = jnp.maximum(_lookup(starts, we) - wt * MOE_TM, 0)
    whi = jnp.minimum(_lookup(ends, we) - wt * MOE_TM, MOE_TM)
    changes = (wt[1:] != wt[:-1]).astype(i32)
    wfirst = jnp.concatenate([jnp.ones((1,), i32), changes])
    wlast = jnp.concatenate([changes, jnp.ones((1,), i32)])
    wlast = jnp.where(jnp.arange(MOE_MAX_ITEMS, dtype=i32) == n_items - 1, 1, wlast)
    wefirst = jnp.concatenate([jnp.ones((1,), i32), (we[1:] != we[:-1]).astype(i32)])
    weslot = (jnp.cumsum(wefirst) - 1) % 2
    ar = jnp.arange(MOE_EXPERTS, dtype=i32)
    later = (ar[None, :] > ar[:, None]) & (counts[None, :] > 0)
    next_expert = jnp.min(jnp.where(later, ar[None, :], MOE_EXPERTS), axis=1)
    wenext = _lookup(jnp.where(next_expert < MOE_EXPERTS, next_expert, -1), we)
    return (pos[0], pos[1]), (wt, we, wlo, whi, wfirst, wlast, wefirst, weslot, wenext, n_items.reshape(1))


def _block_diag(w):
    n, a, b = w.shape
    eye = jnp.eye(n, dtype=w.dtype)
    return (w[:, :, None, :] * eye[:, None, :, None]).reshape(n * a, n * b)


def _ssm_params(a_re, a_im, log_dt, b_re, b_im, c_re, c_im):
    ar, ai = a_re, a_im
    dt = jnp.exp(log_dt)[:, None]
    mag = jnp.exp(ar * dt)
    lb_re = mag * jnp.cos(ai * dt)
    lb_im = mag * jnp.sin(ai * dt)
    den = ar * ar + ai * ai
    nr = lb_re - 1.0
    coef_re = (nr * ar + lb_im * ai) / den
    coef_im = (lb_im * ar - nr * ai) / den
    bb_re = coef_re[..., None] * b_re - coef_im[..., None] * b_im
    bb_im = coef_re[..., None] * b_im + coef_im[..., None] * b_re
    gk = SSM_GROUPS // SSM_KB

    def diag_blocks(w):
        return jnp.stack([_block_diag(w[k * gk:(k + 1) * gk]) for k in range(SSM_KB)]).astype(bf16)

    bbre = diag_blocks(bb_re.transpose(0, 2, 1))
    bbim = diag_blocks(bb_im.transpose(0, 2, 1))
    ccre = diag_blocks(c_re.transpose(0, 2, 1))
    ccim = diag_blocks(c_im.transpose(0, 2, 1))
    return (lb_re.reshape(1, SSM_LANES), lb_im.reshape(1, SSM_LANES), bbre, bbim, ccre, ccim)


def kernel(x_prompt, x_sample, state_ret, state_ssm_re, state_ssm_im, state_lru, state_conv, w_in, ret_gn_g, ret_gn_b, ssm_a_re, ssm_a_im, ssm_log_dt, ssm_b_re, ssm_b_im, ssm_c_re, ssm_c_im, ssm_d, ssm_w_glu, lru_conv_w, lru_conv_b, lru_wa, lru_ba, lru_wx, lru_bx, lru_lambda, w_branch, w_out, ln1_g, ln1_b, moe_w_group, moe_b_group, moe_w_expert, moe_b_expert, moe_w1, moe_w3, moe_w2, ln2_g, ln2_b):
    x, xb = _to_rows(x_prompt, x_sample)
    rope = _rope_tables()
    tabs_p = _ret_tables(RET_SUB_T)
    tabs_s = _ret_tables(DEC_SEQ)
    rows = lambda v: v.reshape(DEPTH, 1, -1)
    w_bf = w_in.astype(bf16)
    s0 = state_ret.reshape(DEPTH, DEC_BATCH, 2, 2 * RET_DK, RET_DV)
    ssm_p = jax.vmap(_ssm_params)(ssm_a_re, ssm_a_im, ssm_log_dt, ssm_b_re, ssm_b_im, ssm_c_re, ssm_c_im)
    h0re = state_ssm_re.reshape(DEPTH, DEC_BATCH, SSM_LANES)
    h0im = state_ssm_im.reshape(DEPTH, DEC_BATCH, SSM_LANES)
    conv0 = state_conv.transpose(0, 2, 1, 3).reshape(DEPTH, LRU_HIST_S, BRANCH_W)
    lru_wa_bd = jax.vmap(_block_diag)(lru_wa).astype(bf16)
    lru_wx_bd = jax.vmap(_block_diag)(lru_wx).astype(bf16)
    pad_w = jnp.zeros((DEPTH, D_MODEL, ROUTE_LANES - MOE_GROUPS - MOE_EXPERTS), f32)
    wr = jnp.concatenate([moe_w_group, moe_w_expert, pad_w], axis=-1).astype(bf16)
    br = jnp.concatenate([moe_b_group, moe_b_expert, pad_w[:, 0, :]], axis=-1).reshape(DEPTH, 1, ROUTE_LANES)
    wb_bf, wo_bf, wglu_bf = w_branch.astype(bf16), w_out.astype(bf16), ssm_w_glu.astype(bf16)

    outs = [[] for _ in range(10)]
    for l in range(DEPTH):
        y_ret, ret_p, ret_s = _retention(xb, w_bf, rope, s0, l, tabs_p, tabs_s, rows(ret_gn_g), rows(ret_gn_b))
        y_ssm, re_p, im_p, re_s, im_s = _ssm(xb, w_bf, l, h0re, h0im, *ssm_p, rows(ssm_d), wglu_bf)
        y_lru, lru_p, conv_p, lru_s, conv_s = _lru(
            xb, w_bf, l, state_lru, conv0, lru_conv_w, rows(lru_conv_b),
            lru_wa_bd, rows(lru_ba), lru_wx_bd, rows(lru_bx), rows(lru_lambda))
        x1, route, route_t, cnt = _merge(y_ret, y_ssm, y_lru, xb, w_bf, l, x, wb_bf, wo_bf,
                                         rows(ln1_g), rows(ln1_b), wr, br)

        pos, plan = _dispatch_plan(route_t, cnt)
        xs = _dispatch(pos, x1)
        ys = _moe(plan, xs, moe_w1, moe_w3, moe_w2, l)
        x, xb = _combine(pos, ys, x1, route, rows(ln2_g), rows(ln2_b), l, final=(l == DEPTH - 1))

        outs[0].append(ret_p.reshape(BATCH, RET_HEADS, RET_DK, RET_DV))
        outs[1].append(re_p.reshape(BATCH, SSM_GROUPS, SSM_STATE))
        outs[2].append(im_p.reshape(BATCH, SSM_GROUPS, SSM_STATE))
        outs[3].append(lru_p)
        outs[4].append(conv_p.reshape(CONV_W - 1, BATCH, BRANCH_W).transpose(1, 0, 2))
        outs[5].append(ret_s.reshape(DEC_BATCH, RET_HEADS, RET_DK, RET_DV))
        outs[6].append(re_s.reshape(DEC_BATCH, SSM_GROUPS, SSM_STATE))
        outs[7].append(im_s.reshape(DEC_BATCH, SSM_GROUPS, SSM_STATE))
        outs[8].append(lru_s)
        outs[9].append(conv_s.reshape(CONV_W - 1, DEC_BATCH, BRANCH_W).transpose(1, 0, 2))

    y_prompt, y_sample = x, xb
    return (y_prompt, y_sample) + tuple(jnp.stack(o) for o in outs)
```
